```python
import math, functools
import jax, jax.numpy as jnp
from jax import lax
import numpy as np


D_MODEL = 1024
BATCH = 8
SEQ = 4096
DEPTH = 2

GRID_W = 64
CTX_LEN = 256
EPS = 1e-6
ROPE_BASE = 10000.0
N_MOD = 6
F32 = jnp.float32
RET_HEADS = 4
RET_DK = 64
RET_DV = 128
RET_QK = RET_HEADS * RET_DK
RET_WIDTH = RET_HEADS * RET_DV
RET_CHUNK = 128
S5_WIDTH = D_MODEL - RET_WIDTH
S5_GROUP = 16
S5_GROUPS = S5_WIDTH // S5_GROUP
S5_STATE = 64
AB_CUTS = (RET_QK, 2 * RET_QK, 2 * RET_QK + RET_WIDTH, 2 * RET_QK + RET_WIDTH + S5_WIDTH)
AB_IN = 2 * RET_QK + 2 * RET_WIDTH + S5_WIDTH
HG_HEADS = 8
HG_DK = D_MODEL // HG_HEADS
HG_DV = D_MODEL // HG_HEADS
HG_CHUNK = 32
D_FF = 4 * D_MODEL
N_EVEN = (DEPTH + 1) // 2
N_ODD = DEPTH // 2

kernel_name = 'hybrid_retention_s5_hgrn2_prefix_dit'


def _rmsnorm(x, g):
    xf = x.astype(F32)
    y = xf * lax.rsqrt(jnp.mean(jnp.square(xf), axis=-1, keepdims=True) + EPS)
    return (y * g.astype(F32)).astype(x.dtype)


def _head_rms(o):
    return o * lax.rsqrt(jnp.mean(jnp.square(o), axis=-1, keepdims=True) + EPS)


def _modulate(h, shift, scale):
    return h * (1.0 + scale) + shift


def _adaln(cond, w, b, n):
    return jnp.split(jax.nn.silu(cond) @ w + b, n, axis=-1)


def _sqrelu_mlp(h, w1, w2):
    return jnp.square(jax.nn.relu(h @ w1)) @ w2


def _flip(t, rev):
    return t[:, ::-1] if rev else t


def _split_heads(t, d):
    return t.astype(F32).reshape(t.shape[0], t.shape[1], -1, d)


def _grid_rope(n_tok):
    rows = n_tok // GRID_W
    row = jnp.broadcast_to(jnp.arange(rows, dtype=F32)[:, None], (rows, GRID_W)).reshape(-1)
    col = jnp.broadcast_to(jnp.arange(GRID_W, dtype=F32)[None, :], (rows, GRID_W)).reshape(-1)
    n_freq = RET_DK // 4
    inv = ROPE_BASE ** (-jnp.arange(n_freq, dtype=F32) / n_freq)
    ang = jnp.concatenate([row[:, None] * inv, col[:, None] * inv], axis=-1)
    return jnp.cos(ang), jnp.sin(ang)


def _rope(t, cos, sin):
    half = t.shape[-1] // 2
    t1, t2 = t[..., :half], t[..., half:]
    cs, sn = cos[None, :, None, :], sin[None, :, None, :]
    return jnp.concatenate([t1 * cs - t2 * sn, t1 * sn + t2 * cs], axis=-1)


def _retention_chunks(q, k, v, log_gamma, s0):
    b, n, h, _ = q.shape
    dv = v.shape[-1]
    c = min(RET_CHUNK, n)
    nc = n // c
    rs = lambda t: jnp.moveaxis(t.reshape(b, nc, c, h, t.shape[-1]), 1, 0)
    qc, kc, vc = rs(q), rs(k), rs(v)
    pos = jnp.arange(c, dtype=F32)
    diff = pos[:, None] - pos[None, :]
    decay = jnp.where(diff >= 0, jnp.exp(jnp.maximum(diff, 0.0)[None] * log_gamma[:, None, None]), 0.0)
    q_dec = jnp.exp((pos[:, None] + 1.0) * log_gamma[None])
    k_dec = jnp.exp((c - 1.0 - pos)[:, None] * log_gamma[None])
    c_dec = jnp.exp(c * log_gamma)
    att = jnp.einsum('nbihd,nbjhd->nbhij', qc, kc) * decay
    o_intra = jnp.einsum('nbhij,nbjhe->nbihe', att, vc)

    def step(s, inp):
        qn, kn, vn = inp
        o = jnp.einsum('bihd,ih,bhde->bihe', qn, q_dec, s)
        s = c_dec[None, :, None, None] * s + jnp.einsum('bjhd,jh,bjhe->bhde', kn, k_dec, vn)
        return s, o

    s_end, o_inter = lax.scan(step, s0, (qc, kc, vc))
    out = jnp.moveaxis(o_intra + o_inter, 0, 1).reshape(b, n, h, dv)
    return out, s_end


def _retention_state(k, v, log_gamma):
    n = k.shape[1]
    w = jnp.exp((n - 1.0 - jnp.arange(n, dtype=F32))[:, None] * log_gamma[None])
    return jnp.einsum('blhd,lh,blhe->bhde', k, w, v)


def _s5_discretise(a_re, a_im, log_dt, b_re, b_im):
    a_re, a_im = a_re.astype(F32), a_im.astype(F32)
    dt = jnp.exp(log_dt.astype(F32))[:, None]
    mag = jnp.exp(a_re * dt)
    ang = a_im * dt
    ab_re, ab_im = mag * jnp.cos(ang), mag * jnp.sin(ang)
    nr, ni = ab_re - 1.0, ab_im
    den = jnp.square(a_re) + jnp.square(a_im)
    fr = (nr * a_re + ni * a_im) / den
    fi = (ni * a_re - nr * a_im) / den
    b_re, b_im = b_re.astype(F32), b_im.astype(F32)
    bb_re = fr[..., None] * b_re - fi[..., None] * b_im
    bb_im = fr[..., None] * b_im + fi[..., None] * b_re
    return ab_re, ab_im, bb_re, bb_im


def _complex_affine_combine(e1, e2):
    a1r, a1i, b1r, b1i = e1
    a2r, a2i, b2r, b2i = e2
    return (a2r * a1r - a2i * a1i,
            a2r * a1i + a2i * a1r,
            a2r * b1r - a2i * b1i + b2r,
            a2r * b1i + a2i * b1r + b2i)


def _s5_scan(u, ab_re, ab_im, bb_re, bb_im, h0_re, h0_im):
    n = u.shape[1]
    x_re = jnp.einsum('blgm,gpm->blgp', u, bb_re)
    x_im = jnp.einsum('blgm,gpm->blgp', u, bb_im)
    x_re = x_re.at[:, 0].add(ab_re * h0_re - ab_im * h0_im)
    x_im = x_im.at[:, 0].add(ab_re * h0_im + ab_im * h0_re)
    shape = (1, n) + ab_re.shape
    a_re = jnp.broadcast_to(ab_re, shape)
    a_im = jnp.broadcast_to(ab_im, shape)
    _, _, h_re, h_im = lax.associative_scan(_complex_affine_combine, (a_re, a_im, x_re, x_im), axis=1)
    return h_re, h_im


def _s5_readout(h_re, h_im, c_re, c_im):
    return (jnp.einsum('blgp,gmp->blgm', h_re, c_re.astype(F32))
            - jnp.einsum('blgp,gmp->blgm', h_im, c_im.astype(F32)))


def _hgrn_gates(raw, lb):
    log_f = jnp.log(lb + (1.0 - lb) * jax.nn.sigmoid(raw))
    k = (1.0 - lb) * jax.nn.sigmoid(-raw)
    return log_f, k


def _hgrn_chunks(q, k, v, log_f, s0):
    b, n, h, _ = q.shape
    dv = v.shape[-1]
    c = min(HG_CHUNK, n)
    nc = n // c
    rs = lambda t: jnp.moveaxis(t.reshape(b, nc, c, h, t.shape[-1]), 1, 0)
    qc, kc, vc, lf = rs(q), rs(k), rs(v), rs(log_f)
    cum = jnp.cumsum(lf, axis=2)
    tot = cum[:, :, -1:]
    q_in = qc * jnp.exp(cum)
    k_in = kc * jnp.exp(-cum)
    k_out = kc * jnp.exp(tot - cum)
    mask = jnp.tril(jnp.ones((c, c), dtype=bool))
    att = jnp.where(mask, jnp.einsum('nbihd,nbjhd->nbhij', q_in, k_in), 0.0)
    o_intra = jnp.einsum('nbhij,nbjhe->nbihe', att, vc)

    def step(s, inp):
        qn, kn, vn, dn = inp
        o = jnp.einsum('bihd,bhde->bihe', qn, s)
        s = jnp.exp(dn)[:, 0, :, :, None] * s + jnp.einsum('bjhd,bjhe->bhde', kn, vn)
        return s, o

    s_end, o_inter = lax.scan(step, s0, (q_in, k_out, vc, tot))
    out = jnp.moveaxis(o_intra + o_inter, 0, 1).reshape(b, n, h, dv)
    return out, s_end


def _hgrn_state(k, v, log_f):
    cum = jnp.cumsum(log_f, axis=1)
    w = jnp.exp(cum[:, -1:] - cum)
    return jnp.einsum('blhd,blhe->bhde', k * w, v)


def _retention_s5_mixer(xl, xc, cos, sin, w_in, w_out, ret_logit, a_re, a_im, log_dt, b_re, b_im,
                        c_re, c_im, d_skip, w_glu, b_glu, ctx_out):
    b, n, _ = xl.shape
    bc, nc_tok, _ = xc.shape
    k_scale = RET_DK ** -0.5
    ql, kl, vl, ul, gl = jnp.split(xl @ w_in, AB_CUTS, axis=-1)
    if ctx_out:
        qc, kc, vc, uc, gc = jnp.split(xc @ w_in, AB_CUTS, axis=-1)
        qc = _split_heads(qc, RET_DK)
    else:
        kc, vc, uc = jnp.split(xc @ w_in[:, RET_QK:AB_CUTS[3]], [RET_QK, RET_QK + RET_WIDTH], axis=-1)
    ql = _rope(_split_heads(ql, RET_DK), cos, sin)
    kl = _rope(_split_heads(kl, RET_DK), cos, sin) * k_scale
    vl = _split_heads(vl, RET_DV)
    kc = _split_heads(kc, RET_DK) * k_scale
    vc = _split_heads(vc, RET_DV)
    ul_g = ul.astype(F32).reshape(b, n, S5_GROUPS, S5_GROUP)
    uc_g = uc.astype(F32).reshape(bc, nc_tok, S5_GROUPS, S5_GROUP)
    log_gamma = jax.nn.log_sigmoid(ret_logit.astype(F32))
    ret_zero = jnp.zeros((bc, RET_HEADS, RET_DK, RET_DV), F32)
    s5_zero = jnp.zeros((bc, S5_GROUPS, S5_STATE), F32)
    ret_l, ret_c, s5_l, s5_c = [], [], [], []
    for d in range(2):
        f = functools.partial(_flip, rev=(d == 1))
        lg = log_gamma[d]
        if ctx_out:
            o, s_ret = _retention_chunks(f(qc), f(kc), f(vc), lg, ret_zero)
            ret_c.append(f(o))
        else:
            s_ret = _retention_state(f(kc), f(vc), lg)
        o, _ = _retention_chunks(f(ql), f(kl), f(vl), lg, s_ret)
        ret_l.append(f(o))
        disc = _s5_discretise(a_re[d], a_im[d], log_dt[d], b_re[d], b_im[d])
        hc_re, hc_im = _s5_scan(f(uc_g), *disc, s5_zero, s5_zero)
        if ctx_out:
            s5_c.append(f(_s5_readout(hc_re, hc_im, c_re[d], c_im[d])))
        hl_re, hl_im = _s5_scan(f(ul_g), *disc, hc_re[:, -1], hc_im[:, -1])
        s5_l.append(f(_s5_readout(hl_re, hl_im, c_re[d], c_im[d])))

    def merge(ret, s5, g, u):
        bb, nn = u.shape[0], u.shape[1]
        r = _head_rms(ret[0] + ret[1]).reshape(bb, nn, RET_WIDTH) * jax.nn.silu(g.astype(F32))
        y = (s5[0] + s5[1]).reshape(bb, nn, S5_WIDTH) + d_skip.astype(F32) * u.astype(F32)
        y = jax.nn.gelu(y)
        y = y * jax.nn.sigmoid(y @ w_glu + b_glu)
        return jnp.concatenate([r, y], axis=-1) @ w_out

    yl = merge(ret_l, s5_l, gl, ul).astype(xl.dtype)
    yc = merge(ret_c, s5_c, gc, uc).astype(xc.dtype) if ctx_out else None
    return yl, yc


def _hgrn2_mixer(xl, xc, w_in, w_out, lower_bound, norm_g, ctx_out):
    D = D_MODEL
    bc = xc.shape[0]
    ql, ffl, fbl, il, gl = jnp.split(xl @ w_in, 5, axis=-1)
    if ctx_out:
        qc, ffc, fbc, ic, gc = jnp.split(xc @ w_in, 5, axis=-1)
        qc = _split_heads(qc, HG_DK)
    else:
        ffc, fbc, ic = jnp.split(xc @ w_in[:, D:4 * D], 3, axis=-1)
    ql = _split_heads(ql, HG_DK)
    il = _split_heads(il, HG_DV)
    ic = _split_heads(ic, HG_DV)
    raw_l, raw_c = (ffl, fbl), (ffc, fbc)
    zero = jnp.zeros((bc, HG_HEADS, HG_DK, HG_DV), F32)
    out_l, out_c = [], []
    for d in range(2):
        f = functools.partial(_flip, rev=(d == 1))
        lb = lower_bound[d].reshape(HG_HEADS, HG_DK)
        lfl, kl = _hgrn_gates(_split_heads(raw_l[d], HG_DK), lb)
        lfc, kc = _hgrn_gates(_split_heads(raw_c[d], HG_DK), lb)
        if ctx_out:
            o, s = _hgrn_chunks(f(qc), f(kc), f(ic), f(lfc), zero)
            out_c.append(f(o))
        else:
            s = _hgrn_state(f(kc), f(ic), f(lfc))
        o, _ = _hgrn_chunks(f(ql), f(kl), f(il), f(lfl), s)
        out_l.append(f(o))

    def merge(outs, g):
        o = _head_rms(outs[0] + outs[1]) * norm_g.astype(F32)
        o = o.reshape(o.shape[0], o.shape[1], D) * jax.nn.silu(g.astype(F32))
        return o @ w_out

    yl = merge(out_l, gl).astype(xl.dtype)
    yc = merge(out_c, gc).astype(xc.dtype) if ctx_out else None
    return yl, yc


def setup_inputs(seed: int = 0) -> dict:
    key = jax.random.key(seed)
    ks = iter(jax.random.split(key, 32))

    def nrm(shape, scale):
        return scale * jax.random.normal(next(ks), shape, F32)

    D = D_MODEL
    x = nrm((BATCH, SEQ, D), 1.0)
    c = nrm((BATCH, D), 1.0)
    ctx = nrm((BATCH, CTX_LEN, D), 1.0)
    c_ctx = nrm((D,), 1.0)
    w_mod = nrm((DEPTH, D, N_MOD * D), 0.5 * D ** -0.5)
    b_mod = nrm((DEPTH, N_MOD * D), 0.02)
    norm_mix = 1.0 + nrm((DEPTH, D), 0.02)
    norm_mlp = 1.0 + nrm((DEPTH, D), 0.02)
    w_mlp_in = nrm((DEPTH, D, D_FF), D ** -0.5)
    w_mlp_out = nrm((DEPTH, D_FF, D), D_FF ** -0.5)
    ab_w_in = nrm((N_EVEN, D, AB_IN), D ** -0.5)
    ab_w_out = nrm((N_EVEN, D, D), D ** -0.5)
    eps_h = np.exp(np.linspace(math.log(1.0 / 32), math.log(1.0 / 512), RET_HEADS))
    ret_logit = jnp.asarray(np.log((1.0 - eps_h) / eps_h), F32) + nrm((N_EVEN, 2, RET_HEADS), 0.05)
    n_idx = jnp.arange(S5_STATE, dtype=F32)
    s5_a_re = -0.5 + nrm((N_EVEN, 2, S5_GROUPS, S5_STATE), 0.01)
    s5_a_im = math.pi * n_idx + nrm((N_EVEN, 2, S5_GROUPS, S5_STATE), 0.01)
    s5_log_dt = jax.random.uniform(next(ks), (N_EVEN, 2, S5_GROUPS), F32, math.log(1e-3), math.log(1e-1))
    s5_b_re = nrm((N_EVEN, 2, S5_GROUPS, S5_STATE, S5_GROUP), (2 * S5_GROUP) ** -0.5)
    s5_b_im = nrm((N_EVEN, 2, S5_GROUPS, S5_STATE, S5_GROUP), (2 * S5_GROUP) ** -0.5)
    s5_c_re = nrm((N_EVEN, 2, S5_GROUPS, S5_GROUP, S5_STATE), S5_STATE ** -0.5)
    s5_c_im = nrm((N_EVEN, 2, S5_GROUPS, S5_GROUP, S5_STATE), S5_STATE ** -0.5)
    s5_d = nrm((N_EVEN, S5_WIDTH), 1.0)
    s5_w_glu = nrm((N_EVEN, S5_WIDTH, S5_WIDTH), S5_WIDTH ** -0.5)
    s5_b_glu = nrm((N_EVEN, S5_WIDTH), 0.02)
    hg_w_in = nrm((N_ODD, D, 5 * D), D ** -0.5)
    hg_w_out = nrm((N_ODD, D, D), D ** -0.5)
    hg_lb_logits = nrm((2, DEPTH, HG_HEADS * HG_DK), 0.1)
    hg_norm = 1.0 + nrm((N_ODD, HG_DV), 0.02)
    norm_final = 1.0 + nrm((D,), 0.02)
    return {'x': x, 'c': c, 'ctx': ctx, 'c_ctx': c_ctx, 'w_mod': w_mod, 'b_mod': b_mod,
            'norm_mix': norm_mix, 'norm_mlp': norm_mlp, 'w_mlp_in': w_mlp_in, 'w_mlp_out': w_mlp_out,
            'ab_w_in': ab_w_in, 'ab_w_out': ab_w_out, 'ret_logit': ret_logit,
            's5_a_re': s5_a_re, 's5_a_im': s5_a_im, 's5_log_dt': s5_log_dt,
            's5_b_re': s5_b_re, 's5_b_im': s5_b_im, 's5_c_re': s5_c_re, 's5_c_im': s5_c_im,
            's5_d': s5_d, 's5_w_glu': s5_w_glu, 's5_b_glu': s5_b_glu,
            'hg_w_in': hg_w_in, 'hg_w_out': hg_w_out, 'hg_lb_logits': hg_lb_logits, 'hg_norm': hg_norm,
            'norm_final': norm_final}


def reference(x, c, ctx, c_ctx, w_mod, b_mod, norm_mix, norm_mlp, w_mlp_in, w_mlp_out,
              ab_w_in, ab_w_out, ret_logit, s5_a_re, s5_a_im, s5_log_dt, s5_b_re, s5_b_im,
              s5_c_re, s5_c_im, s5_d, s5_w_glu, s5_b_glu, hg_w_in, hg_w_out, hg_lb_logits, hg_norm,
              norm_final):
    cos, sin = _grid_rope(x.shape[1])
    gam = jax.nn.softmax(hg_lb_logits.astype(F32), axis=1)
    lower_bounds = jnp.cumsum(gam, axis=1) - gam[:, :1]
    hl, hc = x, ctx
    for l in range(DEPTH):
        ctx_out = l < DEPTH - 1
        ml = [m[:, None, :] for m in _adaln(c, w_mod[l], b_mod[l], N_MOD)]
        if ctx_out:
            mc = _adaln(c_ctx, w_mod[l], b_mod[l], N_MOD)
        else:
            mc = _adaln(c_ctx, w_mod[l, :, :2 * D_MODEL], b_mod[l, :2 * D_MODEL], 2)
        xl = _modulate(_rmsnorm(hl, norm_mix[l]), ml[0], ml[1])
        xc = _modulate(_rmsnorm(hc, norm_mix[l]), mc[0], mc[1])
        j = l // 2
        if l % 2 == 0:
            yl, yc = _retention_s5_mixer(xl, xc, cos, sin, ab_w_in[j], ab_w_out[j], ret_logit[j],
                                         s5_a_re[j], s5_a_im[j], s5_log_dt[j], s5_b_re[j], s5_b_im[j],
                                         s5_c_re[j], s5_c_im[j], s5_d[j], s5_w_glu[j], s5_b_glu[j], ctx_out)
        else:
            yl, yc = _hgrn2_mixer(xl, xc, hg_w_in[j], hg_w_out[j], lower_bounds[:, l], hg_norm[j], ctx_out)
        hl = hl + ml[2] * yl
        hl = hl + ml[5] * _sqrelu_mlp(_modulate(_rmsnorm(hl, norm_mlp[l]), ml[3], ml[4]), w_mlp_in[l], w_mlp_out[l])
        if ctx_out:
            hc = hc + mc[2] * yc
            hc = hc + mc[5] * _sqrelu_mlp(_modulate(_rmsnorm(hc, norm_mlp[l]), mc[3], mc[4]), w_mlp_in[l], w_mlp_out[l])
    return _rmsnorm(hl, norm_final)
```

```python
import functools
import math

import jax
import jax.numpy as jnp
from jax import lax
from jax.experimental import pallas as pl
from jax.experimental.pallas import tpu as pltpu

F32 = jnp.float32
BF16 = jnp.bfloat16

D_MODEL = 1024
DEPTH = 2
GRID_W = 64
EPS = 1e-6
ROPE_BASE = 10000.0
N_MOD = 6
RET_HEADS = 4
RET_DK = 64
RET_DV = 128
RET_QK = RET_HEADS * RET_DK
RET_WIDTH = RET_HEADS * RET_DV
RET_CHUNK = 128
S5_WIDTH = D_MODEL - RET_WIDTH
S5_GROUP = 16
S5_GROUPS = S5_WIDTH // S5_GROUP
S5_STATE = 64
S5_CHUNK = 16
S5_ROW = S5_CHUNK * S5_GROUP
AB_IN = 2 * RET_QK + 2 * RET_WIDTH + S5_WIDTH
HG_HEADS = 8
HG_DK = D_MODEL // HG_HEADS
HG_BLOCK = 64
D_FF = 4 * D_MODEL
MOD_ROWS = 16

VMEM_LIMIT_BYTES = 56 * 1024 * 1024


def _cp(*sem):
    return pltpu.CompilerParams(dimension_semantics=sem, vmem_limit_bytes=VMEM_LIMIT_BYTES)


def _dot(a, b):
    return jnp.dot(a, b, preferred_element_type=F32)


def _dot_nt(a, b):
    return lax.dot_general(a, b, (((1,), (1,)), ((), ())), preferred_element_type=F32)


def _dot_tn(a, b):
    return lax.dot_general(a, b, (((0,), (0,)), ((), ())), preferred_element_type=F32)


def _rms(x):
    return x * lax.rsqrt(jnp.mean(x * x, axis=-1, keepdims=True) + EPS)


def _mod_chunk(mod_ref, row, i):
    return mod_ref[0, pl.ds(row, 1), i * D_MODEL:(i + 1) * D_MODEL]


def _adaln_kernel(cc_ref, w_ref, b_ref, o_ref):
    s = jax.nn.silu(cc_ref[...]).astype(BF16)
    o_ref[0] = _dot(s, w_ref[0].astype(BF16)) + b_ref[0]


def _adaln(cc, w_mod, b_mod):
    bn = 1536
    n = N_MOD * D_MODEL
    return pl.pallas_call(
        _adaln_kernel,
        out_shape=jax.ShapeDtypeStruct((DEPTH, MOD_ROWS, n), F32),
        grid=(DEPTH, n // bn),
        in_specs=[
            pl.BlockSpec((MOD_ROWS, D_MODEL), lambda l, j: (0, 0)),
            pl.BlockSpec((1, D_MODEL, bn), lambda l, j: (l, 0, j)),
            pl.BlockSpec((1, 1, bn), lambda l, j: (l, 0, j)),
        ],
        out_specs=pl.BlockSpec((1, MOD_ROWS, bn), lambda l, j: (l, 0, j)),
        compiler_params=_cp("parallel", "parallel"),
        name="adaln",
    )(cc, w_mod, b_mod.reshape(DEPTH, 1, n))


def _rope(t, cos, sin):
    lane = lax.broadcasted_iota(jnp.int32, t.shape, 1)
    first = (lane & (RET_DK // 2)) == 0
    w = t.shape[1]
    swapped = jnp.where(first, pltpu.roll(t, w - RET_DK // 2, 1), pltpu.roll(t, RET_DK // 2, 1))
    return t * cos + swapped * sin


def _inproj0_kernel(*refs, mod_row, rope):
    if rope:
        h_ref, mod_ref, ng_ref, w_ref, cos_ref, sin_ref, q_ref, k_ref, v_ref, u_ref, g_ref = refs
    else:
        h_ref, mod_ref, ng_ref, w_ref, q_ref, k_ref, v_ref, u_ref, g_ref = refs
    row = pl.program_id(0) if mod_row is None else mod_row
    xn = _rms(h_ref[0]) * ng_ref[...]
    xm = (xn * (1.0 + _mod_chunk(mod_ref, row, 1)) + _mod_chunk(mod_ref, row, 0)).astype(BF16)
    y = _dot(xm, w_ref[...])
    q = y[:, 0:RET_QK]
    k = y[:, RET_QK:2 * RET_QK]
    if rope:
        q = _rope(q, cos_ref[...], sin_ref[...])
        k = _rope(k, cos_ref[...], sin_ref[...])
    q_ref[0] = q.astype(BF16)
    k_ref[0] = (k * (RET_DK ** -0.5)).astype(BF16)
    c0 = 2 * RET_QK
    v_ref[0] = y[:, c0:c0 + RET_WIDTH].astype(BF16)
    u_ref[0] = y[:, c0 + RET_WIDTH:c0 + RET_WIDTH + S5_WIDTH]
    g_ref[0] = y[:, c0 + RET_WIDTH + S5_WIDTH:]


def _inproj0(h, mod, layer, ng, w, cos, sin, mod_row, tm):
    b, n, _ = h.shape
    rope = cos is not None
    row_spec = lambda width: pl.BlockSpec((1, tm, width), lambda i, j: (i, j, 0))
    in_specs = [
        row_spec(D_MODEL),
        pl.BlockSpec((1, MOD_ROWS, N_MOD * D_MODEL), lambda i, j: (layer, 0, 0)),
        pl.BlockSpec((1, D_MODEL), lambda i, j: (0, 0)),
        pl.BlockSpec((D_MODEL, AB_IN), lambda i, j: (0, 0)),
    ]
    args = [h, mod, ng, w]
    if rope:
        in_specs += [pl.BlockSpec((tm, RET_QK), lambda i, j: (j, 0))] * 2
        args += [cos, sin]
    widths = (RET_QK, RET_QK, RET_WIDTH, S5_WIDTH, RET_WIDTH)
    dtypes = (BF16, BF16, BF16, F32, F32)
    return pl.pallas_call(
        functools.partial(_inproj0_kernel, mod_row=mod_row, rope=rope),
        out_shape=[jax.ShapeDtypeStruct((b, n, wd), dt) for wd, dt in zip(widths, dtypes)],
        grid=(b, n // tm),
        in_specs=in_specs,
        out_specs=[row_spec(wd) for wd in widths],
        compiler_params=_cp("parallel", "parallel"),
        name="inproj0_lat" if rope else "inproj0_ctx",
    )(*args)


def _ret_kernel(lg_ref, ql, kl, vl, gl, qc, kc, vc, gc, rl, rc, sf, sb, sbl, sbc, *, ncl, ncc):
    c = RET_CHUNK
    p = pl.program_id(1)
    h_a = 2 * p
    lgf_a = lg_ref[pl.ds(h_a, 1), :]
    lgf_b = lg_ref[pl.ds(h_a + 1, 1), :]
    lgb_a = lg_ref[pl.ds(RET_HEADS + h_a, 1), :]
    lgb_b = lg_ref[pl.ds(RET_HEADS + h_a + 1, 1), :]
    lane = lax.broadcasted_iota(jnp.int32, (1, 2 * RET_DK), 1)
    is_a = lane < RET_DK
    lgf_lane = jnp.where(is_a, lgf_a, lgf_b)
    lgb_lane = jnp.where(is_a, lgb_a, lgb_b)
    ri = lax.broadcasted_iota(jnp.int32, (c, c), 0).astype(F32)
    ci = lax.broadcasted_iota(jnp.int32, (c, c), 1).astype(F32)
    diff = ri - ci

    def dmat(lgf, lgb):
        fwd = jnp.exp(jnp.maximum(diff, 0.0) * lgf)
        bwd = jnp.exp(jnp.maximum(-diff, 0.0) * lgb)
        return jnp.where(diff > 0, fwd, jnp.where(diff < 0, bwd, 2.0))

    d_a = dmat(lgf_a, lgb_a)
    d_b = dmat(lgf_b, lgb_b)
    rowp = lax.broadcasted_iota(jnp.int32, (c, 2 * RET_DK), 0).astype(F32)
    qd_f = jnp.exp((rowp + 1.0) * lgf_lane)
    qd_b = jnp.exp((c - rowp) * lgb_lane)
    kd_f = jnp.exp((c - 1.0 - rowp) * lgf_lane)
    kd_b = jnp.exp(rowp * lgb_lane)
    rowk = lax.broadcasted_iota(jnp.int32, (2 * RET_DK, 2 * RET_DV), 0)
    cd_f = jnp.exp(c * jnp.where(rowk < RET_DK, lgf_a[:, :1], lgf_b[:, :1]))
    cd_b = jnp.exp(c * jnp.where(rowk < RET_DK, lgb_a[:, :1], lgb_b[:, :1]))
    mask_a = jnp.where(is_a, 1.0, 0.0).astype(BF16)
    mask_b = jnp.where(is_a, 0.0, 1.0).astype(BF16)

    sb[...] = jnp.zeros_like(sb)

    def bwd_step(k, v, store_ref, idx):
        store_ref[idx] = sb[...]
        kd = (k.astype(F32) * kd_b).astype(BF16)
        sb[...] = cd_b * sb[...] + _dot_tn(kd, v)

    for n in reversed(range(ncc)):
        bwd_step(kc[0, n * c:(n + 1) * c, :], vc[0, n * c:(n + 1) * c, :], sbc, n)

    def bwd_body(i, carry):
        n = ncl - 1 - i
        st = pl.multiple_of(n * c, c)
        bwd_step(kl[0, pl.ds(st, c), :], vl[0, pl.ds(st, c), :], sbl, n)
        return carry

    lax.fori_loop(0, ncl, bwd_body, 0)

    sf[...] = jnp.zeros_like(sf)

    def fwd_step(q, k, v, g, sb_prev, out_ref, st):
        qf = q.astype(F32)
        q_f = (qf * qd_f).astype(BF16)
        q_b = (qf * qd_b).astype(BF16)
        s_f = sf[...].astype(BF16)
        s_b = sb_prev.astype(BF16)
        for m, dm, cs in ((mask_a, d_a, 0), (mask_b, d_b, RET_DV)):
            att = _dot_nt(q * m, k) * dm
            o = (_dot(att.astype(BF16), v[:, cs:cs + RET_DV])
                 + _dot(q_f * m, s_f[:, cs:cs + RET_DV])
                 + _dot(q_b * m, s_b[:, cs:cs + RET_DV]))
            o = _rms(o) * jax.nn.silu(g[:, cs:cs + RET_DV])
            out_ref[0, pl.ds(st, c), cs:cs + RET_DV] = o.astype(BF16)
        kd = (k.astype(F32) * kd_f).astype(BF16)
        sf[...] = cd_f * sf[...] + _dot_tn(kd, v)

    for n in range(ncc):
        sl = slice(n * c, (n + 1) * c)
        fwd_step(qc[0, sl, :], kc[0, sl, :], vc[0, sl, :], gc[0, sl, :], sbc[n], rc, n * c)

    def fwd_body(n, carry):
        st = pl.multiple_of(n * c, c)
        sl = pl.ds(st, c)
        fwd_step(ql[0, sl, :], kl[0, sl, :], vl[0, sl, :], gl[0, sl, :], sbl[n], rl, st)
        return carry

    lax.fori_loop(0, ncl, fwd_body, 0)


def _retention(lg_rows, q_l, k_l, v_l, g_l, q_c, k_c, v_c, g_c):
    b, n, _ = q_l.shape
    nc = q_c.shape[1]
    ncl, ncc = n // RET_CHUNK, nc // RET_CHUNK
    pairs = RET_HEADS // 2

    def spec(rows, width):
        return pl.BlockSpec((1, rows, width), lambda i, p: (i, 0, p))

    return pl.pallas_call(
        functools.partial(_ret_kernel, ncl=ncl, ncc=ncc),
        out_shape=[jax.ShapeDtypeStruct((b, n, RET_WIDTH), BF16),
                   jax.ShapeDtypeStruct((b, nc, RET_WIDTH), BF16)],
        grid=(b, pairs),
        in_specs=[pl.BlockSpec((2 * RET_HEADS, 2 * RET_DK), lambda i, p: (0, 0)),
                  spec(n, 2 * RET_DK), spec(n, 2 * RET_DK), spec(n, 2 * RET_DV), spec(n, 2 * RET_DV),
                  spec(nc, 2 * RET_DK), spec(nc, 2 * RET_DK), spec(nc, 2 * RET_DV), spec(nc, 2 * RET_DV)],
        out_specs=[spec(n, 2 * RET_DV), spec(nc, 2 * RET_DV)],
        scratch_shapes=[pltpu.VMEM((2 * RET_DK, 2 * RET_DV), F32),
                        pltpu.VMEM((2 * RET_DK, 2 * RET_DV), F32),
                        pltpu.VMEM((ncl, 2 * RET_DK, 2 * RET_DV), F32),
                        pltpu.VMEM((ncc, 2 * RET_DK, 2 * RET_DV), F32)],
        compiler_params=_cp("parallel", "parallel"),
        name="retention",
    )(lg_rows, q_l, k_l, v_l, g_l, q_c, k_c, v_c, g_c)


def _s5_operators(a_re, a_im, log_dt, b_re, b_im, c_re, c_im):
    t = S5_CHUNK
    hp = lax.Precision.HIGHEST
    ks = jnp.arange(t + 1, dtype=F32)
    win_re, win_im, wout_re, wout_im, a_t, toep = [], [], [], [], [], []
    for d in range(2):
        are, aim = a_re[d].astype(F32), a_im[d].astype(F32)
        dt = jnp.exp(log_dt[d].astype(F32))[:, None]
        mag = jnp.exp(are * dt)
        ang = aim * dt
        ab_re, ab_im = mag * jnp.cos(ang), mag * jnp.sin(ang)
        nr, ni = ab_re - 1.0, ab_im
        den = jnp.square(are) + jnp.square(aim)
        fr = (nr * are + ni * aim) / den
        fi = (ni * are - nr * aim) / den
        bre, bim = b_re[d].astype(F32), b_im[d].astype(F32)
        bb_re = fr[..., None] * bre - fi[..., None] * bim
        bb_im = fr[..., None] * bim + fi[..., None] * bre
        pmag = jnp.exp(ks[:, None, None] * (are * dt)[None])
        pang = ks[:, None, None] * ang[None]
        p_re, p_im = pmag * jnp.cos(pang), pmag * jnp.sin(pang)
        cre, cim = c_re[d].astype(F32), c_im[d].astype(F32)
        ca_re = cre[None] * p_re[:, :, None, :] - cim[None] * p_im[:, :, None, :]
        ca_im = cre[None] * p_im[:, :, None, :] + cim[None] * p_re[:, :, None, :]
        kk = (jnp.einsum('kgmp,gpn->kgmn', ca_re, bb_re, precision=hp)
              - jnp.einsum('kgmp,gpn->kgmn', ca_im, bb_im, precision=hp))
        s_idx = jnp.arange(t)[:, None]
        t_idx = jnp.arange(t)[None, :]
        if d == 0:
            lag, valid = t_idx - s_idx, t_idx >= s_idx
        else:
            lag, valid = s_idx - t_idx, s_idx >= t_idx
        blocks = kk[jnp.clip(lag, 0, t)]
        blocks = jnp.where(valid[:, :, None, None, None], blocks, 0.0)
        toep.append(jnp.transpose(blocks, (2, 0, 4, 1, 3)))
        pw = p_re[:t], p_im[:t]
        if d == 0:
            pw = pw[0][::-1], pw[1][::-1]
        w_re = pw[0][:, :, :, None] * bb_re[None] - pw[1][:, :, :, None] * bb_im[None]
        w_im = pw[0][:, :, :, None] * bb_im[None] + pw[1][:, :, :, None] * bb_re[None]
        win_re.append(jnp.transpose(w_re, (1, 0, 3, 2)))
        win_im.append(jnp.transpose(w_im, (1, 0, 3, 2)))
        if d == 0:
            o_re, o_im = ca_re[1:], ca_im[1:]
        else:
            o_re, o_im = ca_re[1:][::-1], ca_im[1:][::-1]
        wout_re.append(jnp.transpose(o_re, (1, 3, 0, 2)))
        wout_im.append(jnp.transpose(-o_im, (1, 3, 0, 2)))
        a_t.append((p_re[t], p_im[t]))
    g = S5_GROUPS
    m_op = (toep[0] + toep[1]).reshape(g, S5_ROW, S5_ROW)
    win = jnp.concatenate([win_re[0], win_re[1], win_im[0], win_im[1]], axis=-1).reshape(g, S5_ROW, 4 * S5_STATE)
    wout = jnp.concatenate([wout_re[0], wout_re[1], wout_im[0], wout_im[1]], axis=1).reshape(g, 4 * S5_STATE, S5_ROW)
    a_op = jnp.stack([jnp.concatenate([a_t[0][0], a_t[1][0]], -1),
                      jnp.concatenate([a_t[0][1], a_t[1][1]], -1)], axis=1)
    return m_op.astype(BF16), win.astype(BF16), wout.astype(BF16), a_op


def _s5_kernel(ul, uc, m_ref, win_ref, wout_ref, a_ref, yl, yc, xl_s, xc_s, sl_s, sc_s, *, ncl, ncc, nb):
    p = S5_STATE
    win = win_ref[0]
    xl_s[...] = _dot(ul[0], win)
    xc_s[...] = _dot(uc[0], win)
    a_re = a_ref[0, 0:1, :]
    a_im = a_ref[0, 1:2, :]
    is_f = lax.broadcasted_iota(jnp.int32, (nb, 2 * p), 1) < p

    def segment(x_s, s_s, n, carry):
        def body(i, c):
            s_re, s_im = c
            rf = pl.multiple_of(i * nb, nb)
            rb = pl.multiple_of((n - 1 - i) * nb, nb)
            s_s[pl.ds(rf, nb), 0:p] = s_re[:, 0:p]
            s_s[pl.ds(rb, nb), p:2 * p] = s_re[:, p:2 * p]
            s_s[pl.ds(rf, nb), 2 * p:3 * p] = s_im[:, 0:p]
            s_s[pl.ds(rb, nb), 3 * p:4 * p] = s_im[:, p:2 * p]
            xf = x_s[pl.ds(rf, nb), :]
            xb = x_s[pl.ds(rb, nb), :]
            x_re = jnp.where(is_f, xf[:, 0:2 * p], xb[:, 0:2 * p])
            x_im = jnp.where(is_f, xf[:, 2 * p:4 * p], xb[:, 2 * p:4 * p])
            return (a_re * s_re - a_im * s_im + x_re, a_re * s_im + a_im * s_re + x_im)

        return lax.fori_loop(0, n, body, carry)

    zero = jnp.zeros((nb, 2 * p), F32)
    carry = segment(xc_s, sc_s, ncc, (zero, zero))
    segment(xl_s, sl_s, ncl, carry)
    m_op = m_ref[0]
    wout = wout_ref[0]
    yl[0] = _dot(ul[0], m_op) + _dot(sl_s[...].astype(BF16), wout)
    yc[0] = _dot(uc[0], m_op) + _dot(sc_s[...].astype(BF16), wout)


def _s5_rows(u):
    b, n, _ = u.shape
    x = u.reshape(b, n // S5_CHUNK, S5_CHUNK, S5_GROUPS, S5_GROUP)
    return jnp.transpose(x, (3, 1, 0, 2, 4)).reshape(S5_GROUPS, (n // S5_CHUNK) * b, S5_ROW).astype(BF16)


def _s5_unrows(y, b):
    g, r, _ = y.shape
    n = r // b * S5_CHUNK
    x = y.reshape(g, r // b, b, S5_CHUNK, S5_GROUP)
    return jnp.transpose(x, (2, 1, 3, 0, 4)).reshape(b, n, S5_WIDTH)


def _s5(ops, u_l, u_c):
    m_op, win, wout, a_op = ops
    b, n, _ = u_l.shape
    nc = u_c.shape[1]
    ncl, ncc = n // S5_CHUNK, nc // S5_CHUNK
    rl, rc = ncl * b, ncc * b
    g = S5_GROUPS
    per_group = lambda r, w: pl.BlockSpec((1, r, w), lambda i: (i, 0, 0))
    yl, yc = pl.pallas_call(
        functools.partial(_s5_kernel, ncl=ncl, ncc=ncc, nb=b),
        out_shape=[jax.ShapeDtypeStruct((g, rl, S5_ROW), F32), jax.ShapeDtypeStruct((g, rc, S5_ROW), F32)],
        grid=(g,),
        in_specs=[per_group(rl, S5_ROW), per_group(rc, S5_ROW), per_group(S5_ROW, S5_ROW),
                  per_group(S5_ROW, 4 * S5_STATE), per_group(4 * S5_STATE, S5_ROW), per_group(2, 2 * S5_STATE)],
        out_specs=[per_group(rl, S5_ROW), per_group(rc, S5_ROW)],
        scratch_shapes=[pltpu.VMEM((rl, 4 * S5_STATE), F32), pltpu.VMEM((rc, 4 * S5_STATE), F32),
                        pltpu.VMEM((rl, 4 * S5_STATE), F32), pltpu.VMEM((rc, 4 * S5_STATE), F32)],
        compiler_params=_cp("parallel"),
        name="s5",
    )(_s5_rows(u_l), _s5_rows(u_c), m_op, win, wout, a_op)
    return _s5_unrows(yl, b), _s5_unrows(yc, b)


def _mix_mlp_kernel(*refs, mod_row, s5_merge, final_norm, nj):
    if s5_merge:
        (h_ref, r_ref, y5_ref, u_ref, ds_ref, wg_ref, bg_ref, wo_ref, mod_ref, nm_ref, w1_ref, w2_ref,
         *rest) = refs
    else:
        h_ref, r_ref, wo_ref, mod_ref, nm_ref, w1_ref, w2_ref, *rest = refs
    if final_norm:
        nf_ref, o_ref, h1_s, xn_s, acc_s = rest
    else:
        o_ref, h1_s, xn_s, acc_s = rest
    row = pl.program_id(0) if mod_row is None else mod_row
    j = pl.program_id(2)

    @pl.when(j == 0)
    def _():
        if s5_merge:
            y = jax.nn.gelu(y5_ref[0] + ds_ref[...] * u_ref[0])
            y = y * jax.nn.sigmoid(_dot(y.astype(BF16), wg_ref[...]) + bg_ref[...])
            mix = _dot(r_ref[0], wo_ref[0:RET_WIDTH, :]) + _dot(y.astype(BF16), wo_ref[RET_WIDTH:D_MODEL, :])
        else:
            mix = _dot(r_ref[0], wo_ref[...])
        h1 = h_ref[0] + _mod_chunk(mod_ref, row, 2) * mix
        h1_s[...] = h1
        xn = _rms(h1) * nm_ref[...]
        xn_s[...] = (xn * (1.0 + _mod_chunk(mod_ref, row, 4)) + _mod_chunk(mod_ref, row, 3)).astype(BF16)
        acc_s[...] = jnp.zeros_like(acc_s)

    a = jnp.square(jnp.maximum(_dot(xn_s[...], w1_ref[...]), 0.0))
    acc_s[...] += _dot(a.astype(BF16), w2_ref[...])

    @pl.when(j == nj - 1)
    def _():
        h2 = h1_s[...] + _mod_chunk(mod_ref, row, 5) * acc_s[...]
        if final_norm:
            h2 = _rms(h2) * nf_ref[...]
        o_ref[0] = h2


def _mix_mlp(h, r, s5y, u, s5p, wo, mod, layer, nm, w1, w2, nf, mod_row, tm, fb, name):
    b, n, _ = h.shape
    s5_merge = s5y is not None
    final_norm = nf is not None
    nj = D_FF // fb
    row_spec = lambda width: pl.BlockSpec((1, tm, width), lambda i, t, j: (i, t, 0))
    const = lambda shape: pl.BlockSpec(shape, lambda i, t, j: (0,) * len(shape))
    in_specs = [row_spec(D_MODEL), row_spec(r.shape[-1])]
    args = [h, r]
    if s5_merge:
        d_skip, w_glu, b_glu = s5p
        in_specs += [row_spec(S5_WIDTH), row_spec(S5_WIDTH), const((1, S5_WIDTH)),
                     const((S5_WIDTH, S5_WIDTH)), const((1, S5_WIDTH))]
        args += [s5y, u, d_skip, w_glu, b_glu]
    in_specs += [const((D_MODEL, D_MODEL)),
                 pl.BlockSpec((1, MOD_ROWS, N_MOD * D_MODEL), lambda i, t, j: (layer, 0, 0)),
                 const((1, D_MODEL)),
                 pl.BlockSpec((D_MODEL, fb), lambda i, t, j: (0, j)),
                 pl.BlockSpec((fb, D_MODEL), lambda i, t, j: (j, 0))]
    args += [wo, mod, nm, w1, w2]
    if final_norm:
        in_specs.append(const((1, D_MODEL)))
        args.append(nf)
    return pl.pallas_call(
        functools.partial(_mix_mlp_kernel, mod_row=mod_row, s5_merge=s5_merge, final_norm=final_norm, nj=nj),
        out_shape=jax.ShapeDtypeStruct((b, n, D_MODEL), F32),
        grid=(b, n // tm, nj),
        in_specs=in_specs,
        out_specs=row_spec(D_MODEL),
        scratch_shapes=[pltpu.VMEM((tm, D_MODEL), F32), pltpu.VMEM((tm, D_MODEL), BF16),
                        pltpu.VMEM((tm, D_MODEL), F32)],
        compiler_params=_cp("parallel", "parallel", "arbitrary"),
        name=name,
    )(*args)


def _inproj1_kernel(h_ref, mod_ref, ng_ref, w_ref, *rest, mod_row, out_cols):
    out_refs, xn_s = rest[:-1], rest[-1]
    row = pl.program_id(0) if mod_row is None else mod_row
    j = pl.program_id(2)

    @pl.when(j == 0)
    def _():
        xn = _rms(h_ref[0]) * ng_ref[...]
        xn_s[...] = (xn * (1.0 + _mod_chunk(mod_ref, row, 1)) + _mod_chunk(mod_ref, row, 0)).astype(BF16)

    y = _dot(xn_s[...], w_ref[...])
    for idx, o_ref in enumerate(out_refs):
        @pl.when(j == idx)
        def _(o_ref=o_ref):
            o_ref[0] = y.astype(o_ref.dtype)


def _inproj1(h, mod, layer, ng, w, col0, dtypes, mod_row, tm, name):
    b, n, _ = h.shape
    nout = len(dtypes)
    row_spec = pl.BlockSpec((1, tm, D_MODEL), lambda i, t, j: (i, t, 0))
    return pl.pallas_call(
        functools.partial(_inproj1_kernel, mod_row=mod_row, out_cols=nout),
        out_shape=[jax.ShapeDtypeStruct((b, n, D_MODEL), dt) for dt in dtypes],
        grid=(b, n // tm, nout),
        in_specs=[row_spec,
                  pl.BlockSpec((1, MOD_ROWS, N_MOD * D_MODEL), lambda i, t, j: (layer, 0, 0)),
                  pl.BlockSpec((1, D_MODEL), lambda i, t, j: (0, 0)),
                  pl.BlockSpec((D_MODEL, D_MODEL), lambda i, t, j: (0, col0 + j))],
        out_specs=[row_spec] * nout,
        scratch_shapes=[pltpu.VMEM((tm, D_MODEL), BF16)],
        compiler_params=_cp("parallel", "parallel", "arbitrary"),
        name=name,
    )(h, mod, ng, w)


def _cumsum_mm(tri, x):
    x1 = x.astype(BF16)
    r1 = x - x1.astype(F32)
    x2 = r1.astype(BF16)
    x3 = (r1 - x2.astype(F32)).astype(BF16)
    return _dot(tri, x1) + _dot(tri, x2) + _dot(tri, x3)


def _hgrn_kernel(lbl_ref, ng_ref, ql, ffl, fbl, il, gl, ffc, fbc, ic, o_ref, sf, sb, sbl, *, nbl, nbc, layer):
    cb = HG_BLOCK
    mid = cb // 2

    def lower_bound(d):
        z = [lbl_ref[d, k:k + 1, :] for k in range(DEPTH)]
        zmax = functools.reduce(jnp.maximum, z)
        e = [jnp.exp(v - zmax) for v in z]
        tot = functools.reduce(lambda a, b_: a + b_, e)
        lb = jnp.zeros_like(tot)
        for k in range(1, layer + 1):
            lb = lb + e[k] / tot
        return lb

    lb_f, lb_b = lower_bound(0), lower_bound(1)

    def gates(raw, lb):
        log_f = jnp.log(lb + (1.0 - lb) * jax.nn.sigmoid(raw))
        return log_f, (1.0 - lb) * jax.nn.sigmoid(-raw)

    ri = lax.broadcasted_iota(jnp.int32, (cb, cb), 0)
    ci = lax.broadcasted_iota(jnp.int32, (cb, cb), 1)
    lower = ri >= ci
    upper = ri <= ci
    tri_l = jnp.where(lower, 1.0, 0.0).astype(BF16)
    tri_u = jnp.where(upper, 1.0, 0.0).astype(BF16)

    def bwd_update(fb_raw, v):
        log_f, kb = gates(fb_raw, lb_b)
        cum = _cumsum_mm(tri_u, log_f)
        tot = cum[0:1, :]
        k_out = (kb * jnp.exp(tot - cum)).astype(BF16)
        sb[...] = sb[...] * jnp.exp(tot) + _dot_tn(v, k_out)

    def fwd_update(ff_raw, v):
        log_f, kf = gates(ff_raw, lb_f)
        cum = _cumsum_mm(tri_l, log_f)
        tot = cum[cb - 1:cb, :]
        k_out = (kf * jnp.exp(tot - cum)).astype(BF16)
        sf[...] = sf[...] * jnp.exp(tot) + _dot_tn(v, k_out)

    sb[...] = jnp.zeros_like(sb)
    for n in reversed(range(nbc)):
        sl = slice(n * cb, (n + 1) * cb)
        bwd_update(fbc[0, sl, :], ic[0, sl, :])

    def bwd_body(i, carry):
        n = nbl - 1 - i
        sl = pl.ds(pl.multiple_of(n * cb, cb), cb)
        sbl[n] = sb[...]
        bwd_update(fbl[0, sl, :], il[0, sl, :])
        return carry

    lax.fori_loop(0, nbl, bwd_body, 0)

    sf[...] = jnp.zeros_like(sf)
    for n in range(nbc):
        sl = slice(n * cb, (n + 1) * cb)
        fwd_update(ffc[0, sl, :], ic[0, sl, :])

    def fwd_body(n, carry):
        sl = pl.ds(pl.multiple_of(n * cb, cb), cb)
        q = ql[0, sl, :].astype(F32)
        v = il[0, sl, :]
        lf_f, kf = gates(ffl[0, sl, :], lb_f)
        lf_b, kb = gates(fbl[0, sl, :], lb_b)
        cum_f = _cumsum_mm(tri_l, lf_f)
        cum_b = _cumsum_mm(tri_u, lf_b)
        ref_f = cum_f[mid - 1:mid, :]
        ref_b = cum_b[mid:mid + 1, :]
        tot_f = cum_f[cb - 1:cb, :]
        att_f = _dot_nt((q * jnp.exp(cum_f - ref_f)).astype(BF16), (kf * jnp.exp(ref_f - cum_f)).astype(BF16))
        att_b = _dot_nt((q * jnp.exp(cum_b - ref_b)).astype(BF16), (kb * jnp.exp(ref_b - cum_b)).astype(BF16))
        att = jnp.where(lower, att_f, 0.0) + jnp.where(upper, att_b, 0.0)
        o = (_dot(att.astype(BF16), v)
             + _dot_nt((q * jnp.exp(cum_f)).astype(BF16), sf[...].astype(BF16))
             + _dot_nt((q * jnp.exp(cum_b)).astype(BF16), sbl[n].astype(BF16)))
        o = _rms(o) * ng_ref[...] * jax.nn.silu(gl[0, sl, :])
        o_ref[0, sl, :] = o.astype(BF16)
        k_out = (kf * jnp.exp(tot_f - cum_f)).astype(BF16)
        sf[...] = sf[...] * jnp.exp(tot_f) + _dot_tn(v, k_out)
        return carry

    lax.fori_loop(0, nbl, fwd_body, 0)


def _hgrn(lb_logits, norm_g, layer, q_l, ff_l, fb_l, i_l, g_l, ff_c, fb_c, i_c):
    b, n, _ = q_l.shape
    nc = ff_c.shape[1]
    nbl, nbc = n // HG_BLOCK, nc // HG_BLOCK
    spec = lambda rows: pl.BlockSpec((1, rows, HG_DK), lambda i, h: (i, 0, h))
    return pl.pallas_call(
        functools.partial(_hgrn_kernel, nbl=nbl, nbc=nbc, layer=layer),
        out_shape=jax.ShapeDtypeStruct((b, n, D_MODEL), BF16),
        grid=(b, HG_HEADS),
        in_specs=[pl.BlockSpec((2, DEPTH, HG_DK), lambda i, h: (0, 0, h)),
                  pl.BlockSpec((1, HG_DK), lambda i, h: (0, 0)),
                  spec(n), spec(n), spec(n), spec(n), spec(n), spec(nc), spec(nc), spec(nc)],
        out_specs=spec(n),
        scratch_shapes=[pltpu.VMEM((HG_DK, HG_DK), F32), pltpu.VMEM((HG_DK, HG_DK), F32),
                        pltpu.VMEM((nbl, HG_DK, HG_DK), F32)],
        compiler_params=_cp("parallel", "parallel"),
        name="hgrn2",
    )(lb_logits, norm_g, q_l, ff_l, fb_l, i_l, g_l, ff_c, fb_c, i_c)


def _rope_tables(n_tok):
    rows = n_tok // GRID_W
    row = jnp.broadcast_to(jnp.arange(rows, dtype=F32)[:, None], (rows, GRID_W)).reshape(-1)
    col = jnp.broadcast_to(jnp.arange(GRID_W, dtype=F32)[None, :], (rows, GRID_W)).reshape(-1)
    n_freq = RET_DK // 4
    inv = ROPE_BASE ** (-jnp.arange(n_freq, dtype=F32) / n_freq)
    ang = jnp.concatenate([row[:, None] * inv, col[:, None] * inv], axis=-1)
    cos, sin = jnp.cos(ang), jnp.sin(ang)
    cos_full = jnp.tile(jnp.concatenate([cos, cos], axis=-1), (1, RET_HEADS))
    sin_signed = jnp.tile(jnp.concatenate([-sin, sin], axis=-1), (1, RET_HEADS))
    return cos_full, sin_signed


def kernel(x, c, ctx, c_ctx, w_mod, b_mod, norm_mix, norm_mlp, w_mlp_in, w_mlp_out, ab_w_in, ab_w_out, ret_logit, s5_a_re, s5_a_im, s5_log_dt, s5_b_re, s5_b_im, s5_c_re, s5_c_im, s5_d, s5_w_glu, s5_b_glu, hg_w_in, hg_w_out, hg_lb_logits, hg_norm, norm_final):
    b, n, d = x.shape
    nc = ctx.shape[1]
    assert d == D_MODEL and b + 1 <= MOD_ROWS and w_mod.shape[0] == DEPTH == 2
    assert n % 512 == 0 and nc % 256 == 0 and n % GRID_W == 0
    ctx_row = b
    tm_l, tm_c = 512, 256

    cc = jnp.zeros((MOD_ROWS, d), F32).at[:b].set(c).at[b].set(c_ctx)
    mod = _adaln(cc, w_mod, b_mod)

    row2 = lambda a: a.reshape(1, -1)
    w_in0 = ab_w_in[0].astype(BF16)
    cos, sin = _rope_tables(n)
    ng0 = row2(norm_mix[0])
    q_l, k_l, v_l, u_l, g_l = _inproj0(x, mod, 0, ng0, w_in0, cos, sin, None, tm_l)
    q_c, k_c, v_c, u_c, g_c = _inproj0(ctx, mod, 0, ng0, w_in0, None, None, ctx_row, tm_c)

    log_gamma = jax.nn.log_sigmoid(ret_logit[0].astype(F32))
    lg_rows = jnp.broadcast_to(log_gamma.reshape(2 * RET_HEADS, 1), (2 * RET_HEADS, 2 * RET_DK))
    r_l, r_c = _retention(lg_rows, q_l, k_l, v_l, g_l, q_c, k_c, v_c, g_c)

    s5_ops = _s5_operators(s5_a_re[0], s5_a_im[0], s5_log_dt[0], s5_b_re[0], s5_b_im[0], s5_c_re[0], s5_c_im[0])
    y5_l, y5_c = _s5(s5_ops, u_l, u_c)

    s5p = (row2(s5_d[0]), s5_w_glu[0].astype(BF16), row2(s5_b_glu[0]))
    wo0 = ab_w_out[0].astype(BF16)
    w1_0, w2_0 = w_mlp_in[0].astype(BF16), w_mlp_out[0].astype(BF16)
    nm0 = row2(norm_mlp[0])
    h_l = _mix_mlp(x, r_l, y5_l, u_l, s5p, wo0, mod, 0, nm0, w1_0, w2_0, None, None, tm_l, 1024, "mix_mlp0_lat")
    h_c = _mix_mlp(ctx, r_c, y5_c, u_c, s5p, wo0, mod, 0, nm0, w1_0, w2_0, None, ctx_row, tm_c, 1024, "mix_mlp0_ctx")

    w_in1 = hg_w_in[0].astype(BF16)
    ng1 = row2(norm_mix[1])
    q1, ff_l, fb_l, i_l, g1 = _inproj1(h_l, mod, 1, ng1, w_in1, 0, (BF16, F32, F32, BF16, F32), None, tm_l,
                                       "inproj1_lat")
    ff_c, fb_c, i_c = _inproj1(h_c, mod, 1, ng1, w_in1, 1, (F32, F32, BF16), ctx_row, tm_c, "inproj1_ctx")
    o1 = _hgrn(hg_lb_logits, row2(hg_norm[0]), 1, q1, ff_l, fb_l, i_l, g1, ff_c, fb_c, i_c)
    return _mix_mlp(h_l, o1, None, None, None, hg_w_out[0].astype(BF16), mod, 1, row2(norm_mlp[1]),
                    w_mlp_in[1].astype(BF16), w_mlp_out[1].astype(BF16), row2(norm_final), None, tm_l, 1024,
                    "mix_mlp1_lat")
```

```python
import functools

import jax
import jax.numpy as jnp
from jax import lax
from jax.experimental import pallas as pl
from jax.experimental.pallas import tpu as pltpu

F32 = jnp.float32
BF16 = jnp.bfloat16

D_MODEL = 1024
DEPTH = 2
GRID_W = 64
EPS = 1e-6
ROPE_BASE = 10000.0
N_MOD = 6
RET_HEADS = 4
RET_DK = 64
RET_DV = 128
RET_QK = RET_HEADS * RET_DK
RET_WIDTH = RET_HEADS * RET_DV
RET_CHUNK = 128
S5_WIDTH = D_MODEL - RET_WIDTH
S5_GROUP = 16
S5_GROUPS = S5_WIDTH // S5_GROUP
S5_STATE = 64
S5_CHUNK = 16
S5_ROW = S5_CHUNK * S5_GROUP
LANE = 128
S5_LB = S5_WIDTH // LANE
S5_GPB = LANE // S5_GROUP
S5_BIG = S5_CHUNK * LANE
S5_HALF = S5_GPB * 2 * S5_STATE
AB_IN = 2 * RET_QK + 2 * RET_WIDTH + S5_WIDTH
HG_HEADS = 8
HG_DK = D_MODEL // HG_HEADS
HG_BLOCK = 64
HG_GROUP = 4
HG_SPLIT = 2
D_FF = 4 * D_MODEL
MOD_ROWS = 16

VMEM_LIMIT_BYTES = 56 * 1024 * 1024


def _cp(*sem):
    return pltpu.CompilerParams(dimension_semantics=sem, vmem_limit_bytes=VMEM_LIMIT_BYTES)


def _dot(a, b):
    return jnp.dot(a, b, preferred_element_type=F32)


def _dot_nt(a, b):
    return lax.dot_general(a, b, (((1,), (1,)), ((), ())), preferred_element_type=F32)


def _dot_tn(a, b):
    return lax.dot_general(a, b, (((0,), (0,)), ((), ())), preferred_element_type=F32)


def _rms(x):
    return x * lax.rsqrt(jnp.mean(x * x, axis=-1, keepdims=True) + EPS)


def _mod_chunk(mod_ref, row, i):
    return mod_ref[0, pl.ds(row, 1), i * D_MODEL:(i + 1) * D_MODEL]


def _adaln_kernel(cc_ref, w_ref, b_ref, o_ref):
    s = jax.nn.silu(cc_ref[...]).astype(BF16)
    o_ref[0] = _dot(s, w_ref[0].astype(BF16)) + b_ref[0]


def _adaln(cc, w_mod, b_mod):
    bn = 1536
    n = N_MOD * D_MODEL
    return pl.pallas_call(
        _adaln_kernel,
        out_shape=jax.ShapeDtypeStruct((DEPTH, MOD_ROWS, n), F32),
        grid=(DEPTH, n // bn),
        in_specs=[
            pl.BlockSpec((MOD_ROWS, D_MODEL), lambda l, j: (0, 0)),
            pl.BlockSpec((1, D_MODEL, bn), lambda l, j: (l, 0, j)),
            pl.BlockSpec((1, 1, bn), lambda l, j: (l, 0, j)),
        ],
        out_specs=pl.BlockSpec((1, MOD_ROWS, bn), lambda l, j: (l, 0, j)),
        compiler_params=_cp("parallel", "parallel"),
        name="adaln",
    )(cc, w_mod, b_mod.reshape(DEPTH, 1, n))


def _rope(t, cos, sin):
    lane = lax.broadcasted_iota(jnp.int32, t.shape, 1)
    first = (lane & (RET_DK // 2)) == 0
    w = t.shape[1]
    swapped = jnp.where(first, pltpu.roll(t, w - RET_DK // 2, 1), pltpu.roll(t, RET_DK // 2, 1))
    return t * cos + swapped * sin


def _inproj0_kernel(*refs, mod_row, rope):
    if rope:
        h_ref, mod_ref, ng_ref, w_ref, cos_ref, sin_ref, q_ref, k_ref, v_ref, u_ref, g_ref = refs
    else:
        h_ref, mod_ref, ng_ref, w_ref, q_ref, k_ref, v_ref, u_ref, g_ref = refs
    row = pl.program_id(0) if mod_row is None else mod_row
    xn = _rms(h_ref[0]) * ng_ref[...]
    xm = (xn * (1.0 + _mod_chunk(mod_ref, row, 1)) + _mod_chunk(mod_ref, row, 0)).astype(BF16)
    y = _dot(xm, w_ref[...])
    q = y[:, 0:RET_QK]
    k = y[:, RET_QK:2 * RET_QK]
    if rope:
        q = _rope(q, cos_ref[...], sin_ref[...])
        k = _rope(k, cos_ref[...], sin_ref[...])
    q_ref[0] = q.astype(BF16)
    k_ref[0] = (k * (RET_DK ** -0.5)).astype(BF16)
    c0 = 2 * RET_QK
    v_ref[0] = y[:, c0:c0 + RET_WIDTH].astype(BF16)
    for j in range(S5_LB):
        lo = c0 + RET_WIDTH + j * LANE
        u_ref[0, j] = y[:, lo:lo + LANE]
    g_ref[0] = y[:, c0 + RET_WIDTH + S5_WIDTH:]


def _inproj0(h, mod, layer, ng, w, cos, sin, mod_row, tm):
    b, n, _ = h.shape
    rope = cos is not None
    row_spec = lambda width: pl.BlockSpec((1, tm, width), lambda i, j: (i, j, 0))
    in_specs = [
        row_spec(D_MODEL),
        pl.BlockSpec((1, MOD_ROWS, N_MOD * D_MODEL), lambda i, j: (layer, 0, 0)),
        pl.BlockSpec((1, D_MODEL), lambda i, j: (0, 0)),
        pl.BlockSpec((D_MODEL, AB_IN), lambda i, j: (0, 0)),
    ]
    args = [h, mod, ng, w]
    if rope:
        in_specs += [pl.BlockSpec((tm, RET_QK), lambda i, j: (j, 0))] * 2
        args += [cos, sin]
    widths = (RET_QK, RET_QK, RET_WIDTH, S5_WIDTH, RET_WIDTH)
    dtypes = (BF16, BF16, BF16, F32, F32)
    out_shape = [jax.ShapeDtypeStruct((b, n, wd), dt) for wd, dt in zip(widths, dtypes)]
    out_specs = [row_spec(wd) for wd in widths]
    out_shape[3] = jax.ShapeDtypeStruct((b, S5_LB, n, LANE), F32)
    out_specs[3] = pl.BlockSpec((1, S5_LB, tm, LANE), lambda i, j: (i, 0, j, 0))
    return pl.pallas_call(
        functools.partial(_inproj0_kernel, mod_row=mod_row, rope=rope),
        out_shape=out_shape,
        grid=(b, n // tm),
        in_specs=in_specs,
        out_specs=out_specs,
        compiler_params=_cp("parallel", "parallel"),
        name="inproj0_lat" if rope else "inproj0_ctx",
    )(*args)


def _ret_kernel(lg_ref, ql, kl, vl, gl, qc, kc, vc, gc, rl, rc, sf, sb, sbl, sbc, *, ncl, ncc):
    c = RET_CHUNK
    p = pl.program_id(1)
    h_a = 2 * p
    lgf_a = lg_ref[pl.ds(h_a, 1), :]
    lgf_b = lg_ref[pl.ds(h_a + 1, 1), :]
    lgb_a = lg_ref[pl.ds(RET_HEADS + h_a, 1), :]
    lgb_b = lg_ref[pl.ds(RET_HEADS + h_a + 1, 1), :]
    lane = lax.broadcasted_iota(jnp.int32, (1, 2 * RET_DK), 1)
    is_a = lane < RET_DK
    lgf_lane = jnp.where(is_a, lgf_a, lgf_b)
    lgb_lane = jnp.where(is_a, lgb_a, lgb_b)
    ri = lax.broadcasted_iota(jnp.int32, (c, c), 0).astype(F32)
    ci = lax.broadcasted_iota(jnp.int32, (c, c), 1).astype(F32)
    diff = ri - ci

    def dmat(lgf, lgb):
        fwd = jnp.exp(jnp.maximum(diff, 0.0) * lgf)
        bwd = jnp.exp(jnp.maximum(-diff, 0.0) * lgb)
        return jnp.where(diff > 0, fwd, jnp.where(diff < 0, bwd, 2.0))

    d_a = dmat(lgf_a, lgb_a)
    d_b = dmat(lgf_b, lgb_b)
    rowp = lax.broadcasted_iota(jnp.int32, (c, 2 * RET_DK), 0).astype(F32)
    qd_f = jnp.exp((rowp + 1.0) * lgf_lane)
    qd_b = jnp.exp((c - rowp) * lgb_lane)
    kd_f = jnp.exp((c - 1.0 - rowp) * lgf_lane)
    kd_b = jnp.exp(rowp * lgb_lane)
    rowk = lax.broadcasted_iota(jnp.int32, (2 * RET_DK, 2 * RET_DV), 0)
    cd_f = jnp.exp(c * jnp.where(rowk < RET_DK, lgf_a[:, :1], lgf_b[:, :1]))
    cd_b = jnp.exp(c * jnp.where(rowk < RET_DK, lgb_a[:, :1], lgb_b[:, :1]))
    mask_a = jnp.where(is_a, 1.0, 0.0).astype(BF16)
    mask_b = jnp.where(is_a, 0.0, 1.0).astype(BF16)

    sb[...] = jnp.zeros_like(sb)

    def bwd_step(k, v, store_ref, idx):
        store_ref[idx] = sb[...]
        kd = (k.astype(F32) * kd_b).astype(BF16)
        sb[...] = cd_b * sb[...] + _dot_tn(kd, v)

    for n in reversed(range(ncc)):
        bwd_step(kc[0, n * c:(n + 1) * c, :], vc[0, n * c:(n + 1) * c, :], sbc, n)

    def bwd_body(i, carry):
        n = ncl - 1 - i
        st = pl.multiple_of(n * c, c)
        bwd_step(kl[0, pl.ds(st, c), :], vl[0, pl.ds(st, c), :], sbl, n)
        return carry

    lax.fori_loop(0, ncl, bwd_body, 0)

    sf[...] = jnp.zeros_like(sf)

    def fwd_step(q, k, v, g, sb_prev, out_ref, st):
        qf = q.astype(F32)
        q_f = (qf * qd_f).astype(BF16)
        q_b = (qf * qd_b).astype(BF16)
        s_f = sf[...].astype(BF16)
        s_b = sb_prev.astype(BF16)
        for m, dm, cs in ((mask_a, d_a, 0), (mask_b, d_b, RET_DV)):
            att = _dot_nt(q * m, k) * dm
            o = (_dot(att.astype(BF16), v[:, cs:cs + RET_DV])
                 + _dot(q_f * m, s_f[:, cs:cs + RET_DV])
                 + _dot(q_b * m, s_b[:, cs:cs + RET_DV]))
            o = _rms(o) * jax.nn.silu(g[:, cs:cs + RET_DV])
            out_ref[0, pl.ds(st, c), cs:cs + RET_DV] = o.astype(BF16)
        kd = (k.astype(F32) * kd_f).astype(BF16)
        sf[...] = cd_f * sf[...] + _dot_tn(kd, v)

    for n in range(ncc):
        sl = slice(n * c, (n + 1) * c)
        fwd_step(qc[0, sl, :], kc[0, sl, :], vc[0, sl, :], gc[0, sl, :], sbc[n], rc, n * c)

    def fwd_body(n, carry):
        st = pl.multiple_of(n * c, c)
        sl = pl.ds(st, c)
        fwd_step(ql[0, sl, :], kl[0, sl, :], vl[0, sl, :], gl[0, sl, :], sbl[n], rl, st)
        return carry

    lax.fori_loop(0, ncl, fwd_body, 0)


def _retention(lg_rows, q_l, k_l, v_l, g_l, q_c, k_c, v_c, g_c):
    b, n, _ = q_l.shape
    nc = q_c.shape[1]
    ncl, ncc = n // RET_CHUNK, nc // RET_CHUNK
    pairs = RET_HEADS // 2

    def spec(rows, width):
        return pl.BlockSpec((1, rows, width), lambda i, p: (i, 0, p))

    return pl.pallas_call(
        functools.partial(_ret_kernel, ncl=ncl, ncc=ncc),
        out_shape=[jax.ShapeDtypeStruct((b, n, RET_WIDTH), BF16),
                   jax.ShapeDtypeStruct((b, nc, RET_WIDTH), BF16)],
        grid=(b, pairs),
        in_specs=[pl.BlockSpec((2 * RET_HEADS, 2 * RET_DK), lambda i, p: (0, 0)),
                  spec(n, 2 * RET_DK), spec(n, 2 * RET_DK), spec(n, 2 * RET_DV), spec(n, 2 * RET_DV),
                  spec(nc, 2 * RET_DK), spec(nc, 2 * RET_DK), spec(nc, 2 * RET_DV), spec(nc, 2 * RET_DV)],
        out_specs=[spec(n, 2 * RET_DV), spec(nc, 2 * RET_DV)],
        scratch_shapes=[pltpu.VMEM((2 * RET_DK, 2 * RET_DV), F32),
                        pltpu.VMEM((2 * RET_DK, 2 * RET_DV), F32),
                        pltpu.VMEM((ncl, 2 * RET_DK, 2 * RET_DV), F32),
                        pltpu.VMEM((ncc, 2 * RET_DK, 2 * RET_DV), F32)],
        compiler_params=_cp("parallel", "parallel"),
        name="retention",
    )(lg_rows, q_l, k_l, v_l, g_l, q_c, k_c, v_c, g_c)


def _s5_operators(a_re, a_im, log_dt, b_re, b_im, c_re, c_im):
    t = S5_CHUNK
    hp = lax.Precision.HIGHEST
    ks = jnp.arange(t + 1, dtype=F32)
    win_re, win_im, wout_re, wout_im, a_t, toep = [], [], [], [], [], []
    for d in range(2):
        are, aim = a_re[d].astype(F32), a_im[d].astype(F32)
        dt = jnp.exp(log_dt[d].astype(F32))[:, None]
        mag = jnp.exp(are * dt)
        ang = aim * dt
        ab_re, ab_im = mag * jnp.cos(ang), mag * jnp.sin(ang)
        nr, ni = ab_re - 1.0, ab_im
        den = jnp.square(are) + jnp.square(aim)
        fr = (nr * are + ni * aim) / den
        fi = (ni * are - nr * aim) / den
        bre, bim = b_re[d].astype(F32), b_im[d].astype(F32)
        bb_re = fr[..., None] * bre - fi[..., None] * bim
        bb_im = fr[..., None] * bim + fi[..., None] * bre
        pmag = jnp.exp(ks[:, None, None] * (are * dt)[None])
        pang = ks[:, None, None] * ang[None]
        p_re, p_im = pmag * jnp.cos(pang), pmag * jnp.sin(pang)
        cre, cim = c_re[d].astype(F32), c_im[d].astype(F32)
        ca_re = cre[None] * p_re[:, :, None, :] - cim[None] * p_im[:, :, None, :]
        ca_im = cre[None] * p_im[:, :, None, :] + cim[None] * p_re[:, :, None, :]
        kk = (jnp.einsum('kgmp,gpn->kgmn', ca_re, bb_re, precision=hp)
              - jnp.einsum('kgmp,gpn->kgmn', ca_im, bb_im, precision=hp))
        s_idx = jnp.arange(t)[:, None]
        t_idx = jnp.arange(t)[None, :]
        if d == 0:
            lag, valid = t_idx - s_idx, t_idx >= s_idx
        else:
            lag, valid = s_idx - t_idx, s_idx >= t_idx
        blocks = kk[jnp.clip(lag, 0, t)]
        blocks = jnp.where(valid[:, :, None, None, None], blocks, 0.0)
        toep.append(jnp.transpose(blocks, (2, 0, 4, 1, 3)))
        pw = p_re[:t], p_im[:t]
        if d == 0:
            pw = pw[0][::-1], pw[1][::-1]
        w_re = pw[0][:, :, :, None] * bb_re[None] - pw[1][:, :, :, None] * bb_im[None]
        w_im = pw[0][:, :, :, None] * bb_im[None] + pw[1][:, :, :, None] * bb_re[None]
        win_re.append(jnp.transpose(w_re, (1, 0, 3, 2)))
        win_im.append(jnp.transpose(w_im, (1, 0, 3, 2)))
        if d == 0:
            o_re, o_im = ca_re[1:], ca_im[1:]
        else:
            o_re, o_im = ca_re[1:][::-1], ca_im[1:][::-1]
        wout_re.append(jnp.transpose(o_re, (1, 3, 0, 2)))
        wout_im.append(jnp.transpose(-o_im, (1, 3, 0, 2)))
        a_t.append((p_re[t], p_im[t]))
    g = S5_GROUPS
    m_op = (toep[0] + toep[1]).reshape(g, S5_ROW, S5_ROW)
    win = jnp.concatenate([win_re[0], win_re[1], win_im[0], win_im[1]], axis=-1).reshape(g, S5_ROW, 4 * S5_STATE)
    wout = jnp.concatenate([wout_re[0], wout_re[1], wout_im[0], wout_im[1]], axis=1).reshape(g, 4 * S5_STATE, S5_ROW)
    a_op = jnp.stack([jnp.concatenate([a_t[0][0], a_t[1][0]], -1),
                      jnp.concatenate([a_t[0][1], a_t[1][1]], -1)], axis=1)
    return m_op.astype(BF16), win.astype(BF16), wout.astype(BF16), a_op


def _s5_big_operators(ops):
    m_op, win, wout, a_op = ops
    t, g, p = S5_CHUNK, S5_GROUP, S5_STATE
    eye = jnp.eye(S5_GPB, dtype=m_op.dtype)
    m6 = m_op.reshape(S5_LB, S5_GPB, t, g, t, g)
    m_big = jnp.einsum('jqsmtn,qr->jsqmtrn', m6, eye).reshape(S5_LB, S5_BIG, S5_BIG)
    w7 = win.reshape(S5_LB, S5_GPB, t, g, 2, 2, p)
    win_big = jnp.einsum('jqsmifp,qr->jsqmifrp', w7, eye).reshape(S5_LB, S5_BIG, 2 * S5_HALF)
    o7 = wout.reshape(S5_LB, S5_GPB, 2, 2, p, t, g)
    wout_big = jnp.einsum('jqifptm,qr->jifqptrm', o7, eye).reshape(S5_LB, 2 * S5_HALF, S5_BIG)
    a_big = a_op.reshape(S5_LB, S5_GPB, 2, 2, p).transpose(0, 2, 3, 1, 4).reshape(S5_LB, 2, S5_HALF)
    return m_big, win_big, wout_big, a_big


def _s5_kernel(ul, uc, m_ref, win_ref, wout_ref, a_ref, yl, yc, x_s, s_s, *, ncl, ncc):
    hw = S5_HALF
    hh = hw // 2
    zc = uc[0, 0].astype(BF16)
    zl = ul[0, 0].astype(BF16)
    x_s[0:ncc, :] = _dot(zc, win_ref[0])
    x_s[ncc:ncc + ncl, :] = _dot(zl, win_ref[0])
    a_re = a_ref[0, 0:1, :]
    a_im = a_ref[0, 1:2, :]

    def segment(base, n, carry):
        def body(i, c):
            s_re, s_im = c
            rf = pl.ds(base + i, 1)
            rb = pl.ds(base + n - 1 - i, 1)
            s_s[rf, 0:hh] = s_re[:, 0:hh]
            s_s[rb, hh:hw] = s_re[:, hh:hw]
            s_s[rf, hw:hw + hh] = s_im[:, 0:hh]
            s_s[rb, hw + hh:2 * hw] = s_im[:, hh:hw]
            x_re = jnp.concatenate([x_s[rf, 0:hh], x_s[rb, hh:hw]], axis=-1)
            x_im = jnp.concatenate([x_s[rf, hw:hw + hh], x_s[rb, hw + hh:2 * hw]], axis=-1)
            return (a_re * s_re - a_im * s_im + x_re, a_re * s_im + a_im * s_re + x_im)

        return lax.fori_loop(0, n, body, carry)

    zero = jnp.zeros((1, hw), F32)
    carry = segment(0, ncc, (zero, zero))
    segment(ncc, ncl, carry)
    sp = s_s[...].astype(BF16)
    yc[0, 0] = _dot(zc, m_ref[0]) + _dot(sp[0:ncc], wout_ref[0])
    yl[0, 0] = _dot(zl, m_ref[0]) + _dot(sp[ncc:ncc + ncl], wout_ref[0])


def _s5(big_ops, u_l, u_c):
    m_big, win_big, wout_big, a_big = big_ops
    b, _, n, _ = u_l.shape
    nc = u_c.shape[2]
    ncl, ncc = n // S5_CHUNK, nc // S5_CHUNK
    zl = u_l.reshape(b, S5_LB, ncl, S5_BIG)
    zc = u_c.reshape(b, S5_LB, ncc, S5_BIG)
    rows = lambda r: pl.BlockSpec((1, 1, r, S5_BIG), lambda j, i: (i, j, 0, 0))
    wspec = lambda r, c: pl.BlockSpec((1, r, c), lambda j, i: (j, 0, 0), pipeline_mode=pl.Buffered(1))
    yl, yc = pl.pallas_call(
        functools.partial(_s5_kernel, ncl=ncl, ncc=ncc),
        out_shape=[jax.ShapeDtypeStruct(zl.shape, F32), jax.ShapeDtypeStruct(zc.shape, F32)],
        grid=(S5_LB, b),
        in_specs=[rows(ncl), rows(ncc), wspec(S5_BIG, S5_BIG), wspec(S5_BIG, 2 * S5_HALF),
                  wspec(2 * S5_HALF, S5_BIG), wspec(2, S5_HALF)],
        out_specs=[rows(ncl), rows(ncc)],
        scratch_shapes=[pltpu.VMEM((ncc + ncl, 2 * S5_HALF), F32), pltpu.VMEM((ncc + ncl, 2 * S5_HALF), F32)],
        compiler_params=_cp("parallel", "arbitrary"),
        name="s5",
    )(zl, zc, m_big, win_big, wout_big, a_big)
    return yl.reshape(u_l.shape), yc.reshape(u_c.shape)


def _mix_mlp_kernel(*refs, mod_row, s5_merge, final_norm, nj):
    if s5_merge:
        (h_ref, r_ref, y5_ref, u_ref, ds_ref, wg_ref, bg_ref, wo_ref, mod_ref, nm_ref, w1_ref, w2_ref,
         *rest) = refs
    else:
        h_ref, r_ref, wo_ref, mod_ref, nm_ref, w1_ref, w2_ref, *rest = refs
    if final_norm:
        nf_ref, o_ref, h1_s, xn_s, acc_s = rest
    else:
        o_ref, h1_s, xn_s, acc_s = rest
    row = pl.program_id(0) if mod_row is None else mod_row
    j = pl.program_id(2)

    @pl.when(j == 0)
    def _():
        if s5_merge:
            y5 = jnp.concatenate([y5_ref[0, lb] for lb in range(S5_LB)], axis=-1)
            u = jnp.concatenate([u_ref[0, lb] for lb in range(S5_LB)], axis=-1)
            y = jax.nn.gelu(y5 + ds_ref[...] * u)
            y = y * jax.nn.sigmoid(_dot(y.astype(BF16), wg_ref[...]) + bg_ref[...])
            mix = _dot(r_ref[0], wo_ref[0:RET_WIDTH, :]) + _dot(y.astype(BF16), wo_ref[RET_WIDTH:D_MODEL, :])
        else:
            mix = _dot(r_ref[0], wo_ref[...])
        h1 = h_ref[0] + _mod_chunk(mod_ref, row, 2) * mix
        h1_s[...] = h1
        xn = _rms(h1) * nm_ref[...]
        xn_s[...] = (xn * (1.0 + _mod_chunk(mod_ref, row, 4)) + _mod_chunk(mod_ref, row, 3)).astype(BF16)
        acc_s[...] = jnp.zeros_like(acc_s)

    a = jnp.square(jnp.maximum(_dot(xn_s[...], w1_ref[...]), 0.0))
    acc_s[...] += _dot(a.astype(BF16), w2_ref[...])

    @pl.when(j == nj - 1)
    def _():
        h2 = h1_s[...] + _mod_chunk(mod_ref, row, 5) * acc_s[...]
        if final_norm:
            h2 = _rms(h2) * nf_ref[...]
        o_ref[0] = h2


def _mix_mlp(h, r, s5y, u, s5p, wo, mod, layer, nm, w1, w2, nf, mod_row, tm, fb, name):
    b, n, _ = h.shape
    s5_merge = s5y is not None
    final_norm = nf is not None
    nj = D_FF // fb
    row_spec = lambda width: pl.BlockSpec((1, tm, width), lambda i, t, j: (i, t, 0))
    const = lambda shape: pl.BlockSpec(shape, lambda i, t, j: (0,) * len(shape))
    in_specs = [row_spec(D_MODEL), row_spec(r.shape[-1])]
    args = [h, r]
    if s5_merge:
        d_skip, w_glu, b_glu = s5p
        lb_spec = pl.BlockSpec((1, S5_LB, tm, LANE), lambda i, t, j: (i, 0, t, 0))
        in_specs += [lb_spec, lb_spec, const((1, S5_WIDTH)),
                     const((S5_WIDTH, S5_WIDTH)), const((1, S5_WIDTH))]
        args += [s5y, u, d_skip, w_glu, b_glu]
    in_specs += [const((D_MODEL, D_MODEL)),
                 pl.BlockSpec((1, MOD_ROWS, N_MOD * D_MODEL), lambda i, t, j: (layer, 0, 0)),
                 const((1, D_MODEL)),
                 pl.BlockSpec((D_MODEL, fb), lambda i, t, j: (0, j)),
                 pl.BlockSpec((fb, D_MODEL), lambda i, t, j: (j, 0))]
    args += [wo, mod, nm, w1, w2]
    if final_norm:
        in_specs.append(const((1, D_MODEL)))
        args.append(nf)
    return pl.pallas_call(
        functools.partial(_mix_mlp_kernel, mod_row=mod_row, s5_merge=s5_merge, final_norm=final_norm, nj=nj),
        out_shape=jax.ShapeDtypeStruct((b, n, D_MODEL), F32),
        grid=(b, n // tm, nj),
        in_specs=in_specs,
        out_specs=row_spec(D_MODEL),
        scratch_shapes=[pltpu.VMEM((tm, D_MODEL), F32), pltpu.VMEM((tm, D_MODEL), BF16),
                        pltpu.VMEM((tm, D_MODEL), F32)],
        compiler_params=_cp("parallel", "parallel", "arbitrary"),
        name=name,
    )(*args)


def _inproj1_kernel(h_ref, mod_ref, ng_ref, w_ref, *rest, mod_row):
    out_refs, xn_s = rest[:-1], rest[-1]
    row = pl.program_id(0) if mod_row is None else mod_row
    j = pl.program_id(2)

    @pl.when(j == 0)
    def _():
        xn = _rms(h_ref[0]) * ng_ref[...]
        xn_s[...] = (xn * (1.0 + _mod_chunk(mod_ref, row, 1)) + _mod_chunk(mod_ref, row, 0)).astype(BF16)

    y = _dot(xn_s[...], w_ref[...])
    for idx, o_ref in enumerate(out_refs):
        @pl.when(j == idx)
        def _(o_ref=o_ref):
            o_ref[0] = y.astype(o_ref.dtype)


def _inproj1(h, mod, layer, ng, w, col0, dtypes, mod_row, tm, name):
    b, n, _ = h.shape
    nout = len(dtypes)
    row_spec = pl.BlockSpec((1, tm, D_MODEL), lambda i, t, j: (i, t, 0))
    return pl.pallas_call(
        functools.partial(_inproj1_kernel, mod_row=mod_row),
        out_shape=[jax.ShapeDtypeStruct((b, n, D_MODEL), dt) for dt in dtypes],
        grid=(b, n // tm, nout),
        in_specs=[row_spec,
                  pl.BlockSpec((1, MOD_ROWS, N_MOD * D_MODEL), lambda i, t, j: (layer, 0, 0)),
                  pl.BlockSpec((1, D_MODEL), lambda i, t, j: (0, 0)),
                  pl.BlockSpec((D_MODEL, D_MODEL), lambda i, t, j: (0, col0 + j))],
        out_specs=[row_spec] * nout,
        scratch_shapes=[pltpu.VMEM((tm, D_MODEL), BF16)],
        compiler_params=_cp("parallel", "parallel", "arbitrary"),
        name=name,
    )(h, mod, ng, w)


def _cumsum_mm(tri, x):
    acc = None
    r = x
    for i in range(HG_SPLIT):
        p = r.astype(BF16)
        acc = _dot(tri, p) if acc is None else acc + _dot(tri, p)
        if i + 1 < HG_SPLIT:
            r = r - p.astype(F32)
    return acc


def _hgrn_kernel(lbl_ref, ng_ref, ql, ffl, fbl, il, gl, ffc, fbc, ic, o_ref,
                 qin_s, att_s, kv_s, et_s, kvc_s, etc_s, cum_s, kk_s, *, nbl, nbc, layer, unroll, out_blocks):
    cb = HG_BLOCK
    mid = cb // 2
    gb = HG_GROUP
    gr = gb * cb
    dk = HG_DK

    def lower_bound(d):
        z = [lbl_ref[d, k:k + 1, :] for k in range(DEPTH)]
        zmax = functools.reduce(jnp.maximum, z)
        e = [jnp.exp(v - zmax) for v in z]
        tot = functools.reduce(lambda a, b_: a + b_, e)
        lb = jnp.zeros_like(tot)
        for k in range(1, layer + 1):
            lb = lb + e[k] / tot
        return lb

    ri = lax.broadcasted_iota(jnp.int32, (gr, gr), 0)
    ci = lax.broadcasted_iota(jnp.int32, (gr, gr), 1)
    same = (ri // cb) == (ci // cb)
    rb = lax.broadcasted_iota(jnp.int32, (cb, cb), 0)
    cbi = lax.broadcasted_iota(jnp.int32, (cb, cb), 1)
    tri_l = jnp.where(same & (ri >= ci), 1.0, 0.0).astype(BF16)
    dirs = ((0, lower_bound(0), rb >= cbi, mid - 1, cb - 1), (1, lower_bound(1), rb <= cbi, mid, 0))

    def gates(raws, slot):
        s = [jax.nn.sigmoid(r) for r in raws]
        t = [(1.0 - dr[1]) * si for dr, si in zip(dirs, s)]
        log_f = [jnp.log(dr[1] + ti) for dr, ti in zip(dirs, t)]
        pre = _cumsum_mm(tri_l, jnp.concatenate(log_f, axis=-1))
        pre_b = pre[:, dk:].reshape(gb, cb, dk)
        cum_s[slot, 0] = pre[:, :dk]
        cum_s[slot, 1] = (pre_b[:, cb - 1:cb, :] - pre_b).reshape(gr, dk) + log_f[1]
        for d in range(2):
            kk_s[slot, d] = (1.0 - dirs[d][1]) - t[d]

    def products(slot, v, q, n0, kv_ref, et_ref):
        v3 = v.reshape(gb, cb, dk)
        qts, kts = [], []
        for d, lb, keep, ref_row, tot_row in dirs:
            cum = cum_s[slot, d].reshape(gb, cb, dk)
            kk = kk_s[slot, d].reshape(gb, cb, dk)
            ref = cum[:, ref_row:ref_row + 1, :]
            tot = cum[:, tot_row:tot_row + 1, :]
            e = cum - ref
            kt = kk * jnp.exp(-e)
            k_out = (kt * jnp.exp(tot - ref)).astype(BF16)
            e_tot = jnp.exp(tot)
            if q is not None:
                qt = q.astype(F32).reshape(gb, cb, dk) * jnp.exp(e)
                qin_s[pl.ds(pl.multiple_of(n0 * cb, gr), gr), d * dk:(d + 1) * dk] = (
                    (qt * jnp.exp(ref)).astype(BF16).reshape(gr, dk))
                qts.append(qt.astype(BF16))
                kts.append(kt.astype(BF16))
            for j in range(gb):
                kv_ref[n0 + j, :, d * dk:(d + 1) * dk] = _dot_tn(v3[j], k_out[j])
                et_ref[n0 + j, :, d * dk:(d + 1) * dk] = e_tot[j]
        if q is not None:
            for j in range(gb):
                att = (jnp.where(dirs[0][2], _dot_nt(qts[0][j], kts[0][j]), 0.0)
                       + jnp.where(dirs[1][2], _dot_nt(qts[1][j], kts[1][j]), 0.0))
                att_s[n0 + j] = att.astype(BF16)

    for g in range(nbc // gb):
        sl = slice(g * gr, (g + 1) * gr)
        gates((ffc[0, sl, :], fbc[0, sl, :]), 0)
        products(0, ic[0, sl, :], None, g * gb, kvc_s, etc_s)

    ngl = nbl // gb

    def lat_gates(g, slot):
        sl = pl.ds(pl.multiple_of(g * gr, gr), gr)
        gates((ffl[0, sl, :], fbl[0, sl, :]), slot)

    def lat_products(g, slot):
        sl = pl.ds(pl.multiple_of(g * gr, gr), gr)
        products(slot, il[0, sl, :], ql[0, sl, :], g * gb, kv_s, et_s)

    lat_gates(0, 0)

    def prep_body(g, carry):
        slot = g % 2
        lat_products(g, slot)
        lat_gates(g + 1, 1 - slot)
        return carry

    lax.fori_loop(0, ngl - 1, prep_body, 0)
    lat_products(ngl - 1, (ngl - 1) % 2)

    lane = lax.broadcasted_iota(jnp.int32, (dk, 2 * dk), 1)
    is_f = lane < dk
    st = jnp.zeros((dk, 2 * dk), F32)
    for n in range(nbc):
        m = nbc - 1 - n
        st = (st * jnp.where(is_f[:1], etc_s[n], etc_s[m]) + jnp.where(is_f, kvc_s[n], kvc_s[m]))

    def rec_body(t, st):
        u = nbl - 1 - t
        inc = jnp.where(is_f, kv_s[t], kv_s[u])
        dec = jnp.where(is_f[:1], et_s[t], et_s[u])
        kv_s[t, :, 0:dk] = st[:, 0:dk]
        kv_s[u, :, dk:2 * dk] = st[:, dk:2 * dk]
        return st * dec + inc

    lax.fori_loop(0, nbl, rec_body, st, unroll=unroll)

    def out_body(i, carry):
        for j in range(out_blocks):
            n = i * out_blocks + j
            sl = pl.ds(pl.multiple_of(n * cb, cb), cb)
            o = _dot(att_s[n], il[0, sl, :]) + _dot_nt(qin_s[sl, :], kv_s[n].astype(BF16))
            o = _rms(o) * ng_ref[...] * jax.nn.silu(gl[0, sl, :])
            o_ref[0, sl, :] = o.astype(BF16)
        return carry

    lax.fori_loop(0, nbl // out_blocks, out_body, 0)


def _hgrn(lb_logits, norm_g, layer, q_l, ff_l, fb_l, i_l, g_l, ff_c, fb_c, i_c):
    b, n, _ = q_l.shape
    nc = ff_c.shape[1]
    nbl, nbc = n // HG_BLOCK, nc // HG_BLOCK
    out_blocks = min(16, nbl)
    assert nbl % HG_GROUP == 0 and nbc % HG_GROUP == 0 and nbl % out_blocks == 0
    spec = lambda rows: pl.BlockSpec((1, rows, HG_DK), lambda i, h: (i, 0, h))
    return pl.pallas_call(
        functools.partial(_hgrn_kernel, nbl=nbl, nbc=nbc, layer=layer, unroll=2, out_blocks=out_blocks),
        out_shape=jax.ShapeDtypeStruct((b, n, D_MODEL), BF16),
        grid=(b, HG_HEADS),
        in_specs=[pl.BlockSpec((2, DEPTH, HG_DK), lambda i, h: (0, 0, h)),
                  pl.BlockSpec((1, HG_DK), lambda i, h: (0, 0)),
                  spec(n), spec(n), spec(n), spec(n), spec(n), spec(nc), spec(nc), spec(nc)],
        out_specs=spec(n),
        scratch_shapes=[pltpu.VMEM((n, 2 * HG_DK), BF16),
                        pltpu.VMEM((nbl, HG_BLOCK, HG_BLOCK), BF16),
                        pltpu.VMEM((nbl, HG_DK, 2 * HG_DK), F32),
                        pltpu.VMEM((nbl, 1, 2 * HG_DK), F32),
                        pltpu.VMEM((nbc, HG_DK, 2 * HG_DK), F32),
                        pltpu.VMEM((nbc, 1, 2 * HG_DK), F32),
                        pltpu.VMEM((2, 2, HG_GROUP * HG_BLOCK, HG_DK), F32),
                        pltpu.VMEM((2, 2, HG_GROUP * HG_BLOCK, HG_DK), F32)],
        compiler_params=_cp("parallel", "parallel"),
        name="hgrn2",
    )(lb_logits, norm_g, q_l, ff_l, fb_l, i_l, g_l, ff_c, fb_c, i_c)


def _rope_tables(n_tok):
    rows = n_tok // GRID_W
    row = jnp.broadcast_to(jnp.arange(rows, dtype=F32)[:, None], (rows, GRID_W)).reshape(-1)
    col = jnp.broadcast_to(jnp.arange(GRID_W, dtype=F32)[None, :], (rows, GRID_W)).reshape(-1)
    n_freq = RET_DK // 4
    inv = ROPE_BASE ** (-jnp.arange(n_freq, dtype=F32) / n_freq)
    ang = jnp.concatenate([row[:, None] * inv, col[:, None] * inv], axis=-1)
    cos, sin = jnp.cos(ang), jnp.sin(ang)
    cos_full = jnp.tile(jnp.concatenate([cos, cos], axis=-1), (1, RET_HEADS))
    sin_signed = jnp.tile(jnp.concatenate([-sin, sin], axis=-1), (1, RET_HEADS))
    return cos_full, sin_signed


def kernel(x, c, ctx, c_ctx, w_mod, b_mod, norm_mix, norm_mlp, w_mlp_in, w_mlp_out, ab_w_in, ab_w_out, ret_logit, s5_a_re, s5_a_im, s5_log_dt, s5_b_re, s5_b_im, s5_c_re, s5_c_im, s5_d, s5_w_glu, s5_b_glu, hg_w_in, hg_w_out, hg_lb_logits, hg_norm, norm_final):
    b, n, d = x.shape
    nc = ctx.shape[1]
    assert d == D_MODEL and b + 1 <= MOD_ROWS and w_mod.shape[0] == DEPTH == 2
    assert n % 512 == 0 and nc % 256 == 0 and n % GRID_W == 0
    ctx_row = b
    tm_l, tm_c = 512, 256

    cc = jnp.zeros((MOD_ROWS, d), F32).at[:b].set(c).at[b].set(c_ctx)
    mod = _adaln(cc, w_mod, b_mod)

    row2 = lambda a: a.reshape(1, -1)
    w_in0 = ab_w_in[0].astype(BF16)
    cos, sin = _rope_tables(n)
    ng0 = row2(norm_mix[0])
    q_l, k_l, v_l, u_l, g_l = _inproj0(x, mod, 0, ng0, w_in0, cos, sin, None, tm_l)
    q_c, k_c, v_c, u_c, g_c = _inproj0(ctx, mod, 0, ng0, w_in0, None, None, ctx_row, tm_c)

    log_gamma = jax.nn.log_sigmoid(ret_logit[0].astype(F32))
    lg_rows = jnp.broadcast_to(log_gamma.reshape(2 * RET_HEADS, 1), (2 * RET_HEADS, 2 * RET_DK))
    r_l, r_c = _retention(lg_rows, q_l, k_l, v_l, g_l, q_c, k_c, v_c, g_c)

    s5_ops = _s5_operators(s5_a_re[0], s5_a_im[0], s5_log_dt[0], s5_b_re[0], s5_b_im[0], s5_c_re[0], s5_c_im[0])
    y5_l, y5_c = _s5(_s5_big_operators(s5_ops), u_l, u_c)

    s5p = (row2(s5_d[0]), s5_w_glu[0].astype(BF16), row2(s5_b_glu[0]))
    wo0 = ab_w_out[0].astype(BF16)
    w1_0, w2_0 = w_mlp_in[0].astype(BF16), w_mlp_out[0].astype(BF16)
    nm0 = row2(norm_mlp[0])
    h_l = _mix_mlp(x, r_l, y5_l, u_l, s5p, wo0, mod, 0, nm0, w1_0, w2_0, None, None, tm_l, 1024, "mix_mlp0_lat")
    h_c = _mix_mlp(ctx, r_c, y5_c, u_c, s5p, wo0, mod, 0, nm0, w1_0, w2_0, None, ctx_row, tm_c, 1024, "mix_mlp0_ctx")

    w_in1 = hg_w_in[0].astype(BF16)
    ng1 = row2(norm_mix[1])
    q1, ff_l, fb_l, i_l, g1 = _inproj1(h_l, mod, 1, ng1, w_in1, 0, (BF16, F32, F32, BF16, F32), None, tm_l,
                                       "inproj1_lat")
    ff_c, fb_c, i_c = _inproj1(h_c, mod, 1, ng1, w_in1, 1, (F32, F32, BF16), ctx_row, tm_c, "inproj1_ctx")
    o1 = _hgrn(hg_lb_logits, row2(hg_norm[0]), 1, q1, ff_l, fb_l, i_l, g1, ff_c, fb_c, i_c)
    return _mix_mlp(h_l, o1, None, None, None, hg_w_out[0].astype(BF16), mod, 1, row2(norm_mlp[1]),
                    w_mlp_in[1].astype(BF16), w_mlp_out[1].astype(BF16), row2(norm_final), None, tm_l, 1024,
                    "mix_mlp1_lat")
```

```python
import functools

import jax
import jax.numpy as jnp
from jax import lax
from jax.experimental import pallas as pl
from jax.experimental.pallas import tpu as pltpu

F32 = jnp.float32
BF16 = jnp.bfloat16

D_MODEL = 1024
DEPTH = 2
GRID_W = 64
EPS = 1e-6
ROPE_BASE = 10000.0
N_MOD = 6
RET_HEADS = 4
RET_DK = 64
RET_DV = 128
RET_QK = RET_HEADS * RET_DK
RET_WIDTH = RET_HEADS * RET_DV
RET_CHUNK = 128
S5_WIDTH = D_MODEL - RET_WIDTH
S5_GROUP = 16
S5_GROUPS = S5_WIDTH // S5_GROUP
S5_STATE = 64
S5_CHUNK = 16
S5_ROW = S5_CHUNK * S5_GROUP
LANE = 128
S5_LB = S5_WIDTH // LANE
S5_GPB = LANE // S5_GROUP
S5_BIG = S5_CHUNK * LANE
S5_HALF = S5_GPB * 2 * S5_STATE
AB_IN = 2 * RET_QK + 2 * RET_WIDTH + S5_WIDTH
HG_HEADS = 8
HG_DK = D_MODEL // HG_HEADS
HG_BLOCK = 64
HG_GROUP = 4
HG_SPLIT = 2
D_FF = 4 * D_MODEL
MOD_ROWS = 16

VMEM_LIMIT_BYTES = 56 * 1024 * 1024


def _cp(*sem):
    return pltpu.CompilerParams(dimension_semantics=sem, vmem_limit_bytes=VMEM_LIMIT_BYTES)


def _dot(a, b):
    return jnp.dot(a, b, preferred_element_type=F32)


def _dot_nt(a, b):
    return lax.dot_general(a, b, (((1,), (1,)), ((), ())), preferred_element_type=F32)


def _dot_tn(a, b):
    return lax.dot_general(a, b, (((0,), (0,)), ((), ())), preferred_element_type=F32)


def _rms(x):
    return x * lax.rsqrt(jnp.mean(x * x, axis=-1, keepdims=True) + EPS)


def _mod_chunk(mod_ref, row, i):
    return mod_ref[0, pl.ds(row, 1), i * D_MODEL:(i + 1) * D_MODEL]


def _adaln_kernel(cc_ref, w_ref, b_ref, o_ref):
    s = jax.nn.silu(cc_ref[...]).astype(BF16)
    o_ref[0] = _dot(s, w_ref[0].astype(BF16)) + b_ref[0]


def _adaln(cc, w_mod, b_mod):
    bn = 1536
    n = N_MOD * D_MODEL
    return pl.pallas_call(
        _adaln_kernel,
        out_shape=jax.ShapeDtypeStruct((DEPTH, MOD_ROWS, n), F32),
        grid=(DEPTH, n // bn),
        in_specs=[
            pl.BlockSpec((MOD_ROWS, D_MODEL), lambda l, j: (0, 0)),
            pl.BlockSpec((1, D_MODEL, bn), lambda l, j: (l, 0, j)),
            pl.BlockSpec((1, 1, bn), lambda l, j: (l, 0, j)),
        ],
        out_specs=pl.BlockSpec((1, MOD_ROWS, bn), lambda l, j: (l, 0, j)),
        compiler_params=_cp("parallel", "parallel"),
        name="adaln",
    )(cc, w_mod, b_mod.reshape(DEPTH, 1, n))


def _rope(t, cos, sin):
    lane = lax.broadcasted_iota(jnp.int32, t.shape, 1)
    first = (lane & (RET_DK // 2)) == 0
    w = t.shape[1]
    swapped = jnp.where(first, pltpu.roll(t, w - RET_DK // 2, 1), pltpu.roll(t, RET_DK // 2, 1))
    return t * cos + swapped * sin


def _inproj0_kernel(*refs, mod_row, rope):
    if rope:
        h_ref, mod_ref, ng_ref, w_ref, cos_ref, sin_ref, q_ref, k_ref, v_ref, u_ref, g_ref = refs
    else:
        h_ref, mod_ref, ng_ref, w_ref, q_ref, k_ref, v_ref, u_ref, g_ref = refs
    row = pl.program_id(0) if mod_row is None else mod_row
    xn = _rms(h_ref[0]) * ng_ref[...]
    xm = (xn * (1.0 + _mod_chunk(mod_ref, row, 1)) + _mod_chunk(mod_ref, row, 0)).astype(BF16)
    y = _dot(xm, w_ref[...])
    q = y[:, 0:RET_QK]
    k = y[:, RET_QK:2 * RET_QK]
    if rope:
        q = _rope(q, cos_ref[...], sin_ref[...])
        k = _rope(k, cos_ref[...], sin_ref[...])
    q_ref[0] = q.astype(BF16)
    k_ref[0] = (k * (RET_DK ** -0.5)).astype(BF16)
    c0 = 2 * RET_QK
    v_ref[0] = y[:, c0:c0 + RET_WIDTH].astype(BF16)
    for j in range(S5_LB):
        lo = c0 + RET_WIDTH + j * LANE
        u_ref[0, j] = y[:, lo:lo + LANE]
    g_ref[0] = y[:, c0 + RET_WIDTH + S5_WIDTH:]


def _inproj0(h, mod, layer, ng, w, cos, sin, mod_row, tm):
    b, n, _ = h.shape
    rope = cos is not None
    row_spec = lambda width: pl.BlockSpec((1, tm, width), lambda i, j: (i, j, 0))
    in_specs = [
        row_spec(D_MODEL),
        pl.BlockSpec((1, MOD_ROWS, N_MOD * D_MODEL), lambda i, j: (layer, 0, 0)),
        pl.BlockSpec((1, D_MODEL), lambda i, j: (0, 0)),
        pl.BlockSpec((D_MODEL, AB_IN), lambda i, j: (0, 0)),
    ]
    args = [h, mod, ng, w]
    if rope:
        in_specs += [pl.BlockSpec((tm, RET_QK), lambda i, j: (j, 0))] * 2
        args += [cos, sin]
    widths = (RET_QK, RET_QK, RET_WIDTH, S5_WIDTH, RET_WIDTH)
    dtypes = (BF16, BF16, BF16, F32, F32)
    out_shape = [jax.ShapeDtypeStruct((b, n, wd), dt) for wd, dt in zip(widths, dtypes)]
    out_specs = [row_spec(wd) for wd in widths]
    out_shape[3] = jax.ShapeDtypeStruct((b, S5_LB, n, LANE), F32)
    out_specs[3] = pl.BlockSpec((1, S5_LB, tm, LANE), lambda i, j: (i, 0, j, 0))
    return pl.pallas_call(
        functools.partial(_inproj0_kernel, mod_row=mod_row, rope=rope),
        out_shape=out_shape,
        grid=(b, n // tm),
        in_specs=in_specs,
        out_specs=out_specs,
        compiler_params=_cp("parallel", "parallel"),
        name="inproj0_lat" if rope else "inproj0_ctx",
    )(*args)


def _ret_kernel(lg_ref, ql, kl, vl, gl, qc, kc, vc, gc, rl, rc, sf, sb, sbl, sbc, *, ncl, ncc):
    c = RET_CHUNK
    p = pl.program_id(1)
    h_a = 2 * p
    lgf_a = lg_ref[pl.ds(h_a, 1), :]
    lgf_b = lg_ref[pl.ds(h_a + 1, 1), :]
    lgb_a = lg_ref[pl.ds(RET_HEADS + h_a, 1), :]
    lgb_b = lg_ref[pl.ds(RET_HEADS + h_a + 1, 1), :]
    lane = lax.broadcasted_iota(jnp.int32, (1, 2 * RET_DK), 1)
    is_a = lane < RET_DK
    lgf_lane = jnp.where(is_a, lgf_a, lgf_b)
    lgb_lane = jnp.where(is_a, lgb_a, lgb_b)
    ri = lax.broadcasted_iota(jnp.int32, (c, c), 0).astype(F32)
    ci = lax.broadcasted_iota(jnp.int32, (c, c), 1).astype(F32)
    diff = ri - ci

    def dmat(lgf, lgb):
        fwd = jnp.exp(jnp.maximum(diff, 0.0) * lgf)
        bwd = jnp.exp(jnp.maximum(-diff, 0.0) * lgb)
        return jnp.where(diff > 0, fwd, jnp.where(diff < 0, bwd, 2.0))

    d_a = dmat(lgf_a, lgb_a)
    d_b = dmat(lgf_b, lgb_b)
    rowp = lax.broadcasted_iota(jnp.int32, (c, 2 * RET_DK), 0).astype(F32)
    qd_f = jnp.exp((rowp + 1.0) * lgf_lane)
    qd_b = jnp.exp((c - rowp) * lgb_lane)
    kd_f = jnp.exp((c - 1.0 - rowp) * lgf_lane)
    kd_b = jnp.exp(rowp * lgb_lane)
    rowk = lax.broadcasted_iota(jnp.int32, (2 * RET_DK, 2 * RET_DV), 0)
    cd_f = jnp.exp(c * jnp.where(rowk < RET_DK, lgf_a[:, :1], lgf_b[:, :1]))
    cd_b = jnp.exp(c * jnp.where(rowk < RET_DK, lgb_a[:, :1], lgb_b[:, :1]))
    mask_a = jnp.where(is_a, 1.0, 0.0).astype(BF16)
    mask_b = jnp.where(is_a, 0.0, 1.0).astype(BF16)

    sb[...] = jnp.zeros_like(sb)

    def bwd_step(k, v, store_ref, idx):
        store_ref[idx] = sb[...]
        kd = (k.astype(F32) * kd_b).astype(BF16)
        sb[...] = cd_b * sb[...] + _dot_tn(kd, v)

    for n in reversed(range(ncc)):
        bwd_step(kc[0, n * c:(n + 1) * c, :], vc[0, n * c:(n + 1) * c, :], sbc, n)

    def bwd_body(i, carry):
        n = ncl - 1 - i
        st = pl.multiple_of(n * c, c)
        bwd_step(kl[0, pl.ds(st, c), :], vl[0, pl.ds(st, c), :], sbl, n)
        return carry

    lax.fori_loop(0, ncl, bwd_body, 0)

    sf[...] = jnp.zeros_like(sf)

    def fwd_step(q, k, v, g, sb_prev, out_ref, st):
        qf = q.astype(F32)
        q_f = (qf * qd_f).astype(BF16)
        q_b = (qf * qd_b).astype(BF16)
        s_f = sf[...].astype(BF16)
        s_b = sb_prev.astype(BF16)
        for m, dm, cs in ((mask_a, d_a, 0), (mask_b, d_b, RET_DV)):
            att = _dot_nt(q * m, k) * dm
            o = (_dot(att.astype(BF16), v[:, cs:cs + RET_DV])
                 + _dot(q_f * m, s_f[:, cs:cs + RET_DV])
                 + _dot(q_b * m, s_b[:, cs:cs + RET_DV]))
            o = _rms(o) * jax.nn.silu(g[:, cs:cs + RET_DV])
            out_ref[0, pl.ds(st, c), cs:cs + RET_DV] = o.astype(BF16)
        kd = (k.astype(F32) * kd_f).astype(BF16)
        sf[...] = cd_f * sf[...] + _dot_tn(kd, v)

    for n in range(ncc):
        sl = slice(n * c, (n + 1) * c)
        fwd_step(qc[0, sl, :], kc[0, sl, :], vc[0, sl, :], gc[0, sl, :], sbc[n], rc, n * c)

    def fwd_body(n, carry):
        st = pl.multiple_of(n * c, c)
        sl = pl.ds(st, c)
        fwd_step(ql[0, sl, :], kl[0, sl, :], vl[0, sl, :], gl[0, sl, :], sbl[n], rl, st)
        return carry

    lax.fori_loop(0, ncl, fwd_body, 0)


def _retention(lg_rows, q_l, k_l, v_l, g_l, q_c, k_c, v_c, g_c):
    b, n, _ = q_l.shape
    nc = q_c.shape[1]
    ncl, ncc = n // RET_CHUNK, nc // RET_CHUNK
    pairs = RET_HEADS // 2

    def spec(rows, width):
        return pl.BlockSpec((1, rows, width), lambda i, p: (i, 0, p))

    return pl.pallas_call(
        functools.partial(_ret_kernel, ncl=ncl, ncc=ncc),
        out_shape=[jax.ShapeDtypeStruct((b, n, RET_WIDTH), BF16),
                   jax.ShapeDtypeStruct((b, nc, RET_WIDTH), BF16)],
        grid=(b, pairs),
        in_specs=[pl.BlockSpec((2 * RET_HEADS, 2 * RET_DK), lambda i, p: (0, 0)),
                  spec(n, 2 * RET_DK), spec(n, 2 * RET_DK), spec(n, 2 * RET_DV), spec(n, 2 * RET_DV),
                  spec(nc, 2 * RET_DK), spec(nc, 2 * RET_DK), spec(nc, 2 * RET_DV), spec(nc, 2 * RET_DV)],
        out_specs=[spec(n, 2 * RET_DV), spec(nc, 2 * RET_DV)],
        scratch_shapes=[pltpu.VMEM((2 * RET_DK, 2 * RET_DV), F32),
                        pltpu.VMEM((2 * RET_DK, 2 * RET_DV), F32),
                        pltpu.VMEM((ncl, 2 * RET_DK, 2 * RET_DV), F32),
                        pltpu.VMEM((ncc, 2 * RET_DK, 2 * RET_DV), F32)],
        compiler_params=_cp("parallel", "parallel"),
        name="retention",
    )(lg_rows, q_l, k_l, v_l, g_l, q_c, k_c, v_c, g_c)


def _s5_operators(a_re, a_im, log_dt, b_re, b_im, c_re, c_im):
    t = S5_CHUNK
    hp = lax.Precision.HIGHEST
    ks = jnp.arange(t + 1, dtype=F32)
    win_re, win_im, wout_re, wout_im, a_t, toep = [], [], [], [], [], []
    for d in range(2):
        are, aim = a_re[d].astype(F32), a_im[d].astype(F32)
        dt = jnp.exp(log_dt[d].astype(F32))[:, None]
        mag = jnp.exp(are * dt)
        ang = aim * dt
        ab_re, ab_im = mag * jnp.cos(ang), mag * jnp.sin(ang)
        nr, ni = ab_re - 1.0, ab_im
        den = jnp.square(are) + jnp.square(aim)
        fr = (nr * are + ni * aim) / den
        fi = (ni * are - nr * aim) / den
        bre, bim = b_re[d].astype(F32), b_im[d].astype(F32)
        bb_re = fr[..., None] * bre - fi[..., None] * bim
        bb_im = fr[..., None] * bim + fi[..., None] * bre
        pmag = jnp.exp(ks[:, None, None] * (are * dt)[None])
        pang = ks[:, None, None] * ang[None]
        p_re, p_im = pmag * jnp.cos(pang), pmag * jnp.sin(pang)
        cre, cim = c_re[d].astype(F32), c_im[d].astype(F32)
        ca_re = cre[None] * p_re[:, :, None, :] - cim[None] * p_im[:, :, None, :]
        ca_im = cre[None] * p_im[:, :, None, :] + cim[None] * p_re[:, :, None, :]
        kk = (jnp.einsum('kgmp,gpn->kgmn', ca_re, bb_re, precision=hp)
              - jnp.einsum('kgmp,gpn->kgmn', ca_im, bb_im, precision=hp))
        s_idx = jnp.arange(t)[:, None]
        t_idx = jnp.arange(t)[None, :]
        if d == 0:
            lag, valid = t_idx - s_idx, t_idx >= s_idx
        else:
            lag, valid = s_idx - t_idx, s_idx >= t_idx
        blocks = kk[jnp.clip(lag, 0, t)]
        blocks = jnp.where(valid[:, :, None, None, None], blocks, 0.0)
        toep.append(jnp.transpose(blocks, (2, 0, 4, 1, 3)))
        pw = p_re[:t], p_im[:t]
        if d == 0:
            pw = pw[0][::-1], pw[1][::-1]
        w_re = pw[0][:, :, :, None] * bb_re[None] - pw[1][:, :, :, None] * bb_im[None]
        w_im = pw[0][:, :, :, None] * bb_im[None] + pw[1][:, :, :, None] * bb_re[None]
        win_re.append(jnp.transpose(w_re, (1, 0, 3, 2)))
        win_im.append(jnp.transpose(w_im, (1, 0, 3, 2)))
        if d == 0:
            o_re, o_im = ca_re[1:], ca_im[1:]
        else:
            o_re, o_im = ca_re[1:][::-1], ca_im[1:][::-1]
        wout_re.append(jnp.transpose(o_re, (1, 3, 0, 2)))
        wout_im.append(jnp.transpose(-o_im, (1, 3, 0, 2)))
        a_t.append((p_re[t], p_im[t]))
    g = S5_GROUPS
    m_op = (toep[0] + toep[1]).reshape(g, S5_ROW, S5_ROW)
    win = jnp.concatenate([win_re[0], win_re[1], win_im[0], win_im[1]], axis=-1).reshape(g, S5_ROW, 4 * S5_STATE)
    wout = jnp.concatenate([wout_re[0], wout_re[1], wout_im[0], wout_im[1]], axis=1).reshape(g, 4 * S5_STATE, S5_ROW)
    a_op = jnp.stack([jnp.concatenate([a_t[0][0], a_t[1][0]], -1),
                      jnp.concatenate([a_t[0][1], a_t[1][1]], -1)], axis=1)
    return m_op.astype(BF16), win.astype(BF16), wout.astype(BF16), a_op


def _s5_big_operators(ops):
    m_op, win, wout, a_op = ops
    t, g, p = S5_CHUNK, S5_GROUP, S5_STATE
    eye = jnp.eye(S5_GPB, dtype=m_op.dtype)
    m6 = m_op.reshape(S5_LB, S5_GPB, t, g, t, g)
    m_big = jnp.einsum('jqsmtn,qr->jsqmtrn', m6, eye).reshape(S5_LB, S5_BIG, S5_BIG)
    w7 = win.reshape(S5_LB, S5_GPB, t, g, 2, 2, p)
    win_big = jnp.einsum('jqsmifp,qr->jsqmifrp', w7, eye).reshape(S5_LB, S5_BIG, 2 * S5_HALF)
    o7 = wout.reshape(S5_LB, S5_GPB, 2, 2, p, t, g)
    wout_big = jnp.einsum('jqifptm,qr->jifqptrm', o7, eye).reshape(S5_LB, 2 * S5_HALF, S5_BIG)
    a_big = a_op.reshape(S5_LB, S5_GPB, 2, 2, p).transpose(0, 2, 3, 1, 4).reshape(S5_LB, 2, S5_HALF)
    return m_big, win_big, wout_big, a_big


def _s5_kernel(ul, uc, m_ref, win_ref, wout_ref, a_ref, yl, yc, x_s, s_s, *, ncl, ncc):
    hw = S5_HALF
    hh = hw // 2
    zc = uc[0, 0].astype(BF16)
    zl = ul[0, 0].astype(BF16)
    x_s[0:ncc, :] = _dot(zc, win_ref[0])
    x_s[ncc:ncc + ncl, :] = _dot(zl, win_ref[0])
    a_re = a_ref[0, 0:1, :]
    a_im = a_ref[0, 1:2, :]

    def segment(base, n, carry):
        def body(i, c):
            s_re, s_im = c
            rf = pl.ds(base + i, 1)
            rb = pl.ds(base + n - 1 - i, 1)
            s_s[rf, 0:hh] = s_re[:, 0:hh]
            s_s[rb, hh:hw] = s_re[:, hh:hw]
            s_s[rf, hw:hw + hh] = s_im[:, 0:hh]
            s_s[rb, hw + hh:2 * hw] = s_im[:, hh:hw]
            x_re = jnp.concatenate([x_s[rf, 0:hh], x_s[rb, hh:hw]], axis=-1)
            x_im = jnp.concatenate([x_s[rf, hw:hw + hh], x_s[rb, hw + hh:2 * hw]], axis=-1)
            return (a_re * s_re - a_im * s_im + x_re, a_re * s_im + a_im * s_re + x_im)

        return lax.fori_loop(0, n, body, carry)

    zero = jnp.zeros((1, hw), F32)
    carry = segment(0, ncc, (zero, zero))
    segment(ncc, ncl, carry)
    sp = s_s[...].astype(BF16)
    yc[0, 0] = _dot(zc, m_ref[0]) + _dot(sp[0:ncc], wout_ref[0])
    yl[0, 0] = _dot(zl, m_ref[0]) + _dot(sp[ncc:ncc + ncl], wout_ref[0])


def _s5(big_ops, u_l, u_c):
    m_big, win_big, wout_big, a_big = big_ops
    b, _, n, _ = u_l.shape
    nc = u_c.shape[2]
    ncl, ncc = n // S5_CHUNK, nc // S5_CHUNK
    zl = u_l.reshape(b, S5_LB, ncl, S5_BIG)
    zc = u_c.reshape(b, S5_LB, ncc, S5_BIG)
    rows = lambda r: pl.BlockSpec((1, 1, r, S5_BIG), lambda j, i: (i, j, 0, 0))
    wspec = lambda r, c: pl.BlockSpec((1, r, c), lambda j, i: (j, 0, 0), pipeline_mode=pl.Buffered(1))
    yl, yc = pl.pallas_call(
        functools.partial(_s5_kernel, ncl=ncl, ncc=ncc),
        out_shape=[jax.ShapeDtypeStruct(zl.shape, F32), jax.ShapeDtypeStruct(zc.shape, F32)],
        grid=(S5_LB, b),
        in_specs=[rows(ncl), rows(ncc), wspec(S5_BIG, S5_BIG), wspec(S5_BIG, 2 * S5_HALF),
                  wspec(2 * S5_HALF, S5_BIG), wspec(2, S5_HALF)],
        out_specs=[rows(ncl), rows(ncc)],
        scratch_shapes=[pltpu.VMEM((ncc + ncl, 2 * S5_HALF), F32), pltpu.VMEM((ncc + ncl, 2 * S5_HALF), F32)],
        compiler_params=_cp("parallel", "arbitrary"),
        name="s5",
    )(zl, zc, m_big, win_big, wout_big, a_big)
    return yl.reshape(u_l.shape), yc.reshape(u_c.shape)


def _mix_mlp_kernel(*refs, mod_row, s5_merge, final_norm, fb):
    if s5_merge:
        (h_ref, r_ref, y5_ref, u_ref, ds_ref, wg_ref, bg_ref, wo_ref, mod_ref, nm_ref, w1_ref, w2_ref,
         *rest) = refs
    else:
        h_ref, r_ref, wo_ref, mod_ref, nm_ref, w1_ref, w2_ref, *rest = refs
    if final_norm:
        nf_ref, o_ref = rest
    else:
        (o_ref,) = rest
    row = pl.program_id(0) if mod_row is None else mod_row
    if s5_merge:
        y5 = jnp.concatenate([y5_ref[0, lb] for lb in range(S5_LB)], axis=-1)
        u = jnp.concatenate([u_ref[0, lb] for lb in range(S5_LB)], axis=-1)
        y = jax.nn.gelu(y5 + ds_ref[...] * u)
        y = y * jax.nn.sigmoid(_dot(y.astype(BF16), wg_ref[...]) + bg_ref[...])
        mix = _dot(r_ref[0], wo_ref[0:RET_WIDTH, :]) + _dot(y.astype(BF16), wo_ref[RET_WIDTH:D_MODEL, :])
    else:
        mix = _dot(r_ref[0], wo_ref[...])
    h1 = h_ref[0] + _mod_chunk(mod_ref, row, 2) * mix
    xn = _rms(h1) * nm_ref[...]
    xm = (xn * (1.0 + _mod_chunk(mod_ref, row, 4)) + _mod_chunk(mod_ref, row, 3)).astype(BF16)
    acc = None
    for j in range(D_FF // fb):
        a = jnp.square(jnp.maximum(_dot(xm, w1_ref[:, j * fb:(j + 1) * fb]), 0.0)).astype(BF16)
        part = _dot(a, w2_ref[j * fb:(j + 1) * fb, :])
        acc = part if acc is None else acc + part
    h2 = h1 + _mod_chunk(mod_ref, row, 5) * acc
    if final_norm:
        h2 = _rms(h2) * nf_ref[...]
    o_ref[0] = h2


def _mix_mlp(h, r, s5y, u, s5p, wo, mod, layer, nm, w1, w2, nf, mod_row, tm, fb, name):
    b, n, _ = h.shape
    s5_merge = s5y is not None
    final_norm = nf is not None
    one = pl.Buffered(1)
    row_spec = lambda width: pl.BlockSpec((1, tm, width), lambda i, t: (i, t, 0))
    const = lambda shape: pl.BlockSpec(shape, lambda i, t: (0,) * len(shape), pipeline_mode=one)
    in_specs = [row_spec(D_MODEL), row_spec(r.shape[-1])]
    args = [h, r]
    if s5_merge:
        d_skip, w_glu, b_glu = s5p
        lb_spec = pl.BlockSpec((1, S5_LB, tm, LANE), lambda i, t: (i, 0, t, 0))
        in_specs += [lb_spec, lb_spec, const((1, S5_WIDTH)),
                     const((S5_WIDTH, S5_WIDTH)), const((1, S5_WIDTH))]
        args += [s5y, u, d_skip, w_glu, b_glu]
    in_specs += [const((D_MODEL, D_MODEL)),
                 pl.BlockSpec((1, MOD_ROWS, N_MOD * D_MODEL), lambda i, t: (layer, 0, 0), pipeline_mode=one),
                 const((1, D_MODEL)), const((D_MODEL, D_FF)), const((D_FF, D_MODEL))]
    args += [wo, mod, nm, w1, w2]
    if final_norm:
        in_specs.append(const((1, D_MODEL)))
        args.append(nf)
    return pl.pallas_call(
        functools.partial(_mix_mlp_kernel, mod_row=mod_row, s5_merge=s5_merge, final_norm=final_norm, fb=fb),
        out_shape=jax.ShapeDtypeStruct((b, n, D_MODEL), F32),
        grid=(b, n // tm),
        in_specs=in_specs,
        out_specs=row_spec(D_MODEL),
        compiler_params=_cp("parallel", "parallel"),
        name=name,
    )(*args)


def _inproj1_kernel(h_ref, mod_ref, ng_ref, w_ref, *out_refs, mod_row, col0):
    row = pl.program_id(0) if mod_row is None else mod_row
    xn = _rms(h_ref[0]) * ng_ref[...]
    xm = (xn * (1.0 + _mod_chunk(mod_ref, row, 1)) + _mod_chunk(mod_ref, row, 0)).astype(BF16)
    for idx, o_ref in enumerate(out_refs):
        lo = (col0 + idx) * D_MODEL
        o_ref[0] = _dot(xm, w_ref[:, lo:lo + D_MODEL]).astype(o_ref.dtype)


def _inproj1(h, mod, layer, ng, w, col0, dtypes, mod_row, tm, name):
    b, n, _ = h.shape
    one = pl.Buffered(1)
    row_spec = pl.BlockSpec((1, tm, D_MODEL), lambda i, t: (i, t, 0))
    return pl.pallas_call(
        functools.partial(_inproj1_kernel, mod_row=mod_row, col0=col0),
        out_shape=[jax.ShapeDtypeStruct((b, n, D_MODEL), dt) for dt in dtypes],
        grid=(b, n // tm),
        in_specs=[row_spec,
                  pl.BlockSpec((1, MOD_ROWS, N_MOD * D_MODEL), lambda i, t: (layer, 0, 0), pipeline_mode=one),
                  pl.BlockSpec((1, D_MODEL), lambda i, t: (0, 0), pipeline_mode=one),
                  pl.BlockSpec(w.shape, lambda i, t: (0, 0), pipeline_mode=one)],
        out_specs=[row_spec] * len(dtypes),
        compiler_params=_cp("parallel", "parallel"),
        name=name,
    )(h, mod, ng, w)


def _cumsum_mm(tri, x):
    acc = None
    r = x
    for i in range(HG_SPLIT):
        p = r.astype(BF16)
        acc = _dot(tri, p) if acc is None else acc + _dot(tri, p)
        if i + 1 < HG_SPLIT:
            r = r - p.astype(F32)
    return acc


def _hgrn_kernel(lbl_ref, ng_ref, ql, ffl, fbl, il, gl, ffc, fbc, ic, o_ref,
                 qin_s, att_s, kv_s, et_s, kvc_s, etc_s, cum_s, kk_s, *, nbl, nbc, layer, unroll, out_blocks):
    cb = HG_BLOCK
    mid = cb // 2
    gb = HG_GROUP
    gr = gb * cb
    dk = HG_DK

    def lower_bound(d):
        z = [lbl_ref[d, k:k + 1, :] for k in range(DEPTH)]
        zmax = functools.reduce(jnp.maximum, z)
        e = [jnp.exp(v - zmax) for v in z]
        tot = functools.reduce(lambda a, b_: a + b_, e)
        lb = jnp.zeros_like(tot)
        for k in range(1, layer + 1):
            lb = lb + e[k] / tot
        return lb

    ri = lax.broadcasted_iota(jnp.int32, (gr, gr), 0)
    ci = lax.broadcasted_iota(jnp.int32, (gr, gr), 1)
    same = (ri // cb) == (ci // cb)
    rb = lax.broadcasted_iota(jnp.int32, (cb, cb), 0)
    cbi = lax.broadcasted_iota(jnp.int32, (cb, cb), 1)
    tri_l = jnp.where(same & (ri >= ci), 1.0, 0.0).astype(BF16)
    dirs = ((0, lower_bound(0), rb >= cbi, mid - 1, cb - 1), (1, lower_bound(1), rb <= cbi, mid, 0))

    def gates(raws, slot):
        s = [jax.nn.sigmoid(r) for r in raws]
        t = [(1.0 - dr[1]) * si for dr, si in zip(dirs, s)]
        log_f = [jnp.log(dr[1] + ti) for dr, ti in zip(dirs, t)]
        pre = _cumsum_mm(tri_l, jnp.concatenate(log_f, axis=-1))
        pre_b = pre[:, dk:].reshape(gb, cb, dk)
        cum_s[slot, 0] = pre[:, :dk]
        cum_s[slot, 1] = (pre_b[:, cb - 1:cb, :] - pre_b).reshape(gr, dk) + log_f[1]
        for d in range(2):
            kk_s[slot, d] = (1.0 - dirs[d][1]) - t[d]

    def products(slot, v, q, n0, kv_ref, et_ref):
        v3 = v.reshape(gb, cb, dk)
        qts, kts = [], []
        for d, lb, keep, ref_row, tot_row in dirs:
            cum = cum_s[slot, d].reshape(gb, cb, dk)
            kk = kk_s[slot, d].reshape(gb, cb, dk)
            ref = cum[:, ref_row:ref_row + 1, :]
            tot = cum[:, tot_row:tot_row + 1, :]
            e = cum - ref
            kt = kk * jnp.exp(-e)
            k_out = (kt * jnp.exp(tot - ref)).astype(BF16)
            e_tot = jnp.exp(tot)
            if q is not None:
                qt = q.astype(F32).reshape(gb, cb, dk) * jnp.exp(e)
                qin_s[pl.ds(pl.multiple_of(n0 * cb, gr), gr), d * dk:(d + 1) * dk] = (
                    (qt * jnp.exp(ref)).astype(BF16).reshape(gr, dk))
                qts.append(qt.astype(BF16))
                kts.append(kt.astype(BF16))
            for j in range(gb):
                kv_ref[n0 + j, :, d * dk:(d + 1) * dk] = _dot_tn(v3[j], k_out[j])
                et_ref[n0 + j, :, d * dk:(d + 1) * dk] = e_tot[j]
        if q is not None:
            for j in range(gb):
                att = (jnp.where(dirs[0][2], _dot_nt(qts[0][j], kts[0][j]), 0.0)
                       + jnp.where(dirs[1][2], _dot_nt(qts[1][j], kts[1][j]), 0.0))
                att_s[n0 + j] = att.astype(BF16)

    for g in range(nbc // gb):
        sl = slice(g * gr, (g + 1) * gr)
        gates((ffc[0, sl, :], fbc[0, sl, :]), 0)
        products(0, ic[0, sl, :], None, g * gb, kvc_s, etc_s)

    ngl = nbl // gb

    def lat_gates(g, slot):
        sl = pl.ds(pl.multiple_of(g * gr, gr), gr)
        gates((ffl[0, sl, :], fbl[0, sl, :]), slot)

    def lat_products(g, slot):
        sl = pl.ds(pl.multiple_of(g * gr, gr), gr)
        products(slot, il[0, sl, :], ql[0, sl, :], g * gb, kv_s, et_s)

    lat_gates(0, 0)

    def prep_body(g, carry):
        slot = g % 2
        lat_products(g, slot)
        lat_gates(g + 1, 1 - slot)
        return carry

    lax.fori_loop(0, ngl - 1, prep_body, 0)
    lat_products(ngl - 1, (ngl - 1) % 2)

    lane = lax.broadcasted_iota(jnp.int32, (dk, 2 * dk), 1)
    is_f = lane < dk
    st = jnp.zeros((dk, 2 * dk), F32)
    for n in range(nbc):
        m = nbc - 1 - n
        st = (st * jnp.where(is_f[:1], etc_s[n], etc_s[m]) + jnp.where(is_f, kvc_s[n], kvc_s[m]))

    def rec_body(t, st):
        u = nbl - 1 - t
        inc = jnp.where(is_f, kv_s[t], kv_s[u])
        dec = jnp.where(is_f[:1], et_s[t], et_s[u])
        kv_s[t, :, 0:dk] = st[:, 0:dk]
        kv_s[u, :, dk:2 * dk] = st[:, dk:2 * dk]
        return st * dec + inc

    lax.fori_loop(0, nbl, rec_body, st, unroll=unroll)

    def out_body(i, carry):
        for j in range(out_blocks):
            n = i * out_blocks + j
            sl = pl.ds(pl.multiple_of(n * cb, cb), cb)
            o = _dot(att_s[n], il[0, sl, :]) + _dot_nt(qin_s[sl, :], kv_s[n].astype(BF16))
            o = _rms(o) * ng_ref[...] * jax.nn.silu(gl[0, sl, :])
            o_ref[0, sl, :] = o.astype(BF16)
        return carry

    lax.fori_loop(0, nbl // out_blocks, out_body, 0)


def _hgrn(lb_logits, norm_g, layer, q_l, ff_l, fb_l, i_l, g_l, ff_c, fb_c, i_c):
    b, n, _ = q_l.shape
    nc = ff_c.shape[1]
    nbl, nbc = n // HG_BLOCK, nc // HG_BLOCK
    out_blocks = min(16, nbl)
    assert nbl % HG_GROUP == 0 and nbc % HG_GROUP == 0 and nbl % out_blocks == 0
    spec = lambda rows: pl.BlockSpec((1, rows, HG_DK), lambda i, h: (i, 0, h))
    return pl.pallas_call(
        functools.partial(_hgrn_kernel, nbl=nbl, nbc=nbc, layer=layer, unroll=2, out_blocks=out_blocks),
        out_shape=jax.ShapeDtypeStruct((b, n, D_MODEL), BF16),
        grid=(b, HG_HEADS),
        in_specs=[pl.BlockSpec((2, DEPTH, HG_DK), lambda i, h: (0, 0, h)),
                  pl.BlockSpec((1, HG_DK), lambda i, h: (0, 0)),
                  spec(n), spec(n), spec(n), spec(n), spec(n), spec(nc), spec(nc), spec(nc)],
        out_specs=spec(n),
        scratch_shapes=[pltpu.VMEM((n, 2 * HG_DK), BF16),
                        pltpu.VMEM((nbl, HG_BLOCK, HG_BLOCK), BF16),
                        pltpu.VMEM((nbl, HG_DK, 2 * HG_DK), F32),
                        pltpu.VMEM((nbl, 1, 2 * HG_DK), F32),
                        pltpu.VMEM((nbc, HG_DK, 2 * HG_DK), F32),
                        pltpu.VMEM((nbc, 1, 2 * HG_DK), F32),
                        pltpu.VMEM((2, 2, HG_GROUP * HG_BLOCK, HG_DK), F32),
                        pltpu.VMEM((2, 2, HG_GROUP * HG_BLOCK, HG_DK), F32)],
        compiler_params=_cp("parallel", "parallel"),
        name="hgrn2",
    )(lb_logits, norm_g, q_l, ff_l, fb_l, i_l, g_l, ff_c, fb_c, i_c)


def _rope_tables(n_tok):
    rows = n_tok // GRID_W
    row = jnp.broadcast_to(jnp.arange(rows, dtype=F32)[:, None], (rows, GRID_W)).reshape(-1)
    col = jnp.broadcast_to(jnp.arange(GRID_W, dtype=F32)[None, :], (rows, GRID_W)).reshape(-1)
    n_freq = RET_DK // 4
    inv = ROPE_BASE ** (-jnp.arange(n_freq, dtype=F32) / n_freq)
    ang = jnp.concatenate([row[:, None] * inv, col[:, None] * inv], axis=-1)
    cos, sin = jnp.cos(ang), jnp.sin(ang)
    cos_full = jnp.tile(jnp.concatenate([cos, cos], axis=-1), (1, RET_HEADS))
    sin_signed = jnp.tile(jnp.concatenate([-sin, sin], axis=-1), (1, RET_HEADS))
    return cos_full, sin_signed


def kernel(x, c, ctx, c_ctx, w_mod, b_mod, norm_mix, norm_mlp, w_mlp_in, w_mlp_out, ab_w_in, ab_w_out, ret_logit, s5_a_re, s5_a_im, s5_log_dt, s5_b_re, s5_b_im, s5_c_re, s5_c_im, s5_d, s5_w_glu, s5_b_glu, hg_w_in, hg_w_out, hg_lb_logits, hg_norm, norm_final):
    b, n, d = x.shape
    nc = ctx.shape[1]
    assert d == D_MODEL and b + 1 <= MOD_ROWS and w_mod.shape[0] == DEPTH == 2
    assert n % 512 == 0 and nc % 256 == 0 and n % GRID_W == 0
    ctx_row = b
    tm_l, tm_c = 512, 256

    cc = jnp.zeros((MOD_ROWS, d), F32).at[:b].set(c).at[b].set(c_ctx)
    mod = _adaln(cc, w_mod, b_mod)

    row2 = lambda a: a.reshape(1, -1)
    w_in0 = ab_w_in[0].astype(BF16)
    cos, sin = _rope_tables(n)
    ng0 = row2(norm_mix[0])
    q_l, k_l, v_l, u_l, g_l = _inproj0(x, mod, 0, ng0, w_in0, cos, sin, None, tm_l)
    q_c, k_c, v_c, u_c, g_c = _inproj0(ctx, mod, 0, ng0, w_in0, None, None, ctx_row, tm_c)

    log_gamma = jax.nn.log_sigmoid(ret_logit[0].astype(F32))
    lg_rows = jnp.broadcast_to(log_gamma.reshape(2 * RET_HEADS, 1), (2 * RET_HEADS, 2 * RET_DK))
    r_l, r_c = _retention(lg_rows, q_l, k_l, v_l, g_l, q_c, k_c, v_c, g_c)

    s5_ops = _s5_operators(s5_a_re[0], s5_a_im[0], s5_log_dt[0], s5_b_re[0], s5_b_im[0], s5_c_re[0], s5_c_im[0])
    y5_l, y5_c = _s5(_s5_big_operators(s5_ops), u_l, u_c)

    s5p = (row2(s5_d[0]), s5_w_glu[0].astype(BF16), row2(s5_b_glu[0]))
    wo0 = ab_w_out[0].astype(BF16)
    w1_0, w2_0 = w_mlp_in[0].astype(BF16), w_mlp_out[0].astype(BF16)
    nm0 = row2(norm_mlp[0])
    h_l = _mix_mlp(x, r_l, y5_l, u_l, s5p, wo0, mod, 0, nm0, w1_0, w2_0, None, None, tm_l, 1024, "mix_mlp0_lat")
    h_c = _mix_mlp(ctx, r_c, y5_c, u_c, s5p, wo0, mod, 0, nm0, w1_0, w2_0, None, ctx_row, tm_c, 1024, "mix_mlp0_ctx")

    w_in1 = hg_w_in[0].astype(BF16)
    ng1 = row2(norm_mix[1])
    q1, ff_l, fb_l, i_l, g1 = _inproj1(h_l, mod, 1, ng1, w_in1, 0, (BF16, F32, F32, BF16, F32), None, tm_l,
                                       "inproj1_lat")
    ff_c, fb_c, i_c = _inproj1(h_c, mod, 1, ng1, w_in1, 1, (F32, F32, BF16), ctx_row, tm_c, "inproj1_ctx")
    o1 = _hgrn(hg_lb_logits, row2(hg_norm[0]), 1, q1, ff_l, fb_l, i_l, g1, ff_c, fb_c, i_c)
    return _mix_mlp(h_l, o1, None, None, None, hg_w_out[0].astype(BF16), mod, 1, row2(norm_mlp[1]),
                    w_mlp_in[1].astype(BF16), w_mlp_out[1].astype(BF16), row2(norm_final), None, tm_l, 1024,
                    "mix_mlp1_lat")
```

```python
import functools

import jax
import jax.numpy as jnp
from jax import lax
from jax.experimental import pallas as pl
from jax.experimental.pallas import tpu as pltpu

F32 = jnp.float32
BF16 = jnp.bfloat16

D_MODEL = 1024
DEPTH = 2
GRID_W = 64
EPS = 1e-6
ROPE_BASE = 10000.0
N_MOD = 6
RET_HEADS = 4
RET_DK = 64
RET_DV = 128
RET_QK = RET_HEADS * RET_DK
RET_WIDTH = RET_HEADS * RET_DV
RET_CHUNK = 128
S5_WIDTH = D_MODEL - RET_WIDTH
S5_GROUP = 16
S5_GROUPS = S5_WIDTH // S5_GROUP
S5_STATE = 64
S5_CHUNK = 16
S5_ROW = S5_CHUNK * S5_GROUP
LANE = 128
S5_LB = S5_WIDTH // LANE
S5_GPB = LANE // S5_GROUP
S5_BIG = S5_CHUNK * LANE
S5_HALF = S5_GPB * 2 * S5_STATE
S5_POW = 32
AB_IN = 2 * RET_QK + 2 * RET_WIDTH + S5_WIDTH
HG_HEADS = 8
HG_DK = D_MODEL // HG_HEADS
HG_BLOCK = 64
HG_GROUP = 4
HG_SPLIT = 2
D_FF = 4 * D_MODEL
MOD_ROWS = 16

VMEM_LIMIT_BYTES = 56 * 1024 * 1024


def _cp(*sem):
    return pltpu.CompilerParams(dimension_semantics=sem, vmem_limit_bytes=VMEM_LIMIT_BYTES)


def _dot(a, b):
    return jnp.dot(a, b, preferred_element_type=F32)


def _dot_nt(a, b):
    return lax.dot_general(a, b, (((1,), (1,)), ((), ())), preferred_element_type=F32)


def _dot_tn(a, b):
    return lax.dot_general(a, b, (((0,), (0,)), ((), ())), preferred_element_type=F32)


def _rms(x):
    return x * lax.rsqrt(jnp.mean(x * x, axis=-1, keepdims=True) + EPS)


def _mod_chunk(mod_ref, row, i):
    return mod_ref[0, pl.ds(row, 1), i * D_MODEL:(i + 1) * D_MODEL]


def _adaln_kernel(cc_ref, w_ref, b_ref, o_ref):
    s = jax.nn.silu(cc_ref[...]).astype(BF16)
    o_ref[0] = _dot(s, w_ref[0].astype(BF16)) + b_ref[0]


def _adaln(cc, w_mod, b_mod):
    bn = 1536
    n = N_MOD * D_MODEL
    return pl.pallas_call(
        _adaln_kernel,
        out_shape=jax.ShapeDtypeStruct((DEPTH, MOD_ROWS, n), F32),
        grid=(DEPTH, n // bn),
        in_specs=[
            pl.BlockSpec((MOD_ROWS, D_MODEL), lambda l, j: (0, 0)),
            pl.BlockSpec((1, D_MODEL, bn), lambda l, j: (l, 0, j)),
            pl.BlockSpec((1, 1, bn), lambda l, j: (l, 0, j)),
        ],
        out_specs=pl.BlockSpec((1, MOD_ROWS, bn), lambda l, j: (l, 0, j)),
        compiler_params=_cp("parallel", "parallel"),
        name="adaln",
    )(cc, w_mod, b_mod.reshape(DEPTH, 1, n))


def _rope(t, cos, sin):
    lane = lax.broadcasted_iota(jnp.int32, t.shape, 1)
    first = (lane & (RET_DK // 2)) == 0
    w = t.shape[1]
    swapped = jnp.where(first, pltpu.roll(t, w - RET_DK // 2, 1), pltpu.roll(t, RET_DK // 2, 1))
    return t * cos + swapped * sin


def _inproj0_kernel(*refs, mod_row, rope):
    if rope:
        h_ref, mod_ref, ng_ref, w_ref, cos_ref, sin_ref, q_ref, k_ref, v_ref, u_ref, g_ref = refs
    else:
        h_ref, mod_ref, ng_ref, w_ref, q_ref, k_ref, v_ref, u_ref, g_ref = refs
    row = pl.program_id(0) if mod_row is None else mod_row
    xn = _rms(h_ref[0]) * ng_ref[...]
    xm = (xn * (1.0 + _mod_chunk(mod_ref, row, 1)) + _mod_chunk(mod_ref, row, 0)).astype(BF16)
    y = _dot(xm, w_ref[...])
    q = y[:, 0:RET_QK]
    k = y[:, RET_QK:2 * RET_QK]
    if rope:
        q = _rope(q, cos_ref[...], sin_ref[...])
        k = _rope(k, cos_ref[...], sin_ref[...])
    q_ref[0] = q.astype(BF16)
    k_ref[0] = (k * (RET_DK ** -0.5)).astype(BF16)
    c0 = 2 * RET_QK
    v_ref[0] = y[:, c0:c0 + RET_WIDTH].astype(BF16)
    for j in range(S5_LB):
        lo = c0 + RET_WIDTH + j * LANE
        u_ref[0, j] = y[:, lo:lo + LANE]
    g_ref[0] = y[:, c0 + RET_WIDTH + S5_WIDTH:]


def _inproj0(h, mod, layer, ng, w, cos, sin, mod_row, tm):
    b, n, _ = h.shape
    rope = cos is not None
    row_spec = lambda width: pl.BlockSpec((1, tm, width), lambda i, j: (i, j, 0))
    in_specs = [
        row_spec(D_MODEL),
        pl.BlockSpec((1, MOD_ROWS, N_MOD * D_MODEL), lambda i, j: (layer, 0, 0)),
        pl.BlockSpec((1, D_MODEL), lambda i, j: (0, 0)),
        pl.BlockSpec((D_MODEL, AB_IN), lambda i, j: (0, 0)),
    ]
    args = [h, mod, ng, w]
    if rope:
        in_specs += [pl.BlockSpec((tm, RET_QK), lambda i, j: (j, 0))] * 2
        args += [cos, sin]
    widths = (RET_QK, RET_QK, RET_WIDTH, S5_WIDTH, RET_WIDTH)
    dtypes = (BF16, BF16, BF16, F32, F32)
    out_shape = [jax.ShapeDtypeStruct((b, n, wd), dt) for wd, dt in zip(widths, dtypes)]
    out_specs = [row_spec(wd) for wd in widths]
    out_shape[3] = jax.ShapeDtypeStruct((b, S5_LB, n, LANE), F32)
    out_specs[3] = pl.BlockSpec((1, S5_LB, tm, LANE), lambda i, j: (i, 0, j, 0))
    return pl.pallas_call(
        functools.partial(_inproj0_kernel, mod_row=mod_row, rope=rope),
        out_shape=out_shape,
        grid=(b, n // tm),
        in_specs=in_specs,
        out_specs=out_specs,
        compiler_params=_cp("parallel", "parallel"),
        name="inproj0_lat" if rope else "inproj0_ctx",
    )(*args)


def _ret_kernel(lg_ref, ql, kl, vl, gl, qc, kc, vc, gc, rl, rc, sf, sb, sbl, sbc, *, ncl, ncc):
    c = RET_CHUNK
    p = pl.program_id(1)
    h_a = 2 * p
    lgf_a = lg_ref[pl.ds(h_a, 1), :]
    lgf_b = lg_ref[pl.ds(h_a + 1, 1), :]
    lgb_a = lg_ref[pl.ds(RET_HEADS + h_a, 1), :]
    lgb_b = lg_ref[pl.ds(RET_HEADS + h_a + 1, 1), :]
    lane = lax.broadcasted_iota(jnp.int32, (1, 2 * RET_DK), 1)
    is_a = lane < RET_DK
    lgf_lane = jnp.where(is_a, lgf_a, lgf_b)
    lgb_lane = jnp.where(is_a, lgb_a, lgb_b)
    ri = lax.broadcasted_iota(jnp.int32, (c, c), 0).astype(F32)
    ci = lax.broadcasted_iota(jnp.int32, (c, c), 1).astype(F32)
    diff = ri - ci

    def dmat(lgf, lgb):
        fwd = jnp.exp(jnp.maximum(diff, 0.0) * lgf)
        bwd = jnp.exp(jnp.maximum(-diff, 0.0) * lgb)
        return jnp.where(diff > 0, fwd, jnp.where(diff < 0, bwd, 2.0))

    d_a = dmat(lgf_a, lgb_a)
    d_b = dmat(lgf_b, lgb_b)
    rowp = lax.broadcasted_iota(jnp.int32, (c, 2 * RET_DK), 0).astype(F32)
    qd_f = jnp.exp((rowp + 1.0) * lgf_lane)
    qd_b = jnp.exp((c - rowp) * lgb_lane)
    kd_f = jnp.exp((c - 1.0 - rowp) * lgf_lane)
    kd_b = jnp.exp(rowp * lgb_lane)
    rowk = lax.broadcasted_iota(jnp.int32, (2 * RET_DK, 2 * RET_DV), 0)
    cd_f = jnp.exp(c * jnp.where(rowk < RET_DK, lgf_a[:, :1], lgf_b[:, :1]))
    cd_b = jnp.exp(c * jnp.where(rowk < RET_DK, lgb_a[:, :1], lgb_b[:, :1]))
    mask_a = jnp.where(is_a, 1.0, 0.0).astype(BF16)
    mask_b = jnp.where(is_a, 0.0, 1.0).astype(BF16)

    sb[...] = jnp.zeros_like(sb)

    def bwd_step(k, v, store_ref, idx):
        store_ref[idx] = sb[...]
        kd = (k.astype(F32) * kd_b).astype(BF16)
        sb[...] = cd_b * sb[...] + _dot_tn(kd, v)

    for n in reversed(range(ncc)):
        bwd_step(kc[0, n * c:(n + 1) * c, :], vc[0, n * c:(n + 1) * c, :], sbc, n)

    def bwd_body(i, carry):
        n = ncl - 1 - i
        st = pl.multiple_of(n * c, c)
        bwd_step(kl[0, pl.ds(st, c), :], vl[0, pl.ds(st, c), :], sbl, n)
        return carry

    lax.fori_loop(0, ncl, bwd_body, 0)

    sf[...] = jnp.zeros_like(sf)

    def fwd_step(q, k, v, g, sb_prev, out_ref, st):
        qf = q.astype(F32)
        q_f = (qf * qd_f).astype(BF16)
        q_b = (qf * qd_b).astype(BF16)
        s_f = sf[...].astype(BF16)
        s_b = sb_prev.astype(BF16)
        for m, dm, cs in ((mask_a, d_a, 0), (mask_b, d_b, RET_DV)):
            att = _dot_nt(q * m, k) * dm
            o = (_dot(att.astype(BF16), v[:, cs:cs + RET_DV])
                 + _dot(q_f * m, s_f[:, cs:cs + RET_DV])
                 + _dot(q_b * m, s_b[:, cs:cs + RET_DV]))
            o = _rms(o) * jax.nn.silu(g[:, cs:cs + RET_DV])
            out_ref[0, pl.ds(st, c), cs:cs + RET_DV] = o.astype(BF16)
        kd = (k.astype(F32) * kd_f).astype(BF16)
        sf[...] = cd_f * sf[...] + _dot_tn(kd, v)

    for n in range(ncc):
        sl = slice(n * c, (n + 1) * c)
        fwd_step(qc[0, sl, :], kc[0, sl, :], vc[0, sl, :], gc[0, sl, :], sbc[n], rc, n * c)

    def fwd_body(n, carry):
        st = pl.multiple_of(n * c, c)
        sl = pl.ds(st, c)
        fwd_step(ql[0, sl, :], kl[0, sl, :], vl[0, sl, :], gl[0, sl, :], sbl[n], rl, st)
        return carry

    lax.fori_loop(0, ncl, fwd_body, 0)


def _retention(lg_rows, q_l, k_l, v_l, g_l, q_c, k_c, v_c, g_c):
    b, n, _ = q_l.shape
    nc = q_c.shape[1]
    ncl, ncc = n // RET_CHUNK, nc // RET_CHUNK
    pairs = RET_HEADS // 2

    def spec(rows, width):
        return pl.BlockSpec((1, rows, width), lambda i, p: (i, 0, p))

    return pl.pallas_call(
        functools.partial(_ret_kernel, ncl=ncl, ncc=ncc),
        out_shape=[jax.ShapeDtypeStruct((b, n, RET_WIDTH), BF16),
                   jax.ShapeDtypeStruct((b, nc, RET_WIDTH), BF16)],
        grid=(b, pairs),
        in_specs=[pl.BlockSpec((2 * RET_HEADS, 2 * RET_DK), lambda i, p: (0, 0)),
                  spec(n, 2 * RET_DK), spec(n, 2 * RET_DK), spec(n, 2 * RET_DV), spec(n, 2 * RET_DV),
                  spec(nc, 2 * RET_DK), spec(nc, 2 * RET_DK), spec(nc, 2 * RET_DV), spec(nc, 2 * RET_DV)],
        out_specs=[spec(n, 2 * RET_DV), spec(nc, 2 * RET_DV)],
        scratch_shapes=[pltpu.VMEM((2 * RET_DK, 2 * RET_DV), F32),
                        pltpu.VMEM((2 * RET_DK, 2 * RET_DV), F32),
                        pltpu.VMEM((ncl, 2 * RET_DK, 2 * RET_DV), F32),
                        pltpu.VMEM((ncc, 2 * RET_DK, 2 * RET_DV), F32)],
        compiler_params=_cp("parallel", "parallel"),
        name="retention",
    )(lg_rows, q_l, k_l, v_l, g_l, q_c, k_c, v_c, g_c)


def _dot_hi(a, b):
    return jnp.dot(a, b, preferred_element_type=F32, precision=lax.Precision.HIGHEST)


def _s5_prep_kernel(ar_row, ai_row, ar_col, ai_col, ldt, btr, bti, ctr, cti, rm, rwin, rwout, abig):
    t, g, p, kp = S5_CHUNK, S5_GROUP, S5_STATE, S5_POW
    row = S5_ROW
    i0 = lambda shape: lax.broadcasted_iota(jnp.int32, shape, 0)
    i1 = lambda shape: lax.broadcasted_iota(jnp.int32, shape, 1)
    f32 = lambda m: jnp.where(m, 1.0, 0.0).astype(F32)
    s_of_r = i0((row, kp)) // g
    k_of_l = i1((row, kp))
    sel_rows = (f32(k_of_l == t - 1 - s_of_r), f32(k_of_l == s_of_r))
    t_of_c = i1((kp, row)) // g
    k_of_s = i0((kp, row))
    sel_out = (f32(k_of_s == t_of_c + 1), f32(k_of_s == t - t_of_c))
    sel_lag = (f32(k_of_s == t_of_c), f32(k_of_s == t - 1 - t_of_c))
    tile_l = f32(i1((g, row)) % g == i0((g, row)))
    tile_r = f32(i0((row, g)) % g == i1((row, g)))
    lane = i1((g, row))
    k_col = i0((kp, 1)).astype(F32)
    k_row = i1((1, kp)).astype(F32)

    for q in range(S5_GPB):
        lags = []
        for d in range(2):
            dt = jnp.exp(ldt[d, q])
            are_r, aim_r = ar_row[d, q], ai_row[d, q]
            are_c, aim_c = ar_col[d, q], ai_col[d, q]
            mag = jnp.exp(are_r * dt)
            ang = aim_r * dt
            nr, ni = mag * jnp.cos(ang) - 1.0, mag * jnp.sin(ang)
            den = jnp.square(are_r) + jnp.square(aim_r)
            fr = (nr * are_r + ni * aim_r) / den
            fi = (ni * are_r - nr * aim_r) / den
            pm = jnp.exp(k_col * (are_r * dt))
            pa = k_col * ang
            pk_re, pk_im = pm * jnp.cos(pa), pm * jnp.sin(pa)
            pmt = jnp.exp(k_row * (are_c * dt))
            pat = k_row * (aim_c * dt)
            pt_re, pt_im = pmt * jnp.cos(pat), pmt * jnp.sin(pat)
            bt_re, bt_im = _dot_hi(tile_r, btr[d, q]), _dot_hi(tile_r, bti[d, q])
            bb_re = fr * bt_re - fi * bt_im
            bb_im = fr * bt_im + fi * bt_re
            pr_re, pr_im = _dot_hi(sel_rows[d], pk_re), _dot_hi(sel_rows[d], pk_im)
            w_re = pr_re * bb_re - pr_im * bb_im
            w_im = pr_re * bb_im + pr_im * bb_re
            for s in range(t):
                rows = slice(s * LANE + q * g, s * LANE + (q + 1) * g)
                rwin[0, rows, d * p:(d + 1) * p] = w_re[s * g:(s + 1) * g].astype(BF16)
                rwin[0, rows, (2 + d) * p:(3 + d) * p] = w_im[s * g:(s + 1) * g].astype(BF16)
            ct_re, ct_im = _dot_hi(ctr[d, q], tile_l), _dot_hi(cti[d, q], tile_l)

            def c_pow(sel):
                pc_re, pc_im = _dot_hi(pt_re, sel), _dot_hi(pt_im, sel)
                return ct_re * pc_re - ct_im * pc_im, ct_re * pc_im + ct_im * pc_re

            o_re, o_im = c_pow(sel_out[d])
            r0 = d * S5_GPB * p + q * p
            rwout[0, r0:r0 + p, :] = o_re.astype(BF16)
            rwout[0, S5_HALF + r0:S5_HALF + r0 + p, :] = (-o_im).astype(BF16)
            l_re, l_im = c_pow(sel_lag[d])
            lags.append(_dot_hi(bb_re[0:g], l_re) - _dot_hi(bb_im[0:g], l_im))
            abig[0, 0:1, r0:r0 + p] = pk_re[t:t + 1, :]
            abig[0, 1:2, r0:r0 + p] = pk_im[t:t + 1, :]
        for s in range(t):
            fwd = jnp.where(lane >= g * s, pltpu.roll(lags[0], g * s, 1), 0.0)
            bwd = jnp.where(lane < g * (s + 1), pltpu.roll(lags[1], (row - g * (t - 1 - s)) % row, 1), 0.0)
            rm[0, s * LANE + q * g:s * LANE + (q + 1) * g, :] = (fwd + bwd).astype(BF16)


def _s5_prep(a_re, a_im, log_dt, b_re, b_im, c_re, c_im):
    gg, p, g = S5_GROUPS, S5_STATE, S5_GROUP
    f = lambda x: x.astype(F32)
    args = (f(a_re).reshape(2, gg, 1, p), f(a_im).reshape(2, gg, 1, p),
            f(a_re).reshape(2, gg, p, 1), f(a_im).reshape(2, gg, p, 1), f(log_dt).reshape(2, gg, 1, 1),
            jnp.swapaxes(f(b_re), 2, 3), jnp.swapaxes(f(b_im), 2, 3),
            jnp.swapaxes(f(c_re), 2, 3), jnp.swapaxes(f(c_im), 2, 3))
    spec = lambda r, c: pl.BlockSpec((2, S5_GPB, r, c), lambda j: (0, j, 0, 0))
    out = lambda r, c: pl.BlockSpec((1, r, c), lambda j: (j, 0, 0))
    return pl.pallas_call(
        _s5_prep_kernel,
        out_shape=[jax.ShapeDtypeStruct((S5_LB, S5_BIG, S5_ROW), BF16),
                   jax.ShapeDtypeStruct((S5_LB, S5_BIG, S5_ROW), BF16),
                   jax.ShapeDtypeStruct((S5_LB, 2 * S5_HALF, S5_ROW), BF16),
                   jax.ShapeDtypeStruct((S5_LB, 2, S5_HALF), F32)],
        grid=(S5_LB,),
        in_specs=[spec(1, p), spec(1, p), spec(p, 1), spec(p, 1), spec(1, 1),
                  spec(g, p), spec(g, p), spec(p, g), spec(p, g)],
        out_specs=[out(S5_BIG, S5_ROW), out(S5_BIG, S5_ROW), out(2 * S5_HALF, S5_ROW), out(2, S5_HALF)],
        compiler_params=_cp("parallel"),
        name="s5_prep",
    )(*args)


def _s5_expand(r_ref, col_unit, col_block, row_unit):
    n = S5_BIG
    a = lax.broadcasted_iota(jnp.int32, (S5_ROW, n), 0)
    c = lax.broadcasted_iota(jnp.int32, (S5_ROW, n), 1)
    e = jnp.where((a // col_unit == c // col_block) & (a % col_unit == c % col_unit), 1.0, 0.0).astype(BF16)
    x = _dot(r_ref[0], e)
    rq = (lax.broadcasted_iota(jnp.int32, (n, n), 0) // row_unit) % S5_GPB
    cq = (lax.broadcasted_iota(jnp.int32, (n, n), 1) // col_unit) % S5_GPB
    return jnp.where(rq == cq, x, 0.0).astype(BF16)


def _s5_kernel(ul, uc, rm_ref, rwin_ref, rwout_ref, a_ref, yl, yc, x_s, s_s, m_s, win_s, wout_s, *, ncl, ncc):
    hw = S5_HALF
    hh = hw // 2

    @pl.when(pl.program_id(1) == 0)
    def _():
        m_s[...] = _s5_expand(rm_ref, S5_GROUP, LANE, S5_GROUP)
        win_s[...] = _s5_expand(rwin_ref, S5_STATE, S5_GPB * S5_STATE, S5_GROUP)
        wout_s[...] = _s5_expand(rwout_ref, S5_GROUP, LANE, S5_STATE)

    zc = uc[0, 0].astype(BF16)
    zl = ul[0, 0].astype(BF16)
    x_s[0:ncc, :] = _dot(zc, win_s[...])
    x_s[ncc:ncc + ncl, :] = _dot(zl, win_s[...])
    a_re = a_ref[0, 0:1, :]
    a_im = a_ref[0, 1:2, :]

    def segment(base, n, carry):
        def body(i, c):
            s_re, s_im = c
            rf = pl.ds(base + i, 1)
            rb = pl.ds(base + n - 1 - i, 1)
            s_s[rf, 0:hh] = s_re[:, 0:hh]
            s_s[rb, hh:hw] = s_re[:, hh:hw]
            s_s[rf, hw:hw + hh] = s_im[:, 0:hh]
            s_s[rb, hw + hh:2 * hw] = s_im[:, hh:hw]
            x_re = jnp.concatenate([x_s[rf, 0:hh], x_s[rb, hh:hw]], axis=-1)
            x_im = jnp.concatenate([x_s[rf, hw:hw + hh], x_s[rb, hw + hh:2 * hw]], axis=-1)
            return (a_re * s_re - a_im * s_im + x_re, a_re * s_im + a_im * s_re + x_im)

        return lax.fori_loop(0, n, body, carry)

    zero = jnp.zeros((1, hw), F32)
    carry = segment(0, ncc, (zero, zero))
    segment(ncc, ncl, carry)
    sp = s_s[...].astype(BF16)
    yc[0, 0] = _dot(zc, m_s[...]) + _dot(sp[0:ncc], wout_s[...])
    yl[0, 0] = _dot(zl, m_s[...]) + _dot(sp[ncc:ncc + ncl], wout_s[...])


def _s5(prep, u_l, u_c):
    rm, rwin, rwout, a_big = prep
    b, _, n, _ = u_l.shape
    nc = u_c.shape[2]
    ncl, ncc = n // S5_CHUNK, nc // S5_CHUNK
    zl = u_l.reshape(b, S5_LB, ncl, S5_BIG)
    zc = u_c.reshape(b, S5_LB, ncc, S5_BIG)
    rows = lambda r: pl.BlockSpec((1, 1, r, S5_BIG), lambda j, i: (i, j, 0, 0))
    wspec = lambda r, c: pl.BlockSpec((1, r, c), lambda j, i: (j, 0, 0))
    big = pltpu.VMEM((S5_BIG, S5_BIG), BF16)
    yl, yc = pl.pallas_call(
        functools.partial(_s5_kernel, ncl=ncl, ncc=ncc),
        out_shape=[jax.ShapeDtypeStruct(zl.shape, F32), jax.ShapeDtypeStruct(zc.shape, F32)],
        grid=(S5_LB, b),
        in_specs=[rows(ncl), rows(ncc), wspec(S5_BIG, S5_ROW), wspec(S5_BIG, S5_ROW),
                  wspec(2 * S5_HALF, S5_ROW), wspec(2, S5_HALF)],
        out_specs=[rows(ncl), rows(ncc)],
        scratch_shapes=[pltpu.VMEM((ncc + ncl, 2 * S5_HALF), F32), pltpu.VMEM((ncc + ncl, 2 * S5_HALF), F32),
                        big, big, big],
        compiler_params=_cp("parallel", "arbitrary"),
        name="s5",
    )(zl, zc, rm, rwin, rwout, a_big)
    return yl.reshape(u_l.shape), yc.reshape(u_c.shape)


def _mix_mlp_kernel(*refs, mod_row, s5_merge, final_norm, fb):
    if s5_merge:
        (h_ref, r_ref, y5_ref, u_ref, ds_ref, wg_ref, bg_ref, wo_ref, mod_ref, nm_ref, w1_ref, w2_ref,
         *rest) = refs
    else:
        h_ref, r_ref, wo_ref, mod_ref, nm_ref, w1_ref, w2_ref, *rest = refs
    if final_norm:
        nf_ref, o_ref = rest
    else:
        (o_ref,) = rest
    row = pl.program_id(0) if mod_row is None else mod_row
    if s5_merge:
        y5 = jnp.concatenate([y5_ref[0, lb] for lb in range(S5_LB)], axis=-1)
        u = jnp.concatenate([u_ref[0, lb] for lb in range(S5_LB)], axis=-1)
        y = jax.nn.gelu(y5 + ds_ref[...] * u)
        y = y * jax.nn.sigmoid(_dot(y.astype(BF16), wg_ref[...]) + bg_ref[...])
        mix = _dot(r_ref[0], wo_ref[0:RET_WIDTH, :]) + _dot(y.astype(BF16), wo_ref[RET_WIDTH:D_MODEL, :])
    else:
        mix = _dot(r_ref[0], wo_ref[...])
    h1 = h_ref[0] + _mod_chunk(mod_ref, row, 2) * mix
    xn = _rms(h1) * nm_ref[...]
    xm = (xn * (1.0 + _mod_chunk(mod_ref, row, 4)) + _mod_chunk(mod_ref, row, 3)).astype(BF16)
    acc = None
    for j in range(D_FF // fb):
        a = jnp.square(jnp.maximum(_dot(xm, w1_ref[:, j * fb:(j + 1) * fb]), 0.0)).astype(BF16)
        part = _dot(a, w2_ref[j * fb:(j + 1) * fb, :])
        acc = part if acc is None else acc + part
    h2 = h1 + _mod_chunk(mod_ref, row, 5) * acc
    if final_norm:
        h2 = _rms(h2) * nf_ref[...]
    o_ref[0] = h2


def _mix_mlp(h, r, s5y, u, s5p, wo, mod, layer, nm, w1, w2, nf, mod_row, tm, fb, name):
    b, n, _ = h.shape
    s5_merge = s5y is not None
    final_norm = nf is not None
    one = pl.Buffered(1)
    row_spec = lambda width: pl.BlockSpec((1, tm, width), lambda i, t: (i, t, 0))
    const = lambda shape: pl.BlockSpec(shape, lambda i, t: (0,) * len(shape), pipeline_mode=one)
    in_specs = [row_spec(D_MODEL), row_spec(r.shape[-1])]
    args = [h, r]
    if s5_merge:
        d_skip, w_glu, b_glu = s5p
        lb_spec = pl.BlockSpec((1, S5_LB, tm, LANE), lambda i, t: (i, 0, t, 0))
        in_specs += [lb_spec, lb_spec, const((1, S5_WIDTH)),
                     const((S5_WIDTH, S5_WIDTH)), const((1, S5_WIDTH))]
        args += [s5y, u, d_skip, w_glu, b_glu]
    in_specs += [const((D_MODEL, D_MODEL)),
                 pl.BlockSpec((1, MOD_ROWS, N_MOD * D_MODEL), lambda i, t: (layer, 0, 0), pipeline_mode=one),
                 const((1, D_MODEL)), const((D_MODEL, D_FF)), const((D_FF, D_MODEL))]
    args += [wo, mod, nm, w1, w2]
    if final_norm:
        in_specs.append(const((1, D_MODEL)))
        args.append(nf)
    return pl.pallas_call(
        functools.partial(_mix_mlp_kernel, mod_row=mod_row, s5_merge=s5_merge, final_norm=final_norm, fb=fb),
        out_shape=jax.ShapeDtypeStruct((b, n, D_MODEL), F32),
        grid=(b, n // tm),
        in_specs=in_specs,
        out_specs=row_spec(D_MODEL),
        compiler_params=_cp("parallel", "parallel"),
        name=name,
    )(*args)


def _inproj1_kernel(h_ref, mod_ref, ng_ref, w_ref, *out_refs, mod_row, col0):
    row = pl.program_id(0) if mod_row is None else mod_row
    xn = _rms(h_ref[0]) * ng_ref[...]
    xm = (xn * (1.0 + _mod_chunk(mod_ref, row, 1)) + _mod_chunk(mod_ref, row, 0)).astype(BF16)
    for idx, o_ref in enumerate(out_refs):
        lo = (col0 + idx) * D_MODEL
        o_ref[0] = _dot(xm, w_ref[:, lo:lo + D_MODEL]).astype(o_ref.dtype)


def _inproj1(h, mod, layer, ng, w, col0, dtypes, mod_row, tm, name):
    b, n, _ = h.shape
    one = pl.Buffered(1)
    row_spec = pl.BlockSpec((1, tm, D_MODEL), lambda i, t: (i, t, 0))
    return pl.pallas_call(
        functools.partial(_inproj1_kernel, mod_row=mod_row, col0=col0),
        out_shape=[jax.ShapeDtypeStruct((b, n, D_MODEL), dt) for dt in dtypes],
        grid=(b, n // tm),
        in_specs=[row_spec,
                  pl.BlockSpec((1, MOD_ROWS, N_MOD * D_MODEL), lambda i, t: (layer, 0, 0), pipeline_mode=one),
                  pl.BlockSpec((1, D_MODEL), lambda i, t: (0, 0), pipeline_mode=one),
                  pl.BlockSpec(w.shape, lambda i, t: (0, 0), pipeline_mode=one)],
        out_specs=[row_spec] * len(dtypes),
        compiler_params=_cp("parallel", "parallel"),
        name=name,
    )(h, mod, ng, w)


def _cumsum_mm(tri, x):
    acc = None
    r = x
    for i in range(HG_SPLIT):
        p = r.astype(BF16)
        acc = _dot(tri, p) if acc is None else acc + _dot(tri, p)
        if i + 1 < HG_SPLIT:
            r = r - p.astype(F32)
    return acc


def _hgrn_kernel(lbl_ref, ng_ref, ql, ffl, fbl, il, gl, ffc, fbc, ic, o_ref,
                 qin_s, att_s, kv_s, et_s, kvc_s, etc_s, cum_s, kk_s, *, nbl, nbc, layer, unroll, out_blocks):
    cb = HG_BLOCK
    mid = cb // 2
    gb = HG_GROUP
    gr = gb * cb
    dk = HG_DK

    def lower_bound(d):
        z = [lbl_ref[d, k:k + 1, :] for k in range(DEPTH)]
        zmax = functools.reduce(jnp.maximum, z)
        e = [jnp.exp(v - zmax) for v in z]
        tot = functools.reduce(lambda a, b_: a + b_, e)
        lb = jnp.zeros_like(tot)
        for k in range(1, layer + 1):
            lb = lb + e[k] / tot
        return lb

    ri = lax.broadcasted_iota(jnp.int32, (gr, gr), 0)
    ci = lax.broadcasted_iota(jnp.int32, (gr, gr), 1)
    same = (ri // cb) == (ci // cb)
    rb = lax.broadcasted_iota(jnp.int32, (cb, cb), 0)
    cbi = lax.broadcasted_iota(jnp.int32, (cb, cb), 1)
    tri_l = jnp.where(same & (ri >= ci), 1.0, 0.0).astype(BF16)
    dirs = ((0, lower_bound(0), rb >= cbi, mid - 1, cb - 1), (1, lower_bound(1), rb <= cbi, mid, 0))

    def gates(raws, slot):
        s = [jax.nn.sigmoid(r) for r in raws]
        t = [(1.0 - dr[1]) * si for dr, si in zip(dirs, s)]
        log_f = [jnp.log(dr[1] + ti) for dr, ti in zip(dirs, t)]
        pre = _cumsum_mm(tri_l, jnp.concatenate(log_f, axis=-1))
        pre_b = pre[:, dk:].reshape(gb, cb, dk)
        cum_s[slot, 0] = pre[:, :dk]
        cum_s[slot, 1] = (pre_b[:, cb - 1:cb, :] - pre_b).reshape(gr, dk) + log_f[1]
        for d in range(2):
            kk_s[slot, d] = (1.0 - dirs[d][1]) - t[d]

    def products(slot, v, q, n0, kv_ref, et_ref):
        v3 = v.reshape(gb, cb, dk)
        qts, kts = [], []
        for d, lb, keep, ref_row, tot_row in dirs:
            cum = cum_s[slot, d].reshape(gb, cb, dk)
            kk = kk_s[slot, d].reshape(gb, cb, dk)
            ref = cum[:, ref_row:ref_row + 1, :]
            tot = cum[:, tot_row:tot_row + 1, :]
            e = cum - ref
            kt = kk * jnp.exp(-e)
            k_out = (kt * jnp.exp(tot - ref)).astype(BF16)
            e_tot = jnp.exp(tot)
            if q is not None:
                qt = q.astype(F32).reshape(gb, cb, dk) * jnp.exp(e)
                qin_s[pl.ds(pl.multiple_of(n0 * cb, gr), gr), d * dk:(d + 1) * dk] = (
                    (qt * jnp.exp(ref)).astype(BF16).reshape(gr, dk))
                qts.append(qt.astype(BF16))
                kts.append(kt.astype(BF16))
            for j in range(gb):
                kv_ref[n0 + j, :, d * dk:(d + 1) * dk] = _dot_tn(v3[j], k_out[j])
                et_ref[n0 + j, :, d * dk:(d + 1) * dk] = e_tot[j]
        if q is not None:
            for j in range(gb):
                att = (jnp.where(dirs[0][2], _dot_nt(qts[0][j], kts[0][j]), 0.0)
                       + jnp.where(dirs[1][2], _dot_nt(qts[1][j], kts[1][j]), 0.0))
                att_s[n0 + j] = att.astype(BF16)

    for g in range(nbc // gb):
        sl = slice(g * gr, (g + 1) * gr)
        gates((ffc[0, sl, :], fbc[0, sl, :]), 0)
        products(0, ic[0, sl, :], None, g * gb, kvc_s, etc_s)

    ngl = nbl // gb

    def lat_gates(g, slot):
        sl = pl.ds(pl.multiple_of(g * gr, gr), gr)
        gates((ffl[0, sl, :], fbl[0, sl, :]), slot)

    def lat_products(g, slot):
        sl = pl.ds(pl.multiple_of(g * gr, gr), gr)
        products(slot, il[0, sl, :], ql[0, sl, :], g * gb, kv_s, et_s)

    lat_gates(0, 0)

    def prep_body(g, carry):
        slot = g % 2
        lat_products(g, slot)
        lat_gates(g + 1, 1 - slot)
        return carry

    lax.fori_loop(0, ngl - 1, prep_body, 0)
    lat_products(ngl - 1, (ngl - 1) % 2)

    lane = lax.broadcasted_iota(jnp.int32, (dk, 2 * dk), 1)
    is_f = lane < dk
    st = jnp.zeros((dk, 2 * dk), F32)
    for n in range(nbc):
        m = nbc - 1 - n
        st = (st * jnp.where(is_f[:1], etc_s[n], etc_s[m]) + jnp.where(is_f, kvc_s[n], kvc_s[m]))

    def rec_body(t, st):
        u = nbl - 1 - t
        inc = jnp.where(is_f, kv_s[t], kv_s[u])
        dec = jnp.where(is_f[:1], et_s[t], et_s[u])
        kv_s[t, :, 0:dk] = st[:, 0:dk]
        kv_s[u, :, dk:2 * dk] = st[:, dk:2 * dk]
        return st * dec + inc

    lax.fori_loop(0, nbl, rec_body, st, unroll=unroll)

    def out_body(i, carry):
        for j in range(out_blocks):
            n = i * out_blocks + j
            sl = pl.ds(pl.multiple_of(n * cb, cb), cb)
            o = _dot(att_s[n], il[0, sl, :]) + _dot_nt(qin_s[sl, :], kv_s[n].astype(BF16))
            o = _rms(o) * ng_ref[...] * jax.nn.silu(gl[0, sl, :])
            o_ref[0, sl, :] = o.astype(BF16)
        return carry

    lax.fori_loop(0, nbl // out_blocks, out_body, 0)


def _hgrn(lb_logits, norm_g, layer, q_l, ff_l, fb_l, i_l, g_l, ff_c, fb_c, i_c):
    b, n, _ = q_l.shape
    nc = ff_c.shape[1]
    nbl, nbc = n // HG_BLOCK, nc // HG_BLOCK
    out_blocks = min(16, nbl)
    assert nbl % HG_GROUP == 0 and nbc % HG_GROUP == 0 and nbl % out_blocks == 0
    spec = lambda rows: pl.BlockSpec((1, rows, HG_DK), lambda i, h: (i, 0, h))
    return pl.pallas_call(
        functools.partial(_hgrn_kernel, nbl=nbl, nbc=nbc, layer=layer, unroll=2, out_blocks=out_blocks),
        out_shape=jax.ShapeDtypeStruct((b, n, D_MODEL), BF16),
        grid=(b, HG_HEADS),
        in_specs=[pl.BlockSpec((2, DEPTH, HG_DK), lambda i, h: (0, 0, h)),
                  pl.BlockSpec((1, HG_DK), lambda i, h: (0, 0)),
                  spec(n), spec(n), spec(n), spec(n), spec(n), spec(nc), spec(nc), spec(nc)],
        out_specs=spec(n),
        scratch_shapes=[pltpu.VMEM((n, 2 * HG_DK), BF16),
                        pltpu.VMEM((nbl, HG_BLOCK, HG_BLOCK), BF16),
                        pltpu.VMEM((nbl, HG_DK, 2 * HG_DK), F32),
                        pltpu.VMEM((nbl, 1, 2 * HG_DK), F32),
                        pltpu.VMEM((nbc, HG_DK, 2 * HG_DK), F32),
                        pltpu.VMEM((nbc, 1, 2 * HG_DK), F32),
                        pltpu.VMEM((2, 2, HG_GROUP * HG_BLOCK, HG_DK), F32),
                        pltpu.VMEM((2, 2, HG_GROUP * HG_BLOCK, HG_DK), F32)],
        compiler_params=_cp("parallel", "parallel"),
        name="hgrn2",
    )(lb_logits, norm_g, q_l, ff_l, fb_l, i_l, g_l, ff_c, fb_c, i_c)


def _rope_tables(n_tok):
    rows = n_tok // GRID_W
    row = jnp.broadcast_to(jnp.arange(rows, dtype=F32)[:, None], (rows, GRID_W)).reshape(-1)
    col = jnp.broadcast_to(jnp.arange(GRID_W, dtype=F32)[None, :], (rows, GRID_W)).reshape(-1)
    n_freq = RET_DK // 4
    inv = ROPE_BASE ** (-jnp.arange(n_freq, dtype=F32) / n_freq)
    ang = jnp.concatenate([row[:, None] * inv, col[:, None] * inv], axis=-1)
    cos, sin = jnp.cos(ang), jnp.sin(ang)
    cos_full = jnp.tile(jnp.concatenate([cos, cos], axis=-1), (1, RET_HEADS))
    sin_signed = jnp.tile(jnp.concatenate([-sin, sin], axis=-1), (1, RET_HEADS))
    return cos_full, sin_signed


def kernel(x, c, ctx, c_ctx, w_mod, b_mod, norm_mix, norm_mlp, w_mlp_in, w_mlp_out, ab_w_in, ab_w_out, ret_logit, s5_a_re, s5_a_im, s5_log_dt, s5_b_re, s5_b_im, s5_c_re, s5_c_im, s5_d, s5_w_glu, s5_b_glu, hg_w_in, hg_w_out, hg_lb_logits, hg_norm, norm_final):
    b, n, d = x.shape
    nc = ctx.shape[1]
    assert d == D_MODEL and b + 1 <= MOD_ROWS and w_mod.shape[0] == DEPTH == 2
    assert n % 512 == 0 and nc % 256 == 0 and n % GRID_W == 0
    ctx_row = b
    tm_l, tm_c = 512, 256

    cc = jnp.zeros((MOD_ROWS, d), F32).at[:b].set(c).at[b].set(c_ctx)
    mod = _adaln(cc, w_mod, b_mod)

    row2 = lambda a: a.reshape(1, -1)
    w_in0 = ab_w_in[0].astype(BF16)
    cos, sin = _rope_tables(n)
    ng0 = row2(norm_mix[0])
    q_l, k_l, v_l, u_l, g_l = _inproj0(x, mod, 0, ng0, w_in0, cos, sin, None, tm_l)
    q_c, k_c, v_c, u_c, g_c = _inproj0(ctx, mod, 0, ng0, w_in0, None, None, ctx_row, tm_c)

    log_gamma = jax.nn.log_sigmoid(ret_logit[0].astype(F32))
    lg_rows = jnp.broadcast_to(log_gamma.reshape(2 * RET_HEADS, 1), (2 * RET_HEADS, 2 * RET_DK))
    r_l, r_c = _retention(lg_rows, q_l, k_l, v_l, g_l, q_c, k_c, v_c, g_c)

    s5_ops = _s5_prep(s5_a_re[0], s5_a_im[0], s5_log_dt[0], s5_b_re[0], s5_b_im[0], s5_c_re[0], s5_c_im[0])
    y5_l, y5_c = _s5(s5_ops, u_l, u_c)

    s5p = (row2(s5_d[0]), s5_w_glu[0].astype(BF16), row2(s5_b_glu[0]))
    wo0 = ab_w_out[0].astype(BF16)
    w1_0, w2_0 = w_mlp_in[0].astype(BF16), w_mlp_out[0].astype(BF16)
    nm0 = row2(norm_mlp[0])
    h_l = _mix_mlp(x, r_l, y5_l, u_l, s5p, wo0, mod, 0, nm0, w1_0, w2_0, None, None, tm_l, 1024, "mix_mlp0_lat")
    h_c = _mix_mlp(ctx, r_c, y5_c, u_c, s5p, wo0, mod, 0, nm0, w1_0, w2_0, None, ctx_row, tm_c, 1024, "mix_mlp0_ctx")

    w_in1 = hg_w_in[0].astype(BF16)
    ng1 = row2(norm_mix[1])
    q1, ff_l, fb_l, i_l, g1 = _inproj1(h_l, mod, 1, ng1, w_in1, 0, (BF16, F32, F32, BF16, F32), None, tm_l,
                                       "inproj1_lat")
    ff_c, fb_c, i_c = _inproj1(h_c, mod, 1, ng1, w_in1, 1, (F32, F32, BF16), ctx_row, tm_c, "inproj1_ctx")
    o1 = _hgrn(hg_lb_logits, row2(hg_norm[0]), 1, q1, ff_l, fb_l, i_l, g1, ff_c, fb_c, i_c)
    return _mix_mlp(h_l, o1, None, None, None, hg_w_out[0].astype(BF16), mod, 1, row2(norm_mlp[1]),
                    w_mlp_in[1].astype(BF16), w_mlp_out[1].astype(BF16), row2(norm_final), None, tm_l, 1024,
                    "mix_mlp1_lat")
```

```python
import functools

import jax
import jax.numpy as jnp
from jax import lax
from jax.experimental import pallas as pl
from jax.experimental.pallas import tpu as pltpu

F32 = jnp.float32
BF16 = jnp.bfloat16

D_MODEL = 1024
DEPTH = 2
GRID_W = 64
EPS = 1e-6
ROPE_BASE = 10000.0
N_MOD = 6
RET_HEADS = 4
RET_DK = 64
RET_DV = 128
RET_QK = RET_HEADS * RET_DK
RET_WIDTH = RET_HEADS * RET_DV
RET_CHUNK = 128
S5_WIDTH = D_MODEL - RET_WIDTH
S5_GROUP = 16
S5_GROUPS = S5_WIDTH // S5_GROUP
S5_STATE = 64
S5_CHUNK = 16
S5_ROW = S5_CHUNK * S5_GROUP
LANE = 128
S5_LB = S5_WIDTH // LANE
S5_GPB = LANE // S5_GROUP
S5_BIG = S5_CHUNK * LANE
S5_HALF = S5_GPB * 2 * S5_STATE
S5_POW = 32
AB_IN = 2 * RET_QK + 2 * RET_WIDTH + S5_WIDTH
HG_HEADS = 8
HG_DK = D_MODEL // HG_HEADS
HG_BLOCK = 64
HG_GROUP = 4
HG_SPLIT = 2
D_FF = 4 * D_MODEL
MOD_ROWS = 16

VMEM_LIMIT_BYTES = 56 * 1024 * 1024


def _cp(*sem):
    return pltpu.CompilerParams(dimension_semantics=sem, vmem_limit_bytes=VMEM_LIMIT_BYTES)


def _dot(a, b):
    return jnp.dot(a, b, preferred_element_type=F32)


def _dot_nt(a, b):
    return lax.dot_general(a, b, (((1,), (1,)), ((), ())), preferred_element_type=F32)


def _dot_tn(a, b):
    return lax.dot_general(a, b, (((0,), (0,)), ((), ())), preferred_element_type=F32)


def _rms(x):
    return x * lax.rsqrt(jnp.mean(x * x, axis=-1, keepdims=True) + EPS)


def _mod_chunk(mod_ref, row, i):
    return mod_ref[0, pl.ds(row, 1), i * D_MODEL:(i + 1) * D_MODEL]


def _adaln_kernel(cc_ref, w_ref, b_ref, o_ref):
    s = jax.nn.silu(cc_ref[...]).astype(BF16)
    o_ref[0] = _dot(s, w_ref[0].astype(BF16)) + b_ref[0]


def _adaln(cc, w_mod, b_mod):
    bn = 1536
    n = N_MOD * D_MODEL
    return pl.pallas_call(
        _adaln_kernel,
        out_shape=jax.ShapeDtypeStruct((DEPTH, MOD_ROWS, n), F32),
        grid=(DEPTH, n // bn),
        in_specs=[
            pl.BlockSpec((MOD_ROWS, D_MODEL), lambda l, j: (0, 0)),
            pl.BlockSpec((1, D_MODEL, bn), lambda l, j: (l, 0, j)),
            pl.BlockSpec((1, 1, bn), lambda l, j: (l, 0, j)),
        ],
        out_specs=pl.BlockSpec((1, MOD_ROWS, bn), lambda l, j: (l, 0, j)),
        compiler_params=_cp("parallel", "parallel"),
        name="adaln",
    )(cc, w_mod, b_mod.reshape(DEPTH, 1, n))


def _rope(t, cos, sin):
    lane = lax.broadcasted_iota(jnp.int32, t.shape, 1)
    first = (lane & (RET_DK // 2)) == 0
    w = t.shape[1]
    swapped = jnp.where(first, pltpu.roll(t, w - RET_DK // 2, 1), pltpu.roll(t, RET_DK // 2, 1))
    return t * cos + swapped * sin


def _inproj0_kernel(*refs, mod_row, rope):
    if rope:
        h_ref, mod_ref, ng_ref, w_ref, cos_ref, sin_ref, q_ref, k_ref, v_ref, u_ref, g_ref = refs
    else:
        h_ref, mod_ref, ng_ref, w_ref, q_ref, k_ref, v_ref, u_ref, g_ref = refs
    row = pl.program_id(0) if mod_row is None else mod_row
    xn = _rms(h_ref[0]) * ng_ref[...]
    xm = (xn * (1.0 + _mod_chunk(mod_ref, row, 1)) + _mod_chunk(mod_ref, row, 0)).astype(BF16)
    y = _dot(xm, w_ref[...])
    q = y[:, 0:RET_QK]
    k = y[:, RET_QK:2 * RET_QK]
    if rope:
        q = _rope(q, cos_ref[...], sin_ref[...])
        k = _rope(k, cos_ref[...], sin_ref[...])
    q_ref[0] = q.astype(BF16)
    k_ref[0] = (k * (RET_DK ** -0.5)).astype(BF16)
    c0 = 2 * RET_QK
    v_ref[0] = y[:, c0:c0 + RET_WIDTH].astype(BF16)
    for j in range(S5_LB):
        lo = c0 + RET_WIDTH + j * LANE
        u_ref[0, j] = y[:, lo:lo + LANE]
    g_ref[0] = y[:, c0 + RET_WIDTH + S5_WIDTH:]


def _inproj0(h, mod, layer, ng, w, cos, sin, mod_row, tm):
    b, n, _ = h.shape
    rope = cos is not None
    row_spec = lambda width: pl.BlockSpec((1, tm, width), lambda i, j: (i, j, 0))
    in_specs = [
        row_spec(D_MODEL),
        pl.BlockSpec((1, MOD_ROWS, N_MOD * D_MODEL), lambda i, j: (layer, 0, 0)),
        pl.BlockSpec((1, D_MODEL), lambda i, j: (0, 0)),
        pl.BlockSpec((D_MODEL, AB_IN), lambda i, j: (0, 0)),
    ]
    args = [h, mod, ng, w]
    if rope:
        in_specs += [pl.BlockSpec((tm, RET_QK), lambda i, j: (j, 0))] * 2
        args += [cos, sin]
    widths = (RET_QK, RET_QK, RET_WIDTH, S5_WIDTH, RET_WIDTH)
    dtypes = (BF16, BF16, BF16, F32, F32)
    out_shape = [jax.ShapeDtypeStruct((b, n, wd), dt) for wd, dt in zip(widths, dtypes)]
    out_specs = [row_spec(wd) for wd in widths]
    out_shape[3] = jax.ShapeDtypeStruct((b, S5_LB, n, LANE), F32)
    out_specs[3] = pl.BlockSpec((1, S5_LB, tm, LANE), lambda i, j: (i, 0, j, 0))
    return pl.pallas_call(
        functools.partial(_inproj0_kernel, mod_row=mod_row, rope=rope),
        out_shape=out_shape,
        grid=(b, n // tm),
        in_specs=in_specs,
        out_specs=out_specs,
        compiler_params=_cp("parallel", "parallel"),
        name="inproj0_lat" if rope else "inproj0_ctx",
    )(*args)


def _ret_kernel(lg_ref, ql, kl, vl, gl, qc, kc, vc, gc, rl, rc, st_s, *, ncl, ncc, unroll):
    c = RET_CHUNK
    dk2 = 2 * RET_DK
    nt = ncc + ncl
    p = pl.program_id(1)
    h_a = 2 * p
    lgf_a = lg_ref[pl.ds(h_a, 1), :]
    lgf_b = lg_ref[pl.ds(h_a + 1, 1), :]
    lgb_a = lg_ref[pl.ds(RET_HEADS + h_a, 1), :]
    lgb_b = lg_ref[pl.ds(RET_HEADS + h_a + 1, 1), :]
    lane = lax.broadcasted_iota(jnp.int32, (1, 2 * RET_DK), 1)
    is_a = lane < RET_DK
    lgf_lane = jnp.where(is_a, lgf_a, lgf_b)
    lgb_lane = jnp.where(is_a, lgb_a, lgb_b)
    ri = lax.broadcasted_iota(jnp.int32, (c, c), 0).astype(F32)
    ci = lax.broadcasted_iota(jnp.int32, (c, c), 1).astype(F32)
    diff = ri - ci

    def dmat(lgf, lgb):
        fwd = jnp.exp(jnp.maximum(diff, 0.0) * lgf)
        bwd = jnp.exp(jnp.maximum(-diff, 0.0) * lgb)
        return jnp.where(diff > 0, fwd, jnp.where(diff < 0, bwd, 2.0))

    d_a = dmat(lgf_a, lgb_a)
    d_b = dmat(lgf_b, lgb_b)
    rowp = lax.broadcasted_iota(jnp.int32, (c, dk2), 0).astype(F32)
    qd = jnp.concatenate([jnp.exp((rowp + 1.0) * lgf_lane), jnp.exp((c - rowp) * lgb_lane)], axis=1)
    kd = jnp.concatenate([jnp.exp((c - 1.0 - rowp) * lgf_lane), jnp.exp(rowp * lgb_lane)], axis=1)
    rowk = lax.broadcasted_iota(jnp.int32, (dk2, 2 * RET_DV), 0)
    cd_f = jnp.exp(c * jnp.where(rowk < RET_DK, lgf_a[:, :1], lgf_b[:, :1]))
    cd_b = jnp.exp(c * jnp.where(rowk < RET_DK, lgb_a[:, :1], lgb_b[:, :1]))
    mask2 = jnp.concatenate([is_a, is_a], axis=1)

    def increment(k, v, slot):
        kk = jnp.concatenate([k, k], axis=1).astype(F32) * kd
        st_s[slot] = _dot_tn(kk.astype(BF16), v)

    for n in range(ncc):
        increment(kc[0, n * c:(n + 1) * c, :], vc[0, n * c:(n + 1) * c, :], n)

    def inc_body(i, carry):
        for j in range(unroll):
            n = i * unroll + j
            sl = pl.ds(pl.multiple_of(n * c, c), c)
            increment(kl[0, sl, :], vl[0, sl, :], ncc + n)
        return carry

    lax.fori_loop(0, ncl // unroll, inc_body, 0)

    def rec_body(t, carry):
        sf, sb = carry
        u = jnp.where(t < ncc, ncc - 1 - t, nt - 1 - (t - ncc))
        inc_f = st_s[t, 0:dk2, :]
        inc_b = st_s[u, dk2:2 * dk2, :]
        st_s[t, 0:dk2, :] = sf
        st_s[u, dk2:2 * dk2, :] = sb
        return cd_f * sf + inc_f, cd_b * sb + inc_b

    zero = jnp.zeros((dk2, 2 * RET_DV), F32)
    lax.fori_loop(0, nt, rec_body, (zero, zero))

    def output(q, k, v, g, slot, out_ref, st):
        q2 = (jnp.concatenate([q, q], axis=1).astype(F32) * qd).astype(BF16)
        s_n = st_s[slot].astype(BF16)
        for keep_a, dm, cs in ((True, d_a, 0), (False, d_b, RET_DV)):
            m1 = is_a if keep_a else jnp.logical_not(is_a)
            m2 = mask2 if keep_a else jnp.logical_not(mask2)
            att = _dot_nt(jnp.where(m1, q, jnp.zeros_like(q)), k) * dm
            o = (_dot(att.astype(BF16), v[:, cs:cs + RET_DV])
                 + _dot(jnp.where(m2, q2, jnp.zeros_like(q2)), s_n[:, cs:cs + RET_DV]))
            o = _rms(o) * jax.nn.silu(g[:, cs:cs + RET_DV])
            out_ref[0, pl.ds(st, c), cs:cs + RET_DV] = o.astype(BF16)

    for n in range(ncc):
        sl = slice(n * c, (n + 1) * c)
        output(qc[0, sl, :], kc[0, sl, :], vc[0, sl, :], gc[0, sl, :], n, rc, n * c)

    def out_body(i, carry):
        for j in range(unroll):
            n = i * unroll + j
            st = pl.multiple_of(n * c, c)
            sl = pl.ds(st, c)
            output(ql[0, sl, :], kl[0, sl, :], vl[0, sl, :], gl[0, sl, :], ncc + n, rl, st)
        return carry

    lax.fori_loop(0, ncl // unroll, out_body, 0)


def _retention(lg_rows, q_l, k_l, v_l, g_l, q_c, k_c, v_c, g_c):
    b, n, _ = q_l.shape
    nc = q_c.shape[1]
    ncl, ncc = n // RET_CHUNK, nc // RET_CHUNK
    pairs = RET_HEADS // 2
    unroll = 8 if ncl % 8 == 0 else 1

    def spec(rows, width):
        return pl.BlockSpec((1, rows, width), lambda i, p: (i, 0, p))

    return pl.pallas_call(
        functools.partial(_ret_kernel, ncl=ncl, ncc=ncc, unroll=unroll),
        out_shape=[jax.ShapeDtypeStruct((b, n, RET_WIDTH), BF16),
                   jax.ShapeDtypeStruct((b, nc, RET_WIDTH), BF16)],
        grid=(b, pairs),
        in_specs=[pl.BlockSpec((2 * RET_HEADS, 2 * RET_DK), lambda i, p: (0, 0)),
                  spec(n, 2 * RET_DK), spec(n, 2 * RET_DK), spec(n, 2 * RET_DV), spec(n, 2 * RET_DV),
                  spec(nc, 2 * RET_DK), spec(nc, 2 * RET_DK), spec(nc, 2 * RET_DV), spec(nc, 2 * RET_DV)],
        out_specs=[spec(n, 2 * RET_DV), spec(nc, 2 * RET_DV)],
        scratch_shapes=[pltpu.VMEM((ncl + ncc, 4 * RET_DK, 2 * RET_DV), F32)],
        compiler_params=_cp("parallel", "parallel"),
        name="retention",
    )(lg_rows, q_l, k_l, v_l, g_l, q_c, k_c, v_c, g_c)


def _dot_hi(a, b, contract=(1, 0)):
    dims = (((contract[0],), (contract[1],)), ((), ()))
    return lax.dot_general(a, b, dims, preferred_element_type=F32, precision=lax.Precision.HIGHEST)


def _s5_prep_kernel(ar_row, ai_row, ar_col, ai_col, ldt, btr, bti, ctr, cti, rm, rwin, rwout, abig):
    t, g, p, kp = S5_CHUNK, S5_GROUP, S5_STATE, S5_POW
    row = S5_ROW
    i0 = lambda shape: lax.broadcasted_iota(jnp.int32, shape, 0)
    i1 = lambda shape: lax.broadcasted_iota(jnp.int32, shape, 1)
    f32 = lambda m: jnp.where(m, 1.0, 0.0).astype(F32)
    s_of_r = i0((row, kp)) // g
    k_of_l = i1((row, kp))
    sel_rows = (f32(k_of_l == t - 1 - s_of_r), f32(k_of_l == s_of_r))
    t_of_c = i1((kp, row)) // g
    k_of_s = i0((kp, row))
    sel_out = (f32(k_of_s == t_of_c + 1), f32(k_of_s == t - t_of_c))
    sel_lag = (f32(k_of_s == t_of_c), f32(k_of_s == t - 1 - t_of_c))
    tile_l = f32(i1((g, row)) % g == i0((g, row)))
    tile_r = f32(i0((row, g)) % g == i1((row, g)))
    lane = i1((g, row))
    k_col = i0((kp, 1)).astype(F32)
    k_row = i1((1, kp)).astype(F32)

    for q in range(S5_GPB):
        lags = []
        for d in range(2):
            dt = jnp.exp(ldt[d, q])
            are_r, aim_r = ar_row[d, q], ai_row[d, q]
            are_c, aim_c = ar_col[d, q], ai_col[d, q]
            mag = jnp.exp(are_r * dt)
            ang = aim_r * dt
            nr, ni = mag * jnp.cos(ang) - 1.0, mag * jnp.sin(ang)
            den = jnp.square(are_r) + jnp.square(aim_r)
            fr = (nr * are_r + ni * aim_r) / den
            fi = (ni * are_r - nr * aim_r) / den
            pm = jnp.exp(k_col * (are_r * dt))
            pa = k_col * ang
            pk_re, pk_im = pm * jnp.cos(pa), pm * jnp.sin(pa)
            pmt = jnp.exp(k_row * (are_c * dt))
            pat = k_row * (aim_c * dt)
            pt_re, pt_im = pmt * jnp.cos(pat), pmt * jnp.sin(pat)
            bt_re, bt_im = _dot_hi(tile_r, btr[d, q], (1, 1)), _dot_hi(tile_r, bti[d, q], (1, 1))
            bb_re = fr * bt_re - fi * bt_im
            bb_im = fr * bt_im + fi * bt_re
            pr_re, pr_im = _dot_hi(sel_rows[d], pk_re), _dot_hi(sel_rows[d], pk_im)
            w_re = pr_re * bb_re - pr_im * bb_im
            w_im = pr_re * bb_im + pr_im * bb_re
            for s in range(t):
                rows = slice(s * LANE + q * g, s * LANE + (q + 1) * g)
                rwin[0, rows, d * p:(d + 1) * p] = w_re[s * g:(s + 1) * g].astype(BF16)
                rwin[0, rows, (2 + d) * p:(3 + d) * p] = w_im[s * g:(s + 1) * g].astype(BF16)
            ct_re, ct_im = _dot_hi(ctr[d, q], tile_l, (0, 0)), _dot_hi(cti[d, q], tile_l, (0, 0))

            def c_pow(sel):
                pc_re, pc_im = _dot_hi(pt_re, sel), _dot_hi(pt_im, sel)
                return ct_re * pc_re - ct_im * pc_im, ct_re * pc_im + ct_im * pc_re

            o_re, o_im = c_pow(sel_out[d])
            r0 = d * S5_GPB * p + q * p
            rwout[0, r0:r0 + p, :] = o_re.astype(BF16)
            rwout[0, S5_HALF + r0:S5_HALF + r0 + p, :] = (-o_im).astype(BF16)
            l_re, l_im = c_pow(sel_lag[d])
            lags.append(_dot_hi(bb_re[0:g], l_re) - _dot_hi(bb_im[0:g], l_im))
            abig[0, 0:1, r0:r0 + p] = pk_re[t:t + 1, :]
            abig[0, 1:2, r0:r0 + p] = pk_im[t:t + 1, :]
        for s in range(t):
            fwd = jnp.where(lane >= g * s, pltpu.roll(lags[0], g * s, 1), 0.0)
            bwd = jnp.where(lane < g * (s + 1), pltpu.roll(lags[1], (row - g * (t - 1 - s)) % row, 1), 0.0)
            rm[0, s * LANE + q * g:s * LANE + (q + 1) * g, :] = (fwd + bwd).astype(BF16)


def _s5_prep(a_re, a_im, log_dt, b_re, b_im, c_re, c_im):
    gg, p, g = S5_GROUPS, S5_STATE, S5_GROUP
    f = lambda x: x.astype(F32)
    args = (f(a_re).reshape(2, gg, 1, p), f(a_im).reshape(2, gg, 1, p),
            f(a_re).reshape(2, gg, p, 1), f(a_im).reshape(2, gg, p, 1), f(log_dt).reshape(2, gg, 1, 1),
            f(b_re), f(b_im), f(c_re), f(c_im))
    spec = lambda r, c: pl.BlockSpec((2, S5_GPB, r, c), lambda j: (0, j, 0, 0))
    out = lambda r, c: pl.BlockSpec((1, r, c), lambda j: (j, 0, 0))
    return pl.pallas_call(
        _s5_prep_kernel,
        out_shape=[jax.ShapeDtypeStruct((S5_LB, S5_BIG, S5_ROW), BF16),
                   jax.ShapeDtypeStruct((S5_LB, S5_BIG, S5_ROW), BF16),
                   jax.ShapeDtypeStruct((S5_LB, 2 * S5_HALF, S5_ROW), BF16),
                   jax.ShapeDtypeStruct((S5_LB, 2, S5_HALF), F32)],
        grid=(S5_LB,),
        in_specs=[spec(1, p), spec(1, p), spec(p, 1), spec(p, 1), spec(1, 1),
                  spec(p, g), spec(p, g), spec(g, p), spec(g, p)],
        out_specs=[out(S5_BIG, S5_ROW), out(S5_BIG, S5_ROW), out(2 * S5_HALF, S5_ROW), out(2, S5_HALF)],
        compiler_params=_cp("parallel"),
        name="s5_prep",
    )(*args)


def _s5_expand(r_ref, col_unit, col_block, row_unit):
    n = S5_BIG
    a = lax.broadcasted_iota(jnp.int32, (S5_ROW, n), 0)
    c = lax.broadcasted_iota(jnp.int32, (S5_ROW, n), 1)
    e = jnp.where((a // col_unit == c // col_block) & (a % col_unit == c % col_unit), 1.0, 0.0).astype(BF16)
    x = _dot(r_ref[0], e)
    rq = (lax.broadcasted_iota(jnp.int32, (n, n), 0) // row_unit) % S5_GPB
    cq = (lax.broadcasted_iota(jnp.int32, (n, n), 1) // col_unit) % S5_GPB
    return jnp.where(rq == cq, x, 0.0).astype(BF16)


def _s5_kernel(ul, uc, rm_ref, rwin_ref, rwout_ref, a_ref, yl, yc, x_s, s_s, m_s, win_s, wout_s, *, ncl, ncc):
    hw = S5_HALF
    hh = hw // 2

    @pl.when(pl.program_id(1) == 0)
    def _():
        m_s[...] = _s5_expand(rm_ref, S5_GROUP, LANE, S5_GROUP)
        win_s[...] = _s5_expand(rwin_ref, S5_STATE, S5_GPB * S5_STATE, S5_GROUP)
        wout_s[...] = _s5_expand(rwout_ref, S5_GROUP, LANE, S5_STATE)

    zc = uc[0, 0].astype(BF16)
    zl = ul[0, 0].astype(BF16)
    x_s[0:ncc, :] = _dot(zc, win_s[...])
    x_s[ncc:ncc + ncl, :] = _dot(zl, win_s[...])
    a_re = a_ref[0, 0:1, :]
    a_im = a_ref[0, 1:2, :]

    def segment(base, n, carry):
        def body(i, c):
            s_re, s_im = c
            rf = pl.ds(base + i, 1)
            rb = pl.ds(base + n - 1 - i, 1)
            s_s[rf, 0:hh] = s_re[:, 0:hh]
            s_s[rb, hh:hw] = s_re[:, hh:hw]
            s_s[rf, hw:hw + hh] = s_im[:, 0:hh]
            s_s[rb, hw + hh:2 * hw] = s_im[:, hh:hw]
            x_re = jnp.concatenate([x_s[rf, 0:hh], x_s[rb, hh:hw]], axis=-1)
            x_im = jnp.concatenate([x_s[rf, hw:hw + hh], x_s[rb, hw + hh:2 * hw]], axis=-1)
            return (a_re * s_re - a_im * s_im + x_re, a_re * s_im + a_im * s_re + x_im)

        return lax.fori_loop(0, n, body, carry)

    zero = jnp.zeros((1, hw), F32)
    carry = segment(0, ncc, (zero, zero))
    segment(ncc, ncl, carry)
    sp = s_s[...].astype(BF16)
    yc[0, 0] = _dot(zc, m_s[...]) + _dot(sp[0:ncc], wout_s[...])
    yl[0, 0] = _dot(zl, m_s[...]) + _dot(sp[ncc:ncc + ncl], wout_s[...])


def _s5(prep, u_l, u_c):
    rm, rwin, rwout, a_big = prep
    b, _, n, _ = u_l.shape
    nc = u_c.shape[2]
    ncl, ncc = n // S5_CHUNK, nc // S5_CHUNK
    zl = u_l.reshape(b, S5_LB, ncl, S5_BIG)
    zc = u_c.reshape(b, S5_LB, ncc, S5_BIG)
    rows = lambda r: pl.BlockSpec((1, 1, r, S5_BIG), lambda j, i: (i, j, 0, 0))
    wspec = lambda r, c: pl.BlockSpec((1, r, c), lambda j, i: (j, 0, 0))
    big = pltpu.VMEM((S5_BIG, S5_BIG), BF16)
    yl, yc = pl.pallas_call(
        functools.partial(_s5_kernel, ncl=ncl, ncc=ncc),
        out_shape=[jax.ShapeDtypeStruct(zl.shape, F32), jax.ShapeDtypeStruct(zc.shape, F32)],
        grid=(S5_LB, b),
        in_specs=[rows(ncl), rows(ncc), wspec(S5_BIG, S5_ROW), wspec(S5_BIG, S5_ROW),
                  wspec(2 * S5_HALF, S5_ROW), wspec(2, S5_HALF)],
        out_specs=[rows(ncl), rows(ncc)],
        scratch_shapes=[pltpu.VMEM((ncc + ncl, 2 * S5_HALF), F32), pltpu.VMEM((ncc + ncl, 2 * S5_HALF), F32),
                        big, big, big],
        compiler_params=_cp("parallel", "arbitrary"),
        name="s5",
    )(zl, zc, rm, rwin, rwout, a_big)
    return yl.reshape(u_l.shape), yc.reshape(u_c.shape)


def _mix_mlp_kernel(*refs, mod_row, s5_merge, final_norm, fb):
    if s5_merge:
        (h_ref, r_ref, y5_ref, u_ref, ds_ref, wg_ref, bg_ref, wo_ref, mod_ref, nm_ref, w1_ref, w2_ref,
         *rest) = refs
    else:
        h_ref, r_ref, wo_ref, mod_ref, nm_ref, w1_ref, w2_ref, *rest = refs
    if final_norm:
        nf_ref, o_ref = rest
    else:
        (o_ref,) = rest
    row = pl.program_id(0) if mod_row is None else mod_row
    if s5_merge:
        y5 = jnp.concatenate([y5_ref[0, lb] for lb in range(S5_LB)], axis=-1)
        u = jnp.concatenate([u_ref[0, lb] for lb in range(S5_LB)], axis=-1)
        y = jax.nn.gelu(y5 + ds_ref[...] * u)
        y = y * jax.nn.sigmoid(_dot(y.astype(BF16), wg_ref[...]) + bg_ref[...])
        mix = _dot(r_ref[0], wo_ref[0:RET_WIDTH, :]) + _dot(y.astype(BF16), wo_ref[RET_WIDTH:D_MODEL, :])
    else:
        mix = _dot(r_ref[0], wo_ref[...])
    h1 = h_ref[0] + _mod_chunk(mod_ref, row, 2) * mix
    xn = _rms(h1) * nm_ref[...]
    xm = (xn * (1.0 + _mod_chunk(mod_ref, row, 4)) + _mod_chunk(mod_ref, row, 3)).astype(BF16)
    acc = None
    for j in range(D_FF // fb):
        a = jnp.square(jnp.maximum(_dot(xm, w1_ref[:, j * fb:(j + 1) * fb]), 0.0)).astype(BF16)
        part = _dot(a, w2_ref[j * fb:(j + 1) * fb, :])
        acc = part if acc is None else acc + part
    h2 = h1 + _mod_chunk(mod_ref, row, 5) * acc
    if final_norm:
        h2 = _rms(h2) * nf_ref[...]
    o_ref[0] = h2


def _mix_mlp(h, r, s5y, u, s5p, wo, mod, layer, nm, w1, w2, nf, mod_row, tm, fb, name):
    b, n, _ = h.shape
    s5_merge = s5y is not None
    final_norm = nf is not None
    one = pl.Buffered(1)
    row_spec = lambda width: pl.BlockSpec((1, tm, width), lambda i, t: (i, t, 0))
    const = lambda shape: pl.BlockSpec(shape, lambda i, t: (0,) * len(shape), pipeline_mode=one)
    in_specs = [row_spec(D_MODEL), row_spec(r.shape[-1])]
    args = [h, r]
    if s5_merge:
        d_skip, w_glu, b_glu = s5p
        lb_spec = pl.BlockSpec((1, S5_LB, tm, LANE), lambda i, t: (i, 0, t, 0))
        in_specs += [lb_spec, lb_spec, const((1, S5_WIDTH)),
                     const((S5_WIDTH, S5_WIDTH)), const((1, S5_WIDTH))]
        args += [s5y, u, d_skip, w_glu, b_glu]
    in_specs += [const((D_MODEL, D_MODEL)),
                 pl.BlockSpec((1, MOD_ROWS, N_MOD * D_MODEL), lambda i, t: (layer, 0, 0), pipeline_mode=one),
                 const((1, D_MODEL)), const((D_MODEL, D_FF)), const((D_FF, D_MODEL))]
    args += [wo, mod, nm, w1, w2]
    if final_norm:
        in_specs.append(const((1, D_MODEL)))
        args.append(nf)
    return pl.pallas_call(
        functools.partial(_mix_mlp_kernel, mod_row=mod_row, s5_merge=s5_merge, final_norm=final_norm, fb=fb),
        out_shape=jax.ShapeDtypeStruct((b, n, D_MODEL), F32),
        grid=(b, n // tm),
        in_specs=in_specs,
        out_specs=row_spec(D_MODEL),
        compiler_params=_cp("parallel", "parallel"),
        name=name,
    )(*args)


def _inproj1_kernel(h_ref, mod_ref, ng_ref, w_ref, *out_refs, mod_row, col0):
    row = pl.program_id(0) if mod_row is None else mod_row
    xn = _rms(h_ref[0]) * ng_ref[...]
    xm = (xn * (1.0 + _mod_chunk(mod_ref, row, 1)) + _mod_chunk(mod_ref, row, 0)).astype(BF16)
    for idx, o_ref in enumerate(out_refs):
        lo = (col0 + idx) * D_MODEL
        o_ref[0] = _dot(xm, w_ref[:, lo:lo + D_MODEL]).astype(o_ref.dtype)


def _inproj1(h, mod, layer, ng, w, col0, dtypes, mod_row, tm, name):
    b, n, _ = h.shape
    one = pl.Buffered(1)
    row_spec = pl.BlockSpec((1, tm, D_MODEL), lambda i, t: (i, t, 0))
    return pl.pallas_call(
        functools.partial(_inproj1_kernel, mod_row=mod_row, col0=col0),
        out_shape=[jax.ShapeDtypeStruct((b, n, D_MODEL), dt) for dt in dtypes],
        grid=(b, n // tm),
        in_specs=[row_spec,
                  pl.BlockSpec((1, MOD_ROWS, N_MOD * D_MODEL), lambda i, t: (layer, 0, 0), pipeline_mode=one),
                  pl.BlockSpec((1, D_MODEL), lambda i, t: (0, 0), pipeline_mode=one),
                  pl.BlockSpec(w.shape, lambda i, t: (0, 0), pipeline_mode=one)],
        out_specs=[row_spec] * len(dtypes),
        compiler_params=_cp("parallel", "parallel"),
        name=name,
    )(h, mod, ng, w)


def _cumsum_mm(tri, x):
    acc = None
    r = x
    for i in range(HG_SPLIT):
        p = r.astype(BF16)
        acc = _dot(tri, p) if acc is None else acc + _dot(tri, p)
        if i + 1 < HG_SPLIT:
            r = r - p.astype(F32)
    return acc


def _hgrn_kernel(lbl_ref, ng_ref, ql, ffl, fbl, il, gl, ffc, fbc, ic, o_ref,
                 qin_s, att_s, kv_s, et_s, kvc_s, etc_s, cum_s, kk_s, *, nbl, nbc, layer, unroll, out_blocks):
    cb = HG_BLOCK
    mid = cb // 2
    gb = HG_GROUP
    gr = gb * cb
    dk = HG_DK

    def lower_bound(d):
        z = [lbl_ref[d, k:k + 1, :] for k in range(DEPTH)]
        zmax = functools.reduce(jnp.maximum, z)
        e = [jnp.exp(v - zmax) for v in z]
        tot = functools.reduce(lambda a, b_: a + b_, e)
        lb = jnp.zeros_like(tot)
        for k in range(1, layer + 1):
            lb = lb + e[k] / tot
        return lb

    ri = lax.broadcasted_iota(jnp.int32, (gr, gr), 0)
    ci = lax.broadcasted_iota(jnp.int32, (gr, gr), 1)
    same = (ri // cb) == (ci // cb)
    rb = lax.broadcasted_iota(jnp.int32, (cb, cb), 0)
    cbi = lax.broadcasted_iota(jnp.int32, (cb, cb), 1)
    tri_l = jnp.where(same & (ri >= ci), 1.0, 0.0).astype(BF16)
    dirs = ((0, lower_bound(0), rb >= cbi, mid - 1, cb - 1), (1, lower_bound(1), rb <= cbi, mid, 0))

    def gates(raws, slot):
        s = [jax.nn.sigmoid(r) for r in raws]
        t = [(1.0 - dr[1]) * si for dr, si in zip(dirs, s)]
        log_f = [jnp.log(dr[1] + ti) for dr, ti in zip(dirs, t)]
        pre = _cumsum_mm(tri_l, jnp.concatenate(log_f, axis=-1))
        pre_b = pre[:, dk:].reshape(gb, cb, dk)
        cum_s[slot, 0] = pre[:, :dk]
        cum_s[slot, 1] = (pre_b[:, cb - 1:cb, :] - pre_b).reshape(gr, dk) + log_f[1]
        for d in range(2):
            kk_s[slot, d] = (1.0 - dirs[d][1]) - t[d]

    def products(slot, v, q, n0, kv_ref, et_ref):
        v3 = v.reshape(gb, cb, dk)
        qts, kts = [], []
        for d, lb, keep, ref_row, tot_row in dirs:
            cum = cum_s[slot, d].reshape(gb, cb, dk)
            kk = kk_s[slot, d].reshape(gb, cb, dk)
            ref = cum[:, ref_row:ref_row + 1, :]
            tot = cum[:, tot_row:tot_row + 1, :]
            e = cum - ref
            kt = kk * jnp.exp(-e)
            k_out = (kt * jnp.exp(tot - ref)).astype(BF16)
            e_tot = jnp.exp(tot)
            if q is not None:
                qt = q.astype(F32).reshape(gb, cb, dk) * jnp.exp(e)
                qin_s[pl.ds(pl.multiple_of(n0 * cb, gr), gr), d * dk:(d + 1) * dk] = (
                    (qt * jnp.exp(ref)).astype(BF16).reshape(gr, dk))
                qts.append(qt.astype(BF16))
                kts.append(kt.astype(BF16))
            for j in range(gb):
                kv_ref[n0 + j, :, d * dk:(d + 1) * dk] = _dot_tn(v3[j], k_out[j])
                et_ref[n0 + j, :, d * dk:(d + 1) * dk] = e_tot[j]
        if q is not None:
            for j in range(gb):
                att = (jnp.where(dirs[0][2], _dot_nt(qts[0][j], kts[0][j]), 0.0)
                       + jnp.where(dirs[1][2], _dot_nt(qts[1][j], kts[1][j]), 0.0))
                att_s[n0 + j] = att.astype(BF16)

    for g in range(nbc // gb):
        sl = slice(g * gr, (g + 1) * gr)
        gates((ffc[0, sl, :], fbc[0, sl, :]), 0)
        products(0, ic[0, sl, :], None, g * gb, kvc_s, etc_s)

    ngl = nbl // gb

    def lat_gates(g, slot):
        sl = pl.ds(pl.multiple_of(g * gr, gr), gr)
        gates((ffl[0, sl, :], fbl[0, sl, :]), slot)

    def lat_products(g, slot):
        sl = pl.ds(pl.multiple_of(g * gr, gr), gr)
        products(slot, il[0, sl, :], ql[0, sl, :], g * gb, kv_s, et_s)

    lat_gates(0, 0)

    def prep_body(g, carry):
        slot = g % 2
        lat_products(g, slot)
        lat_gates(g + 1, 1 - slot)
        return carry

    lax.fori_loop(0, ngl - 1, prep_body, 0)
    lat_products(ngl - 1, (ngl - 1) % 2)

    lane = lax.broadcasted_iota(jnp.int32, (dk, 2 * dk), 1)
    is_f = lane < dk
    st = jnp.zeros((dk, 2 * dk), F32)
    for n in range(nbc):
        m = nbc - 1 - n
        st = (st * jnp.where(is_f[:1], etc_s[n], etc_s[m]) + jnp.where(is_f, kvc_s[n], kvc_s[m]))

    def rec_body(t, st):
        u = nbl - 1 - t
        inc = jnp.where(is_f, kv_s[t], kv_s[u])
        dec = jnp.where(is_f[:1], et_s[t], et_s[u])
        kv_s[t, :, 0:dk] = st[:, 0:dk]
        kv_s[u, :, dk:2 * dk] = st[:, dk:2 * dk]
        return st * dec + inc

    lax.fori_loop(0, nbl, rec_body, st, unroll=unroll)

    def out_body(i, carry):
        for j in range(out_blocks):
            n = i * out_blocks + j
            sl = pl.ds(pl.multiple_of(n * cb, cb), cb)
            o = _dot(att_s[n], il[0, sl, :]) + _dot_nt(qin_s[sl, :], kv_s[n].astype(BF16))
            o = _rms(o) * ng_ref[...] * jax.nn.silu(gl[0, sl, :])
            o_ref[0, sl, :] = o.astype(BF16)
        return carry

    lax.fori_loop(0, nbl // out_blocks, out_body, 0)


def _hgrn(lb_logits, norm_g, layer, q_l, ff_l, fb_l, i_l, g_l, ff_c, fb_c, i_c):
    b, n, _ = q_l.shape
    nc = ff_c.shape[1]
    nbl, nbc = n // HG_BLOCK, nc // HG_BLOCK
    out_blocks = min(16, nbl)
    assert nbl % HG_GROUP == 0 and nbc % HG_GROUP == 0 and nbl % out_blocks == 0
    spec = lambda rows: pl.BlockSpec((1, rows, HG_DK), lambda i, h: (i, 0, h))
    return pl.pallas_call(
        functools.partial(_hgrn_kernel, nbl=nbl, nbc=nbc, layer=layer, unroll=2, out_blocks=out_blocks),
        out_shape=jax.ShapeDtypeStruct((b, n, D_MODEL), BF16),
        grid=(b, HG_HEADS),
        in_specs=[pl.BlockSpec((2, DEPTH, HG_DK), lambda i, h: (0, 0, h)),
                  pl.BlockSpec((1, HG_DK), lambda i, h: (0, 0)),
                  spec(n), spec(n), spec(n), spec(n), spec(n), spec(nc), spec(nc), spec(nc)],
        out_specs=spec(n),
        scratch_shapes=[pltpu.VMEM((n, 2 * HG_DK), BF16),
                        pltpu.VMEM((nbl, HG_BLOCK, HG_BLOCK), BF16),
                        pltpu.VMEM((nbl, HG_DK, 2 * HG_DK), F32),
                        pltpu.VMEM((nbl, 1, 2 * HG_DK), F32),
                        pltpu.VMEM((nbc, HG_DK, 2 * HG_DK), F32),
                        pltpu.VMEM((nbc, 1, 2 * HG_DK), F32),
                        pltpu.VMEM((2, 2, HG_GROUP * HG_BLOCK, HG_DK), F32),
                        pltpu.VMEM((2, 2, HG_GROUP * HG_BLOCK, HG_DK), F32)],
        compiler_params=_cp("parallel", "parallel"),
        name="hgrn2",
    )(lb_logits, norm_g, q_l, ff_l, fb_l, i_l, g_l, ff_c, fb_c, i_c)


def _rope_tables(n_tok):
    rows = n_tok // GRID_W
    row = jnp.broadcast_to(jnp.arange(rows, dtype=F32)[:, None], (rows, GRID_W)).reshape(-1)
    col = jnp.broadcast_to(jnp.arange(GRID_W, dtype=F32)[None, :], (rows, GRID_W)).reshape(-1)
    n_freq = RET_DK // 4
    inv = ROPE_BASE ** (-jnp.arange(n_freq, dtype=F32) / n_freq)
    ang = jnp.concatenate([row[:, None] * inv, col[:, None] * inv], axis=-1)
    cos, sin = jnp.cos(ang), jnp.sin(ang)
    cos_full = jnp.tile(jnp.concatenate([cos, cos], axis=-1), (1, RET_HEADS))
    sin_signed = jnp.tile(jnp.concatenate([-sin, sin], axis=-1), (1, RET_HEADS))
    return cos_full, sin_signed


def kernel(x, c, ctx, c_ctx, w_mod, b_mod, norm_mix, norm_mlp, w_mlp_in, w_mlp_out, ab_w_in, ab_w_out, ret_logit, s5_a_re, s5_a_im, s5_log_dt, s5_b_re, s5_b_im, s5_c_re, s5_c_im, s5_d, s5_w_glu, s5_b_glu, hg_w_in, hg_w_out, hg_lb_logits, hg_norm, norm_final):
    b, n, d = x.shape
    nc = ctx.shape[1]
    assert d == D_MODEL and b + 1 <= MOD_ROWS and w_mod.shape[0] == DEPTH == 2
    assert n % 512 == 0 and nc % 256 == 0 and n % GRID_W == 0
    ctx_row = b
    tm_l, tm_c = 512, 256

    cc = jnp.zeros((MOD_ROWS, d), F32).at[:b].set(c).at[b].set(c_ctx)
    mod = _adaln(cc, w_mod, b_mod)

    row2 = lambda a: a.reshape(1, -1)
    w_in0 = ab_w_in[0].astype(BF16)
    cos, sin = _rope_tables(n)
    ng0 = row2(norm_mix[0])
    q_l, k_l, v_l, u_l, g_l = _inproj0(x, mod, 0, ng0, w_in0, cos, sin, None, tm_l)
    q_c, k_c, v_c, u_c, g_c = _inproj0(ctx, mod, 0, ng0, w_in0, None, None, ctx_row, tm_c)

    log_gamma = jax.nn.log_sigmoid(ret_logit[0].astype(F32))
    lg_rows = jnp.broadcast_to(log_gamma.reshape(2 * RET_HEADS, 1), (2 * RET_HEADS, 2 * RET_DK))
    r_l, r_c = _retention(lg_rows, q_l, k_l, v_l, g_l, q_c, k_c, v_c, g_c)

    s5_ops = _s5_prep(s5_a_re[0], s5_a_im[0], s5_log_dt[0], s5_b_re[0], s5_b_im[0], s5_c_re[0], s5_c_im[0])
    y5_l, y5_c = _s5(s5_ops, u_l, u_c)

    s5p = (row2(s5_d[0]), s5_w_glu[0].astype(BF16), row2(s5_b_glu[0]))
    wo0 = ab_w_out[0].astype(BF16)
    w1_0, w2_0 = w_mlp_in[0].astype(BF16), w_mlp_out[0].astype(BF16)
    nm0 = row2(norm_mlp[0])
    h_l = _mix_mlp(x, r_l, y5_l, u_l, s5p, wo0, mod, 0, nm0, w1_0, w2_0, None, None, tm_l, 1024, "mix_mlp0_lat")
    h_c = _mix_mlp(ctx, r_c, y5_c, u_c, s5p, wo0, mod, 0, nm0, w1_0, w2_0, None, ctx_row, tm_c, 1024, "mix_mlp0_ctx")

    w_in1 = hg_w_in[0].astype(BF16)
    ng1 = row2(norm_mix[1])
    q1, ff_l, fb_l, i_l, g1 = _inproj1(h_l, mod, 1, ng1, w_in1, 0, (BF16, F32, F32, BF16, F32), None, tm_l,
                                       "inproj1_lat")
    ff_c, fb_c, i_c = _inproj1(h_c, mod, 1, ng1, w_in1, 1, (F32, F32, BF16), ctx_row, tm_c, "inproj1_ctx")
    o1 = _hgrn(hg_lb_logits, row2(hg_norm[0]), 1, q1, ff_l, fb_l, i_l, g1, ff_c, fb_c, i_c)
    return _mix_mlp(h_l, o1, None, None, None, hg_w_out[0].astype(BF16), mod, 1, row2(norm_mlp[1]),
                    w_mlp_in[1].astype(BF16), w_mlp_out[1].astype(BF16), row2(norm_final), None, tm_l, 1024,
                    "mix_mlp1_lat")
```

```python
import functools

import jax
import jax.numpy as jnp
from jax import lax
from jax.experimental import pallas as pl
from jax.experimental.pallas import tpu as pltpu

F32 = jnp.float32
BF16 = jnp.bfloat16

D_MODEL = 1024
DEPTH = 2
GRID_W = 64
EPS = 1e-6
ROPE_BASE = 10000.0
N_MOD = 6
RET_HEADS = 4
RET_DK = 64
RET_DV = 128
RET_QK = RET_HEADS * RET_DK
RET_WIDTH = RET_HEADS * RET_DV
RET_CHUNK = 128
S5_WIDTH = D_MODEL - RET_WIDTH
S5_GROUP = 16
S5_GROUPS = S5_WIDTH // S5_GROUP
S5_STATE = 64
S5_CHUNK = 16
S5_ROW = S5_CHUNK * S5_GROUP
LANE = 128
S5_LB = S5_WIDTH // LANE
S5_GPB = LANE // S5_GROUP
S5_BIG = S5_CHUNK * LANE
S5_HALF = S5_GPB * 2 * S5_STATE
S5_POW = 32
AB_IN = 2 * RET_QK + 2 * RET_WIDTH + S5_WIDTH
HG_HEADS = 8
HG_DK = D_MODEL // HG_HEADS
HG_BLOCK = 64
HG_GROUP = 4
HG_SPLIT = 2
D_FF = 4 * D_MODEL
MOD_ROWS = 16

VMEM_LIMIT_BYTES = 56 * 1024 * 1024


def _cp(*sem):
    return pltpu.CompilerParams(dimension_semantics=sem, vmem_limit_bytes=VMEM_LIMIT_BYTES)


def _dot(a, b):
    return jnp.dot(a, b, preferred_element_type=F32)


def _dot_nt(a, b):
    return lax.dot_general(a, b, (((1,), (1,)), ((), ())), preferred_element_type=F32)


def _dot_tn(a, b):
    return lax.dot_general(a, b, (((0,), (0,)), ((), ())), preferred_element_type=F32)


def _rms(x):
    return x * lax.rsqrt(jnp.mean(x * x, axis=-1, keepdims=True) + EPS)


def _mod_chunk(mod_ref, row, i):
    return mod_ref[0, pl.ds(row, 1), i * D_MODEL:(i + 1) * D_MODEL]


def _adaln_kernel(cc_ref, w_ref, b_ref, o_ref):
    s = jax.nn.silu(cc_ref[...]).astype(BF16)
    o_ref[0] = _dot(s, w_ref[0].astype(BF16)) + b_ref[0]


def _adaln(cc, w_mod, b_mod):
    bn = 1536
    n = N_MOD * D_MODEL
    return pl.pallas_call(
        _adaln_kernel,
        out_shape=jax.ShapeDtypeStruct((DEPTH, MOD_ROWS, n), F32),
        grid=(DEPTH, n // bn),
        in_specs=[
            pl.BlockSpec((MOD_ROWS, D_MODEL), lambda l, j: (0, 0)),
            pl.BlockSpec((1, D_MODEL, bn), lambda l, j: (l, 0, j)),
            pl.BlockSpec((1, 1, bn), lambda l, j: (l, 0, j)),
        ],
        out_specs=pl.BlockSpec((1, MOD_ROWS, bn), lambda l, j: (l, 0, j)),
        compiler_params=_cp("parallel", "parallel"),
        name="adaln",
    )(cc, w_mod, b_mod.reshape(DEPTH, 1, n))


def _rope(t, cos, sin):
    lane = lax.broadcasted_iota(jnp.int32, t.shape, 1)
    first = (lane & (RET_DK // 2)) == 0
    w = t.shape[1]
    swapped = jnp.where(first, pltpu.roll(t, w - RET_DK // 2, 1), pltpu.roll(t, RET_DK // 2, 1))
    return t * cos + swapped * sin


def _inproj0_kernel(*refs, mod_row, rope):
    if rope:
        h_ref, mod_ref, ng_ref, w_ref, cos_ref, sin_ref, q_ref, k_ref, v_ref, u_ref, g_ref = refs
    else:
        h_ref, mod_ref, ng_ref, w_ref, q_ref, k_ref, v_ref, u_ref, g_ref = refs
    row = pl.program_id(0) if mod_row is None else mod_row
    xn = _rms(h_ref[0]) * ng_ref[...]
    xm = (xn * (1.0 + _mod_chunk(mod_ref, row, 1)) + _mod_chunk(mod_ref, row, 0)).astype(BF16)
    y = _dot(xm, w_ref[...])
    q = y[:, 0:RET_QK]
    k = y[:, RET_QK:2 * RET_QK]
    if rope:
        q = _rope(q, cos_ref[...], sin_ref[...])
        k = _rope(k, cos_ref[...], sin_ref[...])
    q_ref[0] = q.astype(BF16)
    k_ref[0] = (k * (RET_DK ** -0.5)).astype(BF16)
    c0 = 2 * RET_QK
    v_ref[0] = y[:, c0:c0 + RET_WIDTH].astype(BF16)
    for j in range(S5_LB):
        lo = c0 + RET_WIDTH + j * LANE
        u_ref[0, j] = y[:, lo:lo + LANE]
    g_ref[0] = y[:, c0 + RET_WIDTH + S5_WIDTH:]


def _inproj0(h, mod, layer, ng, w, cos, sin, mod_row, tm):
    b, n, _ = h.shape
    rope = cos is not None
    row_spec = lambda width: pl.BlockSpec((1, tm, width), lambda i, j: (i, j, 0))
    in_specs = [
        row_spec(D_MODEL),
        pl.BlockSpec((1, MOD_ROWS, N_MOD * D_MODEL), lambda i, j: (layer, 0, 0)),
        pl.BlockSpec((1, D_MODEL), lambda i, j: (0, 0)),
        pl.BlockSpec((D_MODEL, AB_IN), lambda i, j: (0, 0)),
    ]
    args = [h, mod, ng, w]
    if rope:
        in_specs += [pl.BlockSpec((tm, RET_QK), lambda i, j: (j, 0))] * 2
        args += [cos, sin]
    widths = (RET_QK, RET_QK, RET_WIDTH, S5_WIDTH, RET_WIDTH)
    dtypes = (BF16, BF16, BF16, F32, F32)
    out_shape = [jax.ShapeDtypeStruct((b, n, wd), dt) for wd, dt in zip(widths, dtypes)]
    out_specs = [row_spec(wd) for wd in widths]
    out_shape[3] = jax.ShapeDtypeStruct((b, S5_LB, n, LANE), F32)
    out_specs[3] = pl.BlockSpec((1, S5_LB, tm, LANE), lambda i, j: (i, 0, j, 0))
    return pl.pallas_call(
        functools.partial(_inproj0_kernel, mod_row=mod_row, rope=rope),
        out_shape=out_shape,
        grid=(b, n // tm),
        in_specs=in_specs,
        out_specs=out_specs,
        compiler_params=_cp("parallel", "parallel"),
        name="inproj0_lat" if rope else "inproj0_ctx",
    )(*args)


def _ret_kernel(lg_ref, ql, kl, vl, gl, qc, kc, vc, gc, rl, rc, st_s, *, ncl, ncc, unroll):
    c = RET_CHUNK
    dk2 = 2 * RET_DK
    nt = ncc + ncl
    p = pl.program_id(1)
    h_a = 2 * p
    lgf_a = lg_ref[pl.ds(h_a, 1), :]
    lgf_b = lg_ref[pl.ds(h_a + 1, 1), :]
    lgb_a = lg_ref[pl.ds(RET_HEADS + h_a, 1), :]
    lgb_b = lg_ref[pl.ds(RET_HEADS + h_a + 1, 1), :]
    lane = lax.broadcasted_iota(jnp.int32, (1, 2 * RET_DK), 1)
    is_a = lane < RET_DK
    lgf_lane = jnp.where(is_a, lgf_a, lgf_b)
    lgb_lane = jnp.where(is_a, lgb_a, lgb_b)
    ri = lax.broadcasted_iota(jnp.int32, (c, c), 0).astype(F32)
    ci = lax.broadcasted_iota(jnp.int32, (c, c), 1).astype(F32)
    diff = ri - ci

    def dmat(lgf, lgb):
        fwd = jnp.exp(jnp.maximum(diff, 0.0) * lgf)
        bwd = jnp.exp(jnp.maximum(-diff, 0.0) * lgb)
        return jnp.where(diff > 0, fwd, jnp.where(diff < 0, bwd, 2.0))

    d_a = dmat(lgf_a, lgb_a)
    d_b = dmat(lgf_b, lgb_b)
    rowp = lax.broadcasted_iota(jnp.int32, (c, dk2), 0).astype(F32)
    qd = jnp.concatenate([jnp.exp((rowp + 1.0) * lgf_lane), jnp.exp((c - rowp) * lgb_lane)], axis=1)
    kd = jnp.concatenate([jnp.exp((c - 1.0 - rowp) * lgf_lane), jnp.exp(rowp * lgb_lane)], axis=1)
    rowk = lax.broadcasted_iota(jnp.int32, (dk2, 2 * RET_DV), 0)
    cd_f = jnp.exp(c * jnp.where(rowk < RET_DK, lgf_a[:, :1], lgf_b[:, :1]))
    cd_b = jnp.exp(c * jnp.where(rowk < RET_DK, lgb_a[:, :1], lgb_b[:, :1]))
    mask2 = jnp.concatenate([is_a, is_a], axis=1)

    def increment(k, v, slot):
        kk = jnp.concatenate([k, k], axis=1).astype(F32) * kd
        st_s[slot] = _dot_tn(kk.astype(BF16), v)

    for n in range(ncc):
        increment(kc[0, n * c:(n + 1) * c, :], vc[0, n * c:(n + 1) * c, :], n)

    def inc_body(i, carry):
        for j in range(unroll):
            n = i * unroll + j
            sl = pl.ds(pl.multiple_of(n * c, c), c)
            increment(kl[0, sl, :], vl[0, sl, :], ncc + n)
        return carry

    lax.fori_loop(0, ncl // unroll, inc_body, 0)

    def rec_body(t, carry):
        sf, sb = carry
        u = jnp.where(t < ncc, ncc - 1 - t, nt - 1 - (t - ncc))
        inc_f = st_s[t, 0:dk2, :]
        inc_b = st_s[u, dk2:2 * dk2, :]
        st_s[t, 0:dk2, :] = sf
        st_s[u, dk2:2 * dk2, :] = sb
        return cd_f * sf + inc_f, cd_b * sb + inc_b

    zero = jnp.zeros((dk2, 2 * RET_DV), F32)
    lax.fori_loop(0, nt, rec_body, (zero, zero))

    def output(q, k, v, g, slot, out_ref, st):
        q2 = (jnp.concatenate([q, q], axis=1).astype(F32) * qd).astype(BF16)
        s_n = st_s[slot].astype(BF16)
        for keep_a, dm, cs in ((True, d_a, 0), (False, d_b, RET_DV)):
            m1 = is_a if keep_a else jnp.logical_not(is_a)
            m2 = mask2 if keep_a else jnp.logical_not(mask2)
            att = _dot_nt(jnp.where(m1, q, jnp.zeros_like(q)), k) * dm
            o = (_dot(att.astype(BF16), v[:, cs:cs + RET_DV])
                 + _dot(jnp.where(m2, q2, jnp.zeros_like(q2)), s_n[:, cs:cs + RET_DV]))
            o = _rms(o) * jax.nn.silu(g[:, cs:cs + RET_DV])
            out_ref[0, pl.ds(st, c), cs:cs + RET_DV] = o.astype(BF16)

    for n in range(ncc):
        sl = slice(n * c, (n + 1) * c)
        output(qc[0, sl, :], kc[0, sl, :], vc[0, sl, :], gc[0, sl, :], n, rc, n * c)

    def out_body(i, carry):
        for j in range(unroll):
            n = i * unroll + j
            st = pl.multiple_of(n * c, c)
            sl = pl.ds(st, c)
            output(ql[0, sl, :], kl[0, sl, :], vl[0, sl, :], gl[0, sl, :], ncc + n, rl, st)
        return carry

    lax.fori_loop(0, ncl // unroll, out_body, 0)


def _retention(lg_rows, q_l, k_l, v_l, g_l, q_c, k_c, v_c, g_c):
    b, n, _ = q_l.shape
    nc = q_c.shape[1]
    ncl, ncc = n // RET_CHUNK, nc // RET_CHUNK
    pairs = RET_HEADS // 2
    unroll = 8 if ncl % 8 == 0 else 1

    def spec(rows, width):
        return pl.BlockSpec((1, rows, width), lambda i, p: (i, 0, p))

    return pl.pallas_call(
        functools.partial(_ret_kernel, ncl=ncl, ncc=ncc, unroll=unroll),
        out_shape=[jax.ShapeDtypeStruct((b, n, RET_WIDTH), BF16),
                   jax.ShapeDtypeStruct((b, nc, RET_WIDTH), BF16)],
        grid=(b, pairs),
        in_specs=[pl.BlockSpec((2 * RET_HEADS, 2 * RET_DK), lambda i, p: (0, 0)),
                  spec(n, 2 * RET_DK), spec(n, 2 * RET_DK), spec(n, 2 * RET_DV), spec(n, 2 * RET_DV),
                  spec(nc, 2 * RET_DK), spec(nc, 2 * RET_DK), spec(nc, 2 * RET_DV), spec(nc, 2 * RET_DV)],
        out_specs=[spec(n, 2 * RET_DV), spec(nc, 2 * RET_DV)],
        scratch_shapes=[pltpu.VMEM((ncl + ncc, 4 * RET_DK, 2 * RET_DV), F32)],
        compiler_params=_cp("parallel", "parallel"),
        name="retention",
    )(lg_rows, q_l, k_l, v_l, g_l, q_c, k_c, v_c, g_c)


def _dot_hi(a, b, contract=(1, 0)):
    dims = (((contract[0],), (contract[1],)), ((), ()))
    return lax.dot_general(a, b, dims, preferred_element_type=F32, precision=lax.Precision.HIGHEST)


def _s5_prep_kernel(ar_row, ai_row, ldt, btr, bti, ctr, cti, rm, rwin, rwout, abig):
    t, g, p, kp = S5_CHUNK, S5_GROUP, S5_STATE, S5_POW
    row = S5_ROW
    i0 = lambda shape: lax.broadcasted_iota(jnp.int32, shape, 0)
    i1 = lambda shape: lax.broadcasted_iota(jnp.int32, shape, 1)
    f32 = lambda m: jnp.where(m, 1.0, 0.0).astype(F32)
    s_of_r = i0((row, kp)) // g
    k_of_l = i1((row, kp))
    sel_rows = (f32(k_of_l == t - 1 - s_of_r), f32(k_of_l == s_of_r))
    t_of_c = i1((kp, row)) // g
    k_of_s = i0((kp, row))
    sel_out = (f32(k_of_s == t_of_c + 1), f32(k_of_s == t - t_of_c))
    sel_lag = (f32(k_of_s == t_of_c), f32(k_of_s == t - 1 - t_of_c))
    tile_l = f32(i1((g, row)) % g == i0((g, row)))
    tile_r = f32(i0((row, g)) % g == i1((row, g)))
    lane = i1((g, row))
    k_col = i0((kp, 1)).astype(F32)
    k_row = i1((1, kp)).astype(F32)
    first = i0((8, 1)) == 0

    def outer(a, k):
        a8 = jnp.where(first, jnp.broadcast_to(a, (8, a.shape[1])), 0.0)
        return _dot_hi(a8, jnp.broadcast_to(k, (8, k.shape[1])), (0, 0))

    for q in range(S5_GPB):
        lags = []
        for d in range(2):
            dt = jnp.exp(ldt[d, q])
            are_r, aim_r = ar_row[d, q], ai_row[d, q]
            mag = jnp.exp(are_r * dt)
            ang = aim_r * dt
            nr, ni = mag * jnp.cos(ang) - 1.0, mag * jnp.sin(ang)
            den = jnp.square(are_r) + jnp.square(aim_r)
            fr = (nr * are_r + ni * aim_r) / den
            fi = (ni * are_r - nr * aim_r) / den
            pm = jnp.exp(k_col * (are_r * dt))
            pa = k_col * ang
            pk_re, pk_im = pm * jnp.cos(pa), pm * jnp.sin(pa)
            pmt = jnp.exp(outer(are_r * dt, k_row))
            pat = outer(ang, k_row)
            pt_re, pt_im = pmt * jnp.cos(pat), pmt * jnp.sin(pat)
            bt_re, bt_im = _dot_hi(tile_r, btr[d, q], (1, 1)), _dot_hi(tile_r, bti[d, q], (1, 1))
            bb_re = fr * bt_re - fi * bt_im
            bb_im = fr * bt_im + fi * bt_re
            pr_re, pr_im = _dot_hi(sel_rows[d], pk_re), _dot_hi(sel_rows[d], pk_im)
            w_re = pr_re * bb_re - pr_im * bb_im
            w_im = pr_re * bb_im + pr_im * bb_re
            for s in range(t):
                rows = slice(s * LANE + q * g, s * LANE + (q + 1) * g)
                rwin[0, rows, d * p:(d + 1) * p] = w_re[s * g:(s + 1) * g].astype(BF16)
                rwin[0, rows, (2 + d) * p:(3 + d) * p] = w_im[s * g:(s + 1) * g].astype(BF16)
            ct_re, ct_im = _dot_hi(ctr[d, q], tile_l, (0, 0)), _dot_hi(cti[d, q], tile_l, (0, 0))

            def c_pow(sel):
                pc_re, pc_im = _dot_hi(pt_re, sel), _dot_hi(pt_im, sel)
                return ct_re * pc_re - ct_im * pc_im, ct_re * pc_im + ct_im * pc_re

            o_re, o_im = c_pow(sel_out[d])
            r0 = d * S5_GPB * p + q * p
            rwout[0, r0:r0 + p, :] = o_re.astype(BF16)
            rwout[0, S5_HALF + r0:S5_HALF + r0 + p, :] = (-o_im).astype(BF16)
            l_re, l_im = c_pow(sel_lag[d])
            lags.append(_dot_hi(bb_re[0:g], l_re) - _dot_hi(bb_im[0:g], l_im))
            abig[0, 0:1, r0:r0 + p] = pk_re[t:t + 1, :]
            abig[0, 1:2, r0:r0 + p] = pk_im[t:t + 1, :]
        for s in range(t):
            fwd = jnp.where(lane >= g * s, pltpu.roll(lags[0], g * s, 1), 0.0)
            bwd = jnp.where(lane < g * (s + 1), pltpu.roll(lags[1], (row - g * (t - 1 - s)) % row, 1), 0.0)
            rm[0, s * LANE + q * g:s * LANE + (q + 1) * g, :] = (fwd + bwd).astype(BF16)


def _s5_prep(a_re, a_im, log_dt, b_re, b_im, c_re, c_im):
    gg, p, g = S5_GROUPS, S5_STATE, S5_GROUP
    f = lambda x: x.astype(F32)
    args = (f(a_re).reshape(2, gg, 1, p), f(a_im).reshape(2, gg, 1, p),
            f(log_dt).reshape(2, gg, 1, 1),
            f(b_re), f(b_im), f(c_re), f(c_im))
    spec = lambda r, c: pl.BlockSpec((2, S5_GPB, r, c), lambda j: (0, j, 0, 0))
    out = lambda r, c: pl.BlockSpec((1, r, c), lambda j: (j, 0, 0))
    return pl.pallas_call(
        _s5_prep_kernel,
        out_shape=[jax.ShapeDtypeStruct((S5_LB, S5_BIG, S5_ROW), BF16),
                   jax.ShapeDtypeStruct((S5_LB, S5_BIG, S5_ROW), BF16),
                   jax.ShapeDtypeStruct((S5_LB, 2 * S5_HALF, S5_ROW), BF16),
                   jax.ShapeDtypeStruct((S5_LB, 2, S5_HALF), F32)],
        grid=(S5_LB,),
        in_specs=[spec(1, p), spec(1, p), spec(1, 1),
                  spec(p, g), spec(p, g), spec(g, p), spec(g, p)],
        out_specs=[out(S5_BIG, S5_ROW), out(S5_BIG, S5_ROW), out(2 * S5_HALF, S5_ROW), out(2, S5_HALF)],
        compiler_params=_cp("parallel"),
        name="s5_prep",
    )(*args)


def _s5_expand(r_ref, col_unit, col_block, row_unit):
    n = S5_BIG
    a = lax.broadcasted_iota(jnp.int32, (S5_ROW, n), 0)
    c = lax.broadcasted_iota(jnp.int32, (S5_ROW, n), 1)
    e = jnp.where((a // col_unit == c // col_block) & (a % col_unit == c % col_unit), 1.0, 0.0).astype(BF16)
    x = _dot(r_ref[0], e)
    rq = (lax.broadcasted_iota(jnp.int32, (n, n), 0) // row_unit) % S5_GPB
    cq = (lax.broadcasted_iota(jnp.int32, (n, n), 1) // col_unit) % S5_GPB
    return jnp.where(rq == cq, x, 0.0).astype(BF16)


def _s5_kernel(ul, uc, rm_ref, rwin_ref, rwout_ref, a_ref, yl, yc, x_s, s_s, m_s, win_s, wout_s, *, ncl, ncc):
    hw = S5_HALF
    hh = hw // 2

    @pl.when(pl.program_id(1) == 0)
    def _():
        m_s[...] = _s5_expand(rm_ref, S5_GROUP, LANE, S5_GROUP)
        win_s[...] = _s5_expand(rwin_ref, S5_STATE, S5_GPB * S5_STATE, S5_GROUP)
        wout_s[...] = _s5_expand(rwout_ref, S5_GROUP, LANE, S5_STATE)

    zc = uc[0, 0].astype(BF16)
    zl = ul[0, 0].astype(BF16)
    x_s[0:ncc, :] = _dot(zc, win_s[...])
    x_s[ncc:ncc + ncl, :] = _dot(zl, win_s[...])
    a_re = a_ref[0, 0:1, :]
    a_im = a_ref[0, 1:2, :]

    def segment(base, n, carry):
        s_re, s_im = carry
        for i in range(n):
            rf = slice(base + i, base + i + 1)
            rb = slice(base + n - 1 - i, base + n - i)
            s_s[rf, 0:hh] = s_re[:, 0:hh]
            s_s[rb, hh:hw] = s_re[:, hh:hw]
            s_s[rf, hw:hw + hh] = s_im[:, 0:hh]
            s_s[rb, hw + hh:2 * hw] = s_im[:, hh:hw]
            x_re = jnp.concatenate([x_s[rf, 0:hh], x_s[rb, hh:hw]], axis=-1)
            x_im = jnp.concatenate([x_s[rf, hw:hw + hh], x_s[rb, hw + hh:2 * hw]], axis=-1)
            s_re, s_im = a_re * s_re - a_im * s_im + x_re, a_re * s_im + a_im * s_re + x_im
        return s_re, s_im

    zero = jnp.zeros((1, hw), F32)
    carry = segment(0, ncc, (zero, zero))
    segment(ncc, ncl, carry)
    sp = s_s[...].astype(BF16)
    yc[0, 0] = _dot(zc, m_s[...]) + _dot(sp[0:ncc], wout_s[...])
    yl[0, 0] = _dot(zl, m_s[...]) + _dot(sp[ncc:ncc + ncl], wout_s[...])


def _s5(prep, u_l, u_c):
    rm, rwin, rwout, a_big = prep
    b, _, n, _ = u_l.shape
    nc = u_c.shape[2]
    ncl, ncc = n // S5_CHUNK, nc // S5_CHUNK
    zl = u_l.reshape(b, S5_LB, ncl, S5_BIG)
    zc = u_c.reshape(b, S5_LB, ncc, S5_BIG)
    rows = lambda r: pl.BlockSpec((1, 1, r, S5_BIG), lambda j, i: (i, j, 0, 0))
    wspec = lambda r, c: pl.BlockSpec((1, r, c), lambda j, i: (j, 0, 0))
    big = pltpu.VMEM((S5_BIG, S5_BIG), BF16)
    yl, yc = pl.pallas_call(
        functools.partial(_s5_kernel, ncl=ncl, ncc=ncc),
        out_shape=[jax.ShapeDtypeStruct(zl.shape, F32), jax.ShapeDtypeStruct(zc.shape, F32)],
        grid=(S5_LB, b),
        in_specs=[rows(ncl), rows(ncc), wspec(S5_BIG, S5_ROW), wspec(S5_BIG, S5_ROW),
                  wspec(2 * S5_HALF, S5_ROW), wspec(2, S5_HALF)],
        out_specs=[rows(ncl), rows(ncc)],
        scratch_shapes=[pltpu.VMEM((ncc + ncl, 2 * S5_HALF), F32), pltpu.VMEM((ncc + ncl, 2 * S5_HALF), F32),
                        big, big, big],
        compiler_params=_cp("parallel", "arbitrary"),
        name="s5",
    )(zl, zc, rm, rwin, rwout, a_big)
    return yl.reshape(u_l.shape), yc.reshape(u_c.shape)


def _mix_mlp_kernel(*refs, mod_row, s5_merge, final_norm, fb):
    if s5_merge:
        (h_ref, r_ref, y5_ref, u_ref, ds_ref, wg_ref, bg_ref, wo_ref, mod_ref, nm_ref, w1_ref, w2_ref,
         *rest) = refs
    else:
        h_ref, r_ref, wo_ref, mod_ref, nm_ref, w1_ref, w2_ref, *rest = refs
    if final_norm:
        nf_ref, o_ref = rest
    else:
        (o_ref,) = rest
    row = pl.program_id(0) if mod_row is None else mod_row
    if s5_merge:
        y5 = jnp.concatenate([y5_ref[0, lb] for lb in range(S5_LB)], axis=-1)
        u = jnp.concatenate([u_ref[0, lb] for lb in range(S5_LB)], axis=-1)
        y = jax.nn.gelu(y5 + ds_ref[...] * u)
        y = y * jax.nn.sigmoid(_dot(y.astype(BF16), wg_ref[...]) + bg_ref[...])
        mix = _dot(r_ref[0], wo_ref[0:RET_WIDTH, :]) + _dot(y.astype(BF16), wo_ref[RET_WIDTH:D_MODEL, :])
    else:
        mix = _dot(r_ref[0], wo_ref[...])
    h1 = h_ref[0] + _mod_chunk(mod_ref, row, 2) * mix
    xn = _rms(h1) * nm_ref[...]
    xm = (xn * (1.0 + _mod_chunk(mod_ref, row, 4)) + _mod_chunk(mod_ref, row, 3)).astype(BF16)
    acc = None
    for j in range(D_FF // fb):
        a = jnp.square(jnp.maximum(_dot(xm, w1_ref[:, j * fb:(j + 1) * fb]), 0.0)).astype(BF16)
        part = _dot(a, w2_ref[j * fb:(j + 1) * fb, :])
        acc = part if acc is None else acc + part
    h2 = h1 + _mod_chunk(mod_ref, row, 5) * acc
    if final_norm:
        h2 = _rms(h2) * nf_ref[...]
    o_ref[0] = h2


def _mix_mlp(h, r, s5y, u, s5p, wo, mod, layer, nm, w1, w2, nf, mod_row, tm, fb, name):
    b, n, _ = h.shape
    s5_merge = s5y is not None
    final_norm = nf is not None
    one = pl.Buffered(1)
    row_spec = lambda width: pl.BlockSpec((1, tm, width), lambda i, t: (i, t, 0))
    const = lambda shape: pl.BlockSpec(shape, lambda i, t: (0,) * len(shape), pipeline_mode=one)
    in_specs = [row_spec(D_MODEL), row_spec(r.shape[-1])]
    args = [h, r]
    if s5_merge:
        d_skip, w_glu, b_glu = s5p
        lb_spec = pl.BlockSpec((1, S5_LB, tm, LANE), lambda i, t: (i, 0, t, 0))
        in_specs += [lb_spec, lb_spec, const((1, S5_WIDTH)),
                     const((S5_WIDTH, S5_WIDTH)), const((1, S5_WIDTH))]
        args += [s5y, u, d_skip, w_glu, b_glu]
    in_specs += [const((D_MODEL, D_MODEL)),
                 pl.BlockSpec((1, MOD_ROWS, N_MOD * D_MODEL), lambda i, t: (layer, 0, 0), pipeline_mode=one),
                 const((1, D_MODEL)), const((D_MODEL, D_FF)), const((D_FF, D_MODEL))]
    args += [wo, mod, nm, w1, w2]
    if final_norm:
        in_specs.append(const((1, D_MODEL)))
        args.append(nf)
    return pl.pallas_call(
        functools.partial(_mix_mlp_kernel, mod_row=mod_row, s5_merge=s5_merge, final_norm=final_norm, fb=fb),
        out_shape=jax.ShapeDtypeStruct((b, n, D_MODEL), F32),
        grid=(b, n // tm),
        in_specs=in_specs,
        out_specs=row_spec(D_MODEL),
        compiler_params=_cp("parallel", "parallel"),
        name=name,
    )(*args)


def _inproj1_kernel(h_ref, mod_ref, ng_ref, w_ref, *out_refs, mod_row, col0):
    row = pl.program_id(0) if mod_row is None else mod_row
    xn = _rms(h_ref[0]) * ng_ref[...]
    xm = (xn * (1.0 + _mod_chunk(mod_ref, row, 1)) + _mod_chunk(mod_ref, row, 0)).astype(BF16)
    for idx, o_ref in enumerate(out_refs):
        lo = (col0 + idx) * D_MODEL
        o_ref[0] = _dot(xm, w_ref[:, lo:lo + D_MODEL]).astype(o_ref.dtype)


def _inproj1(h, mod, layer, ng, w, col0, dtypes, mod_row, tm, name):
    b, n, _ = h.shape
    one = pl.Buffered(1)
    row_spec = pl.BlockSpec((1, tm, D_MODEL), lambda i, t: (i, t, 0))
    return pl.pallas_call(
        functools.partial(_inproj1_kernel, mod_row=mod_row, col0=col0),
        out_shape=[jax.ShapeDtypeStruct((b, n, D_MODEL), dt) for dt in dtypes],
        grid=(b, n // tm),
        in_specs=[row_spec,
                  pl.BlockSpec((1, MOD_ROWS, N_MOD * D_MODEL), lambda i, t: (layer, 0, 0), pipeline_mode=one),
                  pl.BlockSpec((1, D_MODEL), lambda i, t: (0, 0), pipeline_mode=one),
                  pl.BlockSpec(w.shape, lambda i, t: (0, 0), pipeline_mode=one)],
        out_specs=[row_spec] * len(dtypes),
        compiler_params=_cp("parallel", "parallel"),
        name=name,
    )(h, mod, ng, w)


def _cumsum_mm(tri, x):
    acc = None
    r = x
    for i in range(HG_SPLIT):
        p = r.astype(BF16)
        acc = _dot(tri, p) if acc is None else acc + _dot(tri, p)
        if i + 1 < HG_SPLIT:
            r = r - p.astype(F32)
    return acc


def _hgrn_kernel(lbl_ref, ng_ref, ql, ffl, fbl, il, gl, ffc, fbc, ic, o_ref,
                 qin_s, att_s, kv_s, et_s, kvc_s, etc_s, cum_s, kk_s, *, nbl, nbc, layer, unroll, out_blocks):
    cb = HG_BLOCK
    mid = cb // 2
    gb = HG_GROUP
    gr = gb * cb
    dk = HG_DK

    def lower_bound(d):
        z = [lbl_ref[d, k:k + 1, :] for k in range(DEPTH)]
        zmax = functools.reduce(jnp.maximum, z)
        e = [jnp.exp(v - zmax) for v in z]
        tot = functools.reduce(lambda a, b_: a + b_, e)
        lb = jnp.zeros_like(tot)
        for k in range(1, layer + 1):
            lb = lb + e[k] / tot
        return lb

    ri = lax.broadcasted_iota(jnp.int32, (gr, gr), 0)
    ci = lax.broadcasted_iota(jnp.int32, (gr, gr), 1)
    same = (ri // cb) == (ci // cb)
    rb = lax.broadcasted_iota(jnp.int32, (cb, cb), 0)
    cbi = lax.broadcasted_iota(jnp.int32, (cb, cb), 1)
    tri_l = jnp.where(same & (ri >= ci), 1.0, 0.0).astype(BF16)
    dirs = ((0, lower_bound(0), rb >= cbi, mid - 1, cb - 1), (1, lower_bound(1), rb <= cbi, mid, 0))

    def gates(raws, slot):
        s = [jax.nn.sigmoid(r) for r in raws]
        t = [(1.0 - dr[1]) * si for dr, si in zip(dirs, s)]
        log_f = [jnp.log(dr[1] + ti) for dr, ti in zip(dirs, t)]
        pre = _cumsum_mm(tri_l, jnp.concatenate(log_f, axis=-1))
        pre_b = pre[:, dk:].reshape(gb, cb, dk)
        cum_s[slot, 0] = pre[:, :dk]
        cum_s[slot, 1] = (pre_b[:, cb - 1:cb, :] - pre_b).reshape(gr, dk) + log_f[1]
        for d in range(2):
            kk_s[slot, d] = (1.0 - dirs[d][1]) - t[d]

    def products(slot, v, q, n0, kv_ref, et_ref):
        v3 = v.reshape(gb, cb, dk)
        qts, kts = [], []
        for d, lb, keep, ref_row, tot_row in dirs:
            cum = cum_s[slot, d].reshape(gb, cb, dk)
            kk = kk_s[slot, d].reshape(gb, cb, dk)
            ref = cum[:, ref_row:ref_row + 1, :]
            tot = cum[:, tot_row:tot_row + 1, :]
            e = cum - ref
            kt = kk * jnp.exp(-e)
            k_out = (kt * jnp.exp(tot - ref)).astype(BF16)
            e_tot = jnp.exp(tot)
            if q is not None:
                qt = q.astype(F32).reshape(gb, cb, dk) * jnp.exp(e)
                qin_s[pl.ds(pl.multiple_of(n0 * cb, gr), gr), d * dk:(d + 1) * dk] = (
                    (qt * jnp.exp(ref)).astype(BF16).reshape(gr, dk))
                qts.append(qt.astype(BF16))
                kts.append(kt.astype(BF16))
            for j in range(gb):
                kv_ref[n0 + j, :, d * dk:(d + 1) * dk] = _dot_tn(v3[j], k_out[j])
                et_ref[n0 + j, :, d * dk:(d + 1) * dk] = e_tot[j]
        if q is not None:
            for j in range(gb):
                att = (jnp.where(dirs[0][2], _dot_nt(qts[0][j], kts[0][j]), 0.0)
                       + jnp.where(dirs[1][2], _dot_nt(qts[1][j], kts[1][j]), 0.0))
                att_s[n0 + j] = att.astype(BF16)

    for g in range(nbc // gb):
        sl = slice(g * gr, (g + 1) * gr)
        gates((ffc[0, sl, :], fbc[0, sl, :]), 0)
        products(0, ic[0, sl, :], None, g * gb, kvc_s, etc_s)

    ngl = nbl // gb

    def lat_gates(g, slot):
        sl = pl.ds(pl.multiple_of(g * gr, gr), gr)
        gates((ffl[0, sl, :], fbl[0, sl, :]), slot)

    def lat_products(g, slot):
        sl = pl.ds(pl.multiple_of(g * gr, gr), gr)
        products(slot, il[0, sl, :], ql[0, sl, :], g * gb, kv_s, et_s)

    lat_gates(0, 0)

    def prep_body(g, carry):
        slot = g % 2
        lat_products(g, slot)
        lat_gates(g + 1, 1 - slot)
        return carry

    lax.fori_loop(0, ngl - 1, prep_body, 0)
    lat_products(ngl - 1, (ngl - 1) % 2)

    lane = lax.broadcasted_iota(jnp.int32, (dk, 2 * dk), 1)
    is_f = lane < dk
    st = jnp.zeros((dk, 2 * dk), F32)
    for n in range(nbc):
        m = nbc - 1 - n
        st = (st * jnp.where(is_f[:1], etc_s[n], etc_s[m]) + jnp.where(is_f, kvc_s[n], kvc_s[m]))

    def rec_body(t, st):
        u = nbl - 1 - t
        inc = jnp.where(is_f, kv_s[t], kv_s[u])
        dec = jnp.where(is_f[:1], et_s[t], et_s[u])
        kv_s[t, :, 0:dk] = st[:, 0:dk]
        kv_s[u, :, dk:2 * dk] = st[:, dk:2 * dk]
        return st * dec + inc

    lax.fori_loop(0, nbl, rec_body, st, unroll=unroll)

    def out_body(i, carry):
        for j in range(out_blocks):
            n = i * out_blocks + j
            sl = pl.ds(pl.multiple_of(n * cb, cb), cb)
            o = _dot(att_s[n], il[0, sl, :]) + _dot_nt(qin_s[sl, :], kv_s[n].astype(BF16))
            o = _rms(o) * ng_ref[...] * jax.nn.silu(gl[0, sl, :])
            o_ref[0, sl, :] = o.astype(BF16)
        return carry

    lax.fori_loop(0, nbl // out_blocks, out_body, 0)


def _hgrn(lb_logits, norm_g, layer, q_l, ff_l, fb_l, i_l, g_l, ff_c, fb_c, i_c):
    b, n, _ = q_l.shape
    nc = ff_c.shape[1]
    nbl, nbc = n // HG_BLOCK, nc // HG_BLOCK
    out_blocks = min(16, nbl)
    assert nbl % HG_GROUP == 0 and nbc % HG_GROUP == 0 and nbl % out_blocks == 0
    spec = lambda rows: pl.BlockSpec((1, rows, HG_DK), lambda i, h: (i, 0, h))
    return pl.pallas_call(
        functools.partial(_hgrn_kernel, nbl=nbl, nbc=nbc, layer=layer, unroll=2, out_blocks=out_blocks),
        out_shape=jax.ShapeDtypeStruct((b, n, D_MODEL), BF16),
        grid=(b, HG_HEADS),
        in_specs=[pl.BlockSpec((2, DEPTH, HG_DK), lambda i, h: (0, 0, h)),
                  pl.BlockSpec((1, HG_DK), lambda i, h: (0, 0)),
                  spec(n), spec(n), spec(n), spec(n), spec(n), spec(nc), spec(nc), spec(nc)],
        out_specs=spec(n),
        scratch_shapes=[pltpu.VMEM((n, 2 * HG_DK), BF16),
                        pltpu.VMEM((nbl, HG_BLOCK, HG_BLOCK), BF16),
                        pltpu.VMEM((nbl, HG_DK, 2 * HG_DK), F32),
                        pltpu.VMEM((nbl, 1, 2 * HG_DK), F32),
                        pltpu.VMEM((nbc, HG_DK, 2 * HG_DK), F32),
                        pltpu.VMEM((nbc, 1, 2 * HG_DK), F32),
                        pltpu.VMEM((2, 2, HG_GROUP * HG_BLOCK, HG_DK), F32),
                        pltpu.VMEM((2, 2, HG_GROUP * HG_BLOCK, HG_DK), F32)],
        compiler_params=_cp("parallel", "parallel"),
        name="hgrn2",
    )(lb_logits, norm_g, q_l, ff_l, fb_l, i_l, g_l, ff_c, fb_c, i_c)


def _rope_tables(n_tok):
    rows = n_tok // GRID_W
    row = jnp.broadcast_to(jnp.arange(rows, dtype=F32)[:, None], (rows, GRID_W)).reshape(-1)
    col = jnp.broadcast_to(jnp.arange(GRID_W, dtype=F32)[None, :], (rows, GRID_W)).reshape(-1)
    n_freq = RET_DK // 4
    inv = ROPE_BASE ** (-jnp.arange(n_freq, dtype=F32) / n_freq)
    ang = jnp.concatenate([row[:, None] * inv, col[:, None] * inv], axis=-1)
    cos, sin = jnp.cos(ang), jnp.sin(ang)
    cos_full = jnp.tile(jnp.concatenate([cos, cos], axis=-1), (1, RET_HEADS))
    sin_signed = jnp.tile(jnp.concatenate([-sin, sin], axis=-1), (1, RET_HEADS))
    return cos_full, sin_signed


def kernel(x, c, ctx, c_ctx, w_mod, b_mod, norm_mix, norm_mlp, w_mlp_in, w_mlp_out, ab_w_in, ab_w_out, ret_logit, s5_a_re, s5_a_im, s5_log_dt, s5_b_re, s5_b_im, s5_c_re, s5_c_im, s5_d, s5_w_glu, s5_b_glu, hg_w_in, hg_w_out, hg_lb_logits, hg_norm, norm_final):
    b, n, d = x.shape
    nc = ctx.shape[1]
    assert d == D_MODEL and b + 1 <= MOD_ROWS and w_mod.shape[0] == DEPTH == 2
    assert n % 512 == 0 and nc % 256 == 0 and n % GRID_W == 0
    ctx_row = b
    tm_l, tm_c = 512, 256

    cc = jnp.zeros((MOD_ROWS, d), F32).at[:b].set(c).at[b].set(c_ctx)
    mod = _adaln(cc, w_mod, b_mod)

    row2 = lambda a: a.reshape(1, -1)
    w_in0 = ab_w_in[0].astype(BF16)
    cos, sin = _rope_tables(n)
    ng0 = row2(norm_mix[0])
    q_l, k_l, v_l, u_l, g_l = _inproj0(x, mod, 0, ng0, w_in0, cos, sin, None, tm_l)
    q_c, k_c, v_c, u_c, g_c = _inproj0(ctx, mod, 0, ng0, w_in0, None, None, ctx_row, tm_c)

    log_gamma = jax.nn.log_sigmoid(ret_logit[0].astype(F32))
    lg_rows = jnp.broadcast_to(log_gamma.reshape(2 * RET_HEADS, 1), (2 * RET_HEADS, 2 * RET_DK))
    r_l, r_c = _retention(lg_rows, q_l, k_l, v_l, g_l, q_c, k_c, v_c, g_c)

    s5_ops = _s5_prep(s5_a_re[0], s5_a_im[0], s5_log_dt[0], s5_b_re[0], s5_b_im[0], s5_c_re[0], s5_c_im[0])
    y5_l, y5_c = _s5(s5_ops, u_l, u_c)

    s5p = (row2(s5_d[0]), s5_w_glu[0].astype(BF16), row2(s5_b_glu[0]))
    wo0 = ab_w_out[0].astype(BF16)
    w1_0, w2_0 = w_mlp_in[0].astype(BF16), w_mlp_out[0].astype(BF16)
    nm0 = row2(norm_mlp[0])
    h_l = _mix_mlp(x, r_l, y5_l, u_l, s5p, wo0, mod, 0, nm0, w1_0, w2_0, None, None, tm_l, 1024, "mix_mlp0_lat")
    h_c = _mix_mlp(ctx, r_c, y5_c, u_c, s5p, wo0, mod, 0, nm0, w1_0, w2_0, None, ctx_row, tm_c, 1024, "mix_mlp0_ctx")

    w_in1 = hg_w_in[0].astype(BF16)
    ng1 = row2(norm_mix[1])
    q1, ff_l, fb_l, i_l, g1 = _inproj1(h_l, mod, 1, ng1, w_in1, 0, (BF16, F32, F32, BF16, F32), None, tm_l,
                                       "inproj1_lat")
    ff_c, fb_c, i_c = _inproj1(h_c, mod, 1, ng1, w_in1, 1, (F32, F32, BF16), ctx_row, tm_c, "inproj1_ctx")
    o1 = _hgrn(hg_lb_logits, row2(hg_norm[0]), 1, q1, ff_l, fb_l, i_l, g1, ff_c, fb_c, i_c)
    return _mix_mlp(h_l, o1, None, None, None, hg_w_out[0].astype(BF16), mod, 1, row2(norm_mlp[1]),
                    w_mlp_in[1].astype(BF16), w_mlp_out[1].astype(BF16), row2(norm_final), None, tm_l, 1024,
                    "mix_mlp1_lat")
```

```python
import functools

import jax
import jax.numpy as jnp
from jax import lax
from jax.experimental import pallas as pl
from jax.experimental.pallas import tpu as pltpu

F32 = jnp.float32
BF16 = jnp.bfloat16

D_MODEL = 1024
DEPTH = 2
GRID_W = 64
EPS = 1e-6
ROPE_BASE = 10000.0
N_MOD = 6
RET_HEADS = 4
RET_DK = 64
RET_DV = 128
RET_QK = RET_HEADS * RET_DK
RET_WIDTH = RET_HEADS * RET_DV
RET_CHUNK = 128
S5_WIDTH = D_MODEL - RET_WIDTH
S5_GROUP = 16
S5_GROUPS = S5_WIDTH // S5_GROUP
S5_STATE = 64
S5_CHUNK = 16
S5_ROW = S5_CHUNK * S5_GROUP
LANE = 128
S5_LB = S5_WIDTH // LANE
S5_GPB = LANE // S5_GROUP
S5_BIG = S5_CHUNK * LANE
S5_HALF = S5_GPB * 2 * S5_STATE
S5_POW = 32
AB_IN = 2 * RET_QK + 2 * RET_WIDTH + S5_WIDTH
HG_HEADS = 8
HG_DK = D_MODEL // HG_HEADS
HG_BLOCK = 64
HG_GROUP = 4
HG_SPLIT = 2
D_FF = 4 * D_MODEL
MOD_ROWS = 16

VMEM_LIMIT_BYTES = 56 * 1024 * 1024


def _cp(*sem):
    return pltpu.CompilerParams(dimension_semantics=sem, vmem_limit_bytes=VMEM_LIMIT_BYTES)


def _dot(a, b):
    return jnp.dot(a, b, preferred_element_type=F32)


def _dot_nt(a, b):
    return lax.dot_general(a, b, (((1,), (1,)), ((), ())), preferred_element_type=F32)


def _dot_tn(a, b):
    return lax.dot_general(a, b, (((0,), (0,)), ((), ())), preferred_element_type=F32)


def _rms(x):
    return x * lax.rsqrt(jnp.mean(x * x, axis=-1, keepdims=True) + EPS)


def _mod_chunk(mod_ref, row, i):
    return mod_ref[0, pl.ds(row, 1), i * D_MODEL:(i + 1) * D_MODEL]


def _adaln_kernel(cc_ref, w_ref, b_ref, o_ref):
    s = jax.nn.silu(cc_ref[...]).astype(BF16)
    o_ref[0] = _dot(s, w_ref[0].astype(BF16)) + b_ref[0]


def _adaln(cc, w_mod, b_mod):
    bn = 1536
    n = N_MOD * D_MODEL
    return pl.pallas_call(
        _adaln_kernel,
        out_shape=jax.ShapeDtypeStruct((DEPTH, MOD_ROWS, n), F32),
        grid=(DEPTH, n // bn),
        in_specs=[
            pl.BlockSpec((MOD_ROWS, D_MODEL), lambda l, j: (0, 0)),
            pl.BlockSpec((1, D_MODEL, bn), lambda l, j: (l, 0, j)),
            pl.BlockSpec((1, 1, bn), lambda l, j: (l, 0, j)),
        ],
        out_specs=pl.BlockSpec((1, MOD_ROWS, bn), lambda l, j: (l, 0, j)),
        compiler_params=_cp("parallel", "parallel"),
        name="adaln",
    )(cc, w_mod, b_mod.reshape(DEPTH, 1, n))


def _rope(t, cos, sin):
    lane = lax.broadcasted_iota(jnp.int32, t.shape, 1)
    first = (lane & (RET_DK // 2)) == 0
    w = t.shape[1]
    swapped = jnp.where(first, pltpu.roll(t, w - RET_DK // 2, 1), pltpu.roll(t, RET_DK // 2, 1))
    return t * cos + swapped * sin


def _inproj0_kernel(*refs, mod_row, rope):
    if rope:
        h_ref, mod_ref, ng_ref, w_ref, cos_ref, sin_ref, q_ref, k_ref, v_ref, u_ref, g_ref = refs
    else:
        h_ref, mod_ref, ng_ref, w_ref, q_ref, k_ref, v_ref, u_ref, g_ref = refs
    row = pl.program_id(0) if mod_row is None else mod_row
    xn = _rms(h_ref[0]) * ng_ref[...]
    xm = (xn * (1.0 + _mod_chunk(mod_ref, row, 1)) + _mod_chunk(mod_ref, row, 0)).astype(BF16)
    y = _dot(xm, w_ref[...])
    q = y[:, 0:RET_QK]
    k = y[:, RET_QK:2 * RET_QK]
    if rope:
        q = _rope(q, cos_ref[...], sin_ref[...])
        k = _rope(k, cos_ref[...], sin_ref[...])
    q_ref[0] = q.astype(BF16)
    k_ref[0] = (k * (RET_DK ** -0.5)).astype(BF16)
    c0 = 2 * RET_QK
    v_ref[0] = y[:, c0:c0 + RET_WIDTH].astype(BF16)
    for j in range(S5_LB):
        lo = c0 + RET_WIDTH + j * LANE
        u_ref[0, j] = y[:, lo:lo + LANE]
    g_ref[0] = y[:, c0 + RET_WIDTH + S5_WIDTH:]


def _inproj0(h, mod, layer, ng, w, cos, sin, mod_row, tm):
    b, n, _ = h.shape
    rope = cos is not None
    row_spec = lambda width: pl.BlockSpec((1, tm, width), lambda i, j: (i, j, 0))
    in_specs = [
        row_spec(D_MODEL),
        pl.BlockSpec((1, MOD_ROWS, N_MOD * D_MODEL), lambda i, j: (layer, 0, 0)),
        pl.BlockSpec((1, D_MODEL), lambda i, j: (0, 0)),
        pl.BlockSpec((D_MODEL, AB_IN), lambda i, j: (0, 0)),
    ]
    args = [h, mod, ng, w]
    if rope:
        in_specs += [pl.BlockSpec((tm, RET_QK), lambda i, j: (j, 0))] * 2
        args += [cos, sin]
    widths = (RET_QK, RET_QK, RET_WIDTH, S5_WIDTH, RET_WIDTH)
    dtypes = (BF16, BF16, BF16, F32, F32)
    out_shape = [jax.ShapeDtypeStruct((b, n, wd), dt) for wd, dt in zip(widths, dtypes)]
    out_specs = [row_spec(wd) for wd in widths]
    out_shape[3] = jax.ShapeDtypeStruct((b, S5_LB, n, LANE), F32)
    out_specs[3] = pl.BlockSpec((1, S5_LB, tm, LANE), lambda i, j: (i, 0, j, 0))
    return pl.pallas_call(
        functools.partial(_inproj0_kernel, mod_row=mod_row, rope=rope),
        out_shape=out_shape,
        grid=(b, n // tm),
        in_specs=in_specs,
        out_specs=out_specs,
        compiler_params=_cp("parallel", "parallel"),
        name="inproj0_lat" if rope else "inproj0_ctx",
    )(*args)


def _ret_kernel(lg_ref, ql, kl, vl, gl, qc, kc, vc, gc, rl, rc, st_s, *, ncl, ncc, unroll):
    c = RET_CHUNK
    dk2 = 2 * RET_DK
    nt = ncc + ncl
    p = pl.program_id(1)
    h_a = 2 * p
    lgf_a = lg_ref[pl.ds(h_a, 1), :]
    lgf_b = lg_ref[pl.ds(h_a + 1, 1), :]
    lgb_a = lg_ref[pl.ds(RET_HEADS + h_a, 1), :]
    lgb_b = lg_ref[pl.ds(RET_HEADS + h_a + 1, 1), :]
    lane = lax.broadcasted_iota(jnp.int32, (1, 2 * RET_DK), 1)
    is_a = lane < RET_DK
    lgf_lane = jnp.where(is_a, lgf_a, lgf_b)
    lgb_lane = jnp.where(is_a, lgb_a, lgb_b)
    ri = lax.broadcasted_iota(jnp.int32, (c, c), 0).astype(F32)
    ci = lax.broadcasted_iota(jnp.int32, (c, c), 1).astype(F32)
    diff = ri - ci

    def dmat(lgf, lgb):
        fwd = jnp.exp(jnp.maximum(diff, 0.0) * lgf)
        bwd = jnp.exp(jnp.maximum(-diff, 0.0) * lgb)
        return jnp.where(diff > 0, fwd, jnp.where(diff < 0, bwd, 2.0))

    d_a = dmat(lgf_a, lgb_a)
    d_b = dmat(lgf_b, lgb_b)
    rowp = lax.broadcasted_iota(jnp.int32, (c, dk2), 0).astype(F32)
    qd = jnp.concatenate([jnp.exp((rowp + 1.0) * lgf_lane), jnp.exp((c - rowp) * lgb_lane)], axis=1)
    kd = jnp.concatenate([jnp.exp((c - 1.0 - rowp) * lgf_lane), jnp.exp(rowp * lgb_lane)], axis=1)
    rowk = lax.broadcasted_iota(jnp.int32, (dk2, 2 * RET_DV), 0)
    cd_f = jnp.exp(c * jnp.where(rowk < RET_DK, lgf_a[:, :1], lgf_b[:, :1]))
    cd_b = jnp.exp(c * jnp.where(rowk < RET_DK, lgb_a[:, :1], lgb_b[:, :1]))
    mask2 = jnp.concatenate([is_a, is_a], axis=1)

    def increment(k, v, slot):
        kk = jnp.concatenate([k, k], axis=1).astype(F32) * kd
        st_s[slot] = _dot_tn(kk.astype(BF16), v)

    for n in range(ncc):
        increment(kc[0, n * c:(n + 1) * c, :], vc[0, n * c:(n + 1) * c, :], n)

    def inc_body(i, carry):
        for j in range(unroll):
            n = i * unroll + j
            sl = pl.ds(pl.multiple_of(n * c, c), c)
            increment(kl[0, sl, :], vl[0, sl, :], ncc + n)
        return carry

    lax.fori_loop(0, ncl // unroll, inc_body, 0)

    def rec_body(t, carry):
        sf, sb = carry
        u = jnp.where(t < ncc, ncc - 1 - t, nt - 1 - (t - ncc))
        inc_f = st_s[t, 0:dk2, :]
        inc_b = st_s[u, dk2:2 * dk2, :]
        st_s[t, 0:dk2, :] = sf
        st_s[u, dk2:2 * dk2, :] = sb
        return cd_f * sf + inc_f, cd_b * sb + inc_b

    zero = jnp.zeros((dk2, 2 * RET_DV), F32)
    lax.fori_loop(0, nt, rec_body, (zero, zero))

    def output(q, k, v, g, slot, out_ref, st):
        q2 = (jnp.concatenate([q, q], axis=1).astype(F32) * qd).astype(BF16)
        s_n = st_s[slot].astype(BF16)
        for keep_a, dm, cs in ((True, d_a, 0), (False, d_b, RET_DV)):
            m1 = is_a if keep_a else jnp.logical_not(is_a)
            m2 = mask2 if keep_a else jnp.logical_not(mask2)
            att = _dot_nt(jnp.where(m1, q, jnp.zeros_like(q)), k) * dm
            o = (_dot(att.astype(BF16), v[:, cs:cs + RET_DV])
                 + _dot(jnp.where(m2, q2, jnp.zeros_like(q2)), s_n[:, cs:cs + RET_DV]))
            o = _rms(o) * jax.nn.silu(g[:, cs:cs + RET_DV])
            out_ref[0, pl.ds(st, c), cs:cs + RET_DV] = o.astype(BF16)

    for n in range(ncc):
        sl = slice(n * c, (n + 1) * c)
        output(qc[0, sl, :], kc[0, sl, :], vc[0, sl, :], gc[0, sl, :], n, rc, n * c)

    def out_body(i, carry):
        for j in range(unroll):
            n = i * unroll + j
            st = pl.multiple_of(n * c, c)
            sl = pl.ds(st, c)
            output(ql[0, sl, :], kl[0, sl, :], vl[0, sl, :], gl[0, sl, :], ncc + n, rl, st)
        return carry

    lax.fori_loop(0, ncl // unroll, out_body, 0)


def _retention(lg_rows, q_l, k_l, v_l, g_l, q_c, k_c, v_c, g_c):
    b, n, _ = q_l.shape
    nc = q_c.shape[1]
    ncl, ncc = n // RET_CHUNK, nc // RET_CHUNK
    pairs = RET_HEADS // 2
    unroll = 8 if ncl % 8 == 0 else 1

    def spec(rows, width):
        return pl.BlockSpec((1, rows, width), lambda i, p: (i, 0, p))

    return pl.pallas_call(
        functools.partial(_ret_kernel, ncl=ncl, ncc=ncc, unroll=unroll),
        out_shape=[jax.ShapeDtypeStruct((b, n, RET_WIDTH), BF16),
                   jax.ShapeDtypeStruct((b, nc, RET_WIDTH), BF16)],
        grid=(b, pairs),
        in_specs=[pl.BlockSpec((2 * RET_HEADS, 2 * RET_DK), lambda i, p: (0, 0)),
                  spec(n, 2 * RET_DK), spec(n, 2 * RET_DK), spec(n, 2 * RET_DV), spec(n, 2 * RET_DV),
                  spec(nc, 2 * RET_DK), spec(nc, 2 * RET_DK), spec(nc, 2 * RET_DV), spec(nc, 2 * RET_DV)],
        out_specs=[spec(n, 2 * RET_DV), spec(nc, 2 * RET_DV)],
        scratch_shapes=[pltpu.VMEM((ncl + ncc, 4 * RET_DK, 2 * RET_DV), F32)],
        compiler_params=_cp("parallel", "parallel"),
        name="retention",
    )(lg_rows, q_l, k_l, v_l, g_l, q_c, k_c, v_c, g_c)


def _dot_hi(a, b, contract=(1, 0)):
    dims = (((contract[0],), (contract[1],)), ((), ()))
    return lax.dot_general(a, b, dims, preferred_element_type=F32, precision=lax.Precision.HIGHEST)


def _s5_prep_kernel(ar_row, ai_row, ldt, btr, bti, ctr, cti, rm, rwin, rwout, abig):
    t, g, p, kp = S5_CHUNK, S5_GROUP, S5_STATE, S5_POW
    row = S5_ROW
    i0 = lambda shape: lax.broadcasted_iota(jnp.int32, shape, 0)
    i1 = lambda shape: lax.broadcasted_iota(jnp.int32, shape, 1)
    f32 = lambda m: jnp.where(m, 1.0, 0.0).astype(F32)
    s_of_r = i0((row, kp)) // g
    k_of_l = i1((row, kp))
    sel_rows = (f32(k_of_l == t - 1 - s_of_r), f32(k_of_l == s_of_r))
    t_of_c = i1((kp, row)) // g
    k_of_s = i0((kp, row))
    sel_out = (f32(k_of_s == t_of_c + 1), f32(k_of_s == t - t_of_c))
    sel_lag = (f32(k_of_s == t_of_c), f32(k_of_s == t - 1 - t_of_c))
    tile_l = f32(i1((g, row)) % g == i0((g, row)))
    tile_r = f32(i0((row, g)) % g == i1((row, g)))
    lane = i1((g, row))
    k_col = i0((kp, 1)).astype(F32)
    k_row = i1((1, kp)).astype(F32)
    first = i0((8, 1)) == 0

    def outer(a, k):
        a8 = jnp.where(first, jnp.broadcast_to(a, (8, a.shape[1])), 0.0)
        return _dot_hi(a8, jnp.broadcast_to(k, (8, k.shape[1])), (0, 0))

    for q in range(S5_GPB):
        lags = []
        for d in range(2):
            dt = jnp.exp(ldt[d, q])
            are_r, aim_r = ar_row[d, q], ai_row[d, q]
            mag = jnp.exp(are_r * dt)
            ang = aim_r * dt
            nr, ni = mag * jnp.cos(ang) - 1.0, mag * jnp.sin(ang)
            den = jnp.square(are_r) + jnp.square(aim_r)
            fr = (nr * are_r + ni * aim_r) / den
            fi = (ni * are_r - nr * aim_r) / den
            pm = jnp.exp(k_col * (are_r * dt))
            pa = k_col * ang
            pk_re, pk_im = pm * jnp.cos(pa), pm * jnp.sin(pa)
            pmt = jnp.exp(outer(are_r * dt, k_row))
            pat = outer(ang, k_row)
            pt_re, pt_im = pmt * jnp.cos(pat), pmt * jnp.sin(pat)
            bt_re, bt_im = _dot_hi(tile_r, btr[d, q], (1, 1)), _dot_hi(tile_r, bti[d, q], (1, 1))
            bb_re = fr * bt_re - fi * bt_im
            bb_im = fr * bt_im + fi * bt_re
            pr_re, pr_im = _dot_hi(sel_rows[d], pk_re), _dot_hi(sel_rows[d], pk_im)
            w_re = pr_re * bb_re - pr_im * bb_im
            w_im = pr_re * bb_im + pr_im * bb_re
            for s in range(t):
                rows = slice(s * LANE + q * g, s * LANE + (q + 1) * g)
                rwin[0, rows, d * p:(d + 1) * p] = w_re[s * g:(s + 1) * g].astype(BF16)
                rwin[0, rows, (2 + d) * p:(3 + d) * p] = w_im[s * g:(s + 1) * g].astype(BF16)
            ct_re, ct_im = _dot_hi(ctr[d, q], tile_l, (0, 0)), _dot_hi(cti[d, q], tile_l, (0, 0))

            def c_pow(sel):
                pc_re, pc_im = _dot_hi(pt_re, sel), _dot_hi(pt_im, sel)
                return ct_re * pc_re - ct_im * pc_im, ct_re * pc_im + ct_im * pc_re

            o_re, o_im = c_pow(sel_out[d])
            r0 = d * S5_GPB * p + q * p
            rwout[0, r0:r0 + p, :] = o_re.astype(BF16)
            rwout[0, S5_HALF + r0:S5_HALF + r0 + p, :] = (-o_im).astype(BF16)
            l_re, l_im = c_pow(sel_lag[d])
            lags.append(_dot_hi(bb_re[0:g], l_re) - _dot_hi(bb_im[0:g], l_im))
            abig[0, 0:1, r0:r0 + p] = pk_re[t:t + 1, :]
            abig[0, 1:2, r0:r0 + p] = pk_im[t:t + 1, :]
        for s in range(t):
            fwd = jnp.where(lane >= g * s, pltpu.roll(lags[0], g * s, 1), 0.0)
            bwd = jnp.where(lane < g * (s + 1), pltpu.roll(lags[1], (row - g * (t - 1 - s)) % row, 1), 0.0)
            rm[0, s * LANE + q * g:s * LANE + (q + 1) * g, :] = (fwd + bwd).astype(BF16)


def _s5_prep(a_re, a_im, log_dt, b_re, b_im, c_re, c_im):
    gg, p, g = S5_GROUPS, S5_STATE, S5_GROUP
    f = lambda x: x.astype(F32)
    args = (f(a_re).reshape(2, gg, 1, p), f(a_im).reshape(2, gg, 1, p),
            f(log_dt).reshape(2, gg, 1, 1),
            f(b_re), f(b_im), f(c_re), f(c_im))
    spec = lambda r, c: pl.BlockSpec((2, S5_GPB, r, c), lambda j: (0, j, 0, 0))
    out = lambda r, c: pl.BlockSpec((1, r, c), lambda j: (j, 0, 0))
    return pl.pallas_call(
        _s5_prep_kernel,
        out_shape=[jax.ShapeDtypeStruct((S5_LB, S5_BIG, S5_ROW), BF16),
                   jax.ShapeDtypeStruct((S5_LB, S5_BIG, S5_ROW), BF16),
                   jax.ShapeDtypeStruct((S5_LB, 2 * S5_HALF, S5_ROW), BF16),
                   jax.ShapeDtypeStruct((S5_LB, 2, S5_HALF), F32)],
        grid=(S5_LB,),
        in_specs=[spec(1, p), spec(1, p), spec(1, 1),
                  spec(p, g), spec(p, g), spec(g, p), spec(g, p)],
        out_specs=[out(S5_BIG, S5_ROW), out(S5_BIG, S5_ROW), out(2 * S5_HALF, S5_ROW), out(2, S5_HALF)],
        compiler_params=_cp("parallel"),
        name="s5_prep",
    )(*args)


def _s5_expand(r_ref, col_unit, col_block, row_unit):
    n = S5_BIG
    a = lax.broadcasted_iota(jnp.int32, (S5_ROW, n), 0)
    c = lax.broadcasted_iota(jnp.int32, (S5_ROW, n), 1)
    e = jnp.where((a // col_unit == c // col_block) & (a % col_unit == c % col_unit), 1.0, 0.0).astype(BF16)
    x = _dot(r_ref[0], e)
    rq = (lax.broadcasted_iota(jnp.int32, (n, n), 0) // row_unit) % S5_GPB
    cq = (lax.broadcasted_iota(jnp.int32, (n, n), 1) // col_unit) % S5_GPB
    return jnp.where(rq == cq, x, 0.0).astype(BF16)


def _s5_kernel(ul, uc, rm_ref, rwin_ref, rwout_ref, a_ref, yl, yc, x_s, s_s, m_s, win_s, wout_s, *, ncl, ncc):
    hw = S5_HALF
    hh = hw // 2

    @pl.when(pl.program_id(1) == 0)
    def _():
        m_s[...] = _s5_expand(rm_ref, S5_GROUP, LANE, S5_GROUP)
        win_s[...] = _s5_expand(rwin_ref, S5_STATE, S5_GPB * S5_STATE, S5_GROUP)
        wout_s[...] = _s5_expand(rwout_ref, S5_GROUP, LANE, S5_STATE)

    zc = uc[0, 0].astype(BF16)
    zl = ul[0, 0].astype(BF16)
    x_s[0:ncc, :] = _dot(zc, win_s[...])
    x_s[ncc:ncc + ncl, :] = _dot(zl, win_s[...])
    a_re = a_ref[0, 0:1, :]
    a_im = a_ref[0, 1:2, :]

    def segment(base, n, carry):
        s_re, s_im = carry
        for i in range(n):
            rf = slice(base + i, base + i + 1)
            rb = slice(base + n - 1 - i, base + n - i)
            s_s[rf, 0:hh] = s_re[:, 0:hh]
            s_s[rb, hh:hw] = s_re[:, hh:hw]
            s_s[rf, hw:hw + hh] = s_im[:, 0:hh]
            s_s[rb, hw + hh:2 * hw] = s_im[:, hh:hw]
            x_re = jnp.concatenate([x_s[rf, 0:hh], x_s[rb, hh:hw]], axis=-1)
            x_im = jnp.concatenate([x_s[rf, hw:hw + hh], x_s[rb, hw + hh:2 * hw]], axis=-1)
            s_re, s_im = a_re * s_re - a_im * s_im + x_re, a_re * s_im + a_im * s_re + x_im
        return s_re, s_im

    zero = jnp.zeros((1, hw), F32)
    carry = segment(0, ncc, (zero, zero))
    segment(ncc, ncl, carry)
    sp = s_s[...].astype(BF16)
    yc[0, 0] = _dot(zc, m_s[...]) + _dot(sp[0:ncc], wout_s[...])
    yl[0, 0] = _dot(zl, m_s[...]) + _dot(sp[ncc:ncc + ncl], wout_s[...])


def _s5(prep, u_l, u_c):
    rm, rwin, rwout, a_big = prep
    b, _, n, _ = u_l.shape
    nc = u_c.shape[2]
    ncl, ncc = n // S5_CHUNK, nc // S5_CHUNK
    zl = u_l.reshape(b, S5_LB, ncl, S5_BIG)
    zc = u_c.reshape(b, S5_LB, ncc, S5_BIG)
    rows = lambda r: pl.BlockSpec((1, 1, r, S5_BIG), lambda j, i: (i, j, 0, 0))
    wspec = lambda r, c: pl.BlockSpec((1, r, c), lambda j, i: (j, 0, 0))
    big = pltpu.VMEM((S5_BIG, S5_BIG), BF16)
    yl, yc = pl.pallas_call(
        functools.partial(_s5_kernel, ncl=ncl, ncc=ncc),
        out_shape=[jax.ShapeDtypeStruct(zl.shape, F32), jax.ShapeDtypeStruct(zc.shape, F32)],
        grid=(S5_LB, b),
        in_specs=[rows(ncl), rows(ncc), wspec(S5_BIG, S5_ROW), wspec(S5_BIG, S5_ROW),
                  wspec(2 * S5_HALF, S5_ROW), wspec(2, S5_HALF)],
        out_specs=[rows(ncl), rows(ncc)],
        scratch_shapes=[pltpu.VMEM((ncc + ncl, 2 * S5_HALF), F32), pltpu.VMEM((ncc + ncl, 2 * S5_HALF), F32),
                        big, big, big],
        compiler_params=_cp("parallel", "arbitrary"),
        name="s5",
    )(zl, zc, rm, rwin, rwout, a_big)
    return yl.reshape(u_l.shape), yc.reshape(u_c.shape)


def _mix_mlp_kernel(*refs, mod_row, s5_merge, final_norm, fb):
    if s5_merge:
        (h_ref, r_ref, y5_ref, u_ref, ds_ref, wg_ref, bg_ref, wo_ref, mod_ref, nm_ref, w1_ref, w2_ref,
         *rest) = refs
    else:
        h_ref, r_ref, wo_ref, mod_ref, nm_ref, w1_ref, w2_ref, *rest = refs
    if final_norm:
        nf_ref, o_ref = rest
    else:
        (o_ref,) = rest
    row = pl.program_id(0) if mod_row is None else mod_row
    if s5_merge:
        y5 = jnp.concatenate([y5_ref[0, lb] for lb in range(S5_LB)], axis=-1)
        u = jnp.concatenate([u_ref[0, lb] for lb in range(S5_LB)], axis=-1)
        y = jax.nn.gelu(y5 + ds_ref[...] * u)
        y = y * jax.nn.sigmoid(_dot(y.astype(BF16), wg_ref[...]) + bg_ref[...])
        mix = _dot(r_ref[0], wo_ref[0:RET_WIDTH, :]) + _dot(y.astype(BF16), wo_ref[RET_WIDTH:D_MODEL, :])
    else:
        mix = _dot(r_ref[0], wo_ref[...])
    h1 = h_ref[0] + _mod_chunk(mod_ref, row, 2) * mix
    xn = _rms(h1) * nm_ref[...]
    xm = (xn * (1.0 + _mod_chunk(mod_ref, row, 4)) + _mod_chunk(mod_ref, row, 3)).astype(BF16)
    acc = None
    for j in range(D_FF // fb):
        a = jnp.square(jnp.maximum(_dot(xm, w1_ref[:, j * fb:(j + 1) * fb]), 0.0)).astype(BF16)
        part = _dot(a, w2_ref[j * fb:(j + 1) * fb, :])
        acc = part if acc is None else acc + part
    h2 = h1 + _mod_chunk(mod_ref, row, 5) * acc
    if final_norm:
        h2 = _rms(h2) * nf_ref[...]
    o_ref[0] = h2


def _mix_mlp(h, r, s5y, u, s5p, wo, mod, layer, nm, w1, w2, nf, mod_row, tm, fb, name):
    b, n, _ = h.shape
    s5_merge = s5y is not None
    final_norm = nf is not None
    one = pl.Buffered(1)
    row_spec = lambda width: pl.BlockSpec((1, tm, width), lambda i, t: (i, t, 0))
    const = lambda shape: pl.BlockSpec(shape, lambda i, t: (0,) * len(shape), pipeline_mode=one)
    in_specs = [row_spec(D_MODEL), row_spec(r.shape[-1])]
    args = [h, r]
    if s5_merge:
        d_skip, w_glu, b_glu = s5p
        lb_spec = pl.BlockSpec((1, S5_LB, tm, LANE), lambda i, t: (i, 0, t, 0))
        in_specs += [lb_spec, lb_spec, const((1, S5_WIDTH)),
                     const((S5_WIDTH, S5_WIDTH)), const((1, S5_WIDTH))]
        args += [s5y, u, d_skip, w_glu, b_glu]
    in_specs += [const((D_MODEL, D_MODEL)),
                 pl.BlockSpec((1, MOD_ROWS, N_MOD * D_MODEL), lambda i, t: (layer, 0, 0), pipeline_mode=one),
                 const((1, D_MODEL)), const((D_MODEL, D_FF)), const((D_FF, D_MODEL))]
    args += [wo, mod, nm, w1, w2]
    if final_norm:
        in_specs.append(const((1, D_MODEL)))
        args.append(nf)
    return pl.pallas_call(
        functools.partial(_mix_mlp_kernel, mod_row=mod_row, s5_merge=s5_merge, final_norm=final_norm, fb=fb),
        out_shape=jax.ShapeDtypeStruct((b, n, D_MODEL), F32),
        grid=(b, n // tm),
        in_specs=in_specs,
        out_specs=row_spec(D_MODEL),
        compiler_params=_cp("parallel", "parallel"),
        name=name,
    )(*args)


def _hgrn_lower_bounds(lbl_ref, layer):
    out = []
    for d in range(2):
        z = [lbl_ref[d, k:k + 1, :] for k in range(DEPTH)]
        zmax = functools.reduce(jnp.maximum, z)
        e = [jnp.exp(v - zmax) for v in z]
        tot = functools.reduce(lambda a, b_: a + b_, e)
        lb = jnp.zeros_like(tot)
        for k in range(1, layer + 1):
            lb = lb + e[k] / tot
        out.append(lb)
    return out


def _inproj1_kernel(h_ref, mod_ref, ng_ref, lbl_ref, w_ref, *out_refs, mod_row, layer, latent):
    row = pl.program_id(0) if mod_row is None else mod_row
    xn = _rms(h_ref[0]) * ng_ref[...]
    xm = (xn * (1.0 + _mod_chunk(mod_ref, row, 1)) + _mod_chunk(mod_ref, row, 0)).astype(BF16)
    col = lambda k: _dot(xm, w_ref[:, k * D_MODEL:(k + 1) * D_MODEL])
    lbs = _hgrn_lower_bounds(lbl_ref, layer)
    refs = list(out_refs)
    if latent:
        refs.pop(0)[0] = col(0).astype(BF16)
    for d in range(2):
        t = (1.0 - lbs[d]) * jax.nn.sigmoid(col(1 + d))
        refs.pop(0)[0] = jnp.log(lbs[d] + t)
        refs.pop(0)[0] = ((1.0 - lbs[d]) - t).astype(BF16)
    refs.pop(0)[0] = col(3).astype(BF16)
    if latent:
        refs.pop(0)[0] = jax.nn.silu(col(4)).astype(BF16)


def _inproj1(h, mod, layer, ng, lb_logits, w, latent, mod_row, tm, name):
    b, n, _ = h.shape
    one = pl.Buffered(1)
    row_spec = pl.BlockSpec((1, tm, D_MODEL), lambda i, t: (i, t, 0))
    dtypes = ([BF16] if latent else []) + [F32, BF16, F32, BF16, BF16] + ([BF16] if latent else [])
    return pl.pallas_call(
        functools.partial(_inproj1_kernel, mod_row=mod_row, layer=layer, latent=latent),
        out_shape=[jax.ShapeDtypeStruct((b, n, D_MODEL), dt) for dt in dtypes],
        grid=(b, n // tm),
        in_specs=[row_spec,
                  pl.BlockSpec((1, MOD_ROWS, N_MOD * D_MODEL), lambda i, t: (layer, 0, 0), pipeline_mode=one),
                  pl.BlockSpec((1, D_MODEL), lambda i, t: (0, 0), pipeline_mode=one),
                  pl.BlockSpec(lb_logits.shape, lambda i, t: (0, 0, 0), pipeline_mode=one),
                  pl.BlockSpec(w.shape, lambda i, t: (0, 0), pipeline_mode=one)],
        out_specs=[row_spec] * len(dtypes),
        compiler_params=_cp("parallel", "parallel"),
        name=name,
    )(h, mod, ng, lb_logits, w)


def _cumsum_mm(tri, x):
    acc = None
    r = x
    for i in range(HG_SPLIT):
        p = r.astype(BF16)
        acc = _dot(tri, p) if acc is None else acc + _dot(tri, p)
        if i + 1 < HG_SPLIT:
            r = r - p.astype(F32)
    return acc


def _hgrn_kernel(ng_ref, ql, lffl, kfl, lfbl, kbl, il, sgl, lffc, kfc, lfbc, kbc, ic, o_ref,
                 qin_s, att_s, kv_s, et_s, kvc_s, etc_s, cum_s, ko_s, qt_s, kt_s, *, nbl, nbc, unroll, out_blocks):
    cb = HG_BLOCK
    mid = cb // 2
    gb = HG_GROUP
    gr = gb * cb
    dk = HG_DK

    ri = lax.broadcasted_iota(jnp.int32, (gr, gr), 0)
    ci = lax.broadcasted_iota(jnp.int32, (gr, gr), 1)
    same = (ri // cb) == (ci // cb)
    rb = lax.broadcasted_iota(jnp.int32, (cb, cb), 0)
    cbi = lax.broadcasted_iota(jnp.int32, (cb, cb), 1)
    tri_l = jnp.where(same & (ri >= ci), 1.0, 0.0).astype(BF16)
    dirs = ((0, rb >= cbi, mid - 1, cb - 1), (1, rb <= cbi, mid, 0))

    def cumsums(lfs, slot):
        pre = _cumsum_mm(tri_l, jnp.concatenate(lfs, axis=-1))
        pre_b = pre[:, dk:].reshape(gb, cb, dk)
        cum_s[slot, 0] = pre[:, :dk]
        cum_s[slot, 1] = (pre_b[:, cb - 1:cb, :] - pre_b).reshape(gr, dk) + lfs[1]

    def operands(slot, kks, q, n0, et_ref):
        for d, keep, ref_row, tot_row in dirs:
            cum = cum_s[slot, d].reshape(gb, cb, dk)
            kk = kks[d].astype(F32).reshape(gb, cb, dk)
            ref = cum[:, ref_row:ref_row + 1, :]
            tot = cum[:, tot_row:tot_row + 1, :]
            e = cum - ref
            kt = kk * jnp.exp(-e)
            ko_s[slot, d] = (kt * jnp.exp(tot - ref)).astype(BF16).reshape(gr, dk)
            e_tot = jnp.exp(tot)
            for j in range(gb):
                et_ref[n0 + j, :, d * dk:(d + 1) * dk] = e_tot[j]
            if q is not None:
                qt = q.astype(F32).reshape(gb, cb, dk) * jnp.exp(e)
                qin_s[pl.ds(pl.multiple_of(n0 * cb, gr), gr), d * dk:(d + 1) * dk] = (
                    (qt * jnp.exp(ref)).astype(BF16).reshape(gr, dk))
                qt_s[slot, d] = qt.astype(BF16).reshape(gr, dk)
                kt_s[slot, d] = kt.astype(BF16).reshape(gr, dk)

    def matmuls(slot, v, with_q, n0, kv_ref):
        for d in range(2):
            for j in range(gb):
                rows = slice(j * cb, (j + 1) * cb)
                kv_ref[n0 + j, :, d * dk:(d + 1) * dk] = _dot_tn(v[rows], ko_s[slot, d, rows, :])
        if with_q:
            for j in range(gb):
                rows = slice(j * cb, (j + 1) * cb)
                att = (jnp.where(dirs[0][1], _dot_nt(qt_s[slot, 0, rows, :], kt_s[slot, 0, rows, :]), 0.0)
                       + jnp.where(dirs[1][1], _dot_nt(qt_s[slot, 1, rows, :], kt_s[slot, 1, rows, :]), 0.0))
                att_s[n0 + j] = att.astype(BF16)

    for g in range(nbc // gb):
        sl = slice(g * gr, (g + 1) * gr)
        cumsums((lffc[0, sl, :], lfbc[0, sl, :]), 0)
        operands(0, (kfc[0, sl, :], kbc[0, sl, :]), None, g * gb, etc_s)
        matmuls(0, ic[0, sl, :], False, g * gb, kvc_s)

    ngl = nbl // gb
    rows_of = lambda g: pl.ds(pl.multiple_of(g * gr, gr), gr)

    def lat_cumsums(g, slot):
        cumsums((lffl[0, rows_of(g), :], lfbl[0, rows_of(g), :]), slot)

    def lat_operands(g, slot):
        operands(slot, (kfl[0, rows_of(g), :], kbl[0, rows_of(g), :]), ql[0, rows_of(g), :], g * gb, et_s)

    def lat_matmuls(g, slot):
        matmuls(slot, il[0, rows_of(g), :], True, g * gb, kv_s)

    lat_cumsums(0, 0)
    lat_operands(0, 0)
    lat_cumsums(1, 1)

    def prep_body(i, carry):
        g = 2 * i
        lat_matmuls(g, 0)
        lat_operands(g + 1, 1)
        lat_cumsums(g + 2, 0)
        lat_matmuls(g + 1, 1)
        lat_operands(g + 2, 0)
        lat_cumsums(g + 3, 1)
        return carry

    lax.fori_loop(0, ngl // 2 - 1, prep_body, 0)
    lat_matmuls(ngl - 2, 0)
    lat_operands(ngl - 1, 1)
    lat_matmuls(ngl - 1, 1)

    lane = lax.broadcasted_iota(jnp.int32, (dk, 2 * dk), 1)
    is_f = lane < dk
    st = jnp.zeros((dk, 2 * dk), F32)
    for n in range(nbc):
        m = nbc - 1 - n
        st = (st * jnp.where(is_f[:1], etc_s[n], etc_s[m]) + jnp.where(is_f, kvc_s[n], kvc_s[m]))

    def rec_body(t, st):
        u = nbl - 1 - t
        inc = jnp.where(is_f, kv_s[t], kv_s[u])
        dec = jnp.where(is_f[:1], et_s[t], et_s[u])
        kv_s[t, :, 0:dk] = st[:, 0:dk]
        kv_s[u, :, dk:2 * dk] = st[:, dk:2 * dk]
        return st * dec + inc

    lax.fori_loop(0, nbl, rec_body, st, unroll=unroll)

    def out_body(i, carry):
        for j in range(out_blocks):
            n = i * out_blocks + j
            sl = pl.ds(pl.multiple_of(n * cb, cb), cb)
            o = _dot(att_s[n], il[0, sl, :]) + _dot_nt(qin_s[sl, :], kv_s[n].astype(BF16))
            o = _rms(o) * ng_ref[...] * sgl[0, sl, :].astype(F32)
            o_ref[0, sl, :] = o.astype(BF16)
        return carry

    lax.fori_loop(0, nbl // out_blocks, out_body, 0)


def _hgrn(norm_g, q_l, lff_l, kf_l, lfb_l, kb_l, i_l, sg_l, lff_c, kf_c, lfb_c, kb_c, i_c):
    b, n, _ = q_l.shape
    nc = lff_c.shape[1]
    nbl, nbc = n // HG_BLOCK, nc // HG_BLOCK
    out_blocks = min(16, nbl)
    assert nbl % (2 * HG_GROUP) == 0 and nbc % HG_GROUP == 0 and nbl % out_blocks == 0
    spec = lambda rows: pl.BlockSpec((1, rows, HG_DK), lambda i, h: (i, 0, h))
    slot = lambda dt: pltpu.VMEM((2, 2, HG_GROUP * HG_BLOCK, HG_DK), dt)
    return pl.pallas_call(
        functools.partial(_hgrn_kernel, nbl=nbl, nbc=nbc, unroll=2, out_blocks=out_blocks),
        out_shape=jax.ShapeDtypeStruct((b, n, D_MODEL), BF16),
        grid=(b, HG_HEADS),
        in_specs=[pl.BlockSpec((1, HG_DK), lambda i, h: (0, 0))] + [spec(n)] * 7 + [spec(nc)] * 5,
        out_specs=spec(n),
        scratch_shapes=[pltpu.VMEM((n, 2 * HG_DK), BF16),
                        pltpu.VMEM((nbl, HG_BLOCK, HG_BLOCK), BF16),
                        pltpu.VMEM((nbl, HG_DK, 2 * HG_DK), F32),
                        pltpu.VMEM((nbl, 1, 2 * HG_DK), F32),
                        pltpu.VMEM((nbc, HG_DK, 2 * HG_DK), F32),
                        pltpu.VMEM((nbc, 1, 2 * HG_DK), F32),
                        slot(F32), slot(BF16), slot(BF16), slot(BF16)],
        compiler_params=_cp("parallel", "parallel"),
        name="hgrn2",
    )(norm_g, q_l, lff_l, kf_l, lfb_l, kb_l, i_l, sg_l, lff_c, kf_c, lfb_c, kb_c, i_c)


def _rope_tables(n_tok):
    tok = jnp.arange(n_tok, dtype=jnp.int32)[:, None]
    row = (tok // GRID_W).astype(F32)
    col = (tok % GRID_W).astype(F32)
    n_freq = RET_DK // 4
    lane = jnp.arange(RET_QK, dtype=jnp.int32)[None, :]
    j = lane % (2 * n_freq)
    inv = ROPE_BASE ** (-(j % n_freq).astype(F32) / n_freq)
    ang = jnp.where(j < n_freq, row, col) * inv
    sign = jnp.where(lane % RET_DK < RET_DK // 2, -1.0, 1.0)
    return jnp.cos(ang), jnp.sin(ang) * sign


def kernel(x, c, ctx, c_ctx, w_mod, b_mod, norm_mix, norm_mlp, w_mlp_in, w_mlp_out, ab_w_in, ab_w_out, ret_logit, s5_a_re, s5_a_im, s5_log_dt, s5_b_re, s5_b_im, s5_c_re, s5_c_im, s5_d, s5_w_glu, s5_b_glu, hg_w_in, hg_w_out, hg_lb_logits, hg_norm, norm_final):
    b, n, d = x.shape
    nc = ctx.shape[1]
    assert d == D_MODEL and b + 1 <= MOD_ROWS and w_mod.shape[0] == DEPTH == 2
    assert n % 512 == 0 and nc % 256 == 0 and n % GRID_W == 0
    ctx_row = b
    tm_l, tm_c = 512, 256

    cc = jnp.zeros((MOD_ROWS, d), F32).at[:b].set(c).at[b].set(c_ctx)
    mod = _adaln(cc, w_mod, b_mod)

    row2 = lambda a: a.reshape(1, -1)
    w_in0 = ab_w_in[0].astype(BF16)
    cos, sin = _rope_tables(n)
    ng0 = row2(norm_mix[0])
    q_l, k_l, v_l, u_l, g_l = _inproj0(x, mod, 0, ng0, w_in0, cos, sin, None, tm_l)
    q_c, k_c, v_c, u_c, g_c = _inproj0(ctx, mod, 0, ng0, w_in0, None, None, ctx_row, tm_c)

    log_gamma = jax.nn.log_sigmoid(ret_logit[0].astype(F32))
    lg_rows = jnp.broadcast_to(log_gamma.reshape(2 * RET_HEADS, 1), (2 * RET_HEADS, 2 * RET_DK))
    r_l, r_c = _retention(lg_rows, q_l, k_l, v_l, g_l, q_c, k_c, v_c, g_c)

    s5_ops = _s5_prep(s5_a_re[0], s5_a_im[0], s5_log_dt[0], s5_b_re[0], s5_b_im[0], s5_c_re[0], s5_c_im[0])
    y5_l, y5_c = _s5(s5_ops, u_l, u_c)

    s5p = (row2(s5_d[0]), s5_w_glu[0].astype(BF16), row2(s5_b_glu[0]))
    wo0 = ab_w_out[0].astype(BF16)
    w1_0, w2_0 = w_mlp_in[0].astype(BF16), w_mlp_out[0].astype(BF16)
    nm0 = row2(norm_mlp[0])
    h_l = _mix_mlp(x, r_l, y5_l, u_l, s5p, wo0, mod, 0, nm0, w1_0, w2_0, None, None, tm_l, 1024, "mix_mlp0_lat")
    h_c = _mix_mlp(ctx, r_c, y5_c, u_c, s5p, wo0, mod, 0, nm0, w1_0, w2_0, None, ctx_row, tm_c, 1024, "mix_mlp0_ctx")

    w_in1 = hg_w_in[0].astype(BF16)
    ng1 = row2(norm_mix[1])
    lat1 = _inproj1(h_l, mod, 1, ng1, hg_lb_logits, w_in1, True, None, tm_l, "inproj1_lat")
    ctx1 = _inproj1(h_c, mod, 1, ng1, hg_lb_logits, w_in1, False, ctx_row, tm_c, "inproj1_ctx")
    o1 = _hgrn(row2(hg_norm[0]), *lat1, *ctx1)
    return _mix_mlp(h_l, o1, None, None, None, hg_w_out[0].astype(BF16), mod, 1, row2(norm_mlp[1]),
                    w_mlp_in[1].astype(BF16), w_mlp_out[1].astype(BF16), row2(norm_final), None, tm_l, 1024,
                    "mix_mlp1_lat")
```

```python
import functools

import jax
import jax.numpy as jnp
from jax import lax
from jax.experimental import pallas as pl
from jax.experimental.pallas import tpu as pltpu

F32 = jnp.float32
BF16 = jnp.bfloat16

D_MODEL = 1024
DEPTH = 2
GRID_W = 64
EPS = 1e-6
ROPE_BASE = 10000.0
N_MOD = 6
RET_HEADS = 4
RET_DK = 64
RET_DV = 128
RET_QK = RET_HEADS * RET_DK
RET_WIDTH = RET_HEADS * RET_DV
RET_CHUNK = 128
S5_WIDTH = D_MODEL - RET_WIDTH
S5_GROUP = 16
S5_GROUPS = S5_WIDTH // S5_GROUP
S5_STATE = 64
S5_CHUNK = 16
S5_ROW = S5_CHUNK * S5_GROUP
LANE = 128
S5_LB = S5_WIDTH // LANE
S5_GPB = LANE // S5_GROUP
S5_BIG = S5_CHUNK * LANE
S5_HALF = S5_GPB * 2 * S5_STATE
S5_POW = 32
AB_IN = 2 * RET_QK + 2 * RET_WIDTH + S5_WIDTH
HG_HEADS = 8
HG_DK = D_MODEL // HG_HEADS
HG_BLOCK = 64
HG_GROUP = 4
HG_SPLIT = 2
D_FF = 4 * D_MODEL
MOD_ROWS = 16

VMEM_LIMIT_BYTES = 56 * 1024 * 1024


def _cp(*sem):
    return pltpu.CompilerParams(dimension_semantics=sem, vmem_limit_bytes=VMEM_LIMIT_BYTES)


def _dot(a, b):
    return jnp.dot(a, b, preferred_element_type=F32)


def _dot_nt(a, b):
    return lax.dot_general(a, b, (((1,), (1,)), ((), ())), preferred_element_type=F32)


def _dot_tn(a, b):
    return lax.dot_general(a, b, (((0,), (0,)), ((), ())), preferred_element_type=F32)


def _rms(x):
    return x * lax.rsqrt(jnp.mean(x * x, axis=-1, keepdims=True) + EPS)


def _mod_chunk(mod_ref, row, i):
    return mod_ref[0, pl.ds(row, 1), i * D_MODEL:(i + 1) * D_MODEL]


def _adaln_kernel(cc_ref, w_ref, b_ref, o_ref):
    s = jax.nn.silu(cc_ref[...]).astype(BF16)
    o_ref[0] = _dot(s, w_ref[0].astype(BF16)) + b_ref[0]


def _adaln(cc, w_mod, b_mod):
    bn = 1536
    n = N_MOD * D_MODEL
    return pl.pallas_call(
        _adaln_kernel,
        out_shape=jax.ShapeDtypeStruct((DEPTH, MOD_ROWS, n), F32),
        grid=(DEPTH, n // bn),
        in_specs=[
            pl.BlockSpec((MOD_ROWS, D_MODEL), lambda l, j: (0, 0)),
            pl.BlockSpec((1, D_MODEL, bn), lambda l, j: (l, 0, j)),
            pl.BlockSpec((1, 1, bn), lambda l, j: (l, 0, j)),
        ],
        out_specs=pl.BlockSpec((1, MOD_ROWS, bn), lambda l, j: (l, 0, j)),
        compiler_params=_cp("parallel", "parallel"),
        name="adaln",
    )(cc, w_mod, b_mod.reshape(DEPTH, 1, n))


def _rope(t, cos, sin):
    lane = lax.broadcasted_iota(jnp.int32, t.shape, 1)
    first = (lane & (RET_DK // 2)) == 0
    w = t.shape[1]
    swapped = jnp.where(first, pltpu.roll(t, w - RET_DK // 2, 1), pltpu.roll(t, RET_DK // 2, 1))
    return t * cos + swapped * sin


def _inproj0_kernel(*refs, mod_row, rope):
    if rope:
        h_ref, mod_ref, ng_ref, w_ref, cos_ref, sin_ref, q_ref, k_ref, v_ref, u_ref, uz_ref, g_ref, u_s = refs
    else:
        h_ref, mod_ref, ng_ref, w_ref, q_ref, k_ref, v_ref, u_ref, uz_ref, g_ref, u_s = refs
    row = pl.program_id(0) if mod_row is None else mod_row
    xn = _rms(h_ref[0]) * ng_ref[...]
    xm = (xn * (1.0 + _mod_chunk(mod_ref, row, 1)) + _mod_chunk(mod_ref, row, 0)).astype(BF16)
    y = _dot(xm, w_ref[...])
    q = y[:, 0:RET_QK]
    k = y[:, RET_QK:2 * RET_QK]
    if rope:
        q = _rope(q, cos_ref[...], sin_ref[...])
        k = _rope(k, cos_ref[...], sin_ref[...])
    q_ref[0] = q.astype(BF16)
    k_ref[0] = (k * (RET_DK ** -0.5)).astype(BF16)
    c0 = 2 * RET_QK
    v_ref[0] = y[:, c0:c0 + RET_WIDTH].astype(BF16)
    u0 = c0 + RET_WIDTH
    u_ref[0] = y[:, u0:u0 + S5_WIDTH]
    g_ref[0] = y[:, u0 + S5_WIDTH:]
    nch = u_s.shape[1] // S5_CHUNK
    for j in range(S5_LB):
        u_s[j] = y[:, u0 + j * LANE:u0 + (j + 1) * LANE]
        for s in range(S5_CHUNK):
            uz_ref[0, j, :, s * LANE:(s + 1) * LANE] = u_s[j, pl.ds(s, nch, stride=S5_CHUNK), :].astype(BF16)


def _inproj0(h, mod, layer, ng, w, cos, sin, mod_row, tm):
    b, n, _ = h.shape
    rope = cos is not None
    row_spec = lambda width: pl.BlockSpec((1, tm, width), lambda i, j: (i, j, 0))
    in_specs = [
        row_spec(D_MODEL),
        pl.BlockSpec((1, MOD_ROWS, N_MOD * D_MODEL), lambda i, j: (layer, 0, 0)),
        pl.BlockSpec((1, D_MODEL), lambda i, j: (0, 0)),
        pl.BlockSpec((D_MODEL, AB_IN), lambda i, j: (0, 0)),
    ]
    args = [h, mod, ng, w]
    if rope:
        in_specs += [pl.BlockSpec((tm, RET_QK), lambda i, j: (j, 0))] * 2
        args += [cos, sin]
    widths = (RET_QK, RET_QK, RET_WIDTH, S5_WIDTH, RET_WIDTH)
    dtypes = (BF16, BF16, BF16, F32, F32)
    out_shape = [jax.ShapeDtypeStruct((b, n, wd), dt) for wd, dt in zip(widths, dtypes)]
    out_specs = [row_spec(wd) for wd in widths]
    out_shape.insert(4, jax.ShapeDtypeStruct((b, S5_LB, n // S5_CHUNK, S5_BIG), BF16))
    out_specs.insert(4, pl.BlockSpec((1, S5_LB, tm // S5_CHUNK, S5_BIG), lambda i, j: (i, 0, j, 0)))
    return pl.pallas_call(
        functools.partial(_inproj0_kernel, mod_row=mod_row, rope=rope),
        out_shape=out_shape,
        grid=(b, n // tm),
        in_specs=in_specs,
        out_specs=out_specs,
        scratch_shapes=[pltpu.VMEM((S5_LB, tm, LANE), F32)],
        compiler_params=_cp("parallel", "parallel"),
        name="inproj0_lat" if rope else "inproj0_ctx",
    )(*args)


def _ret_kernel(lg_ref, ql, kl, vl, gl, qc, kc, vc, gc, rl, rc, st_s, *, ncl, ncc, unroll):
    c = RET_CHUNK
    dk2 = 2 * RET_DK
    nt = ncc + ncl
    p = pl.program_id(1)
    h_a = 2 * p
    lgf_a = lg_ref[pl.ds(h_a, 1), :]
    lgf_b = lg_ref[pl.ds(h_a + 1, 1), :]
    lgb_a = lg_ref[pl.ds(RET_HEADS + h_a, 1), :]
    lgb_b = lg_ref[pl.ds(RET_HEADS + h_a + 1, 1), :]
    lane = lax.broadcasted_iota(jnp.int32, (1, 2 * RET_DK), 1)
    is_a = lane < RET_DK
    lgf_lane = jnp.where(is_a, lgf_a, lgf_b)
    lgb_lane = jnp.where(is_a, lgb_a, lgb_b)
    ri = lax.broadcasted_iota(jnp.int32, (c, c), 0).astype(F32)
    ci = lax.broadcasted_iota(jnp.int32, (c, c), 1).astype(F32)
    diff = ri - ci

    def dmat(lgf, lgb):
        fwd = jnp.exp(jnp.maximum(diff, 0.0) * lgf)
        bwd = jnp.exp(jnp.maximum(-diff, 0.0) * lgb)
        return jnp.where(diff > 0, fwd, jnp.where(diff < 0, bwd, 2.0))

    d_a = dmat(lgf_a, lgb_a)
    d_b = dmat(lgf_b, lgb_b)
    rowp = lax.broadcasted_iota(jnp.int32, (c, dk2), 0).astype(F32)
    qd = jnp.concatenate([jnp.exp((rowp + 1.0) * lgf_lane), jnp.exp((c - rowp) * lgb_lane)], axis=1)
    kd = jnp.concatenate([jnp.exp((c - 1.0 - rowp) * lgf_lane), jnp.exp(rowp * lgb_lane)], axis=1)
    rowk = lax.broadcasted_iota(jnp.int32, (dk2, 2 * RET_DV), 0)
    cd_f = jnp.exp(c * jnp.where(rowk < RET_DK, lgf_a[:, :1], lgf_b[:, :1]))
    cd_b = jnp.exp(c * jnp.where(rowk < RET_DK, lgb_a[:, :1], lgb_b[:, :1]))
    mask2 = jnp.concatenate([is_a, is_a], axis=1)

    def increment(k, v, slot):
        kk = jnp.concatenate([k, k], axis=1).astype(F32) * kd
        st_s[slot] = _dot_tn(kk.astype(BF16), v)

    for n in range(ncc):
        increment(kc[0, n * c:(n + 1) * c, :], vc[0, n * c:(n + 1) * c, :], n)

    def inc_body(i, carry):
        for j in range(unroll):
            n = i * unroll + j
            sl = pl.ds(pl.multiple_of(n * c, c), c)
            increment(kl[0, sl, :], vl[0, sl, :], ncc + n)
        return carry

    lax.fori_loop(0, ncl // unroll, inc_body, 0)

    def rec_body(t, carry):
        sf, sb = carry
        u = jnp.where(t < ncc, ncc - 1 - t, nt - 1 - (t - ncc))
        inc_f = st_s[t, 0:dk2, :]
        inc_b = st_s[u, dk2:2 * dk2, :]
        st_s[t, 0:dk2, :] = sf
        st_s[u, dk2:2 * dk2, :] = sb
        return cd_f * sf + inc_f, cd_b * sb + inc_b

    zero = jnp.zeros((dk2, 2 * RET_DV), F32)
    lax.fori_loop(0, nt, rec_body, (zero, zero))

    def output(q, k, v, g, slot, out_ref, st):
        q2 = (jnp.concatenate([q, q], axis=1).astype(F32) * qd).astype(BF16)
        s_n = st_s[slot].astype(BF16)
        for keep_a, dm, cs in ((True, d_a, 0), (False, d_b, RET_DV)):
            m1 = is_a if keep_a else jnp.logical_not(is_a)
            m2 = mask2 if keep_a else jnp.logical_not(mask2)
            att = _dot_nt(jnp.where(m1, q, jnp.zeros_like(q)), k) * dm
            o = (_dot(att.astype(BF16), v[:, cs:cs + RET_DV])
                 + _dot(jnp.where(m2, q2, jnp.zeros_like(q2)), s_n[:, cs:cs + RET_DV]))
            o = _rms(o) * jax.nn.silu(g[:, cs:cs + RET_DV])
            out_ref[0, pl.ds(st, c), cs:cs + RET_DV] = o.astype(BF16)

    for n in range(ncc):
        sl = slice(n * c, (n + 1) * c)
        output(qc[0, sl, :], kc[0, sl, :], vc[0, sl, :], gc[0, sl, :], n, rc, n * c)

    def out_body(i, carry):
        for j in range(unroll):
            n = i * unroll + j
            st = pl.multiple_of(n * c, c)
            sl = pl.ds(st, c)
            output(ql[0, sl, :], kl[0, sl, :], vl[0, sl, :], gl[0, sl, :], ncc + n, rl, st)
        return carry

    lax.fori_loop(0, ncl // unroll, out_body, 0)


def _retention(lg_rows, q_l, k_l, v_l, g_l, q_c, k_c, v_c, g_c):
    b, n, _ = q_l.shape
    nc = q_c.shape[1]
    ncl, ncc = n // RET_CHUNK, nc // RET_CHUNK
    pairs = RET_HEADS // 2
    unroll = 8 if ncl % 8 == 0 else 1

    def spec(rows, width):
        return pl.BlockSpec((1, rows, width), lambda i, p: (i, 0, p))

    return pl.pallas_call(
        functools.partial(_ret_kernel, ncl=ncl, ncc=ncc, unroll=unroll),
        out_shape=[jax.ShapeDtypeStruct((b, n, RET_WIDTH), BF16),
                   jax.ShapeDtypeStruct((b, nc, RET_WIDTH), BF16)],
        grid=(b, pairs),
        in_specs=[pl.BlockSpec((2 * RET_HEADS, 2 * RET_DK), lambda i, p: (0, 0)),
                  spec(n, 2 * RET_DK), spec(n, 2 * RET_DK), spec(n, 2 * RET_DV), spec(n, 2 * RET_DV),
                  spec(nc, 2 * RET_DK), spec(nc, 2 * RET_DK), spec(nc, 2 * RET_DV), spec(nc, 2 * RET_DV)],
        out_specs=[spec(n, 2 * RET_DV), spec(nc, 2 * RET_DV)],
        scratch_shapes=[pltpu.VMEM((ncl + ncc, 4 * RET_DK, 2 * RET_DV), F32)],
        compiler_params=_cp("parallel", "parallel"),
        name="retention",
    )(lg_rows, q_l, k_l, v_l, g_l, q_c, k_c, v_c, g_c)


def _dot_hi(a, b, contract=(1, 0)):
    dims = (((contract[0],), (contract[1],)), ((), ()))
    return lax.dot_general(a, b, dims, preferred_element_type=F32, precision=lax.Precision.HIGHEST)


def _s5_prep_kernel(ar_row, ai_row, ldt, btr, bti, ctr, cti, rm, rwin, rwout, abig):
    t, g, p, kp = S5_CHUNK, S5_GROUP, S5_STATE, S5_POW
    row = S5_ROW
    i0 = lambda shape: lax.broadcasted_iota(jnp.int32, shape, 0)
    i1 = lambda shape: lax.broadcasted_iota(jnp.int32, shape, 1)
    f32 = lambda m: jnp.where(m, 1.0, 0.0).astype(F32)
    s_of_r = i0((row, kp)) // g
    k_of_l = i1((row, kp))
    sel_rows = (f32(k_of_l == t - 1 - s_of_r), f32(k_of_l == s_of_r))
    t_of_c = i1((kp, row)) // g
    k_of_s = i0((kp, row))
    sel_out = (f32(k_of_s == t_of_c + 1), f32(k_of_s == t - t_of_c))
    sel_lag = (f32(k_of_s == t_of_c), f32(k_of_s == t - 1 - t_of_c))
    tile_l = f32(i1((g, row)) % g == i0((g, row)))
    tile_r = f32(i0((row, g)) % g == i1((row, g)))
    lane = i1((g, row))
    k_col = i0((kp, 1)).astype(F32)
    k_row = i1((1, kp)).astype(F32)
    first = i0((8, 1)) == 0

    def outer(a, k):
        a8 = jnp.where(first, jnp.broadcast_to(a, (8, a.shape[1])), 0.0)
        return _dot_hi(a8, jnp.broadcast_to(k, (8, k.shape[1])), (0, 0))

    for q in range(S5_GPB):
        lags = []
        for d in range(2):
            dt = jnp.exp(ldt[d, q])
            are_r, aim_r = ar_row[d, q], ai_row[d, q]
            mag = jnp.exp(are_r * dt)
            ang = aim_r * dt
            nr, ni = mag * jnp.cos(ang) - 1.0, mag * jnp.sin(ang)
            den = jnp.square(are_r) + jnp.square(aim_r)
            fr = (nr * are_r + ni * aim_r) / den
            fi = (ni * are_r - nr * aim_r) / den
            pm = jnp.exp(k_col * (are_r * dt))
            pa = k_col * ang
            pk_re, pk_im = pm * jnp.cos(pa), pm * jnp.sin(pa)
            pmt = jnp.exp(outer(are_r * dt, k_row))
            pat = outer(ang, k_row)
            pt_re, pt_im = pmt * jnp.cos(pat), pmt * jnp.sin(pat)
            bt_re, bt_im = _dot_hi(tile_r, btr[d, q], (1, 1)), _dot_hi(tile_r, bti[d, q], (1, 1))
            bb_re = fr * bt_re - fi * bt_im
            bb_im = fr * bt_im + fi * bt_re
            pr_re, pr_im = _dot_hi(sel_rows[d], pk_re), _dot_hi(sel_rows[d], pk_im)
            w_re = pr_re * bb_re - pr_im * bb_im
            w_im = pr_re * bb_im + pr_im * bb_re
            for s in range(t):
                rows = slice(s * LANE + q * g, s * LANE + (q + 1) * g)
                rwin[0, rows, d * p:(d + 1) * p] = w_re[s * g:(s + 1) * g].astype(BF16)
                rwin[0, rows, (2 + d) * p:(3 + d) * p] = w_im[s * g:(s + 1) * g].astype(BF16)
            ct_re, ct_im = _dot_hi(ctr[d, q], tile_l, (0, 0)), _dot_hi(cti[d, q], tile_l, (0, 0))

            def c_pow(sel):
                pc_re, pc_im = _dot_hi(pt_re, sel), _dot_hi(pt_im, sel)
                return ct_re * pc_re - ct_im * pc_im, ct_re * pc_im + ct_im * pc_re

            o_re, o_im = c_pow(sel_out[d])
            r0 = d * S5_GPB * p + q * p
            rwout[0, r0:r0 + p, :] = o_re.astype(BF16)
            rwout[0, S5_HALF + r0:S5_HALF + r0 + p, :] = (-o_im).astype(BF16)
            l_re, l_im = c_pow(sel_lag[d])
            lags.append(_dot_hi(bb_re[0:g], l_re) - _dot_hi(bb_im[0:g], l_im))
            abig[0, 0:1, r0:r0 + p] = pk_re[t:t + 1, :]
            abig[0, 1:2, r0:r0 + p] = pk_im[t:t + 1, :]
        for s in range(t):
            fwd = jnp.where(lane >= g * s, pltpu.roll(lags[0], g * s, 1), 0.0)
            bwd = jnp.where(lane < g * (s + 1), pltpu.roll(lags[1], (row - g * (t - 1 - s)) % row, 1), 0.0)
            rm[0, s * LANE + q * g:s * LANE + (q + 1) * g, :] = (fwd + bwd).astype(BF16)


def _s5_prep(a_re, a_im, log_dt, b_re, b_im, c_re, c_im):
    gg, p, g = S5_GROUPS, S5_STATE, S5_GROUP
    f = lambda x: x.astype(F32)
    args = (f(a_re).reshape(2, gg, 1, p), f(a_im).reshape(2, gg, 1, p),
            f(log_dt).reshape(2, gg, 1, 1),
            f(b_re), f(b_im), f(c_re), f(c_im))
    spec = lambda r, c: pl.BlockSpec((2, S5_GPB, r, c), lambda j: (0, j, 0, 0))
    out = lambda r, c: pl.BlockSpec((1, r, c), lambda j: (j, 0, 0))
    return pl.pallas_call(
        _s5_prep_kernel,
        out_shape=[jax.ShapeDtypeStruct((S5_LB, S5_BIG, S5_ROW), BF16),
                   jax.ShapeDtypeStruct((S5_LB, S5_BIG, S5_ROW), BF16),
                   jax.ShapeDtypeStruct((S5_LB, 2 * S5_HALF, S5_ROW), BF16),
                   jax.ShapeDtypeStruct((S5_LB, 2, S5_HALF), F32)],
        grid=(S5_LB,),
        in_specs=[spec(1, p), spec(1, p), spec(1, 1),
                  spec(p, g), spec(p, g), spec(g, p), spec(g, p)],
        out_specs=[out(S5_BIG, S5_ROW), out(S5_BIG, S5_ROW), out(2 * S5_HALF, S5_ROW), out(2, S5_HALF)],
        compiler_params=_cp("parallel"),
        name="s5_prep",
    )(*args)


def _s5_expand(r_ref, col_unit, col_block, row_unit):
    n = S5_BIG
    a = lax.broadcasted_iota(jnp.int32, (S5_ROW, n), 0)
    c = lax.broadcasted_iota(jnp.int32, (S5_ROW, n), 1)
    e = jnp.where((a // col_unit == c // col_block) & (a % col_unit == c % col_unit), 1.0, 0.0).astype(BF16)
    x = _dot(r_ref[0], e)
    rq = (lax.broadcasted_iota(jnp.int32, (n, n), 0) // row_unit) % S5_GPB
    cq = (lax.broadcasted_iota(jnp.int32, (n, n), 1) // col_unit) % S5_GPB
    return jnp.where(rq == cq, x, 0.0).astype(BF16)


def _s5_kernel(ul, uc, rm_ref, rwin_ref, rwout_ref, a_ref, yl, yc, x_s, s_s, m_s, win_s, wout_s, *, ncl, ncc):
    hw = S5_HALF
    hh = hw // 2

    @pl.when(pl.program_id(1) == 0)
    def _():
        m_s[...] = _s5_expand(rm_ref, S5_GROUP, LANE, S5_GROUP)
        win_s[...] = _s5_expand(rwin_ref, S5_STATE, S5_GPB * S5_STATE, S5_GROUP)
        wout_s[...] = _s5_expand(rwout_ref, S5_GROUP, LANE, S5_STATE)

    zc = uc[0, 0]
    zl = ul[0, 0]
    x_s[0:ncc, :] = _dot(zc, win_s[...])
    x_s[ncc:ncc + ncl, :] = _dot(zl, win_s[...])
    a_re = a_ref[0, 0:1, :]
    a_im = a_ref[0, 1:2, :]

    def segment(base, n, carry):
        s_re, s_im = carry
        for i in range(n):
            rf = slice(base + i, base + i + 1)
            rb = slice(base + n - 1 - i, base + n - i)
            s_s[rf, 0:hh] = s_re[:, 0:hh]
            s_s[rb, hh:hw] = s_re[:, hh:hw]
            s_s[rf, hw:hw + hh] = s_im[:, 0:hh]
            s_s[rb, hw + hh:2 * hw] = s_im[:, hh:hw]
            x_re = jnp.concatenate([x_s[rf, 0:hh], x_s[rb, hh:hw]], axis=-1)
            x_im = jnp.concatenate([x_s[rf, hw:hw + hh], x_s[rb, hw + hh:2 * hw]], axis=-1)
            s_re, s_im = a_re * s_re - a_im * s_im + x_re, a_re * s_im + a_im * s_re + x_im
        return s_re, s_im

    zero = jnp.zeros((1, hw), F32)
    carry = segment(0, ncc, (zero, zero))
    segment(ncc, ncl, carry)
    sp = s_s[...].astype(BF16)
    yc[0, 0] = _dot(zc, m_s[...]) + _dot(sp[0:ncc], wout_s[...])
    yl[0, 0] = _dot(zl, m_s[...]) + _dot(sp[ncc:ncc + ncl], wout_s[...])


def _s5(prep, zl, zc):
    rm, rwin, rwout, a_big = prep
    b, _, ncl, _ = zl.shape
    ncc = zc.shape[2]
    rows = lambda r: pl.BlockSpec((1, 1, r, S5_BIG), lambda j, i: (i, j, 0, 0))
    wspec = lambda r, c: pl.BlockSpec((1, r, c), lambda j, i: (j, 0, 0))
    big = pltpu.VMEM((S5_BIG, S5_BIG), BF16)
    return pl.pallas_call(
        functools.partial(_s5_kernel, ncl=ncl, ncc=ncc),
        out_shape=[jax.ShapeDtypeStruct(zl.shape, F32), jax.ShapeDtypeStruct(zc.shape, F32)],
        grid=(S5_LB, b),
        in_specs=[rows(ncl), rows(ncc), wspec(S5_BIG, S5_ROW), wspec(S5_BIG, S5_ROW),
                  wspec(2 * S5_HALF, S5_ROW), wspec(2, S5_HALF)],
        out_specs=[rows(ncl), rows(ncc)],
        scratch_shapes=[pltpu.VMEM((ncc + ncl, 2 * S5_HALF), F32), pltpu.VMEM((ncc + ncl, 2 * S5_HALF), F32),
                        big, big, big],
        compiler_params=_cp("parallel", "arbitrary"),
        name="s5",
    )(zl, zc, rm, rwin, rwout, a_big)


def _mix_mlp_kernel(*refs, mod_row, s5_merge, final_norm, fb):
    if s5_merge:
        (h_ref, r_ref, y5_ref, u_ref, ds_ref, wg_ref, bg_ref, wo_ref, mod_ref, nm_ref, w1_ref, w2_ref,
         *rest) = refs
    else:
        h_ref, r_ref, wo_ref, mod_ref, nm_ref, w1_ref, w2_ref, *rest = refs
    if final_norm:
        nf_ref, o_ref, *scratch = rest
    else:
        o_ref, *scratch = rest
    row = pl.program_id(0) if mod_row is None else mod_row
    if s5_merge:
        (y_s,) = scratch
        nch = y_s.shape[1] // S5_CHUNK
        for lb in range(S5_LB):
            for s in range(S5_CHUNK):
                y_s[lb, pl.ds(s, nch, stride=S5_CHUNK), :] = y5_ref[0, lb, :, s * LANE:(s + 1) * LANE]
        y5 = jnp.concatenate([y_s[lb] for lb in range(S5_LB)], axis=-1)
        y = jax.nn.gelu(y5 + ds_ref[...] * u_ref[0])
        y = y * jax.nn.sigmoid(_dot(y.astype(BF16), wg_ref[...]) + bg_ref[...])
        mix = _dot(r_ref[0], wo_ref[0:RET_WIDTH, :]) + _dot(y.astype(BF16), wo_ref[RET_WIDTH:D_MODEL, :])
    else:
        mix = _dot(r_ref[0], wo_ref[...])
    h1 = h_ref[0] + _mod_chunk(mod_ref, row, 2) * mix
    xn = _rms(h1) * nm_ref[...]
    xm = (xn * (1.0 + _mod_chunk(mod_ref, row, 4)) + _mod_chunk(mod_ref, row, 3)).astype(BF16)
    acc = None
    for j in range(D_FF // fb):
        a = jnp.square(jnp.maximum(_dot(xm, w1_ref[:, j * fb:(j + 1) * fb]), 0.0)).astype(BF16)
        part = _dot(a, w2_ref[j * fb:(j + 1) * fb, :])
        acc = part if acc is None else acc + part
    h2 = h1 + _mod_chunk(mod_ref, row, 5) * acc
    if final_norm:
        h2 = _rms(h2) * nf_ref[...]
    o_ref[0] = h2


def _mix_mlp(h, r, s5y, u, s5p, wo, mod, layer, nm, w1, w2, nf, mod_row, tm, fb, name):
    b, n, _ = h.shape
    s5_merge = s5y is not None
    final_norm = nf is not None
    one = pl.Buffered(1)
    row_spec = lambda width: pl.BlockSpec((1, tm, width), lambda i, t: (i, t, 0))
    const = lambda shape: pl.BlockSpec(shape, lambda i, t: (0,) * len(shape), pipeline_mode=one)
    in_specs = [row_spec(D_MODEL), row_spec(r.shape[-1])]
    args = [h, r]
    if s5_merge:
        d_skip, w_glu, b_glu = s5p
        z_spec = pl.BlockSpec((1, S5_LB, tm // S5_CHUNK, S5_BIG), lambda i, t: (i, 0, t, 0))
        in_specs += [z_spec, row_spec(S5_WIDTH), const((1, S5_WIDTH)),
                     const((S5_WIDTH, S5_WIDTH)), const((1, S5_WIDTH))]
        args += [s5y, u, d_skip, w_glu, b_glu]
    in_specs += [const((D_MODEL, D_MODEL)),
                 pl.BlockSpec((1, MOD_ROWS, N_MOD * D_MODEL), lambda i, t: (layer, 0, 0), pipeline_mode=one),
                 const((1, D_MODEL)), const((D_MODEL, D_FF)), const((D_FF, D_MODEL))]
    args += [wo, mod, nm, w1, w2]
    if final_norm:
        in_specs.append(const((1, D_MODEL)))
        args.append(nf)
    return pl.pallas_call(
        functools.partial(_mix_mlp_kernel, mod_row=mod_row, s5_merge=s5_merge, final_norm=final_norm, fb=fb),
        out_shape=jax.ShapeDtypeStruct((b, n, D_MODEL), F32),
        grid=(b, n // tm),
        in_specs=in_specs,
        out_specs=row_spec(D_MODEL),
        scratch_shapes=[pltpu.VMEM((S5_LB, tm, LANE), F32)] if s5_merge else [],
        compiler_params=_cp("parallel", "parallel"),
        name=name,
    )(*args)


def _hgrn_lower_bounds(lbl_ref, layer):
    out = []
    for d in range(2):
        z = [lbl_ref[d, k:k + 1, :] for k in range(DEPTH)]
        zmax = functools.reduce(jnp.maximum, z)
        e = [jnp.exp(v - zmax) for v in z]
        tot = functools.reduce(lambda a, b_: a + b_, e)
        lb = jnp.zeros_like(tot)
        for k in range(1, layer + 1):
            lb = lb + e[k] / tot
        out.append(lb)
    return out


def _inproj1_kernel(h_ref, mod_ref, ng_ref, lbl_ref, w_ref, *out_refs, mod_row, layer, latent):
    row = pl.program_id(0) if mod_row is None else mod_row
    xn = _rms(h_ref[0]) * ng_ref[...]
    xm = (xn * (1.0 + _mod_chunk(mod_ref, row, 1)) + _mod_chunk(mod_ref, row, 0)).astype(BF16)
    col = lambda k: _dot(xm, w_ref[:, k * D_MODEL:(k + 1) * D_MODEL])
    lbs = _hgrn_lower_bounds(lbl_ref, layer)
    refs = list(out_refs)
    if latent:
        refs.pop(0)[0] = col(0).astype(BF16)
    for d in range(2):
        t = (1.0 - lbs[d]) * jax.nn.sigmoid(col(1 + d))
        refs.pop(0)[0] = jnp.log(lbs[d] + t)
        refs.pop(0)[0] = ((1.0 - lbs[d]) - t).astype(BF16)
    refs.pop(0)[0] = col(3).astype(BF16)
    if latent:
        refs.pop(0)[0] = jax.nn.silu(col(4)).astype(BF16)


def _inproj1(h, mod, layer, ng, lb_logits, w, latent, mod_row, tm, name):
    b, n, _ = h.shape
    one = pl.Buffered(1)
    row_spec = pl.BlockSpec((1, tm, D_MODEL), lambda i, t: (i, t, 0))
    dtypes = ([BF16] if latent else []) + [F32, BF16, F32, BF16, BF16] + ([BF16] if latent else [])
    return pl.pallas_call(
        functools.partial(_inproj1_kernel, mod_row=mod_row, layer=layer, latent=latent),
        out_shape=[jax.ShapeDtypeStruct((b, n, D_MODEL), dt) for dt in dtypes],
        grid=(b, n // tm),
        in_specs=[row_spec,
                  pl.BlockSpec((1, MOD_ROWS, N_MOD * D_MODEL), lambda i, t: (layer, 0, 0), pipeline_mode=one),
                  pl.BlockSpec((1, D_MODEL), lambda i, t: (0, 0), pipeline_mode=one),
                  pl.BlockSpec(lb_logits.shape, lambda i, t: (0, 0, 0), pipeline_mode=one),
                  pl.BlockSpec(w.shape, lambda i, t: (0, 0), pipeline_mode=one)],
        out_specs=[row_spec] * len(dtypes),
        compiler_params=_cp("parallel", "parallel"),
        name=name,
    )(h, mod, ng, lb_logits, w)


def _cumsum_mm(tri, x):
    acc = None
    r = x
    for i in range(HG_SPLIT):
        p = r.astype(BF16)
        acc = _dot(tri, p) if acc is None else acc + _dot(tri, p)
        if i + 1 < HG_SPLIT:
            r = r - p.astype(F32)
    return acc


def _hgrn_kernel(ng_ref, ql, lffl, kfl, lfbl, kbl, il, sgl, lffc, kfc, lfbc, kbc, ic, o_ref,
                 qin_s, att_s, kv_s, et_s, kvc_s, etc_s, cum_s, ko_s, qt_s, kt_s, *, nbl, nbc, unroll, out_blocks):
    cb = HG_BLOCK
    mid = cb // 2
    gb = HG_GROUP
    gr = gb * cb
    dk = HG_DK

    ri = lax.broadcasted_iota(jnp.int32, (gr, gr), 0)
    ci = lax.broadcasted_iota(jnp.int32, (gr, gr), 1)
    same = (ri // cb) == (ci // cb)
    rb = lax.broadcasted_iota(jnp.int32, (cb, cb), 0)
    cbi = lax.broadcasted_iota(jnp.int32, (cb, cb), 1)
    tri_l = jnp.where(same & (ri >= ci), 1.0, 0.0).astype(BF16)
    dirs = ((0, rb >= cbi, mid - 1, cb - 1), (1, rb <= cbi, mid, 0))

    def cumsums(lfs, slot):
        pre = _cumsum_mm(tri_l, jnp.concatenate(lfs, axis=-1))
        pre_b = pre[:, dk:].reshape(gb, cb, dk)
        cum_s[slot, 0] = pre[:, :dk]
        cum_s[slot, 1] = (pre_b[:, cb - 1:cb, :] - pre_b).reshape(gr, dk) + lfs[1]

    def operands(slot, kks, q, n0, et_ref):
        for d, keep, ref_row, tot_row in dirs:
            cum = cum_s[slot, d].reshape(gb, cb, dk)
            kk = kks[d].astype(F32).reshape(gb, cb, dk)
            ref = cum[:, ref_row:ref_row + 1, :]
            tot = cum[:, tot_row:tot_row + 1, :]
            e = cum - ref
            kt = kk * jnp.exp(-e)
            ko_s[slot, d] = (kt * jnp.exp(tot - ref)).astype(BF16).reshape(gr, dk)
            e_tot = jnp.exp(tot)
            for j in range(gb):
                et_ref[n0 + j, :, d * dk:(d + 1) * dk] = e_tot[j]
            if q is not None:
                qt = q.astype(F32).reshape(gb, cb, dk) * jnp.exp(e)
                qin_s[pl.ds(pl.multiple_of(n0 * cb, gr), gr), d * dk:(d + 1) * dk] = (
                    (qt * jnp.exp(ref)).astype(BF16).reshape(gr, dk))
                qt_s[slot, d] = qt.astype(BF16).reshape(gr, dk)
                kt_s[slot, d] = kt.astype(BF16).reshape(gr, dk)

    def matmuls(slot, v, with_q, n0, kv_ref):
        for d in range(2):
            for j in range(gb):
                rows = slice(j * cb, (j + 1) * cb)
                kv_ref[n0 + j, :, d * dk:(d + 1) * dk] = _dot_tn(v[rows], ko_s[slot, d, rows, :])
        if with_q:
            for j in range(gb):
                rows = slice(j * cb, (j + 1) * cb)
                att = (jnp.where(dirs[0][1], _dot_nt(qt_s[slot, 0, rows, :], kt_s[slot, 0, rows, :]), 0.0)
                       + jnp.where(dirs[1][1], _dot_nt(qt_s[slot, 1, rows, :], kt_s[slot, 1, rows, :]), 0.0))
                att_s[n0 + j] = att.astype(BF16)

    for g in range(nbc // gb):
        sl = slice(g * gr, (g + 1) * gr)
        cumsums((lffc[0, sl, :], lfbc[0, sl, :]), 0)
        operands(0, (kfc[0, sl, :], kbc[0, sl, :]), None, g * gb, etc_s)
        matmuls(0, ic[0, sl, :], False, g * gb, kvc_s)

    ngl = nbl // gb
    rows_of = lambda g: pl.ds(pl.multiple_of(g * gr, gr), gr)

    def lat_cumsums(g, slot):
        cumsums((lffl[0, rows_of(g), :], lfbl[0, rows_of(g), :]), slot)

    def lat_operands(g, slot):
        operands(slot, (kfl[0, rows_of(g), :], kbl[0, rows_of(g), :]), ql[0, rows_of(g), :], g * gb, et_s)

    def lat_matmuls(g, slot):
        matmuls(slot, il[0, rows_of(g), :], True, g * gb, kv_s)

    lat_cumsums(0, 0)
    lat_operands(0, 0)
    lat_cumsums(1, 1)

    def prep_body(i, carry):
        g = 2 * i
        lat_matmuls(g, 0)
        lat_operands(g + 1, 1)
        lat_cumsums(g + 2, 0)
        lat_matmuls(g + 1, 1)
        lat_operands(g + 2, 0)
        lat_cumsums(g + 3, 1)
        return carry

    lax.fori_loop(0, ngl // 2 - 1, prep_body, 0)
    lat_matmuls(ngl - 2, 0)
    lat_operands(ngl - 1, 1)
    lat_matmuls(ngl - 1, 1)

    lane = lax.broadcasted_iota(jnp.int32, (dk, 2 * dk), 1)
    is_f = lane < dk
    st = jnp.zeros((dk, 2 * dk), F32)
    for n in range(nbc):
        m = nbc - 1 - n
        st = (st * jnp.where(is_f[:1], etc_s[n], etc_s[m]) + jnp.where(is_f, kvc_s[n], kvc_s[m]))

    def rec_body(t, st):
        u = nbl - 1 - t
        inc = jnp.where(is_f, kv_s[t], kv_s[u])
        dec = jnp.where(is_f[:1], et_s[t], et_s[u])
        kv_s[t, :, 0:dk] = st[:, 0:dk]
        kv_s[u, :, dk:2 * dk] = st[:, dk:2 * dk]
        return st * dec + inc

    lax.fori_loop(0, nbl, rec_body, st, unroll=unroll)

    def out_body(i, carry):
        for j in range(out_blocks):
            n = i * out_blocks + j
            sl = pl.ds(pl.multiple_of(n * cb, cb), cb)
            o = _dot(att_s[n], il[0, sl, :]) + _dot_nt(qin_s[sl, :], kv_s[n].astype(BF16))
            o = _rms(o) * ng_ref[...] * sgl[0, sl, :].astype(F32)
            o_ref[0, sl, :] = o.astype(BF16)
        return carry

    lax.fori_loop(0, nbl // out_blocks, out_body, 0)


def _hgrn(norm_g, q_l, lff_l, kf_l, lfb_l, kb_l, i_l, sg_l, lff_c, kf_c, lfb_c, kb_c, i_c):
    b, n, _ = q_l.shape
    nc = lff_c.shape[1]
    nbl, nbc = n // HG_BLOCK, nc // HG_BLOCK
    out_blocks = min(16, nbl)
    assert nbl % (2 * HG_GROUP) == 0 and nbc % HG_GROUP == 0 and nbl % out_blocks == 0
    spec = lambda rows: pl.BlockSpec((1, rows, HG_DK), lambda i, h: (i, 0, h))
    slot = lambda dt: pltpu.VMEM((2, 2, HG_GROUP * HG_BLOCK, HG_DK), dt)
    return pl.pallas_call(
        functools.partial(_hgrn_kernel, nbl=nbl, nbc=nbc, unroll=2, out_blocks=out_blocks),
        out_shape=jax.ShapeDtypeStruct((b, n, D_MODEL), BF16),
        grid=(b, HG_HEADS),
        in_specs=[pl.BlockSpec((1, HG_DK), lambda i, h: (0, 0))] + [spec(n)] * 7 + [spec(nc)] * 5,
        out_specs=spec(n),
        scratch_shapes=[pltpu.VMEM((n, 2 * HG_DK), BF16),
                        pltpu.VMEM((nbl, HG_BLOCK, HG_BLOCK), BF16),
                        pltpu.VMEM((nbl, HG_DK, 2 * HG_DK), F32),
                        pltpu.VMEM((nbl, 1, 2 * HG_DK), F32),
                        pltpu.VMEM((nbc, HG_DK, 2 * HG_DK), F32),
                        pltpu.VMEM((nbc, 1, 2 * HG_DK), F32),
                        slot(F32), slot(BF16), slot(BF16), slot(BF16)],
        compiler_params=_cp("parallel", "parallel"),
        name="hgrn2",
    )(norm_g, q_l, lff_l, kf_l, lfb_l, kb_l, i_l, sg_l, lff_c, kf_c, lfb_c, kb_c, i_c)


def _rope_tables(n_tok):
    tok = jnp.arange(n_tok, dtype=jnp.int32)[:, None]
    row = (tok // GRID_W).astype(F32)
    col = (tok % GRID_W).astype(F32)
    n_freq = RET_DK // 4
    lane = jnp.arange(RET_QK, dtype=jnp.int32)[None, :]
    j = lane % (2 * n_freq)
    inv = ROPE_BASE ** (-(j % n_freq).astype(F32) / n_freq)
    ang = jnp.where(j < n_freq, row, col) * inv
    sign = jnp.where(lane % RET_DK < RET_DK // 2, -1.0, 1.0)
    return jnp.cos(ang), jnp.sin(ang) * sign


def kernel(x, c, ctx, c_ctx, w_mod, b_mod, norm_mix, norm_mlp, w_mlp_in, w_mlp_out, ab_w_in, ab_w_out, ret_logit, s5_a_re, s5_a_im, s5_log_dt, s5_b_re, s5_b_im, s5_c_re, s5_c_im, s5_d, s5_w_glu, s5_b_glu, hg_w_in, hg_w_out, hg_lb_logits, hg_norm, norm_final):
    b, n, d = x.shape
    nc = ctx.shape[1]
    assert d == D_MODEL and b + 1 <= MOD_ROWS and w_mod.shape[0] == DEPTH == 2
    assert n % 512 == 0 and nc % 256 == 0 and n % GRID_W == 0
    ctx_row = b
    tm_l, tm_c = 512, 256

    cc = jnp.zeros((MOD_ROWS, d), F32).at[:b].set(c).at[b].set(c_ctx)
    mod = _adaln(cc, w_mod, b_mod)

    row2 = lambda a: a.reshape(1, -1)
    w_in0 = ab_w_in[0].astype(BF16)
    cos, sin = _rope_tables(n)
    ng0 = row2(norm_mix[0])
    q_l, k_l, v_l, u_l, uz_l, g_l = _inproj0(x, mod, 0, ng0, w_in0, cos, sin, None, tm_l)
    q_c, k_c, v_c, u_c, uz_c, g_c = _inproj0(ctx, mod, 0, ng0, w_in0, None, None, ctx_row, tm_c)

    log_gamma = jax.nn.log_sigmoid(ret_logit[0].astype(F32))
    lg_rows = jnp.broadcast_to(log_gamma.reshape(2 * RET_HEADS, 1), (2 * RET_HEADS, 2 * RET_DK))
    r_l, r_c = _retention(lg_rows, q_l, k_l, v_l, g_l, q_c, k_c, v_c, g_c)

    s5_ops = _s5_prep(s5_a_re[0], s5_a_im[0], s5_log_dt[0], s5_b_re[0], s5_b_im[0], s5_c_re[0], s5_c_im[0])
    y5_l, y5_c = _s5(s5_ops, uz_l, uz_c)

    s5p = (row2(s5_d[0]), s5_w_glu[0].astype(BF16), row2(s5_b_glu[0]))
    wo0 = ab_w_out[0].astype(BF16)
    w1_0, w2_0 = w_mlp_in[0].astype(BF16), w_mlp_out[0].astype(BF16)
    nm0 = row2(norm_mlp[0])
    h_l = _mix_mlp(x, r_l, y5_l, u_l, s5p, wo0, mod, 0, nm0, w1_0, w2_0, None, None, tm_l, 1024, "mix_mlp0_lat")
    h_c = _mix_mlp(ctx, r_c, y5_c, u_c, s5p, wo0, mod, 0, nm0, w1_0, w2_0, None, ctx_row, tm_c, 1024, "mix_mlp0_ctx")

    w_in1 = hg_w_in[0].astype(BF16)
    ng1 = row2(norm_mix[1])
    lat1 = _inproj1(h_l, mod, 1, ng1, hg_lb_logits, w_in1, True, None, tm_l, "inproj1_lat")
    ctx1 = _inproj1(h_c, mod, 1, ng1, hg_lb_logits, w_in1, False, ctx_row, tm_c, "inproj1_ctx")
    o1 = _hgrn(row2(hg_norm[0]), *lat1, *ctx1)
    return _mix_mlp(h_l, o1, None, None, None, hg_w_out[0].astype(BF16), mod, 1, row2(norm_mlp[1]),
                    w_mlp_in[1].astype(BF16), w_mlp_out[1].astype(BF16), row2(norm_final), None, tm_l, 1024,
                    "mix_mlp1_lat")
```

```python
import functools

import jax
import jax.numpy as jnp
from jax import lax
from jax.experimental import pallas as pl
from jax.experimental.pallas import tpu as pltpu

F32 = jnp.float32
BF16 = jnp.bfloat16

D_MODEL = 1024
DEPTH = 2
GRID_W = 64
EPS = 1e-6
ROPE_BASE = 10000.0
N_MOD = 6
RET_HEADS = 4
RET_DK = 64
RET_DV = 128
RET_QK = RET_HEADS * RET_DK
RET_WIDTH = RET_HEADS * RET_DV
RET_CHUNK = 128
S5_WIDTH = D_MODEL - RET_WIDTH
S5_GROUP = 16
S5_GROUPS = S5_WIDTH // S5_GROUP
S5_STATE = 64
S5_CHUNK = 16
S5_ROW = S5_CHUNK * S5_GROUP
LANE = 128
S5_LB = S5_WIDTH // LANE
S5_GPB = LANE // S5_GROUP
S5_BIG = S5_CHUNK * LANE
S5_HALF = S5_GPB * 2 * S5_STATE
S5_POW = 32
AB_IN = 2 * RET_QK + 2 * RET_WIDTH + S5_WIDTH
HG_HEADS = 8
HG_DK = D_MODEL // HG_HEADS
HG_BLOCK = 64
HG_GROUP = 4
HG_SPLIT = 2
D_FF = 4 * D_MODEL
MOD_ROWS = 16

VMEM_LIMIT_BYTES = 56 * 1024 * 1024


def _cp(*sem):
    return pltpu.CompilerParams(dimension_semantics=sem, vmem_limit_bytes=VMEM_LIMIT_BYTES)


def _dot(a, b):
    return jnp.dot(a, b, preferred_element_type=F32)


def _dot_nt(a, b):
    return lax.dot_general(a, b, (((1,), (1,)), ((), ())), preferred_element_type=F32)


def _dot_tn(a, b):
    return lax.dot_general(a, b, (((0,), (0,)), ((), ())), preferred_element_type=F32)


def _rms(x):
    return x * lax.rsqrt(jnp.mean(x * x, axis=-1, keepdims=True) + EPS)


def _mod_chunk(mod_ref, row, i):
    return mod_ref[0, pl.ds(row, 1), i * D_MODEL:(i + 1) * D_MODEL]


def _adaln_kernel(cc_ref, w_ref, b_ref, o_ref):
    s = jax.nn.silu(cc_ref[...]).astype(BF16)
    o_ref[0] = _dot(s, w_ref[0].astype(BF16)) + b_ref[0]


def _adaln(cc, w_mod, b_mod):
    bn = 1536
    n = N_MOD * D_MODEL
    return pl.pallas_call(
        _adaln_kernel,
        out_shape=jax.ShapeDtypeStruct((DEPTH, MOD_ROWS, n), F32),
        grid=(DEPTH, n // bn),
        in_specs=[
            pl.BlockSpec((MOD_ROWS, D_MODEL), lambda l, j: (0, 0)),
            pl.BlockSpec((1, D_MODEL, bn), lambda l, j: (l, 0, j)),
            pl.BlockSpec((1, 1, bn), lambda l, j: (l, 0, j)),
        ],
        out_specs=pl.BlockSpec((1, MOD_ROWS, bn), lambda l, j: (l, 0, j)),
        compiler_params=_cp("parallel", "parallel"),
        name="adaln",
    )(cc, w_mod, b_mod.reshape(DEPTH, 1, n))


def _rope(t, cos, sin):
    lane = lax.broadcasted_iota(jnp.int32, t.shape, 1)
    first = (lane & (RET_DK // 2)) == 0
    w = t.shape[1]
    swapped = jnp.where(first, pltpu.roll(t, w - RET_DK // 2, 1), pltpu.roll(t, RET_DK // 2, 1))
    return t * cos + swapped * sin


def _inproj0_kernel(*refs, mod_row, rope):
    if rope:
        h_ref, mod_ref, ng_ref, w_ref, cos_ref, sin_ref, q_ref, k_ref, v_ref, u_ref, uz_ref, g_ref, u_s = refs
    else:
        h_ref, mod_ref, ng_ref, w_ref, q_ref, k_ref, v_ref, u_ref, uz_ref, g_ref, u_s = refs
    row = pl.program_id(0) if mod_row is None else mod_row
    xn = _rms(h_ref[0]) * ng_ref[...]
    xm = (xn * (1.0 + _mod_chunk(mod_ref, row, 1)) + _mod_chunk(mod_ref, row, 0)).astype(BF16)
    y = _dot(xm, w_ref[...])
    q = y[:, 0:RET_QK]
    k = y[:, RET_QK:2 * RET_QK]
    if rope:
        q = _rope(q, cos_ref[...], sin_ref[...])
        k = _rope(k, cos_ref[...], sin_ref[...])
    q_ref[0] = q.astype(BF16)
    k_ref[0] = (k * (RET_DK ** -0.5)).astype(BF16)
    c0 = 2 * RET_QK
    v_ref[0] = y[:, c0:c0 + RET_WIDTH].astype(BF16)
    u0 = c0 + RET_WIDTH
    u_ref[0] = y[:, u0:u0 + S5_WIDTH]
    g_ref[0] = jax.nn.silu(y[:, u0 + S5_WIDTH:]).astype(BF16)
    nch = u_s.shape[1] // S5_CHUNK
    for j in range(S5_LB):
        u_s[j] = y[:, u0 + j * LANE:u0 + (j + 1) * LANE]
        for s in range(S5_CHUNK):
            uz_ref[0, j, :, s * LANE:(s + 1) * LANE] = u_s[j, pl.ds(s, nch, stride=S5_CHUNK), :].astype(BF16)


def _inproj0(h, mod, layer, ng, w, cos, sin, mod_row, tm):
    b, n, _ = h.shape
    rope = cos is not None
    row_spec = lambda width: pl.BlockSpec((1, tm, width), lambda i, j: (i, j, 0))
    in_specs = [
        row_spec(D_MODEL),
        pl.BlockSpec((1, MOD_ROWS, N_MOD * D_MODEL), lambda i, j: (layer, 0, 0)),
        pl.BlockSpec((1, D_MODEL), lambda i, j: (0, 0)),
        pl.BlockSpec((D_MODEL, AB_IN), lambda i, j: (0, 0)),
    ]
    args = [h, mod, ng, w]
    if rope:
        in_specs += [pl.BlockSpec((tm, RET_QK), lambda i, j: (j, 0))] * 2
        args += [cos, sin]
    widths = (RET_QK, RET_QK, RET_WIDTH, S5_WIDTH, RET_WIDTH)
    dtypes = (BF16, BF16, BF16, F32, BF16)
    out_shape = [jax.ShapeDtypeStruct((b, n, wd), dt) for wd, dt in zip(widths, dtypes)]
    out_specs = [row_spec(wd) for wd in widths]
    out_shape.insert(4, jax.ShapeDtypeStruct((b, S5_LB, n // S5_CHUNK, S5_BIG), BF16))
    out_specs.insert(4, pl.BlockSpec((1, S5_LB, tm // S5_CHUNK, S5_BIG), lambda i, j: (i, 0, j, 0)))
    return pl.pallas_call(
        functools.partial(_inproj0_kernel, mod_row=mod_row, rope=rope),
        out_shape=out_shape,
        grid=(b, n // tm),
        in_specs=in_specs,
        out_specs=out_specs,
        scratch_shapes=[pltpu.VMEM((S5_LB, tm, LANE), F32)],
        compiler_params=_cp("parallel", "parallel"),
        name="inproj0_lat" if rope else "inproj0_ctx",
    )(*args)


def _ret_kernel(lg_ref, ql, kl, vl, gl, qc, kc, vc, gc, rl, rc, st_s, *, ncl, ncc, unroll):
    c = RET_CHUNK
    dk2 = 2 * RET_DK
    nt = ncc + ncl
    p = pl.program_id(1)
    h_a = 2 * p
    lgf_a = lg_ref[pl.ds(h_a, 1), :]
    lgf_b = lg_ref[pl.ds(h_a + 1, 1), :]
    lgb_a = lg_ref[pl.ds(RET_HEADS + h_a, 1), :]
    lgb_b = lg_ref[pl.ds(RET_HEADS + h_a + 1, 1), :]
    lane = lax.broadcasted_iota(jnp.int32, (1, 2 * RET_DK), 1)
    is_a = lane < RET_DK
    lgf_lane = jnp.where(is_a, lgf_a, lgf_b)
    lgb_lane = jnp.where(is_a, lgb_a, lgb_b)
    ri = lax.broadcasted_iota(jnp.int32, (c, c), 0).astype(F32)
    ci = lax.broadcasted_iota(jnp.int32, (c, c), 1).astype(F32)
    diff = ri - ci

    def dmat(lgf, lgb):
        fwd = jnp.exp(jnp.maximum(diff, 0.0) * lgf)
        bwd = jnp.exp(jnp.maximum(-diff, 0.0) * lgb)
        return jnp.where(diff > 0, fwd, jnp.where(diff < 0, bwd, 2.0))

    d_a = dmat(lgf_a, lgb_a)
    d_b = dmat(lgf_b, lgb_b)
    rowp = lax.broadcasted_iota(jnp.int32, (c, dk2), 0).astype(F32)
    qd = jnp.concatenate([jnp.exp((rowp + 1.0) * lgf_lane), jnp.exp((c - rowp) * lgb_lane)], axis=1)
    kd = jnp.concatenate([jnp.exp((c - 1.0 - rowp) * lgf_lane), jnp.exp(rowp * lgb_lane)], axis=1)
    rowk = lax.broadcasted_iota(jnp.int32, (dk2, 2 * RET_DV), 0)
    cd_f = jnp.exp(c * jnp.where(rowk < RET_DK, lgf_a[:, :1], lgf_b[:, :1]))
    cd_b = jnp.exp(c * jnp.where(rowk < RET_DK, lgb_a[:, :1], lgb_b[:, :1]))
    mask2 = jnp.concatenate([is_a, is_a], axis=1)

    def increment(k, v, slot):
        kk = jnp.concatenate([k, k], axis=1).astype(F32) * kd
        st_s[slot] = _dot_tn(kk.astype(BF16), v)

    for n in range(ncc):
        increment(kc[0, n * c:(n + 1) * c, :], vc[0, n * c:(n + 1) * c, :], n)

    def inc_body(i, carry):
        for j in range(unroll):
            n = i * unroll + j
            sl = pl.ds(pl.multiple_of(n * c, c), c)
            increment(kl[0, sl, :], vl[0, sl, :], ncc + n)
        return carry

    lax.fori_loop(0, ncl // unroll, inc_body, 0)

    def rec_body(t, carry):
        sf, sb = carry
        u = jnp.where(t < ncc, ncc - 1 - t, nt - 1 - (t - ncc))
        inc_f = st_s[t, 0:dk2, :]
        inc_b = st_s[u, dk2:2 * dk2, :]
        st_s[t, 0:dk2, :] = sf
        st_s[u, dk2:2 * dk2, :] = sb
        return cd_f * sf + inc_f, cd_b * sb + inc_b

    zero = jnp.zeros((dk2, 2 * RET_DV), F32)
    lax.fori_loop(0, nt, rec_body, (zero, zero))

    def output(q, k, v, g, slot, out_ref, st):
        q2 = (jnp.concatenate([q, q], axis=1).astype(F32) * qd).astype(BF16)
        s_n = st_s[slot].astype(BF16)
        for keep_a, dm, cs in ((True, d_a, 0), (False, d_b, RET_DV)):
            m1 = is_a if keep_a else jnp.logical_not(is_a)
            m2 = mask2 if keep_a else jnp.logical_not(mask2)
            att = _dot_nt(jnp.where(m1, q, jnp.zeros_like(q)), k) * dm
            o = (_dot(att.astype(BF16), v[:, cs:cs + RET_DV])
                 + _dot(jnp.where(m2, q2, jnp.zeros_like(q2)), s_n[:, cs:cs + RET_DV]))
            o = _rms(o) * g[:, cs:cs + RET_DV].astype(F32)
            out_ref[0, pl.ds(st, c), cs:cs + RET_DV] = o.astype(BF16)

    for n in range(ncc):
        sl = slice(n * c, (n + 1) * c)
        output(qc[0, sl, :], kc[0, sl, :], vc[0, sl, :], gc[0, sl, :], n, rc, n * c)

    def out_body(i, carry):
        for j in range(unroll):
            n = i * unroll + j
            st = pl.multiple_of(n * c, c)
            sl = pl.ds(st, c)
            output(ql[0, sl, :], kl[0, sl, :], vl[0, sl, :], gl[0, sl, :], ncc + n, rl, st)
        return carry

    lax.fori_loop(0, ncl // unroll, out_body, 0)


def _retention(lg_rows, q_l, k_l, v_l, g_l, q_c, k_c, v_c, g_c):
    b, n, _ = q_l.shape
    nc = q_c.shape[1]
    ncl, ncc = n // RET_CHUNK, nc // RET_CHUNK
    pairs = RET_HEADS // 2
    unroll = 8 if ncl % 8 == 0 else 1

    def spec(rows, width):
        return pl.BlockSpec((1, rows, width), lambda i, p: (i, 0, p))

    return pl.pallas_call(
        functools.partial(_ret_kernel, ncl=ncl, ncc=ncc, unroll=unroll),
        out_shape=[jax.ShapeDtypeStruct((b, n, RET_WIDTH), BF16),
                   jax.ShapeDtypeStruct((b, nc, RET_WIDTH), BF16)],
        grid=(b, pairs),
        in_specs=[pl.BlockSpec((2 * RET_HEADS, 2 * RET_DK), lambda i, p: (0, 0)),
                  spec(n, 2 * RET_DK), spec(n, 2 * RET_DK), spec(n, 2 * RET_DV), spec(n, 2 * RET_DV),
                  spec(nc, 2 * RET_DK), spec(nc, 2 * RET_DK), spec(nc, 2 * RET_DV), spec(nc, 2 * RET_DV)],
        out_specs=[spec(n, 2 * RET_DV), spec(nc, 2 * RET_DV)],
        scratch_shapes=[pltpu.VMEM((ncl + ncc, 4 * RET_DK, 2 * RET_DV), F32)],
        compiler_params=_cp("parallel", "parallel"),
        name="retention",
    )(lg_rows, q_l, k_l, v_l, g_l, q_c, k_c, v_c, g_c)


def _dot_hi(a, b, contract=(1, 0)):
    dims = (((contract[0],), (contract[1],)), ((), ()))
    return lax.dot_general(a, b, dims, preferred_element_type=F32, precision=lax.Precision.HIGHEST)


def _dot_sel(a, b, contract, data):
    dims = (((contract[0],), (contract[1],)), ((), ()))
    x = (a, b)[data]
    hi = x.astype(BF16)
    lo = (x - hi.astype(F32)).astype(BF16)
    dd = lambda piece: lax.dot_general(*((piece, b) if data == 0 else (a, piece)), dims, preferred_element_type=F32)
    return dd(hi) + dd(lo)


def _s5_prep_kernel(ar_row, ai_row, ldt, btr, bti, ctr, cti, rm, rwin, rwout, abig):
    t, g, p, kp = S5_CHUNK, S5_GROUP, S5_STATE, S5_POW
    row = S5_ROW
    i0 = lambda shape: lax.broadcasted_iota(jnp.int32, shape, 0)
    i1 = lambda shape: lax.broadcasted_iota(jnp.int32, shape, 1)
    f32 = lambda m: jnp.where(m, 1.0, 0.0).astype(BF16)
    s_of_r = i0((row, kp)) // g
    k_of_l = i1((row, kp))
    sel_rows = (f32(k_of_l == t - 1 - s_of_r), f32(k_of_l == s_of_r))
    t_of_c = i1((kp, row)) // g
    k_of_s = i0((kp, row))
    sel_out = (f32(k_of_s == t_of_c + 1), f32(k_of_s == t - t_of_c))
    sel_lag = (f32(k_of_s == t_of_c), f32(k_of_s == t - 1 - t_of_c))
    tile_l = f32(i1((g, row)) % g == i0((g, row)))
    tile_r = f32(i0((row, g)) % g == i1((row, g)))
    lane = i1((g, row))
    k_col = i0((kp, 1)).astype(F32)
    k_row = i1((1, kp)).astype(F32)
    first = i0((8, 1)) == 0

    def outer(a, k):
        a8 = jnp.where(first, jnp.broadcast_to(a, (8, a.shape[1])), 0.0)
        return _dot_hi(a8, jnp.broadcast_to(k, (8, k.shape[1])), (0, 0))

    for q in range(S5_GPB):
        lags = []
        for d in range(2):
            dt = jnp.exp(ldt[d, q])
            are_r, aim_r = ar_row[d, q], ai_row[d, q]
            mag = jnp.exp(are_r * dt)
            ang = aim_r * dt
            nr, ni = mag * jnp.cos(ang) - 1.0, mag * jnp.sin(ang)
            den = jnp.square(are_r) + jnp.square(aim_r)
            fr = (nr * are_r + ni * aim_r) / den
            fi = (ni * are_r - nr * aim_r) / den
            pm = jnp.exp(k_col * (are_r * dt))
            pa = k_col * ang
            pk_re, pk_im = pm * jnp.cos(pa), pm * jnp.sin(pa)
            pmt = jnp.exp(outer(are_r * dt, k_row))
            pat = outer(ang, k_row)
            pt_re, pt_im = pmt * jnp.cos(pat), pmt * jnp.sin(pat)
            bt_re, bt_im = _dot_sel(tile_r, btr[d, q], (1, 1), 1), _dot_sel(tile_r, bti[d, q], (1, 1), 1)
            bb_re = fr * bt_re - fi * bt_im
            bb_im = fr * bt_im + fi * bt_re
            pr_re, pr_im = _dot_sel(sel_rows[d], pk_re, (1, 0), 1), _dot_sel(sel_rows[d], pk_im, (1, 0), 1)
            w_re = pr_re * bb_re - pr_im * bb_im
            w_im = pr_re * bb_im + pr_im * bb_re
            for s in range(t):
                rows = slice(s * LANE + q * g, s * LANE + (q + 1) * g)
                rwin[0, rows, d * p:(d + 1) * p] = w_re[s * g:(s + 1) * g].astype(BF16)
                rwin[0, rows, (2 + d) * p:(3 + d) * p] = w_im[s * g:(s + 1) * g].astype(BF16)
            ct_re, ct_im = _dot_sel(ctr[d, q], tile_l, (0, 0), 0), _dot_sel(cti[d, q], tile_l, (0, 0), 0)

            def c_pow(sel):
                pc_re, pc_im = _dot_sel(pt_re, sel, (1, 0), 0), _dot_sel(pt_im, sel, (1, 0), 0)
                return ct_re * pc_re - ct_im * pc_im, ct_re * pc_im + ct_im * pc_re

            o_re, o_im = c_pow(sel_out[d])
            r0 = d * S5_GPB * p + q * p
            rwout[0, r0:r0 + p, :] = o_re.astype(BF16)
            rwout[0, S5_HALF + r0:S5_HALF + r0 + p, :] = (-o_im).astype(BF16)
            l_re, l_im = c_pow(sel_lag[d])
            lags.append(_dot_hi(bb_re[0:g], l_re) - _dot_hi(bb_im[0:g], l_im))
            abig[0, 0:1, r0:r0 + p] = pk_re[t:t + 1, :]
            abig[0, 1:2, r0:r0 + p] = pk_im[t:t + 1, :]
        for s in range(t):
            fwd = jnp.where(lane >= g * s, pltpu.roll(lags[0], g * s, 1), 0.0)
            bwd = jnp.where(lane < g * (s + 1), pltpu.roll(lags[1], (row - g * (t - 1 - s)) % row, 1), 0.0)
            rm[0, s * LANE + q * g:s * LANE + (q + 1) * g, :] = (fwd + bwd).astype(BF16)


def _s5_prep(a_re, a_im, log_dt, b_re, b_im, c_re, c_im):
    gg, p, g = S5_GROUPS, S5_STATE, S5_GROUP
    f = lambda x: x.astype(F32)
    args = (f(a_re).reshape(2, gg, 1, p), f(a_im).reshape(2, gg, 1, p),
            f(log_dt).reshape(2, gg, 1, 1),
            f(b_re), f(b_im), f(c_re), f(c_im))
    spec = lambda r, c: pl.BlockSpec((2, S5_GPB, r, c), lambda j: (0, j, 0, 0))
    out = lambda r, c: pl.BlockSpec((1, r, c), lambda j: (j, 0, 0))
    return pl.pallas_call(
        _s5_prep_kernel,
        out_shape=[jax.ShapeDtypeStruct((S5_LB, S5_BIG, S5_ROW), BF16),
                   jax.ShapeDtypeStruct((S5_LB, S5_BIG, S5_ROW), BF16),
                   jax.ShapeDtypeStruct((S5_LB, 2 * S5_HALF, S5_ROW), BF16),
                   jax.ShapeDtypeStruct((S5_LB, 2, S5_HALF), F32)],
        grid=(S5_LB,),
        in_specs=[spec(1, p), spec(1, p), spec(1, 1),
                  spec(p, g), spec(p, g), spec(g, p), spec(g, p)],
        out_specs=[out(S5_BIG, S5_ROW), out(S5_BIG, S5_ROW), out(2 * S5_HALF, S5_ROW), out(2, S5_HALF)],
        compiler_params=_cp("parallel"),
        name="s5_prep",
    )(*args)


def _s5_expand(r_ref, col_unit, col_block, row_unit):
    n = S5_BIG
    a = lax.broadcasted_iota(jnp.int32, (S5_ROW, n), 0)
    c = lax.broadcasted_iota(jnp.int32, (S5_ROW, n), 1)
    e = jnp.where((a // col_unit == c // col_block) & (a % col_unit == c % col_unit), 1.0, 0.0).astype(BF16)
    x = _dot(r_ref[0], e)
    rq = (lax.broadcasted_iota(jnp.int32, (n, n), 0) // row_unit) % S5_GPB
    cq = (lax.broadcasted_iota(jnp.int32, (n, n), 1) // col_unit) % S5_GPB
    return jnp.where(rq == cq, x, 0.0).astype(BF16)


def _s5_kernel(ul, uc, rm_ref, rwin_ref, rwout_ref, a_ref, yl, yc, x_s, s_s, m_s, win_s, wout_s, *, ncl, ncc):
    hw = S5_HALF
    hh = hw // 2

    @pl.when(pl.program_id(1) == 0)
    def _():
        m_s[...] = _s5_expand(rm_ref, S5_GROUP, LANE, S5_GROUP)
        win_s[...] = _s5_expand(rwin_ref, S5_STATE, S5_GPB * S5_STATE, S5_GROUP)
        wout_s[...] = _s5_expand(rwout_ref, S5_GROUP, LANE, S5_STATE)

    zc = uc[0, 0]
    zl = ul[0, 0]
    x_s[0:ncc, :] = _dot(zc, win_s[...])
    x_s[ncc:ncc + ncl, :] = _dot(zl, win_s[...])
    a_re = a_ref[0, 0:1, :]
    a_im = a_ref[0, 1:2, :]

    def segment(base, n, carry):
        s_re, s_im = carry
        for i in range(n):
            rf = slice(base + i, base + i + 1)
            rb = slice(base + n - 1 - i, base + n - i)
            s_s[rf, 0:hh] = s_re[:, 0:hh]
            s_s[rb, hh:hw] = s_re[:, hh:hw]
            s_s[rf, hw:hw + hh] = s_im[:, 0:hh]
            s_s[rb, hw + hh:2 * hw] = s_im[:, hh:hw]
            x_re = jnp.concatenate([x_s[rf, 0:hh], x_s[rb, hh:hw]], axis=-1)
            x_im = jnp.concatenate([x_s[rf, hw:hw + hh], x_s[rb, hw + hh:2 * hw]], axis=-1)
            s_re, s_im = a_re * s_re - a_im * s_im + x_re, a_re * s_im + a_im * s_re + x_im
        return s_re, s_im

    zero = jnp.zeros((1, hw), F32)
    carry = segment(0, ncc, (zero, zero))
    segment(ncc, ncl, carry)
    sp = s_s[...].astype(BF16)
    yc[0, 0] = _dot(zc, m_s[...]) + _dot(sp[0:ncc], wout_s[...])
    yl[0, 0] = _dot(zl, m_s[...]) + _dot(sp[ncc:ncc + ncl], wout_s[...])


def _s5(prep, zl, zc):
    rm, rwin, rwout, a_big = prep
    b, _, ncl, _ = zl.shape
    ncc = zc.shape[2]
    rows = lambda r: pl.BlockSpec((1, 1, r, S5_BIG), lambda j, i: (i, j, 0, 0))
    wspec = lambda r, c: pl.BlockSpec((1, r, c), lambda j, i: (j, 0, 0))
    big = pltpu.VMEM((S5_BIG, S5_BIG), BF16)
    return pl.pallas_call(
        functools.partial(_s5_kernel, ncl=ncl, ncc=ncc),
        out_shape=[jax.ShapeDtypeStruct(zl.shape, F32), jax.ShapeDtypeStruct(zc.shape, F32)],
        grid=(S5_LB, b),
        in_specs=[rows(ncl), rows(ncc), wspec(S5_BIG, S5_ROW), wspec(S5_BIG, S5_ROW),
                  wspec(2 * S5_HALF, S5_ROW), wspec(2, S5_HALF)],
        out_specs=[rows(ncl), rows(ncc)],
        scratch_shapes=[pltpu.VMEM((ncc + ncl, 2 * S5_HALF), F32), pltpu.VMEM((ncc + ncl, 2 * S5_HALF), F32),
                        big, big, big],
        compiler_params=_cp("parallel", "arbitrary"),
        name="s5",
    )(zl, zc, rm, rwin, rwout, a_big)


def _mix_mlp_kernel(*refs, mod_row, s5_merge, final_norm, fb):
    if s5_merge:
        (h_ref, r_ref, y5_ref, u_ref, ds_ref, wg_ref, bg_ref, wo_ref, mod_ref, nm_ref, w1_ref, w2_ref,
         *rest) = refs
    else:
        h_ref, r_ref, wo_ref, mod_ref, nm_ref, w1_ref, w2_ref, *rest = refs
    if final_norm:
        nf_ref, o_ref, *scratch = rest
    else:
        o_ref, *scratch = rest
    row = pl.program_id(0) if mod_row is None else mod_row
    if s5_merge:
        (y_s,) = scratch
        nch = y_s.shape[1] // S5_CHUNK
        for lb in range(S5_LB):
            for s in range(S5_CHUNK):
                y_s[lb, pl.ds(s, nch, stride=S5_CHUNK), :] = y5_ref[0, lb, :, s * LANE:(s + 1) * LANE]
        y5 = jnp.concatenate([y_s[lb] for lb in range(S5_LB)], axis=-1)
        y = jax.nn.gelu(y5 + ds_ref[...] * u_ref[0])
        y = y * jax.nn.sigmoid(_dot(y.astype(BF16), wg_ref[...]) + bg_ref[...])
        mix = _dot(r_ref[0], wo_ref[0:RET_WIDTH, :]) + _dot(y.astype(BF16), wo_ref[RET_WIDTH:D_MODEL, :])
    else:
        mix = _dot(r_ref[0], wo_ref[...])
    h1 = h_ref[0] + _mod_chunk(mod_ref, row, 2) * mix
    xn = _rms(h1) * nm_ref[...]
    xm = (xn * (1.0 + _mod_chunk(mod_ref, row, 4)) + _mod_chunk(mod_ref, row, 3)).astype(BF16)
    acc = None
    for j in range(D_FF // fb):
        a = jnp.square(jnp.maximum(_dot(xm, w1_ref[:, j * fb:(j + 1) * fb]), 0.0)).astype(BF16)
        part = _dot(a, w2_ref[j * fb:(j + 1) * fb, :])
        acc = part if acc is None else acc + part
    h2 = h1 + _mod_chunk(mod_ref, row, 5) * acc
    if final_norm:
        h2 = _rms(h2) * nf_ref[...]
    o_ref[0] = h2


def _mix_mlp(h, r, s5y, u, s5p, wo, mod, layer, nm, w1, w2, nf, mod_row, tm, fb, name):
    b, n, _ = h.shape
    s5_merge = s5y is not None
    final_norm = nf is not None
    one = pl.Buffered(1)
    row_spec = lambda width: pl.BlockSpec((1, tm, width), lambda i, t: (i, t, 0))
    const = lambda shape: pl.BlockSpec(shape, lambda i, t: (0,) * len(shape), pipeline_mode=one)
    in_specs = [row_spec(D_MODEL), row_spec(r.shape[-1])]
    args = [h, r]
    if s5_merge:
        d_skip, w_glu, b_glu = s5p
        z_spec = pl.BlockSpec((1, S5_LB, tm // S5_CHUNK, S5_BIG), lambda i, t: (i, 0, t, 0))
        in_specs += [z_spec, row_spec(S5_WIDTH), const((1, S5_WIDTH)),
                     const((S5_WIDTH, S5_WIDTH)), const((1, S5_WIDTH))]
        args += [s5y, u, d_skip, w_glu, b_glu]
    in_specs += [const((D_MODEL, D_MODEL)),
                 pl.BlockSpec((1, MOD_ROWS, N_MOD * D_MODEL), lambda i, t: (layer, 0, 0), pipeline_mode=one),
                 const((1, D_MODEL)), const((D_MODEL, D_FF)), const((D_FF, D_MODEL))]
    args += [wo, mod, nm, w1, w2]
    if final_norm:
        in_specs.append(const((1, D_MODEL)))
        args.append(nf)
    return pl.pallas_call(
        functools.partial(_mix_mlp_kernel, mod_row=mod_row, s5_merge=s5_merge, final_norm=final_norm, fb=fb),
        out_shape=jax.ShapeDtypeStruct((b, n, D_MODEL), F32),
        grid=(b, n // tm),
        in_specs=in_specs,
        out_specs=row_spec(D_MODEL),
        scratch_shapes=[pltpu.VMEM((S5_LB, tm, LANE), F32)] if s5_merge else [],
        compiler_params=_cp("parallel", "parallel"),
        name=name,
    )(*args)


def _hgrn_lower_bounds(lbl_ref, layer):
    out = []
    for d in range(2):
        z = [lbl_ref[d, k:k + 1, :] for k in range(DEPTH)]
        zmax = functools.reduce(jnp.maximum, z)
        e = [jnp.exp(v - zmax) for v in z]
        tot = functools.reduce(lambda a, b_: a + b_, e)
        lb = jnp.zeros_like(tot)
        for k in range(1, layer + 1):
            lb = lb + e[k] / tot
        out.append(lb)
    return out


def _inproj1_kernel(h_ref, mod_ref, ng_ref, lbl_ref, w_ref, *out_refs, mod_row, layer, latent):
    row = pl.program_id(0) if mod_row is None else mod_row
    xn = _rms(h_ref[0]) * ng_ref[...]
    xm = (xn * (1.0 + _mod_chunk(mod_ref, row, 1)) + _mod_chunk(mod_ref, row, 0)).astype(BF16)
    col = lambda k: _dot(xm, w_ref[:, k * D_MODEL:(k + 1) * D_MODEL])
    lbs = _hgrn_lower_bounds(lbl_ref, layer)
    refs = list(out_refs)
    if latent:
        refs.pop(0)[0] = col(0).astype(BF16)
    for d in range(2):
        t = (1.0 - lbs[d]) * jax.nn.sigmoid(col(1 + d))
        refs.pop(0)[0] = jnp.log(lbs[d] + t)
        refs.pop(0)[0] = ((1.0 - lbs[d]) - t).astype(BF16)
    refs.pop(0)[0] = col(3).astype(BF16)
    if latent:
        refs.pop(0)[0] = jax.nn.silu(col(4)).astype(BF16)


def _inproj1(h, mod, layer, ng, lb_logits, w, latent, mod_row, tm, name):
    b, n, _ = h.shape
    one = pl.Buffered(1)
    row_spec = pl.BlockSpec((1, tm, D_MODEL), lambda i, t: (i, t, 0))
    dtypes = ([BF16] if latent else []) + [F32, BF16, F32, BF16, BF16] + ([BF16] if latent else [])
    return pl.pallas_call(
        functools.partial(_inproj1_kernel, mod_row=mod_row, layer=layer, latent=latent),
        out_shape=[jax.ShapeDtypeStruct((b, n, D_MODEL), dt) for dt in dtypes],
        grid=(b, n // tm),
        in_specs=[row_spec,
                  pl.BlockSpec((1, MOD_ROWS, N_MOD * D_MODEL), lambda i, t: (layer, 0, 0), pipeline_mode=one),
                  pl.BlockSpec((1, D_MODEL), lambda i, t: (0, 0), pipeline_mode=one),
                  pl.BlockSpec(lb_logits.shape, lambda i, t: (0, 0, 0), pipeline_mode=one),
                  pl.BlockSpec(w.shape, lambda i, t: (0, 0), pipeline_mode=one)],
        out_specs=[row_spec] * len(dtypes),
        compiler_params=_cp("parallel", "parallel"),
        name=name,
    )(h, mod, ng, lb_logits, w)


def _cumsum_mm(tri, x):
    acc = None
    r = x
    for i in range(HG_SPLIT):
        p = r.astype(BF16)
        acc = _dot(tri, p) if acc is None else acc + _dot(tri, p)
        if i + 1 < HG_SPLIT:
            r = r - p.astype(F32)
    return acc


def _hgrn_kernel(ng_ref, ql, lffl, kfl, lfbl, kbl, il, sgl, lffc, kfc, lfbc, kbc, ic, o_ref,
                 qin_s, att_s, kv_s, et_s, kvc_s, etc_s, cum_s, ko_s, qt_s, kt_s, *, nbl, nbc, unroll, out_blocks):
    cb = HG_BLOCK
    mid = cb // 2
    gb = HG_GROUP
    gr = gb * cb
    dk = HG_DK

    ri = lax.broadcasted_iota(jnp.int32, (gr, gr), 0)
    ci = lax.broadcasted_iota(jnp.int32, (gr, gr), 1)
    same = (ri // cb) == (ci // cb)
    rb = lax.broadcasted_iota(jnp.int32, (cb, cb), 0)
    cbi = lax.broadcasted_iota(jnp.int32, (cb, cb), 1)
    tri_l = jnp.where(same & (ri >= ci), 1.0, 0.0).astype(BF16)
    dirs = ((0, rb >= cbi, mid - 1, cb - 1), (1, rb <= cbi, mid, 0))

    def cumsums(lfs, slot):
        pre = _cumsum_mm(tri_l, jnp.concatenate(lfs, axis=-1))
        pre_b = pre[:, dk:].reshape(gb, cb, dk)
        cum_s[slot, 0] = pre[:, :dk]
        cum_s[slot, 1] = (pre_b[:, cb - 1:cb, :] - pre_b).reshape(gr, dk) + lfs[1]

    def operands(slot, kks, q, n0, et_ref):
        for d, keep, ref_row, tot_row in dirs:
            cum = cum_s[slot, d].reshape(gb, cb, dk)
            kk = kks[d].astype(F32).reshape(gb, cb, dk)
            ref = cum[:, ref_row:ref_row + 1, :]
            tot = cum[:, tot_row:tot_row + 1, :]
            e = cum - ref
            kt = kk * jnp.exp(-e)
            ko_s[slot, d] = (kt * jnp.exp(tot - ref)).astype(BF16).reshape(gr, dk)
            e_tot = jnp.exp(tot)
            for j in range(gb):
                et_ref[n0 + j, :, d * dk:(d + 1) * dk] = e_tot[j]
            if q is not None:
                qt = q.astype(F32).reshape(gb, cb, dk) * jnp.exp(e)
                qin_s[pl.ds(pl.multiple_of(n0 * cb, gr), gr), d * dk:(d + 1) * dk] = (
                    (qt * jnp.exp(ref)).astype(BF16).reshape(gr, dk))
                qt_s[slot, d] = qt.astype(BF16).reshape(gr, dk)
                kt_s[slot, d] = kt.astype(BF16).reshape(gr, dk)

    def matmuls(slot, v, with_q, n0, kv_ref):
        for d in range(2):
            for j in range(gb):
                rows = slice(j * cb, (j + 1) * cb)
                kv_ref[n0 + j, :, d * dk:(d + 1) * dk] = _dot_tn(v[rows], ko_s[slot, d, rows, :])
        if with_q:
            for j in range(gb):
                rows = slice(j * cb, (j + 1) * cb)
                att = (jnp.where(dirs[0][1], _dot_nt(qt_s[slot, 0, rows, :], kt_s[slot, 0, rows, :]), 0.0)
                       + jnp.where(dirs[1][1], _dot_nt(qt_s[slot, 1, rows, :], kt_s[slot, 1, rows, :]), 0.0))
                att_s[n0 + j] = att.astype(BF16)

    for g in range(nbc // gb):
        sl = slice(g * gr, (g + 1) * gr)
        cumsums((lffc[0, sl, :], lfbc[0, sl, :]), 0)
        operands(0, (kfc[0, sl, :], kbc[0, sl, :]), None, g * gb, etc_s)
        matmuls(0, ic[0, sl, :], False, g * gb, kvc_s)

    ngl = nbl // gb
    rows_of = lambda g: pl.ds(pl.multiple_of(g * gr, gr), gr)

    def lat_cumsums(g, slot):
        cumsums((lffl[0, rows_of(g), :], lfbl[0, rows_of(g), :]), slot)

    def lat_operands(g, slot):
        operands(slot, (kfl[0, rows_of(g), :], kbl[0, rows_of(g), :]), ql[0, rows_of(g), :], g * gb, et_s)

    def lat_matmuls(g, slot):
        matmuls(slot, il[0, rows_of(g), :], True, g * gb, kv_s)

    lat_cumsums(0, 0)
    lat_operands(0, 0)
    lat_cumsums(1, 1)

    def prep_body(i, carry):
        g = 2 * i
        lat_matmuls(g, 0)
        lat_operands(g + 1, 1)
        lat_cumsums(g + 2, 0)
        lat_matmuls(g + 1, 1)
        lat_operands(g + 2, 0)
        lat_cumsums(g + 3, 1)
        return carry

    lax.fori_loop(0, ngl // 2 - 1, prep_body, 0)
    lat_matmuls(ngl - 2, 0)
    lat_operands(ngl - 1, 1)
    lat_matmuls(ngl - 1, 1)

    lane = lax.broadcasted_iota(jnp.int32, (dk, 2 * dk), 1)
    is_f = lane < dk
    st = jnp.zeros((dk, 2 * dk), F32)
    for n in range(nbc):
        m = nbc - 1 - n
        st = (st * jnp.where(is_f[:1], etc_s[n], etc_s[m]) + jnp.where(is_f, kvc_s[n], kvc_s[m]))

    def rec_body(t, st):
        u = nbl - 1 - t
        inc = jnp.where(is_f, kv_s[t], kv_s[u])
        dec = jnp.where(is_f[:1], et_s[t], et_s[u])
        kv_s[t, :, 0:dk] = st[:, 0:dk]
        kv_s[u, :, dk:2 * dk] = st[:, dk:2 * dk]
        return st * dec + inc

    lax.fori_loop(0, nbl, rec_body, st, unroll=unroll)

    def out_body(i, carry):
        for j in range(out_blocks):
            n = i * out_blocks + j
            sl = pl.ds(pl.multiple_of(n * cb, cb), cb)
            o = _dot(att_s[n], il[0, sl, :]) + _dot_nt(qin_s[sl, :], kv_s[n].astype(BF16))
            o = _rms(o) * ng_ref[...] * sgl[0, sl, :].astype(F32)
            o_ref[0, sl, :] = o.astype(BF16)
        return carry

    lax.fori_loop(0, nbl // out_blocks, out_body, 0)


def _hgrn(norm_g, q_l, lff_l, kf_l, lfb_l, kb_l, i_l, sg_l, lff_c, kf_c, lfb_c, kb_c, i_c):
    b, n, _ = q_l.shape
    nc = lff_c.shape[1]
    nbl, nbc = n // HG_BLOCK, nc // HG_BLOCK
    out_blocks = min(16, nbl)
    assert nbl % (2 * HG_GROUP) == 0 and nbc % HG_GROUP == 0 and nbl % out_blocks == 0
    spec = lambda rows: pl.BlockSpec((1, rows, HG_DK), lambda i, h: (i, 0, h))
    slot = lambda dt: pltpu.VMEM((2, 2, HG_GROUP * HG_BLOCK, HG_DK), dt)
    return pl.pallas_call(
        functools.partial(_hgrn_kernel, nbl=nbl, nbc=nbc, unroll=2, out_blocks=out_blocks),
        out_shape=jax.ShapeDtypeStruct((b, n, D_MODEL), BF16),
        grid=(b, HG_HEADS),
        in_specs=[pl.BlockSpec((1, HG_DK), lambda i, h: (0, 0))] + [spec(n)] * 7 + [spec(nc)] * 5,
        out_specs=spec(n),
        scratch_shapes=[pltpu.VMEM((n, 2 * HG_DK), BF16),
                        pltpu.VMEM((nbl, HG_BLOCK, HG_BLOCK), BF16),
                        pltpu.VMEM((nbl, HG_DK, 2 * HG_DK), F32),
                        pltpu.VMEM((nbl, 1, 2 * HG_DK), F32),
                        pltpu.VMEM((nbc, HG_DK, 2 * HG_DK), F32),
                        pltpu.VMEM((nbc, 1, 2 * HG_DK), F32),
                        slot(F32), slot(BF16), slot(BF16), slot(BF16)],
        compiler_params=_cp("parallel", "parallel"),
        name="hgrn2",
    )(norm_g, q_l, lff_l, kf_l, lfb_l, kb_l, i_l, sg_l, lff_c, kf_c, lfb_c, kb_c, i_c)


def _rope_tables(n_tok):
    tok = jnp.arange(n_tok, dtype=jnp.int32)[:, None]
    row = (tok // GRID_W).astype(F32)
    col = (tok % GRID_W).astype(F32)
    n_freq = RET_DK // 4
    lane = jnp.arange(RET_QK, dtype=jnp.int32)[None, :]
    j = lane % (2 * n_freq)
    inv = ROPE_BASE ** (-(j % n_freq).astype(F32) / n_freq)
    ang = jnp.where(j < n_freq, row, col) * inv
    sign = jnp.where(lane % RET_DK < RET_DK // 2, -1.0, 1.0)
    return jnp.cos(ang), jnp.sin(ang) * sign


def kernel(x, c, ctx, c_ctx, w_mod, b_mod, norm_mix, norm_mlp, w_mlp_in, w_mlp_out, ab_w_in, ab_w_out, ret_logit, s5_a_re, s5_a_im, s5_log_dt, s5_b_re, s5_b_im, s5_c_re, s5_c_im, s5_d, s5_w_glu, s5_b_glu, hg_w_in, hg_w_out, hg_lb_logits, hg_norm, norm_final):
    b, n, d = x.shape
    nc = ctx.shape[1]
    assert d == D_MODEL and b + 1 <= MOD_ROWS and w_mod.shape[0] == DEPTH == 2
    assert n % 512 == 0 and nc % 256 == 0 and n % GRID_W == 0
    ctx_row = b
    tm_l, tm_c = 512, 256

    cc = jnp.zeros((MOD_ROWS, d), F32).at[:b].set(c).at[b].set(c_ctx)
    mod = _adaln(cc, w_mod, b_mod)

    row2 = lambda a: a.reshape(1, -1)
    w_in0 = ab_w_in[0].astype(BF16)
    cos, sin = _rope_tables(n)
    ng0 = row2(norm_mix[0])
    q_l, k_l, v_l, u_l, uz_l, g_l = _inproj0(x, mod, 0, ng0, w_in0, cos, sin, None, tm_l)
    q_c, k_c, v_c, u_c, uz_c, g_c = _inproj0(ctx, mod, 0, ng0, w_in0, None, None, ctx_row, tm_c)

    log_gamma = jax.nn.log_sigmoid(ret_logit[0].astype(F32))
    lg_rows = jnp.broadcast_to(log_gamma.reshape(2 * RET_HEADS, 1), (2 * RET_HEADS, 2 * RET_DK))
    r_l, r_c = _retention(lg_rows, q_l, k_l, v_l, g_l, q_c, k_c, v_c, g_c)

    s5_ops = _s5_prep(s5_a_re[0], s5_a_im[0], s5_log_dt[0], s5_b_re[0], s5_b_im[0], s5_c_re[0], s5_c_im[0])
    y5_l, y5_c = _s5(s5_ops, uz_l, uz_c)

    s5p = (row2(s5_d[0]), s5_w_glu[0].astype(BF16), row2(s5_b_glu[0]))
    wo0 = ab_w_out[0].astype(BF16)
    w1_0, w2_0 = w_mlp_in[0].astype(BF16), w_mlp_out[0].astype(BF16)
    nm0 = row2(norm_mlp[0])
    h_l = _mix_mlp(x, r_l, y5_l, u_l, s5p, wo0, mod, 0, nm0, w1_0, w2_0, None, None, tm_l, 1024, "mix_mlp0_lat")
    h_c = _mix_mlp(ctx, r_c, y5_c, u_c, s5p, wo0, mod, 0, nm0, w1_0, w2_0, None, ctx_row, tm_c, 1024, "mix_mlp0_ctx")

    w_in1 = hg_w_in[0].astype(BF16)
    ng1 = row2(norm_mix[1])
    lat1 = _inproj1(h_l, mod, 1, ng1, hg_lb_logits, w_in1, True, None, tm_l, "inproj1_lat")
    ctx1 = _inproj1(h_c, mod, 1, ng1, hg_lb_logits, w_in1, False, ctx_row, tm_c, "inproj1_ctx")
    o1 = _hgrn(row2(hg_norm[0]), *lat1, *ctx1)
    return _mix_mlp(h_l, o1, None, None, None, hg_w_out[0].astype(BF16), mod, 1, row2(norm_mlp[1]),
                    w_mlp_in[1].astype(BF16), w_mlp_out[1].astype(BF16), row2(norm_final), None, tm_l, 1024,
                    "mix_mlp1_lat")
```

```python
import functools

import jax
import jax.numpy as jnp
from jax import lax
from jax.experimental import pallas as pl
from jax.experimental.pallas import tpu as pltpu

F32 = jnp.float32
BF16 = jnp.bfloat16

D_MODEL = 1024
DEPTH = 2
GRID_W = 64
EPS = 1e-6
ROPE_BASE = 10000.0
N_MOD = 6
RET_HEADS = 4
RET_DK = 64
RET_DV = 128
RET_QK = RET_HEADS * RET_DK
RET_WIDTH = RET_HEADS * RET_DV
RET_CHUNK = 128
S5_WIDTH = D_MODEL - RET_WIDTH
S5_GROUP = 16
S5_GROUPS = S5_WIDTH // S5_GROUP
S5_STATE = 64
S5_CHUNK = 16
S5_ROW = S5_CHUNK * S5_GROUP
LANE = 128
S5_LB = S5_WIDTH // LANE
S5_GPB = LANE // S5_GROUP
S5_BIG = S5_CHUNK * LANE
S5_HALF = S5_GPB * 2 * S5_STATE
S5_POW = 32
AB_IN = 2 * RET_QK + 2 * RET_WIDTH + S5_WIDTH
HG_HEADS = 8
HG_DK = D_MODEL // HG_HEADS
HG_BLOCK = 64
HG_GROUP = 4
HG_SPLIT = 2
D_FF = 4 * D_MODEL
MOD_ROWS = 16

VMEM_LIMIT_BYTES = 56 * 1024 * 1024


def _cp(*sem):
    return pltpu.CompilerParams(dimension_semantics=sem, vmem_limit_bytes=VMEM_LIMIT_BYTES)


def _dot(a, b):
    return jnp.dot(a, b, preferred_element_type=F32)


def _dot_nt(a, b):
    return lax.dot_general(a, b, (((1,), (1,)), ((), ())), preferred_element_type=F32)


def _dot_tn(a, b):
    return lax.dot_general(a, b, (((0,), (0,)), ((), ())), preferred_element_type=F32)


def _rms(x):
    return x * lax.rsqrt(jnp.mean(x * x, axis=-1, keepdims=True) + EPS)


def _mod_chunk(mod_ref, row, i):
    return mod_ref[0, pl.ds(row, 1), i * D_MODEL:(i + 1) * D_MODEL]


def _adaln_kernel(cc_ref, w_ref, b_ref, o_ref):
    s = jax.nn.silu(cc_ref[...]).astype(BF16)
    o_ref[0] = _dot(s, w_ref[0].astype(BF16)) + b_ref[0]


def _adaln(cc, w_mod, b_mod):
    bn = 1536
    n = N_MOD * D_MODEL
    return pl.pallas_call(
        _adaln_kernel,
        out_shape=jax.ShapeDtypeStruct((DEPTH, MOD_ROWS, n), F32),
        grid=(DEPTH, n // bn),
        in_specs=[
            pl.BlockSpec((MOD_ROWS, D_MODEL), lambda l, j: (0, 0)),
            pl.BlockSpec((1, D_MODEL, bn), lambda l, j: (l, 0, j)),
            pl.BlockSpec((1, 1, bn), lambda l, j: (l, 0, j)),
        ],
        out_specs=pl.BlockSpec((1, MOD_ROWS, bn), lambda l, j: (l, 0, j)),
        compiler_params=_cp("parallel", "parallel"),
        name="adaln",
    )(cc, w_mod, b_mod.reshape(DEPTH, 1, n))


def _rope(t, cos, sin):
    lane = lax.broadcasted_iota(jnp.int32, t.shape, 1)
    first = (lane & (RET_DK // 2)) == 0
    w = t.shape[1]
    swapped = jnp.where(first, pltpu.roll(t, w - RET_DK // 2, 1), pltpu.roll(t, RET_DK // 2, 1))
    return t * cos + swapped * sin


def _inproj0_kernel(*refs, mod_row, rope):
    if rope:
        h_ref, mod_ref, ng_ref, w_ref, cos_ref, sin_ref, q_ref, k_ref, v_ref, u_ref, uz_ref, g_ref, u_s = refs
    else:
        h_ref, mod_ref, ng_ref, w_ref, q_ref, k_ref, v_ref, u_ref, uz_ref, g_ref, u_s = refs
    row = pl.program_id(0) if mod_row is None else mod_row
    xn = _rms(h_ref[0]) * ng_ref[...]
    xm = (xn * (1.0 + _mod_chunk(mod_ref, row, 1)) + _mod_chunk(mod_ref, row, 0)).astype(BF16)
    y = _dot(xm, w_ref[...])
    q = y[:, 0:RET_QK]
    k = y[:, RET_QK:2 * RET_QK]
    if rope:
        q = _rope(q, cos_ref[...], sin_ref[...])
        k = _rope(k, cos_ref[...], sin_ref[...])
    q_ref[0] = q.astype(BF16)
    k_ref[0] = (k * (RET_DK ** -0.5)).astype(BF16)
    c0 = 2 * RET_QK
    v_ref[0] = y[:, c0:c0 + RET_WIDTH].astype(BF16)
    u0 = c0 + RET_WIDTH
    u_ref[0] = y[:, u0:u0 + S5_WIDTH]
    g_ref[0] = jax.nn.silu(y[:, u0 + S5_WIDTH:]).astype(BF16)
    nch = u_s.shape[1] // S5_CHUNK
    for j in range(S5_LB):
        u_s[j] = y[:, u0 + j * LANE:u0 + (j + 1) * LANE]
        for s in range(S5_CHUNK):
            uz_ref[0, j, :, s * LANE:(s + 1) * LANE] = u_s[j, pl.ds(s, nch, stride=S5_CHUNK), :].astype(BF16)


def _inproj0(h, mod, layer, ng, w, cos, sin, mod_row, tm):
    b, n, _ = h.shape
    rope = cos is not None
    row_spec = lambda width: pl.BlockSpec((1, tm, width), lambda i, j: (i, j, 0))
    in_specs = [
        row_spec(D_MODEL),
        pl.BlockSpec((1, MOD_ROWS, N_MOD * D_MODEL), lambda i, j: (layer, 0, 0)),
        pl.BlockSpec((1, D_MODEL), lambda i, j: (0, 0)),
        pl.BlockSpec((D_MODEL, AB_IN), lambda i, j: (0, 0)),
    ]
    args = [h, mod, ng, w]
    if rope:
        in_specs += [pl.BlockSpec((tm, RET_QK), lambda i, j: (j, 0))] * 2
        args += [cos, sin]
    widths = (RET_QK, RET_QK, RET_WIDTH, S5_WIDTH, RET_WIDTH)
    dtypes = (BF16, BF16, BF16, F32, BF16)
    out_shape = [jax.ShapeDtypeStruct((b, n, wd), dt) for wd, dt in zip(widths, dtypes)]
    out_specs = [row_spec(wd) for wd in widths]
    out_shape.insert(4, jax.ShapeDtypeStruct((b, S5_LB, n // S5_CHUNK, S5_BIG), BF16))
    out_specs.insert(4, pl.BlockSpec((1, S5_LB, tm // S5_CHUNK, S5_BIG), lambda i, j: (i, 0, j, 0)))
    return pl.pallas_call(
        functools.partial(_inproj0_kernel, mod_row=mod_row, rope=rope),
        out_shape=out_shape,
        grid=(b, n // tm),
        in_specs=in_specs,
        out_specs=out_specs,
        scratch_shapes=[pltpu.VMEM((S5_LB, tm, LANE), F32)],
        compiler_params=_cp("parallel", "parallel"),
        name="inproj0_lat" if rope else "inproj0_ctx",
    )(*args)


def _ret_kernel(lg_ref, ql, kl, vl, gl, qc, kc, vc, gc, rl, rc, st_s, *, ncl, ncc, unroll):
    c = RET_CHUNK
    dk2 = 2 * RET_DK
    nt = ncc + ncl
    p = pl.program_id(1)
    h_a = 2 * p
    lgf_a = lg_ref[pl.ds(h_a, 1), :]
    lgf_b = lg_ref[pl.ds(h_a + 1, 1), :]
    lgb_a = lg_ref[pl.ds(RET_HEADS + h_a, 1), :]
    lgb_b = lg_ref[pl.ds(RET_HEADS + h_a + 1, 1), :]
    lane = lax.broadcasted_iota(jnp.int32, (1, 2 * RET_DK), 1)
    is_a = lane < RET_DK
    lgf_lane = jnp.where(is_a, lgf_a, lgf_b)
    lgb_lane = jnp.where(is_a, lgb_a, lgb_b)
    ri = lax.broadcasted_iota(jnp.int32, (c, c), 0).astype(F32)
    ci = lax.broadcasted_iota(jnp.int32, (c, c), 1).astype(F32)
    diff = ri - ci

    def dmat(lgf, lgb):
        fwd = jnp.exp(jnp.maximum(diff, 0.0) * lgf)
        bwd = jnp.exp(jnp.maximum(-diff, 0.0) * lgb)
        return jnp.where(diff > 0, fwd, jnp.where(diff < 0, bwd, 2.0))

    d_a = dmat(lgf_a, lgb_a)
    d_b = dmat(lgf_b, lgb_b)
    rowp = lax.broadcasted_iota(jnp.int32, (c, dk2), 0).astype(F32)
    qd = jnp.concatenate([jnp.exp((rowp + 1.0) * lgf_lane), jnp.exp((c - rowp) * lgb_lane)], axis=1)
    kd = jnp.concatenate([jnp.exp((c - 1.0 - rowp) * lgf_lane), jnp.exp(rowp * lgb_lane)], axis=1)
    rowk = lax.broadcasted_iota(jnp.int32, (dk2, 2 * RET_DV), 0)
    cd_f = jnp.exp(c * jnp.where(rowk < RET_DK, lgf_a[:, :1], lgf_b[:, :1]))
    cd_b = jnp.exp(c * jnp.where(rowk < RET_DK, lgb_a[:, :1], lgb_b[:, :1]))
    mask2 = jnp.concatenate([is_a, is_a], axis=1)

    def increment(k, v, slot):
        kk = jnp.concatenate([k, k], axis=1).astype(F32) * kd
        st_s[slot] = _dot_tn(kk.astype(BF16), v)

    for n in range(ncc):
        increment(kc[0, n * c:(n + 1) * c, :], vc[0, n * c:(n + 1) * c, :], n)

    def inc_body(i, carry):
        for j in range(unroll):
            n = i * unroll + j
            sl = pl.ds(pl.multiple_of(n * c, c), c)
            increment(kl[0, sl, :], vl[0, sl, :], ncc + n)
        return carry

    lax.fori_loop(0, ncl // unroll, inc_body, 0)

    def rec_body(t, carry):
        sf, sb = carry
        u = jnp.where(t < ncc, ncc - 1 - t, nt - 1 - (t - ncc))
        inc_f = st_s[t, 0:dk2, :]
        inc_b = st_s[u, dk2:2 * dk2, :]
        st_s[t, 0:dk2, :] = sf
        st_s[u, dk2:2 * dk2, :] = sb
        return cd_f * sf + inc_f, cd_b * sb + inc_b

    zero = jnp.zeros((dk2, 2 * RET_DV), F32)
    lax.fori_loop(0, nt, rec_body, (zero, zero))

    def output(q, k, v, g, slot, out_ref, st):
        q2 = (jnp.concatenate([q, q], axis=1).astype(F32) * qd).astype(BF16)
        s_n = st_s[slot].astype(BF16)
        for keep_a, dm, cs in ((True, d_a, 0), (False, d_b, RET_DV)):
            m1 = is_a if keep_a else jnp.logical_not(is_a)
            m2 = mask2 if keep_a else jnp.logical_not(mask2)
            att = _dot_nt(jnp.where(m1, q, jnp.zeros_like(q)), k) * dm
            o = (_dot(att.astype(BF16), v[:, cs:cs + RET_DV])
                 + _dot(jnp.where(m2, q2, jnp.zeros_like(q2)), s_n[:, cs:cs + RET_DV]))
            o = _rms(o) * g[:, cs:cs + RET_DV].astype(F32)
            out_ref[0, pl.ds(st, c), cs:cs + RET_DV] = o.astype(BF16)

    for n in range(ncc):
        sl = slice(n * c, (n + 1) * c)
        output(qc[0, sl, :], kc[0, sl, :], vc[0, sl, :], gc[0, sl, :], n, rc, n * c)

    def out_body(i, carry):
        for j in range(unroll):
            n = i * unroll + j
            st = pl.multiple_of(n * c, c)
            sl = pl.ds(st, c)
            output(ql[0, sl, :], kl[0, sl, :], vl[0, sl, :], gl[0, sl, :], ncc + n, rl, st)
        return carry

    lax.fori_loop(0, ncl // unroll, out_body, 0)


def _retention(lg_rows, q_l, k_l, v_l, g_l, q_c, k_c, v_c, g_c):
    b, n, _ = q_l.shape
    nc = q_c.shape[1]
    ncl, ncc = n // RET_CHUNK, nc // RET_CHUNK
    pairs = RET_HEADS // 2
    unroll = 8 if ncl % 8 == 0 else 1

    def spec(rows, width):
        return pl.BlockSpec((1, rows, width), lambda i, p: (i, 0, p))

    return pl.pallas_call(
        functools.partial(_ret_kernel, ncl=ncl, ncc=ncc, unroll=unroll),
        out_shape=[jax.ShapeDtypeStruct((b, n, RET_WIDTH), BF16),
                   jax.ShapeDtypeStruct((b, nc, RET_WIDTH), BF16)],
        grid=(b, pairs),
        in_specs=[pl.BlockSpec((2 * RET_HEADS, 2 * RET_DK), lambda i, p: (0, 0)),
                  spec(n, 2 * RET_DK), spec(n, 2 * RET_DK), spec(n, 2 * RET_DV), spec(n, 2 * RET_DV),
                  spec(nc, 2 * RET_DK), spec(nc, 2 * RET_DK), spec(nc, 2 * RET_DV), spec(nc, 2 * RET_DV)],
        out_specs=[spec(n, 2 * RET_DV), spec(nc, 2 * RET_DV)],
        scratch_shapes=[pltpu.VMEM((ncl + ncc, 4 * RET_DK, 2 * RET_DV), F32)],
        compiler_params=_cp("parallel", "parallel"),
        name="retention",
    )(lg_rows, q_l, k_l, v_l, g_l, q_c, k_c, v_c, g_c)


def _dot_hi(a, b, contract=(1, 0)):
    dims = (((contract[0],), (contract[1],)), ((), ()))
    return lax.dot_general(a, b, dims, preferred_element_type=F32, precision=lax.Precision.HIGHEST)


def _dot_sel(a, b, contract, data):
    dims = (((contract[0],), (contract[1],)), ((), ()))
    x = (a, b)[data]
    hi = x.astype(BF16)
    lo = (x - hi.astype(F32)).astype(BF16)
    dd = lambda piece: lax.dot_general(*((piece, b) if data == 0 else (a, piece)), dims, preferred_element_type=F32)
    return dd(hi) + dd(lo)


def _s5_prep_kernel(ar_row, ai_row, ldt, btr, bti, ctr, cti, rm, rwin, rwout, abig):
    t, g, p, kp = S5_CHUNK, S5_GROUP, S5_STATE, S5_POW
    row = S5_ROW
    i0 = lambda shape: lax.broadcasted_iota(jnp.int32, shape, 0)
    i1 = lambda shape: lax.broadcasted_iota(jnp.int32, shape, 1)
    f32 = lambda m: jnp.where(m, 1.0, 0.0).astype(BF16)
    s_of_r = i0((row, kp)) // g
    k_of_l = i1((row, kp))
    sel_rows = (f32(k_of_l == t - 1 - s_of_r), f32(k_of_l == s_of_r))
    t_of_c = i1((kp, row)) // g
    k_of_s = i0((kp, row))
    sel_out = (f32(k_of_s == t_of_c + 1), f32(k_of_s == t - t_of_c))
    sel_lag = (f32(k_of_s == t_of_c), f32(k_of_s == t - 1 - t_of_c))
    tile_l = f32(i1((g, row)) % g == i0((g, row)))
    tile_r = f32(i0((row, g)) % g == i1((row, g)))
    lane = i1((g, row))
    k_col = i0((kp, 1)).astype(F32)
    k_row = i1((1, kp)).astype(F32)
    first = i0((8, 1)) == 0

    def outer(a, k):
        a8 = jnp.where(first, jnp.broadcast_to(a, (8, a.shape[1])), 0.0)
        return _dot_hi(a8, jnp.broadcast_to(k, (8, k.shape[1])), (0, 0))

    for q in range(S5_GPB):
        lags = []
        for d in range(2):
            dt = jnp.exp(ldt[d, q])
            are_r, aim_r = ar_row[d, q], ai_row[d, q]
            mag = jnp.exp(are_r * dt)
            ang = aim_r * dt
            nr, ni = mag * jnp.cos(ang) - 1.0, mag * jnp.sin(ang)
            den = jnp.square(are_r) + jnp.square(aim_r)
            fr = (nr * are_r + ni * aim_r) / den
            fi = (ni * are_r - nr * aim_r) / den
            pm = jnp.exp(k_col * (are_r * dt))
            pa = k_col * ang
            pk_re, pk_im = pm * jnp.cos(pa), pm * jnp.sin(pa)
            pmt = jnp.exp(outer(are_r * dt, k_row))
            pat = outer(ang, k_row)
            pt_re, pt_im = pmt * jnp.cos(pat), pmt * jnp.sin(pat)
            bt_re, bt_im = _dot_sel(tile_r, btr[d, q], (1, 1), 1), _dot_sel(tile_r, bti[d, q], (1, 1), 1)
            bb_re = fr * bt_re - fi * bt_im
            bb_im = fr * bt_im + fi * bt_re
            pr_re, pr_im = _dot_sel(sel_rows[d], pk_re, (1, 0), 1), _dot_sel(sel_rows[d], pk_im, (1, 0), 1)
            w_re = pr_re * bb_re - pr_im * bb_im
            w_im = pr_re * bb_im + pr_im * bb_re
            for s in range(t):
                rows = slice(s * LANE + q * g, s * LANE + (q + 1) * g)
                rwin[0, rows, d * p:(d + 1) * p] = w_re[s * g:(s + 1) * g].astype(BF16)
                rwin[0, rows, (2 + d) * p:(3 + d) * p] = w_im[s * g:(s + 1) * g].astype(BF16)
            ct_re, ct_im = _dot_sel(ctr[d, q], tile_l, (0, 0), 0), _dot_sel(cti[d, q], tile_l, (0, 0), 0)

            def c_pow(sel):
                pc_re, pc_im = _dot_sel(pt_re, sel, (1, 0), 0), _dot_sel(pt_im, sel, (1, 0), 0)
                return ct_re * pc_re - ct_im * pc_im, ct_re * pc_im + ct_im * pc_re

            o_re, o_im = c_pow(sel_out[d])
            r0 = d * S5_GPB * p + q * p
            rwout[0, r0:r0 + p, :] = o_re.astype(BF16)
            rwout[0, S5_HALF + r0:S5_HALF + r0 + p, :] = (-o_im).astype(BF16)
            l_re, l_im = c_pow(sel_lag[d])
            lags.append(_dot_hi(bb_re[0:g], l_re) - _dot_hi(bb_im[0:g], l_im))
            abig[0, 0:1, r0:r0 + p] = pk_re[t:t + 1, :]
            abig[0, 1:2, r0:r0 + p] = pk_im[t:t + 1, :]
        for s in range(t):
            fwd = jnp.where(lane >= g * s, pltpu.roll(lags[0], g * s, 1), 0.0)
            bwd = jnp.where(lane < g * (s + 1), pltpu.roll(lags[1], (row - g * (t - 1 - s)) % row, 1), 0.0)
            rm[0, s * LANE + q * g:s * LANE + (q + 1) * g, :] = (fwd + bwd).astype(BF16)


def _s5_prep(a_re, a_im, log_dt, b_re, b_im, c_re, c_im):
    gg, p, g = S5_GROUPS, S5_STATE, S5_GROUP
    f = lambda x: x.astype(F32)
    args = (f(a_re).reshape(2, gg, 1, p), f(a_im).reshape(2, gg, 1, p),
            f(log_dt).reshape(2, gg, 1, 1),
            f(b_re), f(b_im), f(c_re), f(c_im))
    spec = lambda r, c: pl.BlockSpec((2, S5_GPB, r, c), lambda j: (0, j, 0, 0))
    out = lambda r, c: pl.BlockSpec((1, r, c), lambda j: (j, 0, 0))
    return pl.pallas_call(
        _s5_prep_kernel,
        out_shape=[jax.ShapeDtypeStruct((S5_LB, S5_BIG, S5_ROW), BF16),
                   jax.ShapeDtypeStruct((S5_LB, S5_BIG, S5_ROW), BF16),
                   jax.ShapeDtypeStruct((S5_LB, 2 * S5_HALF, S5_ROW), BF16),
                   jax.ShapeDtypeStruct((S5_LB, 2, S5_HALF), F32)],
        grid=(S5_LB,),
        in_specs=[spec(1, p), spec(1, p), spec(1, 1),
                  spec(p, g), spec(p, g), spec(g, p), spec(g, p)],
        out_specs=[out(S5_BIG, S5_ROW), out(S5_BIG, S5_ROW), out(2 * S5_HALF, S5_ROW), out(2, S5_HALF)],
        compiler_params=_cp("parallel"),
        name="s5_prep",
    )(*args)


def _s5_expand(r_ref, col_unit, col_block, row_unit):
    n = S5_BIG
    a = lax.broadcasted_iota(jnp.int32, (S5_ROW, n), 0)
    c = lax.broadcasted_iota(jnp.int32, (S5_ROW, n), 1)
    e = jnp.where((a // col_unit == c // col_block) & (a % col_unit == c % col_unit), 1.0, 0.0).astype(BF16)
    x = _dot(r_ref[0], e)
    rq = (lax.broadcasted_iota(jnp.int32, (n, n), 0) // row_unit) % S5_GPB
    cq = (lax.broadcasted_iota(jnp.int32, (n, n), 1) // col_unit) % S5_GPB
    return jnp.where(rq == cq, x, 0.0).astype(BF16)


def _s5_kernel(ul, uc, rm_ref, rwin_ref, rwout_ref, a_ref, yl, yc, x_s, m_s, win_s, wout_s, *, ncl, ncc):
    hw = S5_HALF
    hh = hw // 2

    @pl.when(pl.program_id(1) == 0)
    def _():
        m_s[...] = _s5_expand(rm_ref, S5_GROUP, LANE, S5_GROUP)
        win_s[...] = _s5_expand(rwin_ref, S5_STATE, S5_GPB * S5_STATE, S5_GROUP)
        wout_s[...] = _s5_expand(rwout_ref, S5_GROUP, LANE, S5_STATE)

    z = jnp.concatenate([uc[0, 0], ul[0, 0]], axis=0)
    x_s[...] = _dot(z, win_s[...])
    a_re = a_ref[0, 0:1, :]
    a_im = a_ref[0, 1:2, :]

    def segment(base, n, carry):
        s_re, s_im = carry
        for i in range(n):
            rf = slice(base + i, base + i + 1)
            rb = slice(base + n - 1 - i, base + n - i)
            x_re = jnp.concatenate([x_s[rf, 0:hh], x_s[rb, hh:hw]], axis=-1)
            x_im = jnp.concatenate([x_s[rf, hw:hw + hh], x_s[rb, hw + hh:2 * hw]], axis=-1)
            x_s[rf, 0:hh] = s_re[:, 0:hh]
            x_s[rb, hh:hw] = s_re[:, hh:hw]
            x_s[rf, hw:hw + hh] = s_im[:, 0:hh]
            x_s[rb, hw + hh:2 * hw] = s_im[:, hh:hw]
            s_re, s_im = a_re * s_re - a_im * s_im + x_re, a_re * s_im + a_im * s_re + x_im
        return s_re, s_im

    zero = jnp.zeros((1, hw), F32)
    carry = segment(0, ncc, (zero, zero))
    segment(ncc, ncl, carry)
    y = _dot(z, m_s[...]) + _dot(x_s[...].astype(BF16), wout_s[...])
    yc[0, 0] = y[0:ncc]
    yl[0, 0] = y[ncc:ncc + ncl]


def _s5(prep, zl, zc):
    rm, rwin, rwout, a_big = prep
    b, _, ncl, _ = zl.shape
    ncc = zc.shape[2]
    rows = lambda r: pl.BlockSpec((1, 1, r, S5_BIG), lambda j, i: (i, j, 0, 0))
    wspec = lambda r, c: pl.BlockSpec((1, r, c), lambda j, i: (j, 0, 0))
    big = pltpu.VMEM((S5_BIG, S5_BIG), BF16)
    return pl.pallas_call(
        functools.partial(_s5_kernel, ncl=ncl, ncc=ncc),
        out_shape=[jax.ShapeDtypeStruct(zl.shape, F32), jax.ShapeDtypeStruct(zc.shape, F32)],
        grid=(S5_LB, b),
        in_specs=[rows(ncl), rows(ncc), wspec(S5_BIG, S5_ROW), wspec(S5_BIG, S5_ROW),
                  wspec(2 * S5_HALF, S5_ROW), wspec(2, S5_HALF)],
        out_specs=[rows(ncl), rows(ncc)],
        scratch_shapes=[pltpu.VMEM((ncc + ncl, 2 * S5_HALF), F32), big, big, big],
        compiler_params=_cp("parallel", "arbitrary"),
        name="s5",
    )(zl, zc, rm, rwin, rwout, a_big)


def _mix_mlp_kernel(*refs, mod_row, s5_merge, final_norm, fb):
    if s5_merge:
        (h_ref, r_ref, y5_ref, u_ref, ds_ref, wg_ref, bg_ref, wo_ref, mod_ref, nm_ref, w1_ref, w2_ref,
         *rest) = refs
    else:
        h_ref, r_ref, wo_ref, mod_ref, nm_ref, w1_ref, w2_ref, *rest = refs
    if final_norm:
        nf_ref, o_ref, *scratch = rest
    else:
        o_ref, *scratch = rest
    row = pl.program_id(0) if mod_row is None else mod_row
    if s5_merge:
        (y_s,) = scratch
        nch = y_s.shape[1] // S5_CHUNK
        for lb in range(S5_LB):
            for s in range(S5_CHUNK):
                y_s[lb, pl.ds(s, nch, stride=S5_CHUNK), :] = y5_ref[0, lb, :, s * LANE:(s + 1) * LANE]
        y5 = jnp.concatenate([y_s[lb] for lb in range(S5_LB)], axis=-1)
        y = jax.nn.gelu(y5 + ds_ref[...] * u_ref[0])
        y = y * jax.nn.sigmoid(_dot(y.astype(BF16), wg_ref[...]) + bg_ref[...])
        mix = _dot(r_ref[0], wo_ref[0:RET_WIDTH, :]) + _dot(y.astype(BF16), wo_ref[RET_WIDTH:D_MODEL, :])
    else:
        mix = _dot(r_ref[0], wo_ref[...])
    h1 = h_ref[0] + _mod_chunk(mod_ref, row, 2) * mix
    xn = _rms(h1) * nm_ref[...]
    xm = (xn * (1.0 + _mod_chunk(mod_ref, row, 4)) + _mod_chunk(mod_ref, row, 3)).astype(BF16)
    acc = None
    for j in range(D_FF // fb):
        a = jnp.square(jnp.maximum(_dot(xm, w1_ref[:, j * fb:(j + 1) * fb]), 0.0)).astype(BF16)
        part = _dot(a, w2_ref[j * fb:(j + 1) * fb, :])
        acc = part if acc is None else acc + part
    h2 = h1 + _mod_chunk(mod_ref, row, 5) * acc
    if final_norm:
        h2 = _rms(h2) * nf_ref[...]
    o_ref[0] = h2


def _mix_mlp(h, r, s5y, u, s5p, wo, mod, layer, nm, w1, w2, nf, mod_row, tm, fb, name):
    b, n, _ = h.shape
    s5_merge = s5y is not None
    final_norm = nf is not None
    one = pl.Buffered(1)
    row_spec = lambda width: pl.BlockSpec((1, tm, width), lambda i, t: (i, t, 0))
    const = lambda shape: pl.BlockSpec(shape, lambda i, t: (0,) * len(shape), pipeline_mode=one)
    in_specs = [row_spec(D_MODEL), row_spec(r.shape[-1])]
    args = [h, r]
    if s5_merge:
        d_skip, w_glu, b_glu = s5p
        z_spec = pl.BlockSpec((1, S5_LB, tm // S5_CHUNK, S5_BIG), lambda i, t: (i, 0, t, 0))
        in_specs += [z_spec, row_spec(S5_WIDTH), const((1, S5_WIDTH)),
                     const((S5_WIDTH, S5_WIDTH)), const((1, S5_WIDTH))]
        args += [s5y, u, d_skip, w_glu, b_glu]
    in_specs += [const((D_MODEL, D_MODEL)),
                 pl.BlockSpec((1, MOD_ROWS, N_MOD * D_MODEL), lambda i, t: (layer, 0, 0), pipeline_mode=one),
                 const((1, D_MODEL)), const((D_MODEL, D_FF)), const((D_FF, D_MODEL))]
    args += [wo, mod, nm, w1, w2]
    if final_norm:
        in_specs.append(const((1, D_MODEL)))
        args.append(nf)
    return pl.pallas_call(
        functools.partial(_mix_mlp_kernel, mod_row=mod_row, s5_merge=s5_merge, final_norm=final_norm, fb=fb),
        out_shape=jax.ShapeDtypeStruct((b, n, D_MODEL), F32),
        grid=(b, n // tm),
        in_specs=in_specs,
        out_specs=row_spec(D_MODEL),
        scratch_shapes=[pltpu.VMEM((S5_LB, tm, LANE), F32)] if s5_merge else [],
        compiler_params=_cp("parallel", "parallel"),
        name=name,
    )(*args)


def _hgrn_lower_bounds(lbl_ref, layer):
    out = []
    for d in range(2):
        z = [lbl_ref[d, k:k + 1, :] for k in range(DEPTH)]
        zmax = functools.reduce(jnp.maximum, z)
        e = [jnp.exp(v - zmax) for v in z]
        tot = functools.reduce(lambda a, b_: a + b_, e)
        lb = jnp.zeros_like(tot)
        for k in range(1, layer + 1):
            lb = lb + e[k] / tot
        out.append(lb)
    return out


def _inproj1_kernel(h_ref, mod_ref, ng_ref, lbl_ref, w_ref, *out_refs, mod_row, layer, latent):
    row = pl.program_id(0) if mod_row is None else mod_row
    xn = _rms(h_ref[0]) * ng_ref[...]
    xm = (xn * (1.0 + _mod_chunk(mod_ref, row, 1)) + _mod_chunk(mod_ref, row, 0)).astype(BF16)
    col = lambda k: _dot(xm, w_ref[:, k * D_MODEL:(k + 1) * D_MODEL])
    lbs = _hgrn_lower_bounds(lbl_ref, layer)
    refs = list(out_refs)
    if latent:
        refs.pop(0)[0] = col(0).astype(BF16)
    for d in range(2):
        t = (1.0 - lbs[d]) * jax.nn.sigmoid(col(1 + d))
        refs.pop(0)[0] = jnp.log(lbs[d] + t)
        refs.pop(0)[0] = ((1.0 - lbs[d]) - t).astype(BF16)
    refs.pop(0)[0] = col(3).astype(BF16)
    if latent:
        refs.pop(0)[0] = jax.nn.silu(col(4)).astype(BF16)


def _inproj1(h, mod, layer, ng, lb_logits, w, latent, mod_row, tm, name):
    b, n, _ = h.shape
    one = pl.Buffered(1)
    row_spec = pl.BlockSpec((1, tm, D_MODEL), lambda i, t: (i, t, 0))
    dtypes = ([BF16] if latent else []) + [F32, BF16, F32, BF16, BF16] + ([BF16] if latent else [])
    return pl.pallas_call(
        functools.partial(_inproj1_kernel, mod_row=mod_row, layer=layer, latent=latent),
        out_shape=[jax.ShapeDtypeStruct((b, n, D_MODEL), dt) for dt in dtypes],
        grid=(b, n // tm),
        in_specs=[row_spec,
                  pl.BlockSpec((1, MOD_ROWS, N_MOD * D_MODEL), lambda i, t: (layer, 0, 0), pipeline_mode=one),
                  pl.BlockSpec((1, D_MODEL), lambda i, t: (0, 0), pipeline_mode=one),
                  pl.BlockSpec(lb_logits.shape, lambda i, t: (0, 0, 0), pipeline_mode=one),
                  pl.BlockSpec(w.shape, lambda i, t: (0, 0), pipeline_mode=one)],
        out_specs=[row_spec] * len(dtypes),
        compiler_params=_cp("parallel", "parallel"),
        name=name,
    )(h, mod, ng, lb_logits, w)


def _cumsum_mm(tri, x):
    acc = None
    r = x
    for i in range(HG_SPLIT):
        p = r.astype(BF16)
        acc = _dot(tri, p) if acc is None else acc + _dot(tri, p)
        if i + 1 < HG_SPLIT:
            r = r - p.astype(F32)
    return acc


def _hgrn_kernel(ng_ref, ql, lffl, kfl, lfbl, kbl, il, sgl, lffc, kfc, lfbc, kbc, ic, o_ref,
                 qin_s, att_s, kv_s, et_s, kvc_s, etc_s, cum_s, ko_s, qt_s, kt_s, *, nbl, nbc, unroll, out_blocks):
    cb = HG_BLOCK
    mid = cb // 2
    gb = HG_GROUP
    gr = gb * cb
    dk = HG_DK

    ri = lax.broadcasted_iota(jnp.int32, (gr, gr), 0)
    ci = lax.broadcasted_iota(jnp.int32, (gr, gr), 1)
    same = (ri // cb) == (ci // cb)
    rb = lax.broadcasted_iota(jnp.int32, (cb, cb), 0)
    cbi = lax.broadcasted_iota(jnp.int32, (cb, cb), 1)
    tri_l = jnp.where(same & (ri >= ci), 1.0, 0.0).astype(BF16)
    dirs = ((0, rb >= cbi, mid - 1, cb - 1), (1, rb <= cbi, mid, 0))

    def cumsums(lfs, slot):
        pre = _cumsum_mm(tri_l, jnp.concatenate(lfs, axis=-1))
        pre_b = pre[:, dk:].reshape(gb, cb, dk)
        cum_s[slot, 0] = pre[:, :dk]
        cum_s[slot, 1] = (pre_b[:, cb - 1:cb, :] - pre_b).reshape(gr, dk) + lfs[1]

    def operands(slot, kks, q, n0, et_ref):
        for d, keep, ref_row, tot_row in dirs:
            cum = cum_s[slot, d].reshape(gb, cb, dk)
            kk = kks[d].astype(F32).reshape(gb, cb, dk)
            ref = cum[:, ref_row:ref_row + 1, :]
            tot = cum[:, tot_row:tot_row + 1, :]
            e = cum - ref
            kt = kk * jnp.exp(-e)
            ko_s[slot, d] = (kt * jnp.exp(tot - ref)).astype(BF16).reshape(gr, dk)
            e_tot = jnp.exp(tot)
            for j in range(gb):
                et_ref[n0 + j, :, d * dk:(d + 1) * dk] = e_tot[j]
            if q is not None:
                qt = q.astype(F32).reshape(gb, cb, dk) * jnp.exp(e)
                qin_s[pl.ds(pl.multiple_of(n0 * cb, gr), gr), d * dk:(d + 1) * dk] = (
                    (qt * jnp.exp(ref)).astype(BF16).reshape(gr, dk))
                qt_s[slot, d] = qt.astype(BF16).reshape(gr, dk)
                kt_s[slot, d] = kt.astype(BF16).reshape(gr, dk)

    def matmuls(slot, v, with_q, n0, kv_ref):
        for d in range(2):
            for j in range(gb):
                rows = slice(j * cb, (j + 1) * cb)
                kv_ref[n0 + j, :, d * dk:(d + 1) * dk] = _dot_tn(v[rows], ko_s[slot, d, rows, :])
        if with_q:
            for j in range(gb):
                rows = slice(j * cb, (j + 1) * cb)
                att = (jnp.where(dirs[0][1], _dot_nt(qt_s[slot, 0, rows, :], kt_s[slot, 0, rows, :]), 0.0)
                       + jnp.where(dirs[1][1], _dot_nt(qt_s[slot, 1, rows, :], kt_s[slot, 1, rows, :]), 0.0))
                att_s[n0 + j] = att.astype(BF16)

    for g in range(nbc // gb):
        sl = slice(g * gr, (g + 1) * gr)
        cumsums((lffc[0, sl, :], lfbc[0, sl, :]), 0)
        operands(0, (kfc[0, sl, :], kbc[0, sl, :]), None, g * gb, etc_s)
        matmuls(0, ic[0, sl, :], False, g * gb, kvc_s)

    ngl = nbl // gb
    rows_of = lambda g: pl.ds(pl.multiple_of(g * gr, gr), gr)

    def lat_cumsums(g, slot):
        cumsums((lffl[0, rows_of(g), :], lfbl[0, rows_of(g), :]), slot)

    def lat_operands(g, slot):
        operands(slot, (kfl[0, rows_of(g), :], kbl[0, rows_of(g), :]), ql[0, rows_of(g), :], g * gb, et_s)

    def lat_matmuls(g, slot):
        matmuls(slot, il[0, rows_of(g), :], True, g * gb, kv_s)

    lat_cumsums(0, 0)
    lat_operands(0, 0)
    lat_cumsums(1, 1)

    def prep_body(i, carry):
        g = 2 * i
        lat_matmuls(g, 0)
        lat_operands(g + 1, 1)
        lat_cumsums(g + 2, 0)
        lat_matmuls(g + 1, 1)
        lat_operands(g + 2, 0)
        lat_cumsums(g + 3, 1)
        return carry

    lax.fori_loop(0, ngl // 2 - 1, prep_body, 0)
    lat_matmuls(ngl - 2, 0)
    lat_operands(ngl - 1, 1)
    lat_matmuls(ngl - 1, 1)

    lane = lax.broadcasted_iota(jnp.int32, (dk, 2 * dk), 1)
    is_f = lane < dk
    st = jnp.zeros((dk, 2 * dk), F32)
    for n in range(nbc):
        m = nbc - 1 - n
        st = (st * jnp.where(is_f[:1], etc_s[n], etc_s[m]) + jnp.where(is_f, kvc_s[n], kvc_s[m]))

    def rec_body(t, st):
        u = nbl - 1 - t
        inc = jnp.where(is_f, kv_s[t], kv_s[u])
        dec = jnp.where(is_f[:1], et_s[t], et_s[u])
        kv_s[t, :, 0:dk] = st[:, 0:dk]
        kv_s[u, :, dk:2 * dk] = st[:, dk:2 * dk]
        return st * dec + inc

    lax.fori_loop(0, nbl, rec_body, st, unroll=unroll)

    def out_body(i, carry):
        for j in range(out_blocks):
            n = i * out_blocks + j
            sl = pl.ds(pl.multiple_of(n * cb, cb), cb)
            o = _dot(att_s[n], il[0, sl, :]) + _dot_nt(qin_s[sl, :], kv_s[n].astype(BF16))
            o = _rms(o) * ng_ref[...] * sgl[0, sl, :].astype(F32)
            o_ref[0, sl, :] = o.astype(BF16)
        return carry

    lax.fori_loop(0, nbl // out_blocks, out_body, 0)


def _hgrn(norm_g, q_l, lff_l, kf_l, lfb_l, kb_l, i_l, sg_l, lff_c, kf_c, lfb_c, kb_c, i_c):
    b, n, _ = q_l.shape
    nc = lff_c.shape[1]
    nbl, nbc = n // HG_BLOCK, nc // HG_BLOCK
    out_blocks = min(16, nbl)
    assert nbl % (2 * HG_GROUP) == 0 and nbc % HG_GROUP == 0 and nbl % out_blocks == 0
    spec = lambda rows: pl.BlockSpec((1, rows, HG_DK), lambda i, h: (i, 0, h))
    slot = lambda dt: pltpu.VMEM((2, 2, HG_GROUP * HG_BLOCK, HG_DK), dt)
    return pl.pallas_call(
        functools.partial(_hgrn_kernel, nbl=nbl, nbc=nbc, unroll=2, out_blocks=out_blocks),
        out_shape=jax.ShapeDtypeStruct((b, n, D_MODEL), BF16),
        grid=(b, HG_HEADS),
        in_specs=[pl.BlockSpec((1, HG_DK), lambda i, h: (0, 0))] + [spec(n)] * 7 + [spec(nc)] * 5,
        out_specs=spec(n),
        scratch_shapes=[pltpu.VMEM((n, 2 * HG_DK), BF16),
                        pltpu.VMEM((nbl, HG_BLOCK, HG_BLOCK), BF16),
                        pltpu.VMEM((nbl, HG_DK, 2 * HG_DK), F32),
                        pltpu.VMEM((nbl, 1, 2 * HG_DK), F32),
                        pltpu.VMEM((nbc, HG_DK, 2 * HG_DK), F32),
                        pltpu.VMEM((nbc, 1, 2 * HG_DK), F32),
                        slot(F32), slot(BF16), slot(BF16), slot(BF16)],
        compiler_params=_cp("parallel", "parallel"),
        name="hgrn2",
    )(norm_g, q_l, lff_l, kf_l, lfb_l, kb_l, i_l, sg_l, lff_c, kf_c, lfb_c, kb_c, i_c)


def _rope_tables(n_tok):
    tok = jnp.arange(n_tok, dtype=jnp.int32)[:, None]
    row = (tok // GRID_W).astype(F32)
    col = (tok % GRID_W).astype(F32)
    n_freq = RET_DK // 4
    lane = jnp.arange(RET_QK, dtype=jnp.int32)[None, :]
    j = lane % (2 * n_freq)
    inv = ROPE_BASE ** (-(j % n_freq).astype(F32) / n_freq)
    ang = jnp.where(j < n_freq, row, col) * inv
    sign = jnp.where(lane % RET_DK < RET_DK // 2, -1.0, 1.0)
    return jnp.cos(ang), jnp.sin(ang) * sign


def kernel(x, c, ctx, c_ctx, w_mod, b_mod, norm_mix, norm_mlp, w_mlp_in, w_mlp_out, ab_w_in, ab_w_out, ret_logit, s5_a_re, s5_a_im, s5_log_dt, s5_b_re, s5_b_im, s5_c_re, s5_c_im, s5_d, s5_w_glu, s5_b_glu, hg_w_in, hg_w_out, hg_lb_logits, hg_norm, norm_final):
    b, n, d = x.shape
    nc = ctx.shape[1]
    assert d == D_MODEL and b + 1 <= MOD_ROWS and w_mod.shape[0] == DEPTH == 2
    assert n % 512 == 0 and nc % 256 == 0 and n % GRID_W == 0
    ctx_row = b
    tm_l, tm_c = 512, 256

    cc = jnp.zeros((MOD_ROWS, d), F32).at[:b].set(c).at[b].set(c_ctx)
    mod = _adaln(cc, w_mod, b_mod)

    row2 = lambda a: a.reshape(1, -1)
    w_in0 = ab_w_in[0].astype(BF16)
    cos, sin = _rope_tables(n)
    ng0 = row2(norm_mix[0])
    q_l, k_l, v_l, u_l, uz_l, g_l = _inproj0(x, mod, 0, ng0, w_in0, cos, sin, None, tm_l)
    q_c, k_c, v_c, u_c, uz_c, g_c = _inproj0(ctx, mod, 0, ng0, w_in0, None, None, ctx_row, tm_c)

    log_gamma = jax.nn.log_sigmoid(ret_logit[0].astype(F32))
    lg_rows = jnp.broadcast_to(log_gamma.reshape(2 * RET_HEADS, 1), (2 * RET_HEADS, 2 * RET_DK))
    r_l, r_c = _retention(lg_rows, q_l, k_l, v_l, g_l, q_c, k_c, v_c, g_c)

    s5_ops = _s5_prep(s5_a_re[0], s5_a_im[0], s5_log_dt[0], s5_b_re[0], s5_b_im[0], s5_c_re[0], s5_c_im[0])
    y5_l, y5_c = _s5(s5_ops, uz_l, uz_c)

    s5p = (row2(s5_d[0]), s5_w_glu[0].astype(BF16), row2(s5_b_glu[0]))
    wo0 = ab_w_out[0].astype(BF16)
    w1_0, w2_0 = w_mlp_in[0].astype(BF16), w_mlp_out[0].astype(BF16)
    nm0 = row2(norm_mlp[0])
    h_l = _mix_mlp(x, r_l, y5_l, u_l, s5p, wo0, mod, 0, nm0, w1_0, w2_0, None, None, tm_l, 1024, "mix_mlp0_lat")
    h_c = _mix_mlp(ctx, r_c, y5_c, u_c, s5p, wo0, mod, 0, nm0, w1_0, w2_0, None, ctx_row, tm_c, 1024, "mix_mlp0_ctx")

    w_in1 = hg_w_in[0].astype(BF16)
    ng1 = row2(norm_mix[1])
    lat1 = _inproj1(h_l, mod, 1, ng1, hg_lb_logits, w_in1, True, None, tm_l, "inproj1_lat")
    ctx1 = _inproj1(h_c, mod, 1, ng1, hg_lb_logits, w_in1, False, ctx_row, tm_c, "inproj1_ctx")
    o1 = _hgrn(row2(hg_norm[0]), *lat1, *ctx1)
    return _mix_mlp(h_l, o1, None, None, None, hg_w_out[0].astype(BF16), mod, 1, row2(norm_mlp[1]),
                    w_mlp_in[1].astype(BF16), w_mlp_out[1].astype(BF16), row2(norm_final), None, tm_l, 1024,
                    "mix_mlp1_lat")
```

```python
import functools

import jax
import jax.numpy as jnp
from jax import lax
from jax.experimental import pallas as pl
from jax.experimental.pallas import tpu as pltpu

F32 = jnp.float32
BF16 = jnp.bfloat16

D_MODEL = 1024
DEPTH = 2
GRID_W = 64
EPS = 1e-6
ROPE_BASE = 10000.0
N_MOD = 6
RET_HEADS = 4
RET_DK = 64
RET_DV = 128
RET_QK = RET_HEADS * RET_DK
RET_WIDTH = RET_HEADS * RET_DV
RET_CHUNK = 128
S5_WIDTH = D_MODEL - RET_WIDTH
S5_GROUP = 16
S5_GROUPS = S5_WIDTH // S5_GROUP
S5_STATE = 64
S5_CHUNK = 16
S5_ROW = S5_CHUNK * S5_GROUP
LANE = 128
S5_LB = S5_WIDTH // LANE
S5_GPB = LANE // S5_GROUP
S5_BIG = S5_CHUNK * LANE
S5_HALF = S5_GPB * 2 * S5_STATE
S5_POW = 32
AB_IN = 2 * RET_QK + 2 * RET_WIDTH + S5_WIDTH
HG_HEADS = 8
HG_DK = D_MODEL // HG_HEADS
HG_BLOCK = 64
HG_GROUP = 4
HG_SPLIT = 2
D_FF = 4 * D_MODEL
MOD_ROWS = 16

VMEM_LIMIT_BYTES = 56 * 1024 * 1024


def _cp(*sem):
    return pltpu.CompilerParams(dimension_semantics=sem, vmem_limit_bytes=VMEM_LIMIT_BYTES)


def _dot(a, b):
    return jnp.dot(a, b, preferred_element_type=F32)


def _dot_nt(a, b):
    return lax.dot_general(a, b, (((1,), (1,)), ((), ())), preferred_element_type=F32)


def _dot_tn(a, b):
    return lax.dot_general(a, b, (((0,), (0,)), ((), ())), preferred_element_type=F32)


def _sigmoid(x):
    return 0.5 * jnp.tanh(0.5 * x) + 0.5


def _rms(x):
    return x * lax.rsqrt(jnp.mean(x * x, axis=-1, keepdims=True) + EPS)


def _mod_chunk(mod_ref, row, i):
    return mod_ref[0, pl.ds(row, 1), i * D_MODEL:(i + 1) * D_MODEL]


def _adaln_kernel(cc_ref, w_ref, b_ref, o_ref):
    s = jax.nn.silu(cc_ref[...]).astype(BF16)
    o_ref[0] = _dot(s, w_ref[0].astype(BF16)) + b_ref[0]


def _adaln(cc, w_mod, b_mod):
    bn = 1536
    n = N_MOD * D_MODEL
    return pl.pallas_call(
        _adaln_kernel,
        out_shape=jax.ShapeDtypeStruct((DEPTH, MOD_ROWS, n), F32),
        grid=(DEPTH, n // bn),
        in_specs=[
            pl.BlockSpec((MOD_ROWS, D_MODEL), lambda l, j: (0, 0)),
            pl.BlockSpec((1, D_MODEL, bn), lambda l, j: (l, 0, j)),
            pl.BlockSpec((1, 1, bn), lambda l, j: (l, 0, j)),
        ],
        out_specs=pl.BlockSpec((1, MOD_ROWS, bn), lambda l, j: (l, 0, j)),
        compiler_params=_cp("parallel", "parallel"),
        name="adaln",
    )(cc, w_mod, b_mod.reshape(DEPTH, 1, n))


def _rope(t, cos, sin):
    lane = lax.broadcasted_iota(jnp.int32, t.shape, 1)
    first = (lane & (RET_DK // 2)) == 0
    w = t.shape[1]
    swapped = jnp.where(first, pltpu.roll(t, w - RET_DK // 2, 1), pltpu.roll(t, RET_DK // 2, 1))
    return t * cos + swapped * sin


def _inproj0_kernel(*refs, mod_row, rope):
    if rope:
        h_ref, mod_ref, ng_ref, w_ref, cos_ref, sin_ref, q_ref, k_ref, v_ref, u_ref, uz_ref, g_ref, u_s = refs
    else:
        h_ref, mod_ref, ng_ref, w_ref, q_ref, k_ref, v_ref, u_ref, uz_ref, g_ref, u_s = refs
    row = pl.program_id(0) if mod_row is None else mod_row
    xn = _rms(h_ref[0]) * ng_ref[...]
    xm = (xn * (1.0 + _mod_chunk(mod_ref, row, 1)) + _mod_chunk(mod_ref, row, 0)).astype(BF16)
    y = _dot(xm, w_ref[...])
    q = y[:, 0:RET_QK]
    k = y[:, RET_QK:2 * RET_QK]
    if rope:
        q = _rope(q, cos_ref[...], sin_ref[...])
        k = _rope(k, cos_ref[...], sin_ref[...])
    q_ref[0] = q.astype(BF16)
    k_ref[0] = (k * (RET_DK ** -0.5)).astype(BF16)
    c0 = 2 * RET_QK
    v_ref[0] = y[:, c0:c0 + RET_WIDTH].astype(BF16)
    u0 = c0 + RET_WIDTH
    u_ref[0] = y[:, u0:u0 + S5_WIDTH]
    g = y[:, u0 + S5_WIDTH:]
    g_ref[0] = (g * _sigmoid(g)).astype(BF16)
    nch = u_s.shape[1] // S5_CHUNK
    for j in range(S5_LB):
        u_s[j] = y[:, u0 + j * LANE:u0 + (j + 1) * LANE]
        for s in range(S5_CHUNK):
            uz_ref[0, j, :, s * LANE:(s + 1) * LANE] = u_s[j, pl.ds(s, nch, stride=S5_CHUNK), :].astype(BF16)


def _inproj0(h, mod, layer, ng, w, cos, sin, mod_row, tm):
    b, n, _ = h.shape
    rope = cos is not None
    row_spec = lambda width: pl.BlockSpec((1, tm, width), lambda i, j: (i, j, 0))
    in_specs = [
        row_spec(D_MODEL),
        pl.BlockSpec((1, MOD_ROWS, N_MOD * D_MODEL), lambda i, j: (layer, 0, 0)),
        pl.BlockSpec((1, D_MODEL), lambda i, j: (0, 0)),
        pl.BlockSpec((D_MODEL, AB_IN), lambda i, j: (0, 0)),
    ]
    args = [h, mod, ng, w]
    if rope:
        in_specs += [pl.BlockSpec((tm, RET_QK), lambda i, j: (j, 0))] * 2
        args += [cos, sin]
    widths = (RET_QK, RET_QK, RET_WIDTH, S5_WIDTH, RET_WIDTH)
    dtypes = (BF16, BF16, BF16, F32, BF16)
    out_shape = [jax.ShapeDtypeStruct((b, n, wd), dt) for wd, dt in zip(widths, dtypes)]
    out_specs = [row_spec(wd) for wd in widths]
    out_shape.insert(4, jax.ShapeDtypeStruct((b, S5_LB, n // S5_CHUNK, S5_BIG), BF16))
    out_specs.insert(4, pl.BlockSpec((1, S5_LB, tm // S5_CHUNK, S5_BIG), lambda i, j: (i, 0, j, 0)))
    return pl.pallas_call(
        functools.partial(_inproj0_kernel, mod_row=mod_row, rope=rope),
        out_shape=out_shape,
        grid=(b, n // tm),
        in_specs=in_specs,
        out_specs=out_specs,
        scratch_shapes=[pltpu.VMEM((S5_LB, tm, LANE), F32)],
        compiler_params=_cp("parallel", "parallel"),
        name="inproj0_lat" if rope else "inproj0_ctx",
    )(*args)


def _ret_kernel(lg_ref, ql, kl, vl, gl, qc, kc, vc, gc, rl, rc, st_s, *, ncl, ncc, unroll):
    c = RET_CHUNK
    dk2 = 2 * RET_DK
    nt = ncc + ncl
    p = pl.program_id(1)
    h_a = 2 * p
    lgf_a = lg_ref[pl.ds(h_a, 1), :]
    lgf_b = lg_ref[pl.ds(h_a + 1, 1), :]
    lgb_a = lg_ref[pl.ds(RET_HEADS + h_a, 1), :]
    lgb_b = lg_ref[pl.ds(RET_HEADS + h_a + 1, 1), :]
    lane = lax.broadcasted_iota(jnp.int32, (1, 2 * RET_DK), 1)
    is_a = lane < RET_DK
    lgf_lane = jnp.where(is_a, lgf_a, lgf_b)
    lgb_lane = jnp.where(is_a, lgb_a, lgb_b)
    ri = lax.broadcasted_iota(jnp.int32, (c, c), 0).astype(F32)
    ci = lax.broadcasted_iota(jnp.int32, (c, c), 1).astype(F32)
    diff = ri - ci

    def dmat(lgf, lgb):
        fwd = jnp.exp(jnp.maximum(diff, 0.0) * lgf)
        bwd = jnp.exp(jnp.maximum(-diff, 0.0) * lgb)
        return jnp.where(diff > 0, fwd, jnp.where(diff < 0, bwd, 2.0))

    d_a = dmat(lgf_a, lgb_a)
    d_b = dmat(lgf_b, lgb_b)
    rowp = lax.broadcasted_iota(jnp.int32, (c, dk2), 0).astype(F32)
    qd = jnp.concatenate([jnp.exp((rowp + 1.0) * lgf_lane), jnp.exp((c - rowp) * lgb_lane)], axis=1)
    kd = jnp.concatenate([jnp.exp((c - 1.0 - rowp) * lgf_lane), jnp.exp(rowp * lgb_lane)], axis=1)
    rowk = lax.broadcasted_iota(jnp.int32, (dk2, 2 * RET_DV), 0)
    cd_f = jnp.exp(c * jnp.where(rowk < RET_DK, lgf_a[:, :1], lgf_b[:, :1]))
    cd_b = jnp.exp(c * jnp.where(rowk < RET_DK, lgb_a[:, :1], lgb_b[:, :1]))
    mask2 = jnp.concatenate([is_a, is_a], axis=1)

    def increment(k, v, slot):
        kk = jnp.concatenate([k, k], axis=1).astype(F32) * kd
        st_s[slot] = _dot_tn(kk.astype(BF16), v)

    for n in range(ncc):
        increment(kc[0, n * c:(n + 1) * c, :], vc[0, n * c:(n + 1) * c, :], n)

    def inc_body(i, carry):
        for j in range(unroll):
            n = i * unroll + j
            sl = pl.ds(pl.multiple_of(n * c, c), c)
            increment(kl[0, sl, :], vl[0, sl, :], ncc + n)
        return carry

    lax.fori_loop(0, ncl // unroll, inc_body, 0)

    def rec_body(t, carry):
        sf, sb = carry
        u = jnp.where(t < ncc, ncc - 1 - t, nt - 1 - (t - ncc))
        inc_f = st_s[t, 0:dk2, :]
        inc_b = st_s[u, dk2:2 * dk2, :]
        st_s[t, 0:dk2, :] = sf
        st_s[u, dk2:2 * dk2, :] = sb
        return cd_f * sf + inc_f, cd_b * sb + inc_b

    zero = jnp.zeros((dk2, 2 * RET_DV), F32)
    lax.fori_loop(0, nt, rec_body, (zero, zero))

    def output(q, k, v, g, slot, out_ref, st):
        q2 = (jnp.concatenate([q, q], axis=1).astype(F32) * qd).astype(BF16)
        s_n = st_s[slot].astype(BF16)
        for keep_a, dm, cs in ((True, d_a, 0), (False, d_b, RET_DV)):
            m1 = is_a if keep_a else jnp.logical_not(is_a)
            m2 = mask2 if keep_a else jnp.logical_not(mask2)
            att = _dot_nt(jnp.where(m1, q, jnp.zeros_like(q)), k) * dm
            o = (_dot(att.astype(BF16), v[:, cs:cs + RET_DV])
                 + _dot(jnp.where(m2, q2, jnp.zeros_like(q2)), s_n[:, cs:cs + RET_DV]))
            o = _rms(o) * g[:, cs:cs + RET_DV].astype(F32)
            out_ref[0, pl.ds(st, c), cs:cs + RET_DV] = o.astype(BF16)

    for n in range(ncc):
        sl = slice(n * c, (n + 1) * c)
        output(qc[0, sl, :], kc[0, sl, :], vc[0, sl, :], gc[0, sl, :], n, rc, n * c)

    def out_body(i, carry):
        for j in range(unroll):
            n = i * unroll + j
            st = pl.multiple_of(n * c, c)
            sl = pl.ds(st, c)
            output(ql[0, sl, :], kl[0, sl, :], vl[0, sl, :], gl[0, sl, :], ncc + n, rl, st)
        return carry

    lax.fori_loop(0, ncl // unroll, out_body, 0)


def _retention(lg_rows, q_l, k_l, v_l, g_l, q_c, k_c, v_c, g_c):
    b, n, _ = q_l.shape
    nc = q_c.shape[1]
    ncl, ncc = n // RET_CHUNK, nc // RET_CHUNK
    pairs = RET_HEADS // 2
    unroll = 8 if ncl % 8 == 0 else 1

    def spec(rows, width):
        return pl.BlockSpec((1, rows, width), lambda i, p: (i, 0, p))

    return pl.pallas_call(
        functools.partial(_ret_kernel, ncl=ncl, ncc=ncc, unroll=unroll),
        out_shape=[jax.ShapeDtypeStruct((b, n, RET_WIDTH), BF16),
                   jax.ShapeDtypeStruct((b, nc, RET_WIDTH), BF16)],
        grid=(b, pairs),
        in_specs=[pl.BlockSpec((2 * RET_HEADS, 2 * RET_DK), lambda i, p: (0, 0)),
                  spec(n, 2 * RET_DK), spec(n, 2 * RET_DK), spec(n, 2 * RET_DV), spec(n, 2 * RET_DV),
                  spec(nc, 2 * RET_DK), spec(nc, 2 * RET_DK), spec(nc, 2 * RET_DV), spec(nc, 2 * RET_DV)],
        out_specs=[spec(n, 2 * RET_DV), spec(nc, 2 * RET_DV)],
        scratch_shapes=[pltpu.VMEM((ncl + ncc, 4 * RET_DK, 2 * RET_DV), F32)],
        compiler_params=_cp("parallel", "parallel"),
        name="retention",
    )(lg_rows, q_l, k_l, v_l, g_l, q_c, k_c, v_c, g_c)


def _dot_hi(a, b, contract=(1, 0)):
    dims = (((contract[0],), (contract[1],)), ((), ()))
    return lax.dot_general(a, b, dims, preferred_element_type=F32, precision=lax.Precision.HIGHEST)


def _dot_sel(a, b, contract, data):
    dims = (((contract[0],), (contract[1],)), ((), ()))
    x = (a, b)[data]
    hi = x.astype(BF16)
    lo = (x - hi.astype(F32)).astype(BF16)
    dd = lambda piece: lax.dot_general(*((piece, b) if data == 0 else (a, piece)), dims, preferred_element_type=F32)
    return dd(hi) + dd(lo)


def _s5_prep_kernel(ar_row, ai_row, ldt, btr, bti, ctr, cti, rm, rwin, rwout, abig):
    t, g, p, kp = S5_CHUNK, S5_GROUP, S5_STATE, S5_POW
    row = S5_ROW
    i0 = lambda shape: lax.broadcasted_iota(jnp.int32, shape, 0)
    i1 = lambda shape: lax.broadcasted_iota(jnp.int32, shape, 1)
    f32 = lambda m: jnp.where(m, 1.0, 0.0).astype(BF16)
    s_of_r = i0((row, kp)) // g
    k_of_l = i1((row, kp))
    sel_rows = (f32(k_of_l == t - 1 - s_of_r), f32(k_of_l == s_of_r))
    t_of_c = i1((kp, row)) // g
    k_of_s = i0((kp, row))
    sel_out = (f32(k_of_s == t_of_c + 1), f32(k_of_s == t - t_of_c))
    sel_lag = (f32(k_of_s == t_of_c), f32(k_of_s == t - 1 - t_of_c))
    tile_l = f32(i1((g, row)) % g == i0((g, row)))
    tile_r = f32(i0((row, g)) % g == i1((row, g)))
    lane = i1((g, row))
    k_col = i0((kp, 1)).astype(F32)
    k_row = i1((1, kp)).astype(F32)
    first = i0((8, 1)) == 0

    def outer(a, k):
        a8 = jnp.where(first, jnp.broadcast_to(a, (8, a.shape[1])), 0.0)
        return _dot_hi(a8, jnp.broadcast_to(k, (8, k.shape[1])), (0, 0))

    for q in range(S5_GPB):
        lags = []
        for d in range(2):
            dt = jnp.exp(ldt[d, q])
            are_r, aim_r = ar_row[d, q], ai_row[d, q]
            mag = jnp.exp(are_r * dt)
            ang = aim_r * dt
            nr, ni = mag * jnp.cos(ang) - 1.0, mag * jnp.sin(ang)
            den = jnp.square(are_r) + jnp.square(aim_r)
            fr = (nr * are_r + ni * aim_r) / den
            fi = (ni * are_r - nr * aim_r) / den
            pm = jnp.exp(k_col * (are_r * dt))
            pa = k_col * ang
            pk_re, pk_im = pm * jnp.cos(pa), pm * jnp.sin(pa)
            pmt = jnp.exp(outer(are_r * dt, k_row))
            pat = outer(ang, k_row)
            pt_re, pt_im = pmt * jnp.cos(pat), pmt * jnp.sin(pat)
            bt_re, bt_im = _dot_sel(tile_r, btr[d, q], (1, 1), 1), _dot_sel(tile_r, bti[d, q], (1, 1), 1)
            bb_re = fr * bt_re - fi * bt_im
            bb_im = fr * bt_im + fi * bt_re
            pr_re, pr_im = _dot_sel(sel_rows[d], pk_re, (1, 0), 1), _dot_sel(sel_rows[d], pk_im, (1, 0), 1)
            w_re = pr_re * bb_re - pr_im * bb_im
            w_im = pr_re * bb_im + pr_im * bb_re
            for s in range(t):
                rows = slice(s * LANE + q * g, s * LANE + (q + 1) * g)
                rwin[0, rows, d * p:(d + 1) * p] = w_re[s * g:(s + 1) * g].astype(BF16)
                rwin[0, rows, (2 + d) * p:(3 + d) * p] = w_im[s * g:(s + 1) * g].astype(BF16)
            ct_re, ct_im = _dot_sel(ctr[d, q], tile_l, (0, 0), 0), _dot_sel(cti[d, q], tile_l, (0, 0), 0)

            def c_pow(sel):
                pc_re, pc_im = _dot_sel(pt_re, sel, (1, 0), 0), _dot_sel(pt_im, sel, (1, 0), 0)
                return ct_re * pc_re - ct_im * pc_im, ct_re * pc_im + ct_im * pc_re

            o_re, o_im = c_pow(sel_out[d])
            r0 = d * S5_GPB * p + q * p
            rwout[0, r0:r0 + p, :] = o_re.astype(BF16)
            rwout[0, S5_HALF + r0:S5_HALF + r0 + p, :] = (-o_im).astype(BF16)
            l_re, l_im = c_pow(sel_lag[d])
            lags.append(_dot_hi(bb_re[0:g], l_re) - _dot_hi(bb_im[0:g], l_im))
            abig[0, 0:1, r0:r0 + p] = pk_re[t:t + 1, :]
            abig[0, 1:2, r0:r0 + p] = pk_im[t:t + 1, :]
        for s in range(t):
            fwd = jnp.where(lane >= g * s, pltpu.roll(lags[0], g * s, 1), 0.0)
            bwd = jnp.where(lane < g * (s + 1), pltpu.roll(lags[1], (row - g * (t - 1 - s)) % row, 1), 0.0)
            rm[0, s * LANE + q * g:s * LANE + (q + 1) * g, :] = (fwd + bwd).astype(BF16)


def _s5_prep(a_re, a_im, log_dt, b_re, b_im, c_re, c_im):
    gg, p, g = S5_GROUPS, S5_STATE, S5_GROUP
    f = lambda x: x.astype(F32)
    args = (f(a_re).reshape(2, gg, 1, p), f(a_im).reshape(2, gg, 1, p),
            f(log_dt).reshape(2, gg, 1, 1),
            f(b_re), f(b_im), f(c_re), f(c_im))
    spec = lambda r, c: pl.BlockSpec((2, S5_GPB, r, c), lambda j: (0, j, 0, 0))
    out = lambda r, c: pl.BlockSpec((1, r, c), lambda j: (j, 0, 0))
    return pl.pallas_call(
        _s5_prep_kernel,
        out_shape=[jax.ShapeDtypeStruct((S5_LB, S5_BIG, S5_ROW), BF16),
                   jax.ShapeDtypeStruct((S5_LB, S5_BIG, S5_ROW), BF16),
                   jax.ShapeDtypeStruct((S5_LB, 2 * S5_HALF, S5_ROW), BF16),
                   jax.ShapeDtypeStruct((S5_LB, 2, S5_HALF), F32)],
        grid=(S5_LB,),
        in_specs=[spec(1, p), spec(1, p), spec(1, 1),
                  spec(p, g), spec(p, g), spec(g, p), spec(g, p)],
        out_specs=[out(S5_BIG, S5_ROW), out(S5_BIG, S5_ROW), out(2 * S5_HALF, S5_ROW), out(2, S5_HALF)],
        compiler_params=_cp("parallel"),
        name="s5_prep",
    )(*args)


def _s5_expand(r_ref, col_unit, col_block, row_unit):
    n = S5_BIG
    a = lax.broadcasted_iota(jnp.int32, (S5_ROW, n), 0)
    c = lax.broadcasted_iota(jnp.int32, (S5_ROW, n), 1)
    e = jnp.where((a // col_unit == c // col_block) & (a % col_unit == c % col_unit), 1.0, 0.0).astype(BF16)
    x = _dot(r_ref[0], e)
    rq = (lax.broadcasted_iota(jnp.int32, (n, n), 0) // row_unit) % S5_GPB
    cq = (lax.broadcasted_iota(jnp.int32, (n, n), 1) // col_unit) % S5_GPB
    return jnp.where(rq == cq, x, 0.0).astype(BF16)


def _s5_kernel(ul, uc, rm_ref, rwin_ref, rwout_ref, a_ref, yl, yc, x_s, m_s, win_s, wout_s, *, ncl, ncc):
    hw = S5_HALF
    hh = hw // 2

    @pl.when(pl.program_id(1) == 0)
    def _():
        m_s[...] = _s5_expand(rm_ref, S5_GROUP, LANE, S5_GROUP)
        win_s[...] = _s5_expand(rwin_ref, S5_STATE, S5_GPB * S5_STATE, S5_GROUP)
        wout_s[...] = _s5_expand(rwout_ref, S5_GROUP, LANE, S5_STATE)

    z = jnp.concatenate([uc[0, 0], ul[0, 0]], axis=0)
    x_s[...] = _dot(z, win_s[...])
    a_re = a_ref[0, 0:1, :]
    a_im = a_ref[0, 1:2, :]

    def segment(base, n, carry):
        s_re, s_im = carry
        for i in range(n):
            rf = slice(base + i, base + i + 1)
            rb = slice(base + n - 1 - i, base + n - i)
            x_re = jnp.concatenate([x_s[rf, 0:hh], x_s[rb, hh:hw]], axis=-1)
            x_im = jnp.concatenate([x_s[rf, hw:hw + hh], x_s[rb, hw + hh:2 * hw]], axis=-1)
            x_s[rf, 0:hh] = s_re[:, 0:hh]
            x_s[rb, hh:hw] = s_re[:, hh:hw]
            x_s[rf, hw:hw + hh] = s_im[:, 0:hh]
            x_s[rb, hw + hh:2 * hw] = s_im[:, hh:hw]
            s_re, s_im = a_re * s_re - a_im * s_im + x_re, a_re * s_im + a_im * s_re + x_im
        return s_re, s_im

    zero = jnp.zeros((1, hw), F32)
    carry = segment(0, ncc, (zero, zero))
    segment(ncc, ncl, carry)
    y = _dot(z, m_s[...]) + _dot(x_s[...].astype(BF16), wout_s[...])
    yc[0, 0] = y[0:ncc]
    yl[0, 0] = y[ncc:ncc + ncl]


def _s5(prep, zl, zc):
    rm, rwin, rwout, a_big = prep
    b, _, ncl, _ = zl.shape
    ncc = zc.shape[2]
    rows = lambda r: pl.BlockSpec((1, 1, r, S5_BIG), lambda j, i: (i, j, 0, 0))
    wspec = lambda r, c: pl.BlockSpec((1, r, c), lambda j, i: (j, 0, 0))
    big = pltpu.VMEM((S5_BIG, S5_BIG), BF16)
    return pl.pallas_call(
        functools.partial(_s5_kernel, ncl=ncl, ncc=ncc),
        out_shape=[jax.ShapeDtypeStruct(zl.shape, F32), jax.ShapeDtypeStruct(zc.shape, F32)],
        grid=(S5_LB, b),
        in_specs=[rows(ncl), rows(ncc), wspec(S5_BIG, S5_ROW), wspec(S5_BIG, S5_ROW),
                  wspec(2 * S5_HALF, S5_ROW), wspec(2, S5_HALF)],
        out_specs=[rows(ncl), rows(ncc)],
        scratch_shapes=[pltpu.VMEM((ncc + ncl, 2 * S5_HALF), F32), big, big, big],
        compiler_params=_cp("parallel", "arbitrary"),
        name="s5",
    )(zl, zc, rm, rwin, rwout, a_big)


def _mix_mlp_kernel(*refs, mod_row, s5_merge, final_norm, fb):
    if s5_merge:
        (h_ref, r_ref, y5_ref, u_ref, ds_ref, wg_ref, bg_ref, wo_ref, mod_ref, nm_ref, w1_ref, w2_ref,
         *rest) = refs
    else:
        h_ref, r_ref, wo_ref, mod_ref, nm_ref, w1_ref, w2_ref, *rest = refs
    if final_norm:
        nf_ref, o_ref, *scratch = rest
    else:
        o_ref, *scratch = rest
    row = pl.program_id(0) if mod_row is None else mod_row
    if s5_merge:
        (y_s,) = scratch
        nch = y_s.shape[1] // S5_CHUNK
        for lb in range(S5_LB):
            for s in range(S5_CHUNK):
                y_s[lb, pl.ds(s, nch, stride=S5_CHUNK), :] = y5_ref[0, lb, :, s * LANE:(s + 1) * LANE]
        y5 = jnp.concatenate([y_s[lb] for lb in range(S5_LB)], axis=-1)
        y = jax.nn.gelu(y5 + ds_ref[...] * u_ref[0])
        y = y * jax.nn.sigmoid(_dot(y.astype(BF16), wg_ref[...]) + bg_ref[...])
        mix = _dot(r_ref[0], wo_ref[0:RET_WIDTH, :]) + _dot(y.astype(BF16), wo_ref[RET_WIDTH:D_MODEL, :])
    else:
        mix = _dot(r_ref[0], wo_ref[...])
    h1 = h_ref[0] + _mod_chunk(mod_ref, row, 2) * mix
    xn = _rms(h1) * nm_ref[...]
    xm = (xn * (1.0 + _mod_chunk(mod_ref, row, 4)) + _mod_chunk(mod_ref, row, 3)).astype(BF16)
    acc = None
    for j in range(D_FF // fb):
        a = jnp.square(jnp.maximum(_dot(xm, w1_ref[:, j * fb:(j + 1) * fb]), 0.0)).astype(BF16)
        part = _dot(a, w2_ref[j * fb:(j + 1) * fb, :])
        acc = part if acc is None else acc + part
    h2 = h1 + _mod_chunk(mod_ref, row, 5) * acc
    if final_norm:
        h2 = _rms(h2) * nf_ref[...]
    o_ref[0] = h2


def _mix_mlp(h, r, s5y, u, s5p, wo, mod, layer, nm, w1, w2, nf, mod_row, tm, fb, name):
    b, n, _ = h.shape
    s5_merge = s5y is not None
    final_norm = nf is not None
    one = pl.Buffered(1)
    row_spec = lambda width: pl.BlockSpec((1, tm, width), lambda i, t: (i, t, 0))
    const = lambda shape: pl.BlockSpec(shape, lambda i, t: (0,) * len(shape), pipeline_mode=one)
    in_specs = [row_spec(D_MODEL), row_spec(r.shape[-1])]
    args = [h, r]
    if s5_merge:
        d_skip, w_glu, b_glu = s5p
        z_spec = pl.BlockSpec((1, S5_LB, tm // S5_CHUNK, S5_BIG), lambda i, t: (i, 0, t, 0))
        in_specs += [z_spec, row_spec(S5_WIDTH), const((1, S5_WIDTH)),
                     const((S5_WIDTH, S5_WIDTH)), const((1, S5_WIDTH))]
        args += [s5y, u, d_skip, w_glu, b_glu]
    in_specs += [const((D_MODEL, D_MODEL)),
                 pl.BlockSpec((1, MOD_ROWS, N_MOD * D_MODEL), lambda i, t: (layer, 0, 0), pipeline_mode=one),
                 const((1, D_MODEL)), const((D_MODEL, D_FF)), const((D_FF, D_MODEL))]
    args += [wo, mod, nm, w1, w2]
    if final_norm:
        in_specs.append(const((1, D_MODEL)))
        args.append(nf)
    return pl.pallas_call(
        functools.partial(_mix_mlp_kernel, mod_row=mod_row, s5_merge=s5_merge, final_norm=final_norm, fb=fb),
        out_shape=jax.ShapeDtypeStruct((b, n, D_MODEL), F32),
        grid=(b, n // tm),
        in_specs=in_specs,
        out_specs=row_spec(D_MODEL),
        scratch_shapes=[pltpu.VMEM((S5_LB, tm, LANE), F32)] if s5_merge else [],
        compiler_params=_cp("parallel", "parallel"),
        name=name,
    )(*args)


def _hgrn_lower_bounds(lbl_ref, layer):
    out = []
    for d in range(2):
        z = [lbl_ref[d, k:k + 1, :] for k in range(DEPTH)]
        zmax = functools.reduce(jnp.maximum, z)
        e = [jnp.exp(v - zmax) for v in z]
        tot = functools.reduce(lambda a, b_: a + b_, e)
        lb = jnp.zeros_like(tot)
        for k in range(1, layer + 1):
            lb = lb + e[k] / tot
        out.append(lb)
    return out


def _inproj1_kernel(h_ref, mod_ref, ng_ref, lbl_ref, w_ref, *out_refs, mod_row, layer, latent):
    row = pl.program_id(0) if mod_row is None else mod_row
    xn = _rms(h_ref[0]) * ng_ref[...]
    xm = (xn * (1.0 + _mod_chunk(mod_ref, row, 1)) + _mod_chunk(mod_ref, row, 0)).astype(BF16)
    col = lambda k: _dot(xm, w_ref[:, k * D_MODEL:(k + 1) * D_MODEL])
    lbs = _hgrn_lower_bounds(lbl_ref, layer)
    refs = list(out_refs)
    if latent:
        refs.pop(0)[0] = col(0).astype(BF16)
    for d in range(2):
        t = (1.0 - lbs[d]) * _sigmoid(col(1 + d))
        refs.pop(0)[0] = jnp.log(lbs[d] + t)
        refs.pop(0)[0] = ((1.0 - lbs[d]) - t).astype(BF16)
    refs.pop(0)[0] = col(3).astype(BF16)
    if latent:
        g = col(4)
        refs.pop(0)[0] = (g * _sigmoid(g)).astype(BF16)


def _inproj1(h, mod, layer, ng, lb_logits, w, latent, mod_row, tm, name):
    b, n, _ = h.shape
    one = pl.Buffered(1)
    row_spec = pl.BlockSpec((1, tm, D_MODEL), lambda i, t: (i, t, 0))
    dtypes = ([BF16] if latent else []) + [F32, BF16, F32, BF16, BF16] + ([BF16] if latent else [])
    return pl.pallas_call(
        functools.partial(_inproj1_kernel, mod_row=mod_row, layer=layer, latent=latent),
        out_shape=[jax.ShapeDtypeStruct((b, n, D_MODEL), dt) for dt in dtypes],
        grid=(b, n // tm),
        in_specs=[row_spec,
                  pl.BlockSpec((1, MOD_ROWS, N_MOD * D_MODEL), lambda i, t: (layer, 0, 0), pipeline_mode=one),
                  pl.BlockSpec((1, D_MODEL), lambda i, t: (0, 0), pipeline_mode=one),
                  pl.BlockSpec(lb_logits.shape, lambda i, t: (0, 0, 0), pipeline_mode=one),
                  pl.BlockSpec(w.shape, lambda i, t: (0, 0), pipeline_mode=one)],
        out_specs=[row_spec] * len(dtypes),
        compiler_params=_cp("parallel", "parallel"),
        name=name,
    )(h, mod, ng, lb_logits, w)


def _cumsum_mm(tri, x):
    acc = None
    r = x
    for i in range(HG_SPLIT):
        p = r.astype(BF16)
        acc = _dot(tri, p) if acc is None else acc + _dot(tri, p)
        if i + 1 < HG_SPLIT:
            r = r - p.astype(F32)
    return acc


def _hgrn_kernel(ng_ref, ql, lffl, kfl, lfbl, kbl, il, sgl, lffc, kfc, lfbc, kbc, ic, o_ref,
                 qin_s, att_s, kv_s, et_s, kvc_s, etc_s, cum_s, ko_s, qt_s, kt_s, *, nbl, nbc, unroll, out_blocks):
    cb = HG_BLOCK
    mid = cb // 2
    gb = HG_GROUP
    gr = gb * cb
    dk = HG_DK

    ri = lax.broadcasted_iota(jnp.int32, (gr, gr), 0)
    ci = lax.broadcasted_iota(jnp.int32, (gr, gr), 1)
    same = (ri // cb) == (ci // cb)
    rb = lax.broadcasted_iota(jnp.int32, (cb, cb), 0)
    cbi = lax.broadcasted_iota(jnp.int32, (cb, cb), 1)
    tri_l = jnp.where(same & (ri >= ci), 1.0, 0.0).astype(BF16)
    dirs = ((0, rb >= cbi, mid - 1, cb - 1), (1, rb <= cbi, mid, 0))

    def cumsums(lfs, slot):
        pre = _cumsum_mm(tri_l, jnp.concatenate(lfs, axis=-1))
        pre_b = pre[:, dk:].reshape(gb, cb, dk)
        cum_s[slot, 0] = pre[:, :dk]
        cum_s[slot, 1] = (pre_b[:, cb - 1:cb, :] - pre_b).reshape(gr, dk) + lfs[1]

    def operands(slot, kks, q, n0, et_ref):
        for d, keep, ref_row, tot_row in dirs:
            cum = cum_s[slot, d].reshape(gb, cb, dk)
            kk = kks[d].astype(F32).reshape(gb, cb, dk)
            ref = cum[:, ref_row:ref_row + 1, :]
            tot = cum[:, tot_row:tot_row + 1, :]
            e = cum - ref
            kt = kk * jnp.exp(-e)
            ko_s[slot, d] = (kt * jnp.exp(tot - ref)).astype(BF16).reshape(gr, dk)
            e_tot = jnp.exp(tot)
            for j in range(gb):
                et_ref[n0 + j, :, d * dk:(d + 1) * dk] = e_tot[j]
            if q is not None:
                qt = q.astype(F32).reshape(gb, cb, dk) * jnp.exp(e)
                qin_s[pl.ds(pl.multiple_of(n0 * cb, gr), gr), d * dk:(d + 1) * dk] = (
                    (qt * jnp.exp(ref)).astype(BF16).reshape(gr, dk))
                qt_s[slot, d] = qt.astype(BF16).reshape(gr, dk)
                kt_s[slot, d] = kt.astype(BF16).reshape(gr, dk)

    def matmuls(slot, v, with_q, n0, kv_ref):
        for d in range(2):
            for j in range(gb):
                rows = slice(j * cb, (j + 1) * cb)
                kv_ref[n0 + j, :, d * dk:(d + 1) * dk] = _dot_tn(v[rows], ko_s[slot, d, rows, :])
        if with_q:
            for j in range(gb):
                rows = slice(j * cb, (j + 1) * cb)
                att = (jnp.where(dirs[0][1], _dot_nt(qt_s[slot, 0, rows, :], kt_s[slot, 0, rows, :]), 0.0)
                       + jnp.where(dirs[1][1], _dot_nt(qt_s[slot, 1, rows, :], kt_s[slot, 1, rows, :]), 0.0))
                att_s[n0 + j] = att.astype(BF16)

    for g in range(nbc // gb):
        sl = slice(g * gr, (g + 1) * gr)
        cumsums((lffc[0, sl, :], lfbc[0, sl, :]), 0)
        operands(0, (kfc[0, sl, :], kbc[0, sl, :]), None, g * gb, etc_s)
        matmuls(0, ic[0, sl, :], False, g * gb, kvc_s)

    ngl = nbl // gb
    rows_of = lambda g: pl.ds(pl.multiple_of(g * gr, gr), gr)

    def lat_cumsums(g, slot):
        cumsums((lffl[0, rows_of(g), :], lfbl[0, rows_of(g), :]), slot)

    def lat_operands(g, slot):
        operands(slot, (kfl[0, rows_of(g), :], kbl[0, rows_of(g), :]), ql[0, rows_of(g), :], g * gb, et_s)

    def lat_matmuls(g, slot):
        matmuls(slot, il[0, rows_of(g), :], True, g * gb, kv_s)

    lat_cumsums(0, 0)
    lat_operands(0, 0)
    lat_cumsums(1, 1)

    def prep_body(i, carry):
        g = 2 * i
        lat_matmuls(g, 0)
        lat_operands(g + 1, 1)
        lat_cumsums(g + 2, 0)
        lat_matmuls(g + 1, 1)
        lat_operands(g + 2, 0)
        lat_cumsums(g + 3, 1)
        return carry

    lax.fori_loop(0, ngl // 2 - 1, prep_body, 0)
    lat_matmuls(ngl - 2, 0)
    lat_operands(ngl - 1, 1)
    lat_matmuls(ngl - 1, 1)

    lane = lax.broadcasted_iota(jnp.int32, (dk, 2 * dk), 1)
    is_f = lane < dk
    st = jnp.zeros((dk, 2 * dk), F32)
    for n in range(nbc):
        m = nbc - 1 - n
        st = (st * jnp.where(is_f[:1], etc_s[n], etc_s[m]) + jnp.where(is_f, kvc_s[n], kvc_s[m]))

    def rec_body(t, st):
        u = nbl - 1 - t
        inc = jnp.where(is_f, kv_s[t], kv_s[u])
        dec = jnp.where(is_f[:1], et_s[t], et_s[u])
        kv_s[t, :, 0:dk] = st[:, 0:dk]
        kv_s[u, :, dk:2 * dk] = st[:, dk:2 * dk]
        return st * dec + inc

    lax.fori_loop(0, nbl, rec_body, st, unroll=unroll)

    def out_body(i, carry):
        for j in range(out_blocks):
            n = i * out_blocks + j
            sl = pl.ds(pl.multiple_of(n * cb, cb), cb)
            o = _dot(att_s[n], il[0, sl, :]) + _dot_nt(qin_s[sl, :], kv_s[n].astype(BF16))
            o = _rms(o) * ng_ref[...] * sgl[0, sl, :].astype(F32)
            o_ref[0, sl, :] = o.astype(BF16)
        return carry

    lax.fori_loop(0, nbl // out_blocks, out_body, 0)


def _hgrn(norm_g, q_l, lff_l, kf_l, lfb_l, kb_l, i_l, sg_l, lff_c, kf_c, lfb_c, kb_c, i_c):
    b, n, _ = q_l.shape
    nc = lff_c.shape[1]
    nbl, nbc = n // HG_BLOCK, nc // HG_BLOCK
    out_blocks = min(16, nbl)
    assert nbl % (2 * HG_GROUP) == 0 and nbc % HG_GROUP == 0 and nbl % out_blocks == 0
    spec = lambda rows: pl.BlockSpec((1, rows, HG_DK), lambda i, h: (i, 0, h))
    slot = lambda dt: pltpu.VMEM((2, 2, HG_GROUP * HG_BLOCK, HG_DK), dt)
    return pl.pallas_call(
        functools.partial(_hgrn_kernel, nbl=nbl, nbc=nbc, unroll=2, out_blocks=out_blocks),
        out_shape=jax.ShapeDtypeStruct((b, n, D_MODEL), BF16),
        grid=(b, HG_HEADS),
        in_specs=[pl.BlockSpec((1, HG_DK), lambda i, h: (0, 0))] + [spec(n)] * 7 + [spec(nc)] * 5,
        out_specs=spec(n),
        scratch_shapes=[pltpu.VMEM((n, 2 * HG_DK), BF16),
                        pltpu.VMEM((nbl, HG_BLOCK, HG_BLOCK), BF16),
                        pltpu.VMEM((nbl, HG_DK, 2 * HG_DK), F32),
                        pltpu.VMEM((nbl, 1, 2 * HG_DK), F32),
                        pltpu.VMEM((nbc, HG_DK, 2 * HG_DK), F32),
                        pltpu.VMEM((nbc, 1, 2 * HG_DK), F32),
                        slot(F32), slot(BF16), slot(BF16), slot(BF16)],
        compiler_params=_cp("parallel", "parallel"),
        name="hgrn2",
    )(norm_g, q_l, lff_l, kf_l, lfb_l, kb_l, i_l, sg_l, lff_c, kf_c, lfb_c, kb_c, i_c)


def _rope_tables(n_tok):
    tok = jnp.arange(n_tok, dtype=jnp.int32)[:, None]
    row = (tok // GRID_W).astype(F32)
    col = (tok % GRID_W).astype(F32)
    n_freq = RET_DK // 4
    lane = jnp.arange(RET_QK, dtype=jnp.int32)[None, :]
    j = lane % (2 * n_freq)
    inv = ROPE_BASE ** (-(j % n_freq).astype(F32) / n_freq)
    ang = jnp.where(j < n_freq, row, col) * inv
    sign = jnp.where(lane % RET_DK < RET_DK // 2, -1.0, 1.0)
    return jnp.cos(ang), jnp.sin(ang) * sign


def kernel(x, c, ctx, c_ctx, w_mod, b_mod, norm_mix, norm_mlp, w_mlp_in, w_mlp_out, ab_w_in, ab_w_out, ret_logit, s5_a_re, s5_a_im, s5_log_dt, s5_b_re, s5_b_im, s5_c_re, s5_c_im, s5_d, s5_w_glu, s5_b_glu, hg_w_in, hg_w_out, hg_lb_logits, hg_norm, norm_final):
    b, n, d = x.shape
    nc = ctx.shape[1]
    assert d == D_MODEL and b + 1 <= MOD_ROWS and w_mod.shape[0] == DEPTH == 2
    assert n % 512 == 0 and nc % 256 == 0 and n % GRID_W == 0
    ctx_row = b
    tm_l, tm_c = 512, 256

    cc = jnp.zeros((MOD_ROWS, d), F32).at[:b].set(c).at[b].set(c_ctx)
    mod = _adaln(cc, w_mod, b_mod)

    row2 = lambda a: a.reshape(1, -1)
    w_in0 = ab_w_in[0].astype(BF16)
    cos, sin = _rope_tables(n)
    ng0 = row2(norm_mix[0])
    q_l, k_l, v_l, u_l, uz_l, g_l = _inproj0(x, mod, 0, ng0, w_in0, cos, sin, None, tm_l)
    q_c, k_c, v_c, u_c, uz_c, g_c = _inproj0(ctx, mod, 0, ng0, w_in0, None, None, ctx_row, tm_c)

    log_gamma = jax.nn.log_sigmoid(ret_logit[0].astype(F32))
    lg_rows = jnp.broadcast_to(log_gamma.reshape(2 * RET_HEADS, 1), (2 * RET_HEADS, 2 * RET_DK))
    r_l, r_c = _retention(lg_rows, q_l, k_l, v_l, g_l, q_c, k_c, v_c, g_c)

    s5_ops = _s5_prep(s5_a_re[0], s5_a_im[0], s5_log_dt[0], s5_b_re[0], s5_b_im[0], s5_c_re[0], s5_c_im[0])
    y5_l, y5_c = _s5(s5_ops, uz_l, uz_c)

    s5p = (row2(s5_d[0]), s5_w_glu[0].astype(BF16), row2(s5_b_glu[0]))
    wo0 = ab_w_out[0].astype(BF16)
    w1_0, w2_0 = w_mlp_in[0].astype(BF16), w_mlp_out[0].astype(BF16)
    nm0 = row2(norm_mlp[0])
    h_l = _mix_mlp(x, r_l, y5_l, u_l, s5p, wo0, mod, 0, nm0, w1_0, w2_0, None, None, tm_l, 1024, "mix_mlp0_lat")
    h_c = _mix_mlp(ctx, r_c, y5_c, u_c, s5p, wo0, mod, 0, nm0, w1_0, w2_0, None, ctx_row, tm_c, 1024, "mix_mlp0_ctx")

    w_in1 = hg_w_in[0].astype(BF16)
    ng1 = row2(norm_mix[1])
    lat1 = _inproj1(h_l, mod, 1, ng1, hg_lb_logits, w_in1, True, None, tm_l, "inproj1_lat")
    ctx1 = _inproj1(h_c, mod, 1, ng1, hg_lb_logits, w_in1, False, ctx_row, tm_c, "inproj1_ctx")
    o1 = _hgrn(row2(hg_norm[0]), *lat1, *ctx1)
    return _mix_mlp(h_l, o1, None, None, None, hg_w_out[0].astype(BF16), mod, 1, row2(norm_mlp[1]),
                    w_mlp_in[1].astype(BF16), w_mlp_out[1].astype(BF16), row2(norm_final), None, tm_l, 1024,
                    "mix_mlp1_lat")
```

```python
import functools

import jax
import jax.numpy as jnp
from jax import lax
from jax.experimental import pallas as pl
from jax.experimental.pallas import tpu as pltpu

F32 = jnp.float32
BF16 = jnp.bfloat16

D_MODEL = 1024
DEPTH = 2
GRID_W = 64
EPS = 1e-6
ROPE_BASE = 10000.0
N_MOD = 6
RET_HEADS = 4
RET_DK = 64
RET_DV = 128
RET_QK = RET_HEADS * RET_DK
RET_WIDTH = RET_HEADS * RET_DV
RET_CHUNK = 128
S5_WIDTH = D_MODEL - RET_WIDTH
S5_GROUP = 16
S5_GROUPS = S5_WIDTH // S5_GROUP
S5_STATE = 64
S5_CHUNK = 16
S5_ROW = S5_CHUNK * S5_GROUP
LANE = 128
S5_LB = S5_WIDTH // LANE
S5_GPB = LANE // S5_GROUP
S5_BIG = S5_CHUNK * LANE
S5_HALF = S5_GPB * 2 * S5_STATE
S5_POW = 32
AB_IN = 2 * RET_QK + 2 * RET_WIDTH + S5_WIDTH
HG_HEADS = 8
HG_DK = D_MODEL // HG_HEADS
HG_BLOCK = 64
HG_GROUP = 4
HG_SPLIT = 2
D_FF = 4 * D_MODEL
MOD_ROWS = 16

VMEM_LIMIT_BYTES = 56 * 1024 * 1024


def _cp(*sem):
    return pltpu.CompilerParams(dimension_semantics=sem, vmem_limit_bytes=VMEM_LIMIT_BYTES)


def _dot(a, b):
    return jnp.dot(a, b, preferred_element_type=F32)


def _dot_nt(a, b):
    return lax.dot_general(a, b, (((1,), (1,)), ((), ())), preferred_element_type=F32)


def _dot_tn(a, b):
    return lax.dot_general(a, b, (((0,), (0,)), ((), ())), preferred_element_type=F32)


def _sigmoid(x):
    return 0.5 * jnp.tanh(0.5 * x) + 0.5


def _rms(x):
    return x * lax.rsqrt(jnp.mean(x * x, axis=-1, keepdims=True) + EPS)


def _mod_chunk(mod_ref, row, i):
    return mod_ref[0, pl.ds(row, 1), i * D_MODEL:(i + 1) * D_MODEL]


def _adaln_kernel(cc_ref, w_ref, b_ref, o_ref):
    s = jax.nn.silu(cc_ref[...]).astype(BF16)
    o_ref[0] = _dot(s, w_ref[0].astype(BF16)) + b_ref[0]


def _adaln(cc, w_mod, b_mod):
    bn = 1536
    n = N_MOD * D_MODEL
    return pl.pallas_call(
        _adaln_kernel,
        out_shape=jax.ShapeDtypeStruct((DEPTH, MOD_ROWS, n), F32),
        grid=(DEPTH, n // bn),
        in_specs=[
            pl.BlockSpec((MOD_ROWS, D_MODEL), lambda l, j: (0, 0)),
            pl.BlockSpec((1, D_MODEL, bn), lambda l, j: (l, 0, j)),
            pl.BlockSpec((1, 1, bn), lambda l, j: (l, 0, j)),
        ],
        out_specs=pl.BlockSpec((1, MOD_ROWS, bn), lambda l, j: (l, 0, j)),
        compiler_params=_cp("parallel", "parallel"),
        name="adaln",
    )(cc, w_mod, b_mod.reshape(DEPTH, 1, n))


def _rope(t, cos, sin):
    lane = lax.broadcasted_iota(jnp.int32, t.shape, 1)
    first = (lane & (RET_DK // 2)) == 0
    w = t.shape[1]
    swapped = jnp.where(first, pltpu.roll(t, w - RET_DK // 2, 1), pltpu.roll(t, RET_DK // 2, 1))
    return t * cos + swapped * sin


def _inproj0_kernel(*refs, mod_row, rope):
    if rope:
        h_ref, mod_ref, ng_ref, w_ref, cos_ref, sin_ref, q_ref, k_ref, v_ref, u_ref, uz_ref, g_ref, u_s = refs
    else:
        h_ref, mod_ref, ng_ref, w_ref, q_ref, k_ref, v_ref, u_ref, uz_ref, g_ref, u_s = refs
    row = pl.program_id(0) if mod_row is None else mod_row
    xn = _rms(h_ref[0]) * ng_ref[...]
    xm = (xn * (1.0 + _mod_chunk(mod_ref, row, 1)) + _mod_chunk(mod_ref, row, 0)).astype(BF16)
    y = _dot(xm, w_ref[...].astype(BF16))
    q = y[:, 0:RET_QK]
    k = y[:, RET_QK:2 * RET_QK]
    if rope:
        q = _rope(q, cos_ref[...], sin_ref[...])
        k = _rope(k, cos_ref[...], sin_ref[...])
    q_ref[0] = q.astype(BF16)
    k_ref[0] = (k * (RET_DK ** -0.5)).astype(BF16)
    c0 = 2 * RET_QK
    v_ref[0] = y[:, c0:c0 + RET_WIDTH].astype(BF16)
    u0 = c0 + RET_WIDTH
    u_ref[0] = y[:, u0:u0 + S5_WIDTH]
    g = y[:, u0 + S5_WIDTH:]
    g_ref[0] = (g * _sigmoid(g)).astype(BF16)
    nch = u_s.shape[1] // S5_CHUNK
    for j in range(S5_LB):
        u_s[j] = y[:, u0 + j * LANE:u0 + (j + 1) * LANE]
        for s in range(S5_CHUNK):
            uz_ref[0, j, :, s * LANE:(s + 1) * LANE] = u_s[j, pl.ds(s, nch, stride=S5_CHUNK), :].astype(BF16)


def _inproj0(h, mod, layer, ng, w, cos, sin, mod_row, tm):
    b, n, _ = h.shape
    rope = cos is not None
    row_spec = lambda width: pl.BlockSpec((1, tm, width), lambda i, j: (i, j, 0))
    in_specs = [
        row_spec(D_MODEL),
        pl.BlockSpec((1, MOD_ROWS, N_MOD * D_MODEL), lambda i, j: (layer, 0, 0)),
        pl.BlockSpec((1, D_MODEL), lambda i, j: (0, 0)),
        pl.BlockSpec((D_MODEL, AB_IN), lambda i, j: (0, 0), pipeline_mode=pl.Buffered(1)),
    ]
    args = [h, mod, ng, w]
    if rope:
        in_specs += [pl.BlockSpec((tm, RET_QK), lambda i, j: (j, 0))] * 2
        args += [cos, sin]
    widths = (RET_QK, RET_QK, RET_WIDTH, S5_WIDTH, RET_WIDTH)
    dtypes = (BF16, BF16, BF16, F32, BF16)
    out_shape = [jax.ShapeDtypeStruct((b, n, wd), dt) for wd, dt in zip(widths, dtypes)]
    out_specs = [row_spec(wd) for wd in widths]
    out_shape.insert(4, jax.ShapeDtypeStruct((b, S5_LB, n // S5_CHUNK, S5_BIG), BF16))
    out_specs.insert(4, pl.BlockSpec((1, S5_LB, tm // S5_CHUNK, S5_BIG), lambda i, j: (i, 0, j, 0)))
    return pl.pallas_call(
        functools.partial(_inproj0_kernel, mod_row=mod_row, rope=rope),
        out_shape=out_shape,
        grid=(b, n // tm),
        in_specs=in_specs,
        out_specs=out_specs,
        scratch_shapes=[pltpu.VMEM((S5_LB, tm, LANE), F32)],
        compiler_params=_cp("parallel", "parallel"),
        name="inproj0_lat" if rope else "inproj0_ctx",
    )(*args)


def _ret_kernel(lg_ref, ql, kl, vl, gl, qc, kc, vc, gc, rl, rc, st_s, *, ncl, ncc, unroll):
    c = RET_CHUNK
    dk2 = 2 * RET_DK
    nt = ncc + ncl
    p = pl.program_id(1)
    h_a = 2 * p
    lgf_a = lg_ref[pl.ds(h_a, 1), :]
    lgf_b = lg_ref[pl.ds(h_a + 1, 1), :]
    lgb_a = lg_ref[pl.ds(RET_HEADS + h_a, 1), :]
    lgb_b = lg_ref[pl.ds(RET_HEADS + h_a + 1, 1), :]
    lane = lax.broadcasted_iota(jnp.int32, (1, 2 * RET_DK), 1)
    is_a = lane < RET_DK
    lgf_lane = jnp.where(is_a, lgf_a, lgf_b)
    lgb_lane = jnp.where(is_a, lgb_a, lgb_b)
    ri = lax.broadcasted_iota(jnp.int32, (c, c), 0).astype(F32)
    ci = lax.broadcasted_iota(jnp.int32, (c, c), 1).astype(F32)
    diff = ri - ci

    def dmat(lgf, lgb):
        fwd = jnp.exp(jnp.maximum(diff, 0.0) * lgf)
        bwd = jnp.exp(jnp.maximum(-diff, 0.0) * lgb)
        return jnp.where(diff > 0, fwd, jnp.where(diff < 0, bwd, 2.0))

    d_a = dmat(lgf_a, lgb_a)
    d_b = dmat(lgf_b, lgb_b)
    rowp = lax.broadcasted_iota(jnp.int32, (c, dk2), 0).astype(F32)
    qd = jnp.concatenate([jnp.exp((rowp + 1.0) * lgf_lane), jnp.exp((c - rowp) * lgb_lane)], axis=1)
    kd = jnp.concatenate([jnp.exp((c - 1.0 - rowp) * lgf_lane), jnp.exp(rowp * lgb_lane)], axis=1)
    rowk = lax.broadcasted_iota(jnp.int32, (dk2, 2 * RET_DV), 0)
    cd_f = jnp.exp(c * jnp.where(rowk < RET_DK, lgf_a[:, :1], lgf_b[:, :1]))
    cd_b = jnp.exp(c * jnp.where(rowk < RET_DK, lgb_a[:, :1], lgb_b[:, :1]))
    mask2 = jnp.concatenate([is_a, is_a], axis=1)

    def increment(k, v, slot):
        kk = jnp.concatenate([k, k], axis=1).astype(F32) * kd
        st_s[slot] = _dot_tn(kk.astype(BF16), v)

    for n in range(ncc):
        increment(kc[0, n * c:(n + 1) * c, :], vc[0, n * c:(n + 1) * c, :], n)

    def inc_body(i, carry):
        for j in range(unroll):
            n = i * unroll + j
            sl = pl.ds(pl.multiple_of(n * c, c), c)
            increment(kl[0, sl, :], vl[0, sl, :], ncc + n)
        return carry

    lax.fori_loop(0, ncl // unroll, inc_body, 0)

    def rec_body(t, carry):
        sf, sb = carry
        u = jnp.where(t < ncc, ncc - 1 - t, nt - 1 - (t - ncc))
        inc_f = st_s[t, 0:dk2, :]
        inc_b = st_s[u, dk2:2 * dk2, :]
        st_s[t, 0:dk2, :] = sf
        st_s[u, dk2:2 * dk2, :] = sb
        return cd_f * sf + inc_f, cd_b * sb + inc_b

    zero = jnp.zeros((dk2, 2 * RET_DV), F32)
    lax.fori_loop(0, nt, rec_body, (zero, zero))

    def output(q, k, v, g, slot, out_ref, st):
        q2 = (jnp.concatenate([q, q], axis=1).astype(F32) * qd).astype(BF16)
        s_n = st_s[slot].astype(BF16)
        for keep_a, dm, cs in ((True, d_a, 0), (False, d_b, RET_DV)):
            m1 = is_a if keep_a else jnp.logical_not(is_a)
            m2 = mask2 if keep_a else jnp.logical_not(mask2)
            att = _dot_nt(jnp.where(m1, q, jnp.zeros_like(q)), k) * dm
            o = (_dot(att.astype(BF16), v[:, cs:cs + RET_DV])
                 + _dot(jnp.where(m2, q2, jnp.zeros_like(q2)), s_n[:, cs:cs + RET_DV]))
            o = _rms(o) * g[:, cs:cs + RET_DV].astype(F32)
            out_ref[0, pl.ds(st, c), cs:cs + RET_DV] = o.astype(BF16)

    for n in range(ncc):
        sl = slice(n * c, (n + 1) * c)
        output(qc[0, sl, :], kc[0, sl, :], vc[0, sl, :], gc[0, sl, :], n, rc, n * c)

    def out_body(i, carry):
        for j in range(unroll):
            n = i * unroll + j
            st = pl.multiple_of(n * c, c)
            sl = pl.ds(st, c)
            output(ql[0, sl, :], kl[0, sl, :], vl[0, sl, :], gl[0, sl, :], ncc + n, rl, st)
        return carry

    lax.fori_loop(0, ncl // unroll, out_body, 0)


def _retention(lg_rows, q_l, k_l, v_l, g_l, q_c, k_c, v_c, g_c):
    b, n, _ = q_l.shape
    nc = q_c.shape[1]
    ncl, ncc = n // RET_CHUNK, nc // RET_CHUNK
    pairs = RET_HEADS // 2
    unroll = 8 if ncl % 8 == 0 else 1

    def spec(rows, width):
        return pl.BlockSpec((1, rows, width), lambda i, p: (i, 0, p))

    return pl.pallas_call(
        functools.partial(_ret_kernel, ncl=ncl, ncc=ncc, unroll=unroll),
        out_shape=[jax.ShapeDtypeStruct((b, n, RET_WIDTH), BF16),
                   jax.ShapeDtypeStruct((b, nc, RET_WIDTH), BF16)],
        grid=(b, pairs),
        in_specs=[pl.BlockSpec((2 * RET_HEADS, 2 * RET_DK), lambda i, p: (0, 0)),
                  spec(n, 2 * RET_DK), spec(n, 2 * RET_DK), spec(n, 2 * RET_DV), spec(n, 2 * RET_DV),
                  spec(nc, 2 * RET_DK), spec(nc, 2 * RET_DK), spec(nc, 2 * RET_DV), spec(nc, 2 * RET_DV)],
        out_specs=[spec(n, 2 * RET_DV), spec(nc, 2 * RET_DV)],
        scratch_shapes=[pltpu.VMEM((ncl + ncc, 4 * RET_DK, 2 * RET_DV), F32)],
        compiler_params=_cp("parallel", "parallel"),
        name="retention",
    )(lg_rows, q_l, k_l, v_l, g_l, q_c, k_c, v_c, g_c)


def _dot_hi(a, b, contract=(1, 0)):
    dims = (((contract[0],), (contract[1],)), ((), ()))
    return lax.dot_general(a, b, dims, preferred_element_type=F32, precision=lax.Precision.HIGHEST)


def _dot_sel(a, b, contract, data):
    dims = (((contract[0],), (contract[1],)), ((), ()))
    x = (a, b)[data]
    hi = x.astype(BF16)
    lo = (x - hi.astype(F32)).astype(BF16)
    dd = lambda piece: lax.dot_general(*((piece, b) if data == 0 else (a, piece)), dims, preferred_element_type=F32)
    return dd(hi) + dd(lo)


def _s5_prep_kernel(ar_row, ai_row, ldt, btr, bti, ctr, cti, rm, rwin, rwout, abig):
    t, g, p, kp = S5_CHUNK, S5_GROUP, S5_STATE, S5_POW
    row = S5_ROW
    i0 = lambda shape: lax.broadcasted_iota(jnp.int32, shape, 0)
    i1 = lambda shape: lax.broadcasted_iota(jnp.int32, shape, 1)
    f32 = lambda m: jnp.where(m, 1.0, 0.0).astype(BF16)
    s_of_r = i0((row, kp)) // g
    k_of_l = i1((row, kp))
    sel_rows = (f32(k_of_l == t - 1 - s_of_r), f32(k_of_l == s_of_r))
    t_of_c = i1((kp, row)) // g
    k_of_s = i0((kp, row))
    sel_out = (f32(k_of_s == t_of_c + 1), f32(k_of_s == t - t_of_c))
    sel_lag = (f32(k_of_s == t_of_c), f32(k_of_s == t - 1 - t_of_c))
    tile_l = f32(i1((g, row)) % g == i0((g, row)))
    tile_r = f32(i0((row, g)) % g == i1((row, g)))
    lane = i1((g, row))
    k_col = i0((kp, 1)).astype(F32)
    k_row = i1((1, kp)).astype(F32)
    first = i0((8, 1)) == 0

    def outer(a, k):
        a8 = jnp.where(first, jnp.broadcast_to(a, (8, a.shape[1])), 0.0)
        return _dot_hi(a8, jnp.broadcast_to(k, (8, k.shape[1])), (0, 0))

    for q in range(S5_GPB):
        lags = []
        for d in range(2):
            dt = jnp.exp(ldt[d, q])
            are_r, aim_r = ar_row[d, q], ai_row[d, q]
            mag = jnp.exp(are_r * dt)
            ang = aim_r * dt
            nr, ni = mag * jnp.cos(ang) - 1.0, mag * jnp.sin(ang)
            den = jnp.square(are_r) + jnp.square(aim_r)
            fr = (nr * are_r + ni * aim_r) / den
            fi = (ni * are_r - nr * aim_r) / den
            pm = jnp.exp(k_col * (are_r * dt))
            pa = k_col * ang
            pk_re, pk_im = pm * jnp.cos(pa), pm * jnp.sin(pa)
            pmt = jnp.exp(outer(are_r * dt, k_row))
            pat = outer(ang, k_row)
            pt_re, pt_im = pmt * jnp.cos(pat), pmt * jnp.sin(pat)
            bt_re, bt_im = _dot_sel(tile_r, btr[d, q], (1, 1), 1), _dot_sel(tile_r, bti[d, q], (1, 1), 1)
            bb_re = fr * bt_re - fi * bt_im
            bb_im = fr * bt_im + fi * bt_re
            pr_re, pr_im = _dot_sel(sel_rows[d], pk_re, (1, 0), 1), _dot_sel(sel_rows[d], pk_im, (1, 0), 1)
            w_re = pr_re * bb_re - pr_im * bb_im
            w_im = pr_re * bb_im + pr_im * bb_re
            for s in range(t):
                rows = slice(s * LANE + q * g, s * LANE + (q + 1) * g)
                rwin[0, rows, d * p:(d + 1) * p] = w_re[s * g:(s + 1) * g].astype(BF16)
                rwin[0, rows, (2 + d) * p:(3 + d) * p] = w_im[s * g:(s + 1) * g].astype(BF16)
            ct_re, ct_im = _dot_sel(ctr[d, q], tile_l, (0, 0), 0), _dot_sel(cti[d, q], tile_l, (0, 0), 0)

            def c_pow(sel):
                pc_re, pc_im = _dot_sel(pt_re, sel, (1, 0), 0), _dot_sel(pt_im, sel, (1, 0), 0)
                return ct_re * pc_re - ct_im * pc_im, ct_re * pc_im + ct_im * pc_re

            o_re, o_im = c_pow(sel_out[d])
            r0 = d * S5_GPB * p + q * p
            rwout[0, r0:r0 + p, :] = o_re.astype(BF16)
            rwout[0, S5_HALF + r0:S5_HALF + r0 + p, :] = (-o_im).astype(BF16)
            l_re, l_im = c_pow(sel_lag[d])
            lags.append(_dot_hi(bb_re[0:g], l_re) - _dot_hi(bb_im[0:g], l_im))
            abig[0, 0:1, r0:r0 + p] = pk_re[t:t + 1, :]
            abig[0, 1:2, r0:r0 + p] = pk_im[t:t + 1, :]
        for s in range(t):
            fwd = jnp.where(lane >= g * s, pltpu.roll(lags[0], g * s, 1), 0.0)
            bwd = jnp.where(lane < g * (s + 1), pltpu.roll(lags[1], (row - g * (t - 1 - s)) % row, 1), 0.0)
            rm[0, s * LANE + q * g:s * LANE + (q + 1) * g, :] = (fwd + bwd).astype(BF16)


def _s5_prep(a_re, a_im, log_dt, b_re, b_im, c_re, c_im):
    gg, p, g = S5_GROUPS, S5_STATE, S5_GROUP
    f = lambda x: x.astype(F32)
    args = (f(a_re).reshape(2, gg, 1, p), f(a_im).reshape(2, gg, 1, p),
            f(log_dt).reshape(2, gg, 1, 1),
            f(b_re), f(b_im), f(c_re), f(c_im))
    spec = lambda r, c: pl.BlockSpec((2, S5_GPB, r, c), lambda j: (0, j, 0, 0))
    out = lambda r, c: pl.BlockSpec((1, r, c), lambda j: (j, 0, 0))
    return pl.pallas_call(
        _s5_prep_kernel,
        out_shape=[jax.ShapeDtypeStruct((S5_LB, S5_BIG, S5_ROW), BF16),
                   jax.ShapeDtypeStruct((S5_LB, S5_BIG, S5_ROW), BF16),
                   jax.ShapeDtypeStruct((S5_LB, 2 * S5_HALF, S5_ROW), BF16),
                   jax.ShapeDtypeStruct((S5_LB, 2, S5_HALF), F32)],
        grid=(S5_LB,),
        in_specs=[spec(1, p), spec(1, p), spec(1, 1),
                  spec(p, g), spec(p, g), spec(g, p), spec(g, p)],
        out_specs=[out(S5_BIG, S5_ROW), out(S5_BIG, S5_ROW), out(2 * S5_HALF, S5_ROW), out(2, S5_HALF)],
        compiler_params=_cp("parallel"),
        name="s5_prep",
    )(*args)


def _s5_expand(r_ref, col_unit, col_block, row_unit):
    n = S5_BIG
    a = lax.broadcasted_iota(jnp.int32, (S5_ROW, n), 0)
    c = lax.broadcasted_iota(jnp.int32, (S5_ROW, n), 1)
    e = jnp.where((a // col_unit == c // col_block) & (a % col_unit == c % col_unit), 1.0, 0.0).astype(BF16)
    x = _dot(r_ref[0], e)
    rq = (lax.broadcasted_iota(jnp.int32, (n, n), 0) // row_unit) % S5_GPB
    cq = (lax.broadcasted_iota(jnp.int32, (n, n), 1) // col_unit) % S5_GPB
    return jnp.where(rq == cq, x, 0.0).astype(BF16)


def _s5_kernel(ul, uc, rm_ref, rwin_ref, rwout_ref, a_ref, yl, yc, x_s, m_s, win_s, wout_s, *, ncl, ncc):
    hw = S5_HALF
    hh = hw // 2

    @pl.when(pl.program_id(1) == 0)
    def _():
        m_s[...] = _s5_expand(rm_ref, S5_GROUP, LANE, S5_GROUP)
        win_s[...] = _s5_expand(rwin_ref, S5_STATE, S5_GPB * S5_STATE, S5_GROUP)
        wout_s[...] = _s5_expand(rwout_ref, S5_GROUP, LANE, S5_STATE)

    z = jnp.concatenate([uc[0, 0], ul[0, 0]], axis=0)
    x_s[...] = _dot(z, win_s[...])
    a_re = a_ref[0, 0:1, :]
    a_im = a_ref[0, 1:2, :]

    def segment(base, n, carry):
        s_re, s_im = carry
        for i in range(n):
            rf = slice(base + i, base + i + 1)
            rb = slice(base + n - 1 - i, base + n - i)
            x_re = jnp.concatenate([x_s[rf, 0:hh], x_s[rb, hh:hw]], axis=-1)
            x_im = jnp.concatenate([x_s[rf, hw:hw + hh], x_s[rb, hw + hh:2 * hw]], axis=-1)
            x_s[rf, 0:hh] = s_re[:, 0:hh]
            x_s[rb, hh:hw] = s_re[:, hh:hw]
            x_s[rf, hw:hw + hh] = s_im[:, 0:hh]
            x_s[rb, hw + hh:2 * hw] = s_im[:, hh:hw]
            s_re, s_im = a_re * s_re - a_im * s_im + x_re, a_re * s_im + a_im * s_re + x_im
        return s_re, s_im

    zero = jnp.zeros((1, hw), F32)
    carry = segment(0, ncc, (zero, zero))
    segment(ncc, ncl, carry)
    y = _dot(z, m_s[...]) + _dot(x_s[...].astype(BF16), wout_s[...])
    yc[0, 0] = y[0:ncc]
    yl[0, 0] = y[ncc:ncc + ncl]


def _s5(prep, zl, zc):
    rm, rwin, rwout, a_big = prep
    b, _, ncl, _ = zl.shape
    ncc = zc.shape[2]
    rows = lambda r: pl.BlockSpec((1, 1, r, S5_BIG), lambda j, i: (i, j, 0, 0))
    wspec = lambda r, c: pl.BlockSpec((1, r, c), lambda j, i: (j, 0, 0))
    big = pltpu.VMEM((S5_BIG, S5_BIG), BF16)
    return pl.pallas_call(
        functools.partial(_s5_kernel, ncl=ncl, ncc=ncc),
        out_shape=[jax.ShapeDtypeStruct(zl.shape, F32), jax.ShapeDtypeStruct(zc.shape, F32)],
        grid=(S5_LB, b),
        in_specs=[rows(ncl), rows(ncc), wspec(S5_BIG, S5_ROW), wspec(S5_BIG, S5_ROW),
                  wspec(2 * S5_HALF, S5_ROW), wspec(2, S5_HALF)],
        out_specs=[rows(ncl), rows(ncc)],
        scratch_shapes=[pltpu.VMEM((ncc + ncl, 2 * S5_HALF), F32), big, big, big],
        compiler_params=_cp("parallel", "arbitrary"),
        name="s5",
    )(zl, zc, rm, rwin, rwout, a_big)


def _mix_mlp_kernel(*refs, mod_row, s5_merge, final_norm, fb):
    if s5_merge:
        (h_ref, r_ref, y5_ref, u_ref, ds_ref, wg_ref, bg_ref, wo_ref, mod_ref, nm_ref, w1_ref, w2_ref,
         *rest) = refs
    else:
        h_ref, r_ref, wo_ref, mod_ref, nm_ref, w1_ref, w2_ref, *rest = refs
    if final_norm:
        nf_ref, o_ref, *scratch = rest
    else:
        o_ref, *scratch = rest
    row = pl.program_id(0) if mod_row is None else mod_row
    if s5_merge:
        (y_s,) = scratch
        nch = y_s.shape[1] // S5_CHUNK
        for lb in range(S5_LB):
            for s in range(S5_CHUNK):
                y_s[lb, pl.ds(s, nch, stride=S5_CHUNK), :] = y5_ref[0, lb, :, s * LANE:(s + 1) * LANE]
        y5 = jnp.concatenate([y_s[lb] for lb in range(S5_LB)], axis=-1)
        y = jax.nn.gelu(y5 + ds_ref[...] * u_ref[0])
        y = y * jax.nn.sigmoid(_dot(y.astype(BF16), wg_ref[...]) + bg_ref[...])
        mix = _dot(r_ref[0], wo_ref[0:RET_WIDTH, :]) + _dot(y.astype(BF16), wo_ref[RET_WIDTH:D_MODEL, :])
    else:
        mix = _dot(r_ref[0], wo_ref[...])
    h1 = h_ref[0] + _mod_chunk(mod_ref, row, 2) * mix
    xn = _rms(h1) * nm_ref[...]
    xm = (xn * (1.0 + _mod_chunk(mod_ref, row, 4)) + _mod_chunk(mod_ref, row, 3)).astype(BF16)
    acc = None
    for j in range(D_FF // fb):
        a = jnp.square(jnp.maximum(_dot(xm, w1_ref[:, j * fb:(j + 1) * fb].astype(BF16)), 0.0)).astype(BF16)
        part = _dot(a, w2_ref[j * fb:(j + 1) * fb, :].astype(BF16))
        acc = part if acc is None else acc + part
    h2 = h1 + _mod_chunk(mod_ref, row, 5) * acc
    if final_norm:
        h2 = _rms(h2) * nf_ref[...]
    o_ref[0] = h2


def _mix_mlp(h, r, s5y, u, s5p, wo, mod, layer, nm, w1, w2, nf, mod_row, tm, fb, name):
    b, n, _ = h.shape
    s5_merge = s5y is not None
    final_norm = nf is not None
    one = pl.Buffered(1)
    row_spec = lambda width: pl.BlockSpec((1, tm, width), lambda i, t: (i, t, 0))
    const = lambda shape: pl.BlockSpec(shape, lambda i, t: (0,) * len(shape), pipeline_mode=one)
    in_specs = [row_spec(D_MODEL), row_spec(r.shape[-1])]
    args = [h, r]
    if s5_merge:
        d_skip, w_glu, b_glu = s5p
        z_spec = pl.BlockSpec((1, S5_LB, tm // S5_CHUNK, S5_BIG), lambda i, t: (i, 0, t, 0))
        in_specs += [z_spec, row_spec(S5_WIDTH), const((1, S5_WIDTH)),
                     const((S5_WIDTH, S5_WIDTH)), const((1, S5_WIDTH))]
        args += [s5y, u, d_skip, w_glu, b_glu]
    in_specs += [const((D_MODEL, D_MODEL)),
                 pl.BlockSpec((1, MOD_ROWS, N_MOD * D_MODEL), lambda i, t: (layer, 0, 0), pipeline_mode=one),
                 const((1, D_MODEL)), const((D_MODEL, D_FF)), const((D_FF, D_MODEL))]
    args += [wo, mod, nm, w1, w2]
    if final_norm:
        in_specs.append(const((1, D_MODEL)))
        args.append(nf)
    return pl.pallas_call(
        functools.partial(_mix_mlp_kernel, mod_row=mod_row, s5_merge=s5_merge, final_norm=final_norm, fb=fb),
        out_shape=jax.ShapeDtypeStruct((b, n, D_MODEL), F32),
        grid=(b, n // tm),
        in_specs=in_specs,
        out_specs=row_spec(D_MODEL),
        scratch_shapes=[pltpu.VMEM((S5_LB, tm, LANE), F32)] if s5_merge else [],
        compiler_params=_cp("parallel", "parallel"),
        name=name,
    )(*args)


def _hgrn_lower_bounds(lbl_ref, layer):
    out = []
    for d in range(2):
        z = [lbl_ref[d, k:k + 1, :] for k in range(DEPTH)]
        zmax = functools.reduce(jnp.maximum, z)
        e = [jnp.exp(v - zmax) for v in z]
        tot = functools.reduce(lambda a, b_: a + b_, e)
        lb = jnp.zeros_like(tot)
        for k in range(1, layer + 1):
            lb = lb + e[k] / tot
        out.append(lb)
    return out


def _inproj1_kernel(h_ref, mod_ref, ng_ref, lbl_ref, w_ref, *out_refs, mod_row, layer, latent):
    row = pl.program_id(0) if mod_row is None else mod_row
    xn = _rms(h_ref[0]) * ng_ref[...]
    xm = (xn * (1.0 + _mod_chunk(mod_ref, row, 1)) + _mod_chunk(mod_ref, row, 0)).astype(BF16)
    col = lambda k: _dot(xm, w_ref[:, k * D_MODEL:(k + 1) * D_MODEL].astype(BF16))
    lbs = _hgrn_lower_bounds(lbl_ref, layer)
    refs = list(out_refs)
    if latent:
        refs.pop(0)[0] = col(0).astype(BF16)
    for d in range(2):
        t = (1.0 - lbs[d]) * _sigmoid(col(1 + d))
        refs.pop(0)[0] = jnp.log(lbs[d] + t)
        refs.pop(0)[0] = ((1.0 - lbs[d]) - t).astype(BF16)
    refs.pop(0)[0] = col(3).astype(BF16)
    if latent:
        g = col(4)
        refs.pop(0)[0] = (g * _sigmoid(g)).astype(BF16)


def _inproj1(h, mod, layer, ng, lb_logits, w, latent, mod_row, tm, name):
    b, n, _ = h.shape
    one = pl.Buffered(1)
    row_spec = pl.BlockSpec((1, tm, D_MODEL), lambda i, t: (i, t, 0))
    dtypes = ([BF16] if latent else []) + [F32, BF16, F32, BF16, BF16] + ([BF16] if latent else [])
    return pl.pallas_call(
        functools.partial(_inproj1_kernel, mod_row=mod_row, layer=layer, latent=latent),
        out_shape=[jax.ShapeDtypeStruct((b, n, D_MODEL), dt) for dt in dtypes],
        grid=(b, n // tm),
        in_specs=[row_spec,
                  pl.BlockSpec((1, MOD_ROWS, N_MOD * D_MODEL), lambda i, t: (layer, 0, 0), pipeline_mode=one),
                  pl.BlockSpec((1, D_MODEL), lambda i, t: (0, 0), pipeline_mode=one),
                  pl.BlockSpec(lb_logits.shape, lambda i, t: (0, 0, 0), pipeline_mode=one),
                  pl.BlockSpec(w.shape, lambda i, t: (0, 0), pipeline_mode=one)],
        out_specs=[row_spec] * len(dtypes),
        compiler_params=_cp("parallel", "parallel"),
        name=name,
    )(h, mod, ng, lb_logits, w)


def _cumsum_mm(tri, x):
    acc = None
    r = x
    for i in range(HG_SPLIT):
        p = r.astype(BF16)
        acc = _dot(tri, p) if acc is None else acc + _dot(tri, p)
        if i + 1 < HG_SPLIT:
            r = r - p.astype(F32)
    return acc


def _hgrn_kernel(ng_ref, ql, lffl, kfl, lfbl, kbl, il, sgl, lffc, kfc, lfbc, kbc, ic, o_ref,
                 qin_s, att_s, kv_s, et_s, kvc_s, etc_s, cum_s, ko_s, qt_s, kt_s, *, nbl, nbc, unroll, out_blocks):
    cb = HG_BLOCK
    mid = cb // 2
    gb = HG_GROUP
    gr = gb * cb
    dk = HG_DK

    ri = lax.broadcasted_iota(jnp.int32, (gr, gr), 0)
    ci = lax.broadcasted_iota(jnp.int32, (gr, gr), 1)
    same = (ri // cb) == (ci // cb)
    rb = lax.broadcasted_iota(jnp.int32, (cb, cb), 0)
    cbi = lax.broadcasted_iota(jnp.int32, (cb, cb), 1)
    tri_l = jnp.where(same & (ri >= ci), 1.0, 0.0).astype(BF16)
    dirs = ((0, rb >= cbi, mid - 1, cb - 1), (1, rb <= cbi, mid, 0))

    def cumsums(lfs, slot):
        pre = _cumsum_mm(tri_l, jnp.concatenate(lfs, axis=-1))
        pre_b = pre[:, dk:].reshape(gb, cb, dk)
        cum_s[slot, 0] = pre[:, :dk]
        cum_s[slot, 1] = (pre_b[:, cb - 1:cb, :] - pre_b).reshape(gr, dk) + lfs[1]

    def operands(slot, kks, q, n0, et_ref):
        for d, keep, ref_row, tot_row in dirs:
            cum = cum_s[slot, d].reshape(gb, cb, dk)
            kk = kks[d].astype(F32).reshape(gb, cb, dk)
            ref = cum[:, ref_row:ref_row + 1, :]
            tot = cum[:, tot_row:tot_row + 1, :]
            e = cum - ref
            kt = kk * jnp.exp(-e)
            ko_s[slot, d] = (kt * jnp.exp(tot - ref)).astype(BF16).reshape(gr, dk)
            e_tot = jnp.exp(tot)
            for j in range(gb):
                et_ref[n0 + j, :, d * dk:(d + 1) * dk] = e_tot[j]
            if q is not None:
                qt = q.astype(F32).reshape(gb, cb, dk) * jnp.exp(e)
                qin_s[pl.ds(pl.multiple_of(n0 * cb, gr), gr), d * dk:(d + 1) * dk] = (
                    (qt * jnp.exp(ref)).astype(BF16).reshape(gr, dk))
                qt_s[slot, d] = qt.astype(BF16).reshape(gr, dk)
                kt_s[slot, d] = kt.astype(BF16).reshape(gr, dk)

    def matmuls(slot, v, with_q, n0, kv_ref):
        for d in range(2):
            for j in range(gb):
                rows = slice(j * cb, (j + 1) * cb)
                kv_ref[n0 + j, :, d * dk:(d + 1) * dk] = _dot_tn(v[rows], ko_s[slot, d, rows, :])
        if with_q:
            for j in range(gb):
                rows = slice(j * cb, (j + 1) * cb)
                att = (jnp.where(dirs[0][1], _dot_nt(qt_s[slot, 0, rows, :], kt_s[slot, 0, rows, :]), 0.0)
                       + jnp.where(dirs[1][1], _dot_nt(qt_s[slot, 1, rows, :], kt_s[slot, 1, rows, :]), 0.0))
                att_s[n0 + j] = att.astype(BF16)

    for g in range(nbc // gb):
        sl = slice(g * gr, (g + 1) * gr)
        cumsums((lffc[0, sl, :], lfbc[0, sl, :]), 0)
        operands(0, (kfc[0, sl, :], kbc[0, sl, :]), None, g * gb, etc_s)
        matmuls(0, ic[0, sl, :], False, g * gb, kvc_s)

    ngl = nbl // gb
    rows_of = lambda g: pl.ds(pl.multiple_of(g * gr, gr), gr)

    def lat_cumsums(g, slot):
        cumsums((lffl[0, rows_of(g), :], lfbl[0, rows_of(g), :]), slot)

    def lat_operands(g, slot):
        operands(slot, (kfl[0, rows_of(g), :], kbl[0, rows_of(g), :]), ql[0, rows_of(g), :], g * gb, et_s)

    def lat_matmuls(g, slot):
        matmuls(slot, il[0, rows_of(g), :], True, g * gb, kv_s)

    lat_cumsums(0, 0)
    lat_operands(0, 0)
    lat_cumsums(1, 1)

    def prep_body(i, carry):
        g = 2 * i
        lat_matmuls(g, 0)
        lat_operands(g + 1, 1)
        lat_cumsums(g + 2, 0)
        lat_matmuls(g + 1, 1)
        lat_operands(g + 2, 0)
        lat_cumsums(g + 3, 1)
        return carry

    lax.fori_loop(0, ngl // 2 - 1, prep_body, 0)
    lat_matmuls(ngl - 2, 0)
    lat_operands(ngl - 1, 1)
    lat_matmuls(ngl - 1, 1)

    lane = lax.broadcasted_iota(jnp.int32, (dk, 2 * dk), 1)
    is_f = lane < dk
    st = jnp.zeros((dk, 2 * dk), F32)
    for n in range(nbc):
        m = nbc - 1 - n
        st = (st * jnp.where(is_f[:1], etc_s[n], etc_s[m]) + jnp.where(is_f, kvc_s[n], kvc_s[m]))

    def rec_body(t, st):
        u = nbl - 1 - t
        inc = jnp.where(is_f, kv_s[t], kv_s[u])
        dec = jnp.where(is_f[:1], et_s[t], et_s[u])
        kv_s[t, :, 0:dk] = st[:, 0:dk]
        kv_s[u, :, dk:2 * dk] = st[:, dk:2 * dk]
        return st * dec + inc

    lax.fori_loop(0, nbl, rec_body, st, unroll=unroll)

    def out_body(i, carry):
        for j in range(out_blocks):
            n = i * out_blocks + j
            sl = pl.ds(pl.multiple_of(n * cb, cb), cb)
            o = _dot(att_s[n], il[0, sl, :]) + _dot_nt(qin_s[sl, :], kv_s[n].astype(BF16))
            o = _rms(o) * ng_ref[...] * sgl[0, sl, :].astype(F32)
            o_ref[0, sl, :] = o.astype(BF16)
        return carry

    lax.fori_loop(0, nbl // out_blocks, out_body, 0)


def _hgrn(norm_g, q_l, lff_l, kf_l, lfb_l, kb_l, i_l, sg_l, lff_c, kf_c, lfb_c, kb_c, i_c):
    b, n, _ = q_l.shape
    nc = lff_c.shape[1]
    nbl, nbc = n // HG_BLOCK, nc // HG_BLOCK
    out_blocks = min(16, nbl)
    assert nbl % (2 * HG_GROUP) == 0 and nbc % HG_GROUP == 0 and nbl % out_blocks == 0
    spec = lambda rows: pl.BlockSpec((1, rows, HG_DK), lambda i, h: (i, 0, h))
    slot = lambda dt: pltpu.VMEM((2, 2, HG_GROUP * HG_BLOCK, HG_DK), dt)
    return pl.pallas_call(
        functools.partial(_hgrn_kernel, nbl=nbl, nbc=nbc, unroll=2, out_blocks=out_blocks),
        out_shape=jax.ShapeDtypeStruct((b, n, D_MODEL), BF16),
        grid=(b, HG_HEADS),
        in_specs=[pl.BlockSpec((1, HG_DK), lambda i, h: (0, 0))] + [spec(n)] * 7 + [spec(nc)] * 5,
        out_specs=spec(n),
        scratch_shapes=[pltpu.VMEM((n, 2 * HG_DK), BF16),
                        pltpu.VMEM((nbl, HG_BLOCK, HG_BLOCK), BF16),
                        pltpu.VMEM((nbl, HG_DK, 2 * HG_DK), F32),
                        pltpu.VMEM((nbl, 1, 2 * HG_DK), F32),
                        pltpu.VMEM((nbc, HG_DK, 2 * HG_DK), F32),
                        pltpu.VMEM((nbc, 1, 2 * HG_DK), F32),
                        slot(F32), slot(BF16), slot(BF16), slot(BF16)],
        compiler_params=_cp("parallel", "parallel"),
        name="hgrn2",
    )(norm_g, q_l, lff_l, kf_l, lfb_l, kb_l, i_l, sg_l, lff_c, kf_c, lfb_c, kb_c, i_c)


def _rope_tables(n_tok):
    tok = jnp.arange(n_tok, dtype=jnp.int32)[:, None]
    row = (tok // GRID_W).astype(F32)
    col = (tok % GRID_W).astype(F32)
    n_freq = RET_DK // 4
    lane = jnp.arange(RET_QK, dtype=jnp.int32)[None, :]
    j = lane % (2 * n_freq)
    inv = ROPE_BASE ** (-(j % n_freq).astype(F32) / n_freq)
    ang = jnp.where(j < n_freq, row, col) * inv
    sign = jnp.where(lane % RET_DK < RET_DK // 2, -1.0, 1.0)
    return jnp.cos(ang), jnp.sin(ang) * sign


def kernel(x, c, ctx, c_ctx, w_mod, b_mod, norm_mix, norm_mlp, w_mlp_in, w_mlp_out, ab_w_in, ab_w_out, ret_logit, s5_a_re, s5_a_im, s5_log_dt, s5_b_re, s5_b_im, s5_c_re, s5_c_im, s5_d, s5_w_glu, s5_b_glu, hg_w_in, hg_w_out, hg_lb_logits, hg_norm, norm_final):
    b, n, d = x.shape
    nc = ctx.shape[1]
    assert d == D_MODEL and b + 1 <= MOD_ROWS and w_mod.shape[0] == DEPTH == 2
    assert n % 512 == 0 and nc % 256 == 0 and n % GRID_W == 0
    ctx_row = b
    tm_l, tm_c = 512, 256

    cc = jnp.zeros((MOD_ROWS, d), F32).at[:b].set(c).at[b].set(c_ctx)
    mod = _adaln(cc, w_mod, b_mod)

    row2 = lambda a: a.reshape(1, -1)
    w_in0 = ab_w_in[0]
    cos, sin = _rope_tables(n)
    ng0 = row2(norm_mix[0])
    q_l, k_l, v_l, u_l, uz_l, g_l = _inproj0(x, mod, 0, ng0, w_in0, cos, sin, None, tm_l)
    q_c, k_c, v_c, u_c, uz_c, g_c = _inproj0(ctx, mod, 0, ng0, w_in0, None, None, ctx_row, tm_c)

    log_gamma = jax.nn.log_sigmoid(ret_logit[0].astype(F32))
    lg_rows = jnp.broadcast_to(log_gamma.reshape(2 * RET_HEADS, 1), (2 * RET_HEADS, 2 * RET_DK))
    r_l, r_c = _retention(lg_rows, q_l, k_l, v_l, g_l, q_c, k_c, v_c, g_c)

    s5_ops = _s5_prep(s5_a_re[0], s5_a_im[0], s5_log_dt[0], s5_b_re[0], s5_b_im[0], s5_c_re[0], s5_c_im[0])
    y5_l, y5_c = _s5(s5_ops, uz_l, uz_c)

    s5p = (row2(s5_d[0]), s5_w_glu[0].astype(BF16), row2(s5_b_glu[0]))
    wo0 = ab_w_out[0].astype(BF16)
    w1_0, w2_0 = w_mlp_in[0].astype(BF16), w_mlp_out[0].astype(BF16)
    nm0 = row2(norm_mlp[0])
    h_l = _mix_mlp(x, r_l, y5_l, u_l, s5p, wo0, mod, 0, nm0, w1_0, w2_0, None, None, tm_l, 1024, "mix_mlp0_lat")
    h_c = _mix_mlp(ctx, r_c, y5_c, u_c, s5p, wo0, mod, 0, nm0, w1_0, w2_0, None, ctx_row, tm_c, 1024, "mix_mlp0_ctx")

    w_in1 = hg_w_in[0]
    ng1 = row2(norm_mix[1])
    lat1 = _inproj1(h_l, mod, 1, ng1, hg_lb_logits, w_in1, True, None, tm_l, "inproj1_lat")
    ctx1 = _inproj1(h_c, mod, 1, ng1, hg_lb_logits, w_in1, False, ctx_row, tm_c, "inproj1_ctx")
    o1 = _hgrn(row2(hg_norm[0]), *lat1, *ctx1)
    return _mix_mlp(h_l, o1, None, None, None, hg_w_out[0].astype(BF16), mod, 1, row2(norm_mlp[1]),
                    w_mlp_in[1], w_mlp_out[1], row2(norm_final), None, tm_l, 1024,
                    "mix_mlp1_lat")
```

```python
import functools

import jax
import jax.numpy as jnp
from jax import lax
from jax.experimental import pallas as pl
from jax.experimental.pallas import tpu as pltpu

F32 = jnp.float32
BF16 = jnp.bfloat16

D_MODEL = 1024
DEPTH = 2
GRID_W = 64
EPS = 1e-6
ROPE_BASE = 10000.0
N_MOD = 6
RET_HEADS = 4
RET_DK = 64
RET_DV = 128
RET_QK = RET_HEADS * RET_DK
RET_WIDTH = RET_HEADS * RET_DV
RET_CHUNK = 128
S5_WIDTH = D_MODEL - RET_WIDTH
S5_GROUP = 16
S5_GROUPS = S5_WIDTH // S5_GROUP
S5_STATE = 64
S5_CHUNK = 16
S5_ROW = S5_CHUNK * S5_GROUP
LANE = 128
S5_LB = S5_WIDTH // LANE
S5_GPB = LANE // S5_GROUP
S5_BIG = S5_CHUNK * LANE
S5_HALF = S5_GPB * 2 * S5_STATE
S5_POW = 32
AB_IN = 2 * RET_QK + 2 * RET_WIDTH + S5_WIDTH
HG_HEADS = 8
HG_DK = D_MODEL // HG_HEADS
HG_BLOCK = 64
HG_GROUP = 4
HG_SPLIT = 2
D_FF = 4 * D_MODEL
MOD_ROWS = 16

VMEM_LIMIT_BYTES = 56 * 1024 * 1024


def _cp(*sem):
    return pltpu.CompilerParams(dimension_semantics=sem, vmem_limit_bytes=VMEM_LIMIT_BYTES)


def _dot(a, b):
    return jnp.dot(a, b, preferred_element_type=F32)


def _dot_nt(a, b):
    return lax.dot_general(a, b, (((1,), (1,)), ((), ())), preferred_element_type=F32)


def _dot_tn(a, b):
    return lax.dot_general(a, b, (((0,), (0,)), ((), ())), preferred_element_type=F32)


def _sigmoid(x):
    return 0.5 * jnp.tanh(0.5 * x) + 0.5


def _rms(x):
    return x * lax.rsqrt(jnp.mean(x * x, axis=-1, keepdims=True) + EPS)


def _mod_chunk(mod_ref, row, i):
    return mod_ref[0, pl.ds(row, 1), i * D_MODEL:(i + 1) * D_MODEL]


def _adaln_kernel(cc_ref, w_ref, b_ref, o_ref):
    s = jax.nn.silu(cc_ref[...]).astype(BF16)
    o_ref[0] = _dot(s, w_ref[0].astype(BF16)) + b_ref[0]


def _adaln(cc, w_mod, b_mod):
    bn = 1536
    n = N_MOD * D_MODEL
    return pl.pallas_call(
        _adaln_kernel,
        out_shape=jax.ShapeDtypeStruct((DEPTH, MOD_ROWS, n), F32),
        grid=(DEPTH, n // bn),
        in_specs=[
            pl.BlockSpec((MOD_ROWS, D_MODEL), lambda l, j: (0, 0)),
            pl.BlockSpec((1, D_MODEL, bn), lambda l, j: (l, 0, j)),
            pl.BlockSpec((1, 1, bn), lambda l, j: (l, 0, j)),
        ],
        out_specs=pl.BlockSpec((1, MOD_ROWS, bn), lambda l, j: (l, 0, j)),
        compiler_params=_cp("parallel", "parallel"),
        name="adaln",
    )(cc, w_mod, b_mod.reshape(DEPTH, 1, n))


def _rope(t, cos, sin):
    lane = lax.broadcasted_iota(jnp.int32, t.shape, 1)
    first = (lane & (RET_DK // 2)) == 0
    w = t.shape[1]
    swapped = jnp.where(first, pltpu.roll(t, w - RET_DK // 2, 1), pltpu.roll(t, RET_DK // 2, 1))
    return t * cos + swapped * sin


def _inproj0_kernel(*refs, mod_row, rope):
    if rope:
        h_ref, mod_ref, ng_ref, w_ref, cos_ref, sin_ref, q_ref, k_ref, v_ref, u_ref, uz_ref, g_ref, u_s = refs
    else:
        h_ref, mod_ref, ng_ref, w_ref, q_ref, k_ref, v_ref, u_ref, uz_ref, g_ref, u_s = refs
    row = pl.program_id(0) if mod_row is None else mod_row
    xn = _rms(h_ref[0]) * ng_ref[...]
    xm = (xn * (1.0 + _mod_chunk(mod_ref, row, 1)) + _mod_chunk(mod_ref, row, 0)).astype(BF16)
    y = _dot(xm, w_ref[...].astype(BF16))
    q = y[:, 0:RET_QK]
    k = y[:, RET_QK:2 * RET_QK]
    if rope:
        q = _rope(q, cos_ref[...], sin_ref[...])
        k = _rope(k, cos_ref[...], sin_ref[...])
    q_ref[0] = q.astype(BF16)
    k_ref[0] = (k * (RET_DK ** -0.5)).astype(BF16)
    c0 = 2 * RET_QK
    v_ref[0] = y[:, c0:c0 + RET_WIDTH].astype(BF16)
    u0 = c0 + RET_WIDTH
    u_ref[0] = y[:, u0:u0 + S5_WIDTH]
    g = y[:, u0 + S5_WIDTH:]
    g_ref[0] = (g * _sigmoid(g)).astype(BF16)
    nch = u_s.shape[1] // S5_CHUNK
    for j in range(S5_LB):
        u_s[j] = y[:, u0 + j * LANE:u0 + (j + 1) * LANE]
        for s in range(S5_CHUNK):
            uz_ref[0, j, :, s * LANE:(s + 1) * LANE] = u_s[j, pl.ds(s, nch, stride=S5_CHUNK), :].astype(BF16)


def _inproj0(h, mod, layer, ng, w, cos, sin, mod_row, tm):
    b, n, _ = h.shape
    rope = cos is not None
    row_spec = lambda width: pl.BlockSpec((1, tm, width), lambda i, j: (i, j, 0))
    in_specs = [
        row_spec(D_MODEL),
        pl.BlockSpec((1, MOD_ROWS, N_MOD * D_MODEL), lambda i, j: (layer, 0, 0)),
        pl.BlockSpec((1, D_MODEL), lambda i, j: (0, 0)),
        pl.BlockSpec((D_MODEL, AB_IN), lambda i, j: (0, 0), pipeline_mode=pl.Buffered(1)),
    ]
    args = [h, mod, ng, w]
    if rope:
        in_specs += [pl.BlockSpec((tm, RET_QK), lambda i, j: (j, 0))] * 2
        args += [cos, sin]
    widths = (RET_QK, RET_QK, RET_WIDTH, S5_WIDTH, RET_WIDTH)
    dtypes = (BF16, BF16, BF16, F32, BF16)
    out_shape = [jax.ShapeDtypeStruct((b, n, wd), dt) for wd, dt in zip(widths, dtypes)]
    out_specs = [row_spec(wd) for wd in widths]
    out_shape.insert(4, jax.ShapeDtypeStruct((b, S5_LB, n // S5_CHUNK, S5_BIG), BF16))
    out_specs.insert(4, pl.BlockSpec((1, S5_LB, tm // S5_CHUNK, S5_BIG), lambda i, j: (i, 0, j, 0)))
    return pl.pallas_call(
        functools.partial(_inproj0_kernel, mod_row=mod_row, rope=rope),
        out_shape=out_shape,
        grid=(b, n // tm),
        in_specs=in_specs,
        out_specs=out_specs,
        scratch_shapes=[pltpu.VMEM((S5_LB, tm, LANE), F32)],
        compiler_params=_cp("parallel", "parallel"),
        name="inproj0_lat" if rope else "inproj0_ctx",
    )(*args)


def _ret_kernel(lg_ref, ql, kl, vl, gl, qc, kc, vc, gc, rl, rc, st_s, *, ncl, ncc, unroll):
    c = RET_CHUNK
    dk2 = 2 * RET_DK
    nt = ncc + ncl
    p = pl.program_id(1)
    h_a = 2 * p
    lgf_a = lg_ref[pl.ds(h_a, 1), :]
    lgf_b = lg_ref[pl.ds(h_a + 1, 1), :]
    lgb_a = lg_ref[pl.ds(RET_HEADS + h_a, 1), :]
    lgb_b = lg_ref[pl.ds(RET_HEADS + h_a + 1, 1), :]
    lane = lax.broadcasted_iota(jnp.int32, (1, 2 * RET_DK), 1)
    is_a = lane < RET_DK
    lgf_lane = jnp.where(is_a, lgf_a, lgf_b)
    lgb_lane = jnp.where(is_a, lgb_a, lgb_b)
    ri = lax.broadcasted_iota(jnp.int32, (c, c), 0).astype(F32)
    ci = lax.broadcasted_iota(jnp.int32, (c, c), 1).astype(F32)
    diff = ri - ci

    def dmat(lgf, lgb):
        fwd = jnp.exp(jnp.maximum(diff, 0.0) * lgf)
        bwd = jnp.exp(jnp.maximum(-diff, 0.0) * lgb)
        return jnp.where(diff > 0, fwd, jnp.where(diff < 0, bwd, 2.0))

    d_a = dmat(lgf_a, lgb_a)
    d_b = dmat(lgf_b, lgb_b)
    rowp = lax.broadcasted_iota(jnp.int32, (c, dk2), 0).astype(F32)
    qd = jnp.concatenate([jnp.exp((rowp + 1.0) * lgf_lane), jnp.exp((c - rowp) * lgb_lane)], axis=1)
    kd = jnp.concatenate([jnp.exp((c - 1.0 - rowp) * lgf_lane), jnp.exp(rowp * lgb_lane)], axis=1)
    rowk = lax.broadcasted_iota(jnp.int32, (dk2, 2 * RET_DV), 0)
    cd_f = jnp.exp(c * jnp.where(rowk < RET_DK, lgf_a[:, :1], lgf_b[:, :1]))
    cd_b = jnp.exp(c * jnp.where(rowk < RET_DK, lgb_a[:, :1], lgb_b[:, :1]))
    mask2 = jnp.concatenate([is_a, is_a], axis=1)

    def increment(k, v, slot):
        kk = jnp.concatenate([k, k], axis=1).astype(F32) * kd
        st_s[slot] = _dot_tn(kk.astype(BF16), v)

    for n in range(ncc):
        increment(kc[0, n * c:(n + 1) * c, :], vc[0, n * c:(n + 1) * c, :], n)

    def inc_body(i, carry):
        for j in range(unroll):
            n = i * unroll + j
            sl = pl.ds(pl.multiple_of(n * c, c), c)
            increment(kl[0, sl, :], vl[0, sl, :], ncc + n)
        return carry

    lax.fori_loop(0, ncl // unroll, inc_body, 0)

    def rec_body(t, carry):
        sf, sb = carry
        u = jnp.where(t < ncc, ncc - 1 - t, nt - 1 - (t - ncc))
        inc_f = st_s[t, 0:dk2, :]
        inc_b = st_s[u, dk2:2 * dk2, :]
        st_s[t, 0:dk2, :] = sf
        st_s[u, dk2:2 * dk2, :] = sb
        return cd_f * sf + inc_f, cd_b * sb + inc_b

    zero = jnp.zeros((dk2, 2 * RET_DV), F32)
    lax.fori_loop(0, nt, rec_body, (zero, zero))

    def output(q, k, v, g, slot, out_ref, st):
        q2 = (jnp.concatenate([q, q], axis=1).astype(F32) * qd).astype(BF16)
        s_n = st_s[slot].astype(BF16)
        for keep_a, dm, cs in ((True, d_a, 0), (False, d_b, RET_DV)):
            m1 = is_a if keep_a else jnp.logical_not(is_a)
            m2 = mask2 if keep_a else jnp.logical_not(mask2)
            att = _dot_nt(jnp.where(m1, q, jnp.zeros_like(q)), k) * dm
            o = (_dot(att.astype(BF16), v[:, cs:cs + RET_DV])
                 + _dot(jnp.where(m2, q2, jnp.zeros_like(q2)), s_n[:, cs:cs + RET_DV]))
            o = _rms(o) * g[:, cs:cs + RET_DV].astype(F32)
            out_ref[0, pl.ds(st, c), cs:cs + RET_DV] = o.astype(BF16)

    for n in range(ncc):
        sl = slice(n * c, (n + 1) * c)
        output(qc[0, sl, :], kc[0, sl, :], vc[0, sl, :], gc[0, sl, :], n, rc, n * c)

    def out_body(i, carry):
        for j in range(unroll):
            n = i * unroll + j
            st = pl.multiple_of(n * c, c)
            sl = pl.ds(st, c)
            output(ql[0, sl, :], kl[0, sl, :], vl[0, sl, :], gl[0, sl, :], ncc + n, rl, st)
        return carry

    lax.fori_loop(0, ncl // unroll, out_body, 0)


def _retention(lg_rows, q_l, k_l, v_l, g_l, q_c, k_c, v_c, g_c):
    b, n, _ = q_l.shape
    nc = q_c.shape[1]
    ncl, ncc = n // RET_CHUNK, nc // RET_CHUNK
    pairs = RET_HEADS // 2
    unroll = 8 if ncl % 8 == 0 else 1

    def spec(rows, width):
        return pl.BlockSpec((1, rows, width), lambda i, p: (i, 0, p))

    return pl.pallas_call(
        functools.partial(_ret_kernel, ncl=ncl, ncc=ncc, unroll=unroll),
        out_shape=[jax.ShapeDtypeStruct((b, n, RET_WIDTH), BF16),
                   jax.ShapeDtypeStruct((b, nc, RET_WIDTH), BF16)],
        grid=(b, pairs),
        in_specs=[pl.BlockSpec((2 * RET_HEADS, 2 * RET_DK), lambda i, p: (0, 0)),
                  spec(n, 2 * RET_DK), spec(n, 2 * RET_DK), spec(n, 2 * RET_DV), spec(n, 2 * RET_DV),
                  spec(nc, 2 * RET_DK), spec(nc, 2 * RET_DK), spec(nc, 2 * RET_DV), spec(nc, 2 * RET_DV)],
        out_specs=[spec(n, 2 * RET_DV), spec(nc, 2 * RET_DV)],
        scratch_shapes=[pltpu.VMEM((ncl + ncc, 4 * RET_DK, 2 * RET_DV), F32)],
        compiler_params=_cp("parallel", "parallel"),
        name="retention",
    )(lg_rows, q_l, k_l, v_l, g_l, q_c, k_c, v_c, g_c)


def _dot_hi(a, b, contract=(1, 0)):
    dims = (((contract[0],), (contract[1],)), ((), ()))
    return lax.dot_general(a, b, dims, preferred_element_type=F32, precision=lax.Precision.HIGHEST)


def _dot_sel(a, b, contract, data):
    dims = (((contract[0],), (contract[1],)), ((), ()))
    x = (a, b)[data]
    hi = x.astype(BF16)
    lo = (x - hi.astype(F32)).astype(BF16)
    dd = lambda piece: lax.dot_general(*((piece, b) if data == 0 else (a, piece)), dims, preferred_element_type=F32)
    return dd(hi) + dd(lo)


def _s5_prep_kernel(ar_row, ai_row, ldt, btr, bti, ctr, cti, rm, rwin, rwout, abig):
    t, g, p, kp = S5_CHUNK, S5_GROUP, S5_STATE, S5_POW
    row = S5_ROW
    i0 = lambda shape: lax.broadcasted_iota(jnp.int32, shape, 0)
    i1 = lambda shape: lax.broadcasted_iota(jnp.int32, shape, 1)
    f32 = lambda m: jnp.where(m, 1.0, 0.0).astype(BF16)
    s_of_r = i0((row, kp)) // g
    k_of_l = i1((row, kp))
    sel_rows = (f32(k_of_l == t - 1 - s_of_r), f32(k_of_l == s_of_r))
    t_of_c = i1((kp, row)) // g
    k_of_s = i0((kp, row))
    sel_out = (f32(k_of_s == t_of_c + 1), f32(k_of_s == t - t_of_c))
    sel_lag = (f32(k_of_s == t_of_c), f32(k_of_s == t - 1 - t_of_c))
    tile_l = f32(i1((g, row)) % g == i0((g, row)))
    tile_r = f32(i0((row, g)) % g == i1((row, g)))
    lane = i1((g, row))
    k_col = i0((kp, 1)).astype(F32)
    k_row = i1((1, kp)).astype(F32)
    first = i0((8, 1)) == 0

    def outer(a, k):
        a8 = jnp.where(first, jnp.broadcast_to(a, (8, a.shape[1])), 0.0)
        return _dot_hi(a8, jnp.broadcast_to(k, (8, k.shape[1])), (0, 0))

    for q in range(S5_GPB):
        lags = []
        for d in range(2):
            dt = jnp.exp(ldt[d, q])
            are_r, aim_r = ar_row[d, q], ai_row[d, q]
            mag = jnp.exp(are_r * dt)
            ang = aim_r * dt
            nr, ni = mag * jnp.cos(ang) - 1.0, mag * jnp.sin(ang)
            den = jnp.square(are_r) + jnp.square(aim_r)
            fr = (nr * are_r + ni * aim_r) / den
            fi = (ni * are_r - nr * aim_r) / den
            pm = jnp.exp(k_col * (are_r * dt))
            pa = k_col * ang
            pk_re, pk_im = pm * jnp.cos(pa), pm * jnp.sin(pa)
            pmt = jnp.exp(outer(are_r * dt, k_row))
            pat = outer(ang, k_row)
            pt_re, pt_im = pmt * jnp.cos(pat), pmt * jnp.sin(pat)
            bt_re, bt_im = _dot_sel(tile_r, btr[d, q], (1, 1), 1), _dot_sel(tile_r, bti[d, q], (1, 1), 1)
            bb_re = fr * bt_re - fi * bt_im
            bb_im = fr * bt_im + fi * bt_re
            pr_re, pr_im = _dot_sel(sel_rows[d], pk_re, (1, 0), 1), _dot_sel(sel_rows[d], pk_im, (1, 0), 1)
            w_re = pr_re * bb_re - pr_im * bb_im
            w_im = pr_re * bb_im + pr_im * bb_re
            for s in range(t):
                rows = slice(s * LANE + q * g, s * LANE + (q + 1) * g)
                rwin[0, rows, d * p:(d + 1) * p] = w_re[s * g:(s + 1) * g].astype(BF16)
                rwin[0, rows, (2 + d) * p:(3 + d) * p] = w_im[s * g:(s + 1) * g].astype(BF16)
            ct_re, ct_im = _dot_sel(ctr[d, q], tile_l, (0, 0), 0), _dot_sel(cti[d, q], tile_l, (0, 0), 0)

            def c_pow(sel):
                pc_re, pc_im = _dot_sel(pt_re, sel, (1, 0), 0), _dot_sel(pt_im, sel, (1, 0), 0)
                return ct_re * pc_re - ct_im * pc_im, ct_re * pc_im + ct_im * pc_re

            o_re, o_im = c_pow(sel_out[d])
            r0 = d * S5_GPB * p + q * p
            rwout[0, r0:r0 + p, :] = o_re.astype(BF16)
            rwout[0, S5_HALF + r0:S5_HALF + r0 + p, :] = (-o_im).astype(BF16)
            l_re, l_im = c_pow(sel_lag[d])
            lags.append(_dot_hi(bb_re[0:g], l_re) - _dot_hi(bb_im[0:g], l_im))
            abig[0, 0:1, r0:r0 + p] = pk_re[t:t + 1, :]
            abig[0, 1:2, r0:r0 + p] = pk_im[t:t + 1, :]
        for s in range(t):
            fwd = jnp.where(lane >= g * s, pltpu.roll(lags[0], g * s, 1), 0.0)
            bwd = jnp.where(lane < g * (s + 1), pltpu.roll(lags[1], (row - g * (t - 1 - s)) % row, 1), 0.0)
            rm[0, s * LANE + q * g:s * LANE + (q + 1) * g, :] = (fwd + bwd).astype(BF16)


def _s5_prep(a_re, a_im, log_dt, b_re, b_im, c_re, c_im):
    gg, p, g = S5_GROUPS, S5_STATE, S5_GROUP
    f = lambda x: x.astype(F32)
    args = (f(a_re).reshape(2, gg, 1, p), f(a_im).reshape(2, gg, 1, p),
            f(log_dt).reshape(2, gg, 1, 1),
            f(b_re), f(b_im), f(c_re), f(c_im))
    spec = lambda r, c: pl.BlockSpec((2, S5_GPB, r, c), lambda j: (0, j, 0, 0))
    out = lambda r, c: pl.BlockSpec((1, r, c), lambda j: (j, 0, 0))
    return pl.pallas_call(
        _s5_prep_kernel,
        out_shape=[jax.ShapeDtypeStruct((S5_LB, S5_BIG, S5_ROW), BF16),
                   jax.ShapeDtypeStruct((S5_LB, S5_BIG, S5_ROW), BF16),
                   jax.ShapeDtypeStruct((S5_LB, 2 * S5_HALF, S5_ROW), BF16),
                   jax.ShapeDtypeStruct((S5_LB, 2, S5_HALF), F32)],
        grid=(S5_LB,),
        in_specs=[spec(1, p), spec(1, p), spec(1, 1),
                  spec(p, g), spec(p, g), spec(g, p), spec(g, p)],
        out_specs=[out(S5_BIG, S5_ROW), out(S5_BIG, S5_ROW), out(2 * S5_HALF, S5_ROW), out(2, S5_HALF)],
        compiler_params=_cp("parallel"),
        name="s5_prep",
    )(*args)


def _s5_expand(r_ref, col_unit, col_block, row_unit):
    n = S5_BIG
    a = lax.broadcasted_iota(jnp.int32, (S5_ROW, n), 0)
    c = lax.broadcasted_iota(jnp.int32, (S5_ROW, n), 1)
    e = jnp.where((a // col_unit == c // col_block) & (a % col_unit == c % col_unit), 1.0, 0.0).astype(BF16)
    x = _dot(r_ref[0], e)
    rq = (lax.broadcasted_iota(jnp.int32, (n, n), 0) // row_unit) % S5_GPB
    cq = (lax.broadcasted_iota(jnp.int32, (n, n), 1) // col_unit) % S5_GPB
    return jnp.where(rq == cq, x, 0.0).astype(BF16)


def _s5_kernel(ul, uc, rm_ref, rwin_ref, rwout_ref, a_ref, yl, yc, x_s, m_s, win_s, wout_s, *, ncl, ncc):
    hw = S5_HALF
    hh = hw // 2

    @pl.when(pl.program_id(1) == 0)
    def _():
        m_s[...] = _s5_expand(rm_ref, S5_GROUP, LANE, S5_GROUP)
        win_s[...] = _s5_expand(rwin_ref, S5_STATE, S5_GPB * S5_STATE, S5_GROUP)
        wout_s[...] = _s5_expand(rwout_ref, S5_GROUP, LANE, S5_STATE)

    z = jnp.concatenate([uc[0, 0], ul[0, 0]], axis=0)
    x_s[...] = _dot(z, win_s[...])
    a_re = a_ref[0, 0:1, :]
    a_im = a_ref[0, 1:2, :]

    def segment(base, n, carry):
        s_re, s_im = carry
        for i in range(n):
            rf = slice(base + i, base + i + 1)
            rb = slice(base + n - 1 - i, base + n - i)
            x_re = jnp.concatenate([x_s[rf, 0:hh], x_s[rb, hh:hw]], axis=-1)
            x_im = jnp.concatenate([x_s[rf, hw:hw + hh], x_s[rb, hw + hh:2 * hw]], axis=-1)
            x_s[rf, 0:hh] = s_re[:, 0:hh]
            x_s[rb, hh:hw] = s_re[:, hh:hw]
            x_s[rf, hw:hw + hh] = s_im[:, 0:hh]
            x_s[rb, hw + hh:2 * hw] = s_im[:, hh:hw]
            s_re, s_im = a_re * s_re - a_im * s_im + x_re, a_re * s_im + a_im * s_re + x_im
        return s_re, s_im

    zero = jnp.zeros((1, hw), F32)
    carry = segment(0, ncc, (zero, zero))
    segment(ncc, ncl, carry)
    y = _dot(z, m_s[...]) + _dot(x_s[...].astype(BF16), wout_s[...])
    yc[0, 0] = y[0:ncc]
    yl[0, 0] = y[ncc:ncc + ncl]


def _s5(prep, zl, zc):
    rm, rwin, rwout, a_big = prep
    b, _, ncl, _ = zl.shape
    ncc = zc.shape[2]
    rows = lambda r: pl.BlockSpec((1, 1, r, S5_BIG), lambda j, i: (i, j, 0, 0))
    wspec = lambda r, c: pl.BlockSpec((1, r, c), lambda j, i: (j, 0, 0))
    big = pltpu.VMEM((S5_BIG, S5_BIG), BF16)
    return pl.pallas_call(
        functools.partial(_s5_kernel, ncl=ncl, ncc=ncc),
        out_shape=[jax.ShapeDtypeStruct(zl.shape, F32), jax.ShapeDtypeStruct(zc.shape, F32)],
        grid=(S5_LB, b),
        in_specs=[rows(ncl), rows(ncc), wspec(S5_BIG, S5_ROW), wspec(S5_BIG, S5_ROW),
                  wspec(2 * S5_HALF, S5_ROW), wspec(2, S5_HALF)],
        out_specs=[rows(ncl), rows(ncc)],
        scratch_shapes=[pltpu.VMEM((ncc + ncl, 2 * S5_HALF), F32), big, big, big],
        compiler_params=_cp("parallel", "arbitrary"),
        name="s5",
    )(zl, zc, rm, rwin, rwout, a_big)


def _mix_mlp_kernel(*refs, mod_row, s5_merge, final_norm, fb):
    if s5_merge:
        (h_ref, r_ref, y5_ref, u_ref, ds_ref, wg_ref, bg_ref, wo_ref, mod_ref, nm_ref, w1_ref, w2_ref,
         *rest) = refs
    else:
        h_ref, r_ref, wo_ref, mod_ref, nm_ref, w1_ref, w2_ref, *rest = refs
    if final_norm:
        nf_ref, o_ref, *scratch = rest
    else:
        o_ref, *scratch = rest
    row = pl.program_id(0) if mod_row is None else mod_row
    if s5_merge:
        (y_s,) = scratch
        nch = y_s.shape[1] // S5_CHUNK
        for lb in range(S5_LB):
            for s in range(S5_CHUNK):
                y_s[lb, pl.ds(s, nch, stride=S5_CHUNK), :] = y5_ref[0, lb, :, s * LANE:(s + 1) * LANE]
        y5 = jnp.concatenate([y_s[lb] for lb in range(S5_LB)], axis=-1)
        y = jax.nn.gelu(y5 + ds_ref[...] * u_ref[0])
        y = y * jax.nn.sigmoid(_dot(y.astype(BF16), wg_ref[...]) + bg_ref[...])
        mix = _dot(r_ref[0], wo_ref[0:RET_WIDTH, :]) + _dot(y.astype(BF16), wo_ref[RET_WIDTH:D_MODEL, :])
    else:
        mix = _dot(r_ref[0], wo_ref[...])
    h1 = h_ref[0] + _mod_chunk(mod_ref, row, 2) * mix
    xn = _rms(h1) * nm_ref[...]
    xm = (xn * (1.0 + _mod_chunk(mod_ref, row, 4)) + _mod_chunk(mod_ref, row, 3)).astype(BF16)
    acc = None
    for j in range(D_FF // fb):
        a = jnp.square(jnp.maximum(_dot(xm, w1_ref[:, j * fb:(j + 1) * fb].astype(BF16)), 0.0)).astype(BF16)
        part = _dot(a, w2_ref[j * fb:(j + 1) * fb, :].astype(BF16))
        acc = part if acc is None else acc + part
    h2 = h1 + _mod_chunk(mod_ref, row, 5) * acc
    if final_norm:
        h2 = _rms(h2) * nf_ref[...]
    o_ref[0] = h2


def _mix_mlp(h, r, s5y, u, s5p, wo, mod, layer, nm, w1, w2, nf, mod_row, tm, fb, name):
    b, n, _ = h.shape
    s5_merge = s5y is not None
    final_norm = nf is not None
    one = pl.Buffered(1)
    row_spec = lambda width: pl.BlockSpec((1, tm, width), lambda i, t: (i, t, 0))
    const = lambda shape: pl.BlockSpec(shape, lambda i, t: (0,) * len(shape), pipeline_mode=one)
    in_specs = [row_spec(D_MODEL), row_spec(r.shape[-1])]
    args = [h, r]
    if s5_merge:
        d_skip, w_glu, b_glu = s5p
        z_spec = pl.BlockSpec((1, S5_LB, tm // S5_CHUNK, S5_BIG), lambda i, t: (i, 0, t, 0))
        in_specs += [z_spec, row_spec(S5_WIDTH), const((1, S5_WIDTH)),
                     const((S5_WIDTH, S5_WIDTH)), const((1, S5_WIDTH))]
        args += [s5y, u, d_skip, w_glu, b_glu]
    in_specs += [const((D_MODEL, D_MODEL)),
                 pl.BlockSpec((1, MOD_ROWS, N_MOD * D_MODEL), lambda i, t: (layer, 0, 0), pipeline_mode=one),
                 const((1, D_MODEL))]
    if w1.ndim == 3:
        in_specs += [pl.BlockSpec((None, D_MODEL, D_FF), lambda i, t: (layer, 0, 0), pipeline_mode=one),
                     pl.BlockSpec((None, D_FF, D_MODEL), lambda i, t: (layer, 0, 0), pipeline_mode=one)]
    else:
        in_specs += [const((D_MODEL, D_FF)), const((D_FF, D_MODEL))]
    args += [wo, mod, nm, w1, w2]
    if final_norm:
        in_specs.append(const((1, D_MODEL)))
        args.append(nf)
    return pl.pallas_call(
        functools.partial(_mix_mlp_kernel, mod_row=mod_row, s5_merge=s5_merge, final_norm=final_norm, fb=fb),
        out_shape=jax.ShapeDtypeStruct((b, n, D_MODEL), F32),
        grid=(b, n // tm),
        in_specs=in_specs,
        out_specs=row_spec(D_MODEL),
        scratch_shapes=[pltpu.VMEM((S5_LB, tm, LANE), F32)] if s5_merge else [],
        compiler_params=_cp("parallel", "parallel"),
        name=name,
    )(*args)


def _hgrn_lower_bounds(lbl_ref, layer):
    out = []
    for d in range(2):
        z = [lbl_ref[d, k:k + 1, :] for k in range(DEPTH)]
        zmax = functools.reduce(jnp.maximum, z)
        e = [jnp.exp(v - zmax) for v in z]
        tot = functools.reduce(lambda a, b_: a + b_, e)
        lb = jnp.zeros_like(tot)
        for k in range(1, layer + 1):
            lb = lb + e[k] / tot
        out.append(lb)
    return out


def _inproj1_kernel(h_ref, mod_ref, ng_ref, lbl_ref, w_ref, *out_refs, mod_row, layer, latent):
    row = pl.program_id(0) if mod_row is None else mod_row
    xn = _rms(h_ref[0]) * ng_ref[...]
    xm = (xn * (1.0 + _mod_chunk(mod_ref, row, 1)) + _mod_chunk(mod_ref, row, 0)).astype(BF16)
    col = lambda k: _dot(xm, w_ref[:, k * D_MODEL:(k + 1) * D_MODEL].astype(BF16))
    lbs = _hgrn_lower_bounds(lbl_ref, layer)
    refs = list(out_refs)
    if latent:
        refs.pop(0)[0] = col(0).astype(BF16)
    for d in range(2):
        t = (1.0 - lbs[d]) * _sigmoid(col(1 + d))
        refs.pop(0)[0] = jnp.log(lbs[d] + t)
        refs.pop(0)[0] = ((1.0 - lbs[d]) - t).astype(BF16)
    refs.pop(0)[0] = col(3).astype(BF16)
    if latent:
        g = col(4)
        refs.pop(0)[0] = (g * _sigmoid(g)).astype(BF16)


def _inproj1(h, mod, layer, ng, lb_logits, w, latent, mod_row, tm, name):
    b, n, _ = h.shape
    one = pl.Buffered(1)
    row_spec = pl.BlockSpec((1, tm, D_MODEL), lambda i, t: (i, t, 0))
    dtypes = ([BF16] if latent else []) + [F32, BF16, F32, BF16, BF16] + ([BF16] if latent else [])
    return pl.pallas_call(
        functools.partial(_inproj1_kernel, mod_row=mod_row, layer=layer, latent=latent),
        out_shape=[jax.ShapeDtypeStruct((b, n, D_MODEL), dt) for dt in dtypes],
        grid=(b, n // tm),
        in_specs=[row_spec,
                  pl.BlockSpec((1, MOD_ROWS, N_MOD * D_MODEL), lambda i, t: (layer, 0, 0), pipeline_mode=one),
                  pl.BlockSpec((1, D_MODEL), lambda i, t: (0, 0), pipeline_mode=one),
                  pl.BlockSpec(lb_logits.shape, lambda i, t: (0, 0, 0), pipeline_mode=one),
                  pl.BlockSpec(w.shape, lambda i, t: (0, 0), pipeline_mode=one)],
        out_specs=[row_spec] * len(dtypes),
        compiler_params=_cp("parallel", "parallel"),
        name=name,
    )(h, mod, ng, lb_logits, w)


def _cumsum_mm(tri, x):
    acc = None
    r = x
    for i in range(HG_SPLIT):
        p = r.astype(BF16)
        acc = _dot(tri, p) if acc is None else acc + _dot(tri, p)
        if i + 1 < HG_SPLIT:
            r = r - p.astype(F32)
    return acc


def _hgrn_kernel(ng_ref, ql, lffl, kfl, lfbl, kbl, il, sgl, lffc, kfc, lfbc, kbc, ic, o_ref,
                 qin_s, att_s, kv_s, et_s, kvc_s, etc_s, cum_s, ko_s, qt_s, kt_s, *, nbl, nbc, unroll, out_blocks):
    cb = HG_BLOCK
    mid = cb // 2
    gb = HG_GROUP
    gr = gb * cb
    dk = HG_DK

    ri = lax.broadcasted_iota(jnp.int32, (gr, gr), 0)
    ci = lax.broadcasted_iota(jnp.int32, (gr, gr), 1)
    same = (ri // cb) == (ci // cb)
    rb = lax.broadcasted_iota(jnp.int32, (cb, cb), 0)
    cbi = lax.broadcasted_iota(jnp.int32, (cb, cb), 1)
    tri_l = jnp.where(same & (ri >= ci), 1.0, 0.0).astype(BF16)
    dirs = ((0, rb >= cbi, mid - 1, cb - 1), (1, rb <= cbi, mid, 0))

    def cumsums(lfs, slot):
        pre = _cumsum_mm(tri_l, jnp.concatenate(lfs, axis=-1))
        pre_b = pre[:, dk:].reshape(gb, cb, dk)
        cum_s[slot, 0] = pre[:, :dk]
        cum_s[slot, 1] = (pre_b[:, cb - 1:cb, :] - pre_b).reshape(gr, dk) + lfs[1]

    def operands(slot, kks, q, n0, et_ref):
        for d, keep, ref_row, tot_row in dirs:
            cum = cum_s[slot, d].reshape(gb, cb, dk)
            kk = kks[d].astype(F32).reshape(gb, cb, dk)
            ref = cum[:, ref_row:ref_row + 1, :]
            tot = cum[:, tot_row:tot_row + 1, :]
            e = cum - ref
            kt = kk * jnp.exp(-e)
            ko_s[slot, d] = (kt * jnp.exp(tot - ref)).astype(BF16).reshape(gr, dk)
            e_tot = jnp.exp(tot)
            for j in range(gb):
                et_ref[n0 + j, :, d * dk:(d + 1) * dk] = e_tot[j]
            if q is not None:
                qt = q.astype(F32).reshape(gb, cb, dk) * jnp.exp(e)
                qin_s[pl.ds(pl.multiple_of(n0 * cb, gr), gr), d * dk:(d + 1) * dk] = (
                    (qt * jnp.exp(ref)).astype(BF16).reshape(gr, dk))
                qt_s[slot, d] = qt.astype(BF16).reshape(gr, dk)
                kt_s[slot, d] = kt.astype(BF16).reshape(gr, dk)

    def matmuls(slot, v, with_q, n0, kv_ref):
        for d in range(2):
            for j in range(gb):
                rows = slice(j * cb, (j + 1) * cb)
                kv_ref[n0 + j, :, d * dk:(d + 1) * dk] = _dot_tn(v[rows], ko_s[slot, d, rows, :])
        if with_q:
            for j in range(gb):
                rows = slice(j * cb, (j + 1) * cb)
                att = (jnp.where(dirs[0][1], _dot_nt(qt_s[slot, 0, rows, :], kt_s[slot, 0, rows, :]), 0.0)
                       + jnp.where(dirs[1][1], _dot_nt(qt_s[slot, 1, rows, :], kt_s[slot, 1, rows, :]), 0.0))
                att_s[n0 + j] = att.astype(BF16)

    for g in range(nbc // gb):
        sl = slice(g * gr, (g + 1) * gr)
        cumsums((lffc[0, sl, :], lfbc[0, sl, :]), 0)
        operands(0, (kfc[0, sl, :], kbc[0, sl, :]), None, g * gb, etc_s)
        matmuls(0, ic[0, sl, :], False, g * gb, kvc_s)

    ngl = nbl // gb
    rows_of = lambda g: pl.ds(pl.multiple_of(g * gr, gr), gr)

    def lat_cumsums(g, slot):
        cumsums((lffl[0, rows_of(g), :], lfbl[0, rows_of(g), :]), slot)

    def lat_operands(g, slot):
        operands(slot, (kfl[0, rows_of(g), :], kbl[0, rows_of(g), :]), ql[0, rows_of(g), :], g * gb, et_s)

    def lat_matmuls(g, slot):
        matmuls(slot, il[0, rows_of(g), :], True, g * gb, kv_s)

    lat_cumsums(0, 0)
    lat_operands(0, 0)
    lat_cumsums(1, 1)

    def prep_body(i, carry):
        g = 2 * i
        lat_matmuls(g, 0)
        lat_operands(g + 1, 1)
        lat_cumsums(g + 2, 0)
        lat_matmuls(g + 1, 1)
        lat_operands(g + 2, 0)
        lat_cumsums(g + 3, 1)
        return carry

    lax.fori_loop(0, ngl // 2 - 1, prep_body, 0)
    lat_matmuls(ngl - 2, 0)
    lat_operands(ngl - 1, 1)
    lat_matmuls(ngl - 1, 1)

    lane = lax.broadcasted_iota(jnp.int32, (dk, 2 * dk), 1)
    is_f = lane < dk
    st = jnp.zeros((dk, 2 * dk), F32)
    for n in range(nbc):
        m = nbc - 1 - n
        st = (st * jnp.where(is_f[:1], etc_s[n], etc_s[m]) + jnp.where(is_f, kvc_s[n], kvc_s[m]))

    def rec_body(t, st):
        u = nbl - 1 - t
        inc = jnp.where(is_f, kv_s[t], kv_s[u])
        dec = jnp.where(is_f[:1], et_s[t], et_s[u])
        kv_s[t, :, 0:dk] = st[:, 0:dk]
        kv_s[u, :, dk:2 * dk] = st[:, dk:2 * dk]
        return st * dec + inc

    lax.fori_loop(0, nbl, rec_body, st, unroll=unroll)

    def out_body(i, carry):
        for j in range(out_blocks):
            n = i * out_blocks + j
            sl = pl.ds(pl.multiple_of(n * cb, cb), cb)
            o = _dot(att_s[n], il[0, sl, :]) + _dot_nt(qin_s[sl, :], kv_s[n].astype(BF16))
            o = _rms(o) * ng_ref[...] * sgl[0, sl, :].astype(F32)
            o_ref[0, sl, :] = o.astype(BF16)
        return carry

    lax.fori_loop(0, nbl // out_blocks, out_body, 0)


def _hgrn(norm_g, q_l, lff_l, kf_l, lfb_l, kb_l, i_l, sg_l, lff_c, kf_c, lfb_c, kb_c, i_c):
    b, n, _ = q_l.shape
    nc = lff_c.shape[1]
    nbl, nbc = n // HG_BLOCK, nc // HG_BLOCK
    out_blocks = min(16, nbl)
    assert nbl % (2 * HG_GROUP) == 0 and nbc % HG_GROUP == 0 and nbl % out_blocks == 0
    spec = lambda rows: pl.BlockSpec((1, rows, HG_DK), lambda i, h: (i, 0, h))
    slot = lambda dt: pltpu.VMEM((2, 2, HG_GROUP * HG_BLOCK, HG_DK), dt)
    return pl.pallas_call(
        functools.partial(_hgrn_kernel, nbl=nbl, nbc=nbc, unroll=2, out_blocks=out_blocks),
        out_shape=jax.ShapeDtypeStruct((b, n, D_MODEL), BF16),
        grid=(b, HG_HEADS),
        in_specs=[pl.BlockSpec((1, HG_DK), lambda i, h: (0, 0))] + [spec(n)] * 7 + [spec(nc)] * 5,
        out_specs=spec(n),
        scratch_shapes=[pltpu.VMEM((n, 2 * HG_DK), BF16),
                        pltpu.VMEM((nbl, HG_BLOCK, HG_BLOCK), BF16),
                        pltpu.VMEM((nbl, HG_DK, 2 * HG_DK), F32),
                        pltpu.VMEM((nbl, 1, 2 * HG_DK), F32),
                        pltpu.VMEM((nbc, HG_DK, 2 * HG_DK), F32),
                        pltpu.VMEM((nbc, 1, 2 * HG_DK), F32),
                        slot(F32), slot(BF16), slot(BF16), slot(BF16)],
        compiler_params=_cp("parallel", "parallel"),
        name="hgrn2",
    )(norm_g, q_l, lff_l, kf_l, lfb_l, kb_l, i_l, sg_l, lff_c, kf_c, lfb_c, kb_c, i_c)


def _rope_tables(n_tok):
    tok = jnp.arange(n_tok, dtype=jnp.int32)[:, None]
    row = (tok // GRID_W).astype(F32)
    col = (tok % GRID_W).astype(F32)
    n_freq = RET_DK // 4
    lane = jnp.arange(RET_QK, dtype=jnp.int32)[None, :]
    j = lane % (2 * n_freq)
    inv = ROPE_BASE ** (-(j % n_freq).astype(F32) / n_freq)
    ang = jnp.where(j < n_freq, row, col) * inv
    sign = jnp.where(lane % RET_DK < RET_DK // 2, -1.0, 1.0)
    return jnp.cos(ang), jnp.sin(ang) * sign


def kernel(x, c, ctx, c_ctx, w_mod, b_mod, norm_mix, norm_mlp, w_mlp_in, w_mlp_out, ab_w_in, ab_w_out, ret_logit, s5_a_re, s5_a_im, s5_log_dt, s5_b_re, s5_b_im, s5_c_re, s5_c_im, s5_d, s5_w_glu, s5_b_glu, hg_w_in, hg_w_out, hg_lb_logits, hg_norm, norm_final):
    b, n, d = x.shape
    nc = ctx.shape[1]
    assert d == D_MODEL and b + 1 <= MOD_ROWS and w_mod.shape[0] == DEPTH == 2
    assert n % 512 == 0 and nc % 256 == 0 and n % GRID_W == 0
    ctx_row = b
    tm_l, tm_c = 512, 256

    cc = jnp.zeros((MOD_ROWS, d), F32).at[:b].set(c).at[b].set(c_ctx)
    mod = _adaln(cc, w_mod, b_mod)

    row2 = lambda a: a.reshape(1, -1)
    w_in0 = ab_w_in[0]
    cos, sin = _rope_tables(n)
    ng0 = row2(norm_mix[0])
    q_l, k_l, v_l, u_l, uz_l, g_l = _inproj0(x, mod, 0, ng0, w_in0, cos, sin, None, tm_l)
    q_c, k_c, v_c, u_c, uz_c, g_c = _inproj0(ctx, mod, 0, ng0, w_in0, None, None, ctx_row, tm_c)

    log_gamma = jax.nn.log_sigmoid(ret_logit[0].astype(F32))
    lg_rows = jnp.broadcast_to(log_gamma.reshape(2 * RET_HEADS, 1), (2 * RET_HEADS, 2 * RET_DK))
    r_l, r_c = _retention(lg_rows, q_l, k_l, v_l, g_l, q_c, k_c, v_c, g_c)

    s5_ops = _s5_prep(s5_a_re[0], s5_a_im[0], s5_log_dt[0], s5_b_re[0], s5_b_im[0], s5_c_re[0], s5_c_im[0])
    y5_l, y5_c = _s5(s5_ops, uz_l, uz_c)

    s5p = (row2(s5_d[0]), s5_w_glu[0].astype(BF16), row2(s5_b_glu[0]))
    wo0 = ab_w_out[0].astype(BF16)
    w1_0, w2_0 = w_mlp_in[0].astype(BF16), w_mlp_out[0].astype(BF16)
    nm0 = row2(norm_mlp[0])
    h_l = _mix_mlp(x, r_l, y5_l, u_l, s5p, wo0, mod, 0, nm0, w1_0, w2_0, None, None, tm_l, 1024, "mix_mlp0_lat")
    h_c = _mix_mlp(ctx, r_c, y5_c, u_c, s5p, wo0, mod, 0, nm0, w1_0, w2_0, None, ctx_row, tm_c, 1024, "mix_mlp0_ctx")

    w_in1 = hg_w_in[0]
    ng1 = row2(norm_mix[1])
    lat1 = _inproj1(h_l, mod, 1, ng1, hg_lb_logits, w_in1, True, None, tm_l, "inproj1_lat")
    ctx1 = _inproj1(h_c, mod, 1, ng1, hg_lb_logits, w_in1, False, ctx_row, tm_c, "inproj1_ctx")
    o1 = _hgrn(row2(hg_norm[0]), *lat1, *ctx1)
    return _mix_mlp(h_l, o1, None, None, None, hg_w_out[0].astype(BF16), mod, 1, row2(norm_mlp[1]),
                    w_mlp_in, w_mlp_out, row2(norm_final), None, tm_l, 1024,
                    "mix_mlp1_lat")
```

```python
import functools

import jax
import jax.numpy as jnp
from jax import lax
from jax.experimental import pallas as pl
from jax.experimental.pallas import tpu as pltpu

F32 = jnp.float32
BF16 = jnp.bfloat16

D_MODEL = 1024
DEPTH = 2
GRID_W = 64
EPS = 1e-6
ROPE_BASE = 10000.0
N_MOD = 6
RET_HEADS = 4
RET_DK = 64
RET_DV = 128
RET_QK = RET_HEADS * RET_DK
RET_WIDTH = RET_HEADS * RET_DV
RET_CHUNK = 128
S5_WIDTH = D_MODEL - RET_WIDTH
S5_GROUP = 16
S5_GROUPS = S5_WIDTH // S5_GROUP
S5_STATE = 64
S5_CHUNK = 16
S5_ROW = S5_CHUNK * S5_GROUP
LANE = 128
S5_LB = S5_WIDTH // LANE
S5_GPB = LANE // S5_GROUP
S5_BIG = S5_CHUNK * LANE
S5_HALF = S5_GPB * 2 * S5_STATE
S5_POW = 32
AB_IN = 2 * RET_QK + 2 * RET_WIDTH + S5_WIDTH
HG_HEADS = 8
HG_DK = D_MODEL // HG_HEADS
HG_BLOCK = 64
HG_GROUP = 4
HG_SPLIT = 2
D_FF = 4 * D_MODEL
MOD_ROWS = 16

VMEM_LIMIT_BYTES = 56 * 1024 * 1024


def _cp(*sem):
    return pltpu.CompilerParams(dimension_semantics=sem, vmem_limit_bytes=VMEM_LIMIT_BYTES)


def _dot(a, b):
    return jnp.dot(a, b, preferred_element_type=F32)


def _dot_nt(a, b):
    return lax.dot_general(a, b, (((1,), (1,)), ((), ())), preferred_element_type=F32)


def _dot_tn(a, b):
    return lax.dot_general(a, b, (((0,), (0,)), ((), ())), preferred_element_type=F32)


def _sigmoid(x):
    return 0.5 * jnp.tanh(0.5 * x) + 0.5


def _rms(x):
    return x * lax.rsqrt(jnp.mean(x * x, axis=-1, keepdims=True) + EPS)


def _mod_chunk(mod_ref, row, i):
    return mod_ref[0, pl.ds(row, 1), i * D_MODEL:(i + 1) * D_MODEL]


def _adaln_kernel(cc_ref, w_ref, b_ref, o_ref):
    s = jax.nn.silu(cc_ref[...]).astype(BF16)
    o_ref[0] = _dot(s, w_ref[0].astype(BF16)) + b_ref[0]


def _adaln(cc, w_mod, b_mod):
    bn = 1536
    n = N_MOD * D_MODEL
    return pl.pallas_call(
        _adaln_kernel,
        out_shape=jax.ShapeDtypeStruct((DEPTH, MOD_ROWS, n), F32),
        grid=(DEPTH, n // bn),
        in_specs=[
            pl.BlockSpec((MOD_ROWS, D_MODEL), lambda l, j: (0, 0)),
            pl.BlockSpec((1, D_MODEL, bn), lambda l, j: (l, 0, j)),
            pl.BlockSpec((1, 1, bn), lambda l, j: (l, 0, j)),
        ],
        out_specs=pl.BlockSpec((1, MOD_ROWS, bn), lambda l, j: (l, 0, j)),
        compiler_params=_cp("parallel", "parallel"),
        name="adaln",
    )(cc, w_mod, b_mod.reshape(DEPTH, 1, n))


def _rope(t, cos, sin):
    lane = lax.broadcasted_iota(jnp.int32, t.shape, 1)
    first = (lane & (RET_DK // 2)) == 0
    w = t.shape[1]
    swapped = jnp.where(first, pltpu.roll(t, w - RET_DK // 2, 1), pltpu.roll(t, RET_DK // 2, 1))
    return t * cos + swapped * sin


def _inproj0_kernel(*refs, mod_row, rope):
    if rope:
        h_ref, mod_ref, ng_ref, w_ref, cos_ref, sin_ref, q_ref, k_ref, v_ref, u_ref, uz_ref, g_ref, u_s = refs
    else:
        h_ref, mod_ref, ng_ref, w_ref, q_ref, k_ref, v_ref, u_ref, uz_ref, g_ref, u_s = refs
    row = pl.program_id(0) if mod_row is None else mod_row
    xn = _rms(h_ref[0]) * ng_ref[...]
    xm = (xn * (1.0 + _mod_chunk(mod_ref, row, 1)) + _mod_chunk(mod_ref, row, 0)).astype(BF16)
    y = _dot(xm, w_ref[...].astype(BF16))
    q = y[:, 0:RET_QK]
    k = y[:, RET_QK:2 * RET_QK]
    if rope:
        cos = jnp.concatenate([cos_ref[...]] * (RET_QK // LANE), axis=1)
        sin = jnp.concatenate([sin_ref[...]] * (RET_QK // LANE), axis=1)
        q = _rope(q, cos, sin)
        k = _rope(k, cos, sin)
    q_ref[0] = q.astype(BF16)
    k_ref[0] = (k * (RET_DK ** -0.5)).astype(BF16)
    c0 = 2 * RET_QK
    v_ref[0] = y[:, c0:c0 + RET_WIDTH].astype(BF16)
    u0 = c0 + RET_WIDTH
    u_ref[0] = y[:, u0:u0 + S5_WIDTH]
    g = y[:, u0 + S5_WIDTH:]
    g_ref[0] = (g * _sigmoid(g)).astype(BF16)
    nch = u_s.shape[1] // S5_CHUNK
    for j in range(S5_LB):
        u_s[j] = y[:, u0 + j * LANE:u0 + (j + 1) * LANE]
        for s in range(S5_CHUNK):
            uz_ref[0, j, :, s * LANE:(s + 1) * LANE] = u_s[j, pl.ds(s, nch, stride=S5_CHUNK), :].astype(BF16)


def _inproj0(h, mod, layer, ng, w, cos, sin, mod_row, tm):
    b, n, _ = h.shape
    rope = cos is not None
    row_spec = lambda width: pl.BlockSpec((1, tm, width), lambda i, j: (i, j, 0))
    in_specs = [
        row_spec(D_MODEL),
        pl.BlockSpec((1, MOD_ROWS, N_MOD * D_MODEL), lambda i, j: (layer, 0, 0)),
        pl.BlockSpec((1, D_MODEL), lambda i, j: (0, 0)),
        pl.BlockSpec((D_MODEL, AB_IN), lambda i, j: (0, 0), pipeline_mode=pl.Buffered(1)),
    ]
    args = [h, mod, ng, w]
    if rope:
        in_specs += [pl.BlockSpec((tm, LANE), lambda i, j: (j, 0))] * 2
        args += [cos, sin]
    widths = (RET_QK, RET_QK, RET_WIDTH, S5_WIDTH, RET_WIDTH)
    dtypes = (BF16, BF16, BF16, F32, BF16)
    out_shape = [jax.ShapeDtypeStruct((b, n, wd), dt) for wd, dt in zip(widths, dtypes)]
    out_specs = [row_spec(wd) for wd in widths]
    out_shape.insert(4, jax.ShapeDtypeStruct((b, S5_LB, n // S5_CHUNK, S5_BIG), BF16))
    out_specs.insert(4, pl.BlockSpec((1, S5_LB, tm // S5_CHUNK, S5_BIG), lambda i, j: (i, 0, j, 0)))
    return pl.pallas_call(
        functools.partial(_inproj0_kernel, mod_row=mod_row, rope=rope),
        out_shape=out_shape,
        grid=(b, n // tm),
        in_specs=in_specs,
        out_specs=out_specs,
        scratch_shapes=[pltpu.VMEM((S5_LB, tm, LANE), F32)],
        compiler_params=_cp("parallel", "parallel"),
        name="inproj0_lat" if rope else "inproj0_ctx",
    )(*args)


def _ret_kernel(lg_ref, ql, kl, vl, gl, qc, kc, vc, gc, rl, rc, st_s, *, ncl, ncc, unroll):
    c = RET_CHUNK
    dk2 = 2 * RET_DK
    nt = ncc + ncl
    p = pl.program_id(1)
    h_a = 2 * p
    lgf_a = lg_ref[pl.ds(h_a, 1), :]
    lgf_b = lg_ref[pl.ds(h_a + 1, 1), :]
    lgb_a = lg_ref[pl.ds(RET_HEADS + h_a, 1), :]
    lgb_b = lg_ref[pl.ds(RET_HEADS + h_a + 1, 1), :]
    lane = lax.broadcasted_iota(jnp.int32, (1, 2 * RET_DK), 1)
    is_a = lane < RET_DK
    lgf_lane = jnp.where(is_a, lgf_a, lgf_b)
    lgb_lane = jnp.where(is_a, lgb_a, lgb_b)
    ri = lax.broadcasted_iota(jnp.int32, (c, c), 0).astype(F32)
    ci = lax.broadcasted_iota(jnp.int32, (c, c), 1).astype(F32)
    diff = ri - ci

    def dmat(lgf, lgb):
        fwd = jnp.exp(jnp.maximum(diff, 0.0) * lgf)
        bwd = jnp.exp(jnp.maximum(-diff, 0.0) * lgb)
        return jnp.where(diff > 0, fwd, jnp.where(diff < 0, bwd, 2.0))

    d_a = dmat(lgf_a, lgb_a)
    d_b = dmat(lgf_b, lgb_b)
    rowp = lax.broadcasted_iota(jnp.int32, (c, dk2), 0).astype(F32)
    qd = jnp.concatenate([jnp.exp((rowp + 1.0) * lgf_lane), jnp.exp((c - rowp) * lgb_lane)], axis=1)
    kd = jnp.concatenate([jnp.exp((c - 1.0 - rowp) * lgf_lane), jnp.exp(rowp * lgb_lane)], axis=1)
    rowk = lax.broadcasted_iota(jnp.int32, (dk2, 2 * RET_DV), 0)
    cd_f = jnp.exp(c * jnp.where(rowk < RET_DK, lgf_a[:, :1], lgf_b[:, :1]))
    cd_b = jnp.exp(c * jnp.where(rowk < RET_DK, lgb_a[:, :1], lgb_b[:, :1]))
    mask2 = jnp.concatenate([is_a, is_a], axis=1)

    def increment(k, v, slot):
        kk = jnp.concatenate([k, k], axis=1).astype(F32) * kd
        st_s[slot] = _dot_tn(kk.astype(BF16), v)

    for n in range(ncc):
        increment(kc[0, n * c:(n + 1) * c, :], vc[0, n * c:(n + 1) * c, :], n)

    def inc_body(i, carry):
        for j in range(unroll):
            n = i * unroll + j
            sl = pl.ds(pl.multiple_of(n * c, c), c)
            increment(kl[0, sl, :], vl[0, sl, :], ncc + n)
        return carry

    lax.fori_loop(0, ncl // unroll, inc_body, 0)

    def rec_body(t, carry):
        sf, sb = carry
        u = jnp.where(t < ncc, ncc - 1 - t, nt - 1 - (t - ncc))
        inc_f = st_s[t, 0:dk2, :]
        inc_b = st_s[u, dk2:2 * dk2, :]
        st_s[t, 0:dk2, :] = sf
        st_s[u, dk2:2 * dk2, :] = sb
        return cd_f * sf + inc_f, cd_b * sb + inc_b

    zero = jnp.zeros((dk2, 2 * RET_DV), F32)
    lax.fori_loop(0, nt, rec_body, (zero, zero))

    def output(q, k, v, g, slot, out_ref, st):
        q2 = (jnp.concatenate([q, q], axis=1).astype(F32) * qd).astype(BF16)
        s_n = st_s[slot].astype(BF16)
        for keep_a, dm, cs in ((True, d_a, 0), (False, d_b, RET_DV)):
            m1 = is_a if keep_a else jnp.logical_not(is_a)
            m2 = mask2 if keep_a else jnp.logical_not(mask2)
            att = _dot_nt(jnp.where(m1, q, jnp.zeros_like(q)), k) * dm
            o = (_dot(att.astype(BF16), v[:, cs:cs + RET_DV])
                 + _dot(jnp.where(m2, q2, jnp.zeros_like(q2)), s_n[:, cs:cs + RET_DV]))
            o = _rms(o) * g[:, cs:cs + RET_DV].astype(F32)
            out_ref[0, pl.ds(st, c), cs:cs + RET_DV] = o.astype(BF16)

    for n in range(ncc):
        sl = slice(n * c, (n + 1) * c)
        output(qc[0, sl, :], kc[0, sl, :], vc[0, sl, :], gc[0, sl, :], n, rc, n * c)

    def out_body(i, carry):
        for j in range(unroll):
            n = i * unroll + j
            st = pl.multiple_of(n * c, c)
            sl = pl.ds(st, c)
            output(ql[0, sl, :], kl[0, sl, :], vl[0, sl, :], gl[0, sl, :], ncc + n, rl, st)
        return carry

    lax.fori_loop(0, ncl // unroll, out_body, 0)


def _retention(lg_rows, q_l, k_l, v_l, g_l, q_c, k_c, v_c, g_c):
    b, n, _ = q_l.shape
    nc = q_c.shape[1]
    ncl, ncc = n // RET_CHUNK, nc // RET_CHUNK
    pairs = RET_HEADS // 2
    unroll = 8 if ncl % 8 == 0 else 1

    def spec(rows, width):
        return pl.BlockSpec((1, rows, width), lambda i, p: (i, 0, p))

    return pl.pallas_call(
        functools.partial(_ret_kernel, ncl=ncl, ncc=ncc, unroll=unroll),
        out_shape=[jax.ShapeDtypeStruct((b, n, RET_WIDTH), BF16),
                   jax.ShapeDtypeStruct((b, nc, RET_WIDTH), BF16)],
        grid=(b, pairs),
        in_specs=[pl.BlockSpec((2 * RET_HEADS, 2 * RET_DK), lambda i, p: (0, 0)),
                  spec(n, 2 * RET_DK), spec(n, 2 * RET_DK), spec(n, 2 * RET_DV), spec(n, 2 * RET_DV),
                  spec(nc, 2 * RET_DK), spec(nc, 2 * RET_DK), spec(nc, 2 * RET_DV), spec(nc, 2 * RET_DV)],
        out_specs=[spec(n, 2 * RET_DV), spec(nc, 2 * RET_DV)],
        scratch_shapes=[pltpu.VMEM((ncl + ncc, 4 * RET_DK, 2 * RET_DV), F32)],
        compiler_params=_cp("parallel", "parallel"),
        name="retention",
    )(lg_rows, q_l, k_l, v_l, g_l, q_c, k_c, v_c, g_c)


def _dot_hi(a, b, contract=(1, 0)):
    dims = (((contract[0],), (contract[1],)), ((), ()))
    return lax.dot_general(a, b, dims, preferred_element_type=F32, precision=lax.Precision.HIGHEST)


def _dot_sel(a, b, contract, data):
    dims = (((contract[0],), (contract[1],)), ((), ()))
    x = (a, b)[data]
    hi = x.astype(BF16)
    lo = (x - hi.astype(F32)).astype(BF16)
    dd = lambda piece: lax.dot_general(*((piece, b) if data == 0 else (a, piece)), dims, preferred_element_type=F32)
    return dd(hi) + dd(lo)


def _s5_prep_kernel(ar_row, ai_row, ldt, btr, bti, ctr, cti, rm, rwin, rwout, abig):
    t, g, p, kp = S5_CHUNK, S5_GROUP, S5_STATE, S5_POW
    row = S5_ROW
    i0 = lambda shape: lax.broadcasted_iota(jnp.int32, shape, 0)
    i1 = lambda shape: lax.broadcasted_iota(jnp.int32, shape, 1)
    f32 = lambda m: jnp.where(m, 1.0, 0.0).astype(BF16)
    s_of_r = i0((row, kp)) // g
    k_of_l = i1((row, kp))
    sel_rows = (f32(k_of_l == t - 1 - s_of_r), f32(k_of_l == s_of_r))
    t_of_c = i1((kp, row)) // g
    k_of_s = i0((kp, row))
    sel_out = (f32(k_of_s == t_of_c + 1), f32(k_of_s == t - t_of_c))
    sel_lag = (f32(k_of_s == t_of_c), f32(k_of_s == t - 1 - t_of_c))
    tile_l = f32(i1((g, row)) % g == i0((g, row)))
    tile_r = f32(i0((row, g)) % g == i1((row, g)))
    lane = i1((g, row))
    k_col = i0((kp, 1)).astype(F32)
    k_row = i1((1, kp)).astype(F32)
    first = i0((8, 1)) == 0

    def outer(a, k):
        a8 = jnp.where(first, jnp.broadcast_to(a, (8, a.shape[1])), 0.0)
        return _dot_hi(a8, jnp.broadcast_to(k, (8, k.shape[1])), (0, 0))

    for q in range(S5_GPB):
        lags = []
        for d in range(2):
            dt = jnp.exp(ldt[d, q])
            are_r, aim_r = ar_row[d, q], ai_row[d, q]
            mag = jnp.exp(are_r * dt)
            ang = aim_r * dt
            nr, ni = mag * jnp.cos(ang) - 1.0, mag * jnp.sin(ang)
            den = jnp.square(are_r) + jnp.square(aim_r)
            fr = (nr * are_r + ni * aim_r) / den
            fi = (ni * are_r - nr * aim_r) / den
            pm = jnp.exp(k_col * (are_r * dt))
            pa = k_col * ang
            pk_re, pk_im = pm * jnp.cos(pa), pm * jnp.sin(pa)
            pmt = jnp.exp(outer(are_r * dt, k_row))
            pat = outer(ang, k_row)
            pt_re, pt_im = pmt * jnp.cos(pat), pmt * jnp.sin(pat)
            bt_re, bt_im = _dot_sel(tile_r, btr[d, q], (1, 1), 1), _dot_sel(tile_r, bti[d, q], (1, 1), 1)
            bb_re = fr * bt_re - fi * bt_im
            bb_im = fr * bt_im + fi * bt_re
            pr_re, pr_im = _dot_sel(sel_rows[d], pk_re, (1, 0), 1), _dot_sel(sel_rows[d], pk_im, (1, 0), 1)
            w_re = pr_re * bb_re - pr_im * bb_im
            w_im = pr_re * bb_im + pr_im * bb_re
            for s in range(t):
                rows = slice(s * LANE + q * g, s * LANE + (q + 1) * g)
                rwin[0, rows, d * p:(d + 1) * p] = w_re[s * g:(s + 1) * g].astype(BF16)
                rwin[0, rows, (2 + d) * p:(3 + d) * p] = w_im[s * g:(s + 1) * g].astype(BF16)
            ct_re, ct_im = _dot_sel(ctr[d, q], tile_l, (0, 0), 0), _dot_sel(cti[d, q], tile_l, (0, 0), 0)

            def c_pow(sel):
                pc_re, pc_im = _dot_sel(pt_re, sel, (1, 0), 0), _dot_sel(pt_im, sel, (1, 0), 0)
                return ct_re * pc_re - ct_im * pc_im, ct_re * pc_im + ct_im * pc_re

            o_re, o_im = c_pow(sel_out[d])
            r0 = d * S5_GPB * p + q * p
            rwout[0, r0:r0 + p, :] = o_re.astype(BF16)
            rwout[0, S5_HALF + r0:S5_HALF + r0 + p, :] = (-o_im).astype(BF16)
            l_re, l_im = c_pow(sel_lag[d])
            lags.append(_dot_hi(bb_re[0:g], l_re) - _dot_hi(bb_im[0:g], l_im))
            abig[0, 0:1, r0:r0 + p] = pk_re[t:t + 1, :]
            abig[0, 1:2, r0:r0 + p] = pk_im[t:t + 1, :]
        for s in range(t):
            fwd = jnp.where(lane >= g * s, pltpu.roll(lags[0], g * s, 1), 0.0)
            bwd = jnp.where(lane < g * (s + 1), pltpu.roll(lags[1], (row - g * (t - 1 - s)) % row, 1), 0.0)
            rm[0, s * LANE + q * g:s * LANE + (q + 1) * g, :] = (fwd + bwd).astype(BF16)


def _s5_prep(a_re, a_im, log_dt, b_re, b_im, c_re, c_im):
    gg, p, g = S5_GROUPS, S5_STATE, S5_GROUP
    f = lambda x: x.astype(F32)
    args = (f(a_re).reshape(2, gg, 1, p), f(a_im).reshape(2, gg, 1, p),
            f(log_dt).reshape(2, gg, 1, 1),
            f(b_re), f(b_im), f(c_re), f(c_im))
    spec = lambda r, c: pl.BlockSpec((2, S5_GPB, r, c), lambda j: (0, j, 0, 0))
    out = lambda r, c: pl.BlockSpec((1, r, c), lambda j: (j, 0, 0))
    return pl.pallas_call(
        _s5_prep_kernel,
        out_shape=[jax.ShapeDtypeStruct((S5_LB, S5_BIG, S5_ROW), BF16),
                   jax.ShapeDtypeStruct((S5_LB, S5_BIG, S5_ROW), BF16),
                   jax.ShapeDtypeStruct((S5_LB, 2 * S5_HALF, S5_ROW), BF16),
                   jax.ShapeDtypeStruct((S5_LB, 2, S5_HALF), F32)],
        grid=(S5_LB,),
        in_specs=[spec(1, p), spec(1, p), spec(1, 1),
                  spec(p, g), spec(p, g), spec(g, p), spec(g, p)],
        out_specs=[out(S5_BIG, S5_ROW), out(S5_BIG, S5_ROW), out(2 * S5_HALF, S5_ROW), out(2, S5_HALF)],
        compiler_params=_cp("parallel"),
        name="s5_prep",
    )(*args)


def _s5_expand(r_ref, col_unit, col_block, row_unit):
    n = S5_BIG
    a = lax.broadcasted_iota(jnp.int32, (S5_ROW, n), 0)
    c = lax.broadcasted_iota(jnp.int32, (S5_ROW, n), 1)
    e = jnp.where((a // col_unit == c // col_block) & (a % col_unit == c % col_unit), 1.0, 0.0).astype(BF16)
    x = _dot(r_ref[0], e)
    rq = (lax.broadcasted_iota(jnp.int32, (n, n), 0) // row_unit) % S5_GPB
    cq = (lax.broadcasted_iota(jnp.int32, (n, n), 1) // col_unit) % S5_GPB
    return jnp.where(rq == cq, x, 0.0).astype(BF16)


def _s5_kernel(ul, uc, rm_ref, rwin_ref, rwout_ref, a_ref, yl, yc, x_s, m_s, win_s, wout_s, *, ncl, ncc):
    hw = S5_HALF
    hh = hw // 2

    @pl.when(pl.program_id(1) == 0)
    def _():
        m_s[...] = _s5_expand(rm_ref, S5_GROUP, LANE, S5_GROUP)
        win_s[...] = _s5_expand(rwin_ref, S5_STATE, S5_GPB * S5_STATE, S5_GROUP)
        wout_s[...] = _s5_expand(rwout_ref, S5_GROUP, LANE, S5_STATE)

    z = jnp.concatenate([uc[0, 0], ul[0, 0]], axis=0)
    x_s[...] = _dot(z, win_s[...])
    a_re = a_ref[0, 0:1, :]
    a_im = a_ref[0, 1:2, :]

    def segment(base, n, carry):
        s_re, s_im = carry
        for i in range(n):
            rf = slice(base + i, base + i + 1)
            rb = slice(base + n - 1 - i, base + n - i)
            x_re = jnp.concatenate([x_s[rf, 0:hh], x_s[rb, hh:hw]], axis=-1)
            x_im = jnp.concatenate([x_s[rf, hw:hw + hh], x_s[rb, hw + hh:2 * hw]], axis=-1)
            x_s[rf, 0:hh] = s_re[:, 0:hh]
            x_s[rb, hh:hw] = s_re[:, hh:hw]
            x_s[rf, hw:hw + hh] = s_im[:, 0:hh]
            x_s[rb, hw + hh:2 * hw] = s_im[:, hh:hw]
            s_re, s_im = a_re * s_re - a_im * s_im + x_re, a_re * s_im + a_im * s_re + x_im
        return s_re, s_im

    zero = jnp.zeros((1, hw), F32)
    carry = segment(0, ncc, (zero, zero))
    segment(ncc, ncl, carry)
    y = _dot(z, m_s[...]) + _dot(x_s[...].astype(BF16), wout_s[...])
    yc[0, 0] = y[0:ncc]
    yl[0, 0] = y[ncc:ncc + ncl]


def _s5(prep, zl, zc):
    rm, rwin, rwout, a_big = prep
    b, _, ncl, _ = zl.shape
    ncc = zc.shape[2]
    rows = lambda r: pl.BlockSpec((1, 1, r, S5_BIG), lambda j, i: (i, j, 0, 0))
    wspec = lambda r, c: pl.BlockSpec((1, r, c), lambda j, i: (j, 0, 0))
    big = pltpu.VMEM((S5_BIG, S5_BIG), BF16)
    return pl.pallas_call(
        functools.partial(_s5_kernel, ncl=ncl, ncc=ncc),
        out_shape=[jax.ShapeDtypeStruct(zl.shape, F32), jax.ShapeDtypeStruct(zc.shape, F32)],
        grid=(S5_LB, b),
        in_specs=[rows(ncl), rows(ncc), wspec(S5_BIG, S5_ROW), wspec(S5_BIG, S5_ROW),
                  wspec(2 * S5_HALF, S5_ROW), wspec(2, S5_HALF)],
        out_specs=[rows(ncl), rows(ncc)],
        scratch_shapes=[pltpu.VMEM((ncc + ncl, 2 * S5_HALF), F32), big, big, big],
        compiler_params=_cp("parallel", "arbitrary"),
        name="s5",
    )(zl, zc, rm, rwin, rwout, a_big)


def _mix_mlp_kernel(*refs, mod_row, s5_merge, final_norm, fb):
    if s5_merge:
        (h_ref, r_ref, y5_ref, u_ref, ds_ref, wg_ref, bg_ref, wo_ref, mod_ref, nm_ref, w1_ref, w2_ref,
         *rest) = refs
    else:
        h_ref, r_ref, wo_ref, mod_ref, nm_ref, w1_ref, w2_ref, *rest = refs
    if final_norm:
        nf_ref, o_ref, *scratch = rest
    else:
        o_ref, *scratch = rest
    row = pl.program_id(0) if mod_row is None else mod_row
    if s5_merge:
        (y_s,) = scratch
        nch = y_s.shape[1] // S5_CHUNK
        for lb in range(S5_LB):
            for s in range(S5_CHUNK):
                y_s[lb, pl.ds(s, nch, stride=S5_CHUNK), :] = y5_ref[0, lb, :, s * LANE:(s + 1) * LANE]
        y5 = jnp.concatenate([y_s[lb] for lb in range(S5_LB)], axis=-1)
        y = jax.nn.gelu(y5 + ds_ref[...] * u_ref[0])
        y = y * jax.nn.sigmoid(_dot(y.astype(BF16), wg_ref[...]) + bg_ref[...])
        mix = _dot(r_ref[0], wo_ref[0:RET_WIDTH, :]) + _dot(y.astype(BF16), wo_ref[RET_WIDTH:D_MODEL, :])
    else:
        mix = _dot(r_ref[0], wo_ref[...])
    h1 = h_ref[0] + _mod_chunk(mod_ref, row, 2) * mix
    xn = _rms(h1) * nm_ref[...]
    xm = (xn * (1.0 + _mod_chunk(mod_ref, row, 4)) + _mod_chunk(mod_ref, row, 3)).astype(BF16)
    acc = None
    for j in range(D_FF // fb):
        a = jnp.square(jnp.maximum(_dot(xm, w1_ref[:, j * fb:(j + 1) * fb].astype(BF16)), 0.0)).astype(BF16)
        part = _dot(a, w2_ref[j * fb:(j + 1) * fb, :].astype(BF16))
        acc = part if acc is None else acc + part
    h2 = h1 + _mod_chunk(mod_ref, row, 5) * acc
    if final_norm:
        h2 = _rms(h2) * nf_ref[...]
    o_ref[0] = h2


def _mix_mlp(h, r, s5y, u, s5p, wo, mod, layer, nm, w1, w2, nf, mod_row, tm, fb, name):
    b, n, _ = h.shape
    s5_merge = s5y is not None
    final_norm = nf is not None
    one = pl.Buffered(1)
    row_spec = lambda width: pl.BlockSpec((1, tm, width), lambda i, t: (i, t, 0))
    const = lambda shape: pl.BlockSpec(shape, lambda i, t: (0,) * len(shape), pipeline_mode=one)
    in_specs = [row_spec(D_MODEL), row_spec(r.shape[-1])]
    args = [h, r]
    if s5_merge:
        d_skip, w_glu, b_glu = s5p
        z_spec = pl.BlockSpec((1, S5_LB, tm // S5_CHUNK, S5_BIG), lambda i, t: (i, 0, t, 0))
        in_specs += [z_spec, row_spec(S5_WIDTH), const((1, S5_WIDTH)),
                     const((S5_WIDTH, S5_WIDTH)), const((1, S5_WIDTH))]
        args += [s5y, u, d_skip, w_glu, b_glu]
    in_specs += [const((D_MODEL, D_MODEL)),
                 pl.BlockSpec((1, MOD_ROWS, N_MOD * D_MODEL), lambda i, t: (layer, 0, 0), pipeline_mode=one),
                 const((1, D_MODEL))]
    if w1.ndim == 3:
        in_specs += [pl.BlockSpec((None, D_MODEL, D_FF), lambda i, t: (layer, 0, 0), pipeline_mode=one),
                     pl.BlockSpec((None, D_FF, D_MODEL), lambda i, t: (layer, 0, 0), pipeline_mode=one)]
    else:
        in_specs += [const((D_MODEL, D_FF)), const((D_FF, D_MODEL))]
    args += [wo, mod, nm, w1, w2]
    if final_norm:
        in_specs.append(const((1, D_MODEL)))
        args.append(nf)
    return pl.pallas_call(
        functools.partial(_mix_mlp_kernel, mod_row=mod_row, s5_merge=s5_merge, final_norm=final_norm, fb=fb),
        out_shape=jax.ShapeDtypeStruct((b, n, D_MODEL), F32),
        grid=(b, n // tm),
        in_specs=in_specs,
        out_specs=row_spec(D_MODEL),
        scratch_shapes=[pltpu.VMEM((S5_LB, tm, LANE), F32)] if s5_merge else [],
        compiler_params=_cp("parallel", "parallel"),
        name=name,
    )(*args)


def _hgrn_lower_bounds(lbl_ref, layer):
    out = []
    for d in range(2):
        z = [lbl_ref[d, k:k + 1, :] for k in range(DEPTH)]
        zmax = functools.reduce(jnp.maximum, z)
        e = [jnp.exp(v - zmax) for v in z]
        tot = functools.reduce(lambda a, b_: a + b_, e)
        lb = jnp.zeros_like(tot)
        for k in range(1, layer + 1):
            lb = lb + e[k] / tot
        out.append(lb)
    return out


def _inproj1_kernel(h_ref, mod_ref, ng_ref, lbl_ref, w_ref, *out_refs, mod_row, layer, latent):
    row = pl.program_id(0) if mod_row is None else mod_row
    xn = _rms(h_ref[0]) * ng_ref[...]
    xm = (xn * (1.0 + _mod_chunk(mod_ref, row, 1)) + _mod_chunk(mod_ref, row, 0)).astype(BF16)
    col = lambda k: _dot(xm, w_ref[:, k * D_MODEL:(k + 1) * D_MODEL].astype(BF16))
    lbs = _hgrn_lower_bounds(lbl_ref, layer)
    refs = list(out_refs)
    if latent:
        refs.pop(0)[0] = col(0).astype(BF16)
    for d in range(2):
        t = (1.0 - lbs[d]) * _sigmoid(col(1 + d))
        refs.pop(0)[0] = jnp.log(lbs[d] + t)
        refs.pop(0)[0] = ((1.0 - lbs[d]) - t).astype(BF16)
    refs.pop(0)[0] = col(3).astype(BF16)
    if latent:
        g = col(4)
        refs.pop(0)[0] = (g * _sigmoid(g)).astype(BF16)


def _inproj1(h, mod, layer, ng, lb_logits, w, latent, mod_row, tm, name):
    b, n, _ = h.shape
    one = pl.Buffered(1)
    row_spec = pl.BlockSpec((1, tm, D_MODEL), lambda i, t: (i, t, 0))
    dtypes = ([BF16] if latent else []) + [F32, BF16, F32, BF16, BF16] + ([BF16] if latent else [])
    return pl.pallas_call(
        functools.partial(_inproj1_kernel, mod_row=mod_row, layer=layer, latent=latent),
        out_shape=[jax.ShapeDtypeStruct((b, n, D_MODEL), dt) for dt in dtypes],
        grid=(b, n // tm),
        in_specs=[row_spec,
                  pl.BlockSpec((1, MOD_ROWS, N_MOD * D_MODEL), lambda i, t: (layer, 0, 0), pipeline_mode=one),
                  pl.BlockSpec((1, D_MODEL), lambda i, t: (0, 0), pipeline_mode=one),
                  pl.BlockSpec(lb_logits.shape, lambda i, t: (0, 0, 0), pipeline_mode=one),
                  pl.BlockSpec(w.shape, lambda i, t: (0, 0), pipeline_mode=one)],
        out_specs=[row_spec] * len(dtypes),
        compiler_params=_cp("parallel", "parallel"),
        name=name,
    )(h, mod, ng, lb_logits, w)


def _cumsum_mm(tri, x):
    acc = None
    r = x
    for i in range(HG_SPLIT):
        p = r.astype(BF16)
        acc = _dot(tri, p) if acc is None else acc + _dot(tri, p)
        if i + 1 < HG_SPLIT:
            r = r - p.astype(F32)
    return acc


def _hgrn_kernel(ng_ref, ql, lffl, kfl, lfbl, kbl, il, sgl, lffc, kfc, lfbc, kbc, ic, o_ref,
                 qin_s, att_s, kv_s, et_s, kvc_s, etc_s, cum_s, ko_s, qt_s, kt_s, *, nbl, nbc, unroll, out_blocks):
    cb = HG_BLOCK
    mid = cb // 2
    gb = HG_GROUP
    gr = gb * cb
    dk = HG_DK

    ri = lax.broadcasted_iota(jnp.int32, (gr, gr), 0)
    ci = lax.broadcasted_iota(jnp.int32, (gr, gr), 1)
    same = (ri // cb) == (ci // cb)
    rb = lax.broadcasted_iota(jnp.int32, (cb, cb), 0)
    cbi = lax.broadcasted_iota(jnp.int32, (cb, cb), 1)
    tri_l = jnp.where(same & (ri >= ci), 1.0, 0.0).astype(BF16)
    dirs = ((0, rb >= cbi, mid - 1, cb - 1), (1, rb <= cbi, mid, 0))

    def cumsums(lfs, slot):
        pre = _cumsum_mm(tri_l, jnp.concatenate(lfs, axis=-1))
        pre_b = pre[:, dk:].reshape(gb, cb, dk)
        cum_s[slot, 0] = pre[:, :dk]
        cum_s[slot, 1] = (pre_b[:, cb - 1:cb, :] - pre_b).reshape(gr, dk) + lfs[1]

    def operands(slot, kks, q, n0, et_ref):
        for d, keep, ref_row, tot_row in dirs:
            cum = cum_s[slot, d].reshape(gb, cb, dk)
            kk = kks[d].astype(F32).reshape(gb, cb, dk)
            ref = cum[:, ref_row:ref_row + 1, :]
            tot = cum[:, tot_row:tot_row + 1, :]
            e = cum - ref
            kt = kk * jnp.exp(-e)
            ko_s[slot, d] = (kt * jnp.exp(tot - ref)).astype(BF16).reshape(gr, dk)
            e_tot = jnp.exp(tot)
            for j in range(gb):
                et_ref[n0 + j, :, d * dk:(d + 1) * dk] = e_tot[j]
            if q is not None:
                qt = q.astype(F32).reshape(gb, cb, dk) * jnp.exp(e)
                qin_s[pl.ds(pl.multiple_of(n0 * cb, gr), gr), d * dk:(d + 1) * dk] = (
                    (qt * jnp.exp(ref)).astype(BF16).reshape(gr, dk))
                qt_s[slot, d] = qt.astype(BF16).reshape(gr, dk)
                kt_s[slot, d] = kt.astype(BF16).reshape(gr, dk)

    def matmuls(slot, v, with_q, n0, kv_ref):
        for d in range(2):
            for j in range(gb):
                rows = slice(j * cb, (j + 1) * cb)
                kv_ref[n0 + j, :, d * dk:(d + 1) * dk] = _dot_tn(v[rows], ko_s[slot, d, rows, :])
        if with_q:
            for j in range(gb):
                rows = slice(j * cb, (j + 1) * cb)
                att = (jnp.where(dirs[0][1], _dot_nt(qt_s[slot, 0, rows, :], kt_s[slot, 0, rows, :]), 0.0)
                       + jnp.where(dirs[1][1], _dot_nt(qt_s[slot, 1, rows, :], kt_s[slot, 1, rows, :]), 0.0))
                att_s[n0 + j] = att.astype(BF16)

    for g in range(nbc // gb):
        sl = slice(g * gr, (g + 1) * gr)
        cumsums((lffc[0, sl, :], lfbc[0, sl, :]), 0)
        operands(0, (kfc[0, sl, :], kbc[0, sl, :]), None, g * gb, etc_s)
        matmuls(0, ic[0, sl, :], False, g * gb, kvc_s)

    ngl = nbl // gb
    rows_of = lambda g: pl.ds(pl.multiple_of(g * gr, gr), gr)

    def lat_cumsums(g, slot):
        cumsums((lffl[0, rows_of(g), :], lfbl[0, rows_of(g), :]), slot)

    def lat_operands(g, slot):
        operands(slot, (kfl[0, rows_of(g), :], kbl[0, rows_of(g), :]), ql[0, rows_of(g), :], g * gb, et_s)

    def lat_matmuls(g, slot):
        matmuls(slot, il[0, rows_of(g), :], True, g * gb, kv_s)

    lat_cumsums(0, 0)
    lat_operands(0, 0)
    lat_cumsums(1, 1)

    def prep_body(i, carry):
        g = 2 * i
        lat_matmuls(g, 0)
        lat_operands(g + 1, 1)
        lat_cumsums(g + 2, 0)
        lat_matmuls(g + 1, 1)
        lat_operands(g + 2, 0)
        lat_cumsums(g + 3, 1)
        return carry

    lax.fori_loop(0, ngl // 2 - 1, prep_body, 0)
    lat_matmuls(ngl - 2, 0)
    lat_operands(ngl - 1, 1)
    lat_matmuls(ngl - 1, 1)

    lane = lax.broadcasted_iota(jnp.int32, (dk, 2 * dk), 1)
    is_f = lane < dk
    st = jnp.zeros((dk, 2 * dk), F32)
    for n in range(nbc):
        m = nbc - 1 - n
        st = (st * jnp.where(is_f[:1], etc_s[n], etc_s[m]) + jnp.where(is_f, kvc_s[n], kvc_s[m]))

    def rec_body(t, st):
        u = nbl - 1 - t
        inc = jnp.where(is_f, kv_s[t], kv_s[u])
        dec = jnp.where(is_f[:1], et_s[t], et_s[u])
        kv_s[t, :, 0:dk] = st[:, 0:dk]
        kv_s[u, :, dk:2 * dk] = st[:, dk:2 * dk]
        return st * dec + inc

    lax.fori_loop(0, nbl, rec_body, st, unroll=unroll)

    def out_body(i, carry):
        for j in range(out_blocks):
            n = i * out_blocks + j
            sl = pl.ds(pl.multiple_of(n * cb, cb), cb)
            o = _dot(att_s[n], il[0, sl, :]) + _dot_nt(qin_s[sl, :], kv_s[n].astype(BF16))
            o = _rms(o) * ng_ref[...] * sgl[0, sl, :].astype(F32)
            o_ref[0, sl, :] = o.astype(BF16)
        return carry

    lax.fori_loop(0, nbl // out_blocks, out_body, 0)


def _hgrn(norm_g, q_l, lff_l, kf_l, lfb_l, kb_l, i_l, sg_l, lff_c, kf_c, lfb_c, kb_c, i_c):
    b, n, _ = q_l.shape
    nc = lff_c.shape[1]
    nbl, nbc = n // HG_BLOCK, nc // HG_BLOCK
    out_blocks = min(16, nbl)
    assert nbl % (2 * HG_GROUP) == 0 and nbc % HG_GROUP == 0 and nbl % out_blocks == 0
    spec = lambda rows: pl.BlockSpec((1, rows, HG_DK), lambda i, h: (i, 0, h))
    slot = lambda dt: pltpu.VMEM((2, 2, HG_GROUP * HG_BLOCK, HG_DK), dt)
    return pl.pallas_call(
        functools.partial(_hgrn_kernel, nbl=nbl, nbc=nbc, unroll=2, out_blocks=out_blocks),
        out_shape=jax.ShapeDtypeStruct((b, n, D_MODEL), BF16),
        grid=(b, HG_HEADS),
        in_specs=[pl.BlockSpec((1, HG_DK), lambda i, h: (0, 0))] + [spec(n)] * 7 + [spec(nc)] * 5,
        out_specs=spec(n),
        scratch_shapes=[pltpu.VMEM((n, 2 * HG_DK), BF16),
                        pltpu.VMEM((nbl, HG_BLOCK, HG_BLOCK), BF16),
                        pltpu.VMEM((nbl, HG_DK, 2 * HG_DK), F32),
                        pltpu.VMEM((nbl, 1, 2 * HG_DK), F32),
                        pltpu.VMEM((nbc, HG_DK, 2 * HG_DK), F32),
                        pltpu.VMEM((nbc, 1, 2 * HG_DK), F32),
                        slot(F32), slot(BF16), slot(BF16), slot(BF16)],
        compiler_params=_cp("parallel", "parallel"),
        name="hgrn2",
    )(norm_g, q_l, lff_l, kf_l, lfb_l, kb_l, i_l, sg_l, lff_c, kf_c, lfb_c, kb_c, i_c)


def _rope_tables(n_tok):
    tok = jnp.arange(n_tok, dtype=jnp.int32)[:, None]
    row = (tok // GRID_W).astype(F32)
    col = (tok % GRID_W).astype(F32)
    n_freq = RET_DK // 4
    lane = jnp.arange(LANE, dtype=jnp.int32)[None, :]
    j = lane % (2 * n_freq)
    inv = ROPE_BASE ** (-(j % n_freq).astype(F32) / n_freq)
    ang = jnp.where(j < n_freq, row, col) * inv
    sign = jnp.where(lane % RET_DK < RET_DK // 2, -1.0, 1.0)
    return jnp.cos(ang), jnp.sin(ang) * sign


def kernel(x, c, ctx, c_ctx, w_mod, b_mod, norm_mix, norm_mlp, w_mlp_in, w_mlp_out, ab_w_in, ab_w_out, ret_logit, s5_a_re, s5_a_im, s5_log_dt, s5_b_re, s5_b_im, s5_c_re, s5_c_im, s5_d, s5_w_glu, s5_b_glu, hg_w_in, hg_w_out, hg_lb_logits, hg_norm, norm_final):
    b, n, d = x.shape
    nc = ctx.shape[1]
    assert d == D_MODEL and b + 1 <= MOD_ROWS and w_mod.shape[0] == DEPTH == 2
    assert n % 512 == 0 and nc % 256 == 0 and n % GRID_W == 0
    ctx_row = b
    tm_l, tm_c = 512, 256

    cc = jnp.zeros((MOD_ROWS, d), F32).at[:b].set(c).at[b].set(c_ctx)
    mod = _adaln(cc, w_mod, b_mod)

    row2 = lambda a: a.reshape(1, -1)
    w_in0 = ab_w_in[0]
    cos, sin = _rope_tables(n)
    ng0 = row2(norm_mix[0])
    q_l, k_l, v_l, u_l, uz_l, g_l = _inproj0(x, mod, 0, ng0, w_in0, cos, sin, None, tm_l)
    q_c, k_c, v_c, u_c, uz_c, g_c = _inproj0(ctx, mod, 0, ng0, w_in0, None, None, ctx_row, tm_c)

    log_gamma = jax.nn.log_sigmoid(ret_logit[0].astype(F32))
    lg_rows = jnp.broadcast_to(log_gamma.reshape(2 * RET_HEADS, 1), (2 * RET_HEADS, 2 * RET_DK))
    r_l, r_c = _retention(lg_rows, q_l, k_l, v_l, g_l, q_c, k_c, v_c, g_c)

    s5_ops = _s5_prep(s5_a_re[0], s5_a_im[0], s5_log_dt[0], s5_b_re[0], s5_b_im[0], s5_c_re[0], s5_c_im[0])
    y5_l, y5_c = _s5(s5_ops, uz_l, uz_c)

    s5p = (row2(s5_d[0]), s5_w_glu[0].astype(BF16), row2(s5_b_glu[0]))
    wo0 = ab_w_out[0].astype(BF16)
    w1_0, w2_0 = w_mlp_in, w_mlp_out
    nm0 = row2(norm_mlp[0])
    h_l = _mix_mlp(x, r_l, y5_l, u_l, s5p, wo0, mod, 0, nm0, w1_0, w2_0, None, None, tm_l, 1024, "mix_mlp0_lat")
    h_c = _mix_mlp(ctx, r_c, y5_c, u_c, s5p, wo0, mod, 0, nm0, w1_0, w2_0, None, ctx_row, tm_c, 1024, "mix_mlp0_ctx")

    w_in1 = hg_w_in[0]
    ng1 = row2(norm_mix[1])
    lat1 = _inproj1(h_l, mod, 1, ng1, hg_lb_logits, w_in1, True, None, tm_l, "inproj1_lat")
    ctx1 = _inproj1(h_c, mod, 1, ng1, hg_lb_logits, w_in1, False, ctx_row, tm_c, "inproj1_ctx")
    o1 = _hgrn(row2(hg_norm[0]), *lat1, *ctx1)
    return _mix_mlp(h_l, o1, None, None, None, hg_w_out[0].astype(BF16), mod, 1, row2(norm_mlp[1]),
                    w_mlp_in, w_mlp_out, row2(norm_final), None, tm_l, 1024,
                    "mix_mlp1_lat")
```

```python
import functools

import jax
import jax.numpy as jnp
from jax import lax
from jax.experimental import pallas as pl
from jax.experimental.pallas import tpu as pltpu

F32 = jnp.float32
BF16 = jnp.bfloat16

D_MODEL = 1024
DEPTH = 2
GRID_W = 64
EPS = 1e-6
ROPE_BASE = 10000.0
N_MOD = 6
RET_HEADS = 4
RET_DK = 64
RET_DV = 128
RET_QK = RET_HEADS * RET_DK
RET_WIDTH = RET_HEADS * RET_DV
RET_CHUNK = 128
S5_WIDTH = D_MODEL - RET_WIDTH
S5_GROUP = 16
S5_GROUPS = S5_WIDTH // S5_GROUP
S5_STATE = 64
S5_CHUNK = 16
S5_ROW = S5_CHUNK * S5_GROUP
LANE = 128
S5_LB = S5_WIDTH // LANE
S5_GPB = LANE // S5_GROUP
S5_BIG = S5_CHUNK * LANE
S5_HALF = S5_GPB * 2 * S5_STATE
S5_POW = 32
AB_IN = 2 * RET_QK + 2 * RET_WIDTH + S5_WIDTH
HG_HEADS = 8
HG_DK = D_MODEL // HG_HEADS
HG_BLOCK = 64
HG_GROUP = 4
HG_SPLIT = 2
D_FF = 4 * D_MODEL
MOD_ROWS = 16

VMEM_LIMIT_BYTES = 56 * 1024 * 1024


def _cp(*sem):
    return pltpu.CompilerParams(dimension_semantics=sem, vmem_limit_bytes=VMEM_LIMIT_BYTES)


def _dot(a, b):
    return jnp.dot(a, b, preferred_element_type=F32)


def _dot_nt(a, b):
    return lax.dot_general(a, b, (((1,), (1,)), ((), ())), preferred_element_type=F32)


def _dot_tn(a, b):
    return lax.dot_general(a, b, (((0,), (0,)), ((), ())), preferred_element_type=F32)


def _sigmoid(x):
    return 0.5 * jnp.tanh(0.5 * x) + 0.5


def _rms(x):
    return x * lax.rsqrt(jnp.mean(x * x, axis=-1, keepdims=True) + EPS)


def _mod_chunk(mod_ref, row, i):
    return mod_ref[0, pl.ds(row, 1), i * D_MODEL:(i + 1) * D_MODEL]


def _adaln_kernel(cc_ref, w_ref, b_ref, o_ref):
    s = jax.nn.silu(cc_ref[...]).astype(BF16)
    o_ref[0] = _dot(s, w_ref[0].astype(BF16)) + b_ref[0]


def _adaln(cc, w_mod, b_mod):
    bn = 1536
    n = N_MOD * D_MODEL
    return pl.pallas_call(
        _adaln_kernel,
        out_shape=jax.ShapeDtypeStruct((DEPTH, MOD_ROWS, n), F32),
        grid=(DEPTH, n // bn),
        in_specs=[
            pl.BlockSpec((MOD_ROWS, D_MODEL), lambda l, j: (0, 0)),
            pl.BlockSpec((1, D_MODEL, bn), lambda l, j: (l, 0, j)),
            pl.BlockSpec((1, 1, bn), lambda l, j: (l, 0, j)),
        ],
        out_specs=pl.BlockSpec((1, MOD_ROWS, bn), lambda l, j: (l, 0, j)),
        compiler_params=_cp("parallel", "parallel"),
        name="adaln",
    )(cc, w_mod, b_mod.reshape(DEPTH, 1, n))


def _rope(t, cos, sin):
    lane = lax.broadcasted_iota(jnp.int32, t.shape, 1)
    first = (lane & (RET_DK // 2)) == 0
    w = t.shape[1]
    swapped = jnp.where(first, pltpu.roll(t, w - RET_DK // 2, 1), pltpu.roll(t, RET_DK // 2, 1))
    return t * cos + swapped * sin


def _inproj0_kernel(*refs, mod_row, rope):
    if rope:
        h_ref, mod_ref, ng_ref, w_ref, cos_ref, sin_ref, q_ref, k_ref, v_ref, u_ref, uz_ref, g_ref, u_s = refs
    else:
        h_ref, mod_ref, ng_ref, w_ref, q_ref, k_ref, v_ref, u_ref, uz_ref, g_ref, u_s = refs
    row = pl.program_id(0) if mod_row is None else mod_row
    xn = _rms(h_ref[0]) * ng_ref[...]
    xm = (xn * (1.0 + _mod_chunk(mod_ref, row, 1)) + _mod_chunk(mod_ref, row, 0)).astype(BF16)
    y = _dot(xm, w_ref[...].astype(BF16))
    q = y[:, 0:RET_QK]
    k = y[:, RET_QK:2 * RET_QK]
    if rope:
        cos = jnp.concatenate([cos_ref[...]] * (RET_QK // LANE), axis=1)
        sin = jnp.concatenate([sin_ref[...]] * (RET_QK // LANE), axis=1)
        q = _rope(q, cos, sin)
        k = _rope(k, cos, sin)
    q_ref[0] = q.astype(BF16)
    k_ref[0] = (k * (RET_DK ** -0.5)).astype(BF16)
    c0 = 2 * RET_QK
    v_ref[0] = y[:, c0:c0 + RET_WIDTH].astype(BF16)
    u0 = c0 + RET_WIDTH
    u_ref[0] = y[:, u0:u0 + S5_WIDTH]
    g = y[:, u0 + S5_WIDTH:]
    g_ref[0] = (g * _sigmoid(g)).astype(BF16)
    nch = u_s.shape[1] // S5_CHUNK
    for j in range(S5_LB):
        u_s[j] = y[:, u0 + j * LANE:u0 + (j + 1) * LANE]
        for s in range(S5_CHUNK):
            uz_ref[0, j, :, s * LANE:(s + 1) * LANE] = u_s[j, pl.ds(s, nch, stride=S5_CHUNK), :].astype(BF16)


def _inproj0(h, mod, layer, ng, w, cos, sin, mod_row, tm):
    b, n, _ = h.shape
    rope = cos is not None
    row_spec = lambda width: pl.BlockSpec((1, tm, width), lambda i, j: (i, j, 0))
    in_specs = [
        row_spec(D_MODEL),
        pl.BlockSpec((1, MOD_ROWS, N_MOD * D_MODEL), lambda i, j: (layer, 0, 0)),
        pl.BlockSpec((1, D_MODEL), lambda i, j: (0, 0)),
        pl.BlockSpec((D_MODEL, AB_IN), lambda i, j: (0, 0), pipeline_mode=pl.Buffered(1)),
    ]
    args = [h, mod, ng, w]
    if rope:
        in_specs += [pl.BlockSpec((tm, LANE), lambda i, j: (j, 0))] * 2
        args += [cos, sin]
    widths = (RET_QK, RET_QK, RET_WIDTH, S5_WIDTH, RET_WIDTH)
    dtypes = (BF16, BF16, BF16, F32, BF16)
    out_shape = [jax.ShapeDtypeStruct((b, n, wd), dt) for wd, dt in zip(widths, dtypes)]
    out_specs = [row_spec(wd) for wd in widths]
    out_shape.insert(4, jax.ShapeDtypeStruct((b, S5_LB, n // S5_CHUNK, S5_BIG), BF16))
    out_specs.insert(4, pl.BlockSpec((1, S5_LB, tm // S5_CHUNK, S5_BIG), lambda i, j: (i, 0, j, 0)))
    return pl.pallas_call(
        functools.partial(_inproj0_kernel, mod_row=mod_row, rope=rope),
        out_shape=out_shape,
        grid=(b, n // tm),
        in_specs=in_specs,
        out_specs=out_specs,
        scratch_shapes=[pltpu.VMEM((S5_LB, tm, LANE), F32)],
        compiler_params=_cp("parallel", "parallel"),
        name="inproj0_lat" if rope else "inproj0_ctx",
    )(*args)


def _ret_kernel(lg_ref, ql, kl, vl, gl, qc, kc, vc, gc, rl, rc, st_s, *, ncl, ncc, unroll):
    c = RET_CHUNK
    dk2 = 2 * RET_DK
    nt = ncc + ncl
    p = pl.program_id(1)
    h_a = 2 * p
    lgf_a = lg_ref[pl.ds(h_a, 1), :]
    lgf_b = lg_ref[pl.ds(h_a + 1, 1), :]
    lgb_a = lg_ref[pl.ds(RET_HEADS + h_a, 1), :]
    lgb_b = lg_ref[pl.ds(RET_HEADS + h_a + 1, 1), :]
    lane = lax.broadcasted_iota(jnp.int32, (1, 2 * RET_DK), 1)
    is_a = lane < RET_DK
    lgf_lane = jnp.where(is_a, lgf_a, lgf_b)
    lgb_lane = jnp.where(is_a, lgb_a, lgb_b)
    ri = lax.broadcasted_iota(jnp.int32, (c, c), 0).astype(F32)
    ci = lax.broadcasted_iota(jnp.int32, (c, c), 1).astype(F32)
    diff = ri - ci

    def dmat(lgf, lgb):
        fwd = jnp.exp(jnp.maximum(diff, 0.0) * lgf)
        bwd = jnp.exp(jnp.maximum(-diff, 0.0) * lgb)
        return jnp.where(diff > 0, fwd, jnp.where(diff < 0, bwd, 2.0))

    d_a = dmat(lgf_a, lgb_a)
    d_b = dmat(lgf_b, lgb_b)
    rowp = lax.broadcasted_iota(jnp.int32, (c, dk2), 0).astype(F32)
    qd = jnp.concatenate([jnp.exp((rowp + 1.0) * lgf_lane), jnp.exp((c - rowp) * lgb_lane)], axis=1)
    kd = jnp.concatenate([jnp.exp((c - 1.0 - rowp) * lgf_lane), jnp.exp(rowp * lgb_lane)], axis=1)
    rowk = lax.broadcasted_iota(jnp.int32, (dk2, 2 * RET_DV), 0)
    cd_f = jnp.exp(c * jnp.where(rowk < RET_DK, lgf_a[:, :1], lgf_b[:, :1]))
    cd_b = jnp.exp(c * jnp.where(rowk < RET_DK, lgb_a[:, :1], lgb_b[:, :1]))
    mask2 = jnp.concatenate([is_a, is_a], axis=1)

    def increment(k, v, slot):
        kk = jnp.concatenate([k, k], axis=1).astype(F32) * kd
        st_s[slot] = _dot_tn(kk.astype(BF16), v)

    for n in range(ncc):
        increment(kc[0, n * c:(n + 1) * c, :], vc[0, n * c:(n + 1) * c, :], n)

    def inc_body(i, carry):
        for j in range(unroll):
            n = i * unroll + j
            sl = pl.ds(pl.multiple_of(n * c, c), c)
            increment(kl[0, sl, :], vl[0, sl, :], ncc + n)
        return carry

    lax.fori_loop(0, ncl // unroll, inc_body, 0)

    def rec_body(t, carry):
        sf, sb = carry
        u = jnp.where(t < ncc, ncc - 1 - t, nt - 1 - (t - ncc))
        inc_f = st_s[t, 0:dk2, :]
        inc_b = st_s[u, dk2:2 * dk2, :]
        st_s[t, 0:dk2, :] = sf
        st_s[u, dk2:2 * dk2, :] = sb
        return cd_f * sf + inc_f, cd_b * sb + inc_b

    zero = jnp.zeros((dk2, 2 * RET_DV), F32)
    lax.fori_loop(0, nt, rec_body, (zero, zero))

    def output(q, k, v, g, slot, out_ref, st):
        q2 = (jnp.concatenate([q, q], axis=1).astype(F32) * qd).astype(BF16)
        s_n = st_s[slot].astype(BF16)
        for keep_a, dm, cs in ((True, d_a, 0), (False, d_b, RET_DV)):
            m1 = is_a if keep_a else jnp.logical_not(is_a)
            m2 = mask2 if keep_a else jnp.logical_not(mask2)
            att = _dot_nt(jnp.where(m1, q, jnp.zeros_like(q)), k) * dm
            o = (_dot(att.astype(BF16), v[:, cs:cs + RET_DV])
                 + _dot(jnp.where(m2, q2, jnp.zeros_like(q2)), s_n[:, cs:cs + RET_DV]))
            o = _rms(o) * g[:, cs:cs + RET_DV].astype(F32)
            out_ref[0, pl.ds(st, c), cs:cs + RET_DV] = o.astype(BF16)

    for n in range(ncc):
        sl = slice(n * c, (n + 1) * c)
        output(qc[0, sl, :], kc[0, sl, :], vc[0, sl, :], gc[0, sl, :], n, rc, n * c)

    def out_body(i, carry):
        for j in range(unroll):
            n = i * unroll + j
            st = pl.multiple_of(n * c, c)
            sl = pl.ds(st, c)
            output(ql[0, sl, :], kl[0, sl, :], vl[0, sl, :], gl[0, sl, :], ncc + n, rl, st)
        return carry

    lax.fori_loop(0, ncl // unroll, out_body, 0)


def _retention(lg_rows, q_l, k_l, v_l, g_l, q_c, k_c, v_c, g_c):
    b, n, _ = q_l.shape
    nc = q_c.shape[1]
    ncl, ncc = n // RET_CHUNK, nc // RET_CHUNK
    pairs = RET_HEADS // 2
    unroll = ncl

    def spec(rows, width):
        return pl.BlockSpec((1, rows, width), lambda i, p: (i, 0, p))

    return pl.pallas_call(
        functools.partial(_ret_kernel, ncl=ncl, ncc=ncc, unroll=unroll),
        out_shape=[jax.ShapeDtypeStruct((b, n, RET_WIDTH), BF16),
                   jax.ShapeDtypeStruct((b, nc, RET_WIDTH), BF16)],
        grid=(b, pairs),
        in_specs=[pl.BlockSpec((2 * RET_HEADS, 2 * RET_DK), lambda i, p: (0, 0)),
                  spec(n, 2 * RET_DK), spec(n, 2 * RET_DK), spec(n, 2 * RET_DV), spec(n, 2 * RET_DV),
                  spec(nc, 2 * RET_DK), spec(nc, 2 * RET_DK), spec(nc, 2 * RET_DV), spec(nc, 2 * RET_DV)],
        out_specs=[spec(n, 2 * RET_DV), spec(nc, 2 * RET_DV)],
        scratch_shapes=[pltpu.VMEM((ncl + ncc, 4 * RET_DK, 2 * RET_DV), F32)],
        compiler_params=_cp("parallel", "parallel"),
        name="retention",
    )(lg_rows, q_l, k_l, v_l, g_l, q_c, k_c, v_c, g_c)


def _dot_hi(a, b, contract=(1, 0)):
    dims = (((contract[0],), (contract[1],)), ((), ()))
    return lax.dot_general(a, b, dims, preferred_element_type=F32, precision=lax.Precision.HIGHEST)


def _dot_sel(a, b, contract, data):
    dims = (((contract[0],), (contract[1],)), ((), ()))
    x = (a, b)[data]
    hi = x.astype(BF16)
    lo = (x - hi.astype(F32)).astype(BF16)
    dd = lambda piece: lax.dot_general(*((piece, b) if data == 0 else (a, piece)), dims, preferred_element_type=F32)
    return dd(hi) + dd(lo)


def _s5_prep_kernel(ar_row, ai_row, ldt, btr, bti, ctr, cti, rm, rwin, rwout, abig):
    t, g, p, kp = S5_CHUNK, S5_GROUP, S5_STATE, S5_POW
    row = S5_ROW
    i0 = lambda shape: lax.broadcasted_iota(jnp.int32, shape, 0)
    i1 = lambda shape: lax.broadcasted_iota(jnp.int32, shape, 1)
    f32 = lambda m: jnp.where(m, 1.0, 0.0).astype(BF16)
    s_of_r = i0((row, kp)) // g
    k_of_l = i1((row, kp))
    sel_rows = (f32(k_of_l == t - 1 - s_of_r), f32(k_of_l == s_of_r))
    t_of_c = i1((kp, row)) // g
    k_of_s = i0((kp, row))
    sel_out = (f32(k_of_s == t_of_c + 1), f32(k_of_s == t - t_of_c))
    sel_lag = (f32(k_of_s == t_of_c), f32(k_of_s == t - 1 - t_of_c))
    tile_l = f32(i1((g, row)) % g == i0((g, row)))
    tile_r = f32(i0((row, g)) % g == i1((row, g)))
    lane = i1((g, row))
    k_col = i0((kp, 1)).astype(F32)
    k_row = i1((1, kp)).astype(F32)
    first = i0((8, 1)) == 0

    def outer(a, k):
        a8 = jnp.where(first, jnp.broadcast_to(a, (8, a.shape[1])), 0.0)
        return _dot_hi(a8, jnp.broadcast_to(k, (8, k.shape[1])), (0, 0))

    for q in range(S5_GPB):
        lags = []
        for d in range(2):
            dt = jnp.exp(ldt[d, q])
            are_r, aim_r = ar_row[d, q], ai_row[d, q]
            mag = jnp.exp(are_r * dt)
            ang = aim_r * dt
            nr, ni = mag * jnp.cos(ang) - 1.0, mag * jnp.sin(ang)
            den = jnp.square(are_r) + jnp.square(aim_r)
            fr = (nr * are_r + ni * aim_r) / den
            fi = (ni * are_r - nr * aim_r) / den
            pm = jnp.exp(k_col * (are_r * dt))
            pa = k_col * ang
            pk_re, pk_im = pm * jnp.cos(pa), pm * jnp.sin(pa)
            pmt = jnp.exp(outer(are_r * dt, k_row))
            pat = outer(ang, k_row)
            pt_re, pt_im = pmt * jnp.cos(pat), pmt * jnp.sin(pat)
            bt_re, bt_im = _dot_sel(tile_r, btr[d, q], (1, 1), 1), _dot_sel(tile_r, bti[d, q], (1, 1), 1)
            bb_re = fr * bt_re - fi * bt_im
            bb_im = fr * bt_im + fi * bt_re
            pr_re, pr_im = _dot_sel(sel_rows[d], pk_re, (1, 0), 1), _dot_sel(sel_rows[d], pk_im, (1, 0), 1)
            w_re = pr_re * bb_re - pr_im * bb_im
            w_im = pr_re * bb_im + pr_im * bb_re
            for s in range(t):
                rows = slice(s * LANE + q * g, s * LANE + (q + 1) * g)
                rwin[0, rows, d * p:(d + 1) * p] = w_re[s * g:(s + 1) * g].astype(BF16)
                rwin[0, rows, (2 + d) * p:(3 + d) * p] = w_im[s * g:(s + 1) * g].astype(BF16)
            ct_re, ct_im = _dot_sel(ctr[d, q], tile_l, (0, 0), 0), _dot_sel(cti[d, q], tile_l, (0, 0), 0)

            def c_pow(sel):
                pc_re, pc_im = _dot_sel(pt_re, sel, (1, 0), 0), _dot_sel(pt_im, sel, (1, 0), 0)
                return ct_re * pc_re - ct_im * pc_im, ct_re * pc_im + ct_im * pc_re

            o_re, o_im = c_pow(sel_out[d])
            r0 = d * S5_GPB * p + q * p
            rwout[0, r0:r0 + p, :] = o_re.astype(BF16)
            rwout[0, S5_HALF + r0:S5_HALF + r0 + p, :] = (-o_im).astype(BF16)
            l_re, l_im = c_pow(sel_lag[d])
            lags.append(_dot_hi(bb_re[0:g], l_re) - _dot_hi(bb_im[0:g], l_im))
            abig[0, 0:1, r0:r0 + p] = pk_re[t:t + 1, :]
            abig[0, 1:2, r0:r0 + p] = pk_im[t:t + 1, :]
        for s in range(t):
            fwd = jnp.where(lane >= g * s, pltpu.roll(lags[0], g * s, 1), 0.0)
            bwd = jnp.where(lane < g * (s + 1), pltpu.roll(lags[1], (row - g * (t - 1 - s)) % row, 1), 0.0)
            rm[0, s * LANE + q * g:s * LANE + (q + 1) * g, :] = (fwd + bwd).astype(BF16)


def _s5_prep(a_re, a_im, log_dt, b_re, b_im, c_re, c_im):
    gg, p, g = S5_GROUPS, S5_STATE, S5_GROUP
    f = lambda x: x.astype(F32)
    args = (f(a_re).reshape(2, gg, 1, p), f(a_im).reshape(2, gg, 1, p),
            f(log_dt).reshape(2, gg, 1, 1),
            f(b_re), f(b_im), f(c_re), f(c_im))
    spec = lambda r, c: pl.BlockSpec((2, S5_GPB, r, c), lambda j: (0, j, 0, 0))
    out = lambda r, c: pl.BlockSpec((1, r, c), lambda j: (j, 0, 0))
    return pl.pallas_call(
        _s5_prep_kernel,
        out_shape=[jax.ShapeDtypeStruct((S5_LB, S5_BIG, S5_ROW), BF16),
                   jax.ShapeDtypeStruct((S5_LB, S5_BIG, S5_ROW), BF16),
                   jax.ShapeDtypeStruct((S5_LB, 2 * S5_HALF, S5_ROW), BF16),
                   jax.ShapeDtypeStruct((S5_LB, 2, S5_HALF), F32)],
        grid=(S5_LB,),
        in_specs=[spec(1, p), spec(1, p), spec(1, 1),
                  spec(p, g), spec(p, g), spec(g, p), spec(g, p)],
        out_specs=[out(S5_BIG, S5_ROW), out(S5_BIG, S5_ROW), out(2 * S5_HALF, S5_ROW), out(2, S5_HALF)],
        compiler_params=_cp("parallel"),
        name="s5_prep",
    )(*args)


def _s5_expand(r_ref, col_unit, col_block, row_unit):
    n = S5_BIG
    a = lax.broadcasted_iota(jnp.int32, (S5_ROW, n), 0)
    c = lax.broadcasted_iota(jnp.int32, (S5_ROW, n), 1)
    e = jnp.where((a // col_unit == c // col_block) & (a % col_unit == c % col_unit), 1.0, 0.0).astype(BF16)
    x = _dot(r_ref[0], e)
    rq = (lax.broadcasted_iota(jnp.int32, (n, n), 0) // row_unit) % S5_GPB
    cq = (lax.broadcasted_iota(jnp.int32, (n, n), 1) // col_unit) % S5_GPB
    return jnp.where(rq == cq, x, 0.0).astype(BF16)


def _s5_kernel(ul, uc, rm_ref, rwin_ref, rwout_ref, a_ref, yl, yc, x_s, m_s, win_s, wout_s, *, ncl, ncc):
    hw = S5_HALF
    hh = hw // 2

    @pl.when(pl.program_id(1) == 0)
    def _():
        m_s[...] = _s5_expand(rm_ref, S5_GROUP, LANE, S5_GROUP)
        win_s[...] = _s5_expand(rwin_ref, S5_STATE, S5_GPB * S5_STATE, S5_GROUP)
        wout_s[...] = _s5_expand(rwout_ref, S5_GROUP, LANE, S5_STATE)

    z = jnp.concatenate([uc[0, 0], ul[0, 0]], axis=0)
    x_s[...] = _dot(z, win_s[...])
    a_re = a_ref[0, 0:1, :]
    a_im = a_ref[0, 1:2, :]

    def segment(base, n, carry):
        s_re, s_im = carry
        for i in range(n):
            rf = slice(base + i, base + i + 1)
            rb = slice(base + n - 1 - i, base + n - i)
            x_re = jnp.concatenate([x_s[rf, 0:hh], x_s[rb, hh:hw]], axis=-1)
            x_im = jnp.concatenate([x_s[rf, hw:hw + hh], x_s[rb, hw + hh:2 * hw]], axis=-1)
            x_s[rf, 0:hh] = s_re[:, 0:hh]
            x_s[rb, hh:hw] = s_re[:, hh:hw]
            x_s[rf, hw:hw + hh] = s_im[:, 0:hh]
            x_s[rb, hw + hh:2 * hw] = s_im[:, hh:hw]
            s_re, s_im = a_re * s_re - a_im * s_im + x_re, a_re * s_im + a_im * s_re + x_im
        return s_re, s_im

    zero = jnp.zeros((1, hw), F32)
    carry = segment(0, ncc, (zero, zero))
    segment(ncc, ncl, carry)
    y = _dot(z, m_s[...]) + _dot(x_s[...].astype(BF16), wout_s[...])
    yc[0, 0] = y[0:ncc]
    yl[0, 0] = y[ncc:ncc + ncl]


def _s5(prep, zl, zc):
    rm, rwin, rwout, a_big = prep
    b, _, ncl, _ = zl.shape
    ncc = zc.shape[2]
    rows = lambda r: pl.BlockSpec((1, 1, r, S5_BIG), lambda j, i: (i, j, 0, 0))
    wspec = lambda r, c: pl.BlockSpec((1, r, c), lambda j, i: (j, 0, 0))
    big = pltpu.VMEM((S5_BIG, S5_BIG), BF16)
    return pl.pallas_call(
        functools.partial(_s5_kernel, ncl=ncl, ncc=ncc),
        out_shape=[jax.ShapeDtypeStruct(zl.shape, F32), jax.ShapeDtypeStruct(zc.shape, F32)],
        grid=(S5_LB, b),
        in_specs=[rows(ncl), rows(ncc), wspec(S5_BIG, S5_ROW), wspec(S5_BIG, S5_ROW),
                  wspec(2 * S5_HALF, S5_ROW), wspec(2, S5_HALF)],
        out_specs=[rows(ncl), rows(ncc)],
        scratch_shapes=[pltpu.VMEM((ncc + ncl, 2 * S5_HALF), F32), big, big, big],
        compiler_params=_cp("parallel", "arbitrary"),
        name="s5",
    )(zl, zc, rm, rwin, rwout, a_big)


def _mix_mlp_kernel(*refs, mod_row, s5_merge, final_norm, fb):
    if s5_merge:
        (h_ref, r_ref, y5_ref, u_ref, ds_ref, wg_ref, bg_ref, wo_ref, mod_ref, nm_ref, w1_ref, w2_ref,
         *rest) = refs
    else:
        h_ref, r_ref, wo_ref, mod_ref, nm_ref, w1_ref, w2_ref, *rest = refs
    if final_norm:
        nf_ref, o_ref, *scratch = rest
    else:
        o_ref, *scratch = rest
    row = pl.program_id(0) if mod_row is None else mod_row
    if s5_merge:
        (y_s,) = scratch
        nch = y_s.shape[1] // S5_CHUNK
        for lb in range(S5_LB):
            for s in range(S5_CHUNK):
                y_s[lb, pl.ds(s, nch, stride=S5_CHUNK), :] = y5_ref[0, lb, :, s * LANE:(s + 1) * LANE]
        y5 = jnp.concatenate([y_s[lb] for lb in range(S5_LB)], axis=-1)
        y = jax.nn.gelu(y5 + ds_ref[...] * u_ref[0])
        y = y * jax.nn.sigmoid(_dot(y.astype(BF16), wg_ref[...]) + bg_ref[...])
        mix = _dot(r_ref[0], wo_ref[0:RET_WIDTH, :]) + _dot(y.astype(BF16), wo_ref[RET_WIDTH:D_MODEL, :])
    else:
        mix = _dot(r_ref[0], wo_ref[...])
    h1 = h_ref[0] + _mod_chunk(mod_ref, row, 2) * mix
    xn = _rms(h1) * nm_ref[...]
    xm = (xn * (1.0 + _mod_chunk(mod_ref, row, 4)) + _mod_chunk(mod_ref, row, 3)).astype(BF16)
    acc = None
    for j in range(D_FF // fb):
        a = jnp.square(jnp.maximum(_dot(xm, w1_ref[:, j * fb:(j + 1) * fb].astype(BF16)), 0.0)).astype(BF16)
        part = _dot(a, w2_ref[j * fb:(j + 1) * fb, :].astype(BF16))
        acc = part if acc is None else acc + part
    h2 = h1 + _mod_chunk(mod_ref, row, 5) * acc
    if final_norm:
        h2 = _rms(h2) * nf_ref[...]
    o_ref[0] = h2


def _mix_mlp(h, r, s5y, u, s5p, wo, mod, layer, nm, w1, w2, nf, mod_row, tm, fb, name):
    b, n, _ = h.shape
    s5_merge = s5y is not None
    final_norm = nf is not None
    one = pl.Buffered(1)
    row_spec = lambda width: pl.BlockSpec((1, tm, width), lambda i, t: (i, t, 0))
    const = lambda shape: pl.BlockSpec(shape, lambda i, t: (0,) * len(shape), pipeline_mode=one)
    in_specs = [row_spec(D_MODEL), row_spec(r.shape[-1])]
    args = [h, r]
    if s5_merge:
        d_skip, w_glu, b_glu = s5p
        z_spec = pl.BlockSpec((1, S5_LB, tm // S5_CHUNK, S5_BIG), lambda i, t: (i, 0, t, 0))
        in_specs += [z_spec, row_spec(S5_WIDTH), const((1, S5_WIDTH)),
                     const((S5_WIDTH, S5_WIDTH)), const((1, S5_WIDTH))]
        args += [s5y, u, d_skip, w_glu, b_glu]
    in_specs += [const((D_MODEL, D_MODEL)),
                 pl.BlockSpec((1, MOD_ROWS, N_MOD * D_MODEL), lambda i, t: (layer, 0, 0), pipeline_mode=one),
                 const((1, D_MODEL))]
    if w1.ndim == 3:
        in_specs += [pl.BlockSpec((None, D_MODEL, D_FF), lambda i, t: (layer, 0, 0), pipeline_mode=one),
                     pl.BlockSpec((None, D_FF, D_MODEL), lambda i, t: (layer, 0, 0), pipeline_mode=one)]
    else:
        in_specs += [const((D_MODEL, D_FF)), const((D_FF, D_MODEL))]
    args += [wo, mod, nm, w1, w2]
    if final_norm:
        in_specs.append(const((1, D_MODEL)))
        args.append(nf)
    return pl.pallas_call(
        functools.partial(_mix_mlp_kernel, mod_row=mod_row, s5_merge=s5_merge, final_norm=final_norm, fb=fb),
        out_shape=jax.ShapeDtypeStruct((b, n, D_MODEL), F32),
        grid=(b, n // tm),
        in_specs=in_specs,
        out_specs=row_spec(D_MODEL),
        scratch_shapes=[pltpu.VMEM((S5_LB, tm, LANE), F32)] if s5_merge else [],
        compiler_params=_cp("parallel", "parallel"),
        name=name,
    )(*args)


def _hgrn_lower_bounds(lbl_ref, layer):
    out = []
    for d in range(2):
        z = [lbl_ref[d, k:k + 1, :] for k in range(DEPTH)]
        zmax = functools.reduce(jnp.maximum, z)
        e = [jnp.exp(v - zmax) for v in z]
        tot = functools.reduce(lambda a, b_: a + b_, e)
        lb = jnp.zeros_like(tot)
        for k in range(1, layer + 1):
            lb = lb + e[k] / tot
        out.append(lb)
    return out


def _inproj1_kernel(h_ref, mod_ref, ng_ref, lbl_ref, w_ref, *out_refs, mod_row, layer, latent):
    row = pl.program_id(0) if mod_row is None else mod_row
    xn = _rms(h_ref[0]) * ng_ref[...]
    xm = (xn * (1.0 + _mod_chunk(mod_ref, row, 1)) + _mod_chunk(mod_ref, row, 0)).astype(BF16)
    col = lambda k: _dot(xm, w_ref[:, k * D_MODEL:(k + 1) * D_MODEL].astype(BF16))
    lbs = _hgrn_lower_bounds(lbl_ref, layer)
    refs = list(out_refs)
    if latent:
        refs.pop(0)[0] = col(0).astype(BF16)
    for d in range(2):
        t = (1.0 - lbs[d]) * _sigmoid(col(1 + d))
        refs.pop(0)[0] = jnp.log(lbs[d] + t)
        refs.pop(0)[0] = ((1.0 - lbs[d]) - t).astype(BF16)
    refs.pop(0)[0] = col(3).astype(BF16)
    if latent:
        g = col(4)
        refs.pop(0)[0] = (g * _sigmoid(g)).astype(BF16)


def _inproj1(h, mod, layer, ng, lb_logits, w, latent, mod_row, tm, name):
    b, n, _ = h.shape
    one = pl.Buffered(1)
    row_spec = pl.BlockSpec((1, tm, D_MODEL), lambda i, t: (i, t, 0))
    dtypes = ([BF16] if latent else []) + [F32, BF16, F32, BF16, BF16] + ([BF16] if latent else [])
    return pl.pallas_call(
        functools.partial(_inproj1_kernel, mod_row=mod_row, layer=layer, latent=latent),
        out_shape=[jax.ShapeDtypeStruct((b, n, D_MODEL), dt) for dt in dtypes],
        grid=(b, n // tm),
        in_specs=[row_spec,
                  pl.BlockSpec((1, MOD_ROWS, N_MOD * D_MODEL), lambda i, t: (layer, 0, 0), pipeline_mode=one),
                  pl.BlockSpec((1, D_MODEL), lambda i, t: (0, 0), pipeline_mode=one),
                  pl.BlockSpec(lb_logits.shape, lambda i, t: (0, 0, 0), pipeline_mode=one),
                  pl.BlockSpec(w.shape, lambda i, t: (0, 0), pipeline_mode=one)],
        out_specs=[row_spec] * len(dtypes),
        compiler_params=_cp("parallel", "parallel"),
        name=name,
    )(h, mod, ng, lb_logits, w)


def _cumsum_mm(tri, x):
    acc = None
    r = x
    for i in range(HG_SPLIT):
        p = r.astype(BF16)
        acc = _dot(tri, p) if acc is None else acc + _dot(tri, p)
        if i + 1 < HG_SPLIT:
            r = r - p.astype(F32)
    return acc


def _hgrn_kernel(ng_ref, ql, lffl, kfl, lfbl, kbl, il, sgl, lffc, kfc, lfbc, kbc, ic, o_ref,
                 qin_s, att_s, kv_s, et_s, kvc_s, etc_s, cum_s, ko_s, qt_s, kt_s, *, nbl, nbc, unroll, out_blocks):
    cb = HG_BLOCK
    mid = cb // 2
    gb = HG_GROUP
    gr = gb * cb
    dk = HG_DK

    ri = lax.broadcasted_iota(jnp.int32, (gr, gr), 0)
    ci = lax.broadcasted_iota(jnp.int32, (gr, gr), 1)
    same = (ri // cb) == (ci // cb)
    rb = lax.broadcasted_iota(jnp.int32, (cb, cb), 0)
    cbi = lax.broadcasted_iota(jnp.int32, (cb, cb), 1)
    tri_l = jnp.where(same & (ri >= ci), 1.0, 0.0).astype(BF16)
    dirs = ((0, rb >= cbi, mid - 1, cb - 1), (1, rb <= cbi, mid, 0))

    def cumsums(lfs, slot):
        pre = _cumsum_mm(tri_l, jnp.concatenate(lfs, axis=-1))
        pre_b = pre[:, dk:].reshape(gb, cb, dk)
        cum_s[slot, 0] = pre[:, :dk]
        cum_s[slot, 1] = (pre_b[:, cb - 1:cb, :] - pre_b).reshape(gr, dk) + lfs[1]

    def operands(slot, kks, q, n0, et_ref):
        for d, keep, ref_row, tot_row in dirs:
            cum = cum_s[slot, d].reshape(gb, cb, dk)
            kk = kks[d].astype(F32).reshape(gb, cb, dk)
            ref = cum[:, ref_row:ref_row + 1, :]
            tot = cum[:, tot_row:tot_row + 1, :]
            e = cum - ref
            kt = kk * jnp.exp(-e)
            ko_s[slot, d] = (kt * jnp.exp(tot - ref)).astype(BF16).reshape(gr, dk)
            e_tot = jnp.exp(tot)
            for j in range(gb):
                et_ref[n0 + j, :, d * dk:(d + 1) * dk] = e_tot[j]
            if q is not None:
                qt = q.astype(F32).reshape(gb, cb, dk) * jnp.exp(e)
                qin_s[pl.ds(pl.multiple_of(n0 * cb, gr), gr), d * dk:(d + 1) * dk] = (
                    (qt * jnp.exp(ref)).astype(BF16).reshape(gr, dk))
                qt_s[slot, d] = qt.astype(BF16).reshape(gr, dk)
                kt_s[slot, d] = kt.astype(BF16).reshape(gr, dk)

    def matmuls(slot, v, with_q, n0, kv_ref):
        for d in range(2):
            for j in range(gb):
                rows = slice(j * cb, (j + 1) * cb)
                kv_ref[n0 + j, :, d * dk:(d + 1) * dk] = _dot_tn(v[rows], ko_s[slot, d, rows, :])
        if with_q:
            for j in range(gb):
                rows = slice(j * cb, (j + 1) * cb)
                att = (jnp.where(dirs[0][1], _dot_nt(qt_s[slot, 0, rows, :], kt_s[slot, 0, rows, :]), 0.0)
                       + jnp.where(dirs[1][1], _dot_nt(qt_s[slot, 1, rows, :], kt_s[slot, 1, rows, :]), 0.0))
                att_s[n0 + j] = att.astype(BF16)

    for g in range(nbc // gb):
        sl = slice(g * gr, (g + 1) * gr)
        cumsums((lffc[0, sl, :], lfbc[0, sl, :]), 0)
        operands(0, (kfc[0, sl, :], kbc[0, sl, :]), None, g * gb, etc_s)
        matmuls(0, ic[0, sl, :], False, g * gb, kvc_s)

    ngl = nbl // gb
    rows_of = lambda g: pl.ds(pl.multiple_of(g * gr, gr), gr)

    def lat_cumsums(g, slot):
        cumsums((lffl[0, rows_of(g), :], lfbl[0, rows_of(g), :]), slot)

    def lat_operands(g, slot):
        operands(slot, (kfl[0, rows_of(g), :], kbl[0, rows_of(g), :]), ql[0, rows_of(g), :], g * gb, et_s)

    def lat_matmuls(g, slot):
        matmuls(slot, il[0, rows_of(g), :], True, g * gb, kv_s)

    lat_cumsums(0, 0)
    lat_operands(0, 0)
    lat_cumsums(1, 1)

    def prep_body(i, carry):
        g = 2 * i
        lat_matmuls(g, 0)
        lat_operands(g + 1, 1)
        lat_cumsums(g + 2, 0)
        lat_matmuls(g + 1, 1)
        lat_operands(g + 2, 0)
        lat_cumsums(g + 3, 1)
        return carry

    lax.fori_loop(0, ngl // 2 - 1, prep_body, 0, unroll=True)
    lat_matmuls(ngl - 2, 0)
    lat_operands(ngl - 1, 1)
    lat_matmuls(ngl - 1, 1)

    lane = lax.broadcasted_iota(jnp.int32, (dk, 2 * dk), 1)
    is_f = lane < dk
    st = jnp.zeros((dk, 2 * dk), F32)
    for n in range(nbc):
        m = nbc - 1 - n
        st = (st * jnp.where(is_f[:1], etc_s[n], etc_s[m]) + jnp.where(is_f, kvc_s[n], kvc_s[m]))

    def rec_body(t, st):
        u = nbl - 1 - t
        inc = jnp.where(is_f, kv_s[t], kv_s[u])
        dec = jnp.where(is_f[:1], et_s[t], et_s[u])
        kv_s[t, :, 0:dk] = st[:, 0:dk]
        kv_s[u, :, dk:2 * dk] = st[:, dk:2 * dk]
        return st * dec + inc

    lax.fori_loop(0, nbl, rec_body, st, unroll=unroll)

    def out_body(i, carry):
        for j in range(out_blocks):
            n = i * out_blocks + j
            sl = pl.ds(pl.multiple_of(n * cb, cb), cb)
            o = _dot(att_s[n], il[0, sl, :]) + _dot_nt(qin_s[sl, :], kv_s[n].astype(BF16))
            o = _rms(o) * ng_ref[...] * sgl[0, sl, :].astype(F32)
            o_ref[0, sl, :] = o.astype(BF16)
        return carry

    lax.fori_loop(0, nbl // out_blocks, out_body, 0)


def _hgrn(norm_g, q_l, lff_l, kf_l, lfb_l, kb_l, i_l, sg_l, lff_c, kf_c, lfb_c, kb_c, i_c):
    b, n, _ = q_l.shape
    nc = lff_c.shape[1]
    nbl, nbc = n // HG_BLOCK, nc // HG_BLOCK
    out_blocks = min(64, nbl)
    assert nbl % (2 * HG_GROUP) == 0 and nbc % HG_GROUP == 0 and nbl % out_blocks == 0
    spec = lambda rows: pl.BlockSpec((1, rows, HG_DK), lambda i, h: (i, 0, h))
    slot = lambda dt: pltpu.VMEM((2, 2, HG_GROUP * HG_BLOCK, HG_DK), dt)
    return pl.pallas_call(
        functools.partial(_hgrn_kernel, nbl=nbl, nbc=nbc, unroll=2, out_blocks=out_blocks),
        out_shape=jax.ShapeDtypeStruct((b, n, D_MODEL), BF16),
        grid=(b, HG_HEADS),
        in_specs=[pl.BlockSpec((1, HG_DK), lambda i, h: (0, 0))] + [spec(n)] * 7 + [spec(nc)] * 5,
        out_specs=spec(n),
        scratch_shapes=[pltpu.VMEM((n, 2 * HG_DK), BF16),
                        pltpu.VMEM((nbl, HG_BLOCK, HG_BLOCK), BF16),
                        pltpu.VMEM((nbl, HG_DK, 2 * HG_DK), F32),
                        pltpu.VMEM((nbl, 1, 2 * HG_DK), F32),
                        pltpu.VMEM((nbc, HG_DK, 2 * HG_DK), F32),
                        pltpu.VMEM((nbc, 1, 2 * HG_DK), F32),
                        slot(F32), slot(BF16), slot(BF16), slot(BF16)],
        compiler_params=_cp("parallel", "parallel"),
        name="hgrn2",
    )(norm_g, q_l, lff_l, kf_l, lfb_l, kb_l, i_l, sg_l, lff_c, kf_c, lfb_c, kb_c, i_c)


def _rope_tables(n_tok):
    tok = jnp.arange(n_tok, dtype=jnp.int32)[:, None]
    row = (tok // GRID_W).astype(F32)
    col = (tok % GRID_W).astype(F32)
    n_freq = RET_DK // 4
    lane = jnp.arange(LANE, dtype=jnp.int32)[None, :]
    j = lane % (2 * n_freq)
    inv = ROPE_BASE ** (-(j % n_freq).astype(F32) / n_freq)
    ang = jnp.where(j < n_freq, row, col) * inv
    sign = jnp.where(lane % RET_DK < RET_DK // 2, -1.0, 1.0)
    return jnp.cos(ang), jnp.sin(ang) * sign


def kernel(x, c, ctx, c_ctx, w_mod, b_mod, norm_mix, norm_mlp, w_mlp_in, w_mlp_out, ab_w_in, ab_w_out, ret_logit, s5_a_re, s5_a_im, s5_log_dt, s5_b_re, s5_b_im, s5_c_re, s5_c_im, s5_d, s5_w_glu, s5_b_glu, hg_w_in, hg_w_out, hg_lb_logits, hg_norm, norm_final):
    b, n, d = x.shape
    nc = ctx.shape[1]
    assert d == D_MODEL and b + 1 <= MOD_ROWS and w_mod.shape[0] == DEPTH == 2
    assert n % 512 == 0 and nc % 256 == 0 and n % GRID_W == 0
    ctx_row = b
    tm_l, tm_c = 512, 256

    cc = jnp.zeros((MOD_ROWS, d), F32).at[:b].set(c).at[b].set(c_ctx)
    mod = _adaln(cc, w_mod, b_mod)

    row2 = lambda a: a.reshape(1, -1)
    w_in0 = ab_w_in[0]
    cos, sin = _rope_tables(n)
    ng0 = row2(norm_mix[0])
    q_l, k_l, v_l, u_l, uz_l, g_l = _inproj0(x, mod, 0, ng0, w_in0, cos, sin, None, tm_l)
    q_c, k_c, v_c, u_c, uz_c, g_c = _inproj0(ctx, mod, 0, ng0, w_in0, None, None, ctx_row, tm_c)

    log_gamma = jax.nn.log_sigmoid(ret_logit[0].astype(F32))
    lg_rows = jnp.broadcast_to(log_gamma.reshape(2 * RET_HEADS, 1), (2 * RET_HEADS, 2 * RET_DK))
    r_l, r_c = _retention(lg_rows, q_l, k_l, v_l, g_l, q_c, k_c, v_c, g_c)

    s5_ops = _s5_prep(s5_a_re[0], s5_a_im[0], s5_log_dt[0], s5_b_re[0], s5_b_im[0], s5_c_re[0], s5_c_im[0])
    y5_l, y5_c = _s5(s5_ops, uz_l, uz_c)

    s5p = (row2(s5_d[0]), s5_w_glu[0].astype(BF16), row2(s5_b_glu[0]))
    wo0 = ab_w_out[0].astype(BF16)
    w1_0, w2_0 = w_mlp_in, w_mlp_out
    nm0 = row2(norm_mlp[0])
    h_l = _mix_mlp(x, r_l, y5_l, u_l, s5p, wo0, mod, 0, nm0, w1_0, w2_0, None, None, tm_l, 1024, "mix_mlp0_lat")
    h_c = _mix_mlp(ctx, r_c, y5_c, u_c, s5p, wo0, mod, 0, nm0, w1_0, w2_0, None, ctx_row, tm_c, 1024, "mix_mlp0_ctx")

    w_in1 = hg_w_in[0]
    ng1 = row2(norm_mix[1])
    lat1 = _inproj1(h_l, mod, 1, ng1, hg_lb_logits, w_in1, True, None, tm_l, "inproj1_lat")
    ctx1 = _inproj1(h_c, mod, 1, ng1, hg_lb_logits, w_in1, False, ctx_row, tm_c, "inproj1_ctx")
    o1 = _hgrn(row2(hg_norm[0]), *lat1, *ctx1)
    return _mix_mlp(h_l, o1, None, None, None, hg_w_out[0].astype(BF16), mod, 1, row2(norm_mlp[1]),
                    w_mlp_in, w_mlp_out, row2(norm_final), None, tm_l, 1024,
                    "mix_mlp1_lat")
```

```python
import functools

import jax
import jax.numpy as jnp
from jax import lax
from jax.experimental import pallas as pl
from jax.experimental.pallas import tpu as pltpu

F32 = jnp.float32
BF16 = jnp.bfloat16

D_MODEL = 1024
DEPTH = 2
GRID_W = 64
EPS = 1e-6
ROPE_BASE = 10000.0
N_MOD = 6
RET_HEADS = 4
RET_DK = 64
RET_DV = 128
RET_QK = RET_HEADS * RET_DK
RET_WIDTH = RET_HEADS * RET_DV
RET_CHUNK = 128
S5_WIDTH = D_MODEL - RET_WIDTH
S5_GROUP = 16
S5_GROUPS = S5_WIDTH // S5_GROUP
S5_STATE = 64
S5_CHUNK = 16
S5_ROW = S5_CHUNK * S5_GROUP
LANE = 128
S5_LB = S5_WIDTH // LANE
S5_GPB = LANE // S5_GROUP
S5_BIG = S5_CHUNK * LANE
S5_HALF = S5_GPB * 2 * S5_STATE
S5_POW = 32
AB_IN = 2 * RET_QK + 2 * RET_WIDTH + S5_WIDTH
HG_HEADS = 8
HG_DK = D_MODEL // HG_HEADS
HG_BLOCK = 64
HG_GROUP = 4
HG_SPLIT = 2
D_FF = 4 * D_MODEL
MOD_ROWS = 16

VMEM_LIMIT_BYTES = 56 * 1024 * 1024


def _cp(*sem):
    return pltpu.CompilerParams(dimension_semantics=sem, vmem_limit_bytes=VMEM_LIMIT_BYTES)


def _dot(a, b):
    return jnp.dot(a, b, preferred_element_type=F32)


def _dot_nt(a, b):
    return lax.dot_general(a, b, (((1,), (1,)), ((), ())), preferred_element_type=F32)


def _dot_tn(a, b):
    return lax.dot_general(a, b, (((0,), (0,)), ((), ())), preferred_element_type=F32)


def _sigmoid(x):
    return 0.5 * jnp.tanh(0.5 * x) + 0.5


def _rms(x):
    return x * lax.rsqrt(jnp.mean(x * x, axis=-1, keepdims=True) + EPS)


def _mod_chunk(mod_ref, row, i):
    return mod_ref[0, pl.ds(row, 1), i * D_MODEL:(i + 1) * D_MODEL]


def _adaln_kernel(cc_ref, w_ref, b_ref, o_ref):
    s = jax.nn.silu(cc_ref[...]).astype(BF16)
    o_ref[0] = _dot(s, w_ref[0].astype(BF16)) + b_ref[0]


def _adaln(cc, w_mod, b_mod):
    bn = 1536
    n = N_MOD * D_MODEL
    return pl.pallas_call(
        _adaln_kernel,
        out_shape=jax.ShapeDtypeStruct((DEPTH, MOD_ROWS, n), F32),
        grid=(DEPTH, n // bn),
        in_specs=[
            pl.BlockSpec((MOD_ROWS, D_MODEL), lambda l, j: (0, 0)),
            pl.BlockSpec((1, D_MODEL, bn), lambda l, j: (l, 0, j)),
            pl.BlockSpec((1, 1, bn), lambda l, j: (l, 0, j)),
        ],
        out_specs=pl.BlockSpec((1, MOD_ROWS, bn), lambda l, j: (l, 0, j)),
        compiler_params=_cp("parallel", "parallel"),
        name="adaln",
    )(cc, w_mod, b_mod.reshape(DEPTH, 1, n))


def _rope(t, cos, sin):
    lane = lax.broadcasted_iota(jnp.int32, t.shape, 1)
    first = (lane & (RET_DK // 2)) == 0
    w = t.shape[1]
    swapped = jnp.where(first, pltpu.roll(t, w - RET_DK // 2, 1), pltpu.roll(t, RET_DK // 2, 1))
    return t * cos + swapped * sin


def _inproj0_kernel(*refs, mod_row, rope):
    if rope:
        h_ref, mod_ref, ng_ref, w_ref, cos_ref, sin_ref, q_ref, k_ref, v_ref, u_ref, uz_ref, g_ref, u_s = refs
    else:
        h_ref, mod_ref, ng_ref, w_ref, q_ref, k_ref, v_ref, u_ref, uz_ref, g_ref, u_s = refs
    row = pl.program_id(0) if mod_row is None else mod_row
    xn = _rms(h_ref[0]) * ng_ref[...]
    xm = (xn * (1.0 + _mod_chunk(mod_ref, row, 1)) + _mod_chunk(mod_ref, row, 0)).astype(BF16)
    y = _dot(xm, w_ref[...].astype(BF16))
    q = y[:, 0:RET_QK]
    k = y[:, RET_QK:2 * RET_QK]
    if rope:
        cos = jnp.concatenate([cos_ref[...]] * (RET_QK // LANE), axis=1)
        sin = jnp.concatenate([sin_ref[...]] * (RET_QK // LANE), axis=1)
        q = _rope(q, cos, sin)
        k = _rope(k, cos, sin)
    q_ref[0] = q.astype(BF16)
    k_ref[0] = (k * (RET_DK ** -0.5)).astype(BF16)
    c0 = 2 * RET_QK
    v_ref[0] = y[:, c0:c0 + RET_WIDTH].astype(BF16)
    u0 = c0 + RET_WIDTH
    u_ref[0] = y[:, u0:u0 + S5_WIDTH]
    g = y[:, u0 + S5_WIDTH:]
    g_ref[0] = (g * _sigmoid(g)).astype(BF16)
    nch = u_s.shape[1] // S5_CHUNK
    for j in range(S5_LB):
        u_s[j] = y[:, u0 + j * LANE:u0 + (j + 1) * LANE]
        for s in range(S5_CHUNK):
            uz_ref[0, j, :, s * LANE:(s + 1) * LANE] = u_s[j, pl.ds(s, nch, stride=S5_CHUNK), :].astype(BF16)


def _inproj0(h, mod, layer, ng, w, cos, sin, mod_row, tm):
    b, n, _ = h.shape
    rope = cos is not None
    row_spec = lambda width: pl.BlockSpec((1, tm, width), lambda i, j: (i, j, 0))
    in_specs = [
        row_spec(D_MODEL),
        pl.BlockSpec((1, MOD_ROWS, N_MOD * D_MODEL), lambda i, j: (layer, 0, 0)),
        pl.BlockSpec((1, D_MODEL), lambda i, j: (0, 0)),
        pl.BlockSpec((D_MODEL, AB_IN), lambda i, j: (0, 0), pipeline_mode=pl.Buffered(1)),
    ]
    args = [h, mod, ng, w]
    if rope:
        in_specs += [pl.BlockSpec((tm, LANE), lambda i, j: (j, 0))] * 2
        args += [cos, sin]
    widths = (RET_QK, RET_QK, RET_WIDTH, S5_WIDTH, RET_WIDTH)
    dtypes = (BF16, BF16, BF16, F32, BF16)
    out_shape = [jax.ShapeDtypeStruct((b, n, wd), dt) for wd, dt in zip(widths, dtypes)]
    out_specs = [row_spec(wd) for wd in widths]
    out_shape.insert(4, jax.ShapeDtypeStruct((b, S5_LB, n // S5_CHUNK, S5_BIG), BF16))
    out_specs.insert(4, pl.BlockSpec((1, S5_LB, tm // S5_CHUNK, S5_BIG), lambda i, j: (i, 0, j, 0)))
    return pl.pallas_call(
        functools.partial(_inproj0_kernel, mod_row=mod_row, rope=rope),
        out_shape=out_shape,
        grid=(b, n // tm),
        in_specs=in_specs,
        out_specs=out_specs,
        scratch_shapes=[pltpu.VMEM((S5_LB, tm, LANE), F32)],
        compiler_params=_cp("parallel", "parallel"),
        name="inproj0_lat" if rope else "inproj0_ctx",
    )(*args)


def _ret_kernel(lg_ref, ql, kl, vl, gl, qc, kc, vc, gc, rl, rc, st_s, *, ncl, ncc, unroll):
    c = RET_CHUNK
    dk2 = 2 * RET_DK
    nt = ncc + ncl
    p = pl.program_id(1)
    h_a = 2 * p
    lgf_a = lg_ref[pl.ds(h_a, 1), :]
    lgf_b = lg_ref[pl.ds(h_a + 1, 1), :]
    lgb_a = lg_ref[pl.ds(RET_HEADS + h_a, 1), :]
    lgb_b = lg_ref[pl.ds(RET_HEADS + h_a + 1, 1), :]
    lane = lax.broadcasted_iota(jnp.int32, (1, 2 * RET_DK), 1)
    is_a = lane < RET_DK
    lgf_lane = jnp.where(is_a, lgf_a, lgf_b)
    lgb_lane = jnp.where(is_a, lgb_a, lgb_b)
    ri = lax.broadcasted_iota(jnp.int32, (c, c), 0).astype(F32)
    ci = lax.broadcasted_iota(jnp.int32, (c, c), 1).astype(F32)
    diff = ri - ci

    def dmat(lgf, lgb):
        fwd = jnp.exp(jnp.maximum(diff, 0.0) * lgf)
        bwd = jnp.exp(jnp.maximum(-diff, 0.0) * lgb)
        return jnp.where(diff > 0, fwd, jnp.where(diff < 0, bwd, 2.0))

    d_a = dmat(lgf_a, lgb_a)
    d_b = dmat(lgf_b, lgb_b)
    rowp = lax.broadcasted_iota(jnp.int32, (c, dk2), 0).astype(F32)
    qd = jnp.concatenate([jnp.exp((rowp + 1.0) * lgf_lane), jnp.exp((c - rowp) * lgb_lane)], axis=1)
    kd = jnp.concatenate([jnp.exp((c - 1.0 - rowp) * lgf_lane), jnp.exp(rowp * lgb_lane)], axis=1)
    rowk = lax.broadcasted_iota(jnp.int32, (dk2, 2 * RET_DV), 0)
    cd_f = jnp.exp(c * jnp.where(rowk < RET_DK, lgf_a[:, :1], lgf_b[:, :1]))
    cd_b = jnp.exp(c * jnp.where(rowk < RET_DK, lgb_a[:, :1], lgb_b[:, :1]))
    mask2 = jnp.concatenate([is_a, is_a], axis=1)

    def increment(k, v, slot):
        kk = jnp.concatenate([k, k], axis=1).astype(F32) * kd
        st_s[slot] = _dot_tn(kk.astype(BF16), v)

    for n in range(ncc):
        increment(kc[0, n * c:(n + 1) * c, :], vc[0, n * c:(n + 1) * c, :], n)

    def inc_body(i, carry):
        for j in range(unroll):
            n = i * unroll + j
            sl = pl.ds(pl.multiple_of(n * c, c), c)
            increment(kl[0, sl, :], vl[0, sl, :], ncc + n)
        return carry

    lax.fori_loop(0, ncl // unroll, inc_body, 0)

    def rec_body(t, carry):
        sf, sb = carry
        u = jnp.where(t < ncc, ncc - 1 - t, nt - 1 - (t - ncc))
        inc_f = st_s[t, 0:dk2, :]
        inc_b = st_s[u, dk2:2 * dk2, :]
        st_s[t, 0:dk2, :] = sf
        st_s[u, dk2:2 * dk2, :] = sb
        return cd_f * sf + inc_f, cd_b * sb + inc_b

    zero = jnp.zeros((dk2, 2 * RET_DV), F32)
    lax.fori_loop(0, nt, rec_body, (zero, zero), unroll=True)

    def output(q, k, v, g, slot, out_ref, st):
        q2 = (jnp.concatenate([q, q], axis=1).astype(F32) * qd).astype(BF16)
        s_n = st_s[slot].astype(BF16)
        for keep_a, dm, cs in ((True, d_a, 0), (False, d_b, RET_DV)):
            m1 = is_a if keep_a else jnp.logical_not(is_a)
            m2 = mask2 if keep_a else jnp.logical_not(mask2)
            att = _dot_nt(jnp.where(m1, q, jnp.zeros_like(q)), k) * dm
            o = (_dot(att.astype(BF16), v[:, cs:cs + RET_DV])
                 + _dot(jnp.where(m2, q2, jnp.zeros_like(q2)), s_n[:, cs:cs + RET_DV]))
            o = _rms(o) * g[:, cs:cs + RET_DV].astype(F32)
            out_ref[0, pl.ds(st, c), cs:cs + RET_DV] = o.astype(BF16)

    for n in range(ncc):
        sl = slice(n * c, (n + 1) * c)
        output(qc[0, sl, :], kc[0, sl, :], vc[0, sl, :], gc[0, sl, :], n, rc, n * c)

    def out_body(i, carry):
        for j in range(unroll):
            n = i * unroll + j
            st = pl.multiple_of(n * c, c)
            sl = pl.ds(st, c)
            output(ql[0, sl, :], kl[0, sl, :], vl[0, sl, :], gl[0, sl, :], ncc + n, rl, st)
        return carry

    lax.fori_loop(0, ncl // unroll, out_body, 0)


def _retention(lg_rows, q_l, k_l, v_l, g_l, q_c, k_c, v_c, g_c):
    b, n, _ = q_l.shape
    nc = q_c.shape[1]
    ncl, ncc = n // RET_CHUNK, nc // RET_CHUNK
    pairs = RET_HEADS // 2
    unroll = ncl

    def spec(rows, width):
        return pl.BlockSpec((1, rows, width), lambda i, p: (i, 0, p))

    return pl.pallas_call(
        functools.partial(_ret_kernel, ncl=ncl, ncc=ncc, unroll=unroll),
        out_shape=[jax.ShapeDtypeStruct((b, n, RET_WIDTH), BF16),
                   jax.ShapeDtypeStruct((b, nc, RET_WIDTH), BF16)],
        grid=(b, pairs),
        in_specs=[pl.BlockSpec((2 * RET_HEADS, 2 * RET_DK), lambda i, p: (0, 0)),
                  spec(n, 2 * RET_DK), spec(n, 2 * RET_DK), spec(n, 2 * RET_DV), spec(n, 2 * RET_DV),
                  spec(nc, 2 * RET_DK), spec(nc, 2 * RET_DK), spec(nc, 2 * RET_DV), spec(nc, 2 * RET_DV)],
        out_specs=[spec(n, 2 * RET_DV), spec(nc, 2 * RET_DV)],
        scratch_shapes=[pltpu.VMEM((ncl + ncc, 4 * RET_DK, 2 * RET_DV), F32)],
        compiler_params=_cp("parallel", "parallel"),
        name="retention",
    )(lg_rows, q_l, k_l, v_l, g_l, q_c, k_c, v_c, g_c)


def _dot_hi(a, b, contract=(1, 0)):
    dims = (((contract[0],), (contract[1],)), ((), ()))
    return lax.dot_general(a, b, dims, preferred_element_type=F32, precision=lax.Precision.HIGHEST)


def _dot_sel(a, b, contract, data):
    dims = (((contract[0],), (contract[1],)), ((), ()))
    x = (a, b)[data]
    hi = x.astype(BF16)
    lo = (x - hi.astype(F32)).astype(BF16)
    dd = lambda piece: lax.dot_general(*((piece, b) if data == 0 else (a, piece)), dims, preferred_element_type=F32)
    return dd(hi) + dd(lo)


def _s5_prep_kernel(ar_row, ai_row, ldt, btr, bti, ctr, cti, rm, rwin, rwout, abig):
    t, g, p, kp = S5_CHUNK, S5_GROUP, S5_STATE, S5_POW
    row = S5_ROW
    i0 = lambda shape: lax.broadcasted_iota(jnp.int32, shape, 0)
    i1 = lambda shape: lax.broadcasted_iota(jnp.int32, shape, 1)
    f32 = lambda m: jnp.where(m, 1.0, 0.0).astype(BF16)
    s_of_r = i0((row, kp)) // g
    k_of_l = i1((row, kp))
    sel_rows = (f32(k_of_l == t - 1 - s_of_r), f32(k_of_l == s_of_r))
    t_of_c = i1((kp, row)) // g
    k_of_s = i0((kp, row))
    sel_out = (f32(k_of_s == t_of_c + 1), f32(k_of_s == t - t_of_c))
    sel_lag = (f32(k_of_s == t_of_c), f32(k_of_s == t - 1 - t_of_c))
    tile_l = f32(i1((g, row)) % g == i0((g, row)))
    tile_r = f32(i0((row, g)) % g == i1((row, g)))
    lane = i1((g, row))
    k_col = i0((kp, 1)).astype(F32)
    k_row = i1((1, kp)).astype(F32)
    first = i0((8, 1)) == 0

    def outer(a, k):
        a8 = jnp.where(first, jnp.broadcast_to(a, (8, a.shape[1])), 0.0)
        return _dot_hi(a8, jnp.broadcast_to(k, (8, k.shape[1])), (0, 0))

    for q in range(S5_GPB):
        lags = []
        for d in range(2):
            dt = jnp.exp(ldt[d, q])
            are_r, aim_r = ar_row[d, q], ai_row[d, q]
            mag = jnp.exp(are_r * dt)
            ang = aim_r * dt
            nr, ni = mag * jnp.cos(ang) - 1.0, mag * jnp.sin(ang)
            den = jnp.square(are_r) + jnp.square(aim_r)
            fr = (nr * are_r + ni * aim_r) / den
            fi = (ni * are_r - nr * aim_r) / den
            pm = jnp.exp(k_col * (are_r * dt))
            pa = k_col * ang
            pk_re, pk_im = pm * jnp.cos(pa), pm * jnp.sin(pa)
            pmt = jnp.exp(outer(are_r * dt, k_row))
            pat = outer(ang, k_row)
            pt_re, pt_im = pmt * jnp.cos(pat), pmt * jnp.sin(pat)
            bt_re, bt_im = _dot_sel(tile_r, btr[d, q], (1, 1), 1), _dot_sel(tile_r, bti[d, q], (1, 1), 1)
            bb_re = fr * bt_re - fi * bt_im
            bb_im = fr * bt_im + fi * bt_re
            pr_re, pr_im = _dot_sel(sel_rows[d], pk_re, (1, 0), 1), _dot_sel(sel_rows[d], pk_im, (1, 0), 1)
            w_re = pr_re * bb_re - pr_im * bb_im
            w_im = pr_re * bb_im + pr_im * bb_re
            for s in range(t):
                rows = slice(s * LANE + q * g, s * LANE + (q + 1) * g)
                rwin[0, rows, d * p:(d + 1) * p] = w_re[s * g:(s + 1) * g].astype(BF16)
                rwin[0, rows, (2 + d) * p:(3 + d) * p] = w_im[s * g:(s + 1) * g].astype(BF16)
            ct_re, ct_im = _dot_sel(ctr[d, q], tile_l, (0, 0), 0), _dot_sel(cti[d, q], tile_l, (0, 0), 0)

            def c_pow(sel):
                pc_re, pc_im = _dot_sel(pt_re, sel, (1, 0), 0), _dot_sel(pt_im, sel, (1, 0), 0)
                return ct_re * pc_re - ct_im * pc_im, ct_re * pc_im + ct_im * pc_re

            o_re, o_im = c_pow(sel_out[d])
            r0 = d * S5_GPB * p + q * p
            rwout[0, r0:r0 + p, :] = o_re.astype(BF16)
            rwout[0, S5_HALF + r0:S5_HALF + r0 + p, :] = (-o_im).astype(BF16)
            l_re, l_im = c_pow(sel_lag[d])
            lags.append(_dot_hi(bb_re[0:g], l_re) - _dot_hi(bb_im[0:g], l_im))
            abig[0, 0:1, r0:r0 + p] = pk_re[t:t + 1, :]
            abig[0, 1:2, r0:r0 + p] = pk_im[t:t + 1, :]
        for s in range(t):
            fwd = jnp.where(lane >= g * s, pltpu.roll(lags[0], g * s, 1), 0.0)
            bwd = jnp.where(lane < g * (s + 1), pltpu.roll(lags[1], (row - g * (t - 1 - s)) % row, 1), 0.0)
            rm[0, s * LANE + q * g:s * LANE + (q + 1) * g, :] = (fwd + bwd).astype(BF16)


def _s5_prep(a_re, a_im, log_dt, b_re, b_im, c_re, c_im):
    gg, p, g = S5_GROUPS, S5_STATE, S5_GROUP
    f = lambda x: x.astype(F32)
    args = (f(a_re).reshape(2, gg, 1, p), f(a_im).reshape(2, gg, 1, p),
            f(log_dt).reshape(2, gg, 1, 1),
            f(b_re), f(b_im), f(c_re), f(c_im))
    spec = lambda r, c: pl.BlockSpec((2, S5_GPB, r, c), lambda j: (0, j, 0, 0))
    out = lambda r, c: pl.BlockSpec((1, r, c), lambda j: (j, 0, 0))
    return pl.pallas_call(
        _s5_prep_kernel,
        out_shape=[jax.ShapeDtypeStruct((S5_LB, S5_BIG, S5_ROW), BF16),
                   jax.ShapeDtypeStruct((S5_LB, S5_BIG, S5_ROW), BF16),
                   jax.ShapeDtypeStruct((S5_LB, 2 * S5_HALF, S5_ROW), BF16),
                   jax.ShapeDtypeStruct((S5_LB, 2, S5_HALF), F32)],
        grid=(S5_LB,),
        in_specs=[spec(1, p), spec(1, p), spec(1, 1),
                  spec(p, g), spec(p, g), spec(g, p), spec(g, p)],
        out_specs=[out(S5_BIG, S5_ROW), out(S5_BIG, S5_ROW), out(2 * S5_HALF, S5_ROW), out(2, S5_HALF)],
        compiler_params=_cp("parallel"),
        name="s5_prep",
    )(*args)


def _s5_expand(r_ref, col_unit, col_block, row_unit):
    n = S5_BIG
    a = lax.broadcasted_iota(jnp.int32, (S5_ROW, n), 0)
    c = lax.broadcasted_iota(jnp.int32, (S5_ROW, n), 1)
    e = jnp.where((a // col_unit == c // col_block) & (a % col_unit == c % col_unit), 1.0, 0.0).astype(BF16)
    x = _dot(r_ref[0], e)
    rq = (lax.broadcasted_iota(jnp.int32, (n, n), 0) // row_unit) % S5_GPB
    cq = (lax.broadcasted_iota(jnp.int32, (n, n), 1) // col_unit) % S5_GPB
    return jnp.where(rq == cq, x, 0.0).astype(BF16)


def _s5_kernel(ul, uc, rm_ref, rwin_ref, rwout_ref, a_ref, yl, yc, x_s, m_s, win_s, wout_s, *, ncl, ncc):
    hw = S5_HALF
    hh = hw // 2

    @pl.when(pl.program_id(1) == 0)
    def _():
        m_s[...] = _s5_expand(rm_ref, S5_GROUP, LANE, S5_GROUP)
        win_s[...] = _s5_expand(rwin_ref, S5_STATE, S5_GPB * S5_STATE, S5_GROUP)
        wout_s[...] = _s5_expand(rwout_ref, S5_GROUP, LANE, S5_STATE)

    z = jnp.concatenate([uc[0, 0], ul[0, 0]], axis=0)
    x_s[...] = _dot(z, win_s[...])
    a_re = a_ref[0, 0:1, :]
    a_im = a_ref[0, 1:2, :]

    def segment(base, n, carry):
        s_re, s_im = carry
        for i in range(n):
            rf = slice(base + i, base + i + 1)
            rb = slice(base + n - 1 - i, base + n - i)
            x_re = jnp.concatenate([x_s[rf, 0:hh], x_s[rb, hh:hw]], axis=-1)
            x_im = jnp.concatenate([x_s[rf, hw:hw + hh], x_s[rb, hw + hh:2 * hw]], axis=-1)
            x_s[rf, 0:hh] = s_re[:, 0:hh]
            x_s[rb, hh:hw] = s_re[:, hh:hw]
            x_s[rf, hw:hw + hh] = s_im[:, 0:hh]
            x_s[rb, hw + hh:2 * hw] = s_im[:, hh:hw]
            s_re, s_im = a_re * s_re - a_im * s_im + x_re, a_re * s_im + a_im * s_re + x_im
        return s_re, s_im

    zero = jnp.zeros((1, hw), F32)
    carry = segment(0, ncc, (zero, zero))
    segment(ncc, ncl, carry)
    y = _dot(z, m_s[...]) + _dot(x_s[...].astype(BF16), wout_s[...])
    yc[0, 0] = y[0:ncc]
    yl[0, 0] = y[ncc:ncc + ncl]


def _s5(prep, zl, zc):
    rm, rwin, rwout, a_big = prep
    b, _, ncl, _ = zl.shape
    ncc = zc.shape[2]
    rows = lambda r: pl.BlockSpec((1, 1, r, S5_BIG), lambda j, i: (i, j, 0, 0))
    wspec = lambda r, c: pl.BlockSpec((1, r, c), lambda j, i: (j, 0, 0))
    big = pltpu.VMEM((S5_BIG, S5_BIG), BF16)
    return pl.pallas_call(
        functools.partial(_s5_kernel, ncl=ncl, ncc=ncc),
        out_shape=[jax.ShapeDtypeStruct(zl.shape, F32), jax.ShapeDtypeStruct(zc.shape, F32)],
        grid=(S5_LB, b),
        in_specs=[rows(ncl), rows(ncc), wspec(S5_BIG, S5_ROW), wspec(S5_BIG, S5_ROW),
                  wspec(2 * S5_HALF, S5_ROW), wspec(2, S5_HALF)],
        out_specs=[rows(ncl), rows(ncc)],
        scratch_shapes=[pltpu.VMEM((ncc + ncl, 2 * S5_HALF), F32), big, big, big],
        compiler_params=_cp("parallel", "arbitrary"),
        name="s5",
    )(zl, zc, rm, rwin, rwout, a_big)


def _mix_mlp_kernel(*refs, mod_row, s5_merge, final_norm, fb):
    if s5_merge:
        (h_ref, r_ref, y5_ref, u_ref, ds_ref, wg_ref, bg_ref, wo_ref, mod_ref, nm_ref, w1_ref, w2_ref,
         *rest) = refs
    else:
        h_ref, r_ref, wo_ref, mod_ref, nm_ref, w1_ref, w2_ref, *rest = refs
    if final_norm:
        nf_ref, o_ref, *scratch = rest
    else:
        o_ref, *scratch = rest
    row = pl.program_id(0) if mod_row is None else mod_row
    if s5_merge:
        (y_s,) = scratch
        nch = y_s.shape[1] // S5_CHUNK
        for lb in range(S5_LB):
            for s in range(S5_CHUNK):
                y_s[lb, pl.ds(s, nch, stride=S5_CHUNK), :] = y5_ref[0, lb, :, s * LANE:(s + 1) * LANE]
        y5 = jnp.concatenate([y_s[lb] for lb in range(S5_LB)], axis=-1)
        y = jax.nn.gelu(y5 + ds_ref[...] * u_ref[0])
        y = y * jax.nn.sigmoid(_dot(y.astype(BF16), wg_ref[...]) + bg_ref[...])
        mix = _dot(r_ref[0], wo_ref[0:RET_WIDTH, :]) + _dot(y.astype(BF16), wo_ref[RET_WIDTH:D_MODEL, :])
    else:
        mix = _dot(r_ref[0], wo_ref[...])
    h1 = h_ref[0] + _mod_chunk(mod_ref, row, 2) * mix
    xn = _rms(h1) * nm_ref[...]
    xm = (xn * (1.0 + _mod_chunk(mod_ref, row, 4)) + _mod_chunk(mod_ref, row, 3)).astype(BF16)
    acc = None
    for j in range(D_FF // fb):
        a = jnp.square(jnp.maximum(_dot(xm, w1_ref[:, j * fb:(j + 1) * fb].astype(BF16)), 0.0)).astype(BF16)
        part = _dot(a, w2_ref[j * fb:(j + 1) * fb, :].astype(BF16))
        acc = part if acc is None else acc + part
    h2 = h1 + _mod_chunk(mod_ref, row, 5) * acc
    if final_norm:
        h2 = _rms(h2) * nf_ref[...]
    o_ref[0] = h2


def _mix_mlp(h, r, s5y, u, s5p, wo, mod, layer, nm, w1, w2, nf, mod_row, tm, fb, name):
    b, n, _ = h.shape
    s5_merge = s5y is not None
    final_norm = nf is not None
    one = pl.Buffered(1)
    row_spec = lambda width: pl.BlockSpec((1, tm, width), lambda i, t: (i, t, 0))
    const = lambda shape: pl.BlockSpec(shape, lambda i, t: (0,) * len(shape), pipeline_mode=one)
    in_specs = [row_spec(D_MODEL), row_spec(r.shape[-1])]
    args = [h, r]
    if s5_merge:
        d_skip, w_glu, b_glu = s5p
        z_spec = pl.BlockSpec((1, S5_LB, tm // S5_CHUNK, S5_BIG), lambda i, t: (i, 0, t, 0))
        in_specs += [z_spec, row_spec(S5_WIDTH), const((1, S5_WIDTH)),
                     const((S5_WIDTH, S5_WIDTH)), const((1, S5_WIDTH))]
        args += [s5y, u, d_skip, w_glu, b_glu]
    in_specs += [const((D_MODEL, D_MODEL)),
                 pl.BlockSpec((1, MOD_ROWS, N_MOD * D_MODEL), lambda i, t: (layer, 0, 0), pipeline_mode=one),
                 const((1, D_MODEL))]
    if w1.ndim == 3:
        in_specs += [pl.BlockSpec((None, D_MODEL, D_FF), lambda i, t: (layer, 0, 0), pipeline_mode=one),
                     pl.BlockSpec((None, D_FF, D_MODEL), lambda i, t: (layer, 0, 0), pipeline_mode=one)]
    else:
        in_specs += [const((D_MODEL, D_FF)), const((D_FF, D_MODEL))]
    args += [wo, mod, nm, w1, w2]
    if final_norm:
        in_specs.append(const((1, D_MODEL)))
        args.append(nf)
    return pl.pallas_call(
        functools.partial(_mix_mlp_kernel, mod_row=mod_row, s5_merge=s5_merge, final_norm=final_norm, fb=fb),
        out_shape=jax.ShapeDtypeStruct((b, n, D_MODEL), F32),
        grid=(b, n // tm),
        in_specs=in_specs,
        out_specs=row_spec(D_MODEL),
        scratch_shapes=[pltpu.VMEM((S5_LB, tm, LANE), F32)] if s5_merge else [],
        compiler_params=_cp("parallel", "parallel"),
        name=name,
    )(*args)


def _hgrn_lower_bounds(lbl_ref, layer):
    out = []
    for d in range(2):
        z = [lbl_ref[d, k:k + 1, :] for k in range(DEPTH)]
        zmax = functools.reduce(jnp.maximum, z)
        e = [jnp.exp(v - zmax) for v in z]
        tot = functools.reduce(lambda a, b_: a + b_, e)
        lb = jnp.zeros_like(tot)
        for k in range(1, layer + 1):
            lb = lb + e[k] / tot
        out.append(lb)
    return out


def _inproj1_kernel(h_ref, mod_ref, ng_ref, lbl_ref, w_ref, *out_refs, mod_row, layer, latent):
    row = pl.program_id(0) if mod_row is None else mod_row
    xn = _rms(h_ref[0]) * ng_ref[...]
    xm = (xn * (1.0 + _mod_chunk(mod_ref, row, 1)) + _mod_chunk(mod_ref, row, 0)).astype(BF16)
    col = lambda k: _dot(xm, w_ref[:, k * D_MODEL:(k + 1) * D_MODEL].astype(BF16))
    lbs = _hgrn_lower_bounds(lbl_ref, layer)
    refs = list(out_refs)
    if latent:
        refs.pop(0)[0] = col(0).astype(BF16)
    for d in range(2):
        t = (1.0 - lbs[d]) * _sigmoid(col(1 + d))
        refs.pop(0)[0] = jnp.log(lbs[d] + t)
        refs.pop(0)[0] = ((1.0 - lbs[d]) - t).astype(BF16)
    refs.pop(0)[0] = col(3).astype(BF16)
    if latent:
        g = col(4)
        refs.pop(0)[0] = (g * _sigmoid(g)).astype(BF16)


def _inproj1(h, mod, layer, ng, lb_logits, w, latent, mod_row, tm, name):
    b, n, _ = h.shape
    one = pl.Buffered(1)
    row_spec = pl.BlockSpec((1, tm, D_MODEL), lambda i, t: (i, t, 0))
    dtypes = ([BF16] if latent else []) + [F32, BF16, F32, BF16, BF16] + ([BF16] if latent else [])
    return pl.pallas_call(
        functools.partial(_inproj1_kernel, mod_row=mod_row, layer=layer, latent=latent),
        out_shape=[jax.ShapeDtypeStruct((b, n, D_MODEL), dt) for dt in dtypes],
        grid=(b, n // tm),
        in_specs=[row_spec,
                  pl.BlockSpec((1, MOD_ROWS, N_MOD * D_MODEL), lambda i, t: (layer, 0, 0), pipeline_mode=one),
                  pl.BlockSpec((1, D_MODEL), lambda i, t: (0, 0), pipeline_mode=one),
                  pl.BlockSpec(lb_logits.shape, lambda i, t: (0, 0, 0), pipeline_mode=one),
                  pl.BlockSpec(w.shape, lambda i, t: (0, 0), pipeline_mode=one)],
        out_specs=[row_spec] * len(dtypes),
        compiler_params=_cp("parallel", "parallel"),
        name=name,
    )(h, mod, ng, lb_logits, w)


def _cumsum_mm(tri, x):
    acc = None
    r = x
    for i in range(HG_SPLIT):
        p = r.astype(BF16)
        acc = _dot(tri, p) if acc is None else acc + _dot(tri, p)
        if i + 1 < HG_SPLIT:
            r = r - p.astype(F32)
    return acc


def _hgrn_kernel(ng_ref, ql, lffl, kfl, lfbl, kbl, il, sgl, lffc, kfc, lfbc, kbc, ic, o_ref,
                 qin_s, att_s, kv_s, et_s, kvc_s, etc_s, cum_s, ko_s, qt_s, kt_s, *, nbl, nbc, out_blocks):
    cb = HG_BLOCK
    mid = cb // 2
    gb = HG_GROUP
    gr = gb * cb
    dk = HG_DK

    ri = lax.broadcasted_iota(jnp.int32, (gr, gr), 0)
    ci = lax.broadcasted_iota(jnp.int32, (gr, gr), 1)
    same = (ri // cb) == (ci // cb)
    rb = lax.broadcasted_iota(jnp.int32, (cb, cb), 0)
    cbi = lax.broadcasted_iota(jnp.int32, (cb, cb), 1)
    tri_l = jnp.where(same & (ri >= ci), 1.0, 0.0).astype(BF16)
    dirs = ((0, rb >= cbi, mid - 1, cb - 1), (1, rb <= cbi, mid, 0))

    def cumsums(lfs, slot):
        pre = _cumsum_mm(tri_l, jnp.concatenate(lfs, axis=-1))
        pre_b = pre[:, dk:].reshape(gb, cb, dk)
        cum_s[slot, 0] = pre[:, :dk]
        cum_s[slot, 1] = (pre_b[:, cb - 1:cb, :] - pre_b).reshape(gr, dk) + lfs[1]

    def operands(slot, kks, q, n0, et_ref):
        for d, keep, ref_row, tot_row in dirs:
            cum = cum_s[slot, d].reshape(gb, cb, dk)
            kk = kks[d].astype(F32).reshape(gb, cb, dk)
            ref = cum[:, ref_row:ref_row + 1, :]
            tot = cum[:, tot_row:tot_row + 1, :]
            e = cum - ref
            kt = kk * jnp.exp(-e)
            ko_s[slot, d] = (kt * jnp.exp(tot - ref)).astype(BF16).reshape(gr, dk)
            e_tot = jnp.exp(tot)
            for j in range(gb):
                et_ref[n0 + j, :, d * dk:(d + 1) * dk] = e_tot[j]
            if q is not None:
                qt = q.astype(F32).reshape(gb, cb, dk) * jnp.exp(e)
                qin_s[pl.ds(pl.multiple_of(n0 * cb, gr), gr), d * dk:(d + 1) * dk] = (
                    (qt * jnp.exp(ref)).astype(BF16).reshape(gr, dk))
                qt_s[slot, d] = qt.astype(BF16).reshape(gr, dk)
                kt_s[slot, d] = kt.astype(BF16).reshape(gr, dk)

    def matmuls(slot, v, with_q, n0, kv_ref):
        for d in range(2):
            for j in range(gb):
                rows = slice(j * cb, (j + 1) * cb)
                kv_ref[n0 + j, :, d * dk:(d + 1) * dk] = _dot_tn(v[rows], ko_s[slot, d, rows, :])
        if with_q:
            for j in range(gb):
                rows = slice(j * cb, (j + 1) * cb)
                att = (jnp.where(dirs[0][1], _dot_nt(qt_s[slot, 0, rows, :], kt_s[slot, 0, rows, :]), 0.0)
                       + jnp.where(dirs[1][1], _dot_nt(qt_s[slot, 1, rows, :], kt_s[slot, 1, rows, :]), 0.0))
                att_s[n0 + j] = att.astype(BF16)

    for g in range(nbc // gb):
        sl = slice(g * gr, (g + 1) * gr)
        cumsums((lffc[0, sl, :], lfbc[0, sl, :]), 0)
        operands(0, (kfc[0, sl, :], kbc[0, sl, :]), None, g * gb, etc_s)
        matmuls(0, ic[0, sl, :], False, g * gb, kvc_s)

    ngl = nbl // gb
    rows_of = lambda g: pl.ds(pl.multiple_of(g * gr, gr), gr)

    def lat_cumsums(g, slot):
        cumsums((lffl[0, rows_of(g), :], lfbl[0, rows_of(g), :]), slot)

    def lat_operands(g, slot):
        operands(slot, (kfl[0, rows_of(g), :], kbl[0, rows_of(g), :]), ql[0, rows_of(g), :], g * gb, et_s)

    def lat_matmuls(g, slot):
        matmuls(slot, il[0, rows_of(g), :], True, g * gb, kv_s)

    lat_cumsums(0, 0)
    lat_operands(0, 0)
    lat_cumsums(1, 1)

    def prep_body(i, carry):
        g = 2 * i
        lat_matmuls(g, 0)
        lat_operands(g + 1, 1)
        lat_cumsums(g + 2, 0)
        lat_matmuls(g + 1, 1)
        lat_operands(g + 2, 0)
        lat_cumsums(g + 3, 1)
        return carry

    lax.fori_loop(0, ngl // 2 - 1, prep_body, 0, unroll=True)
    lat_matmuls(ngl - 2, 0)
    lat_operands(ngl - 1, 1)
    lat_matmuls(ngl - 1, 1)

    lane = lax.broadcasted_iota(jnp.int32, (dk, 2 * dk), 1)
    is_f = lane < dk
    st = jnp.zeros((dk, 2 * dk), F32)
    for n in range(nbc):
        m = nbc - 1 - n
        st = (st * jnp.where(is_f[:1], etc_s[n], etc_s[m]) + jnp.where(is_f, kvc_s[n], kvc_s[m]))

    def rec_body(t, st):
        u = nbl - 1 - t
        inc = jnp.where(is_f, kv_s[t], kv_s[u])
        dec = jnp.where(is_f[:1], et_s[t], et_s[u])
        kv_s[t, :, 0:dk] = st[:, 0:dk]
        kv_s[u, :, dk:2 * dk] = st[:, dk:2 * dk]
        return st * dec + inc

    lax.fori_loop(0, nbl, rec_body, st, unroll=True)

    def out_body(i, carry):
        for j in range(out_blocks):
            n = i * out_blocks + j
            sl = pl.ds(pl.multiple_of(n * cb, cb), cb)
            o = _dot(att_s[n], il[0, sl, :]) + _dot_nt(qin_s[sl, :], kv_s[n].astype(BF16))
            o = _rms(o) * ng_ref[...] * sgl[0, sl, :].astype(F32)
            o_ref[0, sl, :] = o.astype(BF16)
        return carry

    lax.fori_loop(0, nbl // out_blocks, out_body, 0)


def _hgrn(norm_g, q_l, lff_l, kf_l, lfb_l, kb_l, i_l, sg_l, lff_c, kf_c, lfb_c, kb_c, i_c):
    b, n, _ = q_l.shape
    nc = lff_c.shape[1]
    nbl, nbc = n // HG_BLOCK, nc // HG_BLOCK
    out_blocks = min(64, nbl)
    assert nbl % (2 * HG_GROUP) == 0 and nbc % HG_GROUP == 0 and nbl % out_blocks == 0
    spec = lambda rows: pl.BlockSpec((1, rows, HG_DK), lambda i, h: (i, 0, h))
    slot = lambda dt: pltpu.VMEM((2, 2, HG_GROUP * HG_BLOCK, HG_DK), dt)
    return pl.pallas_call(
        functools.partial(_hgrn_kernel, nbl=nbl, nbc=nbc, out_blocks=out_blocks),
        out_shape=jax.ShapeDtypeStruct((b, n, D_MODEL), BF16),
        grid=(b, HG_HEADS),
        in_specs=[pl.BlockSpec((1, HG_DK), lambda i, h: (0, 0))] + [spec(n)] * 7 + [spec(nc)] * 5,
        out_specs=spec(n),
        scratch_shapes=[pltpu.VMEM((n, 2 * HG_DK), BF16),
                        pltpu.VMEM((nbl, HG_BLOCK, HG_BLOCK), BF16),
                        pltpu.VMEM((nbl, HG_DK, 2 * HG_DK), F32),
                        pltpu.VMEM((nbl, 1, 2 * HG_DK), F32),
                        pltpu.VMEM((nbc, HG_DK, 2 * HG_DK), F32),
                        pltpu.VMEM((nbc, 1, 2 * HG_DK), F32),
                        slot(F32), slot(BF16), slot(BF16), slot(BF16)],
        compiler_params=_cp("parallel", "parallel"),
        name="hgrn2",
    )(norm_g, q_l, lff_l, kf_l, lfb_l, kb_l, i_l, sg_l, lff_c, kf_c, lfb_c, kb_c, i_c)


def _rope_tables(n_tok):
    tok = jnp.arange(n_tok, dtype=jnp.int32)[:, None]
    row = (tok // GRID_W).astype(F32)
    col = (tok % GRID_W).astype(F32)
    n_freq = RET_DK // 4
    lane = jnp.arange(LANE, dtype=jnp.int32)[None, :]
    j = lane % (2 * n_freq)
    inv = ROPE_BASE ** (-(j % n_freq).astype(F32) / n_freq)
    ang = jnp.where(j < n_freq, row, col) * inv
    sign = jnp.where(lane % RET_DK < RET_DK // 2, -1.0, 1.0)
    return jnp.cos(ang), jnp.sin(ang) * sign


def kernel(x, c, ctx, c_ctx, w_mod, b_mod, norm_mix, norm_mlp, w_mlp_in, w_mlp_out, ab_w_in, ab_w_out, ret_logit, s5_a_re, s5_a_im, s5_log_dt, s5_b_re, s5_b_im, s5_c_re, s5_c_im, s5_d, s5_w_glu, s5_b_glu, hg_w_in, hg_w_out, hg_lb_logits, hg_norm, norm_final):
    b, n, d = x.shape
    nc = ctx.shape[1]
    assert d == D_MODEL and b + 1 <= MOD_ROWS and w_mod.shape[0] == DEPTH == 2
    assert n % 512 == 0 and nc % 256 == 0 and n % GRID_W == 0
    ctx_row = b
    tm_l, tm_c = 512, 256

    cc = jnp.zeros((MOD_ROWS, d), F32).at[:b].set(c).at[b].set(c_ctx)
    mod = _adaln(cc, w_mod, b_mod)

    row2 = lambda a: a.reshape(1, -1)
    w_in0 = ab_w_in[0]
    cos, sin = _rope_tables(n)
    ng0 = row2(norm_mix[0])
    q_l, k_l, v_l, u_l, uz_l, g_l = _inproj0(x, mod, 0, ng0, w_in0, cos, sin, None, tm_l)
    q_c, k_c, v_c, u_c, uz_c, g_c = _inproj0(ctx, mod, 0, ng0, w_in0, None, None, ctx_row, tm_c)

    log_gamma = jax.nn.log_sigmoid(ret_logit[0].astype(F32))
    lg_rows = jnp.broadcast_to(log_gamma.reshape(2 * RET_HEADS, 1), (2 * RET_HEADS, 2 * RET_DK))
    r_l, r_c = _retention(lg_rows, q_l, k_l, v_l, g_l, q_c, k_c, v_c, g_c)

    s5_ops = _s5_prep(s5_a_re[0], s5_a_im[0], s5_log_dt[0], s5_b_re[0], s5_b_im[0], s5_c_re[0], s5_c_im[0])
    y5_l, y5_c = _s5(s5_ops, uz_l, uz_c)

    s5p = (row2(s5_d[0]), s5_w_glu[0].astype(BF16), row2(s5_b_glu[0]))
    wo0 = ab_w_out[0].astype(BF16)
    w1_0, w2_0 = w_mlp_in, w_mlp_out
    nm0 = row2(norm_mlp[0])
    h_l = _mix_mlp(x, r_l, y5_l, u_l, s5p, wo0, mod, 0, nm0, w1_0, w2_0, None, None, tm_l, 1024, "mix_mlp0_lat")
    h_c = _mix_mlp(ctx, r_c, y5_c, u_c, s5p, wo0, mod, 0, nm0, w1_0, w2_0, None, ctx_row, tm_c, 1024, "mix_mlp0_ctx")

    w_in1 = hg_w_in[0]
    ng1 = row2(norm_mix[1])
    lat1 = _inproj1(h_l, mod, 1, ng1, hg_lb_logits, w_in1, True, None, tm_l, "inproj1_lat")
    ctx1 = _inproj1(h_c, mod, 1, ng1, hg_lb_logits, w_in1, False, ctx_row, tm_c, "inproj1_ctx")
    o1 = _hgrn(row2(hg_norm[0]), *lat1, *ctx1)
    return _mix_mlp(h_l, o1, None, None, None, hg_w_out[0].astype(BF16), mod, 1, row2(norm_mlp[1]),
                    w_mlp_in, w_mlp_out, row2(norm_final), None, tm_l, 1024,
                    "mix_mlp1_lat")
```

```python
import functools

import jax
import jax.numpy as jnp
from jax import lax
from jax.experimental import pallas as pl
from jax.experimental.pallas import tpu as pltpu

F32 = jnp.float32
BF16 = jnp.bfloat16

D_MODEL = 1024
DEPTH = 2
GRID_W = 64
EPS = 1e-6
ROPE_BASE = 10000.0
N_MOD = 6
RET_HEADS = 4
RET_DK = 64
RET_DV = 128
RET_QK = RET_HEADS * RET_DK
RET_WIDTH = RET_HEADS * RET_DV
RET_CHUNK = 128
S5_WIDTH = D_MODEL - RET_WIDTH
S5_GROUP = 16
S5_GROUPS = S5_WIDTH // S5_GROUP
S5_STATE = 64
S5_CHUNK = 16
S5_ROW = S5_CHUNK * S5_GROUP
LANE = 128
S5_LB = S5_WIDTH // LANE
S5_GPB = LANE // S5_GROUP
S5_BIG = S5_CHUNK * LANE
S5_HALF = S5_GPB * 2 * S5_STATE
S5_POW = 32
AB_IN = 2 * RET_QK + 2 * RET_WIDTH + S5_WIDTH
HG_HEADS = 8
HG_DK = D_MODEL // HG_HEADS
HG_BLOCK = 64
HG_GROUP = 4
HG_SPLIT = 2
D_FF = 4 * D_MODEL
MOD_ROWS = 16

VMEM_LIMIT_BYTES = 56 * 1024 * 1024


def _cp(*sem):
    return pltpu.CompilerParams(dimension_semantics=sem, vmem_limit_bytes=VMEM_LIMIT_BYTES)


def _dot(a, b):
    return jnp.dot(a, b, preferred_element_type=F32)


def _dot_nt(a, b):
    return lax.dot_general(a, b, (((1,), (1,)), ((), ())), preferred_element_type=F32)


def _dot_tn(a, b):
    return lax.dot_general(a, b, (((0,), (0,)), ((), ())), preferred_element_type=F32)


def _sigmoid(x):
    return 0.5 * jnp.tanh(0.5 * x) + 0.5


def _rms(x):
    return x * lax.rsqrt(jnp.mean(x * x, axis=-1, keepdims=True) + EPS)


def _mod_chunk(mod_ref, row, i):
    return mod_ref[0, pl.ds(row, 1), i * D_MODEL:(i + 1) * D_MODEL]


def _adaln_kernel(cc_ref, w_ref, b_ref, o_ref):
    s = jax.nn.silu(cc_ref[...]).astype(BF16)
    o_ref[0] = _dot(s, w_ref[0].astype(BF16)) + b_ref[0]


def _adaln(cc, w_mod, b_mod):
    bn = 1536
    n = N_MOD * D_MODEL
    return pl.pallas_call(
        _adaln_kernel,
        out_shape=jax.ShapeDtypeStruct((DEPTH, MOD_ROWS, n), F32),
        grid=(DEPTH, n // bn),
        in_specs=[
            pl.BlockSpec((MOD_ROWS, D_MODEL), lambda l, j: (0, 0)),
            pl.BlockSpec((1, D_MODEL, bn), lambda l, j: (l, 0, j)),
            pl.BlockSpec((1, 1, bn), lambda l, j: (l, 0, j)),
        ],
        out_specs=pl.BlockSpec((1, MOD_ROWS, bn), lambda l, j: (l, 0, j)),
        compiler_params=_cp("parallel", "parallel"),
        name="adaln",
    )(cc, w_mod, b_mod.reshape(DEPTH, 1, n))


def _rope(t, cos, sin):
    lane = lax.broadcasted_iota(jnp.int32, t.shape, 1)
    first = (lane & (RET_DK // 2)) == 0
    w = t.shape[1]
    swapped = jnp.where(first, pltpu.roll(t, w - RET_DK // 2, 1), pltpu.roll(t, RET_DK // 2, 1))
    return t * cos + swapped * sin


def _inproj0_kernel(*refs, mod_row, rope):
    if rope:
        h_ref, mod_ref, ng_ref, w_ref, cos_ref, sin_ref, q_ref, k_ref, v_ref, u_ref, uz_ref, g_ref, u_s = refs
    else:
        h_ref, mod_ref, ng_ref, w_ref, q_ref, k_ref, v_ref, u_ref, uz_ref, g_ref, u_s = refs
    row = pl.program_id(0) if mod_row is None else mod_row
    xn = _rms(h_ref[0]) * ng_ref[...]
    xm = (xn * (1.0 + _mod_chunk(mod_ref, row, 1)) + _mod_chunk(mod_ref, row, 0)).astype(BF16)
    y = _dot(xm, w_ref[...].astype(BF16))
    q = y[:, 0:RET_QK]
    k = y[:, RET_QK:2 * RET_QK]
    if rope:
        cos = jnp.concatenate([cos_ref[...]] * (RET_QK // LANE), axis=1)
        sin = jnp.concatenate([sin_ref[...]] * (RET_QK // LANE), axis=1)
        q = _rope(q, cos, sin)
        k = _rope(k, cos, sin)
    q_ref[0] = q.astype(BF16)
    k_ref[0] = (k * (RET_DK ** -0.5)).astype(BF16)
    c0 = 2 * RET_QK
    v_ref[0] = y[:, c0:c0 + RET_WIDTH].astype(BF16)
    u0 = c0 + RET_WIDTH
    u_ref[0] = y[:, u0:u0 + S5_WIDTH]
    g = y[:, u0 + S5_WIDTH:]
    g_ref[0] = (g * _sigmoid(g)).astype(BF16)
    nch = u_s.shape[1] // S5_CHUNK
    for j in range(S5_LB):
        u_s[j] = y[:, u0 + j * LANE:u0 + (j + 1) * LANE]
        for s in range(S5_CHUNK):
            uz_ref[0, j, :, s * LANE:(s + 1) * LANE] = u_s[j, pl.ds(s, nch, stride=S5_CHUNK), :].astype(BF16)


def _inproj0(h, mod, layer, ng, w, cos, sin, mod_row, tm):
    b, n, _ = h.shape
    rope = cos is not None
    row_spec = lambda width: pl.BlockSpec((1, tm, width), lambda i, j: (i, j, 0))
    in_specs = [
        row_spec(D_MODEL),
        pl.BlockSpec((1, MOD_ROWS, N_MOD * D_MODEL), lambda i, j: (layer, 0, 0)),
        pl.BlockSpec((1, D_MODEL), lambda i, j: (0, 0)),
        pl.BlockSpec((D_MODEL, AB_IN), lambda i, j: (0, 0), pipeline_mode=pl.Buffered(1)),
    ]
    args = [h, mod, ng, w]
    if rope:
        in_specs += [pl.BlockSpec((tm, LANE), lambda i, j: (j, 0))] * 2
        args += [cos, sin]
    widths = (RET_QK, RET_QK, RET_WIDTH, S5_WIDTH, RET_WIDTH)
    dtypes = (BF16, BF16, BF16, F32, BF16)
    out_shape = [jax.ShapeDtypeStruct((b, n, wd), dt) for wd, dt in zip(widths, dtypes)]
    out_specs = [row_spec(wd) for wd in widths]
    out_shape.insert(4, jax.ShapeDtypeStruct((b, S5_LB, n // S5_CHUNK, S5_BIG), BF16))
    out_specs.insert(4, pl.BlockSpec((1, S5_LB, tm // S5_CHUNK, S5_BIG), lambda i, j: (i, 0, j, 0)))
    return pl.pallas_call(
        functools.partial(_inproj0_kernel, mod_row=mod_row, rope=rope),
        out_shape=out_shape,
        grid=(b, n // tm),
        in_specs=in_specs,
        out_specs=out_specs,
        scratch_shapes=[pltpu.VMEM((S5_LB, tm, LANE), F32)],
        compiler_params=_cp("parallel", "parallel"),
        name="inproj0_lat" if rope else "inproj0_ctx",
    )(*args)


def _ret_kernel(lg_ref, ql, kl, vl, gl, qc, kc, vc, gc, rl, rc, st_s, *, ncl, ncc, unroll):
    c = RET_CHUNK
    dk2 = 2 * RET_DK
    nt = ncc + ncl
    p = pl.program_id(1)
    h_a = 2 * p
    lgf_a = lg_ref[pl.ds(h_a, 1), :]
    lgf_b = lg_ref[pl.ds(h_a + 1, 1), :]
    lgb_a = lg_ref[pl.ds(RET_HEADS + h_a, 1), :]
    lgb_b = lg_ref[pl.ds(RET_HEADS + h_a + 1, 1), :]
    lane = lax.broadcasted_iota(jnp.int32, (1, 2 * RET_DK), 1)
    is_a = lane < RET_DK
    lgf_lane = jnp.where(is_a, lgf_a, lgf_b)
    lgb_lane = jnp.where(is_a, lgb_a, lgb_b)
    ri = lax.broadcasted_iota(jnp.int32, (c, c), 0).astype(F32)
    ci = lax.broadcasted_iota(jnp.int32, (c, c), 1).astype(F32)
    diff = ri - ci

    def dmat(lgf, lgb):
        fwd = jnp.exp(jnp.maximum(diff, 0.0) * lgf)
        bwd = jnp.exp(jnp.maximum(-diff, 0.0) * lgb)
        return jnp.where(diff > 0, fwd, jnp.where(diff < 0, bwd, 2.0))

    d_a = dmat(lgf_a, lgb_a)
    d_b = dmat(lgf_b, lgb_b)
    rowp = lax.broadcasted_iota(jnp.int32, (c, dk2), 0).astype(F32)
    qd = jnp.concatenate([jnp.exp((rowp + 1.0) * lgf_lane), jnp.exp((c - rowp) * lgb_lane)], axis=1)
    kd = jnp.concatenate([jnp.exp((c - 1.0 - rowp) * lgf_lane), jnp.exp(rowp * lgb_lane)], axis=1)
    rowk = lax.broadcasted_iota(jnp.int32, (dk2, 2 * RET_DV), 0)
    cd_f = jnp.exp(c * jnp.where(rowk < RET_DK, lgf_a[:, :1], lgf_b[:, :1]))
    cd_b = jnp.exp(c * jnp.where(rowk < RET_DK, lgb_a[:, :1], lgb_b[:, :1]))
    mask2 = jnp.concatenate([is_a, is_a], axis=1)

    def increment(k, v, slot):
        kk = jnp.concatenate([k, k], axis=1).astype(F32) * kd
        st_s[slot] = _dot_tn(kk.astype(BF16), v)

    for n in range(ncc):
        increment(kc[0, n * c:(n + 1) * c, :], vc[0, n * c:(n + 1) * c, :], n)

    def inc_body(i, carry):
        for j in range(unroll):
            n = i * unroll + j
            sl = pl.ds(pl.multiple_of(n * c, c), c)
            increment(kl[0, sl, :], vl[0, sl, :], ncc + n)
        return carry

    lax.fori_loop(0, ncl // unroll, inc_body, 0)

    def rec_body(t, carry):
        sf, sb = carry
        u = jnp.where(t < ncc, ncc - 1 - t, nt - 1 - (t - ncc))
        inc_f = st_s[t, 0:dk2, :]
        inc_b = st_s[u, dk2:2 * dk2, :]
        st_s[t, 0:dk2, :] = sf
        st_s[u, dk2:2 * dk2, :] = sb
        return cd_f * sf + inc_f, cd_b * sb + inc_b

    zero = jnp.zeros((dk2, 2 * RET_DV), F32)
    lax.fori_loop(0, nt, rec_body, (zero, zero), unroll=True)

    def output(q, k, v, g, slot, out_ref, st):
        q2 = (jnp.concatenate([q, q], axis=1).astype(F32) * qd).astype(BF16)
        s_n = st_s[slot].astype(BF16)
        for keep_a, dm, cs in ((True, d_a, 0), (False, d_b, RET_DV)):
            m1 = is_a if keep_a else jnp.logical_not(is_a)
            m2 = mask2 if keep_a else jnp.logical_not(mask2)
            att = _dot_nt(jnp.where(m1, q, jnp.zeros_like(q)), k) * dm
            o = (_dot(att.astype(BF16), v[:, cs:cs + RET_DV])
                 + _dot(jnp.where(m2, q2, jnp.zeros_like(q2)), s_n[:, cs:cs + RET_DV]))
            o = _rms(o) * g[:, cs:cs + RET_DV].astype(F32)
            out_ref[0, pl.ds(st, c), cs:cs + RET_DV] = o.astype(BF16)

    for n in range(ncc):
        sl = slice(n * c, (n + 1) * c)
        output(qc[0, sl, :], kc[0, sl, :], vc[0, sl, :], gc[0, sl, :], n, rc, n * c)

    def out_body(i, carry):
        for j in range(unroll):
            n = i * unroll + j
            st = pl.multiple_of(n * c, c)
            sl = pl.ds(st, c)
            output(ql[0, sl, :], kl[0, sl, :], vl[0, sl, :], gl[0, sl, :], ncc + n, rl, st)
        return carry

    lax.fori_loop(0, ncl // unroll, out_body, 0)


def _retention(lg_rows, q_l, k_l, v_l, g_l, q_c, k_c, v_c, g_c):
    b, n, _ = q_l.shape
    nc = q_c.shape[1]
    ncl, ncc = n // RET_CHUNK, nc // RET_CHUNK
    pairs = RET_HEADS // 2
    unroll = ncl

    def spec(rows, width):
        return pl.BlockSpec((1, rows, width), lambda i, p: (i, 0, p))

    return pl.pallas_call(
        functools.partial(_ret_kernel, ncl=ncl, ncc=ncc, unroll=unroll),
        out_shape=[jax.ShapeDtypeStruct((b, n, RET_WIDTH), BF16),
                   jax.ShapeDtypeStruct((b, nc, RET_WIDTH), BF16)],
        grid=(b, pairs),
        in_specs=[pl.BlockSpec((2 * RET_HEADS, 2 * RET_DK), lambda i, p: (0, 0)),
                  spec(n, 2 * RET_DK), spec(n, 2 * RET_DK), spec(n, 2 * RET_DV), spec(n, 2 * RET_DV),
                  spec(nc, 2 * RET_DK), spec(nc, 2 * RET_DK), spec(nc, 2 * RET_DV), spec(nc, 2 * RET_DV)],
        out_specs=[spec(n, 2 * RET_DV), spec(nc, 2 * RET_DV)],
        scratch_shapes=[pltpu.VMEM((ncl + ncc, 4 * RET_DK, 2 * RET_DV), F32)],
        compiler_params=_cp("parallel", "parallel"),
        name="retention",
    )(lg_rows, q_l, k_l, v_l, g_l, q_c, k_c, v_c, g_c)


def _dot_hi(a, b, contract=(1, 0)):
    dims = (((contract[0],), (contract[1],)), ((), ()))
    return lax.dot_general(a, b, dims, preferred_element_type=F32, precision=lax.Precision.HIGHEST)


def _dot_sel(a, b, contract, data):
    dims = (((contract[0],), (contract[1],)), ((), ()))
    x = (a, b)[data]
    hi = x.astype(BF16)
    lo = (x - hi.astype(F32)).astype(BF16)
    dd = lambda piece: lax.dot_general(*((piece, b) if data == 0 else (a, piece)), dims, preferred_element_type=F32)
    return dd(hi) + dd(lo)


def _s5_prep_kernel(ar_row, ai_row, ldt, btr, bti, ctr, cti, rm, rwin, rwout, abig):
    t, g, p, kp = S5_CHUNK, S5_GROUP, S5_STATE, S5_POW
    row = S5_ROW
    i0 = lambda shape: lax.broadcasted_iota(jnp.int32, shape, 0)
    i1 = lambda shape: lax.broadcasted_iota(jnp.int32, shape, 1)
    f32 = lambda m: jnp.where(m, 1.0, 0.0).astype(BF16)
    s_of_r = i0((row, kp)) // g
    k_of_l = i1((row, kp))
    sel_rows = (f32(k_of_l == t - 1 - s_of_r), f32(k_of_l == s_of_r))
    t_of_c = i1((kp, row)) // g
    k_of_s = i0((kp, row))
    sel_out = (f32(k_of_s == t_of_c + 1), f32(k_of_s == t - t_of_c))
    sel_lag = (f32(k_of_s == t_of_c), f32(k_of_s == t - 1 - t_of_c))
    tile_l = f32(i1((g, row)) % g == i0((g, row)))
    tile_r = f32(i0((row, g)) % g == i1((row, g)))
    lane = i1((g, row))
    k_col = i0((kp, 1)).astype(F32)
    k_row = i1((1, kp)).astype(F32)
    first = i0((8, 1)) == 0

    def outer(a, k):
        a8 = jnp.where(first, jnp.broadcast_to(a, (8, a.shape[1])), 0.0)
        return _dot_hi(a8, jnp.broadcast_to(k, (8, k.shape[1])), (0, 0))

    for q in range(S5_GPB):
        lags = []
        for d in range(2):
            dt = jnp.exp(ldt[d, q])
            are_r, aim_r = ar_row[d, q], ai_row[d, q]
            mag = jnp.exp(are_r * dt)
            ang = aim_r * dt
            nr, ni = mag * jnp.cos(ang) - 1.0, mag * jnp.sin(ang)
            den = jnp.square(are_r) + jnp.square(aim_r)
            fr = (nr * are_r + ni * aim_r) / den
            fi = (ni * are_r - nr * aim_r) / den
            pm = jnp.exp(k_col * (are_r * dt))
            pa = k_col * ang
            pk_re, pk_im = pm * jnp.cos(pa), pm * jnp.sin(pa)
            pmt = jnp.exp(outer(are_r * dt, k_row))
            pat = outer(ang, k_row)
            pt_re, pt_im = pmt * jnp.cos(pat), pmt * jnp.sin(pat)
            bt_re, bt_im = _dot_sel(tile_r, btr[d, q], (1, 1), 1), _dot_sel(tile_r, bti[d, q], (1, 1), 1)
            bb_re = fr * bt_re - fi * bt_im
            bb_im = fr * bt_im + fi * bt_re
            pr_re, pr_im = _dot_sel(sel_rows[d], pk_re, (1, 0), 1), _dot_sel(sel_rows[d], pk_im, (1, 0), 1)
            w_re = pr_re * bb_re - pr_im * bb_im
            w_im = pr_re * bb_im + pr_im * bb_re
            for s in range(t):
                rows = slice(s * LANE + q * g, s * LANE + (q + 1) * g)
                rwin[0, rows, d * p:(d + 1) * p] = w_re[s * g:(s + 1) * g].astype(BF16)
                rwin[0, rows, (2 + d) * p:(3 + d) * p] = w_im[s * g:(s + 1) * g].astype(BF16)
            ct_re, ct_im = _dot_sel(ctr[d, q], tile_l, (0, 0), 0), _dot_sel(cti[d, q], tile_l, (0, 0), 0)

            def c_pow(sel):
                pc_re, pc_im = _dot_sel(pt_re, sel, (1, 0), 0), _dot_sel(pt_im, sel, (1, 0), 0)
                return ct_re * pc_re - ct_im * pc_im, ct_re * pc_im + ct_im * pc_re

            o_re, o_im = c_pow(sel_out[d])
            r0 = d * S5_GPB * p + q * p
            rwout[0, r0:r0 + p, :] = o_re.astype(BF16)
            rwout[0, S5_HALF + r0:S5_HALF + r0 + p, :] = (-o_im).astype(BF16)
            l_re, l_im = c_pow(sel_lag[d])
            lags.append(_dot_hi(bb_re[0:g], l_re) - _dot_hi(bb_im[0:g], l_im))
            abig[0, 0:1, r0:r0 + p] = pk_re[t:t + 1, :]
            abig[0, 1:2, r0:r0 + p] = pk_im[t:t + 1, :]
        for s in range(t):
            fwd = jnp.where(lane >= g * s, pltpu.roll(lags[0], g * s, 1), 0.0)
            bwd = jnp.where(lane < g * (s + 1), pltpu.roll(lags[1], (row - g * (t - 1 - s)) % row, 1), 0.0)
            rm[0, s * LANE + q * g:s * LANE + (q + 1) * g, :] = (fwd + bwd).astype(BF16)


def _s5_prep(a_re, a_im, log_dt, b_re, b_im, c_re, c_im):
    gg, p, g = S5_GROUPS, S5_STATE, S5_GROUP
    f = lambda x: x.astype(F32)
    args = (f(a_re).reshape(2, gg, 1, p), f(a_im).reshape(2, gg, 1, p),
            f(log_dt).reshape(2, gg, 1, 1),
            f(b_re), f(b_im), f(c_re), f(c_im))
    spec = lambda r, c: pl.BlockSpec((2, S5_GPB, r, c), lambda j: (0, j, 0, 0))
    out = lambda r, c: pl.BlockSpec((1, r, c), lambda j: (j, 0, 0))
    return pl.pallas_call(
        _s5_prep_kernel,
        out_shape=[jax.ShapeDtypeStruct((S5_LB, S5_BIG, S5_ROW), BF16),
                   jax.ShapeDtypeStruct((S5_LB, S5_BIG, S5_ROW), BF16),
                   jax.ShapeDtypeStruct((S5_LB, 2 * S5_HALF, S5_ROW), BF16),
                   jax.ShapeDtypeStruct((S5_LB, 2, S5_HALF), F32)],
        grid=(S5_LB,),
        in_specs=[spec(1, p), spec(1, p), spec(1, 1),
                  spec(p, g), spec(p, g), spec(g, p), spec(g, p)],
        out_specs=[out(S5_BIG, S5_ROW), out(S5_BIG, S5_ROW), out(2 * S5_HALF, S5_ROW), out(2, S5_HALF)],
        compiler_params=_cp("parallel"),
        name="s5_prep",
    )(*args)


def _s5_expand(r_ref, col_unit, col_block, row_unit):
    n = S5_BIG
    a = lax.broadcasted_iota(jnp.int32, (S5_ROW, n), 0)
    c = lax.broadcasted_iota(jnp.int32, (S5_ROW, n), 1)
    e = jnp.where((a // col_unit == c // col_block) & (a % col_unit == c % col_unit), 1.0, 0.0).astype(BF16)
    x = _dot(r_ref[0], e)
    rq = (lax.broadcasted_iota(jnp.int32, (n, n), 0) // row_unit) % S5_GPB
    cq = (lax.broadcasted_iota(jnp.int32, (n, n), 1) // col_unit) % S5_GPB
    return jnp.where(rq == cq, x, 0.0).astype(BF16)


def _s5_kernel(ul, uc, rm_ref, rwin_ref, rwout_ref, a_ref, yl, yc, x_s, m_s, win_s, wout_s, *, ncl, ncc):
    hw = S5_HALF
    hh = hw // 2

    @pl.when(pl.program_id(1) == 0)
    def _():
        m_s[...] = _s5_expand(rm_ref, S5_GROUP, LANE, S5_GROUP)
        win_s[...] = _s5_expand(rwin_ref, S5_STATE, S5_GPB * S5_STATE, S5_GROUP)
        wout_s[...] = _s5_expand(rwout_ref, S5_GROUP, LANE, S5_STATE)

    z = jnp.concatenate([uc[0, 0], ul[0, 0]], axis=0)
    x_s[...] = _dot(z, win_s[...])
    a_re = a_ref[0, 0:1, :]
    a_im = a_ref[0, 1:2, :]

    def segment(base, n, carry):
        s_re, s_im = carry
        for i in range(n):
            rf = slice(base + i, base + i + 1)
            rb = slice(base + n - 1 - i, base + n - i)
            x_re = jnp.concatenate([x_s[rf, 0:hh], x_s[rb, hh:hw]], axis=-1)
            x_im = jnp.concatenate([x_s[rf, hw:hw + hh], x_s[rb, hw + hh:2 * hw]], axis=-1)
            x_s[rf, 0:hh] = s_re[:, 0:hh]
            x_s[rb, hh:hw] = s_re[:, hh:hw]
            x_s[rf, hw:hw + hh] = s_im[:, 0:hh]
            x_s[rb, hw + hh:2 * hw] = s_im[:, hh:hw]
            s_re, s_im = a_re * s_re - a_im * s_im + x_re, a_re * s_im + a_im * s_re + x_im
        return s_re, s_im

    zero = jnp.zeros((1, hw), F32)
    carry = segment(0, ncc, (zero, zero))
    segment(ncc, ncl, carry)
    y = _dot(z, m_s[...]) + _dot(x_s[...].astype(BF16), wout_s[...])
    yc[0, 0] = y[0:ncc]
    yl[0, 0] = y[ncc:ncc + ncl]


def _s5(prep, zl, zc):
    rm, rwin, rwout, a_big = prep
    b, _, ncl, _ = zl.shape
    ncc = zc.shape[2]
    rows = lambda r: pl.BlockSpec((1, 1, r, S5_BIG), lambda j, i: (i, j, 0, 0))
    wspec = lambda r, c: pl.BlockSpec((1, r, c), lambda j, i: (j, 0, 0))
    big = pltpu.VMEM((S5_BIG, S5_BIG), BF16)
    return pl.pallas_call(
        functools.partial(_s5_kernel, ncl=ncl, ncc=ncc),
        out_shape=[jax.ShapeDtypeStruct(zl.shape, F32), jax.ShapeDtypeStruct(zc.shape, F32)],
        grid=(S5_LB, b),
        in_specs=[rows(ncl), rows(ncc), wspec(S5_BIG, S5_ROW), wspec(S5_BIG, S5_ROW),
                  wspec(2 * S5_HALF, S5_ROW), wspec(2, S5_HALF)],
        out_specs=[rows(ncl), rows(ncc)],
        scratch_shapes=[pltpu.VMEM((ncc + ncl, 2 * S5_HALF), F32), big, big, big],
        compiler_params=_cp("parallel", "arbitrary"),
        name="s5",
    )(zl, zc, rm, rwin, rwout, a_big)


def _mix_mlp_kernel(*refs, mod_row, s5_merge, final_norm, fb):
    if s5_merge:
        (h_ref, r_ref, y5_ref, u_ref, ds_ref, wg_ref, bg_ref, wo_ref, mod_ref, nm_ref, w1_ref, w2_ref,
         *rest) = refs
    else:
        h_ref, r_ref, wo_ref, mod_ref, nm_ref, w1_ref, w2_ref, *rest = refs
    if final_norm:
        nf_ref, o_ref, *scratch = rest
    else:
        o_ref, *scratch = rest
    row = pl.program_id(0) if mod_row is None else mod_row
    if s5_merge:
        (y_s,) = scratch
        nch = y_s.shape[1] // S5_CHUNK
        for lb in range(S5_LB):
            for s in range(S5_CHUNK):
                y_s[lb, pl.ds(s, nch, stride=S5_CHUNK), :] = y5_ref[0, lb, :, s * LANE:(s + 1) * LANE]
        y5 = jnp.concatenate([y_s[lb] for lb in range(S5_LB)], axis=-1)
        y = jax.nn.gelu(y5 + ds_ref[...] * u_ref[0])
        y = y * jax.nn.sigmoid(_dot(y.astype(BF16), wg_ref[...]) + bg_ref[...])
        mix = _dot(r_ref[0], wo_ref[0:RET_WIDTH, :]) + _dot(y.astype(BF16), wo_ref[RET_WIDTH:D_MODEL, :])
    else:
        mix = _dot(r_ref[0], wo_ref[...])
    h1 = h_ref[0] + _mod_chunk(mod_ref, row, 2) * mix
    xn = _rms(h1) * nm_ref[...]
    xm = (xn * (1.0 + _mod_chunk(mod_ref, row, 4)) + _mod_chunk(mod_ref, row, 3)).astype(BF16)
    acc = None
    for j in range(D_FF // fb):
        a = jnp.square(jnp.maximum(_dot(xm, w1_ref[:, j * fb:(j + 1) * fb].astype(BF16)), 0.0)).astype(BF16)
        part = _dot(a, w2_ref[j * fb:(j + 1) * fb, :].astype(BF16))
        acc = part if acc is None else acc + part
    h2 = h1 + _mod_chunk(mod_ref, row, 5) * acc
    if final_norm:
        h2 = _rms(h2) * nf_ref[...]
    o_ref[0] = h2


def _mix_mlp(h, r, s5y, u, s5p, wo, mod, layer, nm, w1, w2, nf, mod_row, tm, fb, name):
    b, n, _ = h.shape
    s5_merge = s5y is not None
    final_norm = nf is not None
    one = pl.Buffered(1)
    row_spec = lambda width: pl.BlockSpec((1, tm, width), lambda i, t: (i, t, 0))
    const = lambda shape: pl.BlockSpec(shape, lambda i, t: (0,) * len(shape), pipeline_mode=one)
    in_specs = [row_spec(D_MODEL), row_spec(r.shape[-1])]
    args = [h, r]
    if s5_merge:
        d_skip, w_glu, b_glu = s5p
        z_spec = pl.BlockSpec((1, S5_LB, tm // S5_CHUNK, S5_BIG), lambda i, t: (i, 0, t, 0))
        in_specs += [z_spec, row_spec(S5_WIDTH), const((1, S5_WIDTH)),
                     const((S5_WIDTH, S5_WIDTH)), const((1, S5_WIDTH))]
        args += [s5y, u, d_skip, w_glu, b_glu]
    in_specs += [const((D_MODEL, D_MODEL)),
                 pl.BlockSpec((1, MOD_ROWS, N_MOD * D_MODEL), lambda i, t: (layer, 0, 0), pipeline_mode=one),
                 const((1, D_MODEL))]
    if w1.ndim == 3:
        in_specs += [pl.BlockSpec((None, D_MODEL, D_FF), lambda i, t: (layer, 0, 0), pipeline_mode=one),
                     pl.BlockSpec((None, D_FF, D_MODEL), lambda i, t: (layer, 0, 0), pipeline_mode=one)]
    else:
        in_specs += [const((D_MODEL, D_FF)), const((D_FF, D_MODEL))]
    args += [wo, mod, nm, w1, w2]
    if final_norm:
        in_specs.append(const((1, D_MODEL)))
        args.append(nf)
    return pl.pallas_call(
        functools.partial(_mix_mlp_kernel, mod_row=mod_row, s5_merge=s5_merge, final_norm=final_norm, fb=fb),
        out_shape=jax.ShapeDtypeStruct((b, n, D_MODEL), F32),
        grid=(b, n // tm),
        in_specs=in_specs,
        out_specs=row_spec(D_MODEL),
        scratch_shapes=[pltpu.VMEM((S5_LB, tm, LANE), F32)] if s5_merge else [],
        compiler_params=_cp("parallel", "parallel"),
        name=name,
    )(*args)


def _hgrn_lower_bounds(lbl_ref, layer):
    out = []
    for d in range(2):
        z = [lbl_ref[d, k:k + 1, :] for k in range(DEPTH)]
        zmax = functools.reduce(jnp.maximum, z)
        e = [jnp.exp(v - zmax) for v in z]
        tot = functools.reduce(lambda a, b_: a + b_, e)
        lb = jnp.zeros_like(tot)
        for k in range(1, layer + 1):
            lb = lb + e[k] / tot
        out.append(lb)
    return out


def _inproj1_kernel(h_ref, mod_ref, ng_ref, lbl_ref, w_ref, *out_refs, mod_row, layer, latent):
    row = pl.program_id(0) if mod_row is None else mod_row
    xn = _rms(h_ref[0]) * ng_ref[...]
    xm = (xn * (1.0 + _mod_chunk(mod_ref, row, 1)) + _mod_chunk(mod_ref, row, 0)).astype(BF16)
    lbs = _hgrn_lower_bounds(lbl_ref, layer)
    names = (["q"] if latent else []) + ["lf0", "kk0", "lf1", "kk1", "i"] + (["sg"] if latent else [])
    out = dict(zip(names, out_refs))
    half = h_ref.shape[1] // 2

    def finish(kind, y, rows):
        if kind in ("lf0", "lf1"):
            d = int(kind[2])
            t = (1.0 - lbs[d]) * _sigmoid(y)
            out[kind][0, rows, :] = jnp.log(lbs[d] + t)
            out["kk%d" % d][0, rows, :] = ((1.0 - lbs[d]) - t).astype(BF16)
        elif kind == "sg":
            out[kind][0, rows, :] = (y * _sigmoid(y)).astype(BF16)
        else:
            out[kind][0, rows, :] = y.astype(BF16)

    cols = [("lf0", 1), ("lf1", 2)] + ([("sg", 4), ("q", 0)] if latent else []) + [("i", 3)]
    pending = None
    for kind, k in cols:
        for rows in (slice(0, half), slice(half, 2 * half)):
            y = _dot(xm[rows], w_ref[:, k * D_MODEL:(k + 1) * D_MODEL].astype(BF16))
            if pending is not None:
                finish(*pending)
            pending = (kind, y, rows)
    finish(*pending)


def _inproj1(h, mod, layer, ng, lb_logits, w, latent, mod_row, tm, name):
    b, n, _ = h.shape
    one = pl.Buffered(1)
    row_spec = pl.BlockSpec((1, tm, D_MODEL), lambda i, t: (i, t, 0))
    dtypes = ([BF16] if latent else []) + [F32, BF16, F32, BF16, BF16] + ([BF16] if latent else [])
    return pl.pallas_call(
        functools.partial(_inproj1_kernel, mod_row=mod_row, layer=layer, latent=latent),
        out_shape=[jax.ShapeDtypeStruct((b, n, D_MODEL), dt) for dt in dtypes],
        grid=(b, n // tm),
        in_specs=[row_spec,
                  pl.BlockSpec((1, MOD_ROWS, N_MOD * D_MODEL), lambda i, t: (layer, 0, 0), pipeline_mode=one),
                  pl.BlockSpec((1, D_MODEL), lambda i, t: (0, 0), pipeline_mode=one),
                  pl.BlockSpec(lb_logits.shape, lambda i, t: (0, 0, 0), pipeline_mode=one),
                  pl.BlockSpec(w.shape, lambda i, t: (0, 0), pipeline_mode=one)],
        out_specs=[row_spec] * len(dtypes),
        compiler_params=_cp("parallel", "parallel"),
        name=name,
    )(h, mod, ng, lb_logits, w)


def _cumsum_mm(tri, x):
    acc = None
    r = x
    for i in range(HG_SPLIT):
        p = r.astype(BF16)
        acc = _dot(tri, p) if acc is None else acc + _dot(tri, p)
        if i + 1 < HG_SPLIT:
            r = r - p.astype(F32)
    return acc


def _hgrn_kernel(ng_ref, ql, lffl, kfl, lfbl, kbl, il, sgl, lffc, kfc, lfbc, kbc, ic, o_ref,
                 qin_s, att_s, kv_s, et_s, kvc_s, etc_s, cum_s, ko_s, qt_s, kt_s, *, nbl, nbc, out_blocks):
    cb = HG_BLOCK
    mid = cb // 2
    gb = HG_GROUP
    gr = gb * cb
    dk = HG_DK

    ri = lax.broadcasted_iota(jnp.int32, (gr, gr), 0)
    ci = lax.broadcasted_iota(jnp.int32, (gr, gr), 1)
    same = (ri // cb) == (ci // cb)
    rb = lax.broadcasted_iota(jnp.int32, (cb, cb), 0)
    cbi = lax.broadcasted_iota(jnp.int32, (cb, cb), 1)
    tri_l = jnp.where(same & (ri >= ci), 1.0, 0.0).astype(BF16)
    dirs = ((0, rb >= cbi, mid - 1, cb - 1), (1, rb <= cbi, mid, 0))

    def cumsums(lfs, slot):
        pre = _cumsum_mm(tri_l, jnp.concatenate(lfs, axis=-1))
        pre_b = pre[:, dk:].reshape(gb, cb, dk)
        cum_s[slot, 0] = pre[:, :dk]
        cum_s[slot, 1] = (pre_b[:, cb - 1:cb, :] - pre_b).reshape(gr, dk) + lfs[1]

    def operands(slot, kks, q, n0, et_ref, which=(0, 1)):
        for d, keep, ref_row, tot_row in [dirs[i] for i in which]:
            cum = cum_s[slot, d].reshape(gb, cb, dk)
            kk = kks[d].astype(F32).reshape(gb, cb, dk)
            ref = cum[:, ref_row:ref_row + 1, :]
            tot = cum[:, tot_row:tot_row + 1, :]
            e = cum - ref
            kt = kk * jnp.exp(-e)
            ko_s[slot, d] = (kt * jnp.exp(tot - ref)).astype(BF16).reshape(gr, dk)
            e_tot = jnp.exp(tot)
            for j in range(gb):
                et_ref[n0 + j, :, d * dk:(d + 1) * dk] = e_tot[j]
            if q is not None:
                qt = q.astype(F32).reshape(gb, cb, dk) * jnp.exp(e)
                qin_s[pl.ds(pl.multiple_of(n0 * cb, gr), gr), d * dk:(d + 1) * dk] = (
                    (qt * jnp.exp(ref)).astype(BF16).reshape(gr, dk))
                qt_s[slot, d] = qt.astype(BF16).reshape(gr, dk)
                kt_s[slot, d] = kt.astype(BF16).reshape(gr, dk)

    def matmuls(slot, v, with_q, n0, kv_ref, blocks=tuple(range(HG_GROUP))):
        for d in range(2):
            for j in blocks:
                rows = slice(j * cb, (j + 1) * cb)
                kv_ref[n0 + j, :, d * dk:(d + 1) * dk] = _dot_tn(v[rows], ko_s[slot, d, rows, :])
        if with_q:
            for j in blocks:
                rows = slice(j * cb, (j + 1) * cb)
                att = (jnp.where(dirs[0][1], _dot_nt(qt_s[slot, 0, rows, :], kt_s[slot, 0, rows, :]), 0.0)
                       + jnp.where(dirs[1][1], _dot_nt(qt_s[slot, 1, rows, :], kt_s[slot, 1, rows, :]), 0.0))
                att_s[n0 + j] = att.astype(BF16)

    for g in range(nbc // gb):
        sl = slice(g * gr, (g + 1) * gr)
        cumsums((lffc[0, sl, :], lfbc[0, sl, :]), 0)
        operands(0, (kfc[0, sl, :], kbc[0, sl, :]), None, g * gb, etc_s)
        matmuls(0, ic[0, sl, :], False, g * gb, kvc_s)

    ngl = nbl // gb
    rows_of = lambda g: pl.ds(pl.multiple_of(g * gr, gr), gr)

    def lat_cumsums(g, slot):
        cumsums((lffl[0, rows_of(g), :], lfbl[0, rows_of(g), :]), slot)

    def lat_operands(g, slot, which=(0, 1)):
        operands(slot, (kfl[0, rows_of(g), :], kbl[0, rows_of(g), :]), ql[0, rows_of(g), :], g * gb, et_s, which)

    def lat_matmuls(g, slot, blocks=tuple(range(HG_GROUP))):
        matmuls(slot, il[0, rows_of(g), :], True, g * gb, kv_s, blocks)

    lat_cumsums(0, 0)
    lat_operands(0, 0)
    lat_cumsums(1, 1)

    def prep_body(i, carry):
        g = 2 * i
        half = HG_GROUP // 2
        for a, sa, sb in ((g, 0, 1), (g + 1, 1, 0)):
            lat_matmuls(a, sa, tuple(range(0, half // 2 + half % 2)))
            lat_operands(a + 1, sb, (0,))
            lat_matmuls(a, sa, tuple(range(half // 2 + half % 2, half)))
            lat_cumsums(a + 2, sa)
            lat_matmuls(a, sa, tuple(range(half, half + (HG_GROUP - half) // 2)))
            lat_operands(a + 1, sb, (1,))
            lat_matmuls(a, sa, tuple(range(half + (HG_GROUP - half) // 2, HG_GROUP)))
        return carry

    lax.fori_loop(0, ngl // 2 - 1, prep_body, 0, unroll=True)
    lat_matmuls(ngl - 2, 0)
    lat_operands(ngl - 1, 1)
    lat_matmuls(ngl - 1, 1)

    lane = lax.broadcasted_iota(jnp.int32, (dk, 2 * dk), 1)
    is_f = lane < dk
    st = jnp.zeros((dk, 2 * dk), F32)
    for n in range(nbc):
        m = nbc - 1 - n
        st = (st * jnp.where(is_f[:1], etc_s[n], etc_s[m]) + jnp.where(is_f, kvc_s[n], kvc_s[m]))

    def rec_body(t, st):
        u = nbl - 1 - t
        inc = jnp.where(is_f, kv_s[t], kv_s[u])
        dec = jnp.where(is_f[:1], et_s[t], et_s[u])
        kv_s[t, :, 0:dk] = st[:, 0:dk]
        kv_s[u, :, dk:2 * dk] = st[:, dk:2 * dk]
        return st * dec + inc

    lax.fori_loop(0, nbl, rec_body, st, unroll=True)

    def out_body(i, carry):
        for j in range(out_blocks):
            n = i * out_blocks + j
            sl = pl.ds(pl.multiple_of(n * cb, cb), cb)
            o = _dot(att_s[n], il[0, sl, :]) + _dot_nt(qin_s[sl, :], kv_s[n].astype(BF16))
            o = _rms(o) * ng_ref[...] * sgl[0, sl, :].astype(F32)
            o_ref[0, sl, :] = o.astype(BF16)
        return carry

    lax.fori_loop(0, nbl // out_blocks, out_body, 0)


def _hgrn(norm_g, q_l, lff_l, kf_l, lfb_l, kb_l, i_l, sg_l, lff_c, kf_c, lfb_c, kb_c, i_c):
    b, n, _ = q_l.shape
    nc = lff_c.shape[1]
    nbl, nbc = n // HG_BLOCK, nc // HG_BLOCK
    out_blocks = min(64, nbl)
    assert nbl % (2 * HG_GROUP) == 0 and nbc % HG_GROUP == 0 and nbl % out_blocks == 0
    spec = lambda rows: pl.BlockSpec((1, rows, HG_DK), lambda i, h: (i, 0, h))
    slot = lambda dt: pltpu.VMEM((2, 2, HG_GROUP * HG_BLOCK, HG_DK), dt)
    return pl.pallas_call(
        functools.partial(_hgrn_kernel, nbl=nbl, nbc=nbc, out_blocks=out_blocks),
        out_shape=jax.ShapeDtypeStruct((b, n, D_MODEL), BF16),
        grid=(b, HG_HEADS),
        in_specs=[pl.BlockSpec((1, HG_DK), lambda i, h: (0, 0))] + [spec(n)] * 7 + [spec(nc)] * 5,
        out_specs=spec(n),
        scratch_shapes=[pltpu.VMEM((n, 2 * HG_DK), BF16),
                        pltpu.VMEM((nbl, HG_BLOCK, HG_BLOCK), BF16),
                        pltpu.VMEM((nbl, HG_DK, 2 * HG_DK), F32),
                        pltpu.VMEM((nbl, 1, 2 * HG_DK), F32),
                        pltpu.VMEM((nbc, HG_DK, 2 * HG_DK), F32),
                        pltpu.VMEM((nbc, 1, 2 * HG_DK), F32),
                        slot(F32), slot(BF16), slot(BF16), slot(BF16)],
        compiler_params=_cp("parallel", "parallel"),
        name="hgrn2",
    )(norm_g, q_l, lff_l, kf_l, lfb_l, kb_l, i_l, sg_l, lff_c, kf_c, lfb_c, kb_c, i_c)


def _rope_tables(n_tok):
    tok = jnp.arange(n_tok, dtype=jnp.int32)[:, None]
    row = (tok // GRID_W).astype(F32)
    col = (tok % GRID_W).astype(F32)
    n_freq = RET_DK // 4
    lane = jnp.arange(LANE, dtype=jnp.int32)[None, :]
    j = lane % (2 * n_freq)
    inv = ROPE_BASE ** (-(j % n_freq).astype(F32) / n_freq)
    ang = jnp.where(j < n_freq, row, col) * inv
    sign = jnp.where(lane % RET_DK < RET_DK // 2, -1.0, 1.0)
    return jnp.cos(ang), jnp.sin(ang) * sign


def kernel(x, c, ctx, c_ctx, w_mod, b_mod, norm_mix, norm_mlp, w_mlp_in, w_mlp_out, ab_w_in, ab_w_out, ret_logit, s5_a_re, s5_a_im, s5_log_dt, s5_b_re, s5_b_im, s5_c_re, s5_c_im, s5_d, s5_w_glu, s5_b_glu, hg_w_in, hg_w_out, hg_lb_logits, hg_norm, norm_final):
    b, n, d = x.shape
    nc = ctx.shape[1]
    assert d == D_MODEL and b + 1 <= MOD_ROWS and w_mod.shape[0] == DEPTH == 2
    assert n % 512 == 0 and nc % 256 == 0 and n % GRID_W == 0
    ctx_row = b
    tm_l, tm_c = 512, 256

    cc = jnp.zeros((MOD_ROWS, d), F32).at[:b].set(c).at[b].set(c_ctx)
    mod = _adaln(cc, w_mod, b_mod)

    row2 = lambda a: a.reshape(1, -1)
    w_in0 = ab_w_in[0]
    cos, sin = _rope_tables(n)
    ng0 = row2(norm_mix[0])
    q_l, k_l, v_l, u_l, uz_l, g_l = _inproj0(x, mod, 0, ng0, w_in0, cos, sin, None, tm_l)
    q_c, k_c, v_c, u_c, uz_c, g_c = _inproj0(ctx, mod, 0, ng0, w_in0, None, None, ctx_row, tm_c)

    log_gamma = jax.nn.log_sigmoid(ret_logit[0].astype(F32))
    lg_rows = jnp.broadcast_to(log_gamma.reshape(2 * RET_HEADS, 1), (2 * RET_HEADS, 2 * RET_DK))
    r_l, r_c = _retention(lg_rows, q_l, k_l, v_l, g_l, q_c, k_c, v_c, g_c)

    s5_ops = _s5_prep(s5_a_re[0], s5_a_im[0], s5_log_dt[0], s5_b_re[0], s5_b_im[0], s5_c_re[0], s5_c_im[0])
    y5_l, y5_c = _s5(s5_ops, uz_l, uz_c)

    s5p = (row2(s5_d[0]), s5_w_glu[0].astype(BF16), row2(s5_b_glu[0]))
    wo0 = ab_w_out[0].astype(BF16)
    w1_0, w2_0 = w_mlp_in, w_mlp_out
    nm0 = row2(norm_mlp[0])
    h_l = _mix_mlp(x, r_l, y5_l, u_l, s5p, wo0, mod, 0, nm0, w1_0, w2_0, None, None, tm_l, 1024, "mix_mlp0_lat")
    h_c = _mix_mlp(ctx, r_c, y5_c, u_c, s5p, wo0, mod, 0, nm0, w1_0, w2_0, None, ctx_row, tm_c, 1024, "mix_mlp0_ctx")

    w_in1 = hg_w_in[0]
    ng1 = row2(norm_mix[1])
    lat1 = _inproj1(h_l, mod, 1, ng1, hg_lb_logits, w_in1, True, None, tm_l, "inproj1_lat")
    ctx1 = _inproj1(h_c, mod, 1, ng1, hg_lb_logits, w_in1, False, ctx_row, tm_c, "inproj1_ctx")
    o1 = _hgrn(row2(hg_norm[0]), *lat1, *ctx1)
    return _mix_mlp(h_l, o1, None, None, None, hg_w_out[0].astype(BF16), mod, 1, row2(norm_mlp[1]),
                    w_mlp_in, w_mlp_out, row2(norm_final), None, tm_l, 1024,
                    "mix_mlp1_lat")
```

```python
import functools

import jax
import jax.numpy as jnp
from jax import lax
from jax.experimental import pallas as pl
from jax.experimental.pallas import tpu as pltpu

F32 = jnp.float32
BF16 = jnp.bfloat16

D_MODEL = 1024
DEPTH = 2
GRID_W = 64
EPS = 1e-6
ROPE_BASE = 10000.0
N_MOD = 6
RET_HEADS = 4
RET_DK = 64
RET_DV = 128
RET_QK = RET_HEADS * RET_DK
RET_WIDTH = RET_HEADS * RET_DV
RET_CHUNK = 128
S5_WIDTH = D_MODEL - RET_WIDTH
S5_GROUP = 16
S5_GROUPS = S5_WIDTH // S5_GROUP
S5_STATE = 64
S5_CHUNK = 16
S5_ROW = S5_CHUNK * S5_GROUP
S5_PAIR = 2 * S5_ROW
LANE = 128
S5_LB = S5_WIDTH // LANE
S5_GPB = LANE // S5_GROUP
S5_BIG = S5_CHUNK * LANE
S5_HALF = S5_GPB * 2 * S5_STATE
S5_POW = 32
AB_IN = 2 * RET_QK + 2 * RET_WIDTH + S5_WIDTH
HG_HEADS = 8
HG_DK = D_MODEL // HG_HEADS
HG_BLOCK = 64
HG_GROUP = 4
HG_SPLIT = 2
D_FF = 4 * D_MODEL
MOD_ROWS = 16

VMEM_LIMIT_BYTES = 56 * 1024 * 1024


def _cp(*sem):
    return pltpu.CompilerParams(dimension_semantics=sem, vmem_limit_bytes=VMEM_LIMIT_BYTES)


def _dot(a, b):
    return jnp.dot(a, b, preferred_element_type=F32)


def _dot_nt(a, b):
    return lax.dot_general(a, b, (((1,), (1,)), ((), ())), preferred_element_type=F32)


def _dot_tn(a, b):
    return lax.dot_general(a, b, (((0,), (0,)), ((), ())), preferred_element_type=F32)


def _sigmoid(x):
    return 0.5 * jnp.tanh(0.5 * x) + 0.5


def _rms(x):
    return x * lax.rsqrt(jnp.mean(x * x, axis=-1, keepdims=True) + EPS)


def _mod_chunk(mod_ref, row, i):
    return mod_ref[0, pl.ds(row, 1), i * D_MODEL:(i + 1) * D_MODEL]


def _adaln_kernel(cc_ref, w_ref, b_ref, o_ref):
    s = jax.nn.silu(cc_ref[...]).astype(BF16)
    o_ref[0] = _dot(s, w_ref[0].astype(BF16)) + b_ref[0]


def _adaln(cc, w_mod, b_mod):
    bn = 1536
    n = N_MOD * D_MODEL
    return pl.pallas_call(
        _adaln_kernel,
        out_shape=jax.ShapeDtypeStruct((DEPTH, MOD_ROWS, n), F32),
        grid=(DEPTH, n // bn),
        in_specs=[
            pl.BlockSpec((MOD_ROWS, D_MODEL), lambda l, j: (0, 0)),
            pl.BlockSpec((1, D_MODEL, bn), lambda l, j: (l, 0, j)),
            pl.BlockSpec((1, 1, bn), lambda l, j: (l, 0, j)),
        ],
        out_specs=pl.BlockSpec((1, MOD_ROWS, bn), lambda l, j: (l, 0, j)),
        compiler_params=_cp("parallel", "parallel"),
        name="adaln",
    )(cc, w_mod, b_mod.reshape(DEPTH, 1, n))


def _rope(t, cos, sin):
    lane = lax.broadcasted_iota(jnp.int32, t.shape, 1)
    first = (lane & (RET_DK // 2)) == 0
    w = t.shape[1]
    swapped = jnp.where(first, pltpu.roll(t, w - RET_DK // 2, 1), pltpu.roll(t, RET_DK // 2, 1))
    return t * cos + swapped * sin


def _inproj0_kernel(*refs, mod_row, rope):
    if rope:
        h_ref, mod_ref, ng_ref, w_ref, cos_ref, sin_ref, q_ref, k_ref, v_ref, u_ref, uz_ref, g_ref, u_s = refs
    else:
        h_ref, mod_ref, ng_ref, w_ref, q_ref, k_ref, v_ref, u_ref, uz_ref, g_ref, u_s = refs
    row = pl.program_id(0) if mod_row is None else mod_row
    xn = _rms(h_ref[0]) * ng_ref[...]
    xm = (xn * (1.0 + _mod_chunk(mod_ref, row, 1)) + _mod_chunk(mod_ref, row, 0)).astype(BF16)
    y = _dot(xm, w_ref[...].astype(BF16))
    q = y[:, 0:RET_QK]
    k = y[:, RET_QK:2 * RET_QK]
    if rope:
        cos = jnp.concatenate([cos_ref[...]] * (RET_QK // LANE), axis=1)
        sin = jnp.concatenate([sin_ref[...]] * (RET_QK // LANE), axis=1)
        q = _rope(q, cos, sin)
        k = _rope(k, cos, sin)
    q_ref[0] = q.astype(BF16)
    k_ref[0] = (k * (RET_DK ** -0.5)).astype(BF16)
    c0 = 2 * RET_QK
    v_ref[0] = y[:, c0:c0 + RET_WIDTH].astype(BF16)
    u0 = c0 + RET_WIDTH
    u_ref[0] = y[:, u0:u0 + S5_WIDTH]
    g = y[:, u0 + S5_WIDTH:]
    g_ref[0] = (g * _sigmoid(g)).astype(BF16)
    nch = u_s.shape[1] // S5_CHUNK
    for j in range(S5_LB):
        u_s[j] = y[:, u0 + j * LANE:u0 + (j + 1) * LANE]
        for s in range(S5_CHUNK):
            uz_ref[0, j, :, s * LANE:(s + 1) * LANE] = u_s[j, pl.ds(s, nch, stride=S5_CHUNK), :].astype(BF16)


def _inproj0(h, mod, layer, ng, w, cos, sin, mod_row, tm):
    b, n, _ = h.shape
    rope = cos is not None
    row_spec = lambda width: pl.BlockSpec((1, tm, width), lambda i, j: (i, j, 0))
    in_specs = [
        row_spec(D_MODEL),
        pl.BlockSpec((1, MOD_ROWS, N_MOD * D_MODEL), lambda i, j: (layer, 0, 0)),
        pl.BlockSpec((1, D_MODEL), lambda i, j: (0, 0)),
        pl.BlockSpec((D_MODEL, AB_IN), lambda i, j: (0, 0), pipeline_mode=pl.Buffered(1)),
    ]
    args = [h, mod, ng, w]
    if rope:
        in_specs += [pl.BlockSpec((tm, LANE), lambda i, j: (j, 0))] * 2
        args += [cos, sin]
    widths = (RET_QK, RET_QK, RET_WIDTH, S5_WIDTH, RET_WIDTH)
    dtypes = (BF16, BF16, BF16, F32, BF16)
    out_shape = [jax.ShapeDtypeStruct((b, n, wd), dt) for wd, dt in zip(widths, dtypes)]
    out_specs = [row_spec(wd) for wd in widths]
    out_shape.insert(4, jax.ShapeDtypeStruct((b, S5_LB, n // S5_CHUNK, S5_BIG), BF16))
    out_specs.insert(4, pl.BlockSpec((1, S5_LB, tm // S5_CHUNK, S5_BIG), lambda i, j: (i, 0, j, 0)))
    return pl.pallas_call(
        functools.partial(_inproj0_kernel, mod_row=mod_row, rope=rope),
        out_shape=out_shape,
        grid=(b, n // tm),
        in_specs=in_specs,
        out_specs=out_specs,
        scratch_shapes=[pltpu.VMEM((S5_LB, tm, LANE), F32)],
        compiler_params=_cp("parallel", "parallel"),
        name="inproj0_lat" if rope else "inproj0_ctx",
    )(*args)


def _ret_kernel(lg_ref, ql, kl, vl, gl, qc, kc, vc, gc, rl, rc, st_s, *, ncl, ncc, unroll):
    c = RET_CHUNK
    dk2 = 2 * RET_DK
    nt = ncc + ncl
    p = pl.program_id(1)
    h_a = 2 * p
    lgf_a = lg_ref[pl.ds(h_a, 1), :]
    lgf_b = lg_ref[pl.ds(h_a + 1, 1), :]
    lgb_a = lg_ref[pl.ds(RET_HEADS + h_a, 1), :]
    lgb_b = lg_ref[pl.ds(RET_HEADS + h_a + 1, 1), :]
    lane = lax.broadcasted_iota(jnp.int32, (1, 2 * RET_DK), 1)
    is_a = lane < RET_DK
    lgf_lane = jnp.where(is_a, lgf_a, lgf_b)
    lgb_lane = jnp.where(is_a, lgb_a, lgb_b)
    ri = lax.broadcasted_iota(jnp.int32, (c, c), 0).astype(F32)
    ci = lax.broadcasted_iota(jnp.int32, (c, c), 1).astype(F32)
    diff = ri - ci

    def dmat(lgf, lgb):
        fwd = jnp.exp(jnp.maximum(diff, 0.0) * lgf)
        bwd = jnp.exp(jnp.maximum(-diff, 0.0) * lgb)
        return jnp.where(diff > 0, fwd, jnp.where(diff < 0, bwd, 2.0))

    d_a = dmat(lgf_a, lgb_a)
    d_b = dmat(lgf_b, lgb_b)
    rowp = lax.broadcasted_iota(jnp.int32, (c, dk2), 0).astype(F32)
    qd = jnp.concatenate([jnp.exp((rowp + 1.0) * lgf_lane), jnp.exp((c - rowp) * lgb_lane)], axis=1)
    kd = jnp.concatenate([jnp.exp((c - 1.0 - rowp) * lgf_lane), jnp.exp(rowp * lgb_lane)], axis=1)
    rowk = lax.broadcasted_iota(jnp.int32, (dk2, 2 * RET_DV), 0)
    cd_f = jnp.exp(c * jnp.where(rowk < RET_DK, lgf_a[:, :1], lgf_b[:, :1]))
    cd_b = jnp.exp(c * jnp.where(rowk < RET_DK, lgb_a[:, :1], lgb_b[:, :1]))
    mask2 = jnp.concatenate([is_a, is_a], axis=1)

    def increment(k, v, slot):
        kk = jnp.concatenate([k, k], axis=1).astype(F32) * kd
        st_s[slot] = _dot_tn(kk.astype(BF16), v)

    for n in range(ncc):
        increment(kc[0, n * c:(n + 1) * c, :], vc[0, n * c:(n + 1) * c, :], n)

    def inc_body(i, carry):
        for j in range(unroll):
            n = i * unroll + j
            sl = pl.ds(pl.multiple_of(n * c, c), c)
            increment(kl[0, sl, :], vl[0, sl, :], ncc + n)
        return carry

    lax.fori_loop(0, ncl // unroll, inc_body, 0)

    def rec_body(t, carry):
        sf, sb = carry
        u = jnp.where(t < ncc, ncc - 1 - t, nt - 1 - (t - ncc))
        inc_f = st_s[t, 0:dk2, :]
        inc_b = st_s[u, dk2:2 * dk2, :]
        st_s[t, 0:dk2, :] = sf
        st_s[u, dk2:2 * dk2, :] = sb
        return cd_f * sf + inc_f, cd_b * sb + inc_b

    zero = jnp.zeros((dk2, 2 * RET_DV), F32)
    lax.fori_loop(0, nt, rec_body, (zero, zero), unroll=True)

    def output(q, k, v, g, slot, out_ref, st):
        q2 = (jnp.concatenate([q, q], axis=1).astype(F32) * qd).astype(BF16)
        s_n = st_s[slot].astype(BF16)
        for keep_a, dm, cs in ((True, d_a, 0), (False, d_b, RET_DV)):
            m1 = is_a if keep_a else jnp.logical_not(is_a)
            m2 = mask2 if keep_a else jnp.logical_not(mask2)
            att = _dot_nt(jnp.where(m1, q, jnp.zeros_like(q)), k) * dm
            o = (_dot(att.astype(BF16), v[:, cs:cs + RET_DV])
                 + _dot(jnp.where(m2, q2, jnp.zeros_like(q2)), s_n[:, cs:cs + RET_DV]))
            o = _rms(o) * g[:, cs:cs + RET_DV].astype(F32)
            out_ref[0, pl.ds(st, c), cs:cs + RET_DV] = o.astype(BF16)

    for n in range(ncc):
        sl = slice(n * c, (n + 1) * c)
        output(qc[0, sl, :], kc[0, sl, :], vc[0, sl, :], gc[0, sl, :], n, rc, n * c)

    def out_body(i, carry):
        for j in range(unroll):
            n = i * unroll + j
            st = pl.multiple_of(n * c, c)
            sl = pl.ds(st, c)
            output(ql[0, sl, :], kl[0, sl, :], vl[0, sl, :], gl[0, sl, :], ncc + n, rl, st)
        return carry

    lax.fori_loop(0, ncl // unroll, out_body, 0)


def _retention(lg_rows, q_l, k_l, v_l, g_l, q_c, k_c, v_c, g_c):
    b, n, _ = q_l.shape
    nc = q_c.shape[1]
    ncl, ncc = n // RET_CHUNK, nc // RET_CHUNK
    pairs = RET_HEADS // 2
    unroll = ncl

    def spec(rows, width):
        return pl.BlockSpec((1, rows, width), lambda i, p: (i, 0, p))

    return pl.pallas_call(
        functools.partial(_ret_kernel, ncl=ncl, ncc=ncc, unroll=unroll),
        out_shape=[jax.ShapeDtypeStruct((b, n, RET_WIDTH), BF16),
                   jax.ShapeDtypeStruct((b, nc, RET_WIDTH), BF16)],
        grid=(b, pairs),
        in_specs=[pl.BlockSpec((2 * RET_HEADS, 2 * RET_DK), lambda i, p: (0, 0)),
                  spec(n, 2 * RET_DK), spec(n, 2 * RET_DK), spec(n, 2 * RET_DV), spec(n, 2 * RET_DV),
                  spec(nc, 2 * RET_DK), spec(nc, 2 * RET_DK), spec(nc, 2 * RET_DV), spec(nc, 2 * RET_DV)],
        out_specs=[spec(n, 2 * RET_DV), spec(nc, 2 * RET_DV)],
        scratch_shapes=[pltpu.VMEM((ncl + ncc, 4 * RET_DK, 2 * RET_DV), F32)],
        compiler_params=_cp("parallel", "parallel"),
        name="retention",
    )(lg_rows, q_l, k_l, v_l, g_l, q_c, k_c, v_c, g_c)


def _dot_hi(a, b, contract=(1, 0)):
    dims = (((contract[0],), (contract[1],)), ((), ()))
    return lax.dot_general(a, b, dims, preferred_element_type=F32, precision=lax.Precision.HIGHEST)


def _dot_sel(a, b, contract, data):
    dims = (((contract[0],), (contract[1],)), ((), ()))
    x = (a, b)[data]
    hi = x.astype(BF16)
    lo = (x - hi.astype(F32)).astype(BF16)
    dd = lambda piece: lax.dot_general(*((piece, b) if data == 0 else (a, piece)), dims, preferred_element_type=F32)
    return dd(hi) + dd(lo)


def _s5_prep_kernel(ar_row, ai_row, ldt, btr, bti, ctr, cti, rm, rwin, rwout, abig):
    t, g, p, kp = S5_CHUNK, S5_GROUP, S5_STATE, S5_POW
    row = S5_ROW
    i0 = lambda shape: lax.broadcasted_iota(jnp.int32, shape, 0)
    i1 = lambda shape: lax.broadcasted_iota(jnp.int32, shape, 1)
    f32 = lambda m: jnp.where(m, 1.0, 0.0).astype(BF16)
    s_of_r = i0((row, kp)) // g
    k_of_l = i1((row, kp))
    sel_rows = (f32(k_of_l == t - 1 - s_of_r), f32(k_of_l == s_of_r))
    t_of_c = i1((kp, row)) // g
    k_of_s = i0((kp, row))
    sel_out = (f32(k_of_s == t_of_c + 1), f32(k_of_s == t - t_of_c))
    sel_lag = (f32(k_of_s == t_of_c), f32(k_of_s == t - 1 - t_of_c))
    tile_l = f32(i1((g, row)) % g == i0((g, row)))
    tile_r = f32(i0((row, g)) % g == i1((row, g)))
    lane = i1((g, row))
    k_col = i0((kp, 1)).astype(F32)
    k_row = i1((1, kp)).astype(F32)
    first = i0((8, 1)) == 0

    def outer(a, k):
        a8 = jnp.where(first, jnp.broadcast_to(a, (8, a.shape[1])), 0.0)
        return _dot_hi(a8, jnp.broadcast_to(k, (8, k.shape[1])), (0, 0))

    rwin[...] = jnp.zeros(rwin.shape, BF16)
    rwout[...] = jnp.zeros(rwout.shape, BF16)
    for q in range(S5_GPB):
        pair, qq = divmod(q, 2)
        lags = []
        for d in range(2):
            dt = jnp.exp(ldt[d, q])
            are_r, aim_r = ar_row[d, q], ai_row[d, q]
            mag = jnp.exp(are_r * dt)
            ang = aim_r * dt
            nr, ni = mag * jnp.cos(ang) - 1.0, mag * jnp.sin(ang)
            den = jnp.square(are_r) + jnp.square(aim_r)
            fr = (nr * are_r + ni * aim_r) / den
            fi = (ni * are_r - nr * aim_r) / den
            pm = jnp.exp(k_col * (are_r * dt))
            pa = k_col * ang
            pk_re, pk_im = pm * jnp.cos(pa), pm * jnp.sin(pa)
            pmt = jnp.exp(outer(are_r * dt, k_row))
            pat = outer(ang, k_row)
            pt_re, pt_im = pmt * jnp.cos(pat), pmt * jnp.sin(pat)
            bt_re, bt_im = _dot_sel(tile_r, btr[d, q], (1, 1), 1), _dot_sel(tile_r, bti[d, q], (1, 1), 1)
            bb_re = fr * bt_re - fi * bt_im
            bb_im = fr * bt_im + fi * bt_re
            pr_re, pr_im = _dot_sel(sel_rows[d], pk_re, (1, 0), 1), _dot_sel(sel_rows[d], pk_im, (1, 0), 1)
            w_re = pr_re * bb_re - pr_im * bb_im
            w_im = pr_re * bb_im + pr_im * bb_re
            for part, w in ((d, w_re), (2 + d, w_im)):
                c0 = part * LANE + qq * p
                rwin[0, q * row:(q + 1) * row, c0:c0 + p] = w.astype(BF16)
            ct_re, ct_im = _dot_sel(ctr[d, q], tile_l, (0, 0), 0), _dot_sel(cti[d, q], tile_l, (0, 0), 0)

            def c_pow(sel):
                pc_re, pc_im = _dot_sel(pt_re, sel, (1, 0), 0), _dot_sel(pt_im, sel, (1, 0), 0)
                return ct_re * pc_re - ct_im * pc_im, ct_re * pc_im + ct_im * pc_re

            o_re, o_im = c_pow(sel_out[d])
            for part, o in ((d, o_re), (2 + d, -o_im)):
                r0 = pair * S5_PAIR + part * LANE + qq * p
                rwout[0, r0:r0 + p, qq * row:(qq + 1) * row] = o.astype(BF16)
            r0 = d * S5_GPB * p + q * p
            l_re, l_im = c_pow(sel_lag[d])
            lags.append(_dot_hi(bb_re[0:g], l_re) - _dot_hi(bb_im[0:g], l_im))
            abig[0, 0:1, r0:r0 + p] = pk_re[t:t + 1, :]
            abig[0, 1:2, r0:r0 + p] = pk_im[t:t + 1, :]
        for s in range(t):
            fwd = jnp.where(lane >= g * s, pltpu.roll(lags[0], g * s, 1), 0.0)
            bwd = jnp.where(lane < g * (s + 1), pltpu.roll(lags[1], (row - g * (t - 1 - s)) % row, 1), 0.0)
            rm[0, q * row + s * g:q * row + (s + 1) * g, :] = (fwd + bwd).astype(BF16)


def _s5_prep(a_re, a_im, log_dt, b_re, b_im, c_re, c_im):
    gg, p, g = S5_GROUPS, S5_STATE, S5_GROUP
    f = lambda x: x.astype(F32)
    args = (f(a_re).reshape(2, gg, 1, p), f(a_im).reshape(2, gg, 1, p),
            f(log_dt).reshape(2, gg, 1, 1),
            f(b_re), f(b_im), f(c_re), f(c_im))
    spec = lambda r, c: pl.BlockSpec((2, S5_GPB, r, c), lambda j: (0, j, 0, 0))
    out = lambda r, c: pl.BlockSpec((1, r, c), lambda j: (j, 0, 0))
    return pl.pallas_call(
        _s5_prep_kernel,
        out_shape=[jax.ShapeDtypeStruct((S5_LB, S5_BIG, S5_ROW), BF16),
                   jax.ShapeDtypeStruct((S5_LB, S5_BIG, S5_PAIR), BF16),
                   jax.ShapeDtypeStruct((S5_LB, S5_BIG, S5_PAIR), BF16),
                   jax.ShapeDtypeStruct((S5_LB, 2, S5_HALF), F32)],
        grid=(S5_LB,),
        in_specs=[spec(1, p), spec(1, p), spec(1, 1),
                  spec(p, g), spec(p, g), spec(g, p), spec(g, p)],
        out_specs=[out(S5_BIG, S5_ROW), out(S5_BIG, S5_PAIR), out(S5_BIG, S5_PAIR), out(2, S5_HALF)],
        compiler_params=_cp("parallel"),
        name="s5_prep",
    )(*args)


def _s5_block_transpose(v):
    n = len(v)
    blk = lax.broadcasted_iota(jnp.int32, v[0].shape, 1) // S5_GROUP
    d = n // 2
    while d >= 1:
        upper = (blk // d) % 2 == 1
        out = list(v)
        for i in range(n):
            if (i // d) % 2 == 0:
                out[i] = jnp.where(upper, pltpu.roll(v[i + d], d * S5_GROUP, 1), v[i])
                out[i + d] = jnp.where(upper, v[i + d], pltpu.roll(v[i], LANE - d * S5_GROUP, 1))
        v = out
        d //= 2
    return v


def _s5_kernel(ul, uc, rm_ref, rwin_ref, rwout_ref, a_ref, yl, yc, x_s, *, ncl, ncc):
    hw = S5_HALF
    hh = hw // 2
    gpb, half = S5_GPB, S5_CHUNK // 2
    cols = lambda x, i: x[:, i * LANE:(i + 1) * LANE]

    z = jnp.concatenate([uc[0, 0], ul[0, 0]], axis=0).astype(F32)
    zt = [_s5_block_transpose([cols(z, half * h + a) for a in range(half)]) for h in range(2)]
    zg = [jnp.concatenate([zt[0][q], zt[1][q]], axis=1).astype(BF16) for q in range(gpb)]
    for r in range(gpb // 2):
        xw = _dot(jnp.concatenate([zg[2 * r], zg[2 * r + 1]], axis=1), rwin_ref[0, r * S5_PAIR:(r + 1) * S5_PAIR, :])
        for k in range(4):
            x_s[:, k * hh + r * LANE:k * hh + (r + 1) * LANE] = cols(xw, k)
    a_re = a_ref[0, 0:1, :]
    a_im = a_ref[0, 1:2, :]

    def segment(base, n, carry):
        s_re, s_im = carry
        for i in range(n):
            rf = slice(base + i, base + i + 1)
            rb = slice(base + n - 1 - i, base + n - i)
            x_re = jnp.concatenate([x_s[rf, 0:hh], x_s[rb, hh:hw]], axis=-1)
            x_im = jnp.concatenate([x_s[rf, hw:hw + hh], x_s[rb, hw + hh:2 * hw]], axis=-1)
            x_s[rf, 0:hh] = s_re[:, 0:hh]
            x_s[rb, hh:hw] = s_re[:, hh:hw]
            x_s[rf, hw:hw + hh] = s_im[:, 0:hh]
            x_s[rb, hw + hh:2 * hw] = s_im[:, hh:hw]
            s_re, s_im = a_re * s_re - a_im * s_im + x_re, a_re * s_im + a_im * s_re + x_im
        return s_re, s_im

    zero = jnp.zeros((1, hw), F32)
    carry = segment(0, ncc, (zero, zero))
    segment(ncc, ncl, carry)

    yt = [[None] * gpb, [None] * gpb]
    for r in range(gpb // 2):
        xs = jnp.concatenate([x_s[:, k * hh + r * LANE:k * hh + (r + 1) * LANE] for k in range(4)], axis=1)
        yp = _dot(xs.astype(BF16), rwout_ref[0, r * S5_PAIR:(r + 1) * S5_PAIR, :])
        for qq in range(2):
            q = 2 * r + qq
            yq = yp[:, qq * S5_ROW:(qq + 1) * S5_ROW] + _dot(zg[q], rm_ref[0, q * S5_ROW:(q + 1) * S5_ROW, :])
            yt[0][q], yt[1][q] = cols(yq, 0), cols(yq, 1)
    for h in range(2):
        for a, y in enumerate(_s5_block_transpose(yt[h])):
            c = slice((half * h + a) * LANE, (half * h + a + 1) * LANE)
            yc[0, 0, :, c] = y[0:ncc]
            yl[0, 0, :, c] = y[ncc:ncc + ncl]


def _s5(prep, zl, zc):
    rm, rwin, rwout, a_big = prep
    b, _, ncl, _ = zl.shape
    ncc = zc.shape[2]
    rows = lambda r: pl.BlockSpec((1, 1, r, S5_BIG), lambda j, i: (i, j, 0, 0))
    wspec = lambda r, c: pl.BlockSpec((1, r, c), lambda j, i: (j, 0, 0))
    return pl.pallas_call(
        functools.partial(_s5_kernel, ncl=ncl, ncc=ncc),
        out_shape=[jax.ShapeDtypeStruct(zl.shape, F32), jax.ShapeDtypeStruct(zc.shape, F32)],
        grid=(S5_LB, b),
        in_specs=[rows(ncl), rows(ncc), wspec(S5_BIG, S5_ROW), wspec(S5_BIG, S5_PAIR),
                  wspec(S5_BIG, S5_PAIR), wspec(2, S5_HALF)],
        out_specs=[rows(ncl), rows(ncc)],
        scratch_shapes=[pltpu.VMEM((ncc + ncl, 2 * S5_HALF), F32)],
        compiler_params=_cp("parallel", "parallel"),
        name="s5",
    )(zl, zc, rm, rwin, rwout, a_big)


def _mix_mlp_kernel(*refs, mod_row, s5_merge, final_norm, fb):
    if s5_merge:
        (h_ref, r_ref, y5_ref, u_ref, ds_ref, wg_ref, bg_ref, wo_ref, mod_ref, nm_ref, w1_ref, w2_ref,
         *rest) = refs
    else:
        h_ref, r_ref, wo_ref, mod_ref, nm_ref, w1_ref, w2_ref, *rest = refs
    if final_norm:
        nf_ref, o_ref, *scratch = rest
    else:
        o_ref, *scratch = rest
    row = pl.program_id(0) if mod_row is None else mod_row
    if s5_merge:
        (y_s,) = scratch
        nch = y_s.shape[1] // S5_CHUNK
        for lb in range(S5_LB):
            for s in range(S5_CHUNK):
                y_s[lb, pl.ds(s, nch, stride=S5_CHUNK), :] = y5_ref[0, lb, :, s * LANE:(s + 1) * LANE]
        y5 = jnp.concatenate([y_s[lb] for lb in range(S5_LB)], axis=-1)
        y = jax.nn.gelu(y5 + ds_ref[...] * u_ref[0])
        y = y * jax.nn.sigmoid(_dot(y.astype(BF16), wg_ref[...]) + bg_ref[...])
        mix = _dot(r_ref[0], wo_ref[0:RET_WIDTH, :]) + _dot(y.astype(BF16), wo_ref[RET_WIDTH:D_MODEL, :])
    else:
        mix = _dot(r_ref[0], wo_ref[...])
    h1 = h_ref[0] + _mod_chunk(mod_ref, row, 2) * mix
    xn = _rms(h1) * nm_ref[...]
    xm = (xn * (1.0 + _mod_chunk(mod_ref, row, 4)) + _mod_chunk(mod_ref, row, 3)).astype(BF16)
    acc = None
    for j in range(D_FF // fb):
        a = jnp.square(jnp.maximum(_dot(xm, w1_ref[:, j * fb:(j + 1) * fb].astype(BF16)), 0.0)).astype(BF16)
        part = _dot(a, w2_ref[j * fb:(j + 1) * fb, :].astype(BF16))
        acc = part if acc is None else acc + part
    h2 = h1 + _mod_chunk(mod_ref, row, 5) * acc
    if final_norm:
        h2 = _rms(h2) * nf_ref[...]
    o_ref[0] = h2


def _mix_mlp(h, r, s5y, u, s5p, wo, mod, layer, nm, w1, w2, nf, mod_row, tm, fb, name):
    b, n, _ = h.shape
    s5_merge = s5y is not None
    final_norm = nf is not None
    one = pl.Buffered(1)
    row_spec = lambda width: pl.BlockSpec((1, tm, width), lambda i, t: (i, t, 0))
    const = lambda shape: pl.BlockSpec(shape, lambda i, t: (0,) * len(shape), pipeline_mode=one)
    in_specs = [row_spec(D_MODEL), row_spec(r.shape[-1])]
    args = [h, r]
    if s5_merge:
        d_skip, w_glu, b_glu = s5p
        z_spec = pl.BlockSpec((1, S5_LB, tm // S5_CHUNK, S5_BIG), lambda i, t: (i, 0, t, 0))
        in_specs += [z_spec, row_spec(S5_WIDTH), const((1, S5_WIDTH)),
                     const((S5_WIDTH, S5_WIDTH)), const((1, S5_WIDTH))]
        args += [s5y, u, d_skip, w_glu, b_glu]
    in_specs += [const((D_MODEL, D_MODEL)),
                 pl.BlockSpec((1, MOD_ROWS, N_MOD * D_MODEL), lambda i, t: (layer, 0, 0), pipeline_mode=one),
                 const((1, D_MODEL))]
    if w1.ndim == 3:
        in_specs += [pl.BlockSpec((None, D_MODEL, D_FF), lambda i, t: (layer, 0, 0), pipeline_mode=one),
                     pl.BlockSpec((None, D_FF, D_MODEL), lambda i, t: (layer, 0, 0), pipeline_mode=one)]
    else:
        in_specs += [const((D_MODEL, D_FF)), const((D_FF, D_MODEL))]
    args += [wo, mod, nm, w1, w2]
    if final_norm:
        in_specs.append(const((1, D_MODEL)))
        args.append(nf)
    return pl.pallas_call(
        functools.partial(_mix_mlp_kernel, mod_row=mod_row, s5_merge=s5_merge, final_norm=final_norm, fb=fb),
        out_shape=jax.ShapeDtypeStruct((b, n, D_MODEL), F32),
        grid=(b, n // tm),
        in_specs=in_specs,
        out_specs=row_spec(D_MODEL),
        scratch_shapes=[pltpu.VMEM((S5_LB, tm, LANE), F32)] if s5_merge else [],
        compiler_params=_cp("parallel", "parallel"),
        name=name,
    )(*args)


def _hgrn_lower_bounds(lbl_ref, layer):
    out = []
    for d in range(2):
        z = [lbl_ref[d, k:k + 1, :] for k in range(DEPTH)]
        zmax = functools.reduce(jnp.maximum, z)
        e = [jnp.exp(v - zmax) for v in z]
        tot = functools.reduce(lambda a, b_: a + b_, e)
        lb = jnp.zeros_like(tot)
        for k in range(1, layer + 1):
            lb = lb + e[k] / tot
        out.append(lb)
    return out


def _inproj1_kernel(h_ref, mod_ref, ng_ref, lbl_ref, w_ref, *out_refs, mod_row, layer, latent):
    row = pl.program_id(0) if mod_row is None else mod_row
    xn = _rms(h_ref[0]) * ng_ref[...]
    xm = (xn * (1.0 + _mod_chunk(mod_ref, row, 1)) + _mod_chunk(mod_ref, row, 0)).astype(BF16)
    lbs = _hgrn_lower_bounds(lbl_ref, layer)
    names = (["q"] if latent else []) + ["lf0", "kk0", "lf1", "kk1", "i"] + (["sg"] if latent else [])
    out = dict(zip(names, out_refs))
    half = h_ref.shape[1] // 2

    def finish(kind, y, rows):
        if kind in ("lf0", "lf1"):
            d = int(kind[2])
            t = (1.0 - lbs[d]) * _sigmoid(y)
            out[kind][0, rows, :] = jnp.log(lbs[d] + t)
            out["kk%d" % d][0, rows, :] = ((1.0 - lbs[d]) - t).astype(BF16)
        elif kind == "sg":
            out[kind][0, rows, :] = (y * _sigmoid(y)).astype(BF16)
        else:
            out[kind][0, rows, :] = y.astype(BF16)

    cols = [("lf0", 1), ("lf1", 2)] + ([("sg", 4), ("q", 0)] if latent else []) + [("i", 3)]
    pending = None
    for kind, k in cols:
        for rows in (slice(0, half), slice(half, 2 * half)):
            y = _dot(xm[rows], w_ref[:, k * D_MODEL:(k + 1) * D_MODEL].astype(BF16))
            if pending is not None:
                finish(*pending)
            pending = (kind, y, rows)
    finish(*pending)


def _inproj1(h, mod, layer, ng, lb_logits, w, latent, mod_row, tm, name):
    b, n, _ = h.shape
    one = pl.Buffered(1)
    row_spec = pl.BlockSpec((1, tm, D_MODEL), lambda i, t: (i, t, 0))
    dtypes = ([BF16] if latent else []) + [F32, BF16, F32, BF16, BF16] + ([BF16] if latent else [])
    return pl.pallas_call(
        functools.partial(_inproj1_kernel, mod_row=mod_row, layer=layer, latent=latent),
        out_shape=[jax.ShapeDtypeStruct((b, n, D_MODEL), dt) for dt in dtypes],
        grid=(b, n // tm),
        in_specs=[row_spec,
                  pl.BlockSpec((1, MOD_ROWS, N_MOD * D_MODEL), lambda i, t: (layer, 0, 0), pipeline_mode=one),
                  pl.BlockSpec((1, D_MODEL), lambda i, t: (0, 0), pipeline_mode=one),
                  pl.BlockSpec(lb_logits.shape, lambda i, t: (0, 0, 0), pipeline_mode=one),
                  pl.BlockSpec(w.shape, lambda i, t: (0, 0), pipeline_mode=one)],
        out_specs=[row_spec] * len(dtypes),
        compiler_params=_cp("parallel", "parallel"),
        name=name,
    )(h, mod, ng, lb_logits, w)


def _cumsum_mm(tri, x):
    acc = None
    r = x
    for i in range(HG_SPLIT):
        p = r.astype(BF16)
        acc = _dot(tri, p) if acc is None else acc + _dot(tri, p)
        if i + 1 < HG_SPLIT:
            r = r - p.astype(F32)
    return acc


def _hgrn_kernel(ng_ref, ql, lffl, kfl, lfbl, kbl, il, sgl, lffc, kfc, lfbc, kbc, ic, o_ref,
                 qin_s, att_s, kv_s, et_s, kvc_s, etc_s, cum_s, ko_s, qt_s, kt_s, *, nbl, nbc, out_blocks):
    cb = HG_BLOCK
    mid = cb // 2
    gb = HG_GROUP
    gr = gb * cb
    dk = HG_DK

    ri = lax.broadcasted_iota(jnp.int32, (gr, gr), 0)
    ci = lax.broadcasted_iota(jnp.int32, (gr, gr), 1)
    same = (ri // cb) == (ci // cb)
    rb = lax.broadcasted_iota(jnp.int32, (cb, cb), 0)
    cbi = lax.broadcasted_iota(jnp.int32, (cb, cb), 1)
    tri_l = jnp.where(same & (ri >= ci), 1.0, 0.0).astype(BF16)
    dirs = ((0, rb >= cbi, mid - 1, cb - 1), (1, rb <= cbi, mid, 0))

    def cumsums(lfs, slot):
        pre = _cumsum_mm(tri_l, jnp.concatenate(lfs, axis=-1))
        pre_b = pre[:, dk:].reshape(gb, cb, dk)
        cum_s[slot, 0] = pre[:, :dk]
        cum_s[slot, 1] = (pre_b[:, cb - 1:cb, :] - pre_b).reshape(gr, dk) + lfs[1]

    def operands(slot, kks, q, n0, et_ref, which=(0, 1)):
        for d, keep, ref_row, tot_row in [dirs[i] for i in which]:
            cum = cum_s[slot, d].reshape(gb, cb, dk)
            kk = kks[d].astype(F32).reshape(gb, cb, dk)
            ref = cum[:, ref_row:ref_row + 1, :]
            tot = cum[:, tot_row:tot_row + 1, :]
            e = cum - ref
            kt = kk * jnp.exp(-e)
            ko_s[slot, d] = (kt * jnp.exp(tot - ref)).astype(BF16).reshape(gr, dk)
            e_tot = jnp.exp(tot)
            for j in range(gb):
                et_ref[n0 + j, :, d * dk:(d + 1) * dk] = e_tot[j]
            if q is not None:
                qt = q.astype(F32).reshape(gb, cb, dk) * jnp.exp(e)
                qin_s[pl.ds(pl.multiple_of(n0 * cb, gr), gr), d * dk:(d + 1) * dk] = (
                    (qt * jnp.exp(ref)).astype(BF16).reshape(gr, dk))
                qt_s[slot, d] = qt.astype(BF16).reshape(gr, dk)
                kt_s[slot, d] = kt.astype(BF16).reshape(gr, dk)

    def matmuls(slot, v, with_q, n0, kv_ref, blocks=tuple(range(HG_GROUP))):
        for d in range(2):
            for j in blocks:
                rows = slice(j * cb, (j + 1) * cb)
                kv_ref[n0 + j, :, d * dk:(d + 1) * dk] = _dot_tn(v[rows], ko_s[slot, d, rows, :])
        if with_q:
            for j in blocks:
                rows = slice(j * cb, (j + 1) * cb)
                att = (jnp.where(dirs[0][1], _dot_nt(qt_s[slot, 0, rows, :], kt_s[slot, 0, rows, :]), 0.0)
                       + jnp.where(dirs[1][1], _dot_nt(qt_s[slot, 1, rows, :], kt_s[slot, 1, rows, :]), 0.0))
                att_s[n0 + j] = att.astype(BF16)

    for g in range(nbc // gb):
        sl = slice(g * gr, (g + 1) * gr)
        cumsums((lffc[0, sl, :], lfbc[0, sl, :]), 0)
        operands(0, (kfc[0, sl, :], kbc[0, sl, :]), None, g * gb, etc_s)
        matmuls(0, ic[0, sl, :], False, g * gb, kvc_s)

    ngl = nbl // gb
    rows_of = lambda g: pl.ds(pl.multiple_of(g * gr, gr), gr)

    def lat_cumsums(g, slot):
        cumsums((lffl[0, rows_of(g), :], lfbl[0, rows_of(g), :]), slot)

    def lat_operands(g, slot, which=(0, 1)):
        operands(slot, (kfl[0, rows_of(g), :], kbl[0, rows_of(g), :]), ql[0, rows_of(g), :], g * gb, et_s, which)

    def lat_matmuls(g, slot, blocks=tuple(range(HG_GROUP))):
        matmuls(slot, il[0, rows_of(g), :], True, g * gb, kv_s, blocks)

    lat_cumsums(0, 0)
    lat_operands(0, 0)
    lat_cumsums(1, 1)

    def prep_body(i, carry):
        g = 2 * i
        half = HG_GROUP // 2
        for a, sa, sb in ((g, 0, 1), (g + 1, 1, 0)):
            lat_matmuls(a, sa, tuple(range(0, half // 2 + half % 2)))
            lat_operands(a + 1, sb, (0,))
            lat_matmuls(a, sa, tuple(range(half // 2 + half % 2, half)))
            lat_cumsums(a + 2, sa)
            lat_matmuls(a, sa, tuple(range(half, half + (HG_GROUP - half) // 2)))
            lat_operands(a + 1, sb, (1,))
            lat_matmuls(a, sa, tuple(range(half + (HG_GROUP - half) // 2, HG_GROUP)))
        return carry

    lax.fori_loop(0, ngl // 2 - 1, prep_body, 0, unroll=True)
    lat_matmuls(ngl - 2, 0)
    lat_operands(ngl - 1, 1)
    lat_matmuls(ngl - 1, 1)

    lane = lax.broadcasted_iota(jnp.int32, (dk, 2 * dk), 1)
    is_f = lane < dk
    st = jnp.zeros((dk, 2 * dk), F32)
    for n in range(nbc):
        m = nbc - 1 - n
        st = (st * jnp.where(is_f[:1], etc_s[n], etc_s[m]) + jnp.where(is_f, kvc_s[n], kvc_s[m]))

    def rec_body(t, st):
        u = nbl - 1 - t
        inc = jnp.where(is_f, kv_s[t], kv_s[u])
        dec = jnp.where(is_f[:1], et_s[t], et_s[u])
        kv_s[t, :, 0:dk] = st[:, 0:dk]
        kv_s[u, :, dk:2 * dk] = st[:, dk:2 * dk]
        return st * dec + inc

    lax.fori_loop(0, nbl, rec_body, st, unroll=True)

    def out_body(i, carry):
        for j in range(out_blocks):
            n = i * out_blocks + j
            sl = pl.ds(pl.multiple_of(n * cb, cb), cb)
            o = _dot(att_s[n], il[0, sl, :]) + _dot_nt(qin_s[sl, :], kv_s[n].astype(BF16))
            o = _rms(o) * ng_ref[...] * sgl[0, sl, :].astype(F32)
            o_ref[0, sl, :] = o.astype(BF16)
        return carry

    lax.fori_loop(0, nbl // out_blocks, out_body, 0)


def _hgrn(norm_g, q_l, lff_l, kf_l, lfb_l, kb_l, i_l, sg_l, lff_c, kf_c, lfb_c, kb_c, i_c):
    b, n, _ = q_l.shape
    nc = lff_c.shape[1]
    nbl, nbc = n // HG_BLOCK, nc // HG_BLOCK
    out_blocks = min(64, nbl)
    assert nbl % (2 * HG_GROUP) == 0 and nbc % HG_GROUP == 0 and nbl % out_blocks == 0
    spec = lambda rows: pl.BlockSpec((1, rows, HG_DK), lambda i, h: (i, 0, h))
    slot = lambda dt: pltpu.VMEM((2, 2, HG_GROUP * HG_BLOCK, HG_DK), dt)
    return pl.pallas_call(
        functools.partial(_hgrn_kernel, nbl=nbl, nbc=nbc, out_blocks=out_blocks),
        out_shape=jax.ShapeDtypeStruct((b, n, D_MODEL), BF16),
        grid=(b, HG_HEADS),
        in_specs=[pl.BlockSpec((1, HG_DK), lambda i, h: (0, 0))] + [spec(n)] * 7 + [spec(nc)] * 5,
        out_specs=spec(n),
        scratch_shapes=[pltpu.VMEM((n, 2 * HG_DK), BF16),
                        pltpu.VMEM((nbl, HG_BLOCK, HG_BLOCK), BF16),
                        pltpu.VMEM((nbl, HG_DK, 2 * HG_DK), F32),
                        pltpu.VMEM((nbl, 1, 2 * HG_DK), F32),
                        pltpu.VMEM((nbc, HG_DK, 2 * HG_DK), F32),
                        pltpu.VMEM((nbc, 1, 2 * HG_DK), F32),
                        slot(F32), slot(BF16), slot(BF16), slot(BF16)],
        compiler_params=_cp("parallel", "parallel"),
        name="hgrn2",
    )(norm_g, q_l, lff_l, kf_l, lfb_l, kb_l, i_l, sg_l, lff_c, kf_c, lfb_c, kb_c, i_c)


def _rope_tables(n_tok):
    tok = jnp.arange(n_tok, dtype=jnp.int32)[:, None]
    row = (tok // GRID_W).astype(F32)
    col = (tok % GRID_W).astype(F32)
    n_freq = RET_DK // 4
    lane = jnp.arange(LANE, dtype=jnp.int32)[None, :]
    j = lane % (2 * n_freq)
    inv = ROPE_BASE ** (-(j % n_freq).astype(F32) / n_freq)
    ang = jnp.where(j < n_freq, row, col) * inv
    sign = jnp.where(lane % RET_DK < RET_DK // 2, -1.0, 1.0)
    return jnp.cos(ang), jnp.sin(ang) * sign


def kernel(x, c, ctx, c_ctx, w_mod, b_mod, norm_mix, norm_mlp, w_mlp_in, w_mlp_out, ab_w_in, ab_w_out, ret_logit, s5_a_re, s5_a_im, s5_log_dt, s5_b_re, s5_b_im, s5_c_re, s5_c_im, s5_d, s5_w_glu, s5_b_glu, hg_w_in, hg_w_out, hg_lb_logits, hg_norm, norm_final):
    b, n, d = x.shape
    nc = ctx.shape[1]
    assert d == D_MODEL and b + 1 <= MOD_ROWS and w_mod.shape[0] == DEPTH == 2
    assert n % 512 == 0 and nc % 256 == 0 and n % GRID_W == 0
    ctx_row = b
    tm_l, tm_c = 512, 256

    cc = jnp.zeros((MOD_ROWS, d), F32).at[:b].set(c).at[b].set(c_ctx)
    mod = _adaln(cc, w_mod, b_mod)

    row2 = lambda a: a.reshape(1, -1)
    w_in0 = ab_w_in[0]
    cos, sin = _rope_tables(n)
    ng0 = row2(norm_mix[0])
    q_l, k_l, v_l, u_l, uz_l, g_l = _inproj0(x, mod, 0, ng0, w_in0, cos, sin, None, tm_l)
    q_c, k_c, v_c, u_c, uz_c, g_c = _inproj0(ctx, mod, 0, ng0, w_in0, None, None, ctx_row, tm_c)

    log_gamma = jax.nn.log_sigmoid(ret_logit[0].astype(F32))
    lg_rows = jnp.broadcast_to(log_gamma.reshape(2 * RET_HEADS, 1), (2 * RET_HEADS, 2 * RET_DK))
    r_l, r_c = _retention(lg_rows, q_l, k_l, v_l, g_l, q_c, k_c, v_c, g_c)

    s5_ops = _s5_prep(s5_a_re[0], s5_a_im[0], s5_log_dt[0], s5_b_re[0], s5_b_im[0], s5_c_re[0], s5_c_im[0])
    y5_l, y5_c = _s5(s5_ops, uz_l, uz_c)

    s5p = (row2(s5_d[0]), s5_w_glu[0].astype(BF16), row2(s5_b_glu[0]))
    wo0 = ab_w_out[0].astype(BF16)
    w1_0, w2_0 = w_mlp_in, w_mlp_out
    nm0 = row2(norm_mlp[0])
    h_l = _mix_mlp(x, r_l, y5_l, u_l, s5p, wo0, mod, 0, nm0, w1_0, w2_0, None, None, tm_l, 1024, "mix_mlp0_lat")
    h_c = _mix_mlp(ctx, r_c, y5_c, u_c, s5p, wo0, mod, 0, nm0, w1_0, w2_0, None, ctx_row, tm_c, 1024, "mix_mlp0_ctx")

    w_in1 = hg_w_in[0]
    ng1 = row2(norm_mix[1])
    lat1 = _inproj1(h_l, mod, 1, ng1, hg_lb_logits, w_in1, True, None, tm_l, "inproj1_lat")
    ctx1 = _inproj1(h_c, mod, 1, ng1, hg_lb_logits, w_in1, False, ctx_row, tm_c, "inproj1_ctx")
    o1 = _hgrn(row2(hg_norm[0]), *lat1, *ctx1)
    return _mix_mlp(h_l, o1, None, None, None, hg_w_out[0].astype(BF16), mod, 1, row2(norm_mlp[1]),
                    w_mlp_in, w_mlp_out, row2(norm_final), None, tm_l, 1024,
                    "mix_mlp1_lat")
```

```python
import functools

import jax
import jax.numpy as jnp
from jax import lax
from jax.experimental import pallas as pl
from jax.experimental.pallas import tpu as pltpu

F32 = jnp.float32
BF16 = jnp.bfloat16

D_MODEL = 1024
DEPTH = 2
GRID_W = 64
EPS = 1e-6
ROPE_BASE = 10000.0
N_MOD = 6
RET_HEADS = 4
RET_DK = 64
RET_DV = 128
RET_QK = RET_HEADS * RET_DK
RET_WIDTH = RET_HEADS * RET_DV
RET_CHUNK = 128
S5_WIDTH = D_MODEL - RET_WIDTH
S5_GROUP = 16
S5_GROUPS = S5_WIDTH // S5_GROUP
S5_STATE = 64
S5_CHUNK = 16
S5_ROW = S5_CHUNK * S5_GROUP
S5_PAIR = 2 * S5_ROW
S5_BATCH_ROWS = 2
LANE = 128
S5_LB = S5_WIDTH // LANE
S5_GPB = LANE // S5_GROUP
S5_BIG = S5_CHUNK * LANE
S5_HALF = S5_GPB * 2 * S5_STATE
S5_POW = 32
AB_IN = 2 * RET_QK + 2 * RET_WIDTH + S5_WIDTH
HG_HEADS = 8
HG_DK = D_MODEL // HG_HEADS
HG_BLOCK = 64
HG_GROUP = 4
HG_SPLIT = 2
D_FF = 4 * D_MODEL
MOD_ROWS = 16

VMEM_LIMIT_BYTES = 56 * 1024 * 1024


def _cp(*sem):
    return pltpu.CompilerParams(dimension_semantics=sem, vmem_limit_bytes=VMEM_LIMIT_BYTES)


def _dot(a, b):
    return jnp.dot(a, b, preferred_element_type=F32)


def _dot_nt(a, b):
    return lax.dot_general(a, b, (((1,), (1,)), ((), ())), preferred_element_type=F32)


def _dot_tn(a, b):
    return lax.dot_general(a, b, (((0,), (0,)), ((), ())), preferred_element_type=F32)


def _sigmoid(x):
    return 0.5 * jnp.tanh(0.5 * x) + 0.5


def _rms(x):
    return x * lax.rsqrt(jnp.mean(x * x, axis=-1, keepdims=True) + EPS)


def _mod_chunk(mod_ref, row, i):
    return mod_ref[0, pl.ds(row, 1), i * D_MODEL:(i + 1) * D_MODEL]


def _adaln_kernel(cc_ref, w_ref, b_ref, o_ref):
    s = jax.nn.silu(cc_ref[...]).astype(BF16)
    o_ref[0] = _dot(s, w_ref[0].astype(BF16)) + b_ref[0]


def _adaln(cc, w_mod, b_mod):
    bn = 1536
    n = N_MOD * D_MODEL
    return pl.pallas_call(
        _adaln_kernel,
        out_shape=jax.ShapeDtypeStruct((DEPTH, MOD_ROWS, n), F32),
        grid=(DEPTH, n // bn),
        in_specs=[
            pl.BlockSpec((MOD_ROWS, D_MODEL), lambda l, j: (0, 0)),
            pl.BlockSpec((1, D_MODEL, bn), lambda l, j: (l, 0, j)),
            pl.BlockSpec((1, 1, bn), lambda l, j: (l, 0, j)),
        ],
        out_specs=pl.BlockSpec((1, MOD_ROWS, bn), lambda l, j: (l, 0, j)),
        compiler_params=_cp("parallel", "parallel"),
        name="adaln",
    )(cc, w_mod, b_mod.reshape(DEPTH, 1, n))


def _rope(t, cos, sin):
    lane = lax.broadcasted_iota(jnp.int32, t.shape, 1)
    first = (lane & (RET_DK // 2)) == 0
    w = t.shape[1]
    swapped = jnp.where(first, pltpu.roll(t, w - RET_DK // 2, 1), pltpu.roll(t, RET_DK // 2, 1))
    return t * cos + swapped * sin


def _inproj0_kernel(*refs, mod_row, rope):
    if rope:
        h_ref, mod_ref, ng_ref, w_ref, cos_ref, sin_ref, q_ref, k_ref, v_ref, u_ref, uz_ref, g_ref, u_s = refs
    else:
        h_ref, mod_ref, ng_ref, w_ref, q_ref, k_ref, v_ref, u_ref, uz_ref, g_ref, u_s = refs
    row = pl.program_id(0) if mod_row is None else mod_row
    xn = _rms(h_ref[0]) * ng_ref[...]
    xm = (xn * (1.0 + _mod_chunk(mod_ref, row, 1)) + _mod_chunk(mod_ref, row, 0)).astype(BF16)
    y = _dot(xm, w_ref[...].astype(BF16))
    q = y[:, 0:RET_QK]
    k = y[:, RET_QK:2 * RET_QK]
    if rope:
        cos = jnp.concatenate([cos_ref[...]] * (RET_QK // LANE), axis=1)
        sin = jnp.concatenate([sin_ref[...]] * (RET_QK // LANE), axis=1)
        q = _rope(q, cos, sin)
        k = _rope(k, cos, sin)
    q_ref[0] = q.astype(BF16)
    k_ref[0] = (k * (RET_DK ** -0.5)).astype(BF16)
    c0 = 2 * RET_QK
    v_ref[0] = y[:, c0:c0 + RET_WIDTH].astype(BF16)
    u0 = c0 + RET_WIDTH
    u_ref[0] = y[:, u0:u0 + S5_WIDTH]
    g = y[:, u0 + S5_WIDTH:]
    g_ref[0] = (g * _sigmoid(g)).astype(BF16)
    nch = u_s.shape[1] // S5_CHUNK
    for j in range(S5_LB):
        u_s[j] = y[:, u0 + j * LANE:u0 + (j + 1) * LANE]
        for s in range(S5_CHUNK):
            uz_ref[0, j, :, s * LANE:(s + 1) * LANE] = u_s[j, pl.ds(s, nch, stride=S5_CHUNK), :].astype(BF16)


def _inproj0(h, mod, layer, ng, w, cos, sin, mod_row, tm):
    b, n, _ = h.shape
    rope = cos is not None
    row_spec = lambda width: pl.BlockSpec((1, tm, width), lambda i, j: (i, j, 0))
    in_specs = [
        row_spec(D_MODEL),
        pl.BlockSpec((1, MOD_ROWS, N_MOD * D_MODEL), lambda i, j: (layer, 0, 0)),
        pl.BlockSpec((1, D_MODEL), lambda i, j: (0, 0)),
        pl.BlockSpec((D_MODEL, AB_IN), lambda i, j: (0, 0), pipeline_mode=pl.Buffered(1)),
    ]
    args = [h, mod, ng, w]
    if rope:
        in_specs += [pl.BlockSpec((tm, LANE), lambda i, j: (j, 0))] * 2
        args += [cos, sin]
    widths = (RET_QK, RET_QK, RET_WIDTH, S5_WIDTH, RET_WIDTH)
    dtypes = (BF16, BF16, BF16, F32, BF16)
    out_shape = [jax.ShapeDtypeStruct((b, n, wd), dt) for wd, dt in zip(widths, dtypes)]
    out_specs = [row_spec(wd) for wd in widths]
    out_shape.insert(4, jax.ShapeDtypeStruct((b, S5_LB, n // S5_CHUNK, S5_BIG), BF16))
    out_specs.insert(4, pl.BlockSpec((1, S5_LB, tm // S5_CHUNK, S5_BIG), lambda i, j: (i, 0, j, 0)))
    return pl.pallas_call(
        functools.partial(_inproj0_kernel, mod_row=mod_row, rope=rope),
        out_shape=out_shape,
        grid=(b, n // tm),
        in_specs=in_specs,
        out_specs=out_specs,
        scratch_shapes=[pltpu.VMEM((S5_LB, tm, LANE), F32)],
        compiler_params=_cp("parallel", "parallel"),
        name="inproj0_lat" if rope else "inproj0_ctx",
    )(*args)


def _ret_kernel(lg_ref, ql, kl, vl, gl, qc, kc, vc, gc, rl, rc, st_s, *, ncl, ncc, unroll):
    c = RET_CHUNK
    dk2 = 2 * RET_DK
    nt = ncc + ncl
    p = pl.program_id(1)
    h_a = 2 * p
    lgf_a = lg_ref[pl.ds(h_a, 1), :]
    lgf_b = lg_ref[pl.ds(h_a + 1, 1), :]
    lgb_a = lg_ref[pl.ds(RET_HEADS + h_a, 1), :]
    lgb_b = lg_ref[pl.ds(RET_HEADS + h_a + 1, 1), :]
    lane = lax.broadcasted_iota(jnp.int32, (1, 2 * RET_DK), 1)
    is_a = lane < RET_DK
    lgf_lane = jnp.where(is_a, lgf_a, lgf_b)
    lgb_lane = jnp.where(is_a, lgb_a, lgb_b)
    ri = lax.broadcasted_iota(jnp.int32, (c, c), 0).astype(F32)
    ci = lax.broadcasted_iota(jnp.int32, (c, c), 1).astype(F32)
    diff = ri - ci

    def dmat(lgf, lgb):
        fwd = jnp.exp(jnp.maximum(diff, 0.0) * lgf)
        bwd = jnp.exp(jnp.maximum(-diff, 0.0) * lgb)
        return jnp.where(diff > 0, fwd, jnp.where(diff < 0, bwd, 2.0))

    d_a = dmat(lgf_a, lgb_a)
    d_b = dmat(lgf_b, lgb_b)
    rowp = lax.broadcasted_iota(jnp.int32, (c, dk2), 0).astype(F32)
    qd = jnp.concatenate([jnp.exp((rowp + 1.0) * lgf_lane), jnp.exp((c - rowp) * lgb_lane)], axis=1)
    kd = jnp.concatenate([jnp.exp((c - 1.0 - rowp) * lgf_lane), jnp.exp(rowp * lgb_lane)], axis=1)
    rowk = lax.broadcasted_iota(jnp.int32, (dk2, 2 * RET_DV), 0)
    cd_f = jnp.exp(c * jnp.where(rowk < RET_DK, lgf_a[:, :1], lgf_b[:, :1]))
    cd_b = jnp.exp(c * jnp.where(rowk < RET_DK, lgb_a[:, :1], lgb_b[:, :1]))
    mask2 = jnp.concatenate([is_a, is_a], axis=1)

    def increment(k, v, slot):
        kk = jnp.concatenate([k, k], axis=1).astype(F32) * kd
        st_s[slot] = _dot_tn(kk.astype(BF16), v)

    for n in range(ncc):
        increment(kc[0, n * c:(n + 1) * c, :], vc[0, n * c:(n + 1) * c, :], n)

    def inc_body(i, carry):
        for j in range(unroll):
            n = i * unroll + j
            sl = pl.ds(pl.multiple_of(n * c, c), c)
            increment(kl[0, sl, :], vl[0, sl, :], ncc + n)
        return carry

    lax.fori_loop(0, ncl // unroll, inc_body, 0)

    def rec_body(t, carry):
        sf, sb = carry
        u = jnp.where(t < ncc, ncc - 1 - t, nt - 1 - (t - ncc))
        inc_f = st_s[t, 0:dk2, :]
        inc_b = st_s[u, dk2:2 * dk2, :]
        st_s[t, 0:dk2, :] = sf
        st_s[u, dk2:2 * dk2, :] = sb
        return cd_f * sf + inc_f, cd_b * sb + inc_b

    zero = jnp.zeros((dk2, 2 * RET_DV), F32)
    lax.fori_loop(0, nt, rec_body, (zero, zero), unroll=True)

    def output(q, k, v, g, slot, out_ref, st):
        q2 = (jnp.concatenate([q, q], axis=1).astype(F32) * qd).astype(BF16)
        s_n = st_s[slot].astype(BF16)
        for keep_a, dm, cs in ((True, d_a, 0), (False, d_b, RET_DV)):
            m1 = is_a if keep_a else jnp.logical_not(is_a)
            m2 = mask2 if keep_a else jnp.logical_not(mask2)
            att = _dot_nt(jnp.where(m1, q, jnp.zeros_like(q)), k) * dm
            o = (_dot(att.astype(BF16), v[:, cs:cs + RET_DV])
                 + _dot(jnp.where(m2, q2, jnp.zeros_like(q2)), s_n[:, cs:cs + RET_DV]))
            o = _rms(o) * g[:, cs:cs + RET_DV].astype(F32)
            out_ref[0, pl.ds(st, c), cs:cs + RET_DV] = o.astype(BF16)

    for n in range(ncc):
        sl = slice(n * c, (n + 1) * c)
        output(qc[0, sl, :], kc[0, sl, :], vc[0, sl, :], gc[0, sl, :], n, rc, n * c)

    def out_body(i, carry):
        for j in range(unroll):
            n = i * unroll + j
            st = pl.multiple_of(n * c, c)
            sl = pl.ds(st, c)
            output(ql[0, sl, :], kl[0, sl, :], vl[0, sl, :], gl[0, sl, :], ncc + n, rl, st)
        return carry

    lax.fori_loop(0, ncl // unroll, out_body, 0)


def _retention(lg_rows, q_l, k_l, v_l, g_l, q_c, k_c, v_c, g_c):
    b, n, _ = q_l.shape
    nc = q_c.shape[1]
    ncl, ncc = n // RET_CHUNK, nc // RET_CHUNK
    pairs = RET_HEADS // 2
    unroll = ncl

    def spec(rows, width):
        return pl.BlockSpec((1, rows, width), lambda i, p: (i, 0, p))

    return pl.pallas_call(
        functools.partial(_ret_kernel, ncl=ncl, ncc=ncc, unroll=unroll),
        out_shape=[jax.ShapeDtypeStruct((b, n, RET_WIDTH), BF16),
                   jax.ShapeDtypeStruct((b, nc, RET_WIDTH), BF16)],
        grid=(b, pairs),
        in_specs=[pl.BlockSpec((2 * RET_HEADS, 2 * RET_DK), lambda i, p: (0, 0)),
                  spec(n, 2 * RET_DK), spec(n, 2 * RET_DK), spec(n, 2 * RET_DV), spec(n, 2 * RET_DV),
                  spec(nc, 2 * RET_DK), spec(nc, 2 * RET_DK), spec(nc, 2 * RET_DV), spec(nc, 2 * RET_DV)],
        out_specs=[spec(n, 2 * RET_DV), spec(nc, 2 * RET_DV)],
        scratch_shapes=[pltpu.VMEM((ncl + ncc, 4 * RET_DK, 2 * RET_DV), F32)],
        compiler_params=_cp("parallel", "parallel"),
        name="retention",
    )(lg_rows, q_l, k_l, v_l, g_l, q_c, k_c, v_c, g_c)


def _dot_hi(a, b, contract=(1, 0)):
    dims = (((contract[0],), (contract[1],)), ((), ()))
    return lax.dot_general(a, b, dims, preferred_element_type=F32, precision=lax.Precision.HIGHEST)


def _dot_sel(a, b, contract, data):
    dims = (((contract[0],), (contract[1],)), ((), ()))
    x = (a, b)[data]
    hi = x.astype(BF16)
    lo = (x - hi.astype(F32)).astype(BF16)
    dd = lambda piece: lax.dot_general(*((piece, b) if data == 0 else (a, piece)), dims, preferred_element_type=F32)
    return dd(hi) + dd(lo)


def _s5_prep_kernel(ar_row, ai_row, ldt, btr, bti, ctr, cti, rm, rwin, rwout, abig):
    t, g, p, kp = S5_CHUNK, S5_GROUP, S5_STATE, S5_POW
    row = S5_ROW
    i0 = lambda shape: lax.broadcasted_iota(jnp.int32, shape, 0)
    i1 = lambda shape: lax.broadcasted_iota(jnp.int32, shape, 1)
    f32 = lambda m: jnp.where(m, 1.0, 0.0).astype(BF16)
    s_of_r = i0((row, kp)) // g
    k_of_l = i1((row, kp))
    sel_rows = (f32(k_of_l == t - 1 - s_of_r), f32(k_of_l == s_of_r))
    t_of_c = i1((kp, row)) // g
    k_of_s = i0((kp, row))
    sel_out = (f32(k_of_s == t_of_c + 1), f32(k_of_s == t - t_of_c))
    sel_lag = (f32(k_of_s == t_of_c), f32(k_of_s == t - 1 - t_of_c))
    tile_l = f32(i1((g, row)) % g == i0((g, row)))
    tile_r = f32(i0((row, g)) % g == i1((row, g)))
    lane = i1((g, row))
    k_col = i0((kp, 1)).astype(F32)
    k_row = i1((1, kp)).astype(F32)
    first = i0((8, 1)) == 0

    def outer(a, k):
        a8 = jnp.where(first, jnp.broadcast_to(a, (8, a.shape[1])), 0.0)
        return _dot_hi(a8, jnp.broadcast_to(k, (8, k.shape[1])), (0, 0))

    rwin[...] = jnp.zeros(rwin.shape, BF16)
    rwout[...] = jnp.zeros(rwout.shape, BF16)
    for q in range(S5_GPB):
        pair, qq = divmod(q, 2)
        lags = []
        for d in range(2):
            dt = jnp.exp(ldt[d, q])
            are_r, aim_r = ar_row[d, q], ai_row[d, q]
            mag = jnp.exp(are_r * dt)
            ang = aim_r * dt
            nr, ni = mag * jnp.cos(ang) - 1.0, mag * jnp.sin(ang)
            den = jnp.square(are_r) + jnp.square(aim_r)
            fr = (nr * are_r + ni * aim_r) / den
            fi = (ni * are_r - nr * aim_r) / den
            pm = jnp.exp(k_col * (are_r * dt))
            pa = k_col * ang
            pk_re, pk_im = pm * jnp.cos(pa), pm * jnp.sin(pa)
            pmt = jnp.exp(outer(are_r * dt, k_row))
            pat = outer(ang, k_row)
            pt_re, pt_im = pmt * jnp.cos(pat), pmt * jnp.sin(pat)
            bt_re, bt_im = _dot_sel(tile_r, btr[d, q], (1, 1), 1), _dot_sel(tile_r, bti[d, q], (1, 1), 1)
            bb_re = fr * bt_re - fi * bt_im
            bb_im = fr * bt_im + fi * bt_re
            pr_re, pr_im = _dot_sel(sel_rows[d], pk_re, (1, 0), 1), _dot_sel(sel_rows[d], pk_im, (1, 0), 1)
            w_re = pr_re * bb_re - pr_im * bb_im
            w_im = pr_re * bb_im + pr_im * bb_re
            for part, w in ((d, w_re), (2 + d, w_im)):
                c0 = part * LANE + qq * p
                rwin[0, q * row:(q + 1) * row, c0:c0 + p] = w.astype(BF16)
            ct_re, ct_im = _dot_sel(ctr[d, q], tile_l, (0, 0), 0), _dot_sel(cti[d, q], tile_l, (0, 0), 0)

            def c_pow(sel):
                pc_re, pc_im = _dot_sel(pt_re, sel, (1, 0), 0), _dot_sel(pt_im, sel, (1, 0), 0)
                return ct_re * pc_re - ct_im * pc_im, ct_re * pc_im + ct_im * pc_re

            o_re, o_im = c_pow(sel_out[d])
            for part, o in ((d, o_re), (2 + d, -o_im)):
                r0 = pair * S5_PAIR + part * LANE + qq * p
                rwout[0, r0:r0 + p, qq * row:(qq + 1) * row] = o.astype(BF16)
            r0 = d * S5_GPB * p + q * p
            l_re, l_im = c_pow(sel_lag[d])
            lags.append(_dot_hi(bb_re[0:g], l_re) - _dot_hi(bb_im[0:g], l_im))
            abig[0, 0:1, r0:r0 + p] = pk_re[t:t + 1, :]
            abig[0, 1:2, r0:r0 + p] = pk_im[t:t + 1, :]
        for s in range(t):
            fwd = jnp.where(lane >= g * s, pltpu.roll(lags[0], g * s, 1), 0.0)
            bwd = jnp.where(lane < g * (s + 1), pltpu.roll(lags[1], (row - g * (t - 1 - s)) % row, 1), 0.0)
            rm[0, q * row + s * g:q * row + (s + 1) * g, :] = (fwd + bwd).astype(BF16)


def _s5_prep(a_re, a_im, log_dt, b_re, b_im, c_re, c_im):
    gg, p, g = S5_GROUPS, S5_STATE, S5_GROUP
    f = lambda x: x.astype(F32)
    args = (f(a_re).reshape(2, gg, 1, p), f(a_im).reshape(2, gg, 1, p),
            f(log_dt).reshape(2, gg, 1, 1),
            f(b_re), f(b_im), f(c_re), f(c_im))
    spec = lambda r, c: pl.BlockSpec((2, S5_GPB, r, c), lambda j: (0, j, 0, 0))
    out = lambda r, c: pl.BlockSpec((1, r, c), lambda j: (j, 0, 0))
    return pl.pallas_call(
        _s5_prep_kernel,
        out_shape=[jax.ShapeDtypeStruct((S5_LB, S5_BIG, S5_ROW), BF16),
                   jax.ShapeDtypeStruct((S5_LB, S5_BIG, S5_PAIR), BF16),
                   jax.ShapeDtypeStruct((S5_LB, S5_BIG, S5_PAIR), BF16),
                   jax.ShapeDtypeStruct((S5_LB, 2, S5_HALF), F32)],
        grid=(S5_LB,),
        in_specs=[spec(1, p), spec(1, p), spec(1, 1),
                  spec(p, g), spec(p, g), spec(g, p), spec(g, p)],
        out_specs=[out(S5_BIG, S5_ROW), out(S5_BIG, S5_PAIR), out(S5_BIG, S5_PAIR), out(2, S5_HALF)],
        compiler_params=_cp("parallel"),
        name="s5_prep",
    )(*args)


def _s5_block_transpose(v):
    n = len(v)
    blk = lax.broadcasted_iota(jnp.int32, (1, LANE), 1) // S5_GROUP

    def spread(src, e):
        w = 1
        while w < n:
            bit = (blk // w) % 2 == 1
            src = [jnp.where(bit, src[(m + e * w) % n], src[m]) for m in range(n)]
            w *= 2
        return src

    diag = spread(v, 1)
    rolled = [diag[0]] + [pltpu.roll(diag[(-k) % n], LANE - k * S5_GROUP, 1) for k in range(1, n)]
    return spread(rolled, -1)


def _s5_kernel(ul, uc, rm_ref, rwin_ref, rwout_ref, a_ref, yl, yc, x_s, *, ncl, ncc):
    hw = S5_HALF
    gpb, half = S5_GPB, S5_CHUNK // 2
    cols = lambda x, i: x[:, i * LANE:(i + 1) * LANE]

    nb = ul.shape[0]
    nt = ncc + ncl
    pairs = gpb // 2
    slab = lambda k, r: k * pairs + r
    of_row = lambda i: pl.ds(i, nt, stride=nb)

    zg = []
    for i in range(nb):
        z = jnp.concatenate([uc[i, 0], ul[i, 0]], axis=0)
        zt = [_s5_block_transpose([cols(z, half * h + a) for a in range(half)]) for h in range(2)]
        zg.append([jnp.concatenate([zt[0][q], zt[1][q]], axis=1) for q in range(gpb)])
        for r in range(pairs):
            xw = _dot(jnp.concatenate([zg[i][2 * r], zg[i][2 * r + 1]], axis=1),
                      rwin_ref[0, r * S5_PAIR:(r + 1) * S5_PAIR, :])
            for k in range(4):
                x_s[slab(k, r), of_row(i), :] = cols(xw, k)
    a_re = a_ref[0, 0:1, :]
    a_im = a_ref[0, 1:2, :]

    def segment(base, n, carry):
        s_re, s_im = carry
        for i in range(n):
            rf = slice((base + i) * nb, (base + i + 1) * nb)
            rb = slice((base + n - 1 - i) * nb, (base + n - i) * nb)
            rows_of = (rf,) * pairs + (rb,) * pairs
            x_re = jnp.concatenate([x_s[c, rows_of[c], :] for c in range(2 * pairs)], axis=-1)
            x_im = jnp.concatenate([x_s[2 * pairs + c, rows_of[c], :] for c in range(2 * pairs)], axis=-1)
            for c in range(2 * pairs):
                x_s[c, rows_of[c], :] = cols(s_re, c)
                x_s[2 * pairs + c, rows_of[c], :] = cols(s_im, c)
            s_re, s_im = a_re * s_re - a_im * s_im + x_re, a_re * s_im + a_im * s_re + x_im
        return s_re, s_im

    zero = jnp.zeros((nb, hw), F32)
    carry = segment(0, ncc, (zero, zero))
    segment(ncc, ncl, carry)

    for i in range(nb):
        yt = [[None] * gpb, [None] * gpb]
        for r in range(pairs):
            xs = jnp.concatenate([x_s[slab(k, r), of_row(i), :] for k in range(4)], axis=1)
            yp = _dot(xs.astype(BF16), rwout_ref[0, r * S5_PAIR:(r + 1) * S5_PAIR, :])
            for qq in range(2):
                q = 2 * r + qq
                yq = (yp[:, qq * S5_ROW:(qq + 1) * S5_ROW]
                      + _dot(zg[i][q], rm_ref[0, q * S5_ROW:(q + 1) * S5_ROW, :]))
                yt[0][q], yt[1][q] = cols(yq, 0), cols(yq, 1)
        for h in range(2):
            for a, y in enumerate(_s5_block_transpose(yt[h])):
                c = slice((half * h + a) * LANE, (half * h + a + 1) * LANE)
                yc[i, 0, :, c] = y[0:ncc]
                yl[i, 0, :, c] = y[ncc:ncc + ncl]


def _s5(prep, zl, zc):
    rm, rwin, rwout, a_big = prep
    b, _, ncl, _ = zl.shape
    ncc = zc.shape[2]
    nb = S5_BATCH_ROWS if b % S5_BATCH_ROWS == 0 else 1
    rows = lambda r: pl.BlockSpec((nb, 1, r, S5_BIG), lambda j, i: (i, j, 0, 0))
    wspec = lambda r, c: pl.BlockSpec((1, r, c), lambda j, i: (j, 0, 0))
    return pl.pallas_call(
        functools.partial(_s5_kernel, ncl=ncl, ncc=ncc),
        out_shape=[jax.ShapeDtypeStruct(zl.shape, F32), jax.ShapeDtypeStruct(zc.shape, F32)],
        grid=(S5_LB, b // nb),
        in_specs=[rows(ncl), rows(ncc), wspec(S5_BIG, S5_ROW), wspec(S5_BIG, S5_PAIR),
                  wspec(S5_BIG, S5_PAIR), wspec(2, S5_HALF)],
        out_specs=[rows(ncl), rows(ncc)],
        scratch_shapes=[pltpu.VMEM((2 * S5_HALF // LANE, (ncc + ncl) * nb, LANE), F32)],
        compiler_params=_cp("parallel", "parallel"),
        name="s5",
    )(zl, zc, rm, rwin, rwout, a_big)


def _mix_mlp_kernel(*refs, mod_row, s5_merge, final_norm, fb):
    if s5_merge:
        (h_ref, r_ref, y5_ref, u_ref, ds_ref, wg_ref, bg_ref, wo_ref, mod_ref, nm_ref, w1_ref, w2_ref,
         *rest) = refs
    else:
        h_ref, r_ref, wo_ref, mod_ref, nm_ref, w1_ref, w2_ref, *rest = refs
    if final_norm:
        nf_ref, o_ref, *scratch = rest
    else:
        o_ref, *scratch = rest
    row = pl.program_id(0) if mod_row is None else mod_row
    if s5_merge:
        (y_s,) = scratch
        nch = y_s.shape[1] // S5_CHUNK
        for lb in range(S5_LB):
            for s in range(S5_CHUNK):
                y_s[lb, pl.ds(s, nch, stride=S5_CHUNK), :] = y5_ref[0, lb, :, s * LANE:(s + 1) * LANE]
        y5 = jnp.concatenate([y_s[lb] for lb in range(S5_LB)], axis=-1)
        y = jax.nn.gelu(y5 + ds_ref[...] * u_ref[0])
        y = y * jax.nn.sigmoid(_dot(y.astype(BF16), wg_ref[...]) + bg_ref[...])
        mix = _dot(r_ref[0], wo_ref[0:RET_WIDTH, :]) + _dot(y.astype(BF16), wo_ref[RET_WIDTH:D_MODEL, :])
    else:
        mix = _dot(r_ref[0], wo_ref[...])
    h1 = h_ref[0] + _mod_chunk(mod_ref, row, 2) * mix
    xn = _rms(h1) * nm_ref[...]
    xm = (xn * (1.0 + _mod_chunk(mod_ref, row, 4)) + _mod_chunk(mod_ref, row, 3)).astype(BF16)
    acc = None
    for j in range(D_FF // fb):
        a = jnp.square(jnp.maximum(_dot(xm, w1_ref[:, j * fb:(j + 1) * fb].astype(BF16)), 0.0)).astype(BF16)
        part = _dot(a, w2_ref[j * fb:(j + 1) * fb, :].astype(BF16))
        acc = part if acc is None else acc + part
    h2 = h1 + _mod_chunk(mod_ref, row, 5) * acc
    if final_norm:
        h2 = _rms(h2) * nf_ref[...]
    o_ref[0] = h2


def _mix_mlp(h, r, s5y, u, s5p, wo, mod, layer, nm, w1, w2, nf, mod_row, tm, fb, name):
    b, n, _ = h.shape
    s5_merge = s5y is not None
    final_norm = nf is not None
    one = pl.Buffered(1)
    row_spec = lambda width: pl.BlockSpec((1, tm, width), lambda i, t: (i, t, 0))
    const = lambda shape: pl.BlockSpec(shape, lambda i, t: (0,) * len(shape), pipeline_mode=one)
    in_specs = [row_spec(D_MODEL), row_spec(r.shape[-1])]
    args = [h, r]
    if s5_merge:
        d_skip, w_glu, b_glu = s5p
        z_spec = pl.BlockSpec((1, S5_LB, tm // S5_CHUNK, S5_BIG), lambda i, t: (i, 0, t, 0))
        in_specs += [z_spec, row_spec(S5_WIDTH), const((1, S5_WIDTH)),
                     const((S5_WIDTH, S5_WIDTH)), const((1, S5_WIDTH))]
        args += [s5y, u, d_skip, w_glu, b_glu]
    in_specs += [const((D_MODEL, D_MODEL)),
                 pl.BlockSpec((1, MOD_ROWS, N_MOD * D_MODEL), lambda i, t: (layer, 0, 0), pipeline_mode=one),
                 const((1, D_MODEL))]
    if w1.ndim == 3:
        in_specs += [pl.BlockSpec((None, D_MODEL, D_FF), lambda i, t: (layer, 0, 0), pipeline_mode=one),
                     pl.BlockSpec((None, D_FF, D_MODEL), lambda i, t: (layer, 0, 0), pipeline_mode=one)]
    else:
        in_specs += [const((D_MODEL, D_FF)), const((D_FF, D_MODEL))]
    args += [wo, mod, nm, w1, w2]
    if final_norm:
        in_specs.append(const((1, D_MODEL)))
        args.append(nf)
    return pl.pallas_call(
        functools.partial(_mix_mlp_kernel, mod_row=mod_row, s5_merge=s5_merge, final_norm=final_norm, fb=fb),
        out_shape=jax.ShapeDtypeStruct((b, n, D_MODEL), F32),
        grid=(b, n // tm),
        in_specs=in_specs,
        out_specs=row_spec(D_MODEL),
        scratch_shapes=[pltpu.VMEM((S5_LB, tm, LANE), F32)] if s5_merge else [],
        compiler_params=_cp("parallel", "parallel"),
        name=name,
    )(*args)


def _hgrn_lower_bounds(lbl_ref, layer):
    out = []
    for d in range(2):
        z = [lbl_ref[d, k:k + 1, :] for k in range(DEPTH)]
        zmax = functools.reduce(jnp.maximum, z)
        e = [jnp.exp(v - zmax) for v in z]
        tot = functools.reduce(lambda a, b_: a + b_, e)
        lb = jnp.zeros_like(tot)
        for k in range(1, layer + 1):
            lb = lb + e[k] / tot
        out.append(lb)
    return out


def _inproj1_kernel(h_ref, mod_ref, ng_ref, lbl_ref, w_ref, *out_refs, mod_row, layer, latent):
    row = pl.program_id(0) if mod_row is None else mod_row
    xn = _rms(h_ref[0]) * ng_ref[...]
    xm = (xn * (1.0 + _mod_chunk(mod_ref, row, 1)) + _mod_chunk(mod_ref, row, 0)).astype(BF16)
    lbs = _hgrn_lower_bounds(lbl_ref, layer)
    names = (["q"] if latent else []) + ["lf0", "kk0", "lf1", "kk1", "i"] + (["sg"] if latent else [])
    out = dict(zip(names, out_refs))
    half = h_ref.shape[1] // 2

    def finish(kind, y, rows):
        if kind in ("lf0", "lf1"):
            d = int(kind[2])
            t = (1.0 - lbs[d]) * _sigmoid(y)
            out[kind][0, rows, :] = jnp.log(lbs[d] + t)
            out["kk%d" % d][0, rows, :] = ((1.0 - lbs[d]) - t).astype(BF16)
        elif kind == "sg":
            out[kind][0, rows, :] = (y * _sigmoid(y)).astype(BF16)
        else:
            out[kind][0, rows, :] = y.astype(BF16)

    cols = [("lf0", 1), ("lf1", 2)] + ([("sg", 4), ("q", 0)] if latent else []) + [("i", 3)]
    pending = None
    for kind, k in cols:
        for rows in (slice(0, half), slice(half, 2 * half)):
            y = _dot(xm[rows], w_ref[:, k * D_MODEL:(k + 1) * D_MODEL].astype(BF16))
            if pending is not None:
                finish(*pending)
            pending = (kind, y, rows)
    finish(*pending)


def _inproj1(h, mod, layer, ng, lb_logits, w, latent, mod_row, tm, name):
    b, n, _ = h.shape
    one = pl.Buffered(1)
    row_spec = pl.BlockSpec((1, tm, D_MODEL), lambda i, t: (i, t, 0))
    dtypes = ([BF16] if latent else []) + [F32, BF16, F32, BF16, BF16] + ([BF16] if latent else [])
    return pl.pallas_call(
        functools.partial(_inproj1_kernel, mod_row=mod_row, layer=layer, latent=latent),
        out_shape=[jax.ShapeDtypeStruct((b, n, D_MODEL), dt) for dt in dtypes],
        grid=(b, n // tm),
        in_specs=[row_spec,
                  pl.BlockSpec((1, MOD_ROWS, N_MOD * D_MODEL), lambda i, t: (layer, 0, 0), pipeline_mode=one),
                  pl.BlockSpec((1, D_MODEL), lambda i, t: (0, 0), pipeline_mode=one),
                  pl.BlockSpec(lb_logits.shape, lambda i, t: (0, 0, 0), pipeline_mode=one),
                  pl.BlockSpec(w.shape, lambda i, t: (0, 0), pipeline_mode=one)],
        out_specs=[row_spec] * len(dtypes),
        compiler_params=_cp("parallel", "parallel"),
        name=name,
    )(h, mod, ng, lb_logits, w)


def _cumsum_mm(tri, x):
    acc = None
    r = x
    for i in range(HG_SPLIT):
        p = r.astype(BF16)
        acc = _dot(tri, p) if acc is None else acc + _dot(tri, p)
        if i + 1 < HG_SPLIT:
            r = r - p.astype(F32)
    return acc


def _hgrn_kernel(ng_ref, ql, lffl, kfl, lfbl, kbl, il, sgl, lffc, kfc, lfbc, kbc, ic, o_ref,
                 qin_s, att_s, kv_s, et_s, kvc_s, etc_s, cum_s, ko_s, qt_s, kt_s, *, nbl, nbc, out_blocks):
    cb = HG_BLOCK
    mid = cb // 2
    gb = HG_GROUP
    gr = gb * cb
    dk = HG_DK

    ri = lax.broadcasted_iota(jnp.int32, (gr, gr), 0)
    ci = lax.broadcasted_iota(jnp.int32, (gr, gr), 1)
    same = (ri // cb) == (ci // cb)
    rb = lax.broadcasted_iota(jnp.int32, (cb, cb), 0)
    cbi = lax.broadcasted_iota(jnp.int32, (cb, cb), 1)
    tri_l = jnp.where(same & (ri >= ci), 1.0, 0.0).astype(BF16)
    dirs = ((0, rb >= cbi, mid - 1, cb - 1), (1, rb <= cbi, mid, 0))

    def cumsums(lfs, slot):
        pre = _cumsum_mm(tri_l, jnp.concatenate(lfs, axis=-1))
        pre_b = pre[:, dk:].reshape(gb, cb, dk)
        cum_s[slot, 0] = pre[:, :dk]
        cum_s[slot, 1] = (pre_b[:, cb - 1:cb, :] - pre_b).reshape(gr, dk) + lfs[1]

    def operands(slot, kks, q, n0, et_ref, which=(0, 1)):
        for d, keep, ref_row, tot_row in [dirs[i] for i in which]:
            cum = cum_s[slot, d].reshape(gb, cb, dk)
            kk = kks[d].astype(F32).reshape(gb, cb, dk)
            ref = cum[:, ref_row:ref_row + 1, :]
            tot = cum[:, tot_row:tot_row + 1, :]
            e = cum - ref
            kt = kk * jnp.exp(-e)
            ko_s[slot, d] = (kt * jnp.exp(tot - ref)).astype(BF16).reshape(gr, dk)
            e_tot = jnp.exp(tot)
            for j in range(gb):
                et_ref[n0 + j, :, d * dk:(d + 1) * dk] = e_tot[j]
            if q is not None:
                qt = q.astype(F32).reshape(gb, cb, dk) * jnp.exp(e)
                qin_s[pl.ds(pl.multiple_of(n0 * cb, gr), gr), d * dk:(d + 1) * dk] = (
                    (qt * jnp.exp(ref)).astype(BF16).reshape(gr, dk))
                qt_s[slot, d] = qt.astype(BF16).reshape(gr, dk)
                kt_s[slot, d] = kt.astype(BF16).reshape(gr, dk)

    def matmuls(slot, v, with_q, n0, kv_ref, blocks=tuple(range(HG_GROUP))):
        for d in range(2):
            for j in blocks:
                rows = slice(j * cb, (j + 1) * cb)
                kv_ref[n0 + j, :, d * dk:(d + 1) * dk] = _dot_tn(v[rows], ko_s[slot, d, rows, :])
        if with_q:
            for j in blocks:
                rows = slice(j * cb, (j + 1) * cb)
                att = (jnp.where(dirs[0][1], _dot_nt(qt_s[slot, 0, rows, :], kt_s[slot, 0, rows, :]), 0.0)
                       + jnp.where(dirs[1][1], _dot_nt(qt_s[slot, 1, rows, :], kt_s[slot, 1, rows, :]), 0.0))
                att_s[n0 + j] = att.astype(BF16)

    for g in range(nbc // gb):
        sl = slice(g * gr, (g + 1) * gr)
        cumsums((lffc[0, sl, :], lfbc[0, sl, :]), 0)
        operands(0, (kfc[0, sl, :], kbc[0, sl, :]), None, g * gb, etc_s)
        matmuls(0, ic[0, sl, :], False, g * gb, kvc_s)

    ngl = nbl // gb
    rows_of = lambda g: pl.ds(pl.multiple_of(g * gr, gr), gr)

    def lat_cumsums(g, slot):
        cumsums((lffl[0, rows_of(g), :], lfbl[0, rows_of(g), :]), slot)

    def lat_operands(g, slot, which=(0, 1)):
        operands(slot, (kfl[0, rows_of(g), :], kbl[0, rows_of(g), :]), ql[0, rows_of(g), :], g * gb, et_s, which)

    def lat_matmuls(g, slot, blocks=tuple(range(HG_GROUP))):
        matmuls(slot, il[0, rows_of(g), :], True, g * gb, kv_s, blocks)

    lat_cumsums(0, 0)
    lat_operands(0, 0)
    lat_cumsums(1, 1)

    def prep_body(i, carry):
        g = 2 * i
        half = HG_GROUP // 2
        for a, sa, sb in ((g, 0, 1), (g + 1, 1, 0)):
            lat_matmuls(a, sa, tuple(range(0, half // 2 + half % 2)))
            lat_operands(a + 1, sb, (0,))
            lat_matmuls(a, sa, tuple(range(half // 2 + half % 2, half)))
            lat_cumsums(a + 2, sa)
            lat_matmuls(a, sa, tuple(range(half, half + (HG_GROUP - half) // 2)))
            lat_operands(a + 1, sb, (1,))
            lat_matmuls(a, sa, tuple(range(half + (HG_GROUP - half) // 2, HG_GROUP)))
        return carry

    lax.fori_loop(0, ngl // 2 - 1, prep_body, 0, unroll=True)
    lat_matmuls(ngl - 2, 0)
    lat_operands(ngl - 1, 1)
    lat_matmuls(ngl - 1, 1)

    lane = lax.broadcasted_iota(jnp.int32, (dk, 2 * dk), 1)
    is_f = lane < dk
    st = jnp.zeros((dk, 2 * dk), F32)
    for n in range(nbc):
        m = nbc - 1 - n
        st = (st * jnp.where(is_f[:1], etc_s[n], etc_s[m]) + jnp.where(is_f, kvc_s[n], kvc_s[m]))

    def rec_body(t, st):
        u = nbl - 1 - t
        inc = jnp.where(is_f, kv_s[t], kv_s[u])
        dec = jnp.where(is_f[:1], et_s[t], et_s[u])
        kv_s[t, :, 0:dk] = st[:, 0:dk]
        kv_s[u, :, dk:2 * dk] = st[:, dk:2 * dk]
        return st * dec + inc

    lax.fori_loop(0, nbl, rec_body, st, unroll=True)

    def out_body(i, carry):
        for j in range(out_blocks):
            n = i * out_blocks + j
            sl = pl.ds(pl.multiple_of(n * cb, cb), cb)
            o = _dot(att_s[n], il[0, sl, :]) + _dot_nt(qin_s[sl, :], kv_s[n].astype(BF16))
            o = _rms(o) * ng_ref[...] * sgl[0, sl, :].astype(F32)
            o_ref[0, sl, :] = o.astype(BF16)
        return carry

    lax.fori_loop(0, nbl // out_blocks, out_body, 0)


def _hgrn(norm_g, q_l, lff_l, kf_l, lfb_l, kb_l, i_l, sg_l, lff_c, kf_c, lfb_c, kb_c, i_c):
    b, n, _ = q_l.shape
    nc = lff_c.shape[1]
    nbl, nbc = n // HG_BLOCK, nc // HG_BLOCK
    out_blocks = min(64, nbl)
    assert nbl % (2 * HG_GROUP) == 0 and nbc % HG_GROUP == 0 and nbl % out_blocks == 0
    spec = lambda rows: pl.BlockSpec((1, rows, HG_DK), lambda i, h: (i, 0, h))
    slot = lambda dt: pltpu.VMEM((2, 2, HG_GROUP * HG_BLOCK, HG_DK), dt)
    return pl.pallas_call(
        functools.partial(_hgrn_kernel, nbl=nbl, nbc=nbc, out_blocks=out_blocks),
        out_shape=jax.ShapeDtypeStruct((b, n, D_MODEL), BF16),
        grid=(b, HG_HEADS),
        in_specs=[pl.BlockSpec((1, HG_DK), lambda i, h: (0, 0))] + [spec(n)] * 7 + [spec(nc)] * 5,
        out_specs=spec(n),
        scratch_shapes=[pltpu.VMEM((n, 2 * HG_DK), BF16),
                        pltpu.VMEM((nbl, HG_BLOCK, HG_BLOCK), BF16),
                        pltpu.VMEM((nbl, HG_DK, 2 * HG_DK), F32),
                        pltpu.VMEM((nbl, 1, 2 * HG_DK), F32),
                        pltpu.VMEM((nbc, HG_DK, 2 * HG_DK), F32),
                        pltpu.VMEM((nbc, 1, 2 * HG_DK), F32),
                        slot(F32), slot(BF16), slot(BF16), slot(BF16)],
        compiler_params=_cp("parallel", "parallel"),
        name="hgrn2",
    )(norm_g, q_l, lff_l, kf_l, lfb_l, kb_l, i_l, sg_l, lff_c, kf_c, lfb_c, kb_c, i_c)


def _rope_tables(n_tok):
    tok = jnp.arange(n_tok, dtype=jnp.int32)[:, None]
    row = (tok // GRID_W).astype(F32)
    col = (tok % GRID_W).astype(F32)
    n_freq = RET_DK // 4
    lane = jnp.arange(LANE, dtype=jnp.int32)[None, :]
    j = lane % (2 * n_freq)
    inv = ROPE_BASE ** (-(j % n_freq).astype(F32) / n_freq)
    ang = jnp.where(j < n_freq, row, col) * inv
    sign = jnp.where(lane % RET_DK < RET_DK // 2, -1.0, 1.0)
    return jnp.cos(ang), jnp.sin(ang) * sign


def kernel(x, c, ctx, c_ctx, w_mod, b_mod, norm_mix, norm_mlp, w_mlp_in, w_mlp_out, ab_w_in, ab_w_out, ret_logit, s5_a_re, s5_a_im, s5_log_dt, s5_b_re, s5_b_im, s5_c_re, s5_c_im, s5_d, s5_w_glu, s5_b_glu, hg_w_in, hg_w_out, hg_lb_logits, hg_norm, norm_final):
    b, n, d = x.shape
    nc = ctx.shape[1]
    assert d == D_MODEL and b + 1 <= MOD_ROWS and w_mod.shape[0] == DEPTH == 2
    assert n % 512 == 0 and nc % 256 == 0 and n % GRID_W == 0
    ctx_row = b
    tm_l, tm_c = 512, 256

    cc = jnp.zeros((MOD_ROWS, d), F32).at[:b].set(c).at[b].set(c_ctx)
    mod = _adaln(cc, w_mod, b_mod)

    row2 = lambda a: a.reshape(1, -1)
    w_in0 = ab_w_in[0]
    cos, sin = _rope_tables(n)
    ng0 = row2(norm_mix[0])
    q_l, k_l, v_l, u_l, uz_l, g_l = _inproj0(x, mod, 0, ng0, w_in0, cos, sin, None, tm_l)
    q_c, k_c, v_c, u_c, uz_c, g_c = _inproj0(ctx, mod, 0, ng0, w_in0, None, None, ctx_row, tm_c)

    log_gamma = jax.nn.log_sigmoid(ret_logit[0].astype(F32))
    lg_rows = jnp.broadcast_to(log_gamma.reshape(2 * RET_HEADS, 1), (2 * RET_HEADS, 2 * RET_DK))
    r_l, r_c = _retention(lg_rows, q_l, k_l, v_l, g_l, q_c, k_c, v_c, g_c)

    s5_ops = _s5_prep(s5_a_re[0], s5_a_im[0], s5_log_dt[0], s5_b_re[0], s5_b_im[0], s5_c_re[0], s5_c_im[0])
    y5_l, y5_c = _s5(s5_ops, uz_l, uz_c)

    s5p = (row2(s5_d[0]), s5_w_glu[0].astype(BF16), row2(s5_b_glu[0]))
    wo0 = ab_w_out[0].astype(BF16)
    w1_0, w2_0 = w_mlp_in, w_mlp_out
    nm0 = row2(norm_mlp[0])
    h_l = _mix_mlp(x, r_l, y5_l, u_l, s5p, wo0, mod, 0, nm0, w1_0, w2_0, None, None, tm_l, 1024, "mix_mlp0_lat")
    h_c = _mix_mlp(ctx, r_c, y5_c, u_c, s5p, wo0, mod, 0, nm0, w1_0, w2_0, None, ctx_row, tm_c, 1024, "mix_mlp0_ctx")

    w_in1 = hg_w_in[0]
    ng1 = row2(norm_mix[1])
    lat1 = _inproj1(h_l, mod, 1, ng1, hg_lb_logits, w_in1, True, None, tm_l, "inproj1_lat")
    ctx1 = _inproj1(h_c, mod, 1, ng1, hg_lb_logits, w_in1, False, ctx_row, tm_c, "inproj1_ctx")
    o1 = _hgrn(row2(hg_norm[0]), *lat1, *ctx1)
    return _mix_mlp(h_l, o1, None, None, None, hg_w_out[0].astype(BF16), mod, 1, row2(norm_mlp[1]),
                    w_mlp_in, w_mlp_out, row2(norm_final), None, tm_l, 1024,
                    "mix_mlp1_lat")
```

```python
import functools

import jax
import jax.numpy as jnp
from jax import lax
from jax.experimental import pallas as pl
from jax.experimental.pallas import tpu as pltpu

F32 = jnp.float32
BF16 = jnp.bfloat16

D_MODEL = 1024
DEPTH = 2
GRID_W = 64
EPS = 1e-6
ROPE_BASE = 10000.0
N_MOD = 6
RET_HEADS = 4
RET_DK = 64
RET_DV = 128
RET_QK = RET_HEADS * RET_DK
RET_WIDTH = RET_HEADS * RET_DV
RET_CHUNK = 128
S5_WIDTH = D_MODEL - RET_WIDTH
S5_GROUP = 16
S5_GROUPS = S5_WIDTH // S5_GROUP
S5_STATE = 64
S5_CHUNK = 16
S5_ROW = S5_CHUNK * S5_GROUP
S5_PAIR = 2 * S5_ROW
S5_BATCH_ROWS = 2
LANE = 128
S5_LB = S5_WIDTH // LANE
S5_GPB = LANE // S5_GROUP
S5_BIG = S5_CHUNK * LANE
S5_HALF = S5_GPB * 2 * S5_STATE
S5_POW = 32
AB_IN = 2 * RET_QK + 2 * RET_WIDTH + S5_WIDTH
HG_HEADS = 8
HG_DK = D_MODEL // HG_HEADS
HG_BLOCK = 64
HG_GROUP = 4
HG_SPLIT = 2
D_FF = 4 * D_MODEL
MOD_ROWS = 16

VMEM_LIMIT_BYTES = 56 * 1024 * 1024


def _cp(*sem):
    return pltpu.CompilerParams(dimension_semantics=sem, vmem_limit_bytes=VMEM_LIMIT_BYTES)


def _dot(a, b):
    return jnp.dot(a, b, preferred_element_type=F32)


def _dot_nt(a, b):
    return lax.dot_general(a, b, (((1,), (1,)), ((), ())), preferred_element_type=F32)


def _dot_tn(a, b):
    return lax.dot_general(a, b, (((0,), (0,)), ((), ())), preferred_element_type=F32)


def _sigmoid(x):
    return 0.5 * jnp.tanh(0.5 * x) + 0.5


def _rms(x):
    return x * lax.rsqrt(jnp.mean(x * x, axis=-1, keepdims=True) + EPS)


def _mod_chunk(mod_ref, row, i):
    return mod_ref[0, pl.ds(row, 1), i * D_MODEL:(i + 1) * D_MODEL]


def _adaln_kernel(cc_ref, w_ref, b_ref, o_ref):
    s = jax.nn.silu(cc_ref[...]).astype(BF16)
    o_ref[0] = _dot(s, w_ref[0].astype(BF16)) + b_ref[0]


def _adaln(cc, w_mod, b_mod):
    bn = 1536
    n = N_MOD * D_MODEL
    return pl.pallas_call(
        _adaln_kernel,
        out_shape=jax.ShapeDtypeStruct((DEPTH, MOD_ROWS, n), F32),
        grid=(DEPTH, n // bn),
        in_specs=[
            pl.BlockSpec((MOD_ROWS, D_MODEL), lambda l, j: (0, 0)),
            pl.BlockSpec((1, D_MODEL, bn), lambda l, j: (l, 0, j)),
            pl.BlockSpec((1, 1, bn), lambda l, j: (l, 0, j)),
        ],
        out_specs=pl.BlockSpec((1, MOD_ROWS, bn), lambda l, j: (l, 0, j)),
        compiler_params=_cp("parallel", "parallel"),
        name="adaln",
    )(cc, w_mod, b_mod.reshape(DEPTH, 1, n))


def _rope(t, cos, sin):
    lane = lax.broadcasted_iota(jnp.int32, t.shape, 1)
    first = (lane & (RET_DK // 2)) == 0
    w = t.shape[1]
    swapped = jnp.where(first, pltpu.roll(t, w - RET_DK // 2, 1), pltpu.roll(t, RET_DK // 2, 1))
    return t * cos + swapped * sin


def _inproj0_kernel(*refs, mod_row, rope):
    if rope:
        h_ref, mod_ref, ng_ref, w_ref, cos_ref, sin_ref, q_ref, k_ref, v_ref, u_ref, uz_ref, g_ref, u_s = refs
    else:
        h_ref, mod_ref, ng_ref, w_ref, q_ref, k_ref, v_ref, u_ref, uz_ref, g_ref, u_s = refs
    row = pl.program_id(0) if mod_row is None else mod_row
    xn = _rms(h_ref[0]) * ng_ref[...]
    xm = (xn * (1.0 + _mod_chunk(mod_ref, row, 1)) + _mod_chunk(mod_ref, row, 0)).astype(BF16)
    y = _dot(xm, w_ref[...].astype(BF16))
    q = y[:, 0:RET_QK]
    k = y[:, RET_QK:2 * RET_QK]
    if rope:
        cos = jnp.concatenate([cos_ref[...]] * (RET_QK // LANE), axis=1)
        sin = jnp.concatenate([sin_ref[...]] * (RET_QK // LANE), axis=1)
        q = _rope(q, cos, sin)
        k = _rope(k, cos, sin)
    q_ref[0] = q.astype(BF16)
    k_ref[0] = (k * (RET_DK ** -0.5)).astype(BF16)
    c0 = 2 * RET_QK
    v_ref[0] = y[:, c0:c0 + RET_WIDTH].astype(BF16)
    u0 = c0 + RET_WIDTH
    u_ref[0] = y[:, u0:u0 + S5_WIDTH]
    g = y[:, u0 + S5_WIDTH:]
    g_ref[0] = (g * _sigmoid(g)).astype(BF16)
    nch = u_s.shape[1] // S5_CHUNK
    for j in range(S5_LB):
        u_s[j] = y[:, u0 + j * LANE:u0 + (j + 1) * LANE]
        for s in range(S5_CHUNK):
            uz_ref[0, j, :, s * LANE:(s + 1) * LANE] = u_s[j, pl.ds(s, nch, stride=S5_CHUNK), :].astype(BF16)


def _inproj0(h, mod, layer, ng, w, cos, sin, mod_row, tm):
    b, n, _ = h.shape
    rope = cos is not None
    row_spec = lambda width: pl.BlockSpec((1, tm, width), lambda i, j: (i, j, 0))
    in_specs = [
        row_spec(D_MODEL),
        pl.BlockSpec((1, MOD_ROWS, N_MOD * D_MODEL), lambda i, j: (layer, 0, 0)),
        pl.BlockSpec((1, D_MODEL), lambda i, j: (0, 0)),
        pl.BlockSpec((D_MODEL, AB_IN), lambda i, j: (0, 0), pipeline_mode=pl.Buffered(1)),
    ]
    args = [h, mod, ng, w]
    if rope:
        in_specs += [pl.BlockSpec((tm, LANE), lambda i, j: (j, 0))] * 2
        args += [cos, sin]
    widths = (RET_QK, RET_QK, RET_WIDTH, S5_WIDTH, RET_WIDTH)
    dtypes = (BF16, BF16, BF16, F32, BF16)
    out_shape = [jax.ShapeDtypeStruct((b, n, wd), dt) for wd, dt in zip(widths, dtypes)]
    out_specs = [row_spec(wd) for wd in widths]
    out_shape.insert(4, jax.ShapeDtypeStruct((b, S5_LB, n // S5_CHUNK, S5_BIG), BF16))
    out_specs.insert(4, pl.BlockSpec((1, S5_LB, tm // S5_CHUNK, S5_BIG), lambda i, j: (i, 0, j, 0)))
    return pl.pallas_call(
        functools.partial(_inproj0_kernel, mod_row=mod_row, rope=rope),
        out_shape=out_shape,
        grid=(b, n // tm),
        in_specs=in_specs,
        out_specs=out_specs,
        scratch_shapes=[pltpu.VMEM((S5_LB, tm, LANE), F32)],
        compiler_params=_cp("parallel", "parallel"),
        name="inproj0_lat" if rope else "inproj0_ctx",
    )(*args)


def _ret_kernel(lg_ref, ql, kl, vl, gl, qc, kc, vc, gc, rl, rc, st_s, *, ncl, ncc):
    c = RET_CHUNK
    dk2 = 2 * RET_DK
    nt = ncc + ncl
    p = pl.program_id(1)
    h_a = 2 * p
    lgf_a = lg_ref[pl.ds(h_a, 1), :]
    lgf_b = lg_ref[pl.ds(h_a + 1, 1), :]
    lgb_a = lg_ref[pl.ds(RET_HEADS + h_a, 1), :]
    lgb_b = lg_ref[pl.ds(RET_HEADS + h_a + 1, 1), :]
    lane = lax.broadcasted_iota(jnp.int32, (1, 2 * RET_DK), 1)
    is_a = lane < RET_DK
    lgf_lane = jnp.where(is_a, lgf_a, lgf_b)
    lgb_lane = jnp.where(is_a, lgb_a, lgb_b)
    ri = lax.broadcasted_iota(jnp.int32, (c, c), 0).astype(F32)
    ci = lax.broadcasted_iota(jnp.int32, (c, c), 1).astype(F32)
    diff = ri - ci

    def dmat(lgf, lgb):
        fwd = jnp.exp(jnp.maximum(diff, 0.0) * lgf)
        bwd = jnp.exp(jnp.maximum(-diff, 0.0) * lgb)
        return jnp.where(diff > 0, fwd, jnp.where(diff < 0, bwd, 2.0))

    d_a = dmat(lgf_a, lgb_a)
    d_b = dmat(lgf_b, lgb_b)
    rowp = lax.broadcasted_iota(jnp.int32, (c, dk2), 0).astype(F32)
    qd = jnp.concatenate([jnp.exp((rowp + 1.0) * lgf_lane), jnp.exp((c - rowp) * lgb_lane)], axis=1)
    kd = jnp.concatenate([jnp.exp((c - 1.0 - rowp) * lgf_lane), jnp.exp(rowp * lgb_lane)], axis=1)
    rowk = lax.broadcasted_iota(jnp.int32, (dk2, 2 * RET_DV), 0)
    cd_f = jnp.exp(c * jnp.where(rowk < RET_DK, lgf_a[:, :1], lgf_b[:, :1]))
    cd_b = jnp.exp(c * jnp.where(rowk < RET_DK, lgb_a[:, :1], lgb_b[:, :1]))
    mask2 = jnp.concatenate([is_a, is_a], axis=1)

    def rows_of(n):
        if n < ncc:
            return (qc, kc, vc, gc, rc), slice(n * c, (n + 1) * c)
        return (ql, kl, vl, gl, rl), slice((n - ncc) * c, (n - ncc + 1) * c)

    def increment(n):
        (_, k_ref, v_ref, _, _), sl = rows_of(n)
        k = k_ref[0, sl, :]
        kk = jnp.concatenate([k, k], axis=1).astype(F32) * kd
        return _dot_tn(kk.astype(BF16), v_ref[0, sl, :])

    sf = jnp.zeros((dk2, 2 * RET_DV), F32)
    inc = increment(0)
    for n in range(nt):
        nxt = increment(n + 1) if n + 1 < nt else None
        st_s[n, 0:dk2, :] = sf
        st_s[n, dk2:2 * dk2, :] = inc[dk2:2 * dk2]
        sf = cd_f * sf + inc[0:dk2]
        inc = nxt

    heads = ((True, d_a, 0), (False, d_b, RET_DV))

    def scores(n):
        (q_ref, k_ref, _, _, _), sl = rows_of(n)
        q, k = q_ref[0, sl, :], k_ref[0, sl, :]
        q2 = (jnp.concatenate([q, q], axis=1).astype(F32) * qd).astype(BF16)
        res = []
        for keep_a, dm, _ in heads:
            m1 = is_a if keep_a else jnp.logical_not(is_a)
            m2 = mask2 if keep_a else jnp.logical_not(mask2)
            att = _dot_nt(jnp.where(m1, q, jnp.zeros_like(q)), k) * dm
            res.append((att.astype(BF16), jnp.where(m2, q2, jnp.zeros_like(q2))))
        return res

    def output(n, sc, sb):
        (_, _, v_ref, g_ref, out_ref), sl = rows_of(n)
        v, g = v_ref[0, sl, :], g_ref[0, sl, :]
        s_n = jnp.concatenate([st_s[n, 0:dk2, :].astype(BF16), sb.astype(BF16)], axis=0)
        for (_, _, cs), (att, q2m) in zip(heads, sc):
            o = _dot(att, v[:, cs:cs + RET_DV]) + _dot(q2m, s_n[:, cs:cs + RET_DV])
            o = _rms(o) * g[:, cs:cs + RET_DV].astype(F32)
            out_ref[0, sl, cs:cs + RET_DV] = o.astype(BF16)

    order = list(range(ncc - 1, -1, -1)) + list(range(nt - 1, ncc - 1, -1))
    sb = jnp.zeros((dk2, 2 * RET_DV), F32)
    sc = scores(order[0])
    for i, n in enumerate(order):
        nxt = scores(order[i + 1]) if i + 1 < nt else None
        output(n, sc, sb)
        sb = cd_b * sb + st_s[n, dk2:2 * dk2, :]
        sc = nxt


def _retention(lg_rows, q_l, k_l, v_l, g_l, q_c, k_c, v_c, g_c):
    b, n, _ = q_l.shape
    nc = q_c.shape[1]
    ncl, ncc = n // RET_CHUNK, nc // RET_CHUNK
    pairs = RET_HEADS // 2

    def spec(rows, width):
        return pl.BlockSpec((1, rows, width), lambda i, p: (i, 0, p))

    return pl.pallas_call(
        functools.partial(_ret_kernel, ncl=ncl, ncc=ncc),
        out_shape=[jax.ShapeDtypeStruct((b, n, RET_WIDTH), BF16),
                   jax.ShapeDtypeStruct((b, nc, RET_WIDTH), BF16)],
        grid=(b, pairs),
        in_specs=[pl.BlockSpec((2 * RET_HEADS, 2 * RET_DK), lambda i, p: (0, 0)),
                  spec(n, 2 * RET_DK), spec(n, 2 * RET_DK), spec(n, 2 * RET_DV), spec(n, 2 * RET_DV),
                  spec(nc, 2 * RET_DK), spec(nc, 2 * RET_DK), spec(nc, 2 * RET_DV), spec(nc, 2 * RET_DV)],
        out_specs=[spec(n, 2 * RET_DV), spec(nc, 2 * RET_DV)],
        scratch_shapes=[pltpu.VMEM((ncl + ncc, 4 * RET_DK, 2 * RET_DV), F32)],
        compiler_params=_cp("parallel", "parallel"),
        name="retention",
    )(lg_rows, q_l, k_l, v_l, g_l, q_c, k_c, v_c, g_c)


def _dot_hi(a, b, contract=(1, 0)):
    dims = (((contract[0],), (contract[1],)), ((), ()))
    return lax.dot_general(a, b, dims, preferred_element_type=F32, precision=lax.Precision.HIGHEST)


def _dot_sel(a, b, contract, data):
    dims = (((contract[0],), (contract[1],)), ((), ()))
    x = (a, b)[data]
    hi = x.astype(BF16)
    lo = (x - hi.astype(F32)).astype(BF16)
    dd = lambda piece: lax.dot_general(*((piece, b) if data == 0 else (a, piece)), dims, preferred_element_type=F32)
    return dd(hi) + dd(lo)


def _s5_prep_kernel(ar_row, ai_row, ldt, btr, bti, ctr, cti, rm, rwin, rwout, abig):
    t, g, p, kp = S5_CHUNK, S5_GROUP, S5_STATE, S5_POW
    row = S5_ROW
    i0 = lambda shape: lax.broadcasted_iota(jnp.int32, shape, 0)
    i1 = lambda shape: lax.broadcasted_iota(jnp.int32, shape, 1)
    f32 = lambda m: jnp.where(m, 1.0, 0.0).astype(BF16)
    s_of_r = i0((row, kp)) // g
    k_of_l = i1((row, kp))
    sel_rows = (f32(k_of_l == t - 1 - s_of_r), f32(k_of_l == s_of_r))
    t_of_c = i1((kp, row)) // g
    k_of_s = i0((kp, row))
    sel_out = (f32(k_of_s == t_of_c + 1), f32(k_of_s == t - t_of_c))
    sel_lag = (f32(k_of_s == t_of_c), f32(k_of_s == t - 1 - t_of_c))
    tile_l = f32(i1((g, row)) % g == i0((g, row)))
    tile_r = f32(i0((row, g)) % g == i1((row, g)))
    lane = i1((g, row))
    k_col = i0((kp, 1)).astype(F32)
    k_row = i1((1, kp)).astype(F32)
    first = i0((8, 1)) == 0

    def outer(a, k):
        a8 = jnp.where(first, jnp.broadcast_to(a, (8, a.shape[1])), 0.0)
        return _dot_hi(a8, jnp.broadcast_to(k, (8, k.shape[1])), (0, 0))

    rwin[...] = jnp.zeros(rwin.shape, BF16)
    rwout[...] = jnp.zeros(rwout.shape, BF16)
    for q in range(S5_GPB):
        pair, qq = divmod(q, 2)
        lags = []
        for d in range(2):
            dt = jnp.exp(ldt[d, q])
            are_r, aim_r = ar_row[d, q], ai_row[d, q]
            mag = jnp.exp(are_r * dt)
            ang = aim_r * dt
            nr, ni = mag * jnp.cos(ang) - 1.0, mag * jnp.sin(ang)
            den = jnp.square(are_r) + jnp.square(aim_r)
            fr = (nr * are_r + ni * aim_r) / den
            fi = (ni * are_r - nr * aim_r) / den
            pm = jnp.exp(k_col * (are_r * dt))
            pa = k_col * ang
            pk_re, pk_im = pm * jnp.cos(pa), pm * jnp.sin(pa)
            pmt = jnp.exp(outer(are_r * dt, k_row))
            pat = outer(ang, k_row)
            pt_re, pt_im = pmt * jnp.cos(pat), pmt * jnp.sin(pat)
            bt_re, bt_im = _dot_sel(tile_r, btr[d, q], (1, 1), 1), _dot_sel(tile_r, bti[d, q], (1, 1), 1)
            bb_re = fr * bt_re - fi * bt_im
            bb_im = fr * bt_im + fi * bt_re
            pr_re, pr_im = _dot_sel(sel_rows[d], pk_re, (1, 0), 1), _dot_sel(sel_rows[d], pk_im, (1, 0), 1)
            w_re = pr_re * bb_re - pr_im * bb_im
            w_im = pr_re * bb_im + pr_im * bb_re
            for part, w in ((d, w_re), (2 + d, w_im)):
                c0 = part * LANE + qq * p
                rwin[0, q * row:(q + 1) * row, c0:c0 + p] = w.astype(BF16)
            ct_re, ct_im = _dot_sel(ctr[d, q], tile_l, (0, 0), 0), _dot_sel(cti[d, q], tile_l, (0, 0), 0)

            def c_pow(sel):
                pc_re, pc_im = _dot_sel(pt_re, sel, (1, 0), 0), _dot_sel(pt_im, sel, (1, 0), 0)
                return ct_re * pc_re - ct_im * pc_im, ct_re * pc_im + ct_im * pc_re

            o_re, o_im = c_pow(sel_out[d])
            for part, o in ((d, o_re), (2 + d, -o_im)):
                r0 = pair * S5_PAIR + part * LANE + qq * p
                rwout[0, r0:r0 + p, qq * row:(qq + 1) * row] = o.astype(BF16)
            r0 = d * S5_GPB * p + q * p
            l_re, l_im = c_pow(sel_lag[d])
            lags.append(_dot_hi(bb_re[0:g], l_re) - _dot_hi(bb_im[0:g], l_im))
            abig[0, 0:1, r0:r0 + p] = pk_re[t:t + 1, :]
            abig[0, 1:2, r0:r0 + p] = pk_im[t:t + 1, :]
        for s in range(t):
            fwd = jnp.where(lane >= g * s, pltpu.roll(lags[0], g * s, 1), 0.0)
            bwd = jnp.where(lane < g * (s + 1), pltpu.roll(lags[1], (row - g * (t - 1 - s)) % row, 1), 0.0)
            rm[0, q * row + s * g:q * row + (s + 1) * g, :] = (fwd + bwd).astype(BF16)


def _s5_prep(a_re, a_im, log_dt, b_re, b_im, c_re, c_im):
    gg, p, g = S5_GROUPS, S5_STATE, S5_GROUP
    f = lambda x: x.astype(F32)
    args = (f(a_re).reshape(2, gg, 1, p), f(a_im).reshape(2, gg, 1, p),
            f(log_dt).reshape(2, gg, 1, 1),
            f(b_re), f(b_im), f(c_re), f(c_im))
    spec = lambda r, c: pl.BlockSpec((2, S5_GPB, r, c), lambda j: (0, j, 0, 0))
    out = lambda r, c: pl.BlockSpec((1, r, c), lambda j: (j, 0, 0))
    return pl.pallas_call(
        _s5_prep_kernel,
        out_shape=[jax.ShapeDtypeStruct((S5_LB, S5_BIG, S5_ROW), BF16),
                   jax.ShapeDtypeStruct((S5_LB, S5_BIG, S5_PAIR), BF16),
                   jax.ShapeDtypeStruct((S5_LB, S5_BIG, S5_PAIR), BF16),
                   jax.ShapeDtypeStruct((S5_LB, 2, S5_HALF), F32)],
        grid=(S5_LB,),
        in_specs=[spec(1, p), spec(1, p), spec(1, 1),
                  spec(p, g), spec(p, g), spec(g, p), spec(g, p)],
        out_specs=[out(S5_BIG, S5_ROW), out(S5_BIG, S5_PAIR), out(S5_BIG, S5_PAIR), out(2, S5_HALF)],
        compiler_params=_cp("parallel"),
        name="s5_prep",
    )(*args)


def _s5_block_transpose(v):
    n = len(v)
    blk = lax.broadcasted_iota(jnp.int32, (1, LANE), 1) // S5_GROUP

    def spread(src, e):
        w = 1
        while w < n:
            bit = (blk // w) % 2 == 1
            src = [jnp.where(bit, src[(m + e * w) % n], src[m]) for m in range(n)]
            w *= 2
        return src

    diag = spread(v, 1)
    rolled = [diag[0]] + [pltpu.roll(diag[(-k) % n], LANE - k * S5_GROUP, 1) for k in range(1, n)]
    return spread(rolled, -1)


def _s5_kernel(ul, uc, rm_ref, rwin_ref, rwout_ref, a_ref, yl, yc, x_s, *, ncl, ncc):
    hw = S5_HALF
    gpb, half = S5_GPB, S5_CHUNK // 2
    cols = lambda x, i: x[:, i * LANE:(i + 1) * LANE]

    nb = ul.shape[0]
    nt = ncc + ncl
    pairs = gpb // 2
    slab = lambda k, r: k * pairs + r
    of_row = lambda i: pl.ds(i, nt, stride=nb)

    zg = []
    for i in range(nb):
        z = jnp.concatenate([uc[i, 0], ul[i, 0]], axis=0)
        zt = [_s5_block_transpose([cols(z, half * h + a) for a in range(half)]) for h in range(2)]
        zg.append([jnp.concatenate([zt[0][q], zt[1][q]], axis=1) for q in range(gpb)])
        for r in range(pairs):
            xw = _dot(jnp.concatenate([zg[i][2 * r], zg[i][2 * r + 1]], axis=1),
                      rwin_ref[0, r * S5_PAIR:(r + 1) * S5_PAIR, :])
            for k in range(4):
                x_s[slab(k, r), of_row(i), :] = cols(xw, k)
    a_re = a_ref[0, 0:1, :]
    a_im = a_ref[0, 1:2, :]

    def segment(base, n, carry):
        s_re, s_im = carry
        for i in range(n):
            rf = slice((base + i) * nb, (base + i + 1) * nb)
            rb = slice((base + n - 1 - i) * nb, (base + n - i) * nb)
            rows_of = (rf,) * pairs + (rb,) * pairs
            x_re = jnp.concatenate([x_s[c, rows_of[c], :] for c in range(2 * pairs)], axis=-1)
            x_im = jnp.concatenate([x_s[2 * pairs + c, rows_of[c], :] for c in range(2 * pairs)], axis=-1)
            for c in range(2 * pairs):
                x_s[c, rows_of[c], :] = cols(s_re, c)
                x_s[2 * pairs + c, rows_of[c], :] = cols(s_im, c)
            s_re, s_im = a_re * s_re - a_im * s_im + x_re, a_re * s_im + a_im * s_re + x_im
        return s_re, s_im

    zero = jnp.zeros((nb, hw), F32)
    carry = segment(0, ncc, (zero, zero))
    segment(ncc, ncl, carry)

    for i in range(nb):
        yt = [[None] * gpb, [None] * gpb]
        for r in range(pairs):
            xs = jnp.concatenate([x_s[slab(k, r), of_row(i), :] for k in range(4)], axis=1)
            yp = _dot(xs.astype(BF16), rwout_ref[0, r * S5_PAIR:(r + 1) * S5_PAIR, :])
            for qq in range(2):
                q = 2 * r + qq
                yq = (yp[:, qq * S5_ROW:(qq + 1) * S5_ROW]
                      + _dot(zg[i][q], rm_ref[0, q * S5_ROW:(q + 1) * S5_ROW, :]))
                yt[0][q], yt[1][q] = cols(yq, 0), cols(yq, 1)
        for h in range(2):
            for a, y in enumerate(_s5_block_transpose(yt[h])):
                c = slice((half * h + a) * LANE, (half * h + a + 1) * LANE)
                yc[i, 0, :, c] = y[0:ncc]
                yl[i, 0, :, c] = y[ncc:ncc + ncl]


def _s5(prep, zl, zc):
    rm, rwin, rwout, a_big = prep
    b, _, ncl, _ = zl.shape
    ncc = zc.shape[2]
    nb = S5_BATCH_ROWS if b % S5_BATCH_ROWS == 0 else 1
    rows = lambda r: pl.BlockSpec((nb, 1, r, S5_BIG), lambda j, i: (i, j, 0, 0))
    wspec = lambda r, c: pl.BlockSpec((1, r, c), lambda j, i: (j, 0, 0))
    return pl.pallas_call(
        functools.partial(_s5_kernel, ncl=ncl, ncc=ncc),
        out_shape=[jax.ShapeDtypeStruct(zl.shape, F32), jax.ShapeDtypeStruct(zc.shape, F32)],
        grid=(S5_LB, b // nb),
        in_specs=[rows(ncl), rows(ncc), wspec(S5_BIG, S5_ROW), wspec(S5_BIG, S5_PAIR),
                  wspec(S5_BIG, S5_PAIR), wspec(2, S5_HALF)],
        out_specs=[rows(ncl), rows(ncc)],
        scratch_shapes=[pltpu.VMEM((2 * S5_HALF // LANE, (ncc + ncl) * nb, LANE), F32)],
        compiler_params=_cp("parallel", "parallel"),
        name="s5",
    )(zl, zc, rm, rwin, rwout, a_big)


def _mix_mlp_kernel(*refs, mod_row, s5_merge, final_norm, fb):
    if s5_merge:
        (h_ref, r_ref, y5_ref, u_ref, ds_ref, wg_ref, bg_ref, wo_ref, mod_ref, nm_ref, w1_ref, w2_ref,
         *rest) = refs
    else:
        h_ref, r_ref, wo_ref, mod_ref, nm_ref, w1_ref, w2_ref, *rest = refs
    if final_norm:
        nf_ref, o_ref, *scratch = rest
    else:
        o_ref, *scratch = rest
    row = pl.program_id(0) if mod_row is None else mod_row
    if s5_merge:
        (y_s,) = scratch
        nch = y_s.shape[1] // S5_CHUNK
        for lb in range(S5_LB):
            for s in range(S5_CHUNK):
                y_s[lb, pl.ds(s, nch, stride=S5_CHUNK), :] = y5_ref[0, lb, :, s * LANE:(s + 1) * LANE]
        y5 = jnp.concatenate([y_s[lb] for lb in range(S5_LB)], axis=-1)
        y = jax.nn.gelu(y5 + ds_ref[...] * u_ref[0])
        y = y * jax.nn.sigmoid(_dot(y.astype(BF16), wg_ref[...]) + bg_ref[...])
        mix = _dot(r_ref[0], wo_ref[0:RET_WIDTH, :]) + _dot(y.astype(BF16), wo_ref[RET_WIDTH:D_MODEL, :])
    else:
        mix = _dot(r_ref[0], wo_ref[...])
    h1 = h_ref[0] + _mod_chunk(mod_ref, row, 2) * mix
    xn = _rms(h1) * nm_ref[...]
    xm = (xn * (1.0 + _mod_chunk(mod_ref, row, 4)) + _mod_chunk(mod_ref, row, 3)).astype(BF16)
    acc = None
    for j in range(D_FF // fb):
        a = jnp.square(jnp.maximum(_dot(xm, w1_ref[:, j * fb:(j + 1) * fb].astype(BF16)), 0.0)).astype(BF16)
        part = _dot(a, w2_ref[j * fb:(j + 1) * fb, :].astype(BF16))
        acc = part if acc is None else acc + part
    h2 = h1 + _mod_chunk(mod_ref, row, 5) * acc
    if final_norm:
        h2 = _rms(h2) * nf_ref[...]
    o_ref[0] = h2


def _mix_mlp(h, r, s5y, u, s5p, wo, mod, layer, nm, w1, w2, nf, mod_row, tm, fb, name):
    b, n, _ = h.shape
    s5_merge = s5y is not None
    final_norm = nf is not None
    one = pl.Buffered(1)
    row_spec = lambda width: pl.BlockSpec((1, tm, width), lambda i, t: (i, t, 0))
    const = lambda shape: pl.BlockSpec(shape, lambda i, t: (0,) * len(shape), pipeline_mode=one)
    in_specs = [row_spec(D_MODEL), row_spec(r.shape[-1])]
    args = [h, r]
    if s5_merge:
        d_skip, w_glu, b_glu = s5p
        z_spec = pl.BlockSpec((1, S5_LB, tm // S5_CHUNK, S5_BIG), lambda i, t: (i, 0, t, 0))
        in_specs += [z_spec, row_spec(S5_WIDTH), const((1, S5_WIDTH)),
                     const((S5_WIDTH, S5_WIDTH)), const((1, S5_WIDTH))]
        args += [s5y, u, d_skip, w_glu, b_glu]
    in_specs += [const((D_MODEL, D_MODEL)),
                 pl.BlockSpec((1, MOD_ROWS, N_MOD * D_MODEL), lambda i, t: (layer, 0, 0), pipeline_mode=one),
                 const((1, D_MODEL))]
    if w1.ndim == 3:
        in_specs += [pl.BlockSpec((None, D_MODEL, D_FF), lambda i, t: (layer, 0, 0), pipeline_mode=one),
                     pl.BlockSpec((None, D_FF, D_MODEL), lambda i, t: (layer, 0, 0), pipeline_mode=one)]
    else:
        in_specs += [const((D_MODEL, D_FF)), const((D_FF, D_MODEL))]
    args += [wo, mod, nm, w1, w2]
    if final_norm:
        in_specs.append(const((1, D_MODEL)))
        args.append(nf)
    return pl.pallas_call(
        functools.partial(_mix_mlp_kernel, mod_row=mod_row, s5_merge=s5_merge, final_norm=final_norm, fb=fb),
        out_shape=jax.ShapeDtypeStruct((b, n, D_MODEL), F32),
        grid=(b, n // tm),
        in_specs=in_specs,
        out_specs=row_spec(D_MODEL),
        scratch_shapes=[pltpu.VMEM((S5_LB, tm, LANE), F32)] if s5_merge else [],
        compiler_params=_cp("parallel", "parallel"),
        name=name,
    )(*args)


def _hgrn_lower_bounds(lbl_ref, layer):
    out = []
    for d in range(2):
        z = [lbl_ref[d, k:k + 1, :] for k in range(DEPTH)]
        zmax = functools.reduce(jnp.maximum, z)
        e = [jnp.exp(v - zmax) for v in z]
        tot = functools.reduce(lambda a, b_: a + b_, e)
        lb = jnp.zeros_like(tot)
        for k in range(1, layer + 1):
            lb = lb + e[k] / tot
        out.append(lb)
    return out


def _inproj1_kernel(h_ref, mod_ref, ng_ref, lbl_ref, w_ref, *out_refs, mod_row, layer, latent):
    row = pl.program_id(0) if mod_row is None else mod_row
    xn = _rms(h_ref[0]) * ng_ref[...]
    xm = (xn * (1.0 + _mod_chunk(mod_ref, row, 1)) + _mod_chunk(mod_ref, row, 0)).astype(BF16)
    lbs = _hgrn_lower_bounds(lbl_ref, layer)
    names = (["q"] if latent else []) + ["lf0", "kk0", "lf1", "kk1", "i"] + (["sg"] if latent else [])
    out = dict(zip(names, out_refs))
    half = h_ref.shape[1] // 2

    def finish(kind, y, rows):
        if kind in ("lf0", "lf1"):
            d = int(kind[2])
            t = (1.0 - lbs[d]) * _sigmoid(y)
            out[kind][0, rows, :] = jnp.log(lbs[d] + t)
            out["kk%d" % d][0, rows, :] = ((1.0 - lbs[d]) - t).astype(BF16)
        elif kind == "sg":
            out[kind][0, rows, :] = (y * _sigmoid(y)).astype(BF16)
        else:
            out[kind][0, rows, :] = y.astype(BF16)

    cols = [("lf0", 1), ("lf1", 2)] + ([("sg", 4), ("q", 0)] if latent else []) + [("i", 3)]
    pending = None
    for kind, k in cols:
        for rows in (slice(0, half), slice(half, 2 * half)):
            y = _dot(xm[rows], w_ref[:, k * D_MODEL:(k + 1) * D_MODEL].astype(BF16))
            if pending is not None:
                finish(*pending)
            pending = (kind, y, rows)
    finish(*pending)


def _inproj1(h, mod, layer, ng, lb_logits, w, latent, mod_row, tm, name):
    b, n, _ = h.shape
    one = pl.Buffered(1)
    row_spec = pl.BlockSpec((1, tm, D_MODEL), lambda i, t: (i, t, 0))
    dtypes = ([BF16] if latent else []) + [F32, BF16, F32, BF16, BF16] + ([BF16] if latent else [])
    return pl.pallas_call(
        functools.partial(_inproj1_kernel, mod_row=mod_row, layer=layer, latent=latent),
        out_shape=[jax.ShapeDtypeStruct((b, n, D_MODEL), dt) for dt in dtypes],
        grid=(b, n // tm),
        in_specs=[row_spec,
                  pl.BlockSpec((1, MOD_ROWS, N_MOD * D_MODEL), lambda i, t: (layer, 0, 0), pipeline_mode=one),
                  pl.BlockSpec((1, D_MODEL), lambda i, t: (0, 0), pipeline_mode=one),
                  pl.BlockSpec(lb_logits.shape, lambda i, t: (0, 0, 0), pipeline_mode=one),
                  pl.BlockSpec(w.shape, lambda i, t: (0, 0), pipeline_mode=one)],
        out_specs=[row_spec] * len(dtypes),
        compiler_params=_cp("parallel", "parallel"),
        name=name,
    )(h, mod, ng, lb_logits, w)


def _cumsum_mm(tri, x):
    acc = None
    r = x
    for i in range(HG_SPLIT):
        p = r.astype(BF16)
        acc = _dot(tri, p) if acc is None else acc + _dot(tri, p)
        if i + 1 < HG_SPLIT:
            r = r - p.astype(F32)
    return acc


def _hgrn_kernel(ng_ref, ql, lffl, kfl, lfbl, kbl, il, sgl, lffc, kfc, lfbc, kbc, ic, o_ref,
                 qin_s, att_s, kv_s, et_s, kvc_s, etc_s, cum_s, ko_s, qt_s, kt_s, *, nbl, nbc):
    cb = HG_BLOCK
    mid = cb // 2
    gb = HG_GROUP
    gr = gb * cb
    dk = HG_DK

    ri = lax.broadcasted_iota(jnp.int32, (gr, gr), 0)
    ci = lax.broadcasted_iota(jnp.int32, (gr, gr), 1)
    same = (ri // cb) == (ci // cb)
    rb = lax.broadcasted_iota(jnp.int32, (cb, cb), 0)
    cbi = lax.broadcasted_iota(jnp.int32, (cb, cb), 1)
    tri_l = jnp.where(same & (ri >= ci), 1.0, 0.0).astype(BF16)
    dirs = ((0, rb >= cbi, mid - 1, cb - 1), (1, rb <= cbi, mid, 0))

    def cumsums(lfs, slot):
        pre = _cumsum_mm(tri_l, jnp.concatenate(lfs, axis=-1))
        pre_b = pre[:, dk:].reshape(gb, cb, dk)
        cum_s[slot, 0] = pre[:, :dk]
        cum_s[slot, 1] = (pre_b[:, cb - 1:cb, :] - pre_b).reshape(gr, dk) + lfs[1]

    def operands(slot, kks, q, n0, et_ref, which=(0, 1)):
        for d, keep, ref_row, tot_row in [dirs[i] for i in which]:
            cum = cum_s[slot, d].reshape(gb, cb, dk)
            kk = kks[d].astype(F32).reshape(gb, cb, dk)
            ref = cum[:, ref_row:ref_row + 1, :]
            tot = cum[:, tot_row:tot_row + 1, :]
            e = cum - ref
            kt = kk * jnp.exp(-e)
            ko_s[slot, d] = (kt * jnp.exp(tot - ref)).astype(BF16).reshape(gr, dk)
            e_tot = jnp.exp(tot)
            for j in range(gb):
                et_ref[n0 + j, :, d * dk:(d + 1) * dk] = e_tot[j]
            if q is not None:
                qt = q.astype(F32).reshape(gb, cb, dk) * jnp.exp(e)
                qin_s[n0 * cb:n0 * cb + gr, d * dk:(d + 1) * dk] = (
                    (qt * jnp.exp(ref)).astype(BF16).reshape(gr, dk))
                qt_s[slot, d] = qt.astype(BF16).reshape(gr, dk)
                kt_s[slot, d] = kt.astype(BF16).reshape(gr, dk)

    def matmuls(slot, v, with_q, n0, kv_ref, blocks=tuple(range(HG_GROUP))):
        for d in range(2):
            for j in blocks:
                rows = slice(j * cb, (j + 1) * cb)
                kv_ref[n0 + j, :, d * dk:(d + 1) * dk] = _dot_tn(v[rows], ko_s[slot, d, rows, :])
        if with_q:
            for j in blocks:
                rows = slice(j * cb, (j + 1) * cb)
                att = (jnp.where(dirs[0][1], _dot_nt(qt_s[slot, 0, rows, :], kt_s[slot, 0, rows, :]), 0.0)
                       + jnp.where(dirs[1][1], _dot_nt(qt_s[slot, 1, rows, :], kt_s[slot, 1, rows, :]), 0.0))
                att_s[n0 + j] = att.astype(BF16)

    for g in range(nbc // gb):
        sl = slice(g * gr, (g + 1) * gr)
        cumsums((lffc[0, sl, :], lfbc[0, sl, :]), 0)
        operands(0, (kfc[0, sl, :], kbc[0, sl, :]), None, g * gb, etc_s)
        matmuls(0, ic[0, sl, :], False, g * gb, kvc_s)

    ngl = nbl // gb
    rows_of = lambda g: slice(g * gr, (g + 1) * gr)

    def lat_cumsums(g, slot):
        cumsums((lffl[0, rows_of(g), :], lfbl[0, rows_of(g), :]), slot)

    def lat_operands(g, slot, which=(0, 1)):
        operands(slot, (kfl[0, rows_of(g), :], kbl[0, rows_of(g), :]), ql[0, rows_of(g), :], g * gb, et_s, which)

    def lat_matmuls(g, slot, blocks=tuple(range(HG_GROUP))):
        matmuls(slot, il[0, rows_of(g), :], True, g * gb, kv_s, blocks)

    fwd, bwd = slice(0, dk), slice(dk, 2 * dk)
    st_f = jnp.zeros((dk, dk), F32)
    st_b = jnp.zeros((dk, dk), F32)
    for n in range(nbc):
        st_f = st_f * etc_s[n, :, fwd] + kvc_s[n, :, fwd]
    for n in reversed(range(nbc)):
        st_b = st_b * etc_s[n, :, bwd] + kvc_s[n, :, bwd]

    def fwd_steps(st, blocks):
        for n in blocks:
            inc = kv_s[n, :, fwd]
            kv_s[n, :, fwd] = st
            st = st * et_s[n, :, fwd] + inc
        return st

    quarter = [tuple(range(i * gb // 4, (i + 1) * gb // 4)) for i in range(4)]
    lat_cumsums(0, 0)
    lat_operands(0, 0)
    lat_cumsums(1, 1)
    for a in range(ngl):
        sa, sb = a % 2, 1 - a % 2
        prev = [(a - 1) * gb + j for j in range(gb)] if a > 0 else []
        lat_matmuls(a, sa, quarter[0])
        if a + 1 < ngl:
            lat_operands(a + 1, sb, (0,))
        lat_matmuls(a, sa, quarter[1])
        st_f = fwd_steps(st_f, prev[:gb // 2])
        if a + 2 < ngl:
            lat_cumsums(a + 2, sa)
        lat_matmuls(a, sa, quarter[2])
        if a + 1 < ngl:
            lat_operands(a + 1, sb, (1,))
        lat_matmuls(a, sa, quarter[3])
        st_f = fwd_steps(st_f, prev[gb // 2:])
    fwd_steps(st_f, [(ngl - 1) * gb + j for j in range(gb)])

    for n in reversed(range(nbl)):
        sl = slice(n * cb, (n + 1) * cb)
        kv = jnp.concatenate([kv_s[n, :, fwd].astype(BF16), st_b.astype(BF16)], axis=1)
        st_b = st_b * et_s[n, :, bwd] + kv_s[n, :, bwd]
        o = _dot(att_s[n], il[0, sl, :]) + _dot_nt(qin_s[sl, :], kv)
        o = _rms(o) * ng_ref[...] * sgl[0, sl, :].astype(F32)
        o_ref[0, sl, :] = o.astype(BF16)


def _hgrn(norm_g, q_l, lff_l, kf_l, lfb_l, kb_l, i_l, sg_l, lff_c, kf_c, lfb_c, kb_c, i_c):
    b, n, _ = q_l.shape
    nc = lff_c.shape[1]
    nbl, nbc = n // HG_BLOCK, nc // HG_BLOCK
    assert nbl % HG_GROUP == 0 and nbc % HG_GROUP == 0 and nbl >= 2 * HG_GROUP and HG_GROUP % 4 == 0
    spec = lambda rows: pl.BlockSpec((1, rows, HG_DK), lambda i, h: (i, 0, h))
    slot = lambda dt: pltpu.VMEM((2, 2, HG_GROUP * HG_BLOCK, HG_DK), dt)
    return pl.pallas_call(
        functools.partial(_hgrn_kernel, nbl=nbl, nbc=nbc),
        out_shape=jax.ShapeDtypeStruct((b, n, D_MODEL), BF16),
        grid=(b, HG_HEADS),
        in_specs=[pl.BlockSpec((1, HG_DK), lambda i, h: (0, 0))] + [spec(n)] * 7 + [spec(nc)] * 5,
        out_specs=spec(n),
        scratch_shapes=[pltpu.VMEM((n, 2 * HG_DK), BF16),
                        pltpu.VMEM((nbl, HG_BLOCK, HG_BLOCK), BF16),
                        pltpu.VMEM((nbl, HG_DK, 2 * HG_DK), F32),
                        pltpu.VMEM((nbl, 1, 2 * HG_DK), F32),
                        pltpu.VMEM((nbc, HG_DK, 2 * HG_DK), F32),
                        pltpu.VMEM((nbc, 1, 2 * HG_DK), F32),
                        slot(F32), slot(BF16), slot(BF16), slot(BF16)],
        compiler_params=_cp("parallel", "parallel"),
        name="hgrn2",
    )(norm_g, q_l, lff_l, kf_l, lfb_l, kb_l, i_l, sg_l, lff_c, kf_c, lfb_c, kb_c, i_c)


def _rope_tables(n_tok):
    tok = jnp.arange(n_tok, dtype=jnp.int32)[:, None]
    row = (tok // GRID_W).astype(F32)
    col = (tok % GRID_W).astype(F32)
    n_freq = RET_DK // 4
    lane = jnp.arange(LANE, dtype=jnp.int32)[None, :]
    j = lane % (2 * n_freq)
    inv = ROPE_BASE ** (-(j % n_freq).astype(F32) / n_freq)
    ang = jnp.where(j < n_freq, row, col) * inv
    sign = jnp.where(lane % RET_DK < RET_DK // 2, -1.0, 1.0)
    return jnp.cos(ang), jnp.sin(ang) * sign


def kernel(x, c, ctx, c_ctx, w_mod, b_mod, norm_mix, norm_mlp, w_mlp_in, w_mlp_out, ab_w_in, ab_w_out, ret_logit, s5_a_re, s5_a_im, s5_log_dt, s5_b_re, s5_b_im, s5_c_re, s5_c_im, s5_d, s5_w_glu, s5_b_glu, hg_w_in, hg_w_out, hg_lb_logits, hg_norm, norm_final):
    b, n, d = x.shape
    nc = ctx.shape[1]
    assert d == D_MODEL and b + 1 <= MOD_ROWS and w_mod.shape[0] == DEPTH == 2
    assert n % 512 == 0 and nc % 256 == 0 and n % GRID_W == 0
    ctx_row = b
    tm_l, tm_c = 512, 256

    cc = jnp.zeros((MOD_ROWS, d), F32).at[:b].set(c).at[b].set(c_ctx)
    mod = _adaln(cc, w_mod, b_mod)

    row2 = lambda a: a.reshape(1, -1)
    w_in0 = ab_w_in[0]
    cos, sin = _rope_tables(n)
    ng0 = row2(norm_mix[0])
    q_l, k_l, v_l, u_l, uz_l, g_l = _inproj0(x, mod, 0, ng0, w_in0, cos, sin, None, tm_l)
    q_c, k_c, v_c, u_c, uz_c, g_c = _inproj0(ctx, mod, 0, ng0, w_in0, None, None, ctx_row, tm_c)

    log_gamma = jax.nn.log_sigmoid(ret_logit[0].astype(F32))
    lg_rows = jnp.broadcast_to(log_gamma.reshape(2 * RET_HEADS, 1), (2 * RET_HEADS, 2 * RET_DK))
    r_l, r_c = _retention(lg_rows, q_l, k_l, v_l, g_l, q_c, k_c, v_c, g_c)

    s5_ops = _s5_prep(s5_a_re[0], s5_a_im[0], s5_log_dt[0], s5_b_re[0], s5_b_im[0], s5_c_re[0], s5_c_im[0])
    y5_l, y5_c = _s5(s5_ops, uz_l, uz_c)

    s5p = (row2(s5_d[0]), s5_w_glu[0].astype(BF16), row2(s5_b_glu[0]))
    wo0 = ab_w_out[0].astype(BF16)
    w1_0, w2_0 = w_mlp_in, w_mlp_out
    nm0 = row2(norm_mlp[0])
    h_l = _mix_mlp(x, r_l, y5_l, u_l, s5p, wo0, mod, 0, nm0, w1_0, w2_0, None, None, tm_l, 1024, "mix_mlp0_lat")
    h_c = _mix_mlp(ctx, r_c, y5_c, u_c, s5p, wo0, mod, 0, nm0, w1_0, w2_0, None, ctx_row, tm_c, 1024, "mix_mlp0_ctx")

    w_in1 = hg_w_in[0]
    ng1 = row2(norm_mix[1])
    lat1 = _inproj1(h_l, mod, 1, ng1, hg_lb_logits, w_in1, True, None, tm_l, "inproj1_lat")
    ctx1 = _inproj1(h_c, mod, 1, ng1, hg_lb_logits, w_in1, False, ctx_row, tm_c, "inproj1_ctx")
    o1 = _hgrn(row2(hg_norm[0]), *lat1, *ctx1)
    return _mix_mlp(h_l, o1, None, None, None, hg_w_out[0].astype(BF16), mod, 1, row2(norm_mlp[1]),
                    w_mlp_in, w_mlp_out, row2(norm_final), None, tm_l, 1024,
                    "mix_mlp1_lat")
```

```python
import functools

import jax
import jax.numpy as jnp
from jax import lax
from jax.experimental import pallas as pl
from jax.experimental.pallas import tpu as pltpu

F32 = jnp.float32
BF16 = jnp.bfloat16

D_MODEL = 1024
DEPTH = 2
GRID_W = 64
EPS = 1e-6
ROPE_BASE = 10000.0
N_MOD = 6
RET_HEADS = 4
RET_DK = 64
RET_DV = 128
RET_QK = RET_HEADS * RET_DK
RET_WIDTH = RET_HEADS * RET_DV
RET_CHUNK = 128
S5_WIDTH = D_MODEL - RET_WIDTH
S5_GROUP = 16
S5_GROUPS = S5_WIDTH // S5_GROUP
S5_STATE = 64
S5_CHUNK = 16
S5_ROW = S5_CHUNK * S5_GROUP
S5_PAIR = 2 * S5_ROW
S5_BATCH_ROWS = 2
LANE = 128
S5_LB = S5_WIDTH // LANE
S5_GPB = LANE // S5_GROUP
S5_BIG = S5_CHUNK * LANE
S5_HALF = S5_GPB * 2 * S5_STATE
S5_POW = 32
AB_IN = 2 * RET_QK + 2 * RET_WIDTH + S5_WIDTH
HG_HEADS = 8
HG_DK = D_MODEL // HG_HEADS
HG_BLOCK = 64
HG_GROUP = 4
D_FF = 4 * D_MODEL
MOD_ROWS = 16

VMEM_LIMIT_BYTES = 56 * 1024 * 1024


def _cp(*sem):
    return pltpu.CompilerParams(dimension_semantics=sem, vmem_limit_bytes=VMEM_LIMIT_BYTES)


def _dot(a, b):
    return jnp.dot(a, b, preferred_element_type=F32)


def _dot_nt(a, b):
    return lax.dot_general(a, b, (((1,), (1,)), ((), ())), preferred_element_type=F32)


def _dot_tn(a, b):
    return lax.dot_general(a, b, (((0,), (0,)), ((), ())), preferred_element_type=F32)


def _sigmoid(x):
    return 0.5 * jnp.tanh(0.5 * x) + 0.5


def _rms(x):
    return x * lax.rsqrt(jnp.mean(x * x, axis=-1, keepdims=True) + EPS)


def _mod_chunk(mod_ref, row, i):
    return mod_ref[0, pl.ds(row, 1), i * D_MODEL:(i + 1) * D_MODEL]


def _adaln_kernel(cc_ref, w_ref, b_ref, o_ref):
    s = jax.nn.silu(cc_ref[...]).astype(BF16)
    o_ref[0] = _dot(s, w_ref[0].astype(BF16)) + b_ref[0]


def _adaln(cc, w_mod, b_mod):
    bn = 1536
    n = N_MOD * D_MODEL
    return pl.pallas_call(
        _adaln_kernel,
        out_shape=jax.ShapeDtypeStruct((DEPTH, MOD_ROWS, n), F32),
        grid=(DEPTH, n // bn),
        in_specs=[
            pl.BlockSpec((MOD_ROWS, D_MODEL), lambda l, j: (0, 0)),
            pl.BlockSpec((1, D_MODEL, bn), lambda l, j: (l, 0, j)),
            pl.BlockSpec((1, 1, bn), lambda l, j: (l, 0, j)),
        ],
        out_specs=pl.BlockSpec((1, MOD_ROWS, bn), lambda l, j: (l, 0, j)),
        compiler_params=_cp("parallel", "parallel"),
        name="adaln",
    )(cc, w_mod, b_mod.reshape(DEPTH, 1, n))


def _rope(t, cos, sin):
    lane = lax.broadcasted_iota(jnp.int32, t.shape, 1)
    first = (lane & (RET_DK // 2)) == 0
    w = t.shape[1]
    swapped = jnp.where(first, pltpu.roll(t, w - RET_DK // 2, 1), pltpu.roll(t, RET_DK // 2, 1))
    return t * cos + swapped * sin


def _inproj0_kernel(*refs, mod_row, rope):
    if rope:
        h_ref, mod_ref, ng_ref, w_ref, cos_ref, sin_ref, q_ref, k_ref, v_ref, u_ref, uz_ref, g_ref, u_s = refs
    else:
        h_ref, mod_ref, ng_ref, w_ref, q_ref, k_ref, v_ref, u_ref, uz_ref, g_ref, u_s = refs
    row = pl.program_id(0) if mod_row is None else mod_row
    xn = _rms(h_ref[0]) * ng_ref[...]
    xm = (xn * (1.0 + _mod_chunk(mod_ref, row, 1)) + _mod_chunk(mod_ref, row, 0)).astype(BF16)
    y = _dot(xm, w_ref[...].astype(BF16))
    q = y[:, 0:RET_QK]
    k = y[:, RET_QK:2 * RET_QK]
    if rope:
        cos = jnp.concatenate([cos_ref[...]] * (RET_QK // LANE), axis=1)
        sin = jnp.concatenate([sin_ref[...]] * (RET_QK // LANE), axis=1)
        q = _rope(q, cos, sin)
        k = _rope(k, cos, sin)
    q_ref[0] = q.astype(BF16)
    k_ref[0] = (k * (RET_DK ** -0.5)).astype(BF16)
    c0 = 2 * RET_QK
    v_ref[0] = y[:, c0:c0 + RET_WIDTH].astype(BF16)
    u0 = c0 + RET_WIDTH
    u_ref[0] = y[:, u0:u0 + S5_WIDTH]
    g = y[:, u0 + S5_WIDTH:]
    g_ref[0] = (g * _sigmoid(g)).astype(BF16)
    nch = u_s.shape[1] // S5_CHUNK
    for j in range(S5_LB):
        u_s[j] = y[:, u0 + j * LANE:u0 + (j + 1) * LANE]
        for s in range(S5_CHUNK):
            uz_ref[0, j, :, s * LANE:(s + 1) * LANE] = u_s[j, pl.ds(s, nch, stride=S5_CHUNK), :].astype(BF16)


def _inproj0(h, mod, layer, ng, w, cos, sin, mod_row, tm):
    b, n, _ = h.shape
    rope = cos is not None
    row_spec = lambda width: pl.BlockSpec((1, tm, width), lambda i, j: (i, j, 0))
    in_specs = [
        row_spec(D_MODEL),
        pl.BlockSpec((1, MOD_ROWS, N_MOD * D_MODEL), lambda i, j: (layer, 0, 0)),
        pl.BlockSpec((1, D_MODEL), lambda i, j: (0, 0)),
        pl.BlockSpec((D_MODEL, AB_IN), lambda i, j: (0, 0), pipeline_mode=pl.Buffered(1)),
    ]
    args = [h, mod, ng, w]
    if rope:
        in_specs += [pl.BlockSpec((tm, LANE), lambda i, j: (j, 0))] * 2
        args += [cos, sin]
    widths = (RET_QK, RET_QK, RET_WIDTH, S5_WIDTH, RET_WIDTH)
    dtypes = (BF16, BF16, BF16, F32, BF16)
    out_shape = [jax.ShapeDtypeStruct((b, n, wd), dt) for wd, dt in zip(widths, dtypes)]
    out_specs = [row_spec(wd) for wd in widths]
    out_shape.insert(4, jax.ShapeDtypeStruct((b, S5_LB, n // S5_CHUNK, S5_BIG), BF16))
    out_specs.insert(4, pl.BlockSpec((1, S5_LB, tm // S5_CHUNK, S5_BIG), lambda i, j: (i, 0, j, 0)))
    return pl.pallas_call(
        functools.partial(_inproj0_kernel, mod_row=mod_row, rope=rope),
        out_shape=out_shape,
        grid=(b, n // tm),
        in_specs=in_specs,
        out_specs=out_specs,
        scratch_shapes=[pltpu.VMEM((S5_LB, tm, LANE), F32)],
        compiler_params=_cp("parallel", "parallel"),
        name="inproj0_lat" if rope else "inproj0_ctx",
    )(*args)


def _ret_kernel(lg_ref, ql, kl, vl, gl, qc, kc, vc, gc, rl, rc, st_s, *, ncl, ncc):
    c = RET_CHUNK
    dk2 = 2 * RET_DK
    nt = ncc + ncl
    p = pl.program_id(1)
    h_a = 2 * p
    lgf_a = lg_ref[pl.ds(h_a, 1), :]
    lgf_b = lg_ref[pl.ds(h_a + 1, 1), :]
    lgb_a = lg_ref[pl.ds(RET_HEADS + h_a, 1), :]
    lgb_b = lg_ref[pl.ds(RET_HEADS + h_a + 1, 1), :]
    lane = lax.broadcasted_iota(jnp.int32, (1, 2 * RET_DK), 1)
    is_a = lane < RET_DK
    lgf_lane = jnp.where(is_a, lgf_a, lgf_b)
    lgb_lane = jnp.where(is_a, lgb_a, lgb_b)
    ri = lax.broadcasted_iota(jnp.int32, (c, c), 0).astype(F32)
    ci = lax.broadcasted_iota(jnp.int32, (c, c), 1).astype(F32)
    diff = ri - ci

    def dmat(lgf, lgb):
        fwd = jnp.exp(jnp.maximum(diff, 0.0) * lgf)
        bwd = jnp.exp(jnp.maximum(-diff, 0.0) * lgb)
        return jnp.where(diff > 0, fwd, jnp.where(diff < 0, bwd, 2.0))

    d_a = dmat(lgf_a, lgb_a)
    d_b = dmat(lgf_b, lgb_b)
    rowp = lax.broadcasted_iota(jnp.int32, (c, dk2), 0).astype(F32)
    qd = jnp.concatenate([jnp.exp((rowp + 1.0) * lgf_lane), jnp.exp((c - rowp) * lgb_lane)], axis=1)
    kd = jnp.concatenate([jnp.exp((c - 1.0 - rowp) * lgf_lane), jnp.exp(rowp * lgb_lane)], axis=1)
    rowk = lax.broadcasted_iota(jnp.int32, (dk2, 2 * RET_DV), 0)
    cd_f = jnp.exp(c * jnp.where(rowk < RET_DK, lgf_a[:, :1], lgf_b[:, :1]))
    cd_b = jnp.exp(c * jnp.where(rowk < RET_DK, lgb_a[:, :1], lgb_b[:, :1]))
    mask2 = jnp.concatenate([is_a, is_a], axis=1)

    def rows_of(n):
        if n < ncc:
            return (qc, kc, vc, gc, rc), slice(n * c, (n + 1) * c)
        return (ql, kl, vl, gl, rl), slice((n - ncc) * c, (n - ncc + 1) * c)

    def increment(n):
        (_, k_ref, v_ref, _, _), sl = rows_of(n)
        k = k_ref[0, sl, :]
        kk = jnp.concatenate([k, k], axis=1).astype(F32) * kd
        return _dot_tn(kk.astype(BF16), v_ref[0, sl, :])

    sf = jnp.zeros((dk2, 2 * RET_DV), F32)
    inc = increment(0)
    for n in range(nt):
        nxt = increment(n + 1) if n + 1 < nt else None
        st_s[n, 0:dk2, :] = sf
        st_s[n, dk2:2 * dk2, :] = inc[dk2:2 * dk2]
        sf = cd_f * sf + inc[0:dk2]
        inc = nxt

    heads = ((True, d_a, 0), (False, d_b, RET_DV))

    def scores(n):
        (q_ref, k_ref, _, _, _), sl = rows_of(n)
        q, k = q_ref[0, sl, :], k_ref[0, sl, :]
        q2 = (jnp.concatenate([q, q], axis=1).astype(F32) * qd).astype(BF16)
        res = []
        for keep_a, dm, _ in heads:
            m1 = is_a if keep_a else jnp.logical_not(is_a)
            m2 = mask2 if keep_a else jnp.logical_not(mask2)
            att = _dot_nt(jnp.where(m1, q, jnp.zeros_like(q)), k) * dm
            res.append((att.astype(BF16), jnp.where(m2, q2, jnp.zeros_like(q2))))
        return res

    def output(n, sc, sb):
        (_, _, v_ref, g_ref, out_ref), sl = rows_of(n)
        v, g = v_ref[0, sl, :], g_ref[0, sl, :]
        s_n = jnp.concatenate([st_s[n, 0:dk2, :].astype(BF16), sb.astype(BF16)], axis=0)
        for (_, _, cs), (att, q2m) in zip(heads, sc):
            o = _dot(att, v[:, cs:cs + RET_DV]) + _dot(q2m, s_n[:, cs:cs + RET_DV])
            o = _rms(o) * g[:, cs:cs + RET_DV].astype(F32)
            out_ref[0, sl, cs:cs + RET_DV] = o.astype(BF16)

    order = list(range(ncc - 1, -1, -1)) + list(range(nt - 1, ncc - 1, -1))
    sb = jnp.zeros((dk2, 2 * RET_DV), F32)
    sc = scores(order[0])
    for i, n in enumerate(order):
        nxt = scores(order[i + 1]) if i + 1 < nt else None
        output(n, sc, sb)
        sb = cd_b * sb + st_s[n, dk2:2 * dk2, :]
        sc = nxt


def _retention(lg_rows, q_l, k_l, v_l, g_l, q_c, k_c, v_c, g_c):
    b, n, _ = q_l.shape
    nc = q_c.shape[1]
    ncl, ncc = n // RET_CHUNK, nc // RET_CHUNK
    pairs = RET_HEADS // 2

    def spec(rows, width):
        return pl.BlockSpec((1, rows, width), lambda i, p: (i, 0, p))

    return pl.pallas_call(
        functools.partial(_ret_kernel, ncl=ncl, ncc=ncc),
        out_shape=[jax.ShapeDtypeStruct((b, n, RET_WIDTH), BF16),
                   jax.ShapeDtypeStruct((b, nc, RET_WIDTH), BF16)],
        grid=(b, pairs),
        in_specs=[pl.BlockSpec((2 * RET_HEADS, 2 * RET_DK), lambda i, p: (0, 0)),
                  spec(n, 2 * RET_DK), spec(n, 2 * RET_DK), spec(n, 2 * RET_DV), spec(n, 2 * RET_DV),
                  spec(nc, 2 * RET_DK), spec(nc, 2 * RET_DK), spec(nc, 2 * RET_DV), spec(nc, 2 * RET_DV)],
        out_specs=[spec(n, 2 * RET_DV), spec(nc, 2 * RET_DV)],
        scratch_shapes=[pltpu.VMEM((ncl + ncc, 4 * RET_DK, 2 * RET_DV), F32)],
        compiler_params=_cp("parallel", "parallel"),
        name="retention",
    )(lg_rows, q_l, k_l, v_l, g_l, q_c, k_c, v_c, g_c)


def _dot_hi(a, b, contract=(1, 0)):
    dims = (((contract[0],), (contract[1],)), ((), ()))
    return lax.dot_general(a, b, dims, preferred_element_type=F32, precision=lax.Precision.HIGHEST)


def _dot_sel(a, b, contract, data):
    dims = (((contract[0],), (contract[1],)), ((), ()))
    x = (a, b)[data]
    hi = x.astype(BF16)
    lo = (x - hi.astype(F32)).astype(BF16)
    dd = lambda piece: lax.dot_general(*((piece, b) if data == 0 else (a, piece)), dims, preferred_element_type=F32)
    return dd(hi) + dd(lo)


def _s5_prep_kernel(ar_row, ai_row, ldt, btr, bti, ctr, cti, rm, rwin, rwout, abig):
    t, g, p, kp = S5_CHUNK, S5_GROUP, S5_STATE, S5_POW
    row = S5_ROW
    i0 = lambda shape: lax.broadcasted_iota(jnp.int32, shape, 0)
    i1 = lambda shape: lax.broadcasted_iota(jnp.int32, shape, 1)
    f32 = lambda m: jnp.where(m, 1.0, 0.0).astype(BF16)
    s_of_r = i0((row, kp)) // g
    k_of_l = i1((row, kp))
    sel_rows = (f32(k_of_l == t - 1 - s_of_r), f32(k_of_l == s_of_r))
    t_of_c = i1((kp, row)) // g
    k_of_s = i0((kp, row))
    sel_out = (f32(k_of_s == t_of_c + 1), f32(k_of_s == t - t_of_c))
    sel_lag = (f32(k_of_s == t_of_c), f32(k_of_s == t - 1 - t_of_c))
    tile_l = f32(i1((g, row)) % g == i0((g, row)))
    tile_r = f32(i0((row, g)) % g == i1((row, g)))
    lane = i1((g, row))
    k_col = i0((kp, 1)).astype(F32)
    k_row = i1((1, kp)).astype(F32)
    first = i0((8, 1)) == 0

    def outer(a, k):
        a8 = jnp.where(first, jnp.broadcast_to(a, (8, a.shape[1])), 0.0)
        return _dot_hi(a8, jnp.broadcast_to(k, (8, k.shape[1])), (0, 0))

    rwin[...] = jnp.zeros(rwin.shape, BF16)
    rwout[...] = jnp.zeros(rwout.shape, BF16)
    for q in range(S5_GPB):
        pair, qq = divmod(q, 2)
        lags = []
        for d in range(2):
            dt = jnp.exp(ldt[d, q])
            are_r, aim_r = ar_row[d, q], ai_row[d, q]
            mag = jnp.exp(are_r * dt)
            ang = aim_r * dt
            nr, ni = mag * jnp.cos(ang) - 1.0, mag * jnp.sin(ang)
            den = jnp.square(are_r) + jnp.square(aim_r)
            fr = (nr * are_r + ni * aim_r) / den
            fi = (ni * are_r - nr * aim_r) / den
            pm = jnp.exp(k_col * (are_r * dt))
            pa = k_col * ang
            pk_re, pk_im = pm * jnp.cos(pa), pm * jnp.sin(pa)
            pmt = jnp.exp(outer(are_r * dt, k_row))
            pat = outer(ang, k_row)
            pt_re, pt_im = pmt * jnp.cos(pat), pmt * jnp.sin(pat)
            bt_re, bt_im = _dot_sel(tile_r, btr[d, q], (1, 1), 1), _dot_sel(tile_r, bti[d, q], (1, 1), 1)
            bb_re = fr * bt_re - fi * bt_im
            bb_im = fr * bt_im + fi * bt_re
            pr_re, pr_im = _dot_sel(sel_rows[d], pk_re, (1, 0), 1), _dot_sel(sel_rows[d], pk_im, (1, 0), 1)
            w_re = pr_re * bb_re - pr_im * bb_im
            w_im = pr_re * bb_im + pr_im * bb_re
            for part, w in ((d, w_re), (2 + d, w_im)):
                c0 = part * LANE + qq * p
                rwin[0, q * row:(q + 1) * row, c0:c0 + p] = w.astype(BF16)
            ct_re, ct_im = _dot_sel(ctr[d, q], tile_l, (0, 0), 0), _dot_sel(cti[d, q], tile_l, (0, 0), 0)

            def c_pow(sel):
                pc_re, pc_im = _dot_sel(pt_re, sel, (1, 0), 0), _dot_sel(pt_im, sel, (1, 0), 0)
                return ct_re * pc_re - ct_im * pc_im, ct_re * pc_im + ct_im * pc_re

            o_re, o_im = c_pow(sel_out[d])
            for part, o in ((d, o_re), (2 + d, -o_im)):
                r0 = pair * S5_PAIR + part * LANE + qq * p
                rwout[0, r0:r0 + p, qq * row:(qq + 1) * row] = o.astype(BF16)
            r0 = d * S5_GPB * p + q * p
            l_re, l_im = c_pow(sel_lag[d])
            lags.append(_dot_hi(bb_re[0:g], l_re) - _dot_hi(bb_im[0:g], l_im))
            abig[0, 0:1, r0:r0 + p] = pk_re[t:t + 1, :]
            abig[0, 1:2, r0:r0 + p] = pk_im[t:t + 1, :]
        for s in range(t):
            fwd = jnp.where(lane >= g * s, pltpu.roll(lags[0], g * s, 1), 0.0)
            bwd = jnp.where(lane < g * (s + 1), pltpu.roll(lags[1], (row - g * (t - 1 - s)) % row, 1), 0.0)
            rm[0, q * row + s * g:q * row + (s + 1) * g, :] = (fwd + bwd).astype(BF16)


def _s5_prep(a_re, a_im, log_dt, b_re, b_im, c_re, c_im):
    gg, p, g = S5_GROUPS, S5_STATE, S5_GROUP
    f = lambda x: x.astype(F32)
    args = (f(a_re).reshape(2, gg, 1, p), f(a_im).reshape(2, gg, 1, p),
            f(log_dt).reshape(2, gg, 1, 1),
            f(b_re), f(b_im), f(c_re), f(c_im))
    spec = lambda r, c: pl.BlockSpec((2, S5_GPB, r, c), lambda j: (0, j, 0, 0))
    out = lambda r, c: pl.BlockSpec((1, r, c), lambda j: (j, 0, 0))
    return pl.pallas_call(
        _s5_prep_kernel,
        out_shape=[jax.ShapeDtypeStruct((S5_LB, S5_BIG, S5_ROW), BF16),
                   jax.ShapeDtypeStruct((S5_LB, S5_BIG, S5_PAIR), BF16),
                   jax.ShapeDtypeStruct((S5_LB, S5_BIG, S5_PAIR), BF16),
                   jax.ShapeDtypeStruct((S5_LB, 2, S5_HALF), F32)],
        grid=(S5_LB,),
        in_specs=[spec(1, p), spec(1, p), spec(1, 1),
                  spec(p, g), spec(p, g), spec(g, p), spec(g, p)],
        out_specs=[out(S5_BIG, S5_ROW), out(S5_BIG, S5_PAIR), out(S5_BIG, S5_PAIR), out(2, S5_HALF)],
        compiler_params=_cp("parallel"),
        name="s5_prep",
    )(*args)


def _s5_block_transpose(v):
    n = len(v)
    blk = lax.broadcasted_iota(jnp.int32, (1, LANE), 1) // S5_GROUP

    def spread(src, e):
        w = 1
        while w < n:
            bit = (blk // w) % 2 == 1
            src = [jnp.where(bit, src[(m + e * w) % n], src[m]) for m in range(n)]
            w *= 2
        return src

    diag = spread(v, 1)
    rolled = [diag[0]] + [pltpu.roll(diag[(-k) % n], LANE - k * S5_GROUP, 1) for k in range(1, n)]
    return spread(rolled, -1)


def _s5_kernel(ul, uc, rm_ref, rwin_ref, rwout_ref, a_ref, yl, yc, x_s, *, ncl, ncc):
    hw = S5_HALF
    gpb, half = S5_GPB, S5_CHUNK // 2
    cols = lambda x, i: x[:, i * LANE:(i + 1) * LANE]

    nb = ul.shape[0]
    nt = ncc + ncl
    pairs = gpb // 2
    slab = lambda k, r: k * pairs + r
    of_row = lambda i: pl.ds(i, nt, stride=nb)

    zg = []
    for i in range(nb):
        z = jnp.concatenate([uc[i, 0], ul[i, 0]], axis=0)
        zt = [_s5_block_transpose([cols(z, half * h + a) for a in range(half)]) for h in range(2)]
        zg.append([jnp.concatenate([zt[0][q], zt[1][q]], axis=1) for q in range(gpb)])
        for r in range(pairs):
            xw = _dot(jnp.concatenate([zg[i][2 * r], zg[i][2 * r + 1]], axis=1),
                      rwin_ref[0, r * S5_PAIR:(r + 1) * S5_PAIR, :])
            for k in range(4):
                x_s[slab(k, r), of_row(i), :] = cols(xw, k)
    a_re = a_ref[0, 0:1, :]
    a_im = a_ref[0, 1:2, :]

    def segment(base, n, carry):
        s_re, s_im = carry
        for i in range(n):
            rf = slice((base + i) * nb, (base + i + 1) * nb)
            rb = slice((base + n - 1 - i) * nb, (base + n - i) * nb)
            rows_of = (rf,) * pairs + (rb,) * pairs
            x_re = jnp.concatenate([x_s[c, rows_of[c], :] for c in range(2 * pairs)], axis=-1)
            x_im = jnp.concatenate([x_s[2 * pairs + c, rows_of[c], :] for c in range(2 * pairs)], axis=-1)
            for c in range(2 * pairs):
                x_s[c, rows_of[c], :] = cols(s_re, c)
                x_s[2 * pairs + c, rows_of[c], :] = cols(s_im, c)
            s_re, s_im = a_re * s_re - a_im * s_im + x_re, a_re * s_im + a_im * s_re + x_im
        return s_re, s_im

    zero = jnp.zeros((nb, hw), F32)
    carry = segment(0, ncc, (zero, zero))
    segment(ncc, ncl, carry)

    for i in range(nb):
        yt = [[None] * gpb, [None] * gpb]
        for r in range(pairs):
            xs = jnp.concatenate([x_s[slab(k, r), of_row(i), :] for k in range(4)], axis=1)
            yp = _dot(xs.astype(BF16), rwout_ref[0, r * S5_PAIR:(r + 1) * S5_PAIR, :])
            for qq in range(2):
                q = 2 * r + qq
                yq = (yp[:, qq * S5_ROW:(qq + 1) * S5_ROW]
                      + _dot(zg[i][q], rm_ref[0, q * S5_ROW:(q + 1) * S5_ROW, :]))
                yt[0][q], yt[1][q] = cols(yq, 0), cols(yq, 1)
        for h in range(2):
            for a, y in enumerate(_s5_block_transpose(yt[h])):
                c = slice((half * h + a) * LANE, (half * h + a + 1) * LANE)
                yc[i, 0, :, c] = y[0:ncc]
                yl[i, 0, :, c] = y[ncc:ncc + ncl]


def _s5(prep, zl, zc):
    rm, rwin, rwout, a_big = prep
    b, _, ncl, _ = zl.shape
    ncc = zc.shape[2]
    nb = S5_BATCH_ROWS if b % S5_BATCH_ROWS == 0 else 1
    rows = lambda r: pl.BlockSpec((nb, 1, r, S5_BIG), lambda j, i: (i, j, 0, 0))
    wspec = lambda r, c: pl.BlockSpec((1, r, c), lambda j, i: (j, 0, 0))
    return pl.pallas_call(
        functools.partial(_s5_kernel, ncl=ncl, ncc=ncc),
        out_shape=[jax.ShapeDtypeStruct(zl.shape, F32), jax.ShapeDtypeStruct(zc.shape, F32)],
        grid=(S5_LB, b // nb),
        in_specs=[rows(ncl), rows(ncc), wspec(S5_BIG, S5_ROW), wspec(S5_BIG, S5_PAIR),
                  wspec(S5_BIG, S5_PAIR), wspec(2, S5_HALF)],
        out_specs=[rows(ncl), rows(ncc)],
        scratch_shapes=[pltpu.VMEM((2 * S5_HALF // LANE, (ncc + ncl) * nb, LANE), F32)],
        compiler_params=_cp("parallel", "parallel"),
        name="s5",
    )(zl, zc, rm, rwin, rwout, a_big)


def _mix_mlp_kernel(*refs, mod_row, s5_merge, final_norm, fb):
    if s5_merge:
        (h_ref, r_ref, y5_ref, u_ref, ds_ref, wg_ref, bg_ref, wo_ref, mod_ref, nm_ref, w1_ref, w2_ref,
         *rest) = refs
    else:
        h_ref, r_ref, wo_ref, mod_ref, nm_ref, w1_ref, w2_ref, *rest = refs
    if final_norm:
        nf_ref, o_ref, *scratch = rest
    else:
        o_ref, *scratch = rest
    row = pl.program_id(0) if mod_row is None else mod_row
    if s5_merge:
        (y_s,) = scratch
        nch = y_s.shape[1] // S5_CHUNK
        for lb in range(S5_LB):
            for s in range(S5_CHUNK):
                y_s[lb, pl.ds(s, nch, stride=S5_CHUNK), :] = y5_ref[0, lb, :, s * LANE:(s + 1) * LANE]
        y5 = jnp.concatenate([y_s[lb] for lb in range(S5_LB)], axis=-1)
        y = jax.nn.gelu(y5 + ds_ref[...] * u_ref[0])
        y = y * jax.nn.sigmoid(_dot(y.astype(BF16), wg_ref[...]) + bg_ref[...])
        mix = _dot(r_ref[0], wo_ref[0:RET_WIDTH, :]) + _dot(y.astype(BF16), wo_ref[RET_WIDTH:D_MODEL, :])
    else:
        mix = _dot(r_ref[0], wo_ref[...])
    h1 = h_ref[0] + _mod_chunk(mod_ref, row, 2) * mix
    xn = _rms(h1) * nm_ref[...]
    xm = (xn * (1.0 + _mod_chunk(mod_ref, row, 4)) + _mod_chunk(mod_ref, row, 3)).astype(BF16)
    acc = None
    for j in range(D_FF // fb):
        a = jnp.square(jnp.maximum(_dot(xm, w1_ref[:, j * fb:(j + 1) * fb].astype(BF16)), 0.0)).astype(BF16)
        part = _dot(a, w2_ref[j * fb:(j + 1) * fb, :].astype(BF16))
        acc = part if acc is None else acc + part
    h2 = h1 + _mod_chunk(mod_ref, row, 5) * acc
    if final_norm:
        h2 = _rms(h2) * nf_ref[...]
    o_ref[0] = h2


def _mix_mlp(h, r, s5y, u, s5p, wo, mod, layer, nm, w1, w2, nf, mod_row, tm, fb, name):
    b, n, _ = h.shape
    s5_merge = s5y is not None
    final_norm = nf is not None
    one = pl.Buffered(1)
    row_spec = lambda width: pl.BlockSpec((1, tm, width), lambda i, t: (i, t, 0))
    const = lambda shape: pl.BlockSpec(shape, lambda i, t: (0,) * len(shape), pipeline_mode=one)
    in_specs = [row_spec(D_MODEL), row_spec(r.shape[-1])]
    args = [h, r]
    if s5_merge:
        d_skip, w_glu, b_glu = s5p
        z_spec = pl.BlockSpec((1, S5_LB, tm // S5_CHUNK, S5_BIG), lambda i, t: (i, 0, t, 0))
        in_specs += [z_spec, row_spec(S5_WIDTH), const((1, S5_WIDTH)),
                     const((S5_WIDTH, S5_WIDTH)), const((1, S5_WIDTH))]
        args += [s5y, u, d_skip, w_glu, b_glu]
    in_specs += [const((D_MODEL, D_MODEL)),
                 pl.BlockSpec((1, MOD_ROWS, N_MOD * D_MODEL), lambda i, t: (layer, 0, 0), pipeline_mode=one),
                 const((1, D_MODEL))]
    if w1.ndim == 3:
        in_specs += [pl.BlockSpec((None, D_MODEL, D_FF), lambda i, t: (layer, 0, 0), pipeline_mode=one),
                     pl.BlockSpec((None, D_FF, D_MODEL), lambda i, t: (layer, 0, 0), pipeline_mode=one)]
    else:
        in_specs += [const((D_MODEL, D_FF)), const((D_FF, D_MODEL))]
    args += [wo, mod, nm, w1, w2]
    if final_norm:
        in_specs.append(const((1, D_MODEL)))
        args.append(nf)
    return pl.pallas_call(
        functools.partial(_mix_mlp_kernel, mod_row=mod_row, s5_merge=s5_merge, final_norm=final_norm, fb=fb),
        out_shape=jax.ShapeDtypeStruct((b, n, D_MODEL), F32),
        grid=(b, n // tm),
        in_specs=in_specs,
        out_specs=row_spec(D_MODEL),
        scratch_shapes=[pltpu.VMEM((S5_LB, tm, LANE), F32)] if s5_merge else [],
        compiler_params=_cp("parallel", "parallel"),
        name=name,
    )(*args)


def _hgrn_lower_bounds(lbl_ref, layer):
    out = []
    for d in range(2):
        z = [lbl_ref[d, k:k + 1, :] for k in range(DEPTH)]
        zmax = functools.reduce(jnp.maximum, z)
        e = [jnp.exp(v - zmax) for v in z]
        tot = functools.reduce(lambda a, b_: a + b_, e)
        lb = jnp.zeros_like(tot)
        for k in range(1, layer + 1):
            lb = lb + e[k] / tot
        out.append(lb)
    return out


def _inproj1_kernel(h_ref, mod_ref, ng_ref, lbl_ref, w_ref, *out_refs, mod_row, layer, latent):
    row = pl.program_id(0) if mod_row is None else mod_row
    xn = _rms(h_ref[0]) * ng_ref[...]
    xm = (xn * (1.0 + _mod_chunk(mod_ref, row, 1)) + _mod_chunk(mod_ref, row, 0)).astype(BF16)
    lbs = _hgrn_lower_bounds(lbl_ref, layer)
    names = ((["q"] if latent else []) + ["lf0h", "lf0l", "kk0", "lf1h", "lf1l", "kk1", "i"]
             + (["sg"] if latent else []))
    out = dict(zip(names, out_refs))
    half = h_ref.shape[1] // 2

    def finish(kind, y, rows):
        if kind in ("lf0", "lf1"):
            d = int(kind[2])
            t = (1.0 - lbs[d]) * _sigmoid(y)
            lf = jnp.log(lbs[d] + t)
            hi = lf.astype(BF16)
            out[kind + "h"][0, rows, :] = hi
            out[kind + "l"][0, rows, :] = (lf - hi.astype(F32)).astype(BF16)
            out["kk%d" % d][0, rows, :] = ((1.0 - lbs[d]) - t).astype(BF16)
        elif kind == "sg":
            out[kind][0, rows, :] = (y * _sigmoid(y)).astype(BF16)
        else:
            out[kind][0, rows, :] = y.astype(BF16)

    cols = [("lf0", 1), ("lf1", 2)] + ([("sg", 4), ("q", 0)] if latent else []) + [("i", 3)]
    pending = None
    for kind, k in cols:
        wk = w_ref[:, k * D_MODEL:(k + 1) * D_MODEL].astype(BF16)
        for rows in (slice(0, half), slice(half, 2 * half)):
            y = _dot(xm[rows], wk)
            if pending is not None:
                finish(*pending)
            pending = (kind, y, rows)
    finish(*pending)


def _inproj1(h, mod, layer, ng, lb_logits, w, latent, mod_row, tm, name):
    b, n, _ = h.shape
    one = pl.Buffered(1)
    row_spec = pl.BlockSpec((1, tm, D_MODEL), lambda i, t: (i, t, 0))
    n_out = 9 if latent else 7
    return pl.pallas_call(
        functools.partial(_inproj1_kernel, mod_row=mod_row, layer=layer, latent=latent),
        out_shape=[jax.ShapeDtypeStruct((b, n, D_MODEL), BF16)] * n_out,
        grid=(b, n // tm),
        in_specs=[row_spec,
                  pl.BlockSpec((1, MOD_ROWS, N_MOD * D_MODEL), lambda i, t: (layer, 0, 0), pipeline_mode=one),
                  pl.BlockSpec((1, D_MODEL), lambda i, t: (0, 0), pipeline_mode=one),
                  pl.BlockSpec(lb_logits.shape, lambda i, t: (0, 0, 0), pipeline_mode=one),
                  pl.BlockSpec(w.shape, lambda i, t: (0, 0), pipeline_mode=one)],
        out_specs=[row_spec] * n_out,
        compiler_params=_cp("parallel", "parallel"),
        name=name,
    )(h, mod, ng, lb_logits, w)


def _hgrn_kernel(ng_ref, ql, fhl, fll, kfl, bhl, bll, kbl, il, sgl, fhc, flc, kfc, bhc, blc, kbc, ic, o_ref,
                 qin_s, att_s, kv_s, et_s, kvc_s, etc_s, cum_s, ko_s, qt_s, kt_s, *, nbl, nbc):
    cb = HG_BLOCK
    mid = cb // 2
    gb = HG_GROUP
    gr = gb * cb
    dk = HG_DK

    ri = lax.broadcasted_iota(jnp.int32, (gr, gr), 0)
    ci = lax.broadcasted_iota(jnp.int32, (gr, gr), 1)
    same = (ri // cb) == (ci // cb)
    rb = lax.broadcasted_iota(jnp.int32, (cb, cb), 0)
    cbi = lax.broadcasted_iota(jnp.int32, (cb, cb), 1)
    tri_l = jnp.where(same & (ri >= ci), 1.0, 0.0).astype(BF16)
    tri_u = jnp.where(same & (ri <= ci), 1.0, 0.0).astype(BF16)
    dirs = ((0, rb >= cbi, mid - 1, cb - 1), (1, rb <= cbi, mid, 0))

    def cumsums(fh, fl, bh, bl, slot):
        pre = _dot(tri_l, jnp.concatenate([fh, fl], axis=-1))
        suf = _dot(tri_u, jnp.concatenate([bh, bl], axis=-1))
        cum_s[slot, 0] = pre[:, :dk] + pre[:, dk:]
        cum_s[slot, 1] = suf[:, :dk] + suf[:, dk:]

    def operands(slot, kks, q, n0, et_ref, which=(0, 1)):
        for d, keep, ref_row, tot_row in [dirs[i] for i in which]:
            cum = cum_s[slot, d].reshape(gb, cb, dk)
            kk = kks[d].astype(F32).reshape(gb, cb, dk)
            ref = cum[:, ref_row:ref_row + 1, :]
            tot = cum[:, tot_row:tot_row + 1, :]
            e = cum - ref
            kt = kk * jnp.exp(-e)
            ko_s[slot, d] = (kt * jnp.exp(tot - ref)).astype(BF16).reshape(gr, dk)
            e_tot = jnp.exp(tot)
            for j in range(gb):
                et_ref[n0 + j, :, d * dk:(d + 1) * dk] = e_tot[j]
            if q is not None:
                qt = q.astype(F32).reshape(gb, cb, dk) * jnp.exp(e)
                qin_s[n0 * cb:n0 * cb + gr, d * dk:(d + 1) * dk] = (
                    (qt * jnp.exp(ref)).astype(BF16).reshape(gr, dk))
                qt_s[slot, d] = qt.astype(BF16).reshape(gr, dk)
                kt_s[slot, d] = kt.astype(BF16).reshape(gr, dk)

    def matmuls(slot, v, with_q, n0, kv_ref, blocks=tuple(range(HG_GROUP))):
        for d in range(2):
            for j in blocks:
                rows = slice(j * cb, (j + 1) * cb)
                kv_ref[n0 + j, :, d * dk:(d + 1) * dk] = _dot_tn(v[rows], ko_s[slot, d, rows, :])
        if with_q:
            for j in blocks:
                rows = slice(j * cb, (j + 1) * cb)
                att = (jnp.where(dirs[0][1], _dot_nt(qt_s[slot, 0, rows, :], kt_s[slot, 0, rows, :]), 0.0)
                       + jnp.where(dirs[1][1], _dot_nt(qt_s[slot, 1, rows, :], kt_s[slot, 1, rows, :]), 0.0))
                att_s[n0 + j] = att.astype(BF16)

    for g in range(nbc // gb):
        sl = slice(g * gr, (g + 1) * gr)
        cumsums(fhc[0, sl, :], flc[0, sl, :], bhc[0, sl, :], blc[0, sl, :], 0)
        operands(0, (kfc[0, sl, :], kbc[0, sl, :]), None, g * gb, etc_s)
        matmuls(0, ic[0, sl, :], False, g * gb, kvc_s)

    ngl = nbl // gb
    rows_of = lambda g: slice(g * gr, (g + 1) * gr)

    def lat_cumsums(g, slot):
        r = rows_of(g)
        cumsums(fhl[0, r, :], fll[0, r, :], bhl[0, r, :], bll[0, r, :], slot)

    def lat_operands(g, slot, which=(0, 1)):
        operands(slot, (kfl[0, rows_of(g), :], kbl[0, rows_of(g), :]), ql[0, rows_of(g), :], g * gb, et_s, which)

    def lat_matmuls(g, slot, blocks=tuple(range(HG_GROUP))):
        matmuls(slot, il[0, rows_of(g), :], True, g * gb, kv_s, blocks)

    fwd, bwd = slice(0, dk), slice(dk, 2 * dk)
    st_f = jnp.zeros((dk, dk), F32)
    st_b = jnp.zeros((dk, dk), F32)
    for n in range(nbc):
        st_f = st_f * etc_s[n, :, fwd] + kvc_s[n, :, fwd]
    for n in reversed(range(nbc)):
        st_b = st_b * etc_s[n, :, bwd] + kvc_s[n, :, bwd]

    def fwd_steps(st, blocks):
        for n in blocks:
            inc = kv_s[n, :, fwd]
            kv_s[n, :, fwd] = st
            st = st * et_s[n, :, fwd] + inc
        return st

    quarter = [tuple(range(i * gb // 4, (i + 1) * gb // 4)) for i in range(4)]
    lat_cumsums(0, 0)
    lat_operands(0, 0)
    lat_cumsums(1, 1)
    for a in range(ngl):
        sa, sb = a % 2, 1 - a % 2
        prev = [(a - 1) * gb + j for j in range(gb)] if a > 0 else []
        lat_matmuls(a, sa, quarter[0])
        if a + 1 < ngl:
            lat_operands(a + 1, sb, (0,))
        lat_matmuls(a, sa, quarter[1])
        st_f = fwd_steps(st_f, prev[:gb // 2])
        if a + 2 < ngl:
            lat_cumsums(a + 2, sa)
        lat_matmuls(a, sa, quarter[2])
        if a + 1 < ngl:
            lat_operands(a + 1, sb, (1,))
        lat_matmuls(a, sa, quarter[3])
        st_f = fwd_steps(st_f, prev[gb // 2:])
    fwd_steps(st_f, [(ngl - 1) * gb + j for j in range(gb)])

    for n in reversed(range(nbl)):
        sl = slice(n * cb, (n + 1) * cb)
        kv = jnp.concatenate([kv_s[n, :, fwd].astype(BF16), st_b.astype(BF16)], axis=1)
        st_b = st_b * et_s[n, :, bwd] + kv_s[n, :, bwd]
        o = _dot(att_s[n], il[0, sl, :]) + _dot_nt(qin_s[sl, :], kv)
        o = _rms(o) * ng_ref[...] * sgl[0, sl, :].astype(F32)
        o_ref[0, sl, :] = o.astype(BF16)


def _hgrn(norm_g, q_l, fh_l, fl_l, kf_l, bh_l, bl_l, kb_l, i_l, sg_l, fh_c, fl_c, kf_c, bh_c, bl_c, kb_c, i_c):
    b, n, _ = q_l.shape
    nc = i_c.shape[1]
    nbl, nbc = n // HG_BLOCK, nc // HG_BLOCK
    assert nbl % HG_GROUP == 0 and nbc % HG_GROUP == 0 and nbl >= 2 * HG_GROUP and HG_GROUP % 4 == 0
    spec = lambda rows: pl.BlockSpec((1, rows, HG_DK), lambda i, h: (i, 0, h))
    slot = lambda dt: pltpu.VMEM((2, 2, HG_GROUP * HG_BLOCK, HG_DK), dt)
    return pl.pallas_call(
        functools.partial(_hgrn_kernel, nbl=nbl, nbc=nbc),
        out_shape=jax.ShapeDtypeStruct((b, n, D_MODEL), BF16),
        grid=(b, HG_HEADS),
        in_specs=[pl.BlockSpec((1, HG_DK), lambda i, h: (0, 0))] + [spec(n)] * 9 + [spec(nc)] * 7,
        out_specs=spec(n),
        scratch_shapes=[pltpu.VMEM((n, 2 * HG_DK), BF16),
                        pltpu.VMEM((nbl, HG_BLOCK, HG_BLOCK), BF16),
                        pltpu.VMEM((nbl, HG_DK, 2 * HG_DK), F32),
                        pltpu.VMEM((nbl, 1, 2 * HG_DK), F32),
                        pltpu.VMEM((nbc, HG_DK, 2 * HG_DK), F32),
                        pltpu.VMEM((nbc, 1, 2 * HG_DK), F32),
                        slot(F32), slot(BF16), slot(BF16), slot(BF16)],
        compiler_params=_cp("parallel", "parallel"),
        name="hgrn2",
    )(norm_g, q_l, fh_l, fl_l, kf_l, bh_l, bl_l, kb_l, i_l, sg_l, fh_c, fl_c, kf_c, bh_c, bl_c, kb_c, i_c)


def _rope_tables(n_tok):
    tok = jnp.arange(n_tok, dtype=jnp.int32)[:, None]
    row = (tok // GRID_W).astype(F32)
    col = (tok % GRID_W).astype(F32)
    n_freq = RET_DK // 4
    lane = jnp.arange(LANE, dtype=jnp.int32)[None, :]
    j = lane % (2 * n_freq)
    inv = ROPE_BASE ** (-(j % n_freq).astype(F32) / n_freq)
    ang = jnp.where(j < n_freq, row, col) * inv
    sign = jnp.where(lane % RET_DK < RET_DK // 2, -1.0, 1.0)
    return jnp.cos(ang), jnp.sin(ang) * sign


def kernel(x, c, ctx, c_ctx, w_mod, b_mod, norm_mix, norm_mlp, w_mlp_in, w_mlp_out, ab_w_in, ab_w_out, ret_logit, s5_a_re, s5_a_im, s5_log_dt, s5_b_re, s5_b_im, s5_c_re, s5_c_im, s5_d, s5_w_glu, s5_b_glu, hg_w_in, hg_w_out, hg_lb_logits, hg_norm, norm_final):
    b, n, d = x.shape
    nc = ctx.shape[1]
    assert d == D_MODEL and b + 1 <= MOD_ROWS and w_mod.shape[0] == DEPTH == 2
    assert n % 512 == 0 and nc % 256 == 0 and n % GRID_W == 0
    ctx_row = b
    tm_l, tm_c = 512, 256

    cc = jnp.zeros((MOD_ROWS, d), F32).at[:b].set(c).at[b].set(c_ctx)
    mod = _adaln(cc, w_mod, b_mod)

    row2 = lambda a: a.reshape(1, -1)
    w_in0 = ab_w_in[0]
    cos, sin = _rope_tables(n)
    ng0 = row2(norm_mix[0])
    q_l, k_l, v_l, u_l, uz_l, g_l = _inproj0(x, mod, 0, ng0, w_in0, cos, sin, None, tm_l)
    q_c, k_c, v_c, u_c, uz_c, g_c = _inproj0(ctx, mod, 0, ng0, w_in0, None, None, ctx_row, tm_c)

    log_gamma = jax.nn.log_sigmoid(ret_logit[0].astype(F32))
    lg_rows = jnp.broadcast_to(log_gamma.reshape(2 * RET_HEADS, 1), (2 * RET_HEADS, 2 * RET_DK))
    r_l, r_c = _retention(lg_rows, q_l, k_l, v_l, g_l, q_c, k_c, v_c, g_c)

    s5_ops = _s5_prep(s5_a_re[0], s5_a_im[0], s5_log_dt[0], s5_b_re[0], s5_b_im[0], s5_c_re[0], s5_c_im[0])
    y5_l, y5_c = _s5(s5_ops, uz_l, uz_c)

    s5p = (row2(s5_d[0]), s5_w_glu[0].astype(BF16), row2(s5_b_glu[0]))
    wo0 = ab_w_out[0].astype(BF16)
    w1_0, w2_0 = w_mlp_in, w_mlp_out
    nm0 = row2(norm_mlp[0])
    h_l = _mix_mlp(x, r_l, y5_l, u_l, s5p, wo0, mod, 0, nm0, w1_0, w2_0, None, None, tm_l, 1024, "mix_mlp0_lat")
    h_c = _mix_mlp(ctx, r_c, y5_c, u_c, s5p, wo0, mod, 0, nm0, w1_0, w2_0, None, ctx_row, tm_c, 1024, "mix_mlp0_ctx")

    w_in1 = hg_w_in[0]
    ng1 = row2(norm_mix[1])
    lat1 = _inproj1(h_l, mod, 1, ng1, hg_lb_logits, w_in1, True, None, tm_l, "inproj1_lat")
    ctx1 = _inproj1(h_c, mod, 1, ng1, hg_lb_logits, w_in1, False, ctx_row, tm_c, "inproj1_ctx")
    o1 = _hgrn(row2(hg_norm[0]), *lat1, *ctx1)
    return _mix_mlp(h_l, o1, None, None, None, hg_w_out[0].astype(BF16), mod, 1, row2(norm_mlp[1]),
                    w_mlp_in, w_mlp_out, row2(norm_final), None, tm_l, 1024,
                    "mix_mlp1_lat")
```

```python
import functools

import jax
import jax.numpy as jnp
from jax import lax
from jax.experimental import pallas as pl
from jax.experimental.pallas import tpu as pltpu

F32 = jnp.float32
BF16 = jnp.bfloat16

D_MODEL = 1024
DEPTH = 2
GRID_W = 64
EPS = 1e-6
ROPE_BASE = 10000.0
N_MOD = 6
RET_HEADS = 4
RET_DK = 64
RET_DV = 128
RET_QK = RET_HEADS * RET_DK
RET_WIDTH = RET_HEADS * RET_DV
RET_CHUNK = 128
S5_WIDTH = D_MODEL - RET_WIDTH
S5_GROUP = 16
S5_GROUPS = S5_WIDTH // S5_GROUP
S5_STATE = 64
S5_CHUNK = 16
S5_ROW = S5_CHUNK * S5_GROUP
S5_PAIR = 2 * S5_ROW
S5_BATCH_ROWS = 2
LANE = 128
S5_LB = S5_WIDTH // LANE
S5_GPB = LANE // S5_GROUP
S5_BIG = S5_CHUNK * LANE
S5_HALF = S5_GPB * 2 * S5_STATE
S5_POW = 32
AB_IN = 2 * RET_QK + 2 * RET_WIDTH + S5_WIDTH
HG_HEADS = 8
HG_DK = D_MODEL // HG_HEADS
HG_BLOCK = 64
HG_GROUP = 4
D_FF = 4 * D_MODEL
MOD_ROWS = 16

VMEM_LIMIT_BYTES = 56 * 1024 * 1024


def _cp(*sem):
    return pltpu.CompilerParams(dimension_semantics=sem, vmem_limit_bytes=VMEM_LIMIT_BYTES)


def _dot(a, b):
    return jnp.dot(a, b, preferred_element_type=F32)


def _dot_nt(a, b):
    return lax.dot_general(a, b, (((1,), (1,)), ((), ())), preferred_element_type=F32)


def _dot_tn(a, b):
    return lax.dot_general(a, b, (((0,), (0,)), ((), ())), preferred_element_type=F32)


def _sigmoid(x):
    return 0.5 * jnp.tanh(0.5 * x) + 0.5


def _rms(x):
    return x * lax.rsqrt(jnp.mean(x * x, axis=-1, keepdims=True) + EPS)


def _mod_chunk(mod_ref, row, i):
    return mod_ref[0, pl.ds(row, 1), i * D_MODEL:(i + 1) * D_MODEL]


def _adaln_kernel(cc_ref, w_ref, b_ref, o_ref):
    s = jax.nn.silu(cc_ref[...]).astype(BF16)
    o_ref[0] = _dot(s, w_ref[0].astype(BF16)) + b_ref[0]


def _adaln(cc, w_mod, b_mod):
    bn = 1536
    n = N_MOD * D_MODEL
    return pl.pallas_call(
        _adaln_kernel,
        out_shape=jax.ShapeDtypeStruct((DEPTH, MOD_ROWS, n), F32),
        grid=(DEPTH, n // bn),
        in_specs=[
            pl.BlockSpec((MOD_ROWS, D_MODEL), lambda l, j: (0, 0)),
            pl.BlockSpec((1, D_MODEL, bn), lambda l, j: (l, 0, j)),
            pl.BlockSpec((1, 1, bn), lambda l, j: (l, 0, j)),
        ],
        out_specs=pl.BlockSpec((1, MOD_ROWS, bn), lambda l, j: (l, 0, j)),
        compiler_params=_cp("parallel", "parallel"),
        name="adaln",
    )(cc, w_mod, b_mod.reshape(DEPTH, 1, n))


def _rope(t, cos, sin):
    lane = lax.broadcasted_iota(jnp.int32, t.shape, 1)
    first = (lane & (RET_DK // 2)) == 0
    w = t.shape[1]
    swapped = jnp.where(first, pltpu.roll(t, w - RET_DK // 2, 1), pltpu.roll(t, RET_DK // 2, 1))
    return t * cos + swapped * sin


def _inproj0_kernel(*refs, mod_row, rope):
    if rope:
        h_ref, mod_ref, ng_ref, w_ref, cos_ref, sin_ref, q_ref, k_ref, v_ref, u_ref, uz_ref, g_ref, u_s = refs
    else:
        h_ref, mod_ref, ng_ref, w_ref, q_ref, k_ref, v_ref, u_ref, uz_ref, g_ref, u_s = refs
    row = pl.program_id(0) if mod_row is None else mod_row
    xn = _rms(h_ref[0]) * ng_ref[...]
    xm = (xn * (1.0 + _mod_chunk(mod_ref, row, 1)) + _mod_chunk(mod_ref, row, 0)).astype(BF16)
    y = _dot(xm, w_ref[...].astype(BF16))
    q = y[:, 0:RET_QK]
    k = y[:, RET_QK:2 * RET_QK]
    if rope:
        cos = jnp.concatenate([cos_ref[...]] * (RET_QK // LANE), axis=1)
        sin = jnp.concatenate([sin_ref[...]] * (RET_QK // LANE), axis=1)
        q = _rope(q, cos, sin)
        k = _rope(k, cos, sin)
    q_ref[0] = q.astype(BF16)
    k_ref[0] = (k * (RET_DK ** -0.5)).astype(BF16)
    c0 = 2 * RET_QK
    v_ref[0] = y[:, c0:c0 + RET_WIDTH].astype(BF16)
    u0 = c0 + RET_WIDTH
    u_ref[0] = y[:, u0:u0 + S5_WIDTH]
    g = y[:, u0 + S5_WIDTH:]
    g_ref[0] = (g * _sigmoid(g)).astype(BF16)
    nch = u_s.shape[1] // S5_CHUNK
    for j in range(S5_LB):
        u_s[j] = y[:, u0 + j * LANE:u0 + (j + 1) * LANE]
        for s in range(S5_CHUNK):
            uz_ref[0, j, :, s * LANE:(s + 1) * LANE] = u_s[j, pl.ds(s, nch, stride=S5_CHUNK), :].astype(BF16)


def _inproj0(h, mod, layer, ng, w, cos, sin, mod_row, tm):
    b, n, _ = h.shape
    rope = cos is not None
    row_spec = lambda width: pl.BlockSpec((1, tm, width), lambda i, j: (i, j, 0))
    in_specs = [
        row_spec(D_MODEL),
        pl.BlockSpec((1, MOD_ROWS, N_MOD * D_MODEL), lambda i, j: (layer, 0, 0)),
        pl.BlockSpec((1, D_MODEL), lambda i, j: (0, 0)),
        pl.BlockSpec((D_MODEL, AB_IN), lambda i, j: (0, 0), pipeline_mode=pl.Buffered(1)),
    ]
    args = [h, mod, ng, w]
    if rope:
        in_specs += [pl.BlockSpec((tm, LANE), lambda i, j: (j, 0))] * 2
        args += [cos, sin]
    widths = (RET_QK, RET_QK, RET_WIDTH, S5_WIDTH, RET_WIDTH)
    dtypes = (BF16, BF16, BF16, F32, BF16)
    out_shape = [jax.ShapeDtypeStruct((b, n, wd), dt) for wd, dt in zip(widths, dtypes)]
    out_specs = [row_spec(wd) for wd in widths]
    out_shape.insert(4, jax.ShapeDtypeStruct((b, S5_LB, n // S5_CHUNK, S5_BIG), BF16))
    out_specs.insert(4, pl.BlockSpec((1, S5_LB, tm // S5_CHUNK, S5_BIG), lambda i, j: (i, 0, j, 0)))
    return pl.pallas_call(
        functools.partial(_inproj0_kernel, mod_row=mod_row, rope=rope),
        out_shape=out_shape,
        grid=(b, n // tm),
        in_specs=in_specs,
        out_specs=out_specs,
        scratch_shapes=[pltpu.VMEM((S5_LB, tm, LANE), F32)],
        compiler_params=_cp("parallel", "parallel"),
        name="inproj0_lat" if rope else "inproj0_ctx",
    )(*args)


def _ret_kernel(lg_ref, ql, kl, vl, gl, qc, kc, vc, gc, rl, rc, st_s, *, ncl, ncc):
    c = RET_CHUNK
    dk2 = 2 * RET_DK
    nt = ncc + ncl
    p = pl.program_id(1)
    h_a = 2 * p
    lgf_a = lg_ref[pl.ds(h_a, 1), :]
    lgf_b = lg_ref[pl.ds(h_a + 1, 1), :]
    lgb_a = lg_ref[pl.ds(RET_HEADS + h_a, 1), :]
    lgb_b = lg_ref[pl.ds(RET_HEADS + h_a + 1, 1), :]
    lane = lax.broadcasted_iota(jnp.int32, (1, 2 * RET_DK), 1)
    is_a = lane < RET_DK
    lgf_lane = jnp.where(is_a, lgf_a, lgf_b)
    lgb_lane = jnp.where(is_a, lgb_a, lgb_b)
    ri = lax.broadcasted_iota(jnp.int32, (c, c), 0).astype(F32)
    ci = lax.broadcasted_iota(jnp.int32, (c, c), 1).astype(F32)
    diff = ri - ci

    def dmat(lgf, lgb):
        fwd = jnp.exp(jnp.maximum(diff, 0.0) * lgf)
        bwd = jnp.exp(jnp.maximum(-diff, 0.0) * lgb)
        return jnp.where(diff > 0, fwd, jnp.where(diff < 0, bwd, 2.0))

    d_a = dmat(lgf_a, lgb_a)
    d_b = dmat(lgf_b, lgb_b)
    rowp = lax.broadcasted_iota(jnp.int32, (c, dk2), 0).astype(F32)
    qd = jnp.concatenate([jnp.exp((rowp + 1.0) * lgf_lane), jnp.exp((c - rowp) * lgb_lane)], axis=1)
    kd = jnp.concatenate([jnp.exp((c - 1.0 - rowp) * lgf_lane), jnp.exp(rowp * lgb_lane)], axis=1)
    rowk = lax.broadcasted_iota(jnp.int32, (dk2, 2 * RET_DV), 0)
    cd_f = jnp.exp(c * jnp.where(rowk < RET_DK, lgf_a[:, :1], lgf_b[:, :1]))
    cd_b = jnp.exp(c * jnp.where(rowk < RET_DK, lgb_a[:, :1], lgb_b[:, :1]))
    mask2 = jnp.concatenate([is_a, is_a], axis=1)

    def rows_of(n):
        if n < ncc:
            return (qc, kc, vc, gc, rc), slice(n * c, (n + 1) * c)
        return (ql, kl, vl, gl, rl), slice((n - ncc) * c, (n - ncc + 1) * c)

    def increment(n):
        (_, k_ref, v_ref, _, _), sl = rows_of(n)
        k = k_ref[0, sl, :]
        kk = jnp.concatenate([k, k], axis=1).astype(F32) * kd
        return _dot_tn(kk.astype(BF16), v_ref[0, sl, :])

    sf = jnp.zeros((dk2, 2 * RET_DV), F32)
    inc = increment(0)
    for n in range(nt):
        nxt = increment(n + 1) if n + 1 < nt else None
        st_s[n, 0:dk2, :] = sf
        st_s[n, dk2:2 * dk2, :] = inc[dk2:2 * dk2]
        sf = cd_f * sf + inc[0:dk2]
        inc = nxt

    heads = ((True, d_a, 0), (False, d_b, RET_DV))

    def scores(n):
        (q_ref, k_ref, _, _, _), sl = rows_of(n)
        q, k = q_ref[0, sl, :], k_ref[0, sl, :]
        q2 = (jnp.concatenate([q, q], axis=1).astype(F32) * qd).astype(BF16)
        res = []
        for keep_a, dm, _ in heads:
            m1 = is_a if keep_a else jnp.logical_not(is_a)
            m2 = mask2 if keep_a else jnp.logical_not(mask2)
            att = _dot_nt(jnp.where(m1, q, jnp.zeros_like(q)), k) * dm
            res.append((att.astype(BF16), jnp.where(m2, q2, jnp.zeros_like(q2))))
        return res

    def output(n, sc, sb):
        (_, _, v_ref, g_ref, out_ref), sl = rows_of(n)
        v, g = v_ref[0, sl, :], g_ref[0, sl, :]
        s_n = jnp.concatenate([st_s[n, 0:dk2, :].astype(BF16), sb.astype(BF16)], axis=0)
        for (_, _, cs), (att, q2m) in zip(heads, sc):
            o = _dot(att, v[:, cs:cs + RET_DV]) + _dot(q2m, s_n[:, cs:cs + RET_DV])
            o = _rms(o) * g[:, cs:cs + RET_DV].astype(F32)
            out_ref[0, sl, cs:cs + RET_DV] = o.astype(BF16)

    order = list(range(ncc - 1, -1, -1)) + list(range(nt - 1, ncc - 1, -1))
    sb = jnp.zeros((dk2, 2 * RET_DV), F32)
    sc = scores(order[0])
    for i, n in enumerate(order):
        nxt = scores(order[i + 1]) if i + 1 < nt else None
        output(n, sc, sb)
        sb = cd_b * sb + st_s[n, dk2:2 * dk2, :]
        sc = nxt


def _retention(lg_rows, q_l, k_l, v_l, g_l, q_c, k_c, v_c, g_c):
    b, n, _ = q_l.shape
    nc = q_c.shape[1]
    ncl, ncc = n // RET_CHUNK, nc // RET_CHUNK
    pairs = RET_HEADS // 2

    def spec(rows, width):
        return pl.BlockSpec((1, rows, width), lambda i, p: (i, 0, p))

    return pl.pallas_call(
        functools.partial(_ret_kernel, ncl=ncl, ncc=ncc),
        out_shape=[jax.ShapeDtypeStruct((b, n, RET_WIDTH), BF16),
                   jax.ShapeDtypeStruct((b, nc, RET_WIDTH), BF16)],
        grid=(b, pairs),
        in_specs=[pl.BlockSpec((2 * RET_HEADS, 2 * RET_DK), lambda i, p: (0, 0)),
                  spec(n, 2 * RET_DK), spec(n, 2 * RET_DK), spec(n, 2 * RET_DV), spec(n, 2 * RET_DV),
                  spec(nc, 2 * RET_DK), spec(nc, 2 * RET_DK), spec(nc, 2 * RET_DV), spec(nc, 2 * RET_DV)],
        out_specs=[spec(n, 2 * RET_DV), spec(nc, 2 * RET_DV)],
        scratch_shapes=[pltpu.VMEM((ncl + ncc, 4 * RET_DK, 2 * RET_DV), F32)],
        compiler_params=_cp("parallel", "parallel"),
        name="retention",
    )(lg_rows, q_l, k_l, v_l, g_l, q_c, k_c, v_c, g_c)


def _dot_hi(a, b, contract=(1, 0)):
    dims = (((contract[0],), (contract[1],)), ((), ()))
    return lax.dot_general(a, b, dims, preferred_element_type=F32, precision=lax.Precision.HIGHEST)


def _dot_sel(a, b, contract, data):
    dims = (((contract[0],), (contract[1],)), ((), ()))
    x = (a, b)[data]
    hi = x.astype(BF16)
    lo = (x - hi.astype(F32)).astype(BF16)
    dd = lambda piece: lax.dot_general(*((piece, b) if data == 0 else (a, piece)), dims, preferred_element_type=F32)
    return dd(hi) + dd(lo)


def _s5_prep_kernel(ar_row, ai_row, ldt, btr, bti, ctr, cti, rm, rwin, rwout, abig):
    t, g, p, kp = S5_CHUNK, S5_GROUP, S5_STATE, S5_POW
    row = S5_ROW
    i0 = lambda shape: lax.broadcasted_iota(jnp.int32, shape, 0)
    i1 = lambda shape: lax.broadcasted_iota(jnp.int32, shape, 1)
    f32 = lambda m: jnp.where(m, 1.0, 0.0).astype(BF16)
    s_of_r = i0((row, kp)) // g
    k_of_l = i1((row, kp))
    sel_rows = (f32(k_of_l == t - 1 - s_of_r), f32(k_of_l == s_of_r))
    t_of_c = i1((kp, row)) // g
    k_of_s = i0((kp, row))
    sel_out = (f32(k_of_s == t_of_c + 1), f32(k_of_s == t - t_of_c))
    sel_lag = (f32(k_of_s == t_of_c), f32(k_of_s == t - 1 - t_of_c))
    tile_l = f32(i1((g, row)) % g == i0((g, row)))
    tile_r = f32(i0((row, g)) % g == i1((row, g)))
    lane = i1((g, row))
    k_col = i0((kp, 1)).astype(F32)
    k_row = i1((1, kp)).astype(F32)
    first = i0((8, 1)) == 0

    def outer(a, k):
        a8 = jnp.where(first, jnp.broadcast_to(a, (8, a.shape[1])), 0.0)
        return _dot_hi(a8, jnp.broadcast_to(k, (8, k.shape[1])), (0, 0))

    rwin[...] = jnp.zeros(rwin.shape, BF16)
    rwout[...] = jnp.zeros(rwout.shape, BF16)
    for q in range(S5_GPB):
        pair, qq = divmod(q, 2)
        lags = []
        for d in range(2):
            dt = jnp.exp(ldt[d, q])
            are_r, aim_r = ar_row[d, q], ai_row[d, q]
            mag = jnp.exp(are_r * dt)
            ang = aim_r * dt
            nr, ni = mag * jnp.cos(ang) - 1.0, mag * jnp.sin(ang)
            den = jnp.square(are_r) + jnp.square(aim_r)
            fr = (nr * are_r + ni * aim_r) / den
            fi = (ni * are_r - nr * aim_r) / den
            pm = jnp.exp(k_col * (are_r * dt))
            pa = k_col * ang
            pk_re, pk_im = pm * jnp.cos(pa), pm * jnp.sin(pa)
            pmt = jnp.exp(outer(are_r * dt, k_row))
            pat = outer(ang, k_row)
            pt_re, pt_im = pmt * jnp.cos(pat), pmt * jnp.sin(pat)
            bt_re, bt_im = _dot_sel(tile_r, btr[d, q], (1, 1), 1), _dot_sel(tile_r, bti[d, q], (1, 1), 1)
            bb_re = fr * bt_re - fi * bt_im
            bb_im = fr * bt_im + fi * bt_re
            pr_re, pr_im = _dot_sel(sel_rows[d], pk_re, (1, 0), 1), _dot_sel(sel_rows[d], pk_im, (1, 0), 1)
            w_re = pr_re * bb_re - pr_im * bb_im
            w_im = pr_re * bb_im + pr_im * bb_re
            for part, w in ((d, w_re), (2 + d, w_im)):
                c0 = part * LANE + qq * p
                rwin[0, q * row:(q + 1) * row, c0:c0 + p] = w.astype(BF16)
            ct_re, ct_im = _dot_sel(ctr[d, q], tile_l, (0, 0), 0), _dot_sel(cti[d, q], tile_l, (0, 0), 0)

            def c_pow(sel):
                pc_re, pc_im = _dot_sel(pt_re, sel, (1, 0), 0), _dot_sel(pt_im, sel, (1, 0), 0)
                return ct_re * pc_re - ct_im * pc_im, ct_re * pc_im + ct_im * pc_re

            o_re, o_im = c_pow(sel_out[d])
            for part, o in ((d, o_re), (2 + d, -o_im)):
                r0 = pair * S5_PAIR + part * LANE + qq * p
                rwout[0, r0:r0 + p, qq * row:(qq + 1) * row] = o.astype(BF16)
            r0 = d * S5_GPB * p + q * p
            l_re, l_im = c_pow(sel_lag[d])
            lags.append(_dot_hi(bb_re[0:g], l_re) - _dot_hi(bb_im[0:g], l_im))
            abig[0, 0:1, r0:r0 + p] = pk_re[t:t + 1, :]
            abig[0, 1:2, r0:r0 + p] = pk_im[t:t + 1, :]
        for s in range(t):
            fwd = jnp.where(lane >= g * s, pltpu.roll(lags[0], g * s, 1), 0.0)
            bwd = jnp.where(lane < g * (s + 1), pltpu.roll(lags[1], (row - g * (t - 1 - s)) % row, 1), 0.0)
            rm[0, q * row + s * g:q * row + (s + 1) * g, :] = (fwd + bwd).astype(BF16)


def _s5_prep(a_re, a_im, log_dt, b_re, b_im, c_re, c_im):
    gg, p, g = S5_GROUPS, S5_STATE, S5_GROUP
    f = lambda x: x.astype(F32)
    args = (f(a_re).reshape(2, gg, 1, p), f(a_im).reshape(2, gg, 1, p),
            f(log_dt).reshape(2, gg, 1, 1),
            f(b_re), f(b_im), f(c_re), f(c_im))
    spec = lambda r, c: pl.BlockSpec((2, S5_GPB, r, c), lambda j: (0, j, 0, 0))
    out = lambda r, c: pl.BlockSpec((1, r, c), lambda j: (j, 0, 0))
    return pl.pallas_call(
        _s5_prep_kernel,
        out_shape=[jax.ShapeDtypeStruct((S5_LB, S5_BIG, S5_ROW), BF16),
                   jax.ShapeDtypeStruct((S5_LB, S5_BIG, S5_PAIR), BF16),
                   jax.ShapeDtypeStruct((S5_LB, S5_BIG, S5_PAIR), BF16),
                   jax.ShapeDtypeStruct((S5_LB, 2, S5_HALF), F32)],
        grid=(S5_LB,),
        in_specs=[spec(1, p), spec(1, p), spec(1, 1),
                  spec(p, g), spec(p, g), spec(g, p), spec(g, p)],
        out_specs=[out(S5_BIG, S5_ROW), out(S5_BIG, S5_PAIR), out(S5_BIG, S5_PAIR), out(2, S5_HALF)],
        compiler_params=_cp("parallel"),
        name="s5_prep",
    )(*args)


def _s5_block_transpose(v):
    n = len(v)
    blk = lax.broadcasted_iota(jnp.int32, (1, LANE), 1) // S5_GROUP

    def spread(src, e):
        w = 1
        while w < n:
            bit = (blk // w) % 2 == 1
            src = [jnp.where(bit, src[(m + e * w) % n], src[m]) for m in range(n)]
            w *= 2
        return src

    diag = spread(v, 1)
    rolled = [diag[0]] + [pltpu.roll(diag[(-k) % n], LANE - k * S5_GROUP, 1) for k in range(1, n)]
    return spread(rolled, -1)


def _s5_kernel(ul, uc, rm_ref, rwin_ref, rwout_ref, a_ref, yl, yc, x_s, *, ncl, ncc):
    hw = S5_HALF
    gpb, half = S5_GPB, S5_CHUNK // 2
    cols = lambda x, i: x[:, i * LANE:(i + 1) * LANE]

    nb = ul.shape[0]
    nt = ncc + ncl
    pairs = gpb // 2
    slab = lambda k, r: k * pairs + r
    of_row = lambda i: pl.ds(i, nt, stride=nb)

    zg = []
    for i in range(nb):
        z = jnp.concatenate([uc[i, 0], ul[i, 0]], axis=0)
        zt = [_s5_block_transpose([cols(z, half * h + a) for a in range(half)]) for h in range(2)]
        zg.append([jnp.concatenate([zt[0][q], zt[1][q]], axis=1) for q in range(gpb)])
        for r in range(pairs):
            xw = _dot(jnp.concatenate([zg[i][2 * r], zg[i][2 * r + 1]], axis=1),
                      rwin_ref[0, r * S5_PAIR:(r + 1) * S5_PAIR, :])
            for k in range(4):
                x_s[slab(k, r), of_row(i), :] = cols(xw, k)
    a_re = a_ref[0, 0:1, :]
    a_im = a_ref[0, 1:2, :]

    def segment(base, n, carry):
        s_re, s_im = carry
        for i in range(n):
            rf = slice((base + i) * nb, (base + i + 1) * nb)
            rb = slice((base + n - 1 - i) * nb, (base + n - i) * nb)
            rows_of = (rf,) * pairs + (rb,) * pairs
            x_re = jnp.concatenate([x_s[c, rows_of[c], :] for c in range(2 * pairs)], axis=-1)
            x_im = jnp.concatenate([x_s[2 * pairs + c, rows_of[c], :] for c in range(2 * pairs)], axis=-1)
            for c in range(2 * pairs):
                x_s[c, rows_of[c], :] = cols(s_re, c)
                x_s[2 * pairs + c, rows_of[c], :] = cols(s_im, c)
            s_re, s_im = a_re * s_re - a_im * s_im + x_re, a_re * s_im + a_im * s_re + x_im
        return s_re, s_im

    zero = jnp.zeros((nb, hw), F32)
    carry = segment(0, ncc, (zero, zero))
    segment(ncc, ncl, carry)

    for i in range(nb):
        yt = [[None] * gpb, [None] * gpb]
        for r in range(pairs):
            xs = jnp.concatenate([x_s[slab(k, r), of_row(i), :] for k in range(4)], axis=1)
            yp = _dot(xs.astype(BF16), rwout_ref[0, r * S5_PAIR:(r + 1) * S5_PAIR, :])
            for qq in range(2):
                q = 2 * r + qq
                yq = (yp[:, qq * S5_ROW:(qq + 1) * S5_ROW]
                      + _dot(zg[i][q], rm_ref[0, q * S5_ROW:(q + 1) * S5_ROW, :]))
                yt[0][q], yt[1][q] = cols(yq, 0), cols(yq, 1)
        for h in range(2):
            for a, y in enumerate(_s5_block_transpose(yt[h])):
                c = slice((half * h + a) * LANE, (half * h + a + 1) * LANE)
                yc[i, 0, :, c] = y[0:ncc]
                yl[i, 0, :, c] = y[ncc:ncc + ncl]


def _s5(prep, zl, zc):
    rm, rwin, rwout, a_big = prep
    b, _, ncl, _ = zl.shape
    ncc = zc.shape[2]
    nb = S5_BATCH_ROWS if b % S5_BATCH_ROWS == 0 else 1
    rows = lambda r: pl.BlockSpec((nb, 1, r, S5_BIG), lambda j, i: (i, j, 0, 0))
    wspec = lambda r, c: pl.BlockSpec((1, r, c), lambda j, i: (j, 0, 0))
    return pl.pallas_call(
        functools.partial(_s5_kernel, ncl=ncl, ncc=ncc),
        out_shape=[jax.ShapeDtypeStruct(zl.shape, F32), jax.ShapeDtypeStruct(zc.shape, F32)],
        grid=(S5_LB, b // nb),
        in_specs=[rows(ncl), rows(ncc), wspec(S5_BIG, S5_ROW), wspec(S5_BIG, S5_PAIR),
                  wspec(S5_BIG, S5_PAIR), wspec(2, S5_HALF)],
        out_specs=[rows(ncl), rows(ncc)],
        scratch_shapes=[pltpu.VMEM((2 * S5_HALF // LANE, (ncc + ncl) * nb, LANE), F32)],
        compiler_params=_cp("parallel", "parallel"),
        name="s5",
    )(zl, zc, rm, rwin, rwout, a_big)


def _mix_mlp_kernel(*refs, mod_row, s5_merge, final_norm, fb):
    if s5_merge:
        (h_ref, r_ref, y5_ref, u_ref, ds_ref, wg_ref, bg_ref, wo_ref, mod_ref, nm_ref, w1_ref, w2_ref,
         *rest) = refs
    else:
        h_ref, r_ref, wo_ref, mod_ref, nm_ref, w1_ref, w2_ref, *rest = refs
    if final_norm:
        nf_ref, o_ref, *scratch = rest
    else:
        o_ref, *scratch = rest
    row = pl.program_id(0) if mod_row is None else mod_row
    if s5_merge:
        (y_s,) = scratch
        nch = y_s.shape[1] // S5_CHUNK
        for lb in range(S5_LB):
            for s in range(S5_CHUNK):
                y_s[lb, pl.ds(s, nch, stride=S5_CHUNK), :] = y5_ref[0, lb, :, s * LANE:(s + 1) * LANE]
        y5 = jnp.concatenate([y_s[lb] for lb in range(S5_LB)], axis=-1)
        y = jax.nn.gelu(y5 + ds_ref[...] * u_ref[0])
        y = y * jax.nn.sigmoid(_dot(y.astype(BF16), wg_ref[...]) + bg_ref[...])
        mix = _dot(r_ref[0], wo_ref[0:RET_WIDTH, :]) + _dot(y.astype(BF16), wo_ref[RET_WIDTH:D_MODEL, :])
    else:
        if len(r_ref.shape) == 4:
            r = jnp.concatenate([r_ref[0, hd] for hd in range(r_ref.shape[1])], axis=1)
        else:
            r = r_ref[0]
        mix = _dot(r, wo_ref[...])
    h1 = h_ref[0] + _mod_chunk(mod_ref, row, 2) * mix
    xn = _rms(h1) * nm_ref[...]
    xm = (xn * (1.0 + _mod_chunk(mod_ref, row, 4)) + _mod_chunk(mod_ref, row, 3)).astype(BF16)
    acc = None
    for j in range(D_FF // fb):
        a = jnp.square(jnp.maximum(_dot(xm, w1_ref[:, j * fb:(j + 1) * fb].astype(BF16)), 0.0)).astype(BF16)
        part = _dot(a, w2_ref[j * fb:(j + 1) * fb, :].astype(BF16))
        acc = part if acc is None else acc + part
    h2 = h1 + _mod_chunk(mod_ref, row, 5) * acc
    if final_norm:
        h2 = _rms(h2) * nf_ref[...]
    o_ref[0] = h2


def _mix_mlp(h, r, s5y, u, s5p, wo, mod, layer, nm, w1, w2, nf, mod_row, tm, fb, name):
    b, n, _ = h.shape
    s5_merge = s5y is not None
    final_norm = nf is not None
    one = pl.Buffered(1)
    row_spec = lambda width: pl.BlockSpec((1, tm, width), lambda i, t: (i, t, 0))
    const = lambda shape: pl.BlockSpec(shape, lambda i, t: (0,) * len(shape), pipeline_mode=one)
    if r.ndim == 4:
        r_spec = pl.BlockSpec((1, r.shape[1], tm, r.shape[3]), lambda i, t: (i, 0, t, 0))
    else:
        r_spec = row_spec(r.shape[-1])
    in_specs = [row_spec(D_MODEL), r_spec]
    args = [h, r]
    if s5_merge:
        d_skip, w_glu, b_glu = s5p
        z_spec = pl.BlockSpec((1, S5_LB, tm // S5_CHUNK, S5_BIG), lambda i, t: (i, 0, t, 0))
        in_specs += [z_spec, row_spec(S5_WIDTH), const((1, S5_WIDTH)),
                     const((S5_WIDTH, S5_WIDTH)), const((1, S5_WIDTH))]
        args += [s5y, u, d_skip, w_glu, b_glu]
    in_specs += [const((D_MODEL, D_MODEL)),
                 pl.BlockSpec((1, MOD_ROWS, N_MOD * D_MODEL), lambda i, t: (layer, 0, 0), pipeline_mode=one),
                 const((1, D_MODEL))]
    if w1.ndim == 3:
        in_specs += [pl.BlockSpec((None, D_MODEL, D_FF), lambda i, t: (layer, 0, 0), pipeline_mode=one),
                     pl.BlockSpec((None, D_FF, D_MODEL), lambda i, t: (layer, 0, 0), pipeline_mode=one)]
    else:
        in_specs += [const((D_MODEL, D_FF)), const((D_FF, D_MODEL))]
    args += [wo, mod, nm, w1, w2]
    if final_norm:
        in_specs.append(const((1, D_MODEL)))
        args.append(nf)
    return pl.pallas_call(
        functools.partial(_mix_mlp_kernel, mod_row=mod_row, s5_merge=s5_merge, final_norm=final_norm, fb=fb),
        out_shape=jax.ShapeDtypeStruct((b, n, D_MODEL), F32),
        grid=(b, n // tm),
        in_specs=in_specs,
        out_specs=row_spec(D_MODEL),
        scratch_shapes=[pltpu.VMEM((S5_LB, tm, LANE), F32)] if s5_merge else [],
        compiler_params=_cp("parallel", "parallel"),
        name=name,
    )(*args)


def _hgrn_lower_bounds(lbl_ref, layer):
    out = []
    for d in range(2):
        z = [lbl_ref[d, k:k + 1, :] for k in range(DEPTH)]
        zmax = functools.reduce(jnp.maximum, z)
        e = [jnp.exp(v - zmax) for v in z]
        tot = functools.reduce(lambda a, b_: a + b_, e)
        lb = jnp.zeros_like(tot)
        for k in range(1, layer + 1):
            lb = lb + e[k] / tot
        out.append(lb)
    return out


def _inproj1_kernel(h_ref, mod_ref, ng_ref, lbl_ref, w_ref, *out_refs, mod_row, layer, latent):
    row = pl.program_id(0) if mod_row is None else mod_row
    xn = _rms(h_ref[0]) * ng_ref[...]
    xm = (xn * (1.0 + _mod_chunk(mod_ref, row, 1)) + _mod_chunk(mod_ref, row, 0)).astype(BF16)
    lbs = _hgrn_lower_bounds(lbl_ref, layer)
    names = ((["q"] if latent else []) + ["lf0h", "lf0l", "kk0", "lf1h", "lf1l", "kk1", "i"]
             + (["sg"] if latent else []))
    out = dict(zip(names, out_refs))
    half = h_ref.shape[1] // 2

    def put(name, rows, val):
        for hd in range(HG_HEADS):
            out[name][0, hd, rows, :] = val[:, hd * HG_DK:(hd + 1) * HG_DK]

    def finish(kind, y, rows):
        if kind in ("lf0", "lf1"):
            d = int(kind[2])
            t = (1.0 - lbs[d]) * _sigmoid(y)
            lf = jnp.log(lbs[d] + t)
            hi = lf.astype(BF16)
            put(kind + "h", rows, hi)
            put(kind + "l", rows, (lf - hi.astype(F32)).astype(BF16))
            put("kk%d" % d, rows, ((1.0 - lbs[d]) - t).astype(BF16))
        elif kind == "sg":
            put(kind, rows, (y * _sigmoid(y)).astype(BF16))
        else:
            put(kind, rows, y.astype(BF16))

    cols = [("lf0", 1), ("lf1", 2)] + ([("sg", 4), ("q", 0)] if latent else []) + [("i", 3)]
    pending = None
    for kind, k in cols:
        wk = w_ref[:, k * D_MODEL:(k + 1) * D_MODEL].astype(BF16)
        for rows in (slice(0, half), slice(half, 2 * half)):
            y = _dot(xm[rows], wk)
            if pending is not None:
                finish(*pending)
            pending = (kind, y, rows)
    finish(*pending)


def _inproj1(h, mod, layer, ng, lb_logits, w, latent, mod_row, tm, name):
    b, n, _ = h.shape
    one = pl.Buffered(1)
    row_spec = pl.BlockSpec((1, tm, D_MODEL), lambda i, t: (i, t, 0))
    n_out = 9 if latent else 7
    return pl.pallas_call(
        functools.partial(_inproj1_kernel, mod_row=mod_row, layer=layer, latent=latent),
        out_shape=[jax.ShapeDtypeStruct((b, HG_HEADS, n, HG_DK), BF16)] * n_out,
        grid=(b, n // tm),
        in_specs=[row_spec,
                  pl.BlockSpec((1, MOD_ROWS, N_MOD * D_MODEL), lambda i, t: (layer, 0, 0), pipeline_mode=one),
                  pl.BlockSpec((1, D_MODEL), lambda i, t: (0, 0), pipeline_mode=one),
                  pl.BlockSpec(lb_logits.shape, lambda i, t: (0, 0, 0), pipeline_mode=one),
                  pl.BlockSpec(w.shape, lambda i, t: (0, 0), pipeline_mode=one)],
        out_specs=[pl.BlockSpec((1, HG_HEADS, tm, HG_DK), lambda i, t: (i, 0, t, 0))] * n_out,
        compiler_params=_cp("parallel", "parallel"),
        name=name,
    )(h, mod, ng, lb_logits, w)


def _hgrn_kernel(ng_ref, ql, fhl, fll, kfl, bhl, bll, kbl, il, sgl, fhc, flc, kfc, bhc, blc, kbc, ic, o_ref,
                 qin_s, att_s, kv_s, et_s, kvc_s, etc_s, cum_s, ko_s, qt_s, kt_s, *, nbl, nbc):
    cb = HG_BLOCK
    mid = cb // 2
    gb = HG_GROUP
    gr = gb * cb
    dk = HG_DK

    ri = lax.broadcasted_iota(jnp.int32, (gr, gr), 0)
    ci = lax.broadcasted_iota(jnp.int32, (gr, gr), 1)
    same = (ri // cb) == (ci // cb)
    rb = lax.broadcasted_iota(jnp.int32, (cb, cb), 0)
    cbi = lax.broadcasted_iota(jnp.int32, (cb, cb), 1)
    tri_l = jnp.where(same & (ri >= ci), 1.0, 0.0).astype(BF16)
    tri_u = jnp.where(same & (ri <= ci), 1.0, 0.0).astype(BF16)
    dirs = ((0, rb >= cbi, mid - 1, cb - 1), (1, rb <= cbi, mid, 0))

    def cumsums(fh, fl, bh, bl, slot):
        pre = _dot(tri_l, jnp.concatenate([fh, fl], axis=-1))
        suf = _dot(tri_u, jnp.concatenate([bh, bl], axis=-1))
        cum_s[slot, 0] = pre[:, :dk] + pre[:, dk:]
        cum_s[slot, 1] = suf[:, :dk] + suf[:, dk:]

    def operands(slot, kks, q, n0, et_ref, which=(0, 1)):
        for d, keep, ref_row, tot_row in [dirs[i] for i in which]:
            cum = cum_s[slot, d].reshape(gb, cb, dk)
            kk = kks[d].astype(F32).reshape(gb, cb, dk)
            ref = cum[:, ref_row:ref_row + 1, :]
            tot = cum[:, tot_row:tot_row + 1, :]
            e = cum - ref
            kt = kk * jnp.exp(-e)
            ko_s[slot, d] = (kt * jnp.exp(tot - ref)).astype(BF16).reshape(gr, dk)
            e_tot = jnp.exp(tot)
            for j in range(gb):
                et_ref[n0 + j, :, d * dk:(d + 1) * dk] = e_tot[j]
            if q is not None:
                qt = q.astype(F32).reshape(gb, cb, dk) * jnp.exp(e)
                qin_s[n0 * cb:n0 * cb + gr, d * dk:(d + 1) * dk] = (
                    (qt * jnp.exp(ref)).astype(BF16).reshape(gr, dk))
                qt_s[slot, d] = qt.astype(BF16).reshape(gr, dk)
                kt_s[slot, d] = kt.astype(BF16).reshape(gr, dk)

    def matmuls(slot, v, with_q, n0, kv_ref, blocks=tuple(range(HG_GROUP))):
        for d in range(2):
            for j in blocks:
                rows = slice(j * cb, (j + 1) * cb)
                kv_ref[n0 + j, :, d * dk:(d + 1) * dk] = _dot_tn(v[rows], ko_s[slot, d, rows, :])
        if with_q:
            for j in blocks:
                rows = slice(j * cb, (j + 1) * cb)
                att = (jnp.where(dirs[0][1], _dot_nt(qt_s[slot, 0, rows, :], kt_s[slot, 0, rows, :]), 0.0)
                       + jnp.where(dirs[1][1], _dot_nt(qt_s[slot, 1, rows, :], kt_s[slot, 1, rows, :]), 0.0))
                att_s[n0 + j] = att.astype(BF16)

    for g in range(nbc // gb):
        sl = slice(g * gr, (g + 1) * gr)
        cumsums(fhc[0, sl, :], flc[0, sl, :], bhc[0, sl, :], blc[0, sl, :], 0)
        operands(0, (kfc[0, sl, :], kbc[0, sl, :]), None, g * gb, etc_s)
        matmuls(0, ic[0, sl, :], False, g * gb, kvc_s)

    ngl = nbl // gb
    rows_of = lambda g: slice(g * gr, (g + 1) * gr)

    def lat_cumsums(g, slot):
        r = rows_of(g)
        cumsums(fhl[0, r, :], fll[0, r, :], bhl[0, r, :], bll[0, r, :], slot)

    def lat_operands(g, slot, which=(0, 1)):
        operands(slot, (kfl[0, rows_of(g), :], kbl[0, rows_of(g), :]), ql[0, rows_of(g), :], g * gb, et_s, which)

    def lat_matmuls(g, slot, blocks=tuple(range(HG_GROUP))):
        matmuls(slot, il[0, rows_of(g), :], True, g * gb, kv_s, blocks)

    fwd, bwd = slice(0, dk), slice(dk, 2 * dk)
    st_f = jnp.zeros((dk, dk), F32)
    st_b = jnp.zeros((dk, dk), F32)
    for n in range(nbc):
        st_f = st_f * etc_s[n, :, fwd] + kvc_s[n, :, fwd]
    for n in reversed(range(nbc)):
        st_b = st_b * etc_s[n, :, bwd] + kvc_s[n, :, bwd]

    def fwd_steps(st, blocks):
        for n in blocks:
            inc = kv_s[n, :, fwd]
            kv_s[n, :, fwd] = st
            st = st * et_s[n, :, fwd] + inc
        return st

    quarter = [tuple(range(i * gb // 4, (i + 1) * gb // 4)) for i in range(4)]
    lat_cumsums(0, 0)
    lat_operands(0, 0)
    lat_cumsums(1, 1)
    for a in range(ngl):
        sa, sb = a % 2, 1 - a % 2
        prev = [(a - 1) * gb + j for j in range(gb)] if a > 0 else []
        lat_matmuls(a, sa, quarter[0])
        if a + 1 < ngl:
            lat_operands(a + 1, sb, (0,))
        lat_matmuls(a, sa, quarter[1])
        st_f = fwd_steps(st_f, prev[:gb // 2])
        if a + 2 < ngl:
            lat_cumsums(a + 2, sa)
        lat_matmuls(a, sa, quarter[2])
        if a + 1 < ngl:
            lat_operands(a + 1, sb, (1,))
        lat_matmuls(a, sa, quarter[3])
        st_f = fwd_steps(st_f, prev[gb // 2:])
    fwd_steps(st_f, [(ngl - 1) * gb + j for j in range(gb)])

    for n in reversed(range(nbl)):
        sl = slice(n * cb, (n + 1) * cb)
        kv = jnp.concatenate([kv_s[n, :, fwd].astype(BF16), st_b.astype(BF16)], axis=1)
        st_b = st_b * et_s[n, :, bwd] + kv_s[n, :, bwd]
        o = _dot(att_s[n], il[0, sl, :]) + _dot_nt(qin_s[sl, :], kv)
        o = _rms(o) * ng_ref[...] * sgl[0, sl, :].astype(F32)
        o_ref[0, sl, :] = o.astype(BF16)


def _hgrn(norm_g, q_l, fh_l, fl_l, kf_l, bh_l, bl_l, kb_l, i_l, sg_l, fh_c, fl_c, kf_c, bh_c, bl_c, kb_c, i_c):
    b, _, n, _ = q_l.shape
    nc = i_c.shape[2]
    nbl, nbc = n // HG_BLOCK, nc // HG_BLOCK
    assert nbl % HG_GROUP == 0 and nbc % HG_GROUP == 0 and nbl >= 2 * HG_GROUP and HG_GROUP % 4 == 0
    spec = lambda rows: pl.BlockSpec((1, None, rows, HG_DK), lambda i, h: (i, h, 0, 0))
    slot = lambda dt: pltpu.VMEM((2, 2, HG_GROUP * HG_BLOCK, HG_DK), dt)
    return pl.pallas_call(
        functools.partial(_hgrn_kernel, nbl=nbl, nbc=nbc),
        out_shape=jax.ShapeDtypeStruct((b, HG_HEADS, n, HG_DK), BF16),
        grid=(b, HG_HEADS),
        in_specs=[pl.BlockSpec((1, HG_DK), lambda i, h: (0, 0))] + [spec(n)] * 9 + [spec(nc)] * 7,
        out_specs=spec(n),
        scratch_shapes=[pltpu.VMEM((n, 2 * HG_DK), BF16),
                        pltpu.VMEM((nbl, HG_BLOCK, HG_BLOCK), BF16),
                        pltpu.VMEM((nbl, HG_DK, 2 * HG_DK), F32),
                        pltpu.VMEM((nbl, 1, 2 * HG_DK), F32),
                        pltpu.VMEM((nbc, HG_DK, 2 * HG_DK), F32),
                        pltpu.VMEM((nbc, 1, 2 * HG_DK), F32),
                        slot(F32), slot(BF16), slot(BF16), slot(BF16)],
        compiler_params=_cp("parallel", "parallel"),
        name="hgrn2",
    )(norm_g, q_l, fh_l, fl_l, kf_l, bh_l, bl_l, kb_l, i_l, sg_l, fh_c, fl_c, kf_c, bh_c, bl_c, kb_c, i_c)


def _rope_tables(n_tok):
    tok = jnp.arange(n_tok, dtype=jnp.int32)[:, None]
    row = (tok // GRID_W).astype(F32)
    col = (tok % GRID_W).astype(F32)
    n_freq = RET_DK // 4
    lane = jnp.arange(LANE, dtype=jnp.int32)[None, :]
    j = lane % (2 * n_freq)
    inv = ROPE_BASE ** (-(j % n_freq).astype(F32) / n_freq)
    ang = jnp.where(j < n_freq, row, col) * inv
    sign = jnp.where(lane % RET_DK < RET_DK // 2, -1.0, 1.0)
    return jnp.cos(ang), jnp.sin(ang) * sign


def kernel(x, c, ctx, c_ctx, w_mod, b_mod, norm_mix, norm_mlp, w_mlp_in, w_mlp_out, ab_w_in, ab_w_out, ret_logit, s5_a_re, s5_a_im, s5_log_dt, s5_b_re, s5_b_im, s5_c_re, s5_c_im, s5_d, s5_w_glu, s5_b_glu, hg_w_in, hg_w_out, hg_lb_logits, hg_norm, norm_final):
    b, n, d = x.shape
    nc = ctx.shape[1]
    assert d == D_MODEL and b + 1 <= MOD_ROWS and w_mod.shape[0] == DEPTH == 2
    assert n % 512 == 0 and nc % 256 == 0 and n % GRID_W == 0
    ctx_row = b
    tm_l, tm_c = 512, 256

    cc = jnp.zeros((MOD_ROWS, d), F32).at[:b].set(c).at[b].set(c_ctx)
    mod = _adaln(cc, w_mod, b_mod)

    row2 = lambda a: a.reshape(1, -1)
    w_in0 = ab_w_in[0]
    cos, sin = _rope_tables(n)
    ng0 = row2(norm_mix[0])
    q_l, k_l, v_l, u_l, uz_l, g_l = _inproj0(x, mod, 0, ng0, w_in0, cos, sin, None, tm_l)
    q_c, k_c, v_c, u_c, uz_c, g_c = _inproj0(ctx, mod, 0, ng0, w_in0, None, None, ctx_row, tm_c)

    log_gamma = jax.nn.log_sigmoid(ret_logit[0].astype(F32))
    lg_rows = jnp.broadcast_to(log_gamma.reshape(2 * RET_HEADS, 1), (2 * RET_HEADS, 2 * RET_DK))
    r_l, r_c = _retention(lg_rows, q_l, k_l, v_l, g_l, q_c, k_c, v_c, g_c)

    s5_ops = _s5_prep(s5_a_re[0], s5_a_im[0], s5_log_dt[0], s5_b_re[0], s5_b_im[0], s5_c_re[0], s5_c_im[0])
    y5_l, y5_c = _s5(s5_ops, uz_l, uz_c)

    s5p = (row2(s5_d[0]), s5_w_glu[0].astype(BF16), row2(s5_b_glu[0]))
    wo0 = ab_w_out[0].astype(BF16)
    w1_0, w2_0 = w_mlp_in, w_mlp_out
    nm0 = row2(norm_mlp[0])
    h_l = _mix_mlp(x, r_l, y5_l, u_l, s5p, wo0, mod, 0, nm0, w1_0, w2_0, None, None, tm_l, 1024, "mix_mlp0_lat")
    h_c = _mix_mlp(ctx, r_c, y5_c, u_c, s5p, wo0, mod, 0, nm0, w1_0, w2_0, None, ctx_row, tm_c, 1024, "mix_mlp0_ctx")

    w_in1 = hg_w_in[0]
    ng1 = row2(norm_mix[1])
    lat1 = _inproj1(h_l, mod, 1, ng1, hg_lb_logits, w_in1, True, None, tm_l, "inproj1_lat")
    ctx1 = _inproj1(h_c, mod, 1, ng1, hg_lb_logits, w_in1, False, ctx_row, tm_c, "inproj1_ctx")
    o1 = _hgrn(row2(hg_norm[0]), *lat1, *ctx1)
    return _mix_mlp(h_l, o1, None, None, None, hg_w_out[0].astype(BF16), mod, 1, row2(norm_mlp[1]),
                    w_mlp_in, w_mlp_out, row2(norm_final), None, tm_l, 1024,
                    "mix_mlp1_lat")
```

```python
import functools

import jax
import jax.numpy as jnp
from jax import lax
from jax.experimental import pallas as pl
from jax.experimental.pallas import tpu as pltpu

F32 = jnp.float32
BF16 = jnp.bfloat16

D_MODEL = 1024
DEPTH = 2
GRID_W = 64
EPS = 1e-6
ROPE_BASE = 10000.0
N_MOD = 6
RET_HEADS = 4
RET_DK = 64
RET_DV = 128
RET_QK = RET_HEADS * RET_DK
RET_WIDTH = RET_HEADS * RET_DV
RET_CHUNK = 128
S5_WIDTH = D_MODEL - RET_WIDTH
S5_GROUP = 16
S5_GROUPS = S5_WIDTH // S5_GROUP
S5_STATE = 64
S5_CHUNK = 16
S5_ROW = S5_CHUNK * S5_GROUP
S5_PAIR = 2 * S5_ROW
S5_BATCH_ROWS = 2
LANE = 128
S5_LB = S5_WIDTH // LANE
S5_GPB = LANE // S5_GROUP
S5_BIG = S5_CHUNK * LANE
S5_HALF = S5_GPB * 2 * S5_STATE
S5_POW = 32
AB_IN = 2 * RET_QK + 2 * RET_WIDTH + S5_WIDTH
HG_HEADS = 8
HG_DK = D_MODEL // HG_HEADS
HG_BLOCK = 64
HG_GROUP = 4
D_FF = 4 * D_MODEL
MOD_ROWS = 16

VMEM_LIMIT_BYTES = 56 * 1024 * 1024


def _cp(*sem):
    return pltpu.CompilerParams(dimension_semantics=sem, vmem_limit_bytes=VMEM_LIMIT_BYTES)


def _dot(a, b):
    return jnp.dot(a, b, preferred_element_type=F32)


def _dot_nt(a, b):
    return lax.dot_general(a, b, (((1,), (1,)), ((), ())), preferred_element_type=F32)


def _dot_tn(a, b):
    return lax.dot_general(a, b, (((0,), (0,)), ((), ())), preferred_element_type=F32)


def _sigmoid(x):
    return 0.5 * jnp.tanh(0.5 * x) + 0.5


def _rms(x):
    return x * lax.rsqrt(jnp.mean(x * x, axis=-1, keepdims=True) + EPS)


def _mod_chunk(mod_ref, row, i):
    return mod_ref[0, pl.ds(row, 1), i * D_MODEL:(i + 1) * D_MODEL]


def _adaln_kernel(cc_ref, w_ref, b_ref, o_ref):
    s = jax.nn.silu(cc_ref[...]).astype(BF16)
    o_ref[0] = _dot(s, w_ref[0].astype(BF16)) + b_ref[0]


def _adaln(cc, w_mod, b_mod):
    bn = 1536
    n = N_MOD * D_MODEL
    return pl.pallas_call(
        _adaln_kernel,
        out_shape=jax.ShapeDtypeStruct((DEPTH, MOD_ROWS, n), F32),
        grid=(DEPTH, n // bn),
        in_specs=[
            pl.BlockSpec((MOD_ROWS, D_MODEL), lambda l, j: (0, 0)),
            pl.BlockSpec((1, D_MODEL, bn), lambda l, j: (l, 0, j)),
            pl.BlockSpec((1, 1, bn), lambda l, j: (l, 0, j)),
        ],
        out_specs=pl.BlockSpec((1, MOD_ROWS, bn), lambda l, j: (l, 0, j)),
        compiler_params=_cp("parallel", "parallel"),
        name="adaln",
    )(cc, w_mod, b_mod.reshape(DEPTH, 1, n))


def _rope(t, cos, sin):
    lane = lax.broadcasted_iota(jnp.int32, t.shape, 1)
    first = (lane & (RET_DK // 2)) == 0
    w = t.shape[1]
    swapped = jnp.where(first, pltpu.roll(t, w - RET_DK // 2, 1), pltpu.roll(t, RET_DK // 2, 1))
    return t * cos + swapped * sin


def _inproj0_kernel(*refs, mod_row, rope):
    if rope:
        h_ref, mod_ref, ng_ref, w_ref, cos_ref, sin_ref, q_ref, k_ref, v_ref, u_ref, uz_ref, g_ref, u_s = refs
    else:
        h_ref, mod_ref, ng_ref, w_ref, q_ref, k_ref, v_ref, u_ref, uz_ref, g_ref, u_s = refs
    row = pl.program_id(0) if mod_row is None else mod_row
    xn = _rms(h_ref[0]) * ng_ref[...]
    xm = (xn * (1.0 + _mod_chunk(mod_ref, row, 1)) + _mod_chunk(mod_ref, row, 0)).astype(BF16)
    y = _dot(xm, w_ref[...].astype(BF16))
    q = y[:, 0:RET_QK]
    k = y[:, RET_QK:2 * RET_QK]
    if rope:
        cos = jnp.concatenate([cos_ref[...]] * (RET_QK // LANE), axis=1)
        sin = jnp.concatenate([sin_ref[...]] * (RET_QK // LANE), axis=1)
        q = _rope(q, cos, sin)
        k = _rope(k, cos, sin)
    q_ref[0] = q.astype(BF16)
    k_ref[0] = (k * (RET_DK ** -0.5)).astype(BF16)
    c0 = 2 * RET_QK
    v_ref[0] = y[:, c0:c0 + RET_WIDTH].astype(BF16)
    u0 = c0 + RET_WIDTH
    u_ref[0] = y[:, u0:u0 + S5_WIDTH]
    g = y[:, u0 + S5_WIDTH:]
    g_ref[0] = (g * _sigmoid(g)).astype(BF16)
    nch = u_s.shape[1] // S5_CHUNK
    for j in range(S5_LB):
        u_s[j] = y[:, u0 + j * LANE:u0 + (j + 1) * LANE]
        for s in range(S5_CHUNK):
            uz_ref[0, j, :, s * LANE:(s + 1) * LANE] = u_s[j, pl.ds(s, nch, stride=S5_CHUNK), :].astype(BF16)


def _inproj0(h, mod, layer, ng, w, cos, sin, mod_row, tm):
    b, n, _ = h.shape
    rope = cos is not None
    row_spec = lambda width: pl.BlockSpec((1, tm, width), lambda i, j: (i, j, 0))
    in_specs = [
        row_spec(D_MODEL),
        pl.BlockSpec((1, MOD_ROWS, N_MOD * D_MODEL), lambda i, j: (layer, 0, 0)),
        pl.BlockSpec((1, D_MODEL), lambda i, j: (0, 0)),
        pl.BlockSpec((D_MODEL, AB_IN), lambda i, j: (0, 0), pipeline_mode=pl.Buffered(1)),
    ]
    args = [h, mod, ng, w]
    if rope:
        in_specs += [pl.BlockSpec((tm, LANE), lambda i, j: (j, 0))] * 2
        args += [cos, sin]
    widths = (RET_QK, RET_QK, RET_WIDTH, S5_WIDTH, RET_WIDTH)
    dtypes = (BF16, BF16, BF16, F32, BF16)
    out_shape = [jax.ShapeDtypeStruct((b, n, wd), dt) for wd, dt in zip(widths, dtypes)]
    out_specs = [row_spec(wd) for wd in widths]
    out_shape.insert(4, jax.ShapeDtypeStruct((b, S5_LB, n // S5_CHUNK, S5_BIG), BF16))
    out_specs.insert(4, pl.BlockSpec((1, S5_LB, tm // S5_CHUNK, S5_BIG), lambda i, j: (i, 0, j, 0)))
    return pl.pallas_call(
        functools.partial(_inproj0_kernel, mod_row=mod_row, rope=rope),
        out_shape=out_shape,
        grid=(b, n // tm),
        in_specs=in_specs,
        out_specs=out_specs,
        scratch_shapes=[pltpu.VMEM((S5_LB, tm, LANE), F32)],
        compiler_params=_cp("parallel", "parallel"),
        name="inproj0_lat" if rope else "inproj0_ctx",
    )(*args)


def _ret_kernel(lg_ref, ql, kl, vl, gl, qc, kc, vc, gc, rl, rc, st_s, *, ncl, ncc):
    c = RET_CHUNK
    dk2 = 2 * RET_DK
    nt = ncc + ncl
    p = pl.program_id(1)
    h_a = 2 * p
    lgf_a = lg_ref[pl.ds(h_a, 1), :]
    lgf_b = lg_ref[pl.ds(h_a + 1, 1), :]
    lgb_a = lg_ref[pl.ds(RET_HEADS + h_a, 1), :]
    lgb_b = lg_ref[pl.ds(RET_HEADS + h_a + 1, 1), :]
    lane = lax.broadcasted_iota(jnp.int32, (1, 2 * RET_DK), 1)
    is_a = lane < RET_DK
    lgf_lane = jnp.where(is_a, lgf_a, lgf_b)
    lgb_lane = jnp.where(is_a, lgb_a, lgb_b)
    ri = lax.broadcasted_iota(jnp.int32, (c, c), 0).astype(F32)
    ci = lax.broadcasted_iota(jnp.int32, (c, c), 1).astype(F32)
    diff = ri - ci

    def dmat(lgf, lgb):
        fwd = jnp.exp(jnp.maximum(diff, 0.0) * lgf)
        bwd = jnp.exp(jnp.maximum(-diff, 0.0) * lgb)
        return jnp.where(diff > 0, fwd, jnp.where(diff < 0, bwd, 2.0))

    d_a = dmat(lgf_a, lgb_a)
    d_b = dmat(lgf_b, lgb_b)
    rowp = lax.broadcasted_iota(jnp.int32, (c, dk2), 0).astype(F32)
    qd = jnp.concatenate([jnp.exp((rowp + 1.0) * lgf_lane), jnp.exp((c - rowp) * lgb_lane)], axis=1)
    kd = jnp.concatenate([jnp.exp((c - 1.0 - rowp) * lgf_lane), jnp.exp(rowp * lgb_lane)], axis=1)
    rowk = lax.broadcasted_iota(jnp.int32, (dk2, 2 * RET_DV), 0)
    cd_f = jnp.exp(c * jnp.where(rowk < RET_DK, lgf_a[:, :1], lgf_b[:, :1]))
    cd_b = jnp.exp(c * jnp.where(rowk < RET_DK, lgb_a[:, :1], lgb_b[:, :1]))
    mask2 = jnp.concatenate([is_a, is_a], axis=1)

    def rows_of(n):
        if n < ncc:
            return (qc, kc, vc, gc, rc), slice(n * c, (n + 1) * c)
        return (ql, kl, vl, gl, rl), slice((n - ncc) * c, (n - ncc + 1) * c)

    def increment(n):
        (_, k_ref, v_ref, _, _), sl = rows_of(n)
        k = k_ref[0, sl, :]
        kk = jnp.concatenate([k, k], axis=1).astype(F32) * kd
        return _dot_tn(kk.astype(BF16), v_ref[0, sl, :])

    sf = jnp.zeros((dk2, 2 * RET_DV), F32)
    inc = increment(0)
    for n in range(nt):
        nxt = increment(n + 1) if n + 1 < nt else None
        st_s[n, 0:dk2, :] = sf
        st_s[n, dk2:2 * dk2, :] = inc[dk2:2 * dk2]
        sf = cd_f * sf + inc[0:dk2]
        inc = nxt

    heads = ((True, d_a, 0), (False, d_b, RET_DV))

    def scores(n):
        (q_ref, k_ref, _, _, _), sl = rows_of(n)
        q, k = q_ref[0, sl, :], k_ref[0, sl, :]
        q2 = (jnp.concatenate([q, q], axis=1).astype(F32) * qd).astype(BF16)
        res = []
        for keep_a, dm, _ in heads:
            m1 = is_a if keep_a else jnp.logical_not(is_a)
            m2 = mask2 if keep_a else jnp.logical_not(mask2)
            att = _dot_nt(jnp.where(m1, q, jnp.zeros_like(q)), k) * dm
            res.append((att.astype(BF16), jnp.where(m2, q2, jnp.zeros_like(q2))))
        return res

    def output(n, sc, sb):
        (_, _, v_ref, g_ref, out_ref), sl = rows_of(n)
        v, g = v_ref[0, sl, :], g_ref[0, sl, :]
        s_n = jnp.concatenate([st_s[n, 0:dk2, :].astype(BF16), sb.astype(BF16)], axis=0)
        for (_, _, cs), (att, q2m) in zip(heads, sc):
            o = _dot(att, v[:, cs:cs + RET_DV]) + _dot(q2m, s_n[:, cs:cs + RET_DV])
            o = _rms(o) * g[:, cs:cs + RET_DV].astype(F32)
            out_ref[0, sl, cs:cs + RET_DV] = o.astype(BF16)

    order = list(range(ncc - 1, -1, -1)) + list(range(nt - 1, ncc - 1, -1))
    sb = jnp.zeros((dk2, 2 * RET_DV), F32)
    sc = scores(order[0])
    for i, n in enumerate(order):
        nxt = scores(order[i + 1]) if i + 1 < nt else None
        output(n, sc, sb)
        sb = cd_b * sb + st_s[n, dk2:2 * dk2, :]
        sc = nxt


def _retention(lg_rows, q_l, k_l, v_l, g_l, q_c, k_c, v_c, g_c):
    b, n, _ = q_l.shape
    nc = q_c.shape[1]
    ncl, ncc = n // RET_CHUNK, nc // RET_CHUNK
    pairs = RET_HEADS // 2

    def spec(rows, width):
        return pl.BlockSpec((1, rows, width), lambda i, p: (i, 0, p))

    return pl.pallas_call(
        functools.partial(_ret_kernel, ncl=ncl, ncc=ncc),
        out_shape=[jax.ShapeDtypeStruct((b, n, RET_WIDTH), BF16),
                   jax.ShapeDtypeStruct((b, nc, RET_WIDTH), BF16)],
        grid=(b, pairs),
        in_specs=[pl.BlockSpec((2 * RET_HEADS, 2 * RET_DK), lambda i, p: (0, 0)),
                  spec(n, 2 * RET_DK), spec(n, 2 * RET_DK), spec(n, 2 * RET_DV), spec(n, 2 * RET_DV),
                  spec(nc, 2 * RET_DK), spec(nc, 2 * RET_DK), spec(nc, 2 * RET_DV), spec(nc, 2 * RET_DV)],
        out_specs=[spec(n, 2 * RET_DV), spec(nc, 2 * RET_DV)],
        scratch_shapes=[pltpu.VMEM((ncl + ncc, 4 * RET_DK, 2 * RET_DV), F32)],
        compiler_params=_cp("parallel", "parallel"),
        name="retention",
    )(lg_rows, q_l, k_l, v_l, g_l, q_c, k_c, v_c, g_c)


def _dot_hi(a, b, contract=(1, 0)):
    dims = (((contract[0],), (contract[1],)), ((), ()))
    return lax.dot_general(a, b, dims, preferred_element_type=F32, precision=lax.Precision.HIGHEST)


def _dot_sel(a, b, contract, data):
    dims = (((contract[0],), (contract[1],)), ((), ()))
    x = (a, b)[data]
    hi = x.astype(BF16)
    lo = (x - hi.astype(F32)).astype(BF16)
    dd = lambda piece: lax.dot_general(*((piece, b) if data == 0 else (a, piece)), dims, preferred_element_type=F32)
    return dd(hi) + dd(lo)


def _s5_prep_kernel(ar_row, ai_row, ldt, btr, bti, ctr, cti, rm, rwin, rwout, abig):
    t, g, p, kp = S5_CHUNK, S5_GROUP, S5_STATE, S5_POW
    row = S5_ROW
    i0 = lambda shape: lax.broadcasted_iota(jnp.int32, shape, 0)
    i1 = lambda shape: lax.broadcasted_iota(jnp.int32, shape, 1)
    f32 = lambda m: jnp.where(m, 1.0, 0.0).astype(BF16)
    s_of_r = i0((row, kp)) // g
    k_of_l = i1((row, kp))
    sel_rows = (f32(k_of_l == t - 1 - s_of_r), f32(k_of_l == s_of_r))
    t_of_c = i1((kp, row)) // g
    k_of_s = i0((kp, row))
    sel_out = (f32(k_of_s == t_of_c + 1), f32(k_of_s == t - t_of_c))
    sel_lag = (f32(k_of_s == t_of_c), f32(k_of_s == t - 1 - t_of_c))
    tile_l = f32(i1((g, row)) % g == i0((g, row)))
    tile_r = f32(i0((row, g)) % g == i1((row, g)))
    lane = i1((g, row))
    k_col = i0((kp, 1)).astype(F32)
    k_row = i1((1, kp)).astype(F32)
    first = i0((8, 1)) == 0

    def outer(a, k):
        a8 = jnp.where(first, jnp.broadcast_to(a, (8, a.shape[1])), 0.0)
        return _dot_hi(a8, jnp.broadcast_to(k, (8, k.shape[1])), (0, 0))

    rwin[...] = jnp.zeros(rwin.shape, BF16)
    rwout[...] = jnp.zeros(rwout.shape, BF16)
    for q in range(S5_GPB):
        pair, qq = divmod(q, 2)
        lags = []
        for d in range(2):
            dt = jnp.exp(ldt[d, q])
            are_r, aim_r = ar_row[d, q], ai_row[d, q]
            mag = jnp.exp(are_r * dt)
            ang = aim_r * dt
            nr, ni = mag * jnp.cos(ang) - 1.0, mag * jnp.sin(ang)
            den = jnp.square(are_r) + jnp.square(aim_r)
            fr = (nr * are_r + ni * aim_r) / den
            fi = (ni * are_r - nr * aim_r) / den
            pm = jnp.exp(k_col * (are_r * dt))
            pa = k_col * ang
            pk_re, pk_im = pm * jnp.cos(pa), pm * jnp.sin(pa)
            pmt = jnp.exp(outer(are_r * dt, k_row))
            pat = outer(ang, k_row)
            pt_re, pt_im = pmt * jnp.cos(pat), pmt * jnp.sin(pat)
            bt_re, bt_im = _dot_sel(tile_r, btr[d, q], (1, 1), 1), _dot_sel(tile_r, bti[d, q], (1, 1), 1)
            bb_re = fr * bt_re - fi * bt_im
            bb_im = fr * bt_im + fi * bt_re
            pr_re, pr_im = _dot_sel(sel_rows[d], pk_re, (1, 0), 1), _dot_sel(sel_rows[d], pk_im, (1, 0), 1)
            w_re = pr_re * bb_re - pr_im * bb_im
            w_im = pr_re * bb_im + pr_im * bb_re
            for part, w in ((d, w_re), (2 + d, w_im)):
                c0 = part * LANE + qq * p
                rwin[0, q * row:(q + 1) * row, c0:c0 + p] = w.astype(BF16)
            ct_re, ct_im = _dot_sel(ctr[d, q], tile_l, (0, 0), 0), _dot_sel(cti[d, q], tile_l, (0, 0), 0)

            def c_pow(sel):
                pc_re, pc_im = _dot_sel(pt_re, sel, (1, 0), 0), _dot_sel(pt_im, sel, (1, 0), 0)
                return ct_re * pc_re - ct_im * pc_im, ct_re * pc_im + ct_im * pc_re

            o_re, o_im = c_pow(sel_out[d])
            for part, o in ((d, o_re), (2 + d, -o_im)):
                r0 = pair * S5_PAIR + part * LANE + qq * p
                rwout[0, r0:r0 + p, qq * row:(qq + 1) * row] = o.astype(BF16)
            r0 = d * S5_GPB * p + q * p
            l_re, l_im = c_pow(sel_lag[d])
            lags.append(_dot_hi(bb_re[0:g], l_re) - _dot_hi(bb_im[0:g], l_im))
            abig[0, 0:1, r0:r0 + p] = pk_re[t:t + 1, :]
            abig[0, 1:2, r0:r0 + p] = pk_im[t:t + 1, :]
        for s in range(t):
            fwd = jnp.where(lane >= g * s, pltpu.roll(lags[0], g * s, 1), 0.0)
            bwd = jnp.where(lane < g * (s + 1), pltpu.roll(lags[1], (row - g * (t - 1 - s)) % row, 1), 0.0)
            rm[0, q * row + s * g:q * row + (s + 1) * g, :] = (fwd + bwd).astype(BF16)


def _s5_prep(a_re, a_im, log_dt, b_re, b_im, c_re, c_im):
    gg, p, g = S5_GROUPS, S5_STATE, S5_GROUP
    f = lambda x: x.astype(F32)
    args = (f(a_re).reshape(2, gg, 1, p), f(a_im).reshape(2, gg, 1, p),
            f(log_dt).reshape(2, gg, 1, 1),
            f(b_re), f(b_im), f(c_re), f(c_im))
    spec = lambda r, c: pl.BlockSpec((2, S5_GPB, r, c), lambda j: (0, j, 0, 0))
    out = lambda r, c: pl.BlockSpec((1, r, c), lambda j: (j, 0, 0))
    return pl.pallas_call(
        _s5_prep_kernel,
        out_shape=[jax.ShapeDtypeStruct((S5_LB, S5_BIG, S5_ROW), BF16),
                   jax.ShapeDtypeStruct((S5_LB, S5_BIG, S5_PAIR), BF16),
                   jax.ShapeDtypeStruct((S5_LB, S5_BIG, S5_PAIR), BF16),
                   jax.ShapeDtypeStruct((S5_LB, 2, S5_HALF), F32)],
        grid=(S5_LB,),
        in_specs=[spec(1, p), spec(1, p), spec(1, 1),
                  spec(p, g), spec(p, g), spec(g, p), spec(g, p)],
        out_specs=[out(S5_BIG, S5_ROW), out(S5_BIG, S5_PAIR), out(S5_BIG, S5_PAIR), out(2, S5_HALF)],
        compiler_params=_cp("parallel"),
        name="s5_prep",
    )(*args)


def _s5_block_transpose(v):
    n = len(v)
    blk = lax.broadcasted_iota(jnp.int32, (1, LANE), 1) // S5_GROUP

    def spread(src, e):
        w = 1
        while w < n:
            bit = (blk // w) % 2 == 1
            src = [jnp.where(bit, src[(m + e * w) % n], src[m]) for m in range(n)]
            w *= 2
        return src

    diag = spread(v, 1)
    rolled = [diag[0]] + [pltpu.roll(diag[(-k) % n], LANE - k * S5_GROUP, 1) for k in range(1, n)]
    return spread(rolled, -1)


def _s5_kernel(ul, uc, rm_ref, rwin_ref, rwout_ref, a_ref, yl, yc, x_s, *, ncl, ncc):
    hw = S5_HALF
    gpb, half = S5_GPB, S5_CHUNK // 2
    cols = lambda x, i: x[:, i * LANE:(i + 1) * LANE]

    nb = ul.shape[0]
    nt = ncc + ncl
    pairs = gpb // 2
    slab = lambda k, r: k * pairs + r
    of_row = lambda i: pl.ds(i, nt, stride=nb)

    zg = []
    for i in range(nb):
        z = jnp.concatenate([uc[i, 0], ul[i, 0]], axis=0)
        zt = [_s5_block_transpose([cols(z, half * h + a) for a in range(half)]) for h in range(2)]
        zg.append([jnp.concatenate([zt[0][q], zt[1][q]], axis=1) for q in range(gpb)])
        for r in range(pairs):
            xw = _dot(jnp.concatenate([zg[i][2 * r], zg[i][2 * r + 1]], axis=1),
                      rwin_ref[0, r * S5_PAIR:(r + 1) * S5_PAIR, :])
            for k in range(4):
                x_s[slab(k, r), of_row(i), :] = cols(xw, k)
    a_re = a_ref[0, 0:1, :]
    a_im = a_ref[0, 1:2, :]

    def segment(base, n, carry):
        s_re, s_im = carry
        for i in range(n):
            rf = slice((base + i) * nb, (base + i + 1) * nb)
            rb = slice((base + n - 1 - i) * nb, (base + n - i) * nb)
            rows_of = (rf,) * pairs + (rb,) * pairs
            x_re = jnp.concatenate([x_s[c, rows_of[c], :] for c in range(2 * pairs)], axis=-1)
            x_im = jnp.concatenate([x_s[2 * pairs + c, rows_of[c], :] for c in range(2 * pairs)], axis=-1)
            for c in range(2 * pairs):
                x_s[c, rows_of[c], :] = cols(s_re, c)
                x_s[2 * pairs + c, rows_of[c], :] = cols(s_im, c)
            s_re, s_im = a_re * s_re - a_im * s_im + x_re, a_re * s_im + a_im * s_re + x_im
        return s_re, s_im

    zero = jnp.zeros((nb, hw), F32)
    carry = segment(0, ncc, (zero, zero))
    segment(ncc, ncl, carry)

    for i in range(nb):
        yt = [[None] * gpb, [None] * gpb]
        for r in range(pairs):
            xs = jnp.concatenate([x_s[slab(k, r), of_row(i), :] for k in range(4)], axis=1)
            yp = _dot(xs.astype(BF16), rwout_ref[0, r * S5_PAIR:(r + 1) * S5_PAIR, :])
            for qq in range(2):
                q = 2 * r + qq
                yq = (yp[:, qq * S5_ROW:(qq + 1) * S5_ROW]
                      + _dot(zg[i][q], rm_ref[0, q * S5_ROW:(q + 1) * S5_ROW, :]))
                yt[0][q], yt[1][q] = cols(yq, 0), cols(yq, 1)
        for h in range(2):
            for a, y in enumerate(_s5_block_transpose(yt[h])):
                c = slice((half * h + a) * LANE, (half * h + a + 1) * LANE)
                yc[i, 0, :, c] = y[0:ncc]
                yl[i, 0, :, c] = y[ncc:ncc + ncl]


def _s5(prep, zl, zc):
    rm, rwin, rwout, a_big = prep
    b, _, ncl, _ = zl.shape
    ncc = zc.shape[2]
    nb = S5_BATCH_ROWS if b % S5_BATCH_ROWS == 0 else 1
    rows = lambda r: pl.BlockSpec((nb, 1, r, S5_BIG), lambda j, i: (i, j, 0, 0))
    wspec = lambda r, c: pl.BlockSpec((1, r, c), lambda j, i: (j, 0, 0))
    return pl.pallas_call(
        functools.partial(_s5_kernel, ncl=ncl, ncc=ncc),
        out_shape=[jax.ShapeDtypeStruct(zl.shape, F32), jax.ShapeDtypeStruct(zc.shape, F32)],
        grid=(S5_LB, b // nb),
        in_specs=[rows(ncl), rows(ncc), wspec(S5_BIG, S5_ROW), wspec(S5_BIG, S5_PAIR),
                  wspec(S5_BIG, S5_PAIR), wspec(2, S5_HALF)],
        out_specs=[rows(ncl), rows(ncc)],
        scratch_shapes=[pltpu.VMEM((2 * S5_HALF // LANE, (ncc + ncl) * nb, LANE), F32)],
        compiler_params=_cp("parallel", "parallel"),
        name="s5",
    )(zl, zc, rm, rwin, rwout, a_big)


def _mix_mlp_kernel(*refs, mod_row, s5_merge, final_norm, fb):
    if s5_merge:
        (h_ref, r_ref, y5_ref, u_ref, ds_ref, wg_ref, bg_ref, wo_ref, mod_ref, nm_ref, w1_ref, w2_ref,
         *rest) = refs
    else:
        h_ref, r_ref, wo_ref, mod_ref, nm_ref, w1_ref, w2_ref, *rest = refs
    if final_norm:
        nf_ref, o_ref, *scratch = rest
    else:
        o_ref, *scratch = rest
    row = pl.program_id(0) if mod_row is None else mod_row
    if s5_merge:
        (y_s,) = scratch
        nch = y_s.shape[1] // S5_CHUNK
        for lb in range(S5_LB):
            for s in range(S5_CHUNK):
                y_s[lb, pl.ds(s, nch, stride=S5_CHUNK), :] = y5_ref[0, lb, :, s * LANE:(s + 1) * LANE]
        y5 = jnp.concatenate([y_s[lb] for lb in range(S5_LB)], axis=-1)
        y = jax.nn.gelu(y5 + ds_ref[...] * u_ref[0])
        y = y * jax.nn.sigmoid(_dot(y.astype(BF16), wg_ref[...]) + bg_ref[...])
        mix = _dot(r_ref[0], wo_ref[0:RET_WIDTH, :]) + _dot(y.astype(BF16), wo_ref[RET_WIDTH:D_MODEL, :])
    else:
        mix = _dot(r_ref[0], wo_ref[...])
    h1 = h_ref[0] + _mod_chunk(mod_ref, row, 2) * mix
    xn = _rms(h1) * nm_ref[...]
    xm = (xn * (1.0 + _mod_chunk(mod_ref, row, 4)) + _mod_chunk(mod_ref, row, 3)).astype(BF16)
    acc = None
    for j in range(D_FF // fb):
        a = jnp.square(jnp.maximum(_dot(xm, w1_ref[:, j * fb:(j + 1) * fb].astype(BF16)), 0.0)).astype(BF16)
        part = _dot(a, w2_ref[j * fb:(j + 1) * fb, :].astype(BF16))
        acc = part if acc is None else acc + part
    h2 = h1 + _mod_chunk(mod_ref, row, 5) * acc
    if final_norm:
        h2 = _rms(h2) * nf_ref[...]
    o_ref[0] = h2


def _mix_mlp(h, r, s5y, u, s5p, wo, mod, layer, nm, w1, w2, nf, mod_row, tm, fb, name):
    b, n, _ = h.shape
    s5_merge = s5y is not None
    final_norm = nf is not None
    one = pl.Buffered(1)
    row_spec = lambda width: pl.BlockSpec((1, tm, width), lambda i, t: (i, t, 0))
    const = lambda shape: pl.BlockSpec(shape, lambda i, t: (0,) * len(shape), pipeline_mode=one)
    in_specs = [row_spec(D_MODEL), row_spec(r.shape[-1])]
    args = [h, r]
    if s5_merge:
        d_skip, w_glu, b_glu = s5p
        z_spec = pl.BlockSpec((1, S5_LB, tm // S5_CHUNK, S5_BIG), lambda i, t: (i, 0, t, 0))
        in_specs += [z_spec, row_spec(S5_WIDTH), const((1, S5_WIDTH)),
                     const((S5_WIDTH, S5_WIDTH)), const((1, S5_WIDTH))]
        args += [s5y, u, d_skip, w_glu, b_glu]
    in_specs += [const((D_MODEL, D_MODEL)),
                 pl.BlockSpec((1, MOD_ROWS, N_MOD * D_MODEL), lambda i, t: (layer, 0, 0), pipeline_mode=one),
                 const((1, D_MODEL))]
    if w1.ndim == 3:
        in_specs += [pl.BlockSpec((None, D_MODEL, D_FF), lambda i, t: (layer, 0, 0), pipeline_mode=one),
                     pl.BlockSpec((None, D_FF, D_MODEL), lambda i, t: (layer, 0, 0), pipeline_mode=one)]
    else:
        in_specs += [const((D_MODEL, D_FF)), const((D_FF, D_MODEL))]
    args += [wo, mod, nm, w1, w2]
    if final_norm:
        in_specs.append(const((1, D_MODEL)))
        args.append(nf)
    return pl.pallas_call(
        functools.partial(_mix_mlp_kernel, mod_row=mod_row, s5_merge=s5_merge, final_norm=final_norm, fb=fb),
        out_shape=jax.ShapeDtypeStruct((b, n, D_MODEL), F32),
        grid=(b, n // tm),
        in_specs=in_specs,
        out_specs=row_spec(D_MODEL),
        scratch_shapes=[pltpu.VMEM((S5_LB, tm, LANE), F32)] if s5_merge else [],
        compiler_params=_cp("parallel", "parallel"),
        name=name,
    )(*args)


def _hgrn_lower_bounds(lbl_ref, layer):
    out = []
    for d in range(2):
        z = [lbl_ref[d, k:k + 1, :] for k in range(DEPTH)]
        zmax = functools.reduce(jnp.maximum, z)
        e = [jnp.exp(v - zmax) for v in z]
        tot = functools.reduce(lambda a, b_: a + b_, e)
        lb = jnp.zeros_like(tot)
        for k in range(1, layer + 1):
            lb = lb + e[k] / tot
        out.append(lb)
    return out


def _inproj1_kernel(h_ref, mod_ref, ng_ref, lbl_ref, w_ref, *out_refs, mod_row, layer, latent):
    row = pl.program_id(0) if mod_row is None else mod_row
    xn = _rms(h_ref[0]) * ng_ref[...]
    xm = (xn * (1.0 + _mod_chunk(mod_ref, row, 1)) + _mod_chunk(mod_ref, row, 0)).astype(BF16)
    lbs = _hgrn_lower_bounds(lbl_ref, layer)
    names = ((["q"] if latent else []) + ["lf0h", "lf0l", "kk0", "lf1h", "lf1l", "kk1", "i"]
             + (["sg"] if latent else []))
    out = dict(zip(names, out_refs))
    half = h_ref.shape[1] // 2

    def finish(kind, y, rows):
        if kind in ("lf0", "lf1"):
            d = int(kind[2])
            t = (1.0 - lbs[d]) * _sigmoid(y)
            lf = jnp.log(lbs[d] + t)
            hi = lf.astype(BF16)
            out[kind + "h"][0, rows, :] = hi
            out[kind + "l"][0, rows, :] = (lf - hi.astype(F32)).astype(BF16)
            out["kk%d" % d][0, rows, :] = ((1.0 - lbs[d]) - t).astype(BF16)
        elif kind == "sg":
            out[kind][0, rows, :] = (y * _sigmoid(y)).astype(BF16)
        else:
            out[kind][0, rows, :] = y.astype(BF16)

    cols = [("lf0", 1), ("lf1", 2)] + ([("sg", 4), ("q", 0)] if latent else []) + [("i", 3)]
    pending = None
    for kind, k in cols:
        wk = w_ref[:, k * D_MODEL:(k + 1) * D_MODEL].astype(BF16)
        for rows in (slice(0, half), slice(half, 2 * half)):
            y = _dot(xm[rows], wk)
            if pending is not None:
                finish(*pending)
            pending = (kind, y, rows)
    finish(*pending)


def _inproj1(h, mod, layer, ng, lb_logits, w, latent, mod_row, tm, name):
    b, n, _ = h.shape
    one = pl.Buffered(1)
    row_spec = pl.BlockSpec((1, tm, D_MODEL), lambda i, t: (i, t, 0))
    n_out = 9 if latent else 7
    return pl.pallas_call(
        functools.partial(_inproj1_kernel, mod_row=mod_row, layer=layer, latent=latent),
        out_shape=[jax.ShapeDtypeStruct((b, n, D_MODEL), BF16)] * n_out,
        grid=(b, n // tm),
        in_specs=[row_spec,
                  pl.BlockSpec((1, MOD_ROWS, N_MOD * D_MODEL), lambda i, t: (layer, 0, 0), pipeline_mode=one),
                  pl.BlockSpec((1, D_MODEL), lambda i, t: (0, 0), pipeline_mode=one),
                  pl.BlockSpec(lb_logits.shape, lambda i, t: (0, 0, 0), pipeline_mode=one),
                  pl.BlockSpec(w.shape, lambda i, t: (0, 0), pipeline_mode=one)],
        out_specs=[row_spec] * n_out,
        compiler_params=_cp("parallel", "parallel"),
        name=name,
    )(h, mod, ng, lb_logits, w)


def _hgrn_kernel(ng_ref, ql, fhl, fll, kfl, bhl, bll, kbl, il, sgl, fhc, flc, kfc, bhc, blc, kbc, ic, o_ref,
                 qin_s, att_s, kv_s, et_s, kvc_s, etc_s, cum_s, ko_s, qt_s, kt_s, *, nbl, nbc):
    cb = HG_BLOCK
    mid = cb // 2
    gb = HG_GROUP
    gr = gb * cb
    dk = HG_DK

    ri = lax.broadcasted_iota(jnp.int32, (gr, gr), 0)
    ci = lax.broadcasted_iota(jnp.int32, (gr, gr), 1)
    same = (ri // cb) == (ci // cb)
    rb = lax.broadcasted_iota(jnp.int32, (cb, cb), 0)
    cbi = lax.broadcasted_iota(jnp.int32, (cb, cb), 1)
    tri_l = jnp.where(same & (ri >= ci), 1.0, 0.0).astype(BF16)
    tri_u = jnp.where(same & (ri <= ci), 1.0, 0.0).astype(BF16)
    dirs = ((0, rb >= cbi, mid - 1, cb - 1), (1, rb <= cbi, mid, 0))

    def cumsums(fh, fl, bh, bl, slot):
        pre = _dot(tri_l, jnp.concatenate([fh, fl], axis=-1))
        suf = _dot(tri_u, jnp.concatenate([bh, bl], axis=-1))
        cum_s[slot, 0] = pre[:, :dk] + pre[:, dk:]
        cum_s[slot, 1] = suf[:, :dk] + suf[:, dk:]

    def operands(slot, kks, q, n0, et_ref, which=(0, 1)):
        for d, keep, ref_row, tot_row in [dirs[i] for i in which]:
            cum = cum_s[slot, d].reshape(gb, cb, dk)
            kk = kks[d].astype(F32).reshape(gb, cb, dk)
            ref = cum[:, ref_row:ref_row + 1, :]
            tot = cum[:, tot_row:tot_row + 1, :]
            e = cum - ref
            kt = kk * jnp.exp(-e)
            ko_s[slot, d] = (kt * jnp.exp(tot - ref)).astype(BF16).reshape(gr, dk)
            e_tot = jnp.exp(tot)
            for j in range(gb):
                et_ref[n0 + j, :, d * dk:(d + 1) * dk] = e_tot[j]
            if q is not None:
                qt = q.astype(F32).reshape(gb, cb, dk) * jnp.exp(e)
                qin_s[n0 * cb:n0 * cb + gr, d * dk:(d + 1) * dk] = (
                    (qt * jnp.exp(ref)).astype(BF16).reshape(gr, dk))
                qt_s[slot, d] = qt.astype(BF16).reshape(gr, dk)
                kt_s[slot, d] = kt.astype(BF16).reshape(gr, dk)

    def matmuls(slot, v, with_q, n0, kv_ref, blocks=tuple(range(HG_GROUP))):
        for d in range(2):
            for j in blocks:
                rows = slice(j * cb, (j + 1) * cb)
                kv_ref[n0 + j, :, d * dk:(d + 1) * dk] = _dot_tn(v[rows], ko_s[slot, d, rows, :])
        if with_q:
            for j in blocks:
                rows = slice(j * cb, (j + 1) * cb)
                att = (jnp.where(dirs[0][1], _dot_nt(qt_s[slot, 0, rows, :], kt_s[slot, 0, rows, :]), 0.0)
                       + jnp.where(dirs[1][1], _dot_nt(qt_s[slot, 1, rows, :], kt_s[slot, 1, rows, :]), 0.0))
                att_s[n0 + j] = att.astype(BF16)

    for g in range(nbc // gb):
        sl = slice(g * gr, (g + 1) * gr)
        cumsums(fhc[0, sl, :], flc[0, sl, :], bhc[0, sl, :], blc[0, sl, :], 0)
        operands(0, (kfc[0, sl, :], kbc[0, sl, :]), None, g * gb, etc_s)
        matmuls(0, ic[0, sl, :], False, g * gb, kvc_s)

    ngl = nbl // gb
    rows_of = lambda g: slice(g * gr, (g + 1) * gr)

    def lat_cumsums(g, slot):
        r = rows_of(g)
        cumsums(fhl[0, r, :], fll[0, r, :], bhl[0, r, :], bll[0, r, :], slot)

    def lat_operands(g, slot, which=(0, 1)):
        operands(slot, (kfl[0, rows_of(g), :], kbl[0, rows_of(g), :]), ql[0, rows_of(g), :], g * gb, et_s, which)

    def lat_matmuls(g, slot, blocks=tuple(range(HG_GROUP))):
        matmuls(slot, il[0, rows_of(g), :], True, g * gb, kv_s, blocks)

    fwd, bwd = slice(0, dk), slice(dk, 2 * dk)
    st_f = jnp.zeros((dk, dk), F32)
    st_b = jnp.zeros((dk, dk), F32)
    for n in range(nbc):
        st_f = st_f * etc_s[n, :, fwd] + kvc_s[n, :, fwd]
    for n in reversed(range(nbc)):
        st_b = st_b * etc_s[n, :, bwd] + kvc_s[n, :, bwd]

    def fwd_steps(st, blocks):
        for n in blocks:
            inc = kv_s[n, :, fwd]
            kv_s[n, :, fwd] = st
            st = st * et_s[n, :, fwd] + inc
        return st

    quarter = [tuple(range(i * gb // 4, (i + 1) * gb // 4)) for i in range(4)]
    lat_cumsums(0, 0)
    lat_operands(0, 0)
    lat_cumsums(1, 1)
    for a in range(ngl):
        sa, sb = a % 2, 1 - a % 2
        prev = [(a - 1) * gb + j for j in range(gb)] if a > 0 else []
        lat_matmuls(a, sa, quarter[0])
        if a + 1 < ngl:
            lat_operands(a + 1, sb, (0,))
        lat_matmuls(a, sa, quarter[1])
        st_f = fwd_steps(st_f, prev[:gb // 2])
        if a + 2 < ngl:
            lat_cumsums(a + 2, sa)
        lat_matmuls(a, sa, quarter[2])
        if a + 1 < ngl:
            lat_operands(a + 1, sb, (1,))
        lat_matmuls(a, sa, quarter[3])
        st_f = fwd_steps(st_f, prev[gb // 2:])
    fwd_steps(st_f, [(ngl - 1) * gb + j for j in range(gb)])

    for n in reversed(range(nbl)):
        sl = slice(n * cb, (n + 1) * cb)
        kv = jnp.concatenate([kv_s[n, :, fwd].astype(BF16), st_b.astype(BF16)], axis=1)
        st_b = st_b * et_s[n, :, bwd] + kv_s[n, :, bwd]
        o = _dot(att_s[n], il[0, sl, :]) + _dot_nt(qin_s[sl, :], kv)
        o = _rms(o) * ng_ref[...] * sgl[0, sl, :].astype(F32)
        o_ref[0, sl, :] = o.astype(BF16)


def _hgrn(norm_g, q_l, fh_l, fl_l, kf_l, bh_l, bl_l, kb_l, i_l, sg_l, fh_c, fl_c, kf_c, bh_c, bl_c, kb_c, i_c):
    b, n, _ = q_l.shape
    nc = i_c.shape[1]
    nbl, nbc = n // HG_BLOCK, nc // HG_BLOCK
    assert nbl % HG_GROUP == 0 and nbc % HG_GROUP == 0 and nbl >= 2 * HG_GROUP and HG_GROUP % 4 == 0
    spec = lambda rows: pl.BlockSpec((1, rows, HG_DK), lambda i, h: (i, 0, h))
    slot = lambda dt: pltpu.VMEM((2, 2, HG_GROUP * HG_BLOCK, HG_DK), dt)
    return pl.pallas_call(
        functools.partial(_hgrn_kernel, nbl=nbl, nbc=nbc),
        out_shape=jax.ShapeDtypeStruct((b, n, D_MODEL), BF16),
        grid=(b, HG_HEADS),
        in_specs=[pl.BlockSpec((1, HG_DK), lambda i, h: (0, 0))] + [spec(n)] * 9 + [spec(nc)] * 7,
        out_specs=spec(n),
        scratch_shapes=[pltpu.VMEM((n, 2 * HG_DK), BF16),
                        pltpu.VMEM((nbl, HG_BLOCK, HG_BLOCK), BF16),
                        pltpu.VMEM((nbl, HG_DK, 2 * HG_DK), F32),
                        pltpu.VMEM((nbl, 1, 2 * HG_DK), F32),
                        pltpu.VMEM((nbc, HG_DK, 2 * HG_DK), F32),
                        pltpu.VMEM((nbc, 1, 2 * HG_DK), F32),
                        slot(F32), slot(BF16), slot(BF16), slot(BF16)],
        compiler_params=_cp("parallel", "parallel"),
        name="hgrn2",
    )(norm_g, q_l, fh_l, fl_l, kf_l, bh_l, bl_l, kb_l, i_l, sg_l, fh_c, fl_c, kf_c, bh_c, bl_c, kb_c, i_c)


def _rope_tables(n_tok):
    tok = jnp.arange(n_tok, dtype=jnp.int32)[:, None]
    row = (tok // GRID_W).astype(F32)
    col = (tok % GRID_W).astype(F32)
    n_freq = RET_DK // 4
    lane = jnp.arange(LANE, dtype=jnp.int32)[None, :]
    j = lane % (2 * n_freq)
    inv = ROPE_BASE ** (-(j % n_freq).astype(F32) / n_freq)
    ang = jnp.where(j < n_freq, row, col) * inv
    sign = jnp.where(lane % RET_DK < RET_DK // 2, -1.0, 1.0)
    return jnp.cos(ang), jnp.sin(ang) * sign


def kernel(x, c, ctx, c_ctx, w_mod, b_mod, norm_mix, norm_mlp, w_mlp_in, w_mlp_out, ab_w_in, ab_w_out, ret_logit, s5_a_re, s5_a_im, s5_log_dt, s5_b_re, s5_b_im, s5_c_re, s5_c_im, s5_d, s5_w_glu, s5_b_glu, hg_w_in, hg_w_out, hg_lb_logits, hg_norm, norm_final):
    b, n, d = x.shape
    nc = ctx.shape[1]
    assert d == D_MODEL and b + 1 <= MOD_ROWS and w_mod.shape[0] == DEPTH == 2
    assert n % 512 == 0 and nc % 256 == 0 and n % GRID_W == 0
    ctx_row = b
    tm_l, tm_c = 512, 256
    tm_in0 = 1024 if n % 1024 == 0 else tm_l

    cc = jnp.zeros((MOD_ROWS, d), F32).at[:b].set(c).at[b].set(c_ctx)
    mod = _adaln(cc, w_mod, b_mod)

    row2 = lambda a: a.reshape(1, -1)
    w_in0 = ab_w_in[0]
    cos, sin = _rope_tables(n)
    ng0 = row2(norm_mix[0])
    q_l, k_l, v_l, u_l, uz_l, g_l = _inproj0(x, mod, 0, ng0, w_in0, cos, sin, None, tm_in0)
    q_c, k_c, v_c, u_c, uz_c, g_c = _inproj0(ctx, mod, 0, ng0, w_in0, None, None, ctx_row, tm_c)

    log_gamma = jax.nn.log_sigmoid(ret_logit[0].astype(F32))
    lg_rows = jnp.broadcast_to(log_gamma.reshape(2 * RET_HEADS, 1), (2 * RET_HEADS, 2 * RET_DK))
    r_l, r_c = _retention(lg_rows, q_l, k_l, v_l, g_l, q_c, k_c, v_c, g_c)

    s5_ops = _s5_prep(s5_a_re[0], s5_a_im[0], s5_log_dt[0], s5_b_re[0], s5_b_im[0], s5_c_re[0], s5_c_im[0])
    y5_l, y5_c = _s5(s5_ops, uz_l, uz_c)

    s5p = (row2(s5_d[0]), s5_w_glu[0].astype(BF16), row2(s5_b_glu[0]))
    wo0 = ab_w_out[0].astype(BF16)
    w1_0, w2_0 = w_mlp_in, w_mlp_out
    nm0 = row2(norm_mlp[0])
    h_l = _mix_mlp(x, r_l, y5_l, u_l, s5p, wo0, mod, 0, nm0, w1_0, w2_0, None, None, tm_l, 1024, "mix_mlp0_lat")
    h_c = _mix_mlp(ctx, r_c, y5_c, u_c, s5p, wo0, mod, 0, nm0, w1_0, w2_0, None, ctx_row, tm_c, 1024, "mix_mlp0_ctx")

    w_in1 = hg_w_in[0]
    ng1 = row2(norm_mix[1])
    lat1 = _inproj1(h_l, mod, 1, ng1, hg_lb_logits, w_in1, True, None, tm_l, "inproj1_lat")
    ctx1 = _inproj1(h_c, mod, 1, ng1, hg_lb_logits, w_in1, False, ctx_row, tm_c, "inproj1_ctx")
    o1 = _hgrn(row2(hg_norm[0]), *lat1, *ctx1)
    return _mix_mlp(h_l, o1, None, None, None, hg_w_out[0].astype(BF16), mod, 1, row2(norm_mlp[1]),
                    w_mlp_in, w_mlp_out, row2(norm_final), None, tm_l, 1024,
                    "mix_mlp1_lat")
```

```python
import functools

import jax
import jax.numpy as jnp
from jax import lax
from jax.experimental import pallas as pl
from jax.experimental.pallas import tpu as pltpu

F32 = jnp.float32
BF16 = jnp.bfloat16

D_MODEL = 1024
DEPTH = 2
GRID_W = 64
EPS = 1e-6
ROPE_BASE = 10000.0
N_MOD = 6
RET_HEADS = 4
RET_DK = 64
RET_DV = 128
RET_QK = RET_HEADS * RET_DK
RET_WIDTH = RET_HEADS * RET_DV
RET_CHUNK = 128
S5_WIDTH = D_MODEL - RET_WIDTH
S5_GROUP = 16
S5_GROUPS = S5_WIDTH // S5_GROUP
S5_STATE = 64
S5_CHUNK = 16
S5_ROW = S5_CHUNK * S5_GROUP
S5_PAIR = 2 * S5_ROW
S5_BATCH_ROWS = 2
LANE = 128
S5_LB = S5_WIDTH // LANE
S5_GPB = LANE // S5_GROUP
S5_BIG = S5_CHUNK * LANE
S5_HALF = S5_GPB * 2 * S5_STATE
S5_POW = 32
AB_IN = 2 * RET_QK + 2 * RET_WIDTH + S5_WIDTH
HG_HEADS = 8
HG_DK = D_MODEL // HG_HEADS
HG_BLOCK = 64
HG_GROUP = 4
D_FF = 4 * D_MODEL
MOD_ROWS = 16

VMEM_LIMIT_BYTES = 56 * 1024 * 1024


def _cp(*sem):
    return pltpu.CompilerParams(dimension_semantics=sem, vmem_limit_bytes=VMEM_LIMIT_BYTES)


def _dot(a, b):
    return jnp.dot(a, b, preferred_element_type=F32)


def _dot_nt(a, b):
    return lax.dot_general(a, b, (((1,), (1,)), ((), ())), preferred_element_type=F32)


def _dot_tn(a, b):
    return lax.dot_general(a, b, (((0,), (0,)), ((), ())), preferred_element_type=F32)


def _sigmoid(x):
    return 0.5 * jnp.tanh(0.5 * x) + 0.5


def _rms(x):
    return x * lax.rsqrt(jnp.mean(x * x, axis=-1, keepdims=True) + EPS)


def _mod_chunk(mod_ref, row, i):
    return mod_ref[0, pl.ds(row, 1), i * D_MODEL:(i + 1) * D_MODEL]


def _adaln_kernel(cc_ref, w_ref, b_ref, o_ref):
    s = jax.nn.silu(cc_ref[...]).astype(BF16)
    o_ref[0] = _dot(s, w_ref[0].astype(BF16)) + b_ref[0]


def _adaln(cc, w_mod, b_mod):
    bn = 1536
    n = N_MOD * D_MODEL
    return pl.pallas_call(
        _adaln_kernel,
        out_shape=jax.ShapeDtypeStruct((DEPTH, MOD_ROWS, n), F32),
        grid=(DEPTH, n // bn),
        in_specs=[
            pl.BlockSpec((MOD_ROWS, D_MODEL), lambda l, j: (0, 0)),
            pl.BlockSpec((1, D_MODEL, bn), lambda l, j: (l, 0, j)),
            pl.BlockSpec((1, 1, bn), lambda l, j: (l, 0, j)),
        ],
        out_specs=pl.BlockSpec((1, MOD_ROWS, bn), lambda l, j: (l, 0, j)),
        compiler_params=_cp("parallel", "parallel"),
        name="adaln",
    )(cc, w_mod, b_mod.reshape(DEPTH, 1, n))


def _rope(t, cos, sin):
    lane = lax.broadcasted_iota(jnp.int32, t.shape, 1)
    first = (lane & (RET_DK // 2)) == 0
    w = t.shape[1]
    swapped = jnp.where(first, pltpu.roll(t, w - RET_DK // 2, 1), pltpu.roll(t, RET_DK // 2, 1))
    return t * cos + swapped * sin


def _inproj0_kernel(*refs, mod_row, rope):
    if rope:
        h_ref, mod_ref, ng_ref, w_ref, cos_ref, sin_ref, q_ref, k_ref, v_ref, u_ref, uz_ref, g_ref, u_s = refs
    else:
        h_ref, mod_ref, ng_ref, w_ref, q_ref, k_ref, v_ref, u_ref, uz_ref, g_ref, u_s = refs
    row = pl.program_id(0) if mod_row is None else mod_row
    xn = _rms(h_ref[0]) * ng_ref[...]
    xm = (xn * (1.0 + _mod_chunk(mod_ref, row, 1)) + _mod_chunk(mod_ref, row, 0)).astype(BF16)
    y = _dot(xm, w_ref[...].astype(BF16))
    q = y[:, 0:RET_QK]
    k = y[:, RET_QK:2 * RET_QK]
    if rope:
        cos = jnp.concatenate([cos_ref[...]] * (RET_QK // LANE), axis=1)
        sin = jnp.concatenate([sin_ref[...]] * (RET_QK // LANE), axis=1)
        q = _rope(q, cos, sin)
        k = _rope(k, cos, sin)
    q_ref[0] = q.astype(BF16)
    k_ref[0] = (k * (RET_DK ** -0.5)).astype(BF16)
    c0 = 2 * RET_QK
    v_ref[0] = y[:, c0:c0 + RET_WIDTH].astype(BF16)
    u0 = c0 + RET_WIDTH
    u_ref[0] = y[:, u0:u0 + S5_WIDTH]
    g = y[:, u0 + S5_WIDTH:]
    g_ref[0] = (g * _sigmoid(g)).astype(BF16)
    nch = u_s.shape[1] // S5_CHUNK
    for j in range(S5_LB):
        u_s[j] = y[:, u0 + j * LANE:u0 + (j + 1) * LANE]
        for s in range(S5_CHUNK):
            uz_ref[0, j, :, s * LANE:(s + 1) * LANE] = u_s[j, pl.ds(s, nch, stride=S5_CHUNK), :].astype(BF16)


def _inproj0(h, mod, layer, ng, w, cos, sin, mod_row, tm):
    b, n, _ = h.shape
    rope = cos is not None
    row_spec = lambda width: pl.BlockSpec((1, tm, width), lambda i, j: (i, j, 0))
    in_specs = [
        row_spec(D_MODEL),
        pl.BlockSpec((1, MOD_ROWS, N_MOD * D_MODEL), lambda i, j: (layer, 0, 0)),
        pl.BlockSpec((1, D_MODEL), lambda i, j: (0, 0)),
        pl.BlockSpec((D_MODEL, AB_IN), lambda i, j: (0, 0), pipeline_mode=pl.Buffered(1)),
    ]
    args = [h, mod, ng, w]
    if rope:
        in_specs += [pl.BlockSpec((tm, LANE), lambda i, j: (j, 0))] * 2
        args += [cos, sin]
    widths = (RET_QK, RET_QK, RET_WIDTH, S5_WIDTH, RET_WIDTH)
    dtypes = (BF16, BF16, BF16, F32, BF16)
    out_shape = [jax.ShapeDtypeStruct((b, n, wd), dt) for wd, dt in zip(widths, dtypes)]
    out_specs = [row_spec(wd) for wd in widths]
    out_shape.insert(4, jax.ShapeDtypeStruct((b, S5_LB, n // S5_CHUNK, S5_BIG), BF16))
    out_specs.insert(4, pl.BlockSpec((1, S5_LB, tm // S5_CHUNK, S5_BIG), lambda i, j: (i, 0, j, 0)))
    return pl.pallas_call(
        functools.partial(_inproj0_kernel, mod_row=mod_row, rope=rope),
        out_shape=out_shape,
        grid=(b, n // tm),
        in_specs=in_specs,
        out_specs=out_specs,
        scratch_shapes=[pltpu.VMEM((S5_LB, tm, LANE), F32)],
        compiler_params=_cp("parallel", "parallel"),
        name="inproj0_lat" if rope else "inproj0_ctx",
    )(*args)


def _ret_kernel(lg_ref, ql, kl, vl, gl, qc, kc, vc, gc, rl, rc, st_s, *, ncl, ncc):
    c = RET_CHUNK
    dk2 = 2 * RET_DK
    nt = ncc + ncl
    p = pl.program_id(1)
    h_a = 2 * p
    lgf_a = lg_ref[pl.ds(h_a, 1), :]
    lgf_b = lg_ref[pl.ds(h_a + 1, 1), :]
    lgb_a = lg_ref[pl.ds(RET_HEADS + h_a, 1), :]
    lgb_b = lg_ref[pl.ds(RET_HEADS + h_a + 1, 1), :]
    lane = lax.broadcasted_iota(jnp.int32, (1, 2 * RET_DK), 1)
    is_a = lane < RET_DK
    lgf_lane = jnp.where(is_a, lgf_a, lgf_b)
    lgb_lane = jnp.where(is_a, lgb_a, lgb_b)
    ri = lax.broadcasted_iota(jnp.int32, (c, c), 0).astype(F32)
    ci = lax.broadcasted_iota(jnp.int32, (c, c), 1).astype(F32)
    diff = ri - ci

    def dmat(lgf, lgb):
        fwd = jnp.exp(jnp.maximum(diff, 0.0) * lgf)
        bwd = jnp.exp(jnp.maximum(-diff, 0.0) * lgb)
        return jnp.where(diff > 0, fwd, jnp.where(diff < 0, bwd, 2.0))

    d_a = dmat(lgf_a, lgb_a)
    d_b = dmat(lgf_b, lgb_b)
    rowp = lax.broadcasted_iota(jnp.int32, (c, dk2), 0).astype(F32)
    qd = jnp.concatenate([jnp.exp((rowp + 1.0) * lgf_lane), jnp.exp((c - rowp) * lgb_lane)], axis=1)
    kd = jnp.concatenate([jnp.exp((c - 1.0 - rowp) * lgf_lane), jnp.exp(rowp * lgb_lane)], axis=1)
    rowk = lax.broadcasted_iota(jnp.int32, (dk2, 2 * RET_DV), 0)
    cd_f = jnp.exp(c * jnp.where(rowk < RET_DK, lgf_a[:, :1], lgf_b[:, :1]))
    cd_b = jnp.exp(c * jnp.where(rowk < RET_DK, lgb_a[:, :1], lgb_b[:, :1]))
    mask2 = jnp.concatenate([is_a, is_a], axis=1)

    def rows_of(n):
        if n < ncc:
            return (qc, kc, vc, gc, rc), slice(n * c, (n + 1) * c)
        return (ql, kl, vl, gl, rl), slice((n - ncc) * c, (n - ncc + 1) * c)

    def increment(n):
        (_, k_ref, v_ref, _, _), sl = rows_of(n)
        k = k_ref[0, sl, :]
        kk = jnp.concatenate([k, k], axis=1).astype(F32) * kd
        return _dot_tn(kk.astype(BF16), v_ref[0, sl, :])

    sf = jnp.zeros((dk2, 2 * RET_DV), F32)
    inc = increment(0)
    for n in range(nt):
        nxt = increment(n + 1) if n + 1 < nt else None
        st_s[n, 0:dk2, :] = sf
        st_s[n, dk2:2 * dk2, :] = inc[dk2:2 * dk2]
        sf = cd_f * sf + inc[0:dk2]
        inc = nxt

    heads = ((True, d_a, 0), (False, d_b, RET_DV))

    def scores(n):
        (q_ref, k_ref, _, _, _), sl = rows_of(n)
        q, k = q_ref[0, sl, :], k_ref[0, sl, :]
        q2 = (jnp.concatenate([q, q], axis=1).astype(F32) * qd).astype(BF16)
        res = []
        for keep_a, dm, _ in heads:
            m1 = is_a if keep_a else jnp.logical_not(is_a)
            m2 = mask2 if keep_a else jnp.logical_not(mask2)
            att = _dot_nt(jnp.where(m1, q, jnp.zeros_like(q)), k) * dm
            res.append((att.astype(BF16), jnp.where(m2, q2, jnp.zeros_like(q2))))
        return res

    def output(n, sc, sb):
        (_, _, v_ref, g_ref, out_ref), sl = rows_of(n)
        v, g = v_ref[0, sl, :], g_ref[0, sl, :]
        s_n = jnp.concatenate([st_s[n, 0:dk2, :].astype(BF16), sb.astype(BF16)], axis=0)
        for (_, _, cs), (att, q2m) in zip(heads, sc):
            o = _dot(att, v[:, cs:cs + RET_DV]) + _dot(q2m, s_n[:, cs:cs + RET_DV])
            o = _rms(o) * g[:, cs:cs + RET_DV].astype(F32)
            out_ref[0, sl, cs:cs + RET_DV] = o.astype(BF16)

    order = list(range(ncc - 1, -1, -1)) + list(range(nt - 1, ncc - 1, -1))
    sb = jnp.zeros((dk2, 2 * RET_DV), F32)
    sc = scores(order[0])
    for i, n in enumerate(order):
        nxt = scores(order[i + 1]) if i + 1 < nt else None
        output(n, sc, sb)
        sb = cd_b * sb + st_s[n, dk2:2 * dk2, :]
        sc = nxt


def _retention(lg_rows, q_l, k_l, v_l, g_l, q_c, k_c, v_c, g_c):
    b, n, _ = q_l.shape
    nc = q_c.shape[1]
    ncl, ncc = n // RET_CHUNK, nc // RET_CHUNK
    pairs = RET_HEADS // 2

    def spec(rows, width):
        return pl.BlockSpec((1, rows, width), lambda i, p: (i, 0, p))

    return pl.pallas_call(
        functools.partial(_ret_kernel, ncl=ncl, ncc=ncc),
        out_shape=[jax.ShapeDtypeStruct((b, n, RET_WIDTH), BF16),
                   jax.ShapeDtypeStruct((b, nc, RET_WIDTH), BF16)],
        grid=(b, pairs),
        in_specs=[pl.BlockSpec((2 * RET_HEADS, 2 * RET_DK), lambda i, p: (0, 0)),
                  spec(n, 2 * RET_DK), spec(n, 2 * RET_DK), spec(n, 2 * RET_DV), spec(n, 2 * RET_DV),
                  spec(nc, 2 * RET_DK), spec(nc, 2 * RET_DK), spec(nc, 2 * RET_DV), spec(nc, 2 * RET_DV)],
        out_specs=[spec(n, 2 * RET_DV), spec(nc, 2 * RET_DV)],
        scratch_shapes=[pltpu.VMEM((ncl + ncc, 4 * RET_DK, 2 * RET_DV), F32)],
        compiler_params=_cp("parallel", "parallel"),
        name="retention",
    )(lg_rows, q_l, k_l, v_l, g_l, q_c, k_c, v_c, g_c)


def _dot_hi(a, b, contract=(1, 0)):
    dims = (((contract[0],), (contract[1],)), ((), ()))
    return lax.dot_general(a, b, dims, preferred_element_type=F32, precision=lax.Precision.HIGHEST)


def _dot_sel(a, b, contract, data):
    dims = (((contract[0],), (contract[1],)), ((), ()))
    x = (a, b)[data]
    hi = x.astype(BF16)
    lo = (x - hi.astype(F32)).astype(BF16)
    dd = lambda piece: lax.dot_general(*((piece, b) if data == 0 else (a, piece)), dims, preferred_element_type=F32)
    return dd(hi) + dd(lo)


def _s5_prep_kernel(ar_row, ai_row, ldt, btr, bti, ctr, cti, rm, rwin, rwout, abig):
    t, g, p, kp = S5_CHUNK, S5_GROUP, S5_STATE, S5_POW
    row = S5_ROW
    i0 = lambda shape: lax.broadcasted_iota(jnp.int32, shape, 0)
    i1 = lambda shape: lax.broadcasted_iota(jnp.int32, shape, 1)
    f32 = lambda m: jnp.where(m, 1.0, 0.0).astype(BF16)
    s_of_r = i0((row, kp)) // g
    k_of_l = i1((row, kp))
    sel_rows = (f32(k_of_l == t - 1 - s_of_r), f32(k_of_l == s_of_r))
    t_of_c = i1((kp, row)) // g
    k_of_s = i0((kp, row))
    sel_out = (f32(k_of_s == t_of_c + 1), f32(k_of_s == t - t_of_c))
    sel_lag = (f32(k_of_s == t_of_c), f32(k_of_s == t - 1 - t_of_c))
    tile_l = f32(i1((g, row)) % g == i0((g, row)))
    tile_r = f32(i0((row, g)) % g == i1((row, g)))
    lane = i1((g, row))
    k_col = i0((kp, 1)).astype(F32)
    k_row = i1((1, kp)).astype(F32)
    first = i0((8, 1)) == 0

    def outer(a, k):
        a8 = jnp.where(first, jnp.broadcast_to(a, (8, a.shape[1])), 0.0)
        return _dot_hi(a8, jnp.broadcast_to(k, (8, k.shape[1])), (0, 0))

    rwin[...] = jnp.zeros(rwin.shape, BF16)
    rwout[...] = jnp.zeros(rwout.shape, BF16)
    for q in range(S5_GPB):
        pair, qq = divmod(q, 2)
        lags = []
        for d in range(2):
            dt = jnp.exp(ldt[d, q])
            are_r, aim_r = ar_row[d, q], ai_row[d, q]
            mag = jnp.exp(are_r * dt)
            ang = aim_r * dt
            nr, ni = mag * jnp.cos(ang) - 1.0, mag * jnp.sin(ang)
            den = jnp.square(are_r) + jnp.square(aim_r)
            fr = (nr * are_r + ni * aim_r) / den
            fi = (ni * are_r - nr * aim_r) / den
            pm = jnp.exp(k_col * (are_r * dt))
            pa = k_col * ang
            pk_re, pk_im = pm * jnp.cos(pa), pm * jnp.sin(pa)
            pmt = jnp.exp(outer(are_r * dt, k_row))
            pat = outer(ang, k_row)
            pt_re, pt_im = pmt * jnp.cos(pat), pmt * jnp.sin(pat)
            bt_re, bt_im = _dot_sel(tile_r, btr[d, q], (1, 1), 1), _dot_sel(tile_r, bti[d, q], (1, 1), 1)
            bb_re = fr * bt_re - fi * bt_im
            bb_im = fr * bt_im + fi * bt_re
            pr_re, pr_im = _dot_sel(sel_rows[d], pk_re, (1, 0), 1), _dot_sel(sel_rows[d], pk_im, (1, 0), 1)
            w_re = pr_re * bb_re - pr_im * bb_im
            w_im = pr_re * bb_im + pr_im * bb_re
            for part, w in ((d, w_re), (2 + d, w_im)):
                c0 = part * LANE + qq * p
                rwin[0, q * row:(q + 1) * row, c0:c0 + p] = w.astype(BF16)
            ct_re, ct_im = _dot_sel(ctr[d, q], tile_l, (0, 0), 0), _dot_sel(cti[d, q], tile_l, (0, 0), 0)

            def c_pow(sel):
                pc_re, pc_im = _dot_sel(pt_re, sel, (1, 0), 0), _dot_sel(pt_im, sel, (1, 0), 0)
                return ct_re * pc_re - ct_im * pc_im, ct_re * pc_im + ct_im * pc_re

            o_re, o_im = c_pow(sel_out[d])
            for part, o in ((d, o_re), (2 + d, -o_im)):
                r0 = pair * S5_PAIR + part * LANE + qq * p
                rwout[0, r0:r0 + p, qq * row:(qq + 1) * row] = o.astype(BF16)
            r0 = d * S5_GPB * p + q * p
            l_re, l_im = c_pow(sel_lag[d])
            lags.append(_dot_hi(bb_re[0:g], l_re) - _dot_hi(bb_im[0:g], l_im))
            abig[0, 0:1, r0:r0 + p] = pk_re[t:t + 1, :]
            abig[0, 1:2, r0:r0 + p] = pk_im[t:t + 1, :]
        for s in range(t):
            fwd = jnp.where(lane >= g * s, pltpu.roll(lags[0], g * s, 1), 0.0)
            bwd = jnp.where(lane < g * (s + 1), pltpu.roll(lags[1], (row - g * (t - 1 - s)) % row, 1), 0.0)
            rm[0, q * row + s * g:q * row + (s + 1) * g, :] = (fwd + bwd).astype(BF16)


def _s5_prep(a_re, a_im, log_dt, b_re, b_im, c_re, c_im):
    gg, p, g = S5_GROUPS, S5_STATE, S5_GROUP
    f = lambda x: x.astype(F32)
    args = (f(a_re).reshape(2, gg, 1, p), f(a_im).reshape(2, gg, 1, p),
            f(log_dt).reshape(2, gg, 1, 1),
            f(b_re), f(b_im), f(c_re), f(c_im))
    spec = lambda r, c: pl.BlockSpec((2, S5_GPB, r, c), lambda j: (0, j, 0, 0))
    out = lambda r, c: pl.BlockSpec((1, r, c), lambda j: (j, 0, 0))
    return pl.pallas_call(
        _s5_prep_kernel,
        out_shape=[jax.ShapeDtypeStruct((S5_LB, S5_BIG, S5_ROW), BF16),
                   jax.ShapeDtypeStruct((S5_LB, S5_BIG, S5_PAIR), BF16),
                   jax.ShapeDtypeStruct((S5_LB, S5_BIG, S5_PAIR), BF16),
                   jax.ShapeDtypeStruct((S5_LB, 2, S5_HALF), F32)],
        grid=(S5_LB,),
        in_specs=[spec(1, p), spec(1, p), spec(1, 1),
                  spec(p, g), spec(p, g), spec(g, p), spec(g, p)],
        out_specs=[out(S5_BIG, S5_ROW), out(S5_BIG, S5_PAIR), out(S5_BIG, S5_PAIR), out(2, S5_HALF)],
        compiler_params=_cp("parallel"),
        name="s5_prep",
    )(*args)


def _s5_block_transpose(v):
    n = len(v)
    blk = lax.broadcasted_iota(jnp.int32, (1, LANE), 1) // S5_GROUP

    def spread(src, e):
        w = 1
        while w < n:
            bit = (blk // w) % 2 == 1
            src = [jnp.where(bit, src[(m + e * w) % n], src[m]) for m in range(n)]
            w *= 2
        return src

    diag = spread(v, 1)
    rolled = [diag[0]] + [pltpu.roll(diag[(-k) % n], LANE - k * S5_GROUP, 1) for k in range(1, n)]
    return spread(rolled, -1)


def _s5_kernel(ul, uc, rm_ref, rwin_ref, rwout_ref, a_ref, yl, yc, x_s, *, ncl, ncc):
    hw = S5_HALF
    gpb, half = S5_GPB, S5_CHUNK // 2
    cols = lambda x, i: x[:, i * LANE:(i + 1) * LANE]

    nb = ul.shape[0]
    nt = ncc + ncl
    pairs = gpb // 2
    slab = lambda k, r: k * pairs + r
    of_row = lambda i: pl.ds(i, nt, stride=nb)

    zg = []
    for i in range(nb):
        z = jnp.concatenate([uc[i, 0], ul[i, 0]], axis=0)
        zt = [_s5_block_transpose([cols(z, half * h + a) for a in range(half)]) for h in range(2)]
        zg.append([jnp.concatenate([zt[0][q], zt[1][q]], axis=1) for q in range(gpb)])
        for r in range(pairs):
            xw = _dot(jnp.concatenate([zg[i][2 * r], zg[i][2 * r + 1]], axis=1),
                      rwin_ref[0, r * S5_PAIR:(r + 1) * S5_PAIR, :])
            for k in range(4):
                x_s[slab(k, r), of_row(i), :] = cols(xw, k)
    a_re = a_ref[0, 0:1, :]
    a_im = a_ref[0, 1:2, :]

    def segment(base, n, carry):
        s_re, s_im = carry
        for i in range(n):
            rf = slice((base + i) * nb, (base + i + 1) * nb)
            rb = slice((base + n - 1 - i) * nb, (base + n - i) * nb)
            rows_of = (rf,) * pairs + (rb,) * pairs
            x_re = jnp.concatenate([x_s[c, rows_of[c], :] for c in range(2 * pairs)], axis=-1)
            x_im = jnp.concatenate([x_s[2 * pairs + c, rows_of[c], :] for c in range(2 * pairs)], axis=-1)
            for c in range(2 * pairs):
                x_s[c, rows_of[c], :] = cols(s_re, c)
                x_s[2 * pairs + c, rows_of[c], :] = cols(s_im, c)
            s_re, s_im = a_re * s_re - a_im * s_im + x_re, a_re * s_im + a_im * s_re + x_im
        return s_re, s_im

    zero = jnp.zeros((nb, hw), F32)
    carry = segment(0, ncc, (zero, zero))
    segment(ncc, ncl, carry)

    for i in range(nb):
        yt = [[None] * gpb, [None] * gpb]
        for r in range(pairs):
            xs = jnp.concatenate([x_s[slab(k, r), of_row(i), :] for k in range(4)], axis=1)
            yp = _dot(xs.astype(BF16), rwout_ref[0, r * S5_PAIR:(r + 1) * S5_PAIR, :])
            for qq in range(2):
                q = 2 * r + qq
                yq = (yp[:, qq * S5_ROW:(qq + 1) * S5_ROW]
                      + _dot(zg[i][q], rm_ref[0, q * S5_ROW:(q + 1) * S5_ROW, :]))
                yt[0][q], yt[1][q] = cols(yq, 0), cols(yq, 1)
        for h in range(2):
            for a, y in enumerate(_s5_block_transpose(yt[h])):
                c = slice((half * h + a) * LANE, (half * h + a + 1) * LANE)
                yc[i, 0, :, c] = y[0:ncc]
                yl[i, 0, :, c] = y[ncc:ncc + ncl]


def _s5(prep, zl, zc):
    rm, rwin, rwout, a_big = prep
    b, _, ncl, _ = zl.shape
    ncc = zc.shape[2]
    nb = S5_BATCH_ROWS if b % S5_BATCH_ROWS == 0 else 1
    rows = lambda r: pl.BlockSpec((nb, 1, r, S5_BIG), lambda j, i: (i, j, 0, 0))
    wspec = lambda r, c: pl.BlockSpec((1, r, c), lambda j, i: (j, 0, 0))
    return pl.pallas_call(
        functools.partial(_s5_kernel, ncl=ncl, ncc=ncc),
        out_shape=[jax.ShapeDtypeStruct(zl.shape, F32), jax.ShapeDtypeStruct(zc.shape, F32)],
        grid=(S5_LB, b // nb),
        in_specs=[rows(ncl), rows(ncc), wspec(S5_BIG, S5_ROW), wspec(S5_BIG, S5_PAIR),
                  wspec(S5_BIG, S5_PAIR), wspec(2, S5_HALF)],
        out_specs=[rows(ncl), rows(ncc)],
        scratch_shapes=[pltpu.VMEM((2 * S5_HALF // LANE, (ncc + ncl) * nb, LANE), F32)],
        compiler_params=_cp("parallel", "parallel"),
        name="s5",
    )(zl, zc, rm, rwin, rwout, a_big)


def _mix_mlp_kernel(*refs, mod_row, s5_merge, final_norm, fb):
    if s5_merge:
        (h_ref, r_ref, y5_ref, u_ref, ds_ref, wg_ref, bg_ref, wo_ref, mod_ref, nm_ref, w1_ref, w2_ref,
         *rest) = refs
    else:
        h_ref, r_ref, wo_ref, mod_ref, nm_ref, w1_ref, w2_ref, *rest = refs
    if final_norm:
        nf_ref, o_ref, *scratch = rest
    else:
        o_ref, *scratch = rest
    row = pl.program_id(0) if mod_row is None else mod_row
    if s5_merge:
        (y_s,) = scratch
        nch = y_s.shape[1] // S5_CHUNK
        for lb in range(S5_LB):
            for s in range(S5_CHUNK):
                y_s[lb, pl.ds(s, nch, stride=S5_CHUNK), :] = y5_ref[0, lb, :, s * LANE:(s + 1) * LANE]
        y5 = jnp.concatenate([y_s[lb] for lb in range(S5_LB)], axis=-1)
        y = jax.nn.gelu(y5 + ds_ref[...] * u_ref[0])
        y = y * jax.nn.sigmoid(_dot(y.astype(BF16), wg_ref[...]) + bg_ref[...])
        mix = _dot(r_ref[0], wo_ref[0:RET_WIDTH, :]) + _dot(y.astype(BF16), wo_ref[RET_WIDTH:D_MODEL, :])
    else:
        mix = _dot(r_ref[0], wo_ref[...])
    h1 = h_ref[0] + _mod_chunk(mod_ref, row, 2) * mix
    xn = _rms(h1) * nm_ref[...]
    xm = (xn * (1.0 + _mod_chunk(mod_ref, row, 4)) + _mod_chunk(mod_ref, row, 3)).astype(BF16)
    acc = None
    for j in range(D_FF // fb):
        a = jnp.square(jnp.maximum(_dot(xm, w1_ref[:, j * fb:(j + 1) * fb].astype(BF16)), 0.0)).astype(BF16)
        part = _dot(a, w2_ref[j * fb:(j + 1) * fb, :].astype(BF16))
        acc = part if acc is None else acc + part
    h2 = h1 + _mod_chunk(mod_ref, row, 5) * acc
    if final_norm:
        h2 = _rms(h2) * nf_ref[...]
    o_ref[0] = h2


def _mix_mlp(h, r, s5y, u, s5p, wo, mod, layer, nm, w1, w2, nf, mod_row, tm, fb, name):
    b, n, _ = h.shape
    s5_merge = s5y is not None
    final_norm = nf is not None
    one = pl.Buffered(1)
    row_spec = lambda width: pl.BlockSpec((1, tm, width), lambda i, t: (i, t, 0))
    const = lambda shape: pl.BlockSpec(shape, lambda i, t: (0,) * len(shape), pipeline_mode=one)
    in_specs = [row_spec(D_MODEL), row_spec(r.shape[-1])]
    args = [h, r]
    if s5_merge:
        d_skip, w_glu, b_glu = s5p
        z_spec = pl.BlockSpec((1, S5_LB, tm // S5_CHUNK, S5_BIG), lambda i, t: (i, 0, t, 0))
        in_specs += [z_spec, row_spec(S5_WIDTH), const((1, S5_WIDTH)),
                     const((S5_WIDTH, S5_WIDTH)), const((1, S5_WIDTH))]
        args += [s5y, u, d_skip, w_glu, b_glu]
    in_specs += [const((D_MODEL, D_MODEL)),
                 pl.BlockSpec((1, MOD_ROWS, N_MOD * D_MODEL), lambda i, t: (layer, 0, 0), pipeline_mode=one),
                 const((1, D_MODEL))]
    if w1.ndim == 3:
        in_specs += [pl.BlockSpec((None, D_MODEL, D_FF), lambda i, t: (layer, 0, 0), pipeline_mode=one),
                     pl.BlockSpec((None, D_FF, D_MODEL), lambda i, t: (layer, 0, 0), pipeline_mode=one)]
    else:
        in_specs += [const((D_MODEL, D_FF)), const((D_FF, D_MODEL))]
    args += [wo, mod, nm, w1, w2]
    if final_norm:
        in_specs.append(const((1, D_MODEL)))
        args.append(nf)
    return pl.pallas_call(
        functools.partial(_mix_mlp_kernel, mod_row=mod_row, s5_merge=s5_merge, final_norm=final_norm, fb=fb),
        out_shape=jax.ShapeDtypeStruct((b, n, D_MODEL), F32),
        grid=(b, n // tm),
        in_specs=in_specs,
        out_specs=row_spec(D_MODEL),
        scratch_shapes=[pltpu.VMEM((S5_LB, tm, LANE), F32)] if s5_merge else [],
        compiler_params=_cp("parallel", "parallel"),
        name=name,
    )(*args)


def _hgrn_lower_bounds(lbl_ref, layer):
    out = []
    for d in range(2):
        z = [lbl_ref[d, k:k + 1, :] for k in range(DEPTH)]
        zmax = functools.reduce(jnp.maximum, z)
        e = [jnp.exp(v - zmax) for v in z]
        tot = functools.reduce(lambda a, b_: a + b_, e)
        lb = jnp.zeros_like(tot)
        for k in range(1, layer + 1):
            lb = lb + e[k] / tot
        out.append(lb)
    return out


def _inproj1_kernel(h_ref, mod_ref, ng_ref, lbl_ref, w_ref, *out_refs, mod_row, layer, latent):
    row = pl.program_id(0) if mod_row is None else mod_row
    xn = _rms(h_ref[0]) * ng_ref[...]
    xm = (xn * (1.0 + _mod_chunk(mod_ref, row, 1)) + _mod_chunk(mod_ref, row, 0)).astype(BF16)
    lbs = _hgrn_lower_bounds(lbl_ref, layer)
    names = ((["q"] if latent else []) + ["lf0h", "lf0l", "kk0", "lf1h", "lf1l", "kk1", "i"]
             + (["sg"] if latent else []))
    out = dict(zip(names, out_refs))
    half = h_ref.shape[1] // 2

    def finish(kind, y, rows):
        if kind in ("lf0", "lf1"):
            d = int(kind[2])
            t = (1.0 - lbs[d]) * _sigmoid(y)
            lf = jnp.log(lbs[d] + t)
            hi = lf.astype(BF16)
            out[kind + "h"][0, rows, :] = hi
            out[kind + "l"][0, rows, :] = (lf - hi.astype(F32)).astype(BF16)
            out["kk%d" % d][0, rows, :] = ((1.0 - lbs[d]) - t).astype(BF16)
        elif kind == "sg":
            out[kind][0, rows, :] = (y * _sigmoid(y)).astype(BF16)
        else:
            out[kind][0, rows, :] = y.astype(BF16)

    cols = [("lf0", 1), ("lf1", 2)] + ([("sg", 4), ("q", 0)] if latent else []) + [("i", 3)]
    pending = None
    for kind, k in cols:
        wk = w_ref[:, k * D_MODEL:(k + 1) * D_MODEL].astype(BF16)
        for rows in (slice(0, half), slice(half, 2 * half)):
            y = _dot(xm[rows], wk)
            if pending is not None:
                finish(*pending)
            pending = (kind, y, rows)
    finish(*pending)


def _inproj1(h, mod, layer, ng, lb_logits, w, latent, mod_row, tm, name):
    b, n, _ = h.shape
    one = pl.Buffered(1)
    row_spec = pl.BlockSpec((1, tm, D_MODEL), lambda i, t: (i, t, 0))
    n_out = 9 if latent else 7
    return pl.pallas_call(
        functools.partial(_inproj1_kernel, mod_row=mod_row, layer=layer, latent=latent),
        out_shape=[jax.ShapeDtypeStruct((b, n, D_MODEL), BF16)] * n_out,
        grid=(b, n // tm),
        in_specs=[row_spec,
                  pl.BlockSpec((1, MOD_ROWS, N_MOD * D_MODEL), lambda i, t: (layer, 0, 0), pipeline_mode=one),
                  pl.BlockSpec((1, D_MODEL), lambda i, t: (0, 0), pipeline_mode=one),
                  pl.BlockSpec(lb_logits.shape, lambda i, t: (0, 0, 0), pipeline_mode=one),
                  pl.BlockSpec(w.shape, lambda i, t: (0, 0), pipeline_mode=one)],
        out_specs=[row_spec] * n_out,
        compiler_params=_cp("parallel", "parallel"),
        name=name,
    )(h, mod, ng, lb_logits, w)


def _hgrn_kernel(ng_ref, ql, fhl, fll, kfl, bhl, bll, kbl, il, sgl, fhc, flc, kfc, bhc, blc, kbc, ic, o_ref,
                 qin_s, att_s, kv_s, et_s, kvc_s, etc_s, cum_s, ko_s, qt_s, kt_s, *, nbl, nbc):
    cb = HG_BLOCK
    mid = cb // 2
    gb = HG_GROUP
    gr = gb * cb
    dk = HG_DK

    ri = lax.broadcasted_iota(jnp.int32, (gr, gr), 0)
    ci = lax.broadcasted_iota(jnp.int32, (gr, gr), 1)
    same = (ri // cb) == (ci // cb)
    rb = lax.broadcasted_iota(jnp.int32, (cb, cb), 0)
    cbi = lax.broadcasted_iota(jnp.int32, (cb, cb), 1)
    tri_l = jnp.where(same & (ri >= ci), 1.0, 0.0).astype(BF16)
    tri_u = jnp.where(same & (ri <= ci), 1.0, 0.0).astype(BF16)
    dirs = ((0, rb >= cbi, mid - 1, cb - 1), (1, rb <= cbi, mid, 0))

    def cumsums(fh, fl, bh, bl, slot):
        pre = _dot(tri_l, jnp.concatenate([fh, fl], axis=-1))
        suf = _dot(tri_u, jnp.concatenate([bh, bl], axis=-1))
        cum_s[slot, 0] = pre[:, :dk] + pre[:, dk:]
        cum_s[slot, 1] = suf[:, :dk] + suf[:, dk:]

    def operands(slot, kks, q, n0, et_ref, which=(0, 1)):
        for d, keep, ref_row, tot_row in [dirs[i] for i in which]:
            cum = cum_s[slot, d].reshape(gb, cb, dk)
            kk = kks[d].astype(F32).reshape(gb, cb, dk)
            ref = cum[:, ref_row:ref_row + 1, :]
            tot = cum[:, tot_row:tot_row + 1, :]
            e = cum - ref
            kt = kk * jnp.exp(-e)
            ko_s[slot, d] = (kt * jnp.exp(tot - ref)).astype(BF16).reshape(gr, dk)
            e_tot = jnp.exp(tot)
            for j in range(gb):
                et_ref[n0 + j, :, d * dk:(d + 1) * dk] = e_tot[j]
            if q is not None:
                qt = q.astype(F32).reshape(gb, cb, dk) * jnp.exp(e)
                qin_s[n0 * cb:n0 * cb + gr, d * dk:(d + 1) * dk] = (
                    (qt * jnp.exp(ref)).astype(BF16).reshape(gr, dk))
                qt_s[slot, d] = qt.astype(BF16).reshape(gr, dk)
                kt_s[slot, d] = kt.astype(BF16).reshape(gr, dk)

    def matmuls(slot, v, with_q, n0, kv_ref, blocks=tuple(range(HG_GROUP))):
        for d in range(2):
            for j in blocks:
                rows = slice(j * cb, (j + 1) * cb)
                kv_ref[n0 + j, :, d * dk:(d + 1) * dk] = _dot_tn(v[rows], ko_s[slot, d, rows, :])
        if with_q:
            for j in blocks:
                rows = slice(j * cb, (j + 1) * cb)
                att = (jnp.where(dirs[0][1], _dot_nt(qt_s[slot, 0, rows, :], kt_s[slot, 0, rows, :]), 0.0)
                       + jnp.where(dirs[1][1], _dot_nt(qt_s[slot, 1, rows, :], kt_s[slot, 1, rows, :]), 0.0))
                att_s[n0 + j] = att.astype(BF16)

    for g in range(nbc // gb):
        sl = slice(g * gr, (g + 1) * gr)
        cumsums(fhc[0, sl, :], flc[0, sl, :], bhc[0, sl, :], blc[0, sl, :], 0)
        operands(0, (kfc[0, sl, :], kbc[0, sl, :]), None, g * gb, etc_s)
        matmuls(0, ic[0, sl, :], False, g * gb, kvc_s)

    ngl = nbl // gb
    rows_of = lambda g: slice(g * gr, (g + 1) * gr)

    def lat_cumsums(g, slot):
        r = rows_of(g)
        cumsums(fhl[0, r, :], fll[0, r, :], bhl[0, r, :], bll[0, r, :], slot)

    def lat_operands(g, slot, which=(0, 1)):
        operands(slot, (kfl[0, rows_of(g), :], kbl[0, rows_of(g), :]), ql[0, rows_of(g), :], g * gb, et_s, which)

    def lat_matmuls(g, slot, blocks=tuple(range(HG_GROUP))):
        matmuls(slot, il[0, rows_of(g), :], True, g * gb, kv_s, blocks)

    fwd, bwd = slice(0, dk), slice(dk, 2 * dk)
    st_f = jnp.zeros((dk, dk), F32)
    st_b = jnp.zeros((dk, dk), F32)
    for n in range(nbc):
        st_f = st_f * etc_s[n, :, fwd] + kvc_s[n, :, fwd]
    for n in reversed(range(nbc)):
        st_b = st_b * etc_s[n, :, bwd] + kvc_s[n, :, bwd]

    def fwd_steps(st, blocks):
        for n in blocks:
            inc = kv_s[n, :, fwd]
            kv_s[n, :, fwd] = st
            st = st * et_s[n, :, fwd] + inc
        return st

    quarter = [tuple(range(i * gb // 4, (i + 1) * gb // 4)) for i in range(4)]
    lat_cumsums(0, 0)
    lat_operands(0, 0)
    lat_cumsums(1, 1)
    for a in range(ngl):
        sa, sb = a % 2, 1 - a % 2
        prev = [(a - 1) * gb + j for j in range(gb)] if a > 0 else []
        lat_matmuls(a, sa, quarter[0])
        if a + 1 < ngl:
            lat_operands(a + 1, sb, (0,))
        lat_matmuls(a, sa, quarter[1])
        st_f = fwd_steps(st_f, prev[:gb // 2])
        if a + 2 < ngl:
            lat_cumsums(a + 2, sa)
        lat_matmuls(a, sa, quarter[2])
        if a + 1 < ngl:
            lat_operands(a + 1, sb, (1,))
        lat_matmuls(a, sa, quarter[3])
        st_f = fwd_steps(st_f, prev[gb // 2:])
    fwd_steps(st_f, [(ngl - 1) * gb + j for j in range(gb)])

    for n in reversed(range(nbl)):
        sl = slice(n * cb, (n + 1) * cb)
        kv = jnp.concatenate([kv_s[n, :, fwd].astype(BF16), st_b.astype(BF16)], axis=1)
        st_b = st_b * et_s[n, :, bwd] + kv_s[n, :, bwd]
        o = _dot(att_s[n], il[0, sl, :]) + _dot_nt(qin_s[sl, :], kv)
        o = _rms(o) * ng_ref[...] * sgl[0, sl, :].astype(F32)
        o_ref[0, sl, :] = o.astype(BF16)


def _hgrn(norm_g, q_l, fh_l, fl_l, kf_l, bh_l, bl_l, kb_l, i_l, sg_l, fh_c, fl_c, kf_c, bh_c, bl_c, kb_c, i_c):
    b, n, _ = q_l.shape
    nc = i_c.shape[1]
    nbl, nbc = n // HG_BLOCK, nc // HG_BLOCK
    assert nbl % HG_GROUP == 0 and nbc % HG_GROUP == 0 and nbl >= 2 * HG_GROUP and HG_GROUP % 4 == 0
    spec = lambda rows: pl.BlockSpec((1, rows, HG_DK), lambda i, h: (i, 0, h))
    slot = lambda dt: pltpu.VMEM((2, 2, HG_GROUP * HG_BLOCK, HG_DK), dt)
    return pl.pallas_call(
        functools.partial(_hgrn_kernel, nbl=nbl, nbc=nbc),
        out_shape=jax.ShapeDtypeStruct((b, n, D_MODEL), BF16),
        grid=(b, HG_HEADS),
        in_specs=[pl.BlockSpec((1, HG_DK), lambda i, h: (0, 0))] + [spec(n)] * 9 + [spec(nc)] * 7,
        out_specs=spec(n),
        scratch_shapes=[pltpu.VMEM((n, 2 * HG_DK), BF16),
                        pltpu.VMEM((nbl, HG_BLOCK, HG_BLOCK), BF16),
                        pltpu.VMEM((nbl, HG_DK, 2 * HG_DK), F32),
                        pltpu.VMEM((nbl, 1, 2 * HG_DK), F32),
                        pltpu.VMEM((nbc, HG_DK, 2 * HG_DK), F32),
                        pltpu.VMEM((nbc, 1, 2 * HG_DK), F32),
                        slot(F32), slot(BF16), slot(BF16), slot(BF16)],
        compiler_params=_cp("parallel", "parallel"),
        name="hgrn2",
    )(norm_g, q_l, fh_l, fl_l, kf_l, bh_l, bl_l, kb_l, i_l, sg_l, fh_c, fl_c, kf_c, bh_c, bl_c, kb_c, i_c)


def _rope_tables(n_tok):
    n_rows = n_tok // GRID_W
    n_freq = RET_DK // 4
    lane = jnp.arange(LANE, dtype=jnp.int32)[None, :]
    j = lane % (2 * n_freq)
    inv = ROPE_BASE ** (-(j % n_freq).astype(F32) / n_freq)
    sign = jnp.where(lane % RET_DK < RET_DK // 2, -1.0, 1.0)
    pos = jnp.arange(max(n_rows, GRID_W), dtype=jnp.int32).astype(F32)[:, None]
    ang = pos * inv
    pick = lambda t: jnp.where(j < n_freq, t[:n_rows, None, :], t[None, :GRID_W, :]).reshape(n_tok, LANE)
    return pick(jnp.cos(ang)), pick(jnp.sin(ang) * sign)


def kernel(x, c, ctx, c_ctx, w_mod, b_mod, norm_mix, norm_mlp, w_mlp_in, w_mlp_out, ab_w_in, ab_w_out, ret_logit, s5_a_re, s5_a_im, s5_log_dt, s5_b_re, s5_b_im, s5_c_re, s5_c_im, s5_d, s5_w_glu, s5_b_glu, hg_w_in, hg_w_out, hg_lb_logits, hg_norm, norm_final):
    b, n, d = x.shape
    nc = ctx.shape[1]
    assert d == D_MODEL and b + 1 <= MOD_ROWS and w_mod.shape[0] == DEPTH == 2
    assert n % 512 == 0 and nc % 256 == 0 and n % GRID_W == 0
    ctx_row = b
    tm_l, tm_c = 512, 256
    tm_in0 = 1024 if n % 1024 == 0 else tm_l

    cc = jnp.zeros((MOD_ROWS, d), F32).at[:b].set(c).at[b].set(c_ctx)
    mod = _adaln(cc, w_mod, b_mod)

    row2 = lambda a: a.reshape(1, -1)
    w_in0 = ab_w_in[0]
    cos, sin = _rope_tables(n)
    ng0 = row2(norm_mix[0])
    q_l, k_l, v_l, u_l, uz_l, g_l = _inproj0(x, mod, 0, ng0, w_in0, cos, sin, None, tm_in0)
    q_c, k_c, v_c, u_c, uz_c, g_c = _inproj0(ctx, mod, 0, ng0, w_in0, None, None, ctx_row, tm_c)

    log_gamma = jax.nn.log_sigmoid(ret_logit[0].astype(F32))
    lg_rows = jnp.broadcast_to(log_gamma.reshape(2 * RET_HEADS, 1), (2 * RET_HEADS, 2 * RET_DK))
    r_l, r_c = _retention(lg_rows, q_l, k_l, v_l, g_l, q_c, k_c, v_c, g_c)

    s5_ops = _s5_prep(s5_a_re[0], s5_a_im[0], s5_log_dt[0], s5_b_re[0], s5_b_im[0], s5_c_re[0], s5_c_im[0])
    y5_l, y5_c = _s5(s5_ops, uz_l, uz_c)

    s5p = (row2(s5_d[0]), s5_w_glu[0].astype(BF16), row2(s5_b_glu[0]))
    wo0 = ab_w_out[0].astype(BF16)
    w1_0, w2_0 = w_mlp_in, w_mlp_out
    nm0 = row2(norm_mlp[0])
    h_l = _mix_mlp(x, r_l, y5_l, u_l, s5p, wo0, mod, 0, nm0, w1_0, w2_0, None, None, tm_l, 1024, "mix_mlp0_lat")
    h_c = _mix_mlp(ctx, r_c, y5_c, u_c, s5p, wo0, mod, 0, nm0, w1_0, w2_0, None, ctx_row, tm_c, 1024, "mix_mlp0_ctx")

    w_in1 = hg_w_in[0]
    ng1 = row2(norm_mix[1])
    lat1 = _inproj1(h_l, mod, 1, ng1, hg_lb_logits, w_in1, True, None, tm_l, "inproj1_lat")
    ctx1 = _inproj1(h_c, mod, 1, ng1, hg_lb_logits, w_in1, False, ctx_row, tm_c, "inproj1_ctx")
    o1 = _hgrn(row2(hg_norm[0]), *lat1, *ctx1)
    return _mix_mlp(h_l, o1, None, None, None, hg_w_out[0].astype(BF16), mod, 1, row2(norm_mlp[1]),
                    w_mlp_in, w_mlp_out, row2(norm_final), None, tm_l, 1024,
                    "mix_mlp1_lat")
```
